```python
import jax, jax.numpy as jnp
from jax import lax
import numpy as np

D_MODEL = 1024
BATCH = 8
SEQ = 8192
DEPTH = 2

LRU_WIDTH = D_MODEL
LRU_BLOCKS = 8
LRU_BLOCK_W = LRU_WIDTH // LRU_BLOCKS
CONV_W = 4
LRU_C = 8.0
MLA_HEADS = 8
QK_NOPE = 128
QK_ROPE = 64
QK_HEAD = QK_NOPE + QK_ROPE
V_HEAD = D_MODEL // MLA_HEADS
Q_RANK = 256
KV_RANK = 128
ROPE_THETA = 10000.0
Q_BLOCK = 128
D_FF = -(-8 * D_MODEL // (3 * 256)) * 256
EPS = 1e-6
D_IN = LRU_WIDTH + Q_RANK + KV_RANK + QK_ROPE + 2 * D_MODEL
IN_SPLIT_POINTS = (LRU_WIDTH,
                   LRU_WIDTH + Q_RANK,
                   LRU_WIDTH + Q_RANK + KV_RANK,
                   LRU_WIDTH + Q_RANK + KV_RANK + QK_ROPE,
                   LRU_WIDTH + Q_RANK + KV_RANK + QK_ROPE + D_MODEL)

kernel_name = 'hybrid_rglru_mla_adaln_block'


def rms_norm(x, gain=None):
    xf = x.astype(jnp.float32)
    y = xf * lax.rsqrt(jnp.mean(xf * xf, axis=-1, keepdims=True) + EPS)
    if gain is not None:
        y = y * gain.astype(jnp.float32)
    return y.astype(x.dtype)


def rope_tables(positions):
    inv_freq = ROPE_THETA ** (-jnp.arange(0, QK_ROPE, 2, dtype=jnp.float32) / QK_ROPE)
    ang = positions.astype(jnp.float32)[..., None] * inv_freq
    return jnp.cos(ang), jnp.sin(ang)


def apply_rope(x, cos, sin):
    half = QK_ROPE // 2
    xf = x.astype(jnp.float32)
    x1, x2 = xf[..., :half], xf[..., half:]
    return jnp.concatenate([x1 * cos - x2 * sin, x2 * cos + x1 * sin], axis=-1).astype(x.dtype)


def causal_depthwise_conv(x, w, b):
    y = lax.conv_general_dilated(x, w[:, None, :].astype(x.dtype), window_strides=(1,),
                                 padding=((CONV_W - 1, 0),),
                                 dimension_numbers=('NWC', 'WIO', 'NWC'),
                                 feature_group_count=x.shape[-1])
    return y + b


def rg_lru(x, positions, w_a, b_a, w_x, b_x, a_param):
    B, S, _ = x.shape
    xf = x.astype(jnp.float32)
    xb = xf.reshape(B, S, LRU_BLOCKS, LRU_BLOCK_W)
    r = jax.nn.sigmoid(jnp.einsum('bsni,nij->bsnj', xb, w_a.astype(jnp.float32))
                       + b_a.astype(jnp.float32)).reshape(B, S, LRU_WIDTH)
    i = jax.nn.sigmoid(jnp.einsum('bsni,nij->bsnj', xb, w_x.astype(jnp.float32))
                       + b_x.astype(jnp.float32)).reshape(B, S, LRU_WIDTH)
    log_a = -LRU_C * r * jax.nn.softplus(-a_param.astype(jnp.float32))
    reset = (positions == 0)[..., None]
    a = jnp.where(reset, 0.0, jnp.exp(log_a))
    mult = jnp.where(reset, 1.0, jnp.sqrt(-jnp.expm1(2.0 * log_a)))
    b_in = xf * i * mult

    def combine(left, right):
        return (left[0] * right[0], right[0] * left[1] + right[1])

    _, h = lax.associative_scan(combine, (a, b_in), axis=1)
    return h.astype(x.dtype)


def causal_block_attention(q, k, v):
    B, S, H, Dk = q.shape
    nb = S // Q_BLOCK
    scale = QK_HEAD ** -0.5
    qb = q.reshape(B, nb, Q_BLOCK, H, Dk).transpose(1, 0, 2, 3, 4)
    k_idx = jnp.arange(S)

    def one_block(args):
        q_blk, blk = args
        s = jnp.einsum('bqhd,bkhd->bhqk', q_blk, k, preferred_element_type=jnp.float32) * scale
        q_idx = blk * Q_BLOCK + jnp.arange(Q_BLOCK)
        s = jnp.where(k_idx[None, :] <= q_idx[:, None], s, -jnp.inf)
        p = jax.nn.softmax(s, axis=-1)
        o = jnp.einsum('bhqk,bkhd->bqhd', p.astype(v.dtype), v, preferred_element_type=jnp.float32)
        return o.astype(v.dtype)

    o = lax.map(one_block, (qb, jnp.arange(nb)))
    return o.transpose(1, 0, 2, 3, 4).reshape(B, S, H, v.shape[-1])


def mla(q_down, kv_down, k_rope, cos, sin, q_norm_g, kv_norm_g, w_uq, w_ukv):
    B, S, _ = q_down.shape
    c_q = rms_norm(q_down, q_norm_g)
    q = jnp.einsum('bsr,rhd->bshd', c_q, w_uq)
    q = jnp.concatenate([q[..., :QK_NOPE],
                         apply_rope(q[..., QK_NOPE:], cos[:, :, None, :], sin[:, :, None, :])], axis=-1)
    c_kv = rms_norm(kv_down, kv_norm_g)
    kv = jnp.einsum('bsr,rhd->bshd', c_kv, w_ukv)
    k_pe = apply_rope(k_rope, cos, sin)
    k = jnp.concatenate([kv[..., :QK_NOPE],
                         jnp.broadcast_to(k_pe[:, :, None, :], (B, S, MLA_HEADS, QK_ROPE))], axis=-1)
    v = kv[..., QK_NOPE:]
    o = causal_block_attention(q, k, v)
    return o.reshape(B, S, MLA_HEADS * V_HEAD)


def hybrid_layer(x, c, positions, cos, sin, w_ada, b_ada, w_in, conv_w, conv_b,
                 lru_wa, lru_ba, lru_wx, lru_bx, lru_a_param, q_norm_g, kv_norm_g,
                 w_uq, w_ukv, w_out, w_ffn_in, w_ffn_out):
    mod = (c @ w_ada + b_ada)[:, None, :]
    sh_m, sc_m, g_m, sh_f, sc_f, g_f = jnp.split(mod, 6, axis=-1)
    h = rms_norm(x) * (1 + sc_m) + sh_m
    proj = h @ w_in
    x_lru, q_down, kv_down, k_rope, gate_a, gate_b = jnp.split(proj, IN_SPLIT_POINTS, axis=-1)
    y_a = rg_lru(causal_depthwise_conv(x_lru, conv_w, conv_b), positions,
                 lru_wa, lru_ba, lru_wx, lru_bx, lru_a_param)
    y_b = mla(q_down, kv_down, k_rope, cos, sin, q_norm_g, kv_norm_g, w_uq, w_ukv)
    y = jax.nn.sigmoid(gate_a) * y_a + jax.nn.sigmoid(gate_b) * y_b
    x = x + g_m * (y @ w_out)
    h = rms_norm(x) * (1 + sc_f) + sh_f
    gate, up = jnp.split(h @ w_ffn_in, 2, axis=-1)
    x = x + g_f * ((jax.nn.silu(gate) * up) @ w_ffn_out)
    return x


def _fwd_setup_inputs(seed: int = 0) -> dict:
    key = jax.random.key(seed)
    ks = jax.random.split(key, 24)
    f32 = jnp.float32
    x = jax.random.normal(ks[0], (BATCH, SEQ, D_MODEL), f32)
    c = jax.random.normal(ks[1], (BATCH, D_MODEL), f32)
    positions = jnp.broadcast_to(jnp.arange(SEQ, dtype=jnp.int32)[None, :], (BATCH, SEQ))
    w_ada = jax.random.normal(ks[2], (DEPTH, D_MODEL, 6 * D_MODEL), f32) * (0.3 * D_MODEL ** -0.5)
    b_ada = jax.random.normal(ks[3], (DEPTH, 6 * D_MODEL), f32) * 0.02
    w_in = jax.random.normal(ks[4], (DEPTH, D_MODEL, D_IN), f32) * D_MODEL ** -0.5
    conv_w = jax.random.normal(ks[5], (DEPTH, CONV_W, LRU_WIDTH), f32) * CONV_W ** -0.5
    conv_b = jax.random.normal(ks[6], (DEPTH, LRU_WIDTH), f32) * 0.01
    lru_wa = jax.random.normal(ks[7], (DEPTH, LRU_BLOCKS, LRU_BLOCK_W, LRU_BLOCK_W), f32) * LRU_BLOCK_W ** -0.5
    lru_ba = jax.random.normal(ks[8], (DEPTH, LRU_BLOCKS, LRU_BLOCK_W), f32) * 0.01
    lru_wx = jax.random.normal(ks[9], (DEPTH, LRU_BLOCKS, LRU_BLOCK_W, LRU_BLOCK_W), f32) * LRU_BLOCK_W ** -0.5
    lru_bx = jax.random.normal(ks[10], (DEPTH, LRU_BLOCKS, LRU_BLOCK_W), f32) * 0.01
    rad = jax.random.uniform(ks[11], (DEPTH, LRU_WIDTH), f32, minval=0.9, maxval=0.999)
    a0 = rad ** (1.0 / LRU_C)
    lru_a_param = jnp.log(a0) - jnp.log1p(-a0)
    q_norm_g = 1.0 + 0.01 * jax.random.normal(ks[12], (DEPTH, Q_RANK), f32)
    kv_norm_g = 1.0 + 0.01 * jax.random.normal(ks[13], (DEPTH, KV_RANK), f32)
    w_uq = jax.random.normal(ks[14], (DEPTH, Q_RANK, MLA_HEADS, QK_HEAD), f32) * Q_RANK ** -0.5
    w_ukv = jax.random.normal(ks[15], (DEPTH, KV_RANK, MLA_HEADS, QK_NOPE + V_HEAD), f32) * KV_RANK ** -0.5
    w_out = jax.random.normal(ks[16], (DEPTH, D_MODEL, D_MODEL), f32) * D_MODEL ** -0.5
    w_ffn_in = jax.random.normal(ks[17], (DEPTH, D_MODEL, 2 * D_FF), f32) * D_MODEL ** -0.5
    w_ffn_out = jax.random.normal(ks[18], (DEPTH, D_FF, D_MODEL), f32) * D_FF ** -0.5
    final_norm_g = 1.0 + 0.01 * jax.random.normal(ks[19], (D_MODEL,), f32)
    return {'x': x, 'c': c, 'positions': positions, 'w_ada': w_ada, 'b_ada': b_ada,
            'w_in': w_in, 'conv_w': conv_w, 'conv_b': conv_b, 'lru_wa': lru_wa,
            'lru_ba': lru_ba, 'lru_wx': lru_wx, 'lru_bx': lru_bx, 'lru_a_param': lru_a_param,
            'q_norm_g': q_norm_g, 'kv_norm_g': kv_norm_g, 'w_uq': w_uq, 'w_ukv': w_ukv,
            'w_out': w_out, 'w_ffn_in': w_ffn_in, 'w_ffn_out': w_ffn_out,
            'final_norm_g': final_norm_g}


def _fwd_reference(x, c, positions, w_ada, b_ada, w_in, conv_w, conv_b, lru_wa, lru_ba, lru_wx,
              lru_bx, lru_a_param, q_norm_g, kv_norm_g, w_uq, w_ukv, w_out, w_ffn_in,
              w_ffn_out, final_norm_g):
    cos, sin = rope_tables(positions)
    for l in range(DEPTH):
        x = hybrid_layer(x, c, positions, cos, sin, w_ada[l], b_ada[l], w_in[l], conv_w[l],
                         conv_b[l], lru_wa[l], lru_ba[l], lru_wx[l], lru_bx[l], lru_a_param[l],
                         q_norm_g[l], kv_norm_g[l], w_uq[l], w_ukv[l], w_out[l],
                         w_ffn_in[l], w_ffn_out[l])
    return rms_norm(x, final_norm_g)


import jax as _jax
import jax.numpy as _jnp

TWIN_FORMAT = 'train_step'
FWD_PARAMS = ['x', 'c', 'positions', 'w_ada', 'b_ada', 'w_in', 'conv_w', 'conv_b', 'lru_wa', 'lru_ba', 'lru_wx', 'lru_bx', 'lru_a_param', 'q_norm_g', 'kv_norm_g', 'w_uq', 'w_ukv', 'w_out', 'w_ffn_in', 'w_ffn_out', 'final_norm_g']
TWIN_WEIGHTS = ['w_ada', 'b_ada', 'w_in', 'conv_w', 'conv_b', 'lru_wa', 'lru_ba', 'lru_wx', 'lru_bx', 'lru_a_param', 'q_norm_g', 'kv_norm_g', 'w_uq', 'w_ukv', 'w_out', 'w_ffn_in', 'w_ffn_out', 'final_norm_g']
TWIN_DIFF_INPUT = 'x'
TWIN_INPUTS = ['x', 'c', 'positions', 'w_ada', 'b_ada', 'w_in', 'conv_w', 'conv_b', 'lru_wa', 'lru_ba', 'lru_wx', 'lru_bx', 'lru_a_param', 'q_norm_g', 'kv_norm_g', 'w_uq', 'w_ukv', 'w_out', 'w_ffn_in', 'w_ffn_out', 'final_norm_g', 'loss_target', 'm_w_ada', 'm_b_ada', 'm_w_in', 'm_conv_w', 'm_conv_b', 'm_lru_wa', 'm_lru_ba', 'm_lru_wx', 'm_lru_bx', 'm_lru_a_param', 'm_q_norm_g', 'm_kv_norm_g', 'm_w_uq', 'm_w_ukv', 'm_w_out', 'm_w_ffn_in', 'm_w_ffn_out', 'm_final_norm_g', 'v_w_ada', 'v_b_ada', 'v_w_in', 'v_conv_w', 'v_conv_b', 'v_lru_wa', 'v_lru_ba', 'v_lru_wx', 'v_lru_bx', 'v_lru_a_param', 'v_q_norm_g', 'v_kv_norm_g', 'v_w_uq', 'v_w_ukv', 'v_w_out', 'v_w_ffn_in', 'v_w_ffn_out', 'v_final_norm_g']
TWIN_OUTPUTS = ['loss', 'grad_x', 'grad_w_ada', 'grad_b_ada', 'grad_w_in', 'grad_conv_w', 'grad_conv_b', 'grad_lru_wa', 'grad_lru_ba', 'grad_lru_wx', 'grad_lru_bx', 'grad_lru_a_param', 'grad_q_norm_g', 'grad_kv_norm_g', 'grad_w_uq', 'grad_w_ukv', 'grad_w_out', 'grad_w_ffn_in', 'grad_w_ffn_out', 'grad_final_norm_g', 'delta_w_ada', 'delta_b_ada', 'delta_w_in', 'delta_conv_w', 'delta_conv_b', 'delta_lru_wa', 'delta_lru_ba', 'delta_lru_wx', 'delta_lru_bx', 'delta_lru_a_param', 'delta_q_norm_g', 'delta_kv_norm_g', 'delta_w_uq', 'delta_w_ukv', 'delta_w_out', 'delta_w_ffn_in', 'delta_w_ffn_out', 'delta_final_norm_g', 'new_m_w_ada', 'new_m_b_ada', 'new_m_w_in', 'new_m_conv_w', 'new_m_conv_b', 'new_m_lru_wa', 'new_m_lru_ba', 'new_m_lru_wx', 'new_m_lru_bx', 'new_m_lru_a_param', 'new_m_q_norm_g', 'new_m_kv_norm_g', 'new_m_w_uq', 'new_m_w_ukv', 'new_m_w_out', 'new_m_w_ffn_in', 'new_m_w_ffn_out', 'new_m_final_norm_g', 'new_v_w_ada', 'new_v_b_ada', 'new_v_w_in', 'new_v_conv_w', 'new_v_conv_b', 'new_v_lru_wa', 'new_v_lru_ba', 'new_v_lru_wx', 'new_v_lru_bx', 'new_v_lru_a_param', 'new_v_q_norm_g', 'new_v_kv_norm_g', 'new_v_w_uq', 'new_v_w_ukv', 'new_v_w_out', 'new_v_w_ffn_in', 'new_v_w_ffn_out', 'new_v_final_norm_g']
TWIN_LEAF_KINDS = {'loss': 'loss', 'grad_x': 'grad_x', 'grad_w_ada': 'grad_w', 'grad_b_ada': 'grad_w', 'grad_w_in': 'grad_w', 'grad_conv_w': 'grad_w', 'grad_conv_b': 'grad_w', 'grad_lru_wa': 'grad_w', 'grad_lru_ba': 'grad_w', 'grad_lru_wx': 'grad_w', 'grad_lru_bx': 'grad_w', 'grad_lru_a_param': 'grad_w', 'grad_q_norm_g': 'grad_w', 'grad_kv_norm_g': 'grad_w', 'grad_w_uq': 'grad_w', 'grad_w_ukv': 'grad_w', 'grad_w_out': 'grad_w', 'grad_w_ffn_in': 'grad_w', 'grad_w_ffn_out': 'grad_w', 'grad_final_norm_g': 'grad_w', 'delta_w_ada': 'delta_w', 'delta_b_ada': 'delta_w', 'delta_w_in': 'delta_w', 'delta_conv_w': 'delta_w', 'delta_conv_b': 'delta_w', 'delta_lru_wa': 'delta_w', 'delta_lru_ba': 'delta_w', 'delta_lru_wx': 'delta_w', 'delta_lru_bx': 'delta_w', 'delta_lru_a_param': 'delta_w', 'delta_q_norm_g': 'delta_w', 'delta_kv_norm_g': 'delta_w', 'delta_w_uq': 'delta_w', 'delta_w_ukv': 'delta_w', 'delta_w_out': 'delta_w', 'delta_w_ffn_in': 'delta_w', 'delta_w_ffn_out': 'delta_w', 'delta_final_norm_g': 'delta_w', 'new_m_w_ada': 'new_m', 'new_m_b_ada': 'new_m', 'new_m_w_in': 'new_m', 'new_m_conv_w': 'new_m', 'new_m_conv_b': 'new_m', 'new_m_lru_wa': 'new_m', 'new_m_lru_ba': 'new_m', 'new_m_lru_wx': 'new_m', 'new_m_lru_bx': 'new_m', 'new_m_lru_a_param': 'new_m', 'new_m_q_norm_g': 'new_m', 'new_m_kv_norm_g': 'new_m', 'new_m_w_uq': 'new_m', 'new_m_w_ukv': 'new_m', 'new_m_w_out': 'new_m', 'new_m_w_ffn_in': 'new_m', 'new_m_w_ffn_out': 'new_m', 'new_m_final_norm_g': 'new_m', 'new_v_w_ada': 'new_v', 'new_v_b_ada': 'new_v', 'new_v_w_in': 'new_v', 'new_v_conv_w': 'new_v', 'new_v_conv_b': 'new_v', 'new_v_lru_wa': 'new_v', 'new_v_lru_ba': 'new_v', 'new_v_lru_wx': 'new_v', 'new_v_lru_bx': 'new_v', 'new_v_lru_a_param': 'new_v', 'new_v_q_norm_g': 'new_v', 'new_v_kv_norm_g': 'new_v', 'new_v_w_uq': 'new_v', 'new_v_w_ukv': 'new_v', 'new_v_w_out': 'new_v', 'new_v_w_ffn_in': 'new_v', 'new_v_w_ffn_out': 'new_v', 'new_v_final_norm_g': 'new_v'}


def _forward(args):
    return _fwd_reference(*[args[k] for k in FWD_PARAMS])


def _output_shape():
    def fwd():
        inp = _fwd_setup_inputs(0)
        return _fwd_reference(*[inp[k] for k in FWD_PARAMS])
    out = _jax.eval_shape(fwd)
    return out.shape, out.dtype

N_MICROBATCH = 1
ADAM_LR = 0.001
ADAM_B1 = 0.9
ADAM_B2 = 0.999
ADAM_EPS = 1e-08
ADAM_WD = 0.01
ADAM_STEP = 10
PER_EXAMPLE_BATCH_AXIS = {'x': 0, 'c': 0, 'positions': 0, 'loss_target': 0}
SHARED_INPUTS = []
_WEIGHT_DTYPES = {'w_ada': _jnp.float32, 'b_ada': _jnp.float32, 'w_in': _jnp.float32, 'conv_w': _jnp.float32, 'conv_b': _jnp.float32, 'lru_wa': _jnp.float32, 'lru_ba': _jnp.float32, 'lru_wx': _jnp.float32, 'lru_bx': _jnp.float32, 'lru_a_param': _jnp.float32, 'q_norm_g': _jnp.float32, 'kv_norm_g': _jnp.float32, 'w_uq': _jnp.float32, 'w_ukv': _jnp.float32, 'w_out': _jnp.float32, 'w_ffn_in': _jnp.float32, 'w_ffn_out': _jnp.float32, 'final_norm_g': _jnp.float32}
MOMENT_SCALE = {'w_ada': 2.368391e-01, 'b_ada': 2.286922e-01, 'w_in': 7.724673e-02, 'conv_w': 1.410701e-01, 'conv_b': 3.902589e-01, 'lru_wa': 1.359012e-02, 'lru_ba': 2.604695e-02, 'lru_wx': 2.888684e-02, 'lru_bx': 5.283174e-02, 'lru_a_param': 6.702141e-02, 'q_norm_g': 1.353671e-02, 'kv_norm_g': 5.027152e-02, 'w_uq': 5.990350e-03, 'w_ukv': 1.176957e-02, 'w_out': 1.322888e-01, 'w_ffn_in': 3.112412e-02, 'w_ffn_out': 5.065833e-02, 'final_norm_g': 6.415219e+01}


def _to_microbatches(a, axis):
    t = _jnp.moveaxis(a, axis, 0)
    t = t.reshape((N_MICROBATCH, t.shape[0] // N_MICROBATCH) + t.shape[1:])
    return _jnp.moveaxis(t, 1, axis + 1)


def setup_inputs(seed: int = 0) -> dict:
    inp = _fwd_setup_inputs(seed)
    key = _jax.random.fold_in(_jax.random.key(seed), 7919)
    shape, _ = _output_shape()
    out = dict(inp)
    out["loss_target"] = _jax.random.normal(_jax.random.fold_in(key, 0), shape, _jnp.float32)
    for i, name in enumerate(TWIN_WEIGHTS):
        w = inp[name].astype(_jnp.float32)
        if MOMENT_SCALE is None:
            s = _jnp.sqrt(_jnp.mean(_jnp.square(w)) + 1e-30)
        else:
            s = MOMENT_SCALE[name]
        km, kv = _jax.random.split(_jax.random.fold_in(key, i + 1))
        out[name] = w
        out["m_" + name] = s * _jax.random.normal(km, w.shape, _jnp.float32)
        out["v_" + name] = (s * s) * _jax.random.uniform(kv, w.shape, _jnp.float32, 0.5, 1.5)
    if N_MICROBATCH > 1:
        for name, axis in PER_EXAMPLE_BATCH_AXIS.items():
            out[name] = _to_microbatches(out[name], axis)
    return {'x': out['x'], 'c': out['c'], 'positions': out['positions'], 'w_ada': out['w_ada'], 'b_ada': out['b_ada'], 'w_in': out['w_in'], 'conv_w': out['conv_w'], 'conv_b': out['conv_b'], 'lru_wa': out['lru_wa'], 'lru_ba': out['lru_ba'], 'lru_wx': out['lru_wx'], 'lru_bx': out['lru_bx'], 'lru_a_param': out['lru_a_param'], 'q_norm_g': out['q_norm_g'], 'kv_norm_g': out['kv_norm_g'], 'w_uq': out['w_uq'], 'w_ukv': out['w_ukv'], 'w_out': out['w_out'], 'w_ffn_in': out['w_ffn_in'], 'w_ffn_out': out['w_ffn_out'], 'final_norm_g': out['final_norm_g'], 'loss_target': out['loss_target'], 'm_w_ada': out['m_w_ada'], 'm_b_ada': out['m_b_ada'], 'm_w_in': out['m_w_in'], 'm_conv_w': out['m_conv_w'], 'm_conv_b': out['m_conv_b'], 'm_lru_wa': out['m_lru_wa'], 'm_lru_ba': out['m_lru_ba'], 'm_lru_wx': out['m_lru_wx'], 'm_lru_bx': out['m_lru_bx'], 'm_lru_a_param': out['m_lru_a_param'], 'm_q_norm_g': out['m_q_norm_g'], 'm_kv_norm_g': out['m_kv_norm_g'], 'm_w_uq': out['m_w_uq'], 'm_w_ukv': out['m_w_ukv'], 'm_w_out': out['m_w_out'], 'm_w_ffn_in': out['m_w_ffn_in'], 'm_w_ffn_out': out['m_w_ffn_out'], 'm_final_norm_g': out['m_final_norm_g'], 'v_w_ada': out['v_w_ada'], 'v_b_ada': out['v_b_ada'], 'v_w_in': out['v_w_in'], 'v_conv_w': out['v_conv_w'], 'v_conv_b': out['v_conv_b'], 'v_lru_wa': out['v_lru_wa'], 'v_lru_ba': out['v_lru_ba'], 'v_lru_wx': out['v_lru_wx'], 'v_lru_bx': out['v_lru_bx'], 'v_lru_a_param': out['v_lru_a_param'], 'v_q_norm_g': out['v_q_norm_g'], 'v_kv_norm_g': out['v_kv_norm_g'], 'v_w_uq': out['v_w_uq'], 'v_w_ukv': out['v_w_ukv'], 'v_w_out': out['v_w_out'], 'v_w_ffn_in': out['v_w_ffn_in'], 'v_w_ffn_out': out['v_w_ffn_out'], 'v_final_norm_g': out['v_final_norm_g']}


def _loss(weights, diff, rest, loss_target):
    with _jax.named_scope("forward"):
        args = {**rest, TWIN_DIFF_INPUT: diff, **{k: w.astype(_WEIGHT_DTYPES[k]) for k, w in weights.items()}}
        y = _forward(args)
    with _jax.named_scope("loss_head"):
        err = _jnp.square(y.astype(_jnp.float32) - loss_target)
        return 0.5 * _jnp.sum(_jnp.mean(err, axis=-1)) if err.ndim else 0.5 * err


def _adamw(w, g, m, v):
    m = ADAM_B1 * m + (1.0 - ADAM_B1) * g
    v = ADAM_B2 * v + (1.0 - ADAM_B2) * _jnp.square(g)
    m_hat = m / (1.0 - ADAM_B1 ** ADAM_STEP)
    v_hat = v / (1.0 - ADAM_B2 ** ADAM_STEP)
    delta = -ADAM_LR * (m_hat / (_jnp.sqrt(v_hat) + ADAM_EPS) + ADAM_WD * w)
    return delta, m, v


def reference(x, c, positions, w_ada, b_ada, w_in, conv_w, conv_b, lru_wa, lru_ba, lru_wx, lru_bx, lru_a_param, q_norm_g, kv_norm_g, w_uq, w_ukv, w_out, w_ffn_in, w_ffn_out, final_norm_g, loss_target, m_w_ada, m_b_ada, m_w_in, m_conv_w, m_conv_b, m_lru_wa, m_lru_ba, m_lru_wx, m_lru_bx, m_lru_a_param, m_q_norm_g, m_kv_norm_g, m_w_uq, m_w_ukv, m_w_out, m_w_ffn_in, m_w_ffn_out, m_final_norm_g, v_w_ada, v_b_ada, v_w_in, v_conv_w, v_conv_b, v_lru_wa, v_lru_ba, v_lru_wx, v_lru_bx, v_lru_a_param, v_q_norm_g, v_kv_norm_g, v_w_uq, v_w_ukv, v_w_out, v_w_ffn_in, v_w_ffn_out, v_final_norm_g):
    given = dict(x=x, c=c, positions=positions, w_ada=w_ada, b_ada=b_ada, w_in=w_in, conv_w=conv_w, conv_b=conv_b, lru_wa=lru_wa, lru_ba=lru_ba, lru_wx=lru_wx, lru_bx=lru_bx, lru_a_param=lru_a_param, q_norm_g=q_norm_g, kv_norm_g=kv_norm_g, w_uq=w_uq, w_ukv=w_ukv, w_out=w_out, w_ffn_in=w_ffn_in, w_ffn_out=w_ffn_out, final_norm_g=final_norm_g, loss_target=loss_target, m_w_ada=m_w_ada, m_b_ada=m_b_ada, m_w_in=m_w_in, m_conv_w=m_conv_w, m_conv_b=m_conv_b, m_lru_wa=m_lru_wa, m_lru_ba=m_lru_ba, m_lru_wx=m_lru_wx, m_lru_bx=m_lru_bx, m_lru_a_param=m_lru_a_param, m_q_norm_g=m_q_norm_g, m_kv_norm_g=m_kv_norm_g, m_w_uq=m_w_uq, m_w_ukv=m_w_ukv, m_w_out=m_w_out, m_w_ffn_in=m_w_ffn_in, m_w_ffn_out=m_w_ffn_out, m_final_norm_g=m_final_norm_g, v_w_ada=v_w_ada, v_b_ada=v_b_ada, v_w_in=v_w_in, v_conv_w=v_conv_w, v_conv_b=v_conv_b, v_lru_wa=v_lru_wa, v_lru_ba=v_lru_ba, v_lru_wx=v_lru_wx, v_lru_bx=v_lru_bx, v_lru_a_param=v_lru_a_param, v_q_norm_g=v_q_norm_g, v_kv_norm_g=v_kv_norm_g, v_w_uq=v_w_uq, v_w_ukv=v_w_ukv, v_w_out=v_w_out, v_w_ffn_in=v_w_ffn_in, v_w_ffn_out=v_w_ffn_out, v_final_norm_g=v_final_norm_g)
    weights = {n: given[n] for n in TWIN_WEIGHTS}
    shared = {n: given[n] for n in SHARED_INPUTS}
    per_example = {n: given[n] for n in ['x', 'c', 'positions']}
    grad_fn = _jax.value_and_grad(_loss, argnums=(0, 1))

    def one_microbatch(ex, loss_target):
        ex = dict(ex)
        diff = ex.pop(TWIN_DIFF_INPUT)
        return grad_fn(weights, diff, {**shared, **ex}, loss_target)

    if N_MICROBATCH == 1:
        loss, (grad_w, grad_x) = one_microbatch(per_example, given["loss_target"])
    else:
        def body(carry, xs):
            loss_sum, grad_sum = carry
            l_k, (gw_k, gx_k) = one_microbatch(xs[0], xs[1])
            with _jax.named_scope("update"):
                return (loss_sum + l_k, _jax.tree.map(_jnp.add, grad_sum, gw_k)), gx_k

        init = (_jnp.zeros((), _jnp.float32), _jax.tree.map(_jnp.zeros_like, weights))
        (loss, grad_w), grad_x = _jax.lax.scan(body, init, (per_example, given["loss_target"]))
    with _jax.named_scope("update"):
        delta_w, new_m, new_v = {}, {}, {}
        for n in TWIN_WEIGHTS:
            delta_w[n], new_m[n], new_v[n] = _adamw(weights[n], grad_w[n], given["m_" + n], given["v_" + n])
    return (loss, grad_x, *[grad_w[n] for n in TWIN_WEIGHTS], *[delta_w[n] for n in TWIN_WEIGHTS],
            *[new_m[n] for n in TWIN_WEIGHTS], *[new_v[n] for n in TWIN_WEIGHTS])
```

```python
import functools

import jax
import jax.numpy as jnp
from jax import lax
from jax.experimental import pallas as pl
from jax.experimental.pallas import tpu as pltpu

f32 = jnp.float32
bf16 = jnp.bfloat16

D = 1024
NH = 8
DN = 128
DR = 64
DV = 128
DH = 256
QR = 256
KVR = 128
DFF = 2816
NB = 8
BW = 128
CONV = 4
PW = 3584
QKV0 = 3 * D
DIN = 3520
EPS = 1e-6
LRU_C = 8.0
ROPE_THETA = 10000.0
SCALE = (DN + DR) ** -0.5
NEG = -1e30

ADAM_LR = 0.001
ADAM_B1 = 0.9
ADAM_B2 = 0.999
ADAM_EPS = 1e-08
ADAM_WD = 0.01
ADAM_STEP = 10

VMEM_LIMIT = 56 * 1024 * 1024
ROW_BLOCK = 256
ATTN_BLOCK = 512
MESH = pl.DeviceIdType.MESH
ANY = pl.BlockSpec(memory_space=pl.ANY)
CHIP_MASKS = ((1, 0, 0), (0, 1, 0), (1, 1, 0))
ALL_MASKS = ((0, 0, 1), (1, 0, 0), (0, 1, 0), (1, 1, 0), (1, 0, 1), (0, 1, 1), (1, 1, 1))


def _cparams(n_axes):
    return pltpu.CompilerParams(dimension_semantics=("arbitrary",) * n_axes, vmem_limit_bytes=VMEM_LIMIT)


def _pick(n, cands):
    for c in cands:
        if n % c == 0:
            return c
    return n


def _rowwise(name, body, T, tb, row_ins, vec_ins, row_outs, acc_outs=(), scratch=(), reverse=False):
    n = T // tb

    def rmap(i):
        return (n - 1 - i) if reverse else i

    in_specs = []
    for (_, w, cb) in row_ins:
        in_specs.append(pl.BlockSpec((tb, w), functools.partial(lambda i, cb: (rmap(i), cb), cb=cb)))
    for v in vec_ins:
        in_specs.append(pl.BlockSpec(v.shape, functools.partial(lambda i, nd: (0,) * nd, nd=v.ndim)))
    out_specs, out_shape = [], []
    for (w, dt) in row_outs:
        out_specs.append(pl.BlockSpec((tb, w), lambda i: (rmap(i), 0)))
        out_shape.append(jax.ShapeDtypeStruct((T, w), dt))
    for (s, dt) in acc_outs:
        out_specs.append(pl.BlockSpec(s, functools.partial(lambda i, nd: (0,) * nd, nd=len(s))))
        out_shape.append(jax.ShapeDtypeStruct(s, dt))

    def kern(*refs):
        body(*refs)

    return pl.pallas_call(
        kern, name=name, grid=(n,), in_specs=in_specs, out_specs=out_specs, out_shape=out_shape,
        scratch_shapes=list(scratch), compiler_params=_cparams(1),
    )(*[a for (a, _, _) in row_ins], *vec_ins)


def _matmul(name, a, b, mode, out_dtype=f32):
    if mode == "nn":
        (M, K), N = a.shape, b.shape[1]
    elif mode == "nt":
        (M, K), N = a.shape, b.shape[0]
    else:
        (K, M), N = a.shape, b.shape[1]
    tm = _pick(M, (512, 256, 128)) if mode != "tn" else _pick(M, (1024, 1408, 512, 256, 128))
    tn = _pick(N, (1024, 1408, 512, 256, 128))
    tk = _pick(K, (1024, 1408, 512, 256, 128)) if mode != "tn" else _pick(K, (512, 256, 128))
    nk = K // tk
    dims = {"nn": (((1,), (0,)), ((), ())), "nt": (((1,), (1,)), ((), ())), "tn": (((0,), (0,)), ((), ()))}[mode]
    if mode == "tn":
        a_spec = pl.BlockSpec((tk, tm), lambda j, i, k: (k, i))
    else:
        a_spec = pl.BlockSpec((tm, tk), lambda j, i, k: (i, k))
    if mode == "nt":
        b_spec = pl.BlockSpec((tn, tk), lambda j, i, k: (j, k))
    else:
        b_spec = pl.BlockSpec((tk, tn), lambda j, i, k: (k, j))
    o_spec = pl.BlockSpec((tm, tn), lambda j, i, k: (i, j))
    use_acc = nk > 1 and out_dtype != f32

    def kern(a_ref, b_ref, o_ref, *scr):
        k = pl.program_id(2)
        d = lax.dot_general(a_ref[...].astype(bf16), b_ref[...].astype(bf16), dims, preferred_element_type=f32)
        if nk == 1:
            o_ref[...] = d.astype(out_dtype)
        else:
            acc = scr[0] if use_acc else o_ref

            @pl.when(k == 0)
            def _():
                acc[...] = d

            @pl.when(k > 0)
            def _():
                acc[...] += d

            if use_acc:
                @pl.when(k == nk - 1)
                def _():
                    o_ref[...] = acc[...].astype(out_dtype)

    return pl.pallas_call(
        kern, name=name, grid=(N // tn, M // tm, nk), in_specs=[a_spec, b_spec], out_specs=o_spec,
        out_shape=jax.ShapeDtypeStruct((M, N), out_dtype),
        scratch_shapes=[pltpu.VMEM((tm, tn), f32)] if use_acc else [],
        compiler_params=_cparams(3),
    )(a, b)


def _colsum(v):
    return jnp.sum(v, axis=0, keepdims=True)


def _init_acc(step, *refs):
    @pl.when(step == 0)
    def _():
        for r in refs:
            r[...] = jnp.zeros_like(r)


def _norm_mod(name, x, sc, sh, T, tb):
    def body(x_ref, sc_ref, sh_ref, h_ref):
        xv = x_ref[...]
        rstd = lax.rsqrt(jnp.mean(xv * xv, axis=-1, keepdims=True) + EPS)
        h_ref[...] = (xv * rstd * (1.0 + sc_ref[...]) + sh_ref[...]).astype(bf16)

    return _rowwise(name, body, T, tb, [(x, D, 0)], [sc, sh], [(D, bf16)])[0]


def _resid_norm_mod(name, x, g, m, sc, sh, T, tb):
    def body(x_ref, m_ref, g_ref, sc_ref, sh_ref, x2_ref, h_ref):
        xv = x_ref[...] + g_ref[...] * m_ref[...]
        x2_ref[...] = xv
        rstd = lax.rsqrt(jnp.mean(xv * xv, axis=-1, keepdims=True) + EPS)
        h_ref[...] = (xv * rstd * (1.0 + sc_ref[...]) + sh_ref[...]).astype(bf16)

    return _rowwise(name, body, T, tb, [(x, D, 0), (m, D, 0)], [g, sc, sh], [(D, f32), (D, bf16)])


def _norm_mod_bwd(name, dh, x, sc, dres, T, tb):
    def body(dh_ref, x_ref, dr_ref, sc_ref, dx_ref, dsc_ref, dsh_ref):
        _init_acc(pl.program_id(0), dsc_ref, dsh_ref)
        xv = x_ref[...]
        dhv = dh_ref[...]
        rstd = lax.rsqrt(jnp.mean(xv * xv, axis=-1, keepdims=True) + EPS)
        nv = xv * rstd
        dn = dhv * (1.0 + sc_ref[...])
        dx_ref[...] = dr_ref[...] + rstd * (dn - nv * jnp.mean(dn * nv, axis=-1, keepdims=True))
        dsc_ref[...] += _colsum(dhv * nv)
        dsh_ref[...] += _colsum(dhv)

    return _rowwise(name, body, T, tb, [(dh, D, 0), (x, D, 0), (dres, D, 0)], [sc], [(D, f32)],
                    [((1, D), f32), ((1, D), f32)])


def _resid_bwd(name, dx, g, m, T, tb):
    def body(dx_ref, m_ref, g_ref, dm_ref, dg_ref):
        _init_acc(pl.program_id(0), dg_ref)
        dxv = dx_ref[...]
        dm_ref[...] = (dxv * g_ref[...]).astype(bf16)
        dg_ref[...] += _colsum(dxv * m_ref[...])

    return _rowwise(name, body, T, tb, [(dx, D, 0), (m, D, 0)], [g], [(D, bf16)], [((1, D), f32)])


def _expm1_neg(y):
    poly = y * (1.0 + y * (0.5 + y * (1.0 / 6 + y * (1.0 / 24 + y * (1.0 / 120 + y * (1.0 / 720 + y * (1.0 / 5040)))))))
    return jnp.where(y > -0.25, poly, jnp.exp(y) - 1.0)


def _lru_gates(r, sp, reset):
    log_a = -LRU_C * r * sp
    a = jnp.where(reset, 0.0, jnp.exp(log_a))
    mult = jnp.where(reset, 1.0, jnp.sqrt(-_expm1_neg(2.0 * log_a)))
    return a, mult


def _block_dot(v, w_ref, dims):
    outs = [lax.dot_general(v[:, n * BW:(n + 1) * BW], w_ref[n], dims, preferred_element_type=f32) for n in range(NB)]
    return jnp.concatenate(outs, axis=1)


def _lru_fwd(name, proj, pos, cw, cb, wa, ba, wx, bx, ap, T, tb):
    nsteps = tb.bit_length() - 1

    def body(x_ref, pos_ref, cw_ref, cb_ref, wa_ref, ba_ref, wx_ref, bx_ref, ap_ref,
             u_ref, r_ref, i_ref, h_ref, prev_x, carry_h):
        _init_acc(pl.program_id(0), prev_x, carry_h)
        x = x_ref[...]
        px = prev_x[...]
        row = lax.broadcasted_iota(jnp.int32, (tb, 1), 0)
        u = x * cw_ref[CONV - 1:CONV, :] + cb_ref[...]
        for j in range(1, CONV):
            z = jnp.where(row >= tb - j, px, x)
            u = u + pltpu.roll(z, j, 0) * cw_ref[CONV - 1 - j:CONV - j, :]
        ub = u.astype(bf16)
        nn = (((1,), (0,)), ((), ()))
        r = jax.nn.sigmoid(_block_dot(ub, wa_ref, nn) + ba_ref[...])
        ig = jax.nn.sigmoid(_block_dot(ub, wx_ref, nn) + bx_ref[...])
        sp = jax.nn.softplus(-ap_ref[...])
        reset = pos_ref[...] == 0
        a, mult = _lru_gates(r, sp, reset)
        b = u * ig * mult
        for k in range(nsteps):
            s = 1 << k
            keep = row >= s
            a_sh = jnp.where(keep, pltpu.roll(a, s, 0), 1.0)
            b_sh = jnp.where(keep, pltpu.roll(b, s, 0), 0.0)
            b = a * b_sh + b
            a = a * a_sh
        h = b + a * carry_h[...]
        u_ref[...] = u
        r_ref[...] = r
        i_ref[...] = ig
        h_ref[...] = h
        carry_h[...] = h[tb - 1:tb, :]
        prev_x[...] = x

    return _rowwise(name, body, T, tb, [(proj, D, 0), (pos, 1, 0)], [cw, cb, wa, ba, wx, bx, ap],
                    [(D, f32)] * 4, scratch=[pltpu.VMEM((tb, D), f32), pltpu.VMEM((1, D), f32)])


def _lru_bwd(name, dh, proj, u, r, ig, h, pos, cw, wa, wx, ap, T, tb):
    n = T // tb
    nsteps = tb.bit_length() - 1
    t8 = tb // 8

    def rmap(i):
        return n - 1 - i

    def body(dh_ref, x_ref, u_ref, r_ref, i_ref, h_ref, hp_ref, pos_ref, cw_ref, wa_ref, wx_ref, ap_ref,
             dx_ref, dcw_ref, dcb_ref, dwa_ref, dba_ref, dwx_ref, dbx_ref, dsp_ref, next_du, carry_g, carry_a):
        step = pl.program_id(0)
        _init_acc(step, dcw_ref, dcb_ref, dwa_ref, dba_ref, dwx_ref, dbx_ref, dsp_ref, next_du, carry_g, carry_a)
        blk = n - 1 - step
        x = x_ref[...]
        u = u_ref[...]
        r = r_ref[...]
        ig = i_ref[...]
        h = h_ref[...]
        row = lax.broadcasted_iota(jnp.int32, (tb, 1), 0)
        sp = jax.nn.softplus(-ap_ref[...])
        reset = pos_ref[...] == 0
        a, mult = _lru_gates(r, sp, reset)
        A = pltpu.roll(jnp.where(row == 0, carry_a[...], a), tb - 1, 0)
        B = dh_ref[...]
        for k in range(nsteps):
            s = 1 << k
            keep = row < tb - s
            a_sh = jnp.where(keep, pltpu.roll(A, tb - s, 0), 1.0)
            b_sh = jnp.where(keep, pltpu.roll(B, tb - s, 0), 0.0)
            B = B + A * b_sh
            A = A * a_sh
        g = B + A * carry_g[...]
        carry_g[...] = g[0:1, :]
        carry_a[...] = a[0:1, :]
        h_last = jnp.where(blk == 0, 0.0, hp_ref[7:8, :])
        h_prev = pltpu.roll(jnp.where(row == tb - 1, h_last, h), 1, 0)
        da = g * h_prev
        gu = g * u
        dlog_a = jnp.where(reset, 0.0, da * a - gu * ig * (a * a) / mult)
        du = g * ig * mult
        di = gu * mult
        dr = dlog_a * (-LRU_C * sp)
        dsp_ref[...] += _colsum(dlog_a * (-LRU_C * r)) * (-jax.nn.sigmoid(-ap_ref[...]))
        dzr = dr * r * (1.0 - r)
        dzi = di * ig * (1.0 - ig)
        dzr_b = dzr.astype(bf16)
        dzi_b = dzi.astype(bf16)
        nt = (((1,), (1,)), ((), ()))
        du = du + _block_dot(dzr_b, wa_ref, nt) + _block_dot(dzi_b, wx_ref, nt)
        ub = u.astype(bf16)
        tn = (((0,), (0,)), ((), ()))
        for nb in range(NB):
            sl = slice(nb * BW, (nb + 1) * BW)
            dwa_ref[nb] += lax.dot_general(ub[:, sl], dzr_b[:, sl], tn, preferred_element_type=f32)
            dwx_ref[nb] += lax.dot_general(ub[:, sl], dzi_b[:, sl], tn, preferred_element_type=f32)
        dba_ref[...] += _colsum(dzr)
        dbx_ref[...] += _colsum(dzi)
        dcb_ref[...] += _colsum(du)
        ndu = next_du[...]
        dx = du * cw_ref[CONV - 1:CONV, :]
        dcw_ref[CONV - 1:CONV, :] += _colsum(x * du)
        for j in range(1, CONV):
            up = pltpu.roll(jnp.where(row < j, ndu, du), tb - j, 0)
            dx = dx + up * cw_ref[CONV - 1 - j:CONV - j, :]
            dcw_ref[CONV - 1 - j:CONV - j, :] += _colsum(x * up)
        dx_ref[...] = dx.astype(bf16)
        next_du[...] = du

    row_specs = [pl.BlockSpec((tb, D), lambda i: (rmap(i), 0)) for _ in range(6)]
    in_specs = row_specs + [
        pl.BlockSpec((8, D), lambda i: (jnp.maximum(rmap(i) * t8 - 1, 0), 0)),
        pl.BlockSpec((tb, 1), lambda i: (rmap(i), 0)),
        pl.BlockSpec((CONV, D), lambda i: (0, 0)),
        pl.BlockSpec((NB, BW, BW), lambda i: (0, 0, 0)),
        pl.BlockSpec((NB, BW, BW), lambda i: (0, 0, 0)),
        pl.BlockSpec((1, D), lambda i: (0, 0)),
    ]
    vec = lambda s: pl.BlockSpec(s, functools.partial(lambda i, nd: (0,) * nd, nd=len(s)))
    acc_shapes = [(CONV, D), (1, D), (NB, BW, BW), (1, D), (NB, BW, BW), (1, D), (1, D)]
    return pl.pallas_call(
        body, name=name, grid=(n,), in_specs=in_specs,
        out_specs=[pl.BlockSpec((tb, D), lambda i: (rmap(i), 0))] + [vec(s) for s in acc_shapes],
        out_shape=[jax.ShapeDtypeStruct((T, D), bf16)] + [jax.ShapeDtypeStruct(s, f32) for s in acc_shapes],
        scratch_shapes=[pltpu.VMEM((tb, D), f32), pltpu.VMEM((1, D), f32), pltpu.VMEM((1, D), f32)],
        compiler_params=_cparams(1),
    )(dh, proj, u, r, ig, h, h, pos, cw, wa, wx, ap)


def _rope_tables(name, pos, invf, T, tb):
    def body(pos_ref, invf_ref, cos_ref, sa_ref, sb_ref):
        ang = pos_ref[...].astype(f32) * invf_ref[...]
        lane = lax.broadcasted_iota(jnp.int32, (tb, DH), 1)
        first = (lane >= DN) & (lane < DN + DR // 2)
        second = (lane >= DN + DR // 2) & (lane < DN + DR)
        c = jnp.cos(ang)
        s = jnp.sin(ang)
        cos_ref[...] = jnp.where(lane < DN, 1.0, jnp.where(first | second, c, 0.0))
        sa_ref[...] = jnp.where(first, -s, 0.0)
        sb_ref[...] = jnp.where(second, s, 0.0)

    return _rowwise(name, body, T, tb, [(pos, 1, 0)], [invf], [(DH, f32)] * 3)


def _rope(z, cos8, sa8, sb8):
    w = z.shape[1]
    return z * cos8 + pltpu.roll(z, w - DR // 2, 1) * sa8 + pltpu.roll(z, DR // 2, 1) * sb8


def _rope_t(d, cos8, sa8, sb8):
    w = d.shape[1]
    return d * cos8 + pltpu.roll(d * sa8, DR // 2, 1) + pltpu.roll(d * sb8, w - DR // 2, 1)


def _mla_prep(name, proj, cosp, sap, sbp, gq, gkv, wq, wk, wv, T, tb):
    def body(qkv_ref, cos_ref, sa_ref, sb_ref, gq_ref, gkv_ref, wq_ref, wk_ref, wv_ref,
             q_ref, k_ref, v_ref, cq_ref, ckv_ref):
        qkv = qkv_ref[...]
        qd = qkv[:, :QR]
        kvd = qkv[:, QR:QR + KVR]
        slab = qkv[:, QR + KVR:]
        cq = (qd * lax.rsqrt(jnp.mean(qd * qd, axis=-1, keepdims=True) + EPS) * gq_ref[...]).astype(bf16)
        ckv = (kvd * lax.rsqrt(jnp.mean(kvd * kvd, axis=-1, keepdims=True) + EPS) * gkv_ref[...]).astype(bf16)
        cos8 = jnp.tile(cos_ref[...], (1, NH))
        sa8 = jnp.tile(sa_ref[...], (1, NH))
        sb8 = jnp.tile(sb_ref[...], (1, NH))
        qpre = jnp.dot(cq, wq_ref[...], preferred_element_type=f32)
        shared = jnp.concatenate([jnp.zeros((tb, DN), f32), slab], axis=1)
        kpre = jnp.dot(ckv, wk_ref[...], preferred_element_type=f32) + jnp.tile(shared, (1, NH))
        q_ref[...] = _rope(qpre, cos8, sa8, sb8).astype(bf16)
        k_ref[...] = _rope(kpre, cos8, sa8, sb8).astype(bf16)
        v_ref[...] = jnp.dot(ckv, wv_ref[...], preferred_element_type=f32).astype(bf16)
        cq_ref[...] = cq
        ckv_ref[...] = ckv

    return _rowwise(name, body, T, tb,
                    [(proj, 512, QKV0 // 512), (cosp, DH, 0), (sap, DH, 0), (sbp, DH, 0)],
                    [gq, gkv, wq, wk, wv],
                    [(NH * DH, bf16), (NH * DH, bf16), (NH * DV, bf16), (QR, bf16), (KVR, bf16)])


def _mla_bwd(name, dq, dk, dv, proj, cosp, sap, sbp, gq, gkv, wq, wk, wv, T, tb):
    def body(dq_ref, dk_ref, dv_ref, qkv_ref, cos_ref, sa_ref, sb_ref, gq_ref, gkv_ref, wq_ref, wk_ref, wv_ref,
             dqkv_ref, dqp_ref, dkp_ref, dgq_ref, dgkv_ref):
        _init_acc(pl.program_id(0), dgq_ref, dgkv_ref)
        cos8 = jnp.tile(cos_ref[...], (1, NH))
        sa8 = jnp.tile(sa_ref[...], (1, NH))
        sb8 = jnp.tile(sb_ref[...], (1, NH))
        dqp = _rope_t(dq_ref[...], cos8, sa8, sb8)
        dkp = _rope_t(dk_ref[...], cos8, sa8, sb8)
        dqp_b = dqp.astype(bf16)
        dkp_b = dkp.astype(bf16)
        dqp_ref[...] = dqp_b
        dkp_ref[...] = dkp_b
        nt = (((1,), (1,)), ((), ()))
        dcq = lax.dot_general(dqp_b, wq_ref[...], nt, preferred_element_type=f32)
        dckv = (lax.dot_general(dkp_b, wk_ref[...], nt, preferred_element_type=f32)
                + lax.dot_general(dv_ref[...], wv_ref[...], nt, preferred_element_type=f32))
        dshared = dkp[:, 0:DH]
        for hd in range(1, NH):
            dshared = dshared + dkp[:, hd * DH:(hd + 1) * DH]
        qkv = qkv_ref[...]
        qd = qkv[:, :QR]
        kvd = qkv[:, QR:QR + KVR]
        nq = qd * lax.rsqrt(jnp.mean(qd * qd, axis=-1, keepdims=True) + EPS)
        nkv = kvd * lax.rsqrt(jnp.mean(kvd * kvd, axis=-1, keepdims=True) + EPS)
        dgq_ref[...] += _colsum(dcq * nq)
        dgkv_ref[...] += _colsum(dckv * nkv)
        dnq = dcq * gq_ref[...]
        dnkv = dckv * gkv_ref[...]
        rq = lax.rsqrt(jnp.mean(qd * qd, axis=-1, keepdims=True) + EPS)
        rkv = lax.rsqrt(jnp.mean(kvd * kvd, axis=-1, keepdims=True) + EPS)
        dqd = rq * (dnq - nq * jnp.mean(dnq * nq, axis=-1, keepdims=True))
        dkvd = rkv * (dnkv - nkv * jnp.mean(dnkv * nkv, axis=-1, keepdims=True))
        dqkv_ref[...] = jnp.concatenate([dqd, dkvd, dshared[:, DN:]], axis=1).astype(bf16)

    return _rowwise(name, body, T, tb,
                    [(dq, NH * DH, 0), (dk, NH * DH, 0), (dv, NH * DV, 0), (proj, 512, QKV0 // 512),
                     (cosp, DH, 0), (sap, DH, 0), (sbp, DH, 0)],
                    [gq, gkv, wq, wk, wv],
                    [(512, bf16), (NH * DH, bf16), (NH * DH, bf16)],
                    [((1, QR), f32), ((1, KVR), f32)])


def _flash_fwd(name, q, k, v, T, tq):
    nq = T // tq

    def body(q_ref, k_ref, v_ref, o_ref, lse_ref):
        i = pl.program_id(1)
        qv = q_ref[...]
        nt = (((1,), (1,)), ((), ()))

        def step(j, carry, masked):
            m, l, acc = carry
            off = pl.multiple_of(j * tq, tq)
            kj = k_ref[pl.ds(off, tq), :]
            vj = v_ref[pl.ds(off, tq), :]
            s = lax.dot_general(qv, kj, nt, preferred_element_type=f32) * SCALE
            if masked:
                rr = lax.broadcasted_iota(jnp.int32, (tq, tq), 0)
                cc = lax.broadcasted_iota(jnp.int32, (tq, tq), 1)
                s = jnp.where(cc <= rr, s, NEG)
            m_new = jnp.maximum(m, jnp.max(s, axis=-1, keepdims=True))
            alpha = jnp.exp(m - m_new)
            p = jnp.exp(s - m_new)
            l = alpha * l + jnp.sum(p, axis=-1, keepdims=True)
            acc = alpha * acc + jnp.dot(p.astype(bf16), vj, preferred_element_type=f32)
            return m_new, l, acc

        init = (jnp.full((tq, 1), NEG, f32), jnp.zeros((tq, 1), f32), jnp.zeros((tq, DV), f32))
        carry = lax.fori_loop(0, i, lambda j, c: step(j, c, False), init)
        m, l, acc = step(i, carry, True)
        o_ref[...] = acc / l
        lse_ref[0] = m + jnp.log(l)

    return pl.pallas_call(
        body, name=name, grid=(NH, nq),
        in_specs=[pl.BlockSpec((tq, DH), lambda h, i: (i, h)),
                  pl.BlockSpec((T, DH), lambda h, i: (0, h)),
                  pl.BlockSpec((T, DV), lambda h, i: (0, h))],
        out_specs=[pl.BlockSpec((tq, DV), lambda h, i: (i, h)),
                   pl.BlockSpec((1, tq, 1), lambda h, i: (h, i, 0))],
        out_shape=[jax.ShapeDtypeStruct((T, NH * DV), f32), jax.ShapeDtypeStruct((NH, T, 1), f32)],
        compiler_params=_cparams(2),
    )(q, k, v)


def _flash_bwd(name, q, k, v, do, lse, dl, T, tq):
    nq = T // tq

    def body(k_ref, v_ref, q_ref, do_ref, lse_ref, dl_ref, dk_ref, dv_ref, dq_ref):
        j = pl.program_id(1)

        @pl.when(j == 0)
        def _():
            dq_ref[...] = jnp.zeros_like(dq_ref)

        kv = k_ref[...]
        vv = v_ref[...]
        nt = (((1,), (1,)), ((), ()))
        tn = (((0,), (0,)), ((), ()))

        def step(i, carry, masked):
            dk, dv = carry
            off = pl.multiple_of(i * tq, tq)
            qi = q_ref[pl.ds(off, tq), :]
            doi = do_ref[pl.ds(off, tq), :]
            st = lax.dot_general(kv, qi, nt, preferred_element_type=f32) * SCALE
            if masked:
                kr = lax.broadcasted_iota(jnp.int32, (tq, tq), 0)
                qc = lax.broadcasted_iota(jnp.int32, (tq, tq), 1)
                st = jnp.where(qc >= kr, st, NEG)
            pt = jnp.exp(st - lse_ref[0, i])
            dv = dv + jnp.dot(pt.astype(bf16), doi, preferred_element_type=f32)
            dpt = lax.dot_general(vv, doi, nt, preferred_element_type=f32)
            dst = (pt * (dpt - dl_ref[0, i]) * SCALE).astype(bf16)
            dk = dk + jnp.dot(dst, qi, preferred_element_type=f32)
            dq_ref[pl.ds(off, tq), :] += lax.dot_general(dst, kv, tn, preferred_element_type=f32)
            return dk, dv

        carry = step(j, (jnp.zeros((tq, DH), f32), jnp.zeros((tq, DV), f32)), True)
        dk, dv = lax.fori_loop(j + 1, nq, lambda i, c: step(i, c, False), carry)
        dk_ref[...] = dk
        dv_ref[...] = dv.astype(bf16)

    return pl.pallas_call(
        body, name=name, grid=(NH, nq),
        in_specs=[pl.BlockSpec((tq, DH), lambda h, j: (j, h)),
                  pl.BlockSpec((tq, DV), lambda h, j: (j, h)),
                  pl.BlockSpec((T, DH), lambda h, j: (0, h)),
                  pl.BlockSpec((T, DV), lambda h, j: (0, h)),
                  pl.BlockSpec((1, nq, 1, tq), lambda h, j: (h, 0, 0, 0)),
                  pl.BlockSpec((1, nq, 1, tq), lambda h, j: (h, 0, 0, 0))],
        out_specs=[pl.BlockSpec((tq, DH), lambda h, j: (j, h)),
                   pl.BlockSpec((tq, DV), lambda h, j: (j, h)),
                   pl.BlockSpec((T, DH), lambda h, j: (0, h))],
        out_shape=[jax.ShapeDtypeStruct((T, NH * DH), f32), jax.ShapeDtypeStruct((T, NH * DV), bf16),
                   jax.ShapeDtypeStruct((T, NH * DH), f32)],
        compiler_params=_cparams(2),
    )(k, v, q, do, lse, dl)


def _gate_mix(name, proj, ya, o, T, tb):
    def body(ga_ref, gb_ref, ya_ref, o_ref, y_ref):
        y_ref[...] = (jax.nn.sigmoid(ga_ref[...]) * ya_ref[...] + jax.nn.sigmoid(gb_ref[...]) * o_ref[...]).astype(bf16)

    return _rowwise(name, body, T, tb, [(proj, D, 1), (proj, D, 2), (ya, D, 0), (o, D, 0)], [], [(D, bf16)])[0]


def _gate_mix_bwd(name, dy, proj, ya, o, T, tb):
    def body(dy_ref, ga_ref, gb_ref, ya_ref, o_ref, dga_ref, dgb_ref, dya_ref, do_ref, dl_ref):
        dy = dy_ref[...]
        sa = jax.nn.sigmoid(ga_ref[...])
        sb = jax.nn.sigmoid(gb_ref[...])
        ov = o_ref[...]
        dga_ref[...] = (dy * ya_ref[...] * sa * (1.0 - sa)).astype(bf16)
        dgb_ref[...] = (dy * ov * sb * (1.0 - sb)).astype(bf16)
        dya_ref[...] = dy * sa
        do = dy * sb
        do_ref[...] = do.astype(bf16)
        prod = do * ov
        for hd in range(NH):
            dl_ref[:, hd:hd + 1] = jnp.sum(prod[:, hd * DV:(hd + 1) * DV], axis=-1, keepdims=True)

    return _rowwise(name, body, T, tb, [(dy, D, 0), (proj, D, 1), (proj, D, 2), (ya, D, 0), (o, D, 0)], [],
                    [(D, bf16), (D, bf16), (D, f32), (D, bf16), (NH, f32)])


def _swiglu(name, gu, T, tb):
    def body(g_ref, u_ref, a_ref):
        g = g_ref[...]
        a_ref[...] = (g * jax.nn.sigmoid(g) * u_ref[...]).astype(bf16)

    return _rowwise(name, body, T, tb, [(gu, DFF, 0), (gu, DFF, 1)], [], [(DFF, bf16)])[0]


def _swiglu_bwd(name, dact, gu, T, tb):
    def body(da_ref, g_ref, u_ref, dg_ref, du_ref):
        g = g_ref[...]
        da = da_ref[...]
        sg = jax.nn.sigmoid(g)
        dg_ref[...] = (da * u_ref[...] * sg * (1.0 + g * (1.0 - sg))).astype(bf16)
        du_ref[...] = (da * g * sg).astype(bf16)

    dg, du = _rowwise(name, body, T, tb, [(dact, DFF, 0), (gu, DFF, 0), (gu, DFF, 1)], [], [(DFF, bf16), (DFF, bf16)])
    return jnp.concatenate([dg, du], axis=1)


def _final(name, x, g, m, gfin, tgt, T, tb):
    def body(x_ref, m_ref, t_ref, g_ref, gf_ref, dx_ref, loss_ref, dgf_ref):
        _init_acc(pl.program_id(0), loss_ref, dgf_ref)
        xv = x_ref[...] + g_ref[...] * m_ref[...]
        rstd = lax.rsqrt(jnp.mean(xv * xv, axis=-1, keepdims=True) + EPS)
        nv = xv * rstd
        err = nv * gf_ref[...] - t_ref[...]
        loss_ref[...] += 0.5 * jnp.sum(jnp.mean(err * err, axis=-1, keepdims=True), axis=0, keepdims=True)
        dy = err * (1.0 / D)
        dgf_ref[...] += _colsum(dy * nv)
        dn = dy * gf_ref[...]
        dx_ref[...] = rstd * (dn - nv * jnp.mean(dn * nv, axis=-1, keepdims=True))

    return _rowwise(name, body, T, tb, [(x, D, 0), (m, D, 0), (tgt, D, 0)], [g, gfin], [(D, f32)],
                    [((1, 1), f32), ((1, D), f32)])


def _adamw(name, w, g, m, v):
    R, C = w.shape
    tb = _pick(R, (256, 128, 64, 32, 16, 8))
    c1 = 1.0 - ADAM_B1 ** ADAM_STEP
    c2 = 1.0 - ADAM_B2 ** ADAM_STEP

    def body(w_ref, g_ref, m_ref, v_ref, d_ref, m2_ref, v2_ref):
        gv = g_ref[...]
        m2 = ADAM_B1 * m_ref[...] + (1.0 - ADAM_B1) * gv
        v2 = ADAM_B2 * v_ref[...] + (1.0 - ADAM_B2) * (gv * gv)
        m2_ref[...] = m2
        v2_ref[...] = v2
        d_ref[...] = -ADAM_LR * ((m2 / c1) / (jnp.sqrt(v2 / c2) + ADAM_EPS) + ADAM_WD * w_ref[...])

    return _rowwise(name, body, R, tb, [(w, C, 0), (g, C, 0), (m, C, 0), (v, C, 0)], [], [(C, f32)] * 3)


def _ada_fwd(name, c_all, w_ada, b_sh):
    nl, _, ns = w_ada.shape

    def body(c_ref, w_ref, b_ref, o_ref):
        o_ref[0] = jnp.dot(c_ref[...].astype(bf16), w_ref[0].astype(bf16), preferred_element_type=f32) + b_ref[0]

    return pl.pallas_call(
        body, name=name, grid=(nl,),
        in_specs=[pl.BlockSpec((8, D), lambda l: (0, 0)), pl.BlockSpec((1, D, ns), lambda l: (l, 0, 0)),
                  pl.BlockSpec((1, 1, ns), lambda l: (l, 0, 0))],
        out_specs=pl.BlockSpec((1, 8, ns), lambda l: (l, 0, 0)),
        out_shape=jax.ShapeDtypeStruct((nl, 8, ns), f32), compiler_params=_cparams(1),
    )(c_all, w_ada, b_sh)


def _ada_bwd(name, c_t, dm_sh):
    nl, _, ns = dm_sh.shape

    def body(c_ref, dm_ref, o_ref):
        acc = c_ref[:, 0:1] * dm_ref[0, 0:1, :]
        for d in range(1, 8):
            acc = acc + c_ref[:, d:d + 1] * dm_ref[0, d:d + 1, :]
        o_ref[0] = acc

    return pl.pallas_call(
        body, name=name, grid=(nl,),
        in_specs=[pl.BlockSpec((D, 8), lambda l: (0, 0)), pl.BlockSpec((1, 8, ns), lambda l: (l, 0, 0))],
        out_specs=pl.BlockSpec((1, D, ns), lambda l: (l, 0, 0)),
        out_shape=jax.ShapeDtypeStruct((nl, D, ns), f32), compiler_params=_cparams(1),
    )(c_t, dm_sh)


def _sum_slots(name, a):
    n, R, C = a.shape
    tb = _pick(R, (256, 128, 64, 32, 16, 8))

    def body(a_ref, o_ref):
        acc = a_ref[0].astype(f32)
        for s in range(1, n):
            acc = acc + a_ref[s].astype(f32)
        o_ref[...] = acc

    return pl.pallas_call(
        body, name=name, grid=(R // tb,), in_specs=[pl.BlockSpec((n, tb, C), lambda i: (0, i, 0))],
        out_specs=pl.BlockSpec((tb, C), lambda i: (i, 0)), out_shape=jax.ShapeDtypeStruct((R, C), f32),
        compiler_params=_cparams(1),
    )(a)


def _add_pair(name, a, b, out_dtype):
    n, R, C = a.shape
    tb = _pick(R, (256, 128, 64, 32, 16, 8))

    def body(a_ref, b_ref, o_ref):
        o_ref[...] = (a_ref[...].astype(f32) + b_ref[...].astype(f32)).astype(out_dtype)

    return pl.pallas_call(
        body, name=name, grid=(n, R // tb),
        in_specs=[pl.BlockSpec((1, tb, C), lambda s, i: (s, i, 0))] * 2,
        out_specs=pl.BlockSpec((1, tb, C), lambda s, i: (s, i, 0)),
        out_shape=jax.ShapeDtypeStruct((n, R, C), out_dtype), compiler_params=_cparams(2),
    )(a, b)


def _coords():
    return lax.axis_index("x"), lax.axis_index("y"), lax.axis_index("c")


def _flip(v, m):
    return (1 - v) if m else v


def _peer(x, y, c, mask):
    return (_flip(x, mask[0]), _flip(y, mask[1]), _flip(c, mask[2]))


def _all_gather(name, arr, masks, whole_mesh):
    nslots = 8 if whole_mesh else 4

    def slot(x, y, c):
        return 4 * x + 2 * y + c if whole_mesh else 2 * x + y

    def body(a_ref, o_ref, send_sems, recv_sems, local_sem):
        x, y, c = _coords()
        mine = pltpu.make_async_copy(a_ref, o_ref.at[slot(x, y, c)], local_sem)
        mine.start()
        copies = []
        for k, mask in enumerate(masks):
            cp = pltpu.make_async_remote_copy(
                src_ref=a_ref, dst_ref=o_ref.at[slot(x, y, c)], send_sem=send_sems.at[k], recv_sem=recv_sems.at[k],
                device_id=_peer(x, y, c, mask), device_id_type=MESH)
            cp.start()
            copies.append(cp)
        for cp in copies:
            cp.wait()
        mine.wait()

    return pl.pallas_call(
        body, name=name, in_specs=[ANY], out_specs=ANY,
        out_shape=jax.ShapeDtypeStruct((nslots,) + arr.shape, arr.dtype),
        scratch_shapes=[pltpu.SemaphoreType.DMA((len(masks),)), pltpu.SemaphoreType.DMA((len(masks),)),
                        pltpu.SemaphoreType.DMA],
    )(arr)


def _gather_weights(name, pack):
    R, C = pack.shape
    rh = R // 2

    def body(w_ref, o_ref, sa, ra, sb, rb, local_sem):
        x, y, c = _coords()
        chip = 2 * x + y
        half = pl.ds(c * rh, rh)
        mine = pltpu.make_async_copy(w_ref, o_ref.at[chip], local_sem)
        mine.start()
        first = []
        for k, mask in enumerate(CHIP_MASKS):
            cp = pltpu.make_async_remote_copy(
                src_ref=w_ref.at[half], dst_ref=o_ref.at[chip, half], send_sem=sa.at[k], recv_sem=ra.at[k],
                device_id=_peer(x, y, c, mask), device_id_type=MESH)
            cp.start()
            first.append(cp)
        passed = []
        for k, mask in enumerate(CHIP_MASKS):
            px, py, _ = _peer(x, y, c, mask)
            landed = o_ref.at[2 * px + py, half]
            first[k].wait_recv()
            cp = pltpu.make_async_remote_copy(
                src_ref=landed, dst_ref=landed, send_sem=sb.at[k], recv_sem=rb.at[k],
                device_id=(x, y, 1 - c), device_id_type=MESH)
            cp.start()
            passed.append(cp)
        for k in range(len(CHIP_MASKS)):
            first[k].wait_send()
            passed[k].wait()
        mine.wait()

    n = len(CHIP_MASKS)
    return pl.pallas_call(
        body, name=name, in_specs=[ANY], out_specs=ANY, out_shape=jax.ShapeDtypeStruct((4, R, C), pack.dtype),
        scratch_shapes=[pltpu.SemaphoreType.DMA((n,))] * 4 + [pltpu.SemaphoreType.DMA],
    )(pack)


def _swap_halves(name, arrs):
    def body(*refs):
        na = len(arrs)
        a_refs, keep_refs, got_refs = refs[:na], refs[na:2 * na], refs[2 * na:3 * na]
        send_sems, recv_sems, local_sems = refs[3 * na:]
        x, y, c = _coords()
        cps = []
        for k in range(na):
            lc = pltpu.make_async_copy(a_refs[k].at[:, c], keep_refs[k], local_sems.at[k])
            lc.start()
            cp = pltpu.make_async_remote_copy(
                src_ref=a_refs[k].at[:, 1 - c], dst_ref=got_refs[k], send_sem=send_sems.at[k], recv_sem=recv_sems.at[k],
                device_id=(x, y, 1 - c), device_id_type=MESH)
            cp.start()
            cps += [lc, cp]
        for cp in cps:
            cp.wait()

    na = len(arrs)
    halves = [jax.ShapeDtypeStruct((a.shape[0],) + a.shape[2:], a.dtype) for a in arrs]
    return pl.pallas_call(
        body, name=name, in_specs=[ANY] * na, out_specs=[ANY] * (2 * na), out_shape=halves + halves,
        scratch_shapes=[pltpu.SemaphoreType.DMA((na,))] * 3,
    )(*arrs)


def _scatter_chips(name, arrs):
    def body(*refs):
        na = len(arrs)
        a_refs, o_refs = refs[:na], refs[na:2 * na]
        send_sems, recv_sems, local_sems = refs[2 * na:]
        x, y, c = _coords()
        chip = 2 * x + y
        cps = []
        for k in range(na):
            lc = pltpu.make_async_copy(a_refs[k].at[chip], o_refs[k].at[chip], local_sems.at[k])
            lc.start()
            cps.append(lc)
            for j, mask in enumerate(CHIP_MASKS):
                px, py, pc = _peer(x, y, c, mask)
                cp = pltpu.make_async_remote_copy(
                    src_ref=a_refs[k].at[2 * px + py], dst_ref=o_refs[k].at[chip],
                    send_sem=send_sems.at[k, j], recv_sem=recv_sems.at[k, j],
                    device_id=(px, py, pc), device_id_type=MESH)
                cp.start()
                cps.append(cp)
        for cp in cps:
            cp.wait()

    na = len(arrs)
    n = len(CHIP_MASKS)
    return pl.pallas_call(
        body, name=name, in_specs=[ANY] * na, out_specs=[ANY] * na,
        out_shape=[jax.ShapeDtypeStruct(a.shape, a.dtype) for a in arrs],
        scratch_shapes=[pltpu.SemaphoreType.DMA((na, n)), pltpu.SemaphoreType.DMA((na, n)),
                        pltpu.SemaphoreType.DMA((na,))],
    )(*arrs)


def _join_halves(name, a):
    def body(a_ref, o_ref, send_sem, recv_sem, local_sem):
        x, y, c = _coords()
        lc = pltpu.make_async_copy(a_ref, o_ref.at[c], local_sem)
        lc.start()
        cp = pltpu.make_async_remote_copy(
            src_ref=a_ref, dst_ref=o_ref.at[c], send_sem=send_sem, recv_sem=recv_sem,
            device_id=(x, y, 1 - c), device_id_type=MESH)
        cp.start()
        cp.wait()
        lc.wait()

    return pl.pallas_call(
        body, name=name, in_specs=[ANY], out_specs=ANY, out_shape=jax.ShapeDtypeStruct((2,) + a.shape, a.dtype),
        scratch_shapes=[pltpu.SemaphoreType.DMA] * 3,
    )(a)


BIG = ("w_in", "w_uq", "w_ukv", "w_out", "w_ffn_in", "w_ffn_out")
SMALL = ("conv_w", "conv_b", "lru_wa", "lru_ba", "lru_wx", "lru_bx", "lru_a_param", "q_norm_g", "kv_norm_g",
         "final_norm_g")
SMALL_SHAPES = {"conv_w": (2, CONV, D), "conv_b": (2, D), "lru_wa": (2, NB, BW, BW), "lru_ba": (2, NB, BW),
                "lru_wx": (2, NB, BW, BW), "lru_bx": (2, NB, BW), "lru_a_param": (2, D), "q_norm_g": (2, QR),
                "kv_norm_g": (2, KVR), "final_norm_g": (D,)}
SMALL_ROWS = 576


def _size(shape):
    n = 1
    for s in shape:
        n *= s
    return n


def _pack_rows(parts, rows, dtype):
    flat = jnp.concatenate([p.reshape(-1).astype(dtype) for p in parts])
    pad = rows * 1024 - flat.shape[0]
    if pad:
        flat = jnp.concatenate([flat, jnp.zeros((pad,), dtype)])
    return flat.reshape(rows, 1024)


def _unpack_rows(pack, shapes):
    flat = pack.reshape(-1)
    out, off = [], 0
    for s in shapes:
        n = _size(s)
        out.append(flat[off:off + n].reshape(s))
        off += n
    return out


def _full_weights(sh):
    cat = lambda a, axis: jnp.concatenate([a[s] for s in range(4)], axis=axis)
    w_in = cat(sh["w_in"], 2)
    z64 = jnp.zeros((2, D, 64), w_in.dtype)
    win_p = jnp.concatenate([w_in[..., :D], w_in[..., 1472:2496], w_in[..., 2496:], w_in[..., D:1472], z64], axis=2)
    w_uq = cat(sh["w_uq"], 1)
    wq_p = jnp.concatenate([w_uq, jnp.zeros((2, QR, NH, DH - DN - DR), w_uq.dtype)], axis=3).reshape(2, QR, NH * DH)
    w_ukv = cat(sh["w_ukv"], 3)
    wk_p = jnp.concatenate([w_ukv[..., :DN], jnp.zeros((2, KVR, NH, DH - DN), w_ukv.dtype)], axis=3).reshape(2, KVR, NH * DH)
    wv = w_ukv[..., DN:].reshape(2, KVR, NH * DV)
    return dict(win=win_p, wq=wq_p, wk=wk_p, wv=wv, wout=cat(sh["w_out"], 1), wfi=cat(sh["w_ffn_in"], 2),
                wfo=cat(sh["w_ffn_out"], 1))


def _shard_grads(g):
    dwin = g["win"]
    w_in = jnp.concatenate([dwin[..., :D], dwin[..., QKV0:QKV0 + 448], dwin[..., D:QKV0]], axis=2)
    split = lambda a, axis: jnp.stack(jnp.split(a, 4, axis=axis))
    w_uq = g["wq"].reshape(2, QR, NH, DH)[..., :DN + DR]
    w_ukv = jnp.concatenate([g["wk"].reshape(2, KVR, NH, DH)[..., :DN], g["wv"].reshape(2, KVR, NH, DV)], axis=3)
    return [split(w_in, 2), split(w_uq, 1), split(w_ukv, 3), split(g["wout"], 1), split(g["wfi"], 2),
            split(g["wfo"], 1)]


def kernel(x, c, positions, w_ada, b_ada, w_in, conv_w, conv_b, lru_wa, lru_ba, lru_wx, lru_bx, lru_a_param, q_norm_g, kv_norm_g, w_uq, w_ukv, w_out, w_ffn_in, w_ffn_out, final_norm_g, loss_target, m_w_ada, m_b_ada, m_w_in, m_conv_w, m_conv_b, m_lru_wa, m_lru_ba, m_lru_wx, m_lru_bx, m_lru_a_param, m_q_norm_g, m_kv_norm_g, m_w_uq, m_w_ukv, m_w_out, m_w_ffn_in, m_w_ffn_out, m_final_norm_g, v_w_ada, v_b_ada, v_w_in, v_conv_w, v_conv_b, v_lru_wa, v_lru_ba, v_lru_wx, v_lru_bx, v_lru_a_param, v_q_norm_g, v_kv_norm_g, v_w_uq, v_w_ukv, v_w_out, v_w_ffn_in, v_w_ffn_out, v_final_norm_g):
    T = x.shape[1]
    tb = min(ROW_BLOCK, T)
    tq = min(ATTN_BLOCK, T)
    nq = T // tq
    mx, my, mc = _coords()
    chip = 2 * mx + my
    dev = 4 * mx + 2 * my + mc
    weights = dict(w_ada=w_ada, b_ada=b_ada, w_in=w_in, conv_w=conv_w, conv_b=conv_b, lru_wa=lru_wa, lru_ba=lru_ba,
                   lru_wx=lru_wx, lru_bx=lru_bx, lru_a_param=lru_a_param, q_norm_g=q_norm_g, kv_norm_g=kv_norm_g,
                   w_uq=w_uq, w_ukv=w_ukv, w_out=w_out, w_ffn_in=w_ffn_in, w_ffn_out=w_ffn_out,
                   final_norm_g=final_norm_g)
    mom = dict(w_ada=m_w_ada, b_ada=m_b_ada, w_in=m_w_in, conv_w=m_conv_w, conv_b=m_conv_b, lru_wa=m_lru_wa,
               lru_ba=m_lru_ba, lru_wx=m_lru_wx, lru_bx=m_lru_bx, lru_a_param=m_lru_a_param, q_norm_g=m_q_norm_g,
               kv_norm_g=m_kv_norm_g, w_uq=m_w_uq, w_ukv=m_w_ukv, w_out=m_w_out, w_ffn_in=m_w_ffn_in,
               w_ffn_out=m_w_ffn_out, final_norm_g=m_final_norm_g)
    var = dict(w_ada=v_w_ada, b_ada=v_b_ada, w_in=v_w_in, conv_w=v_conv_w, conv_b=v_conv_b, lru_wa=v_lru_wa,
               lru_ba=v_lru_ba, lru_wx=v_lru_wx, lru_bx=v_lru_bx, lru_a_param=v_lru_a_param, q_norm_g=v_q_norm_g,
               kv_norm_g=v_kv_norm_g, w_uq=v_w_uq, w_ukv=v_w_ukv, w_out=v_w_out, w_ffn_in=v_w_ffn_in,
               w_ffn_out=v_w_ffn_out, final_norm_g=v_final_norm_g)
    order = list(weights)

    ns_ada = w_ada.shape[2]
    c_all = _all_gather("gather_c", c, ALL_MASKS, True).reshape(8, D)
    b_sh = lax.dynamic_slice_in_dim(b_ada, chip * ns_ada, ns_ada, axis=1).reshape(2, 1, ns_ada)
    mod_sh = _ada_fwd("ada_fwd", c_all, w_ada, b_sh)
    mod_all = _all_gather("gather_mod", mod_sh.reshape(16, ns_ada), CHIP_MASKS, False)
    mod_mine = lax.dynamic_index_in_dim(mod_all.reshape(4, 2, 8, ns_ada), dev, axis=2, keepdims=False)
    mod = jnp.transpose(mod_mine, (1, 0, 2)).reshape(2, 6, 1, D)

    big_shapes = [weights[n].shape for n in BIG]
    big_rows = sum(_size(s) for s in big_shapes) // 1024
    wpack = _pack_rows([weights[n] for n in BIG], big_rows, bf16)
    wall = _gather_weights("gather_weights", wpack)
    per_chip = [_unpack_rows(wall[s], big_shapes) for s in range(4)]
    W = _full_weights({n: [per_chip[s][i] for s in range(4)] for i, n in enumerate(BIG)})
    wa_b = lru_wa.astype(bf16)
    wx_b = lru_wx.astype(bf16)

    pos = positions.reshape(T, 1)
    invf = ROPE_THETA ** (-jnp.arange(0, DR, 2, dtype=f32) / DR)
    invf_p = jnp.concatenate([jnp.zeros((DN,), f32), invf, invf, jnp.zeros((DH - DN - DR,), f32)]).reshape(1, DH)
    cosp, sap, sbp = _rope_tables("rope_tables", pos, invf_p, T, tb)

    ns_cw = conv_w.shape[2]
    cw_all = _all_gather("gather_conv_w", conv_w.reshape(2 * CONV, ns_cw), CHIP_MASKS, False)
    cw_full = jnp.transpose(cw_all, (1, 0, 2)).reshape(2, CONV, D)
    vec = lambda a, l: a[l].reshape(1, -1)

    xl = x.reshape(T, D)
    saved = []
    h1 = _norm_mod("norm_mod_0", xl, mod[0, 1], mod[0, 0], T, tb)
    for l in range(2):
        sh_m, sc_m, g_m, sh_f, sc_f, g_f = [mod[l, j] for j in range(6)]
        proj = _matmul(f"proj_{l}", h1, W["win"][l], "nn")
        u, r, ig, ya = _lru_fwd(f"lru_fwd_{l}", proj, pos, cw_full[l], vec(conv_b, l), wa_b[l], vec(lru_ba, l),
                                wx_b[l], vec(lru_bx, l), vec(lru_a_param, l), T, tb)
        qp, kp, vp, cq, ckv = _mla_prep(f"mla_prep_{l}", proj, cosp, sap, sbp, vec(q_norm_g, l), vec(kv_norm_g, l),
                                        W["wq"][l], W["wk"][l], W["wv"][l], T, tb)
        o, lse = _flash_fwd(f"flash_fwd_{l}", qp, kp, vp, T, tq)
        y = _gate_mix(f"gate_mix_{l}", proj, ya, o, T, tb)
        mo = _matmul(f"out_proj_{l}", y, W["wout"][l], "nn")
        x2, h2 = _resid_norm_mod(f"resid_norm_f_{l}", xl, g_m, mo, sc_f, sh_f, T, tb)
        gu = _matmul(f"ffn_in_{l}", h2, W["wfi"][l], "nn")
        act = _swiglu(f"swiglu_{l}", gu, T, tb)
        fo = _matmul(f"ffn_out_{l}", act, W["wfo"][l], "nn")
        saved.append(dict(x=xl, h1=h1, proj=proj, u=u, r=r, ig=ig, ya=ya, qp=qp, kp=kp, vp=vp, cq=cq, ckv=ckv, o=o,
                          lse=lse, y=y, mo=mo, x2=x2, h2=h2, gu=gu, act=act, fo=fo))
        if l == 0:
            xl, h1 = _resid_norm_mod("resid_norm_m_1", x2, g_f, fo, mod[1, 1], mod[1, 0], T, tb)

    dx, loss_part, dgfin = _final("final", saved[1]["x2"], mod[1, 5], saved[1]["fo"],
                                  final_norm_g.reshape(1, D), loss_target.reshape(T, D), T, tb)
    loss = lax.psum(loss_part[0, 0], ("x", "y", "c"))

    gl = {n: [None, None] for n in ("win", "wq", "wk", "wv", "wout", "wfi", "wfo", "conv_w", "conv_b", "lru_wa",
                                    "lru_ba", "lru_wx", "lru_bx", "lru_a_param", "q_norm_g", "kv_norm_g")}
    dmod = [None, None]
    for l in (1, 0):
        s = saved[l]
        sh_m, sc_m, g_m, sh_f, sc_f, g_f = [mod[l, j] for j in range(6)]
        dfo, dg_f = _resid_bwd(f"resid_bwd_f_{l}", dx, g_f, s["fo"], T, tb)
        dact = _matmul(f"d_act_{l}", dfo, W["wfo"][l], "nt")
        gl["wfo"][l] = _matmul(f"dw_ffn_out_{l}", s["act"], dfo, "tn")
        dgu = _swiglu_bwd(f"swiglu_bwd_{l}", dact, s["gu"], T, tb)
        dh2 = _matmul(f"d_h2_{l}", dgu, W["wfi"][l], "nt")
        gl["wfi"][l] = _matmul(f"dw_ffn_in_{l}", s["h2"], dgu, "tn")
        dx2, dsc_f, dsh_f = _norm_mod_bwd(f"norm_bwd_f_{l}", dh2, s["x2"], sc_f, dx, T, tb)
        dmo, dg_m = _resid_bwd(f"resid_bwd_m_{l}", dx2, g_m, s["mo"], T, tb)
        dy = _matmul(f"d_y_{l}", dmo, W["wout"][l], "nt")
        gl["wout"][l] = _matmul(f"dw_out_{l}", s["y"], dmo, "tn")
        dga, dgb, dya, do, dlt = _gate_mix_bwd(f"gate_mix_bwd_{l}", dy, s["proj"], s["ya"], s["o"], T, tb)
        lse_r = s["lse"].reshape(NH, nq, 1, tq)
        dl_r = jnp.transpose(dlt).reshape(NH, nq, 1, tq)
        dk, dv, dq = _flash_bwd(f"flash_bwd_{l}", s["qp"], s["kp"], s["vp"], do, lse_r, dl_r, T, tq)
        dqkv, dqp, dkp, dgq, dgkv = _mla_bwd(f"mla_bwd_{l}", dq, dk, dv, s["proj"], cosp, sap, sbp,
                                              vec(q_norm_g, l), vec(kv_norm_g, l), W["wq"][l], W["wk"][l], W["wv"][l],
                                              T, tb)
        gl["wq"][l] = _matmul(f"dw_uq_{l}", s["cq"], dqp, "tn")
        gl["wk"][l] = _matmul(f"dw_uk_{l}", s["ckv"], dkp, "tn")
        gl["wv"][l] = _matmul(f"dw_uv_{l}", s["ckv"], dv, "tn")
        dxl, dcw, dcb, dwa, dba, dwx, dbx, dsp = _lru_bwd(
            f"lru_bwd_{l}", dya, s["proj"], s["u"], s["r"], s["ig"], s["ya"], pos, cw_full[l], wa_b[l], wx_b[l],
            vec(lru_a_param, l), T, tb)
        dproj = jnp.concatenate([dxl, dga, dgb, dqkv], axis=1)
        dh1 = _matmul(f"d_h1_{l}", dproj, W["win"][l], "nt")
        gl["win"][l] = _matmul(f"dw_in_{l}", s["h1"], dproj, "tn")
        dx, dsc_m, dsh_m = _norm_mod_bwd(f"norm_bwd_m_{l}", dh1, s["x"], sc_m, dx2, T, tb)
        dmod[l] = jnp.concatenate([dsh_m, dsc_m, dg_m, dsh_f, dsc_f, dg_f], axis=1)
        gl["conv_w"][l], gl["conv_b"][l] = dcw, dcb[0]
        gl["lru_wa"][l], gl["lru_ba"][l], gl["lru_wx"][l], gl["lru_bx"][l] = dwa, dba[0], dwx, dbx[0]
        gl["lru_a_param"][l] = dsp[0]
        gl["q_norm_g"][l], gl["kv_norm_g"][l] = dgq[0], dgkv[0]
    grad_x = dx.reshape(1, T, D)
    gfull = {n: jnp.stack(v) for n, v in gl.items()}
    gfull["final_norm_g"] = dgfin[0]

    dmod_all = _all_gather("gather_dmod", jnp.concatenate(dmod, axis=1), ALL_MASKS, True).reshape(8, 2, 6, D)
    dmod_sum = _sum_slots("sum_dmod", dmod_all.reshape(8, 12, D)).reshape(2, 6 * D)
    dm_sh = lax.dynamic_slice_in_dim(jnp.transpose(dmod_all.reshape(8, 2, 6 * D), (1, 0, 2)), chip * ns_ada, ns_ada, axis=2)
    grads = {"w_ada": _ada_bwd("ada_bwd", jnp.transpose(c_all), dm_sh), "b_ada": dmod_sum}

    rh = big_rows // 2
    pieces = _shard_grads(gfull)
    gbig = jnp.stack([_pack_rows([p[s] for p in pieces], big_rows, bf16) for s in range(4)]).reshape(4, 2, rh, 1024)
    rs = SMALL_ROWS // 8
    gsmall = _pack_rows([gfull[n] for n in SMALL], SMALL_ROWS, f32).reshape(4, 2, rs, 1024)
    keep_b, keep_s, got_b, got_s = _swap_halves("rs_swap_halves", [gbig, gsmall])
    sum_b = _add_pair("rs_pair_sum_big", keep_b, got_b, bf16)
    sum_s = _add_pair("rs_pair_sum_small", keep_s, got_s, f32)
    land_b, land_s = _scatter_chips("rs_scatter_chips", [sum_b, sum_s])
    tot_b = _sum_slots("rs_chip_sum_big", land_b)
    tot_s = _sum_slots("rs_chip_sum_small", land_s)
    big_mine = _join_halves("rs_join_halves", tot_b).reshape(big_rows, 1024)
    small_all = _all_gather("gather_small_grads", tot_s, ALL_MASKS, True).reshape(SMALL_ROWS, 1024)
    for n, gpiece in zip(BIG, _unpack_rows(big_mine, big_shapes)):
        grads[n] = gpiece
    for n, gs in zip(SMALL, _unpack_rows(small_all, [SMALL_SHAPES[n] for n in SMALL])):
        grads[n] = gs
    grads["conv_w"] = lax.dynamic_slice_in_dim(grads["conv_w"], chip * ns_cw, ns_cw, axis=2)

    deltas, new_m, new_v = {}, {}, {}
    for n in order:
        w = weights[n]
        view = (1, w.shape[0]) if w.ndim == 1 else (_size(w.shape[:-1]), w.shape[-1]) if w.shape[-1] >= 128 else (w.shape[0], _size(w.shape[1:]))
        d, m2, v2 = _adamw(f"adamw_{n}", w.reshape(view), grads[n].reshape(view), mom[n].reshape(view), var[n].reshape(view))
        deltas[n], new_m[n], new_v[n] = d.reshape(w.shape), m2.reshape(w.shape), v2.reshape(w.shape)
    return (loss, grad_x, *[grads[n] for n in order], *[deltas[n] for n in order], *[new_m[n] for n in order],
            *[new_v[n] for n in order])
```

```python
import functools

import jax
import jax.numpy as jnp
from jax import lax
from jax.experimental import pallas as pl
from jax.experimental.pallas import tpu as pltpu

f32 = jnp.float32
bf16 = jnp.bfloat16

D = 1024
NH = 8
DN = 128
DR = 64
DV = 128
DH = 256
QR = 256
KVR = 128
DFF = 2816
NB = 8
BW = 128
CONV = 4
PW = 3584
QKV0 = 3 * D
DIN = 3520
EPS = 1e-6
LRU_C = 8.0
ROPE_THETA = 10000.0
SCALE = (DN + DR) ** -0.5
LOG2E = 1.4426950408889634
LN2 = 0.6931471805599453
QSCALE = SCALE * LOG2E
NEG = -1e30
HPS = 2

ADAM_LR = 0.001
ADAM_B1 = 0.9
ADAM_B2 = 0.999
ADAM_EPS = 1e-08
ADAM_WD = 0.01
ADAM_STEP = 10

VMEM_LIMIT = 56 * 1024 * 1024
ROW_BLOCK = 256
ATTN_BLOCK = 512
MESH = pl.DeviceIdType.MESH
ANY = pl.BlockSpec(memory_space=pl.ANY)
CHIP_MASKS = ((1, 0, 0), (0, 1, 0), (1, 1, 0))
ALL_MASKS = ((0, 0, 1), (1, 0, 0), (0, 1, 0), (1, 1, 0), (1, 0, 1), (0, 1, 1), (1, 1, 1))


def _cparams(n_axes):
    return pltpu.CompilerParams(dimension_semantics=("arbitrary",) * n_axes, vmem_limit_bytes=VMEM_LIMIT)


def _pick(n, cands):
    for c in cands:
        if n % c == 0:
            return c
    return n


def _rowwise(name, body, T, tb, row_ins, vec_ins, row_outs, acc_outs=(), scratch=(), reverse=False):
    n = T // tb

    def rmap(i):
        return (n - 1 - i) if reverse else i

    in_specs = []
    for (_, w, cb) in row_ins:
        in_specs.append(pl.BlockSpec((tb, w), functools.partial(lambda i, cb: (rmap(i), cb), cb=cb)))
    for v in vec_ins:
        in_specs.append(pl.BlockSpec(v.shape, functools.partial(lambda i, nd: (0,) * nd, nd=v.ndim)))
    out_specs, out_shape = [], []
    for (w, dt) in row_outs:
        out_specs.append(pl.BlockSpec((tb, w), lambda i: (rmap(i), 0)))
        out_shape.append(jax.ShapeDtypeStruct((T, w), dt))
    for (s, dt) in acc_outs:
        out_specs.append(pl.BlockSpec(s, functools.partial(lambda i, nd: (0,) * nd, nd=len(s))))
        out_shape.append(jax.ShapeDtypeStruct(s, dt))

    def kern(*refs):
        body(*refs)

    return pl.pallas_call(
        kern, name=name, grid=(n,), in_specs=in_specs, out_specs=out_specs, out_shape=out_shape,
        scratch_shapes=list(scratch), compiler_params=_cparams(1),
    )(*[a for (a, _, _) in row_ins], *vec_ins)


def _matmul(name, a, b, mode, out_dtype=f32):
    if mode == "nn":
        (M, K), N = a.shape, b.shape[1]
    elif mode == "nt":
        (M, K), N = a.shape, b.shape[0]
    else:
        (K, M), N = a.shape, b.shape[1]
    wide = (1792, 1408, 1024, 512, 256, 128)
    tm = _pick(M, (512, 256, 128)) if mode != "tn" else _pick(M, (1024, 1408, 512, 256, 128))
    tn = _pick(N, wide)
    tk = _pick(K, wide) if mode != "tn" else _pick(K, (512, 256, 128))
    nk = K // tk
    dims = {"nn": (((1,), (0,)), ((), ())), "nt": (((1,), (1,)), ((), ())), "tn": (((0,), (0,)), ((), ()))}[mode]
    if mode == "tn":
        a_spec = pl.BlockSpec((tk, tm), lambda j, i, k: (k, i))
    else:
        a_spec = pl.BlockSpec((tm, tk), lambda j, i, k: (i, k))
    if mode == "nt":
        b_spec = pl.BlockSpec((tn, tk), lambda j, i, k: (j, k))
    else:
        b_spec = pl.BlockSpec((tk, tn), lambda j, i, k: (k, j))
    o_spec = pl.BlockSpec((tm, tn), lambda j, i, k: (i, j))
    use_acc = nk > 1 and out_dtype != f32

    def kern(a_ref, b_ref, o_ref, *scr):
        k = pl.program_id(2)
        d = lax.dot_general(a_ref[...].astype(bf16), b_ref[...].astype(bf16), dims, preferred_element_type=f32)
        if nk == 1:
            o_ref[...] = d.astype(out_dtype)
        else:
            acc = scr[0] if use_acc else o_ref

            @pl.when(k == 0)
            def _():
                acc[...] = d

            @pl.when(k > 0)
            def _():
                acc[...] += d

            if use_acc:
                @pl.when(k == nk - 1)
                def _():
                    o_ref[...] = acc[...].astype(out_dtype)

    return pl.pallas_call(
        kern, name=name, grid=(N // tn, M // tm, nk), in_specs=[a_spec, b_spec], out_specs=o_spec,
        out_shape=jax.ShapeDtypeStruct((M, N), out_dtype),
        scratch_shapes=[pltpu.VMEM((tm, tn), f32)] if use_acc else [],
        compiler_params=_cparams(3),
    )(a, b)


def _colsum(v):
    return jnp.sum(v, axis=0, keepdims=True)


def _init_acc(step, *refs):
    @pl.when(step == 0)
    def _():
        for r in refs:
            r[...] = jnp.zeros_like(r)


def _norm_mod(name, x, sc, sh, T, tb):
    def body(x_ref, sc_ref, sh_ref, h_ref):
        xv = x_ref[...]
        rstd = lax.rsqrt(jnp.mean(xv * xv, axis=-1, keepdims=True) + EPS)
        h_ref[...] = (xv * rstd * (1.0 + sc_ref[...]) + sh_ref[...]).astype(bf16)

    return _rowwise(name, body, T, tb, [(x, D, 0)], [sc, sh], [(D, bf16)])[0]


def _resid_norm_mod(name, x, g, m, sc, sh, T, tb):
    def body(x_ref, m_ref, g_ref, sc_ref, sh_ref, x2_ref, h_ref):
        xv = x_ref[...] + g_ref[...] * m_ref[...]
        x2_ref[...] = xv
        rstd = lax.rsqrt(jnp.mean(xv * xv, axis=-1, keepdims=True) + EPS)
        h_ref[...] = (xv * rstd * (1.0 + sc_ref[...]) + sh_ref[...]).astype(bf16)

    return _rowwise(name, body, T, tb, [(x, D, 0), (m, D, 0)], [g, sc, sh], [(D, f32), (D, bf16)])


def _norm_mod_bwd(name, dh, x, sc, dres, T, tb):
    def body(dh_ref, x_ref, dr_ref, sc_ref, dx_ref, dsc_ref, dsh_ref):
        _init_acc(pl.program_id(0), dsc_ref, dsh_ref)
        xv = x_ref[...]
        dhv = dh_ref[...]
        rstd = lax.rsqrt(jnp.mean(xv * xv, axis=-1, keepdims=True) + EPS)
        nv = xv * rstd
        dn = dhv * (1.0 + sc_ref[...])
        dx_ref[...] = dr_ref[...] + rstd * (dn - nv * jnp.mean(dn * nv, axis=-1, keepdims=True))
        dsc_ref[...] += _colsum(dhv * nv)
        dsh_ref[...] += _colsum(dhv)

    return _rowwise(name, body, T, tb, [(dh, D, 0), (x, D, 0), (dres, D, 0)], [sc], [(D, f32)],
                    [((1, D), f32), ((1, D), f32)])


def _resid_bwd(name, dx, g, m, T, tb):
    def body(dx_ref, m_ref, g_ref, dm_ref, dg_ref):
        _init_acc(pl.program_id(0), dg_ref)
        dxv = dx_ref[...]
        dm_ref[...] = (dxv * g_ref[...]).astype(bf16)
        dg_ref[...] += _colsum(dxv * m_ref[...])

    return _rowwise(name, body, T, tb, [(dx, D, 0), (m, D, 0)], [g], [(D, bf16)], [((1, D), f32)])


def _expm1_neg(y):
    poly = y * (1.0 + y * (0.5 + y * (1.0 / 6 + y * (1.0 / 24 + y * (1.0 / 120 + y * (1.0 / 720 + y * (1.0 / 5040)))))))
    return jnp.where(y > -0.25, poly, jnp.exp(y) - 1.0)


def _lru_gates(r, sp, reset):
    log_a = -LRU_C * r * sp
    a = jnp.where(reset, 0.0, jnp.exp(log_a))
    mult = jnp.where(reset, 1.0, jnp.sqrt(-_expm1_neg(2.0 * log_a)))
    return a, mult


def _block_dot(v, w_ref, dims):
    outs = [lax.dot_general(v[:, n * BW:(n + 1) * BW], w_ref[n], dims, preferred_element_type=f32) for n in range(NB)]
    return jnp.concatenate(outs, axis=1)


def _lru_fwd(name, proj, pos, cw, cb, wa, ba, wx, bx, ap, T, tb):
    nsteps = tb.bit_length() - 1

    def body(x_ref, pos_ref, cw_ref, cb_ref, wa_ref, ba_ref, wx_ref, bx_ref, ap_ref,
             u_ref, r_ref, i_ref, h_ref, prev_x, carry_h):
        _init_acc(pl.program_id(0), prev_x, carry_h)
        x = x_ref[...]
        px = prev_x[...]
        row = lax.broadcasted_iota(jnp.int32, (tb, 1), 0)
        u = x * cw_ref[CONV - 1:CONV, :] + cb_ref[...]
        for j in range(1, CONV):
            z = jnp.where(row >= tb - j, px, x)
            u = u + pltpu.roll(z, j, 0) * cw_ref[CONV - 1 - j:CONV - j, :]
        ub = u.astype(bf16)
        nn = (((1,), (0,)), ((), ()))
        r = jax.nn.sigmoid(_block_dot(ub, wa_ref, nn) + ba_ref[...])
        ig = jax.nn.sigmoid(_block_dot(ub, wx_ref, nn) + bx_ref[...])
        sp = jax.nn.softplus(-ap_ref[...])
        reset = pos_ref[...] == 0
        a, mult = _lru_gates(r, sp, reset)
        b = u * ig * mult
        for k in range(nsteps):
            s = 1 << k
            keep = row >= s
            a_sh = jnp.where(keep, pltpu.roll(a, s, 0), 1.0)
            b_sh = jnp.where(keep, pltpu.roll(b, s, 0), 0.0)
            b = a * b_sh + b
            a = a * a_sh
        h = b + a * carry_h[...]
        u_ref[...] = u
        r_ref[...] = r
        i_ref[...] = ig
        h_ref[...] = h
        carry_h[...] = h[tb - 1:tb, :]
        prev_x[...] = x

    return _rowwise(name, body, T, tb, [(proj, D, 0), (pos, 1, 0)], [cw, cb, wa, ba, wx, bx, ap],
                    [(D, f32)] * 4, scratch=[pltpu.VMEM((tb, D), f32), pltpu.VMEM((1, D), f32)])


def _lru_bwd(name, dh, proj, u, r, ig, h, pos, cw, wa, wx, ap, T, tb):
    n = T // tb
    nsteps = tb.bit_length() - 1
    t8 = tb // 8

    def rmap(i):
        return n - 1 - i

    def body(dh_ref, x_ref, u_ref, r_ref, i_ref, h_ref, hp_ref, pos_ref, cw_ref, wa_ref, wx_ref, ap_ref,
             dx_ref, dcw_ref, dcb_ref, dwa_ref, dba_ref, dwx_ref, dbx_ref, dsp_ref, next_du, carry_g, carry_a):
        step = pl.program_id(0)
        _init_acc(step, dcw_ref, dcb_ref, dwa_ref, dba_ref, dwx_ref, dbx_ref, dsp_ref, next_du, carry_g, carry_a)
        blk = n - 1 - step
        x = x_ref[...]
        u = u_ref[...]
        r = r_ref[...]
        ig = i_ref[...]
        h = h_ref[...]
        row = lax.broadcasted_iota(jnp.int32, (tb, 1), 0)
        sp = jax.nn.softplus(-ap_ref[...])
        reset = pos_ref[...] == 0
        a, mult = _lru_gates(r, sp, reset)
        A = pltpu.roll(jnp.where(row == 0, carry_a[...], a), tb - 1, 0)
        B = dh_ref[...]
        for k in range(nsteps):
            s = 1 << k
            keep = row < tb - s
            a_sh = jnp.where(keep, pltpu.roll(A, tb - s, 0), 1.0)
            b_sh = jnp.where(keep, pltpu.roll(B, tb - s, 0), 0.0)
            B = B + A * b_sh
            A = A * a_sh
        g = B + A * carry_g[...]
        carry_g[...] = g[0:1, :]
        carry_a[...] = a[0:1, :]
        h_last = jnp.where(blk == 0, 0.0, hp_ref[7:8, :])
        h_prev = pltpu.roll(jnp.where(row == tb - 1, h_last, h), 1, 0)
        da = g * h_prev
        gu = g * u
        dlog_a = jnp.where(reset, 0.0, da * a - gu * ig * (a * a) / mult)
        du = g * ig * mult
        di = gu * mult
        dr = dlog_a * (-LRU_C * sp)
        dsp_ref[...] += _colsum(dlog_a * (-LRU_C * r)) * (-jax.nn.sigmoid(-ap_ref[...]))
        dzr = dr * r * (1.0 - r)
        dzi = di * ig * (1.0 - ig)
        dzr_b = dzr.astype(bf16)
        dzi_b = dzi.astype(bf16)
        nt = (((1,), (1,)), ((), ()))
        du = du + _block_dot(dzr_b, wa_ref, nt) + _block_dot(dzi_b, wx_ref, nt)
        ub = u.astype(bf16)
        tn = (((0,), (0,)), ((), ()))
        for nb in range(NB):
            sl = slice(nb * BW, (nb + 1) * BW)
            dwa_ref[nb] += lax.dot_general(ub[:, sl], dzr_b[:, sl], tn, preferred_element_type=f32)
            dwx_ref[nb] += lax.dot_general(ub[:, sl], dzi_b[:, sl], tn, preferred_element_type=f32)
        dba_ref[...] += _colsum(dzr)
        dbx_ref[...] += _colsum(dzi)
        dcb_ref[...] += _colsum(du)
        ndu = next_du[...]
        dx = du * cw_ref[CONV - 1:CONV, :]
        dcw_ref[CONV - 1:CONV, :] += _colsum(x * du)
        for j in range(1, CONV):
            up = pltpu.roll(jnp.where(row < j, ndu, du), tb - j, 0)
            dx = dx + up * cw_ref[CONV - 1 - j:CONV - j, :]
            dcw_ref[CONV - 1 - j:CONV - j, :] += _colsum(x * up)
        dx_ref[...] = dx.astype(bf16)
        next_du[...] = du

    row_specs = [pl.BlockSpec((tb, D), lambda i: (rmap(i), 0)) for _ in range(6)]
    in_specs = row_specs + [
        pl.BlockSpec((8, D), lambda i: (jnp.maximum(rmap(i) * t8 - 1, 0), 0)),
        pl.BlockSpec((tb, 1), lambda i: (rmap(i), 0)),
        pl.BlockSpec((CONV, D), lambda i: (0, 0)),
        pl.BlockSpec((NB, BW, BW), lambda i: (0, 0, 0)),
        pl.BlockSpec((NB, BW, BW), lambda i: (0, 0, 0)),
        pl.BlockSpec((1, D), lambda i: (0, 0)),
    ]
    vec = lambda s: pl.BlockSpec(s, functools.partial(lambda i, nd: (0,) * nd, nd=len(s)))
    acc_shapes = [(CONV, D), (1, D), (NB, BW, BW), (1, D), (NB, BW, BW), (1, D), (1, D)]
    return pl.pallas_call(
        body, name=name, grid=(n,), in_specs=in_specs,
        out_specs=[pl.BlockSpec((tb, D), lambda i: (rmap(i), 0))] + [vec(s) for s in acc_shapes],
        out_shape=[jax.ShapeDtypeStruct((T, D), bf16)] + [jax.ShapeDtypeStruct(s, f32) for s in acc_shapes],
        scratch_shapes=[pltpu.VMEM((tb, D), f32), pltpu.VMEM((1, D), f32), pltpu.VMEM((1, D), f32)],
        compiler_params=_cparams(1),
    )(dh, proj, u, r, ig, h, h, pos, cw, wa, wx, ap)


def _rope_tables(name, pos, invf, T, tb):
    def body(pos_ref, invf_ref, cos_ref, sa_ref, sb_ref):
        ang = pos_ref[...].astype(f32) * invf_ref[...]
        lane = lax.broadcasted_iota(jnp.int32, (tb, DH), 1)
        first = (lane >= DN) & (lane < DN + DR // 2)
        second = (lane >= DN + DR // 2) & (lane < DN + DR)
        c = jnp.cos(ang)
        s = jnp.sin(ang)
        cos_ref[...] = jnp.where(lane < DN, 1.0, jnp.where(first | second, c, 0.0))
        sa_ref[...] = jnp.where(first, -s, 0.0)
        sb_ref[...] = jnp.where(second, s, 0.0)

    return _rowwise(name, body, T, tb, [(pos, 1, 0)], [invf], [(DH, f32)] * 3)


def _rope(z, cos8, sa8, sb8):
    w = z.shape[1]
    return z * cos8 + pltpu.roll(z, w - DR // 2, 1) * sa8 + pltpu.roll(z, DR // 2, 1) * sb8


def _rope_t(d, cos8, sa8, sb8):
    w = d.shape[1]
    return d * cos8 + pltpu.roll(d * sa8, DR // 2, 1) + pltpu.roll(d * sb8, w - DR // 2, 1)


def _mla_prep(name, proj, cosp, sap, sbp, gq, gkv, wq, wk, wv, T, tb):
    def body(qkv_ref, cos_ref, sa_ref, sb_ref, gq_ref, gkv_ref, wq_ref, wk_ref, wv_ref,
             q_ref, k_ref, v_ref, cq_ref, ckv_ref):
        qkv = qkv_ref[...]
        qd = qkv[:, :QR]
        kvd = qkv[:, QR:QR + KVR]
        slab = qkv[:, QR + KVR:]
        cq = (qd * lax.rsqrt(jnp.mean(qd * qd, axis=-1, keepdims=True) + EPS) * gq_ref[...]).astype(bf16)
        ckv = (kvd * lax.rsqrt(jnp.mean(kvd * kvd, axis=-1, keepdims=True) + EPS) * gkv_ref[...]).astype(bf16)
        cos8 = jnp.tile(cos_ref[...], (1, NH))
        sa8 = jnp.tile(sa_ref[...], (1, NH))
        sb8 = jnp.tile(sb_ref[...], (1, NH))
        qpre = jnp.dot(cq, wq_ref[...], preferred_element_type=f32)
        shared = jnp.concatenate([jnp.zeros((tb, DN), f32), slab], axis=1)
        kpre = jnp.dot(ckv, wk_ref[...], preferred_element_type=f32) + jnp.tile(shared, (1, NH))
        q_ref[...] = (_rope(qpre, cos8, sa8, sb8) * QSCALE).astype(bf16)
        k_ref[...] = _rope(kpre, cos8, sa8, sb8).astype(bf16)
        v_ref[...] = jnp.dot(ckv, wv_ref[...], preferred_element_type=f32).astype(bf16)
        cq_ref[...] = cq
        ckv_ref[...] = ckv

    return _rowwise(name, body, T, tb,
                    [(proj, 512, QKV0 // 512), (cosp, DH, 0), (sap, DH, 0), (sbp, DH, 0)],
                    [gq, gkv, wq, wk, wv],
                    [(NH * DH, bf16), (NH * DH, bf16), (NH * DV, bf16), (QR, bf16), (KVR, bf16)])


def _mla_bwd(name, dq, dk, dv, proj, cosp, sap, sbp, gq, gkv, wq, wk, wv, T, tb):
    def body(dq_ref, dk_ref, dv_ref, qkv_ref, cos_ref, sa_ref, sb_ref, gq_ref, gkv_ref, wq_ref, wk_ref, wv_ref,
             dqkv_ref, dqp_ref, dkp_ref, dgq_ref, dgkv_ref):
        _init_acc(pl.program_id(0), dgq_ref, dgkv_ref)
        cos8 = jnp.tile(cos_ref[...], (1, NH))
        sa8 = jnp.tile(sa_ref[...], (1, NH))
        sb8 = jnp.tile(sb_ref[...], (1, NH))
        dqp = _rope_t(dq_ref[...] * SCALE, cos8, sa8, sb8)
        dkp = _rope_t(dk_ref[...], cos8, sa8, sb8)
        dqp_b = dqp.astype(bf16)
        dkp_b = dkp.astype(bf16)
        dqp_ref[...] = dqp_b
        dkp_ref[...] = dkp_b
        nt = (((1,), (1,)), ((), ()))
        dcq = lax.dot_general(dqp_b, wq_ref[...], nt, preferred_element_type=f32)
        dckv = (lax.dot_general(dkp_b, wk_ref[...], nt, preferred_element_type=f32)
                + lax.dot_general(dv_ref[...], wv_ref[...], nt, preferred_element_type=f32))
        dshared = dkp[:, 0:DH]
        for hd in range(1, NH):
            dshared = dshared + dkp[:, hd * DH:(hd + 1) * DH]
        qkv = qkv_ref[...]
        qd = qkv[:, :QR]
        kvd = qkv[:, QR:QR + KVR]
        nq = qd * lax.rsqrt(jnp.mean(qd * qd, axis=-1, keepdims=True) + EPS)
        nkv = kvd * lax.rsqrt(jnp.mean(kvd * kvd, axis=-1, keepdims=True) + EPS)
        dgq_ref[...] += _colsum(dcq * nq)
        dgkv_ref[...] += _colsum(dckv * nkv)
        dnq = dcq * gq_ref[...]
        dnkv = dckv * gkv_ref[...]
        rq = lax.rsqrt(jnp.mean(qd * qd, axis=-1, keepdims=True) + EPS)
        rkv = lax.rsqrt(jnp.mean(kvd * kvd, axis=-1, keepdims=True) + EPS)
        dqd = rq * (dnq - nq * jnp.mean(dnq * nq, axis=-1, keepdims=True))
        dkvd = rkv * (dnkv - nkv * jnp.mean(dnkv * nkv, axis=-1, keepdims=True))
        dqkv_ref[...] = jnp.concatenate([dqd, dkvd, dshared[:, DN:]], axis=1).astype(bf16)

    return _rowwise(name, body, T, tb,
                    [(dq, NH * DH, 0), (dk, NH * DH, 0), (dv, NH * DV, 0), (proj, 512, QKV0 // 512),
                     (cosp, DH, 0), (sap, DH, 0), (sbp, DH, 0)],
                    [gq, gkv, wq, wk, wv],
                    [(512, bf16), (NH * DH, bf16), (NH * DH, bf16)],
                    [((1, QR), f32), ((1, KVR), f32)])


def _flash_fwd(name, q, k, v, T, tq):
    nq = T // tq
    unroll = 2

    def body(q_ref, k_ref, v_ref, o_ref, lse_ref):
        i = pl.program_id(1)
        nt = (((1,), (1,)), ((), ()))
        qs = [q_ref[:, hh * DH:(hh + 1) * DH] for hh in range(HPS)]

        def one(hh, j, carry, masked):
            m, l, acc = carry
            off = pl.multiple_of(j * tq, tq)
            kj = k_ref[pl.ds(off, tq), hh * DH:(hh + 1) * DH]
            vj = v_ref[pl.ds(off, tq), hh * DV:(hh + 1) * DV]
            s = lax.dot_general(qs[hh], kj, nt, preferred_element_type=f32)
            if masked:
                rr = lax.broadcasted_iota(jnp.int32, (tq, tq), 0)
                cc = lax.broadcasted_iota(jnp.int32, (tq, tq), 1)
                s = jnp.where(cc <= rr, s, NEG)
            m_new = jnp.maximum(m, jnp.max(s, axis=-1, keepdims=True))
            alpha = jnp.exp2(m - m_new)
            p = jnp.exp2(s - m_new)
            l = alpha * l + jnp.sum(p, axis=-1, keepdims=True)
            acc = alpha * acc + jnp.dot(p.astype(bf16), vj, preferred_element_type=f32)
            return m_new, l, acc

        def step(j, carries, masked):
            return tuple(one(hh, j, carries[hh], masked) for hh in range(HPS))

        def pair(jj, carries):
            for u in range(unroll):
                carries = step(jj * unroll + u, carries, False)
            return carries

        init1 = (jnp.full((tq, 1), NEG, f32), jnp.zeros((tq, 1), f32), jnp.zeros((tq, DV), f32))
        carry = lax.fori_loop(0, i // unroll, pair, tuple(init1 for _ in range(HPS)))
        carry = lax.fori_loop((i // unroll) * unroll, i, lambda j, c: step(j, c, False), carry)
        carry = step(i, carry, True)
        for hh in range(HPS):
            m, l, acc = carry[hh]
            o_ref[:, hh * DV:(hh + 1) * DV] = acc / l
            lse_ref[hh] = m + jnp.log2(l)

    return pl.pallas_call(
        body, name=name, grid=(NH // HPS, nq),
        in_specs=[pl.BlockSpec((tq, HPS * DH), lambda h, i: (i, h)),
                  pl.BlockSpec((T, HPS * DH), lambda h, i: (0, h), pipeline_mode=pl.Buffered(1)),
                  pl.BlockSpec((T, HPS * DV), lambda h, i: (0, h), pipeline_mode=pl.Buffered(1))],
        out_specs=[pl.BlockSpec((tq, HPS * DV), lambda h, i: (i, h)),
                   pl.BlockSpec((HPS, tq, 1), lambda h, i: (h, i, 0))],
        out_shape=[jax.ShapeDtypeStruct((T, NH * DV), f32), jax.ShapeDtypeStruct((NH, T, 1), f32)],
        compiler_params=_cparams(2),
    )(q, k, v)


def _flash_bwd(name, q, k, v, do, lse, dl, T, tq):
    nq = T // tq

    def body(k_ref, v_ref, q_ref, do_ref, lse_ref, dl_ref, dk_ref, dv_ref, dq_ref):
        j = pl.program_id(1)

        @pl.when(j == 0)
        def _():
            dq_ref[...] = jnp.zeros_like(dq_ref)

        nt = (((1,), (1,)), ((), ()))
        tn = (((0,), (0,)), ((), ()))
        ks = [k_ref[:, hh * DH:(hh + 1) * DH] for hh in range(HPS)]
        vs = [v_ref[:, hh * DV:(hh + 1) * DV] for hh in range(HPS)]

        def one(hh, i, carry, masked):
            dk, dv = carry
            off = pl.multiple_of(i * tq, tq)
            qi = q_ref[pl.ds(off, tq), hh * DH:(hh + 1) * DH]
            doi = do_ref[pl.ds(off, tq), hh * DV:(hh + 1) * DV]
            st = lax.dot_general(ks[hh], qi, nt, preferred_element_type=f32)
            if masked:
                kr = lax.broadcasted_iota(jnp.int32, (tq, tq), 0)
                qc = lax.broadcasted_iota(jnp.int32, (tq, tq), 1)
                st = jnp.where(qc >= kr, st, NEG)
            pt = jnp.exp2(st - lse_ref[hh, i])
            dv = dv + jnp.dot(pt.astype(bf16), doi, preferred_element_type=f32)
            dpt = lax.dot_general(vs[hh], doi, nt, preferred_element_type=f32)
            dst = (pt * (dpt - dl_ref[hh, i])).astype(bf16)
            dk = dk + jnp.dot(dst, qi, preferred_element_type=f32)
            dq_ref[pl.ds(off, tq), hh * DH:(hh + 1) * DH] += lax.dot_general(dst, ks[hh], tn, preferred_element_type=f32)
            return dk, dv

        def step(i, carries, masked):
            return tuple(one(hh, i, carries[hh], masked) for hh in range(HPS))

        zero = tuple((jnp.zeros((tq, DH), f32), jnp.zeros((tq, DV), f32)) for _ in range(HPS))
        carry = step(j, zero, True)
        carry = lax.fori_loop(j + 1, nq, lambda i, c: step(i, c, False), carry)
        for hh in range(HPS):
            dk, dv = carry[hh]
            dk_ref[:, hh * DH:(hh + 1) * DH] = dk * LN2
            dv_ref[:, hh * DV:(hh + 1) * DV] = dv.astype(bf16)

    return pl.pallas_call(
        body, name=name, grid=(NH // HPS, nq),
        in_specs=[pl.BlockSpec((tq, HPS * DH), lambda h, j: (j, h)),
                  pl.BlockSpec((tq, HPS * DV), lambda h, j: (j, h)),
                  pl.BlockSpec((T, HPS * DH), lambda h, j: (0, h), pipeline_mode=pl.Buffered(1)),
                  pl.BlockSpec((T, HPS * DV), lambda h, j: (0, h), pipeline_mode=pl.Buffered(1)),
                  pl.BlockSpec((HPS, nq, 1, tq), lambda h, j: (h, 0, 0, 0)),
                  pl.BlockSpec((HPS, nq, 1, tq), lambda h, j: (h, 0, 0, 0))],
        out_specs=[pl.BlockSpec((tq, HPS * DH), lambda h, j: (j, h)),
                   pl.BlockSpec((tq, HPS * DV), lambda h, j: (j, h)),
                   pl.BlockSpec((T, HPS * DH), lambda h, j: (0, h), pipeline_mode=pl.Buffered(1))],
        out_shape=[jax.ShapeDtypeStruct((T, NH * DH), f32), jax.ShapeDtypeStruct((T, NH * DV), bf16),
                   jax.ShapeDtypeStruct((T, NH * DH), f32)],
        compiler_params=_cparams(2),
    )(k, v, q, do, lse, dl)


def _gate_mix(name, proj, ya, o, T, tb):
    def body(ga_ref, gb_ref, ya_ref, o_ref, y_ref):
        y_ref[...] = (jax.nn.sigmoid(ga_ref[...]) * ya_ref[...] + jax.nn.sigmoid(gb_ref[...]) * o_ref[...]).astype(bf16)

    return _rowwise(name, body, T, tb, [(proj, D, 1), (proj, D, 2), (ya, D, 0), (o, D, 0)], [], [(D, bf16)])[0]


def _gate_mix_bwd(name, dy, proj, ya, o, T, tb):
    def body(dy_ref, ga_ref, gb_ref, ya_ref, o_ref, dga_ref, dgb_ref, dya_ref, do_ref, dl_ref):
        dy = dy_ref[...]
        sa = jax.nn.sigmoid(ga_ref[...])
        sb = jax.nn.sigmoid(gb_ref[...])
        ov = o_ref[...]
        dga_ref[...] = (dy * ya_ref[...] * sa * (1.0 - sa)).astype(bf16)
        dgb_ref[...] = (dy * ov * sb * (1.0 - sb)).astype(bf16)
        dya_ref[...] = dy * sa
        do = dy * sb
        do_ref[...] = do.astype(bf16)
        prod = do * ov
        for hd in range(NH):
            dl_ref[:, hd:hd + 1] = jnp.sum(prod[:, hd * DV:(hd + 1) * DV], axis=-1, keepdims=True)

    return _rowwise(name, body, T, tb, [(dy, D, 0), (proj, D, 1), (proj, D, 2), (ya, D, 0), (o, D, 0)], [],
                    [(D, bf16), (D, bf16), (D, f32), (D, bf16), (NH, f32)])


def _swiglu(name, gu, T, tb):
    def body(g_ref, u_ref, a_ref):
        g = g_ref[...]
        a_ref[...] = (g * jax.nn.sigmoid(g) * u_ref[...]).astype(bf16)

    return _rowwise(name, body, T, tb, [(gu, DFF, 0), (gu, DFF, 1)], [], [(DFF, bf16)])[0]


def _swiglu_bwd(name, dact, gu, T, tb):
    def body(da_ref, g_ref, u_ref, dgu_ref):
        g = g_ref[...]
        da = da_ref[...]
        sg = jax.nn.sigmoid(g)
        dgu_ref[:, :DFF] = (da * u_ref[...] * sg * (1.0 + g * (1.0 - sg))).astype(bf16)
        dgu_ref[:, DFF:] = (da * g * sg).astype(bf16)

    return _rowwise(name, body, T, tb, [(dact, DFF, 0), (gu, DFF, 0), (gu, DFF, 1)], [], [(2 * DFF, bf16)])[0]


def _final(name, x, g, m, gfin, tgt, T, tb):
    def body(x_ref, m_ref, t_ref, g_ref, gf_ref, dx_ref, loss_ref, dgf_ref):
        _init_acc(pl.program_id(0), loss_ref, dgf_ref)
        xv = x_ref[...] + g_ref[...] * m_ref[...]
        rstd = lax.rsqrt(jnp.mean(xv * xv, axis=-1, keepdims=True) + EPS)
        nv = xv * rstd
        err = nv * gf_ref[...] - t_ref[...]
        loss_ref[...] += 0.5 * jnp.sum(jnp.mean(err * err, axis=-1, keepdims=True), axis=0, keepdims=True)
        dy = err * (1.0 / D)
        dgf_ref[...] += _colsum(dy * nv)
        dn = dy * gf_ref[...]
        dx_ref[...] = rstd * (dn - nv * jnp.mean(dn * nv, axis=-1, keepdims=True))

    return _rowwise(name, body, T, tb, [(x, D, 0), (m, D, 0), (tgt, D, 0)], [g, gfin], [(D, f32)],
                    [((1, 1), f32), ((1, D), f32)])


def _adamw(name, w, g, m, v):
    R, C = w.shape
    tb = _pick(R, (256, 128, 64, 32, 16, 8))
    c1 = 1.0 - ADAM_B1 ** ADAM_STEP
    c2 = 1.0 - ADAM_B2 ** ADAM_STEP

    def body(w_ref, g_ref, m_ref, v_ref, d_ref, m2_ref, v2_ref):
        gv = g_ref[...]
        m2 = ADAM_B1 * m_ref[...] + (1.0 - ADAM_B1) * gv
        v2 = ADAM_B2 * v_ref[...] + (1.0 - ADAM_B2) * (gv * gv)
        m2_ref[...] = m2
        v2_ref[...] = v2
        d_ref[...] = -ADAM_LR * ((m2 / c1) / (jnp.sqrt(v2 / c2) + ADAM_EPS) + ADAM_WD * w_ref[...])

    return _rowwise(name, body, R, tb, [(w, C, 0), (g, C, 0), (m, C, 0), (v, C, 0)], [], [(C, f32)] * 3)


def _ada_fwd(name, c_all, w_ada, b_sh):
    nl, _, ns = w_ada.shape

    def body(c_ref, w_ref, b_ref, o_ref):
        o_ref[0] = jnp.dot(c_ref[...].astype(bf16), w_ref[0].astype(bf16), preferred_element_type=f32) + b_ref[0]

    return pl.pallas_call(
        body, name=name, grid=(nl,),
        in_specs=[pl.BlockSpec((8, D), lambda l: (0, 0)), pl.BlockSpec((1, D, ns), lambda l: (l, 0, 0)),
                  pl.BlockSpec((1, 1, ns), lambda l: (l, 0, 0))],
        out_specs=pl.BlockSpec((1, 8, ns), lambda l: (l, 0, 0)),
        out_shape=jax.ShapeDtypeStruct((nl, 8, ns), f32), compiler_params=_cparams(1),
    )(c_all, w_ada, b_sh)


def _ada_bwd(name, c_t, dm_sh):
    nl, _, ns = dm_sh.shape

    def body(c_ref, dm_ref, o_ref):
        acc = c_ref[:, 0:1] * dm_ref[0, 0:1, :]
        for d in range(1, 8):
            acc = acc + c_ref[:, d:d + 1] * dm_ref[0, d:d + 1, :]
        o_ref[0] = acc

    return pl.pallas_call(
        body, name=name, grid=(nl,),
        in_specs=[pl.BlockSpec((D, 8), lambda l: (0, 0)), pl.BlockSpec((1, 8, ns), lambda l: (l, 0, 0))],
        out_specs=pl.BlockSpec((1, D, ns), lambda l: (l, 0, 0)),
        out_shape=jax.ShapeDtypeStruct((nl, D, ns), f32), compiler_params=_cparams(1),
    )(c_t, dm_sh)


def _sum_slots(name, a):
    n, R, C = a.shape
    tb = _pick(R, (256, 128, 64, 32, 16, 8))

    def body(a_ref, o_ref):
        acc = a_ref[0].astype(f32)
        for s in range(1, n):
            acc = acc + a_ref[s].astype(f32)
        o_ref[...] = acc

    return pl.pallas_call(
        body, name=name, grid=(R // tb,), in_specs=[pl.BlockSpec((n, tb, C), lambda i: (0, i, 0))],
        out_specs=pl.BlockSpec((tb, C), lambda i: (i, 0)), out_shape=jax.ShapeDtypeStruct((R, C), f32),
        compiler_params=_cparams(1),
    )(a)


def _chunks(rows, dtype, want):
    unit = 16 if dtype == bf16 else 8
    for n in range(want, 0, -1):
        if rows % n == 0 and (rows // n) % unit == 0:
            return n, rows // n
    return 1, rows


def _pair_sum(name, a, got, idx, out_dtype):
    n, _, rh, C = a.shape
    _, tb = _chunks(rh, bf16, 3)

    def body(idx_ref, a_ref, g_ref, o_ref):
        o_ref[...] = (a_ref[0].astype(f32) + g_ref[...].astype(f32)).astype(out_dtype)

    grid_spec = pltpu.PrefetchScalarGridSpec(
        num_scalar_prefetch=1, grid=(n, rh // tb),
        in_specs=[pl.BlockSpec((1, 1, tb, C), lambda s, i, idx_ref: (s, idx_ref[1], i, 0)),
                  pl.BlockSpec((1, tb, C), lambda s, i, idx_ref: (s, i, 0))],
        out_specs=pl.BlockSpec((1, tb, C), lambda s, i, idx_ref: (s, i, 0)))
    return pl.pallas_call(
        body, name=name, grid_spec=grid_spec, out_shape=jax.ShapeDtypeStruct((n, rh, C), out_dtype),
        compiler_params=_cparams(2),
    )(idx, a, got)


def _chip_sum(name, s, land, idx, join):
    _, rh, C = s.shape
    _, tb = _chunks(rh, bf16, 3)

    def body(idx_ref, s_ref, l_ref, o_ref):
        acc = s_ref[0].astype(f32)
        for j in range(l_ref.shape[0]):
            acc = acc + l_ref[j].astype(f32)
        o_ref[...] = acc.reshape(o_ref.shape)

    if join:
        out_spec = pl.BlockSpec((1, tb, C), lambda i, idx_ref: (idx_ref[1], i, 0))
        out_shape = jax.ShapeDtypeStruct((2, rh, C), f32)
    else:
        out_spec = pl.BlockSpec((tb, C), lambda i, idx_ref: (i, 0))
        out_shape = jax.ShapeDtypeStruct((rh, C), f32)
    grid_spec = pltpu.PrefetchScalarGridSpec(
        num_scalar_prefetch=1, grid=(rh // tb,),
        in_specs=[pl.BlockSpec((1, tb, C), lambda i, idx_ref: (idx_ref[0], i, 0)),
                  pl.BlockSpec((land.shape[0], tb, C), lambda i, idx_ref: (0, i, 0))],
        out_specs=out_spec)
    return pl.pallas_call(body, name=name, grid_spec=grid_spec, out_shape=out_shape, compiler_params=_cparams(1))(idx, s, land)


def _coords():
    return lax.axis_index("x"), lax.axis_index("y"), lax.axis_index("c")


def _flip(v, m):
    return (1 - v) if m else v


def _peer(x, y, c, mask):
    return (_flip(x, mask[0]), _flip(y, mask[1]), _flip(c, mask[2]))


def _all_gather(name, arr, masks, whole_mesh):
    nslots = 8 if whole_mesh else 4

    def slot(x, y, c):
        return 4 * x + 2 * y + c if whole_mesh else 2 * x + y

    def body(a_ref, o_ref, send_sems, recv_sems, local_sem):
        x, y, c = _coords()
        mine = pltpu.make_async_copy(a_ref, o_ref.at[slot(x, y, c)], local_sem)
        mine.start()
        copies = []
        for k, mask in enumerate(masks):
            cp = pltpu.make_async_remote_copy(
                src_ref=a_ref, dst_ref=o_ref.at[slot(x, y, c)], send_sem=send_sems.at[k], recv_sem=recv_sems.at[k],
                device_id=_peer(x, y, c, mask), device_id_type=MESH)
            cp.start()
            copies.append(cp)
        for cp in copies:
            cp.wait()
        mine.wait()

    return pl.pallas_call(
        body, name=name, in_specs=[ANY], out_specs=ANY,
        out_shape=jax.ShapeDtypeStruct((nslots,) + arr.shape, arr.dtype),
        scratch_shapes=[pltpu.SemaphoreType.DMA((len(masks),)), pltpu.SemaphoreType.DMA((len(masks),)),
                        pltpu.SemaphoreType.DMA],
    )(arr)


def _gather_weights(name, pack):
    R, C = pack.shape
    rh = R // 2
    nch, rows = _chunks(rh, pack.dtype, 3)
    nloc, lrows = _chunks(R, pack.dtype, 6)
    n = len(CHIP_MASKS)

    def body(w_ref, o_ref, sa, ra, sb, rb, local_sems):
        x, y, c = _coords()
        chip = 2 * x + y
        mine = []
        for k in range(nloc):
            part = pl.ds(k * lrows, lrows)
            cp = pltpu.make_async_copy(w_ref.at[part], o_ref.at[chip, part], local_sems.at[k])
            cp.start()
            mine.append(cp)
        first = []
        for k, mask in enumerate(CHIP_MASKS):
            for ch in range(nch):
                part = pl.ds(c * rh + ch * rows, rows)
                cp = pltpu.make_async_remote_copy(
                    src_ref=w_ref.at[part], dst_ref=o_ref.at[chip, part], send_sem=sa.at[k, ch], recv_sem=ra.at[k, ch],
                    device_id=_peer(x, y, c, mask), device_id_type=MESH)
                cp.start()
                first.append(cp)
        passed = []
        for k, mask in enumerate(CHIP_MASKS):
            px, py, _ = _peer(x, y, c, mask)
            for ch in range(nch):
                landed = o_ref.at[2 * px + py, pl.ds(c * rh + ch * rows, rows)]
                first[k * nch + ch].wait_recv()
                cp = pltpu.make_async_remote_copy(
                    src_ref=landed, dst_ref=landed, send_sem=sb.at[k, ch], recv_sem=rb.at[k, ch],
                    device_id=(x, y, 1 - c), device_id_type=MESH)
                cp.start()
                passed.append(cp)
        for k in range(n * nch):
            first[k].wait_send()
            passed[k].wait()
        for cp in mine:
            cp.wait()

    return pl.pallas_call(
        body, name=name, in_specs=[ANY], out_specs=ANY, out_shape=jax.ShapeDtypeStruct((4, R, C), pack.dtype),
        scratch_shapes=[pltpu.SemaphoreType.DMA((n, nch))] * 4 + [pltpu.SemaphoreType.DMA((nloc,))],
    )(pack)


def _swap_halves(name, arrs):
    na = len(arrs)
    plan = []
    for a in arrs:
        nch, rows = _chunks(a.shape[2], a.dtype, 3)
        plan.append([(s, ch * rows, rows) for s in range(a.shape[0]) for ch in range(nch)])
    ncopies = sum(len(p) for p in plan)

    def body(*refs):
        a_refs, got_refs = refs[:na], refs[na:2 * na]
        send_sems, recv_sems = refs[2 * na:]
        x, y, c = _coords()
        cps = []
        for ai in range(na):
            for (s, r0, rows) in plan[ai]:
                k = len(cps)
                cp = pltpu.make_async_remote_copy(
                    src_ref=a_refs[ai].at[s, 1 - c, pl.ds(r0, rows)], dst_ref=got_refs[ai].at[s, pl.ds(r0, rows)],
                    send_sem=send_sems.at[k], recv_sem=recv_sems.at[k], device_id=(x, y, 1 - c), device_id_type=MESH)
                cp.start()
                cps.append(cp)
        for cp in cps:
            cp.wait()

    halves = [jax.ShapeDtypeStruct((a.shape[0],) + a.shape[2:], a.dtype) for a in arrs]
    return pl.pallas_call(
        body, name=name, in_specs=[ANY] * na, out_specs=[ANY] * na, out_shape=halves,
        scratch_shapes=[pltpu.SemaphoreType.DMA((ncopies,))] * 2,
    )(*arrs)


def _scatter_chips(name, arrs):
    na = len(arrs)
    n = len(CHIP_MASKS)

    def body(*refs):
        a_refs, o_refs = refs[:na], refs[na:2 * na]
        send_sems, recv_sems = refs[2 * na:]
        x, y, c = _coords()
        cps = []
        for k in range(na):
            for j, mask in enumerate(CHIP_MASKS):
                px, py, pc = _peer(x, y, c, mask)
                cp = pltpu.make_async_remote_copy(
                    src_ref=a_refs[k].at[2 * px + py], dst_ref=o_refs[k].at[j],
                    send_sem=send_sems.at[k, j], recv_sem=recv_sems.at[k, j],
                    device_id=(px, py, pc), device_id_type=MESH)
                cp.start()
                cps.append(cp)
        for cp in cps:
            cp.wait()

    return pl.pallas_call(
        body, name=name, in_specs=[ANY] * na, out_specs=[ANY] * na,
        out_shape=[jax.ShapeDtypeStruct((n,) + a.shape[1:], a.dtype) for a in arrs],
        scratch_shapes=[pltpu.SemaphoreType.DMA((na, n)), pltpu.SemaphoreType.DMA((na, n))],
    )(*arrs)


def _join_halves(name, buf):
    _, rh, C = buf.shape
    nch, rows = _chunks(rh, buf.dtype, 6)

    def body(b_ref, o_ref, send_sems, recv_sems):
        x, y, c = _coords()
        cps = []
        for k in range(nch):
            part = o_ref.at[c, pl.ds(k * rows, rows)]
            cp = pltpu.make_async_remote_copy(
                src_ref=part, dst_ref=part, send_sem=send_sems.at[k], recv_sem=recv_sems.at[k],
                device_id=(x, y, 1 - c), device_id_type=MESH)
            cp.start()
            cps.append(cp)
        for cp in cps:
            cp.wait()

    return pl.pallas_call(
        body, name=name, in_specs=[ANY], out_specs=ANY, out_shape=jax.ShapeDtypeStruct(buf.shape, buf.dtype),
        input_output_aliases={0: 0}, scratch_shapes=[pltpu.SemaphoreType.DMA((nch,))] * 2,
    )(buf)


BIG = ("w_in", "w_uq", "w_ukv", "w_out", "w_ffn_in", "w_ffn_out")
SMALL = ("conv_w", "conv_b", "lru_wa", "lru_ba", "lru_wx", "lru_bx", "lru_a_param", "q_norm_g", "kv_norm_g",
         "final_norm_g")
SMALL_SHAPES = {"conv_w": (2, CONV, D), "conv_b": (2, D), "lru_wa": (2, NB, BW, BW), "lru_ba": (2, NB, BW),
                "lru_wx": (2, NB, BW, BW), "lru_bx": (2, NB, BW), "lru_a_param": (2, D), "q_norm_g": (2, QR),
                "kv_norm_g": (2, KVR), "final_norm_g": (D,)}
SMALL_ROWS = 576


def _size(shape):
    n = 1
    for s in shape:
        n *= s
    return n


def _pack_rows(parts, rows, dtype):
    flat = jnp.concatenate([p.reshape(-1).astype(dtype) for p in parts])
    pad = rows * 1024 - flat.shape[0]
    if pad:
        flat = jnp.concatenate([flat, jnp.zeros((pad,), dtype)])
    return flat.reshape(rows, 1024)


def _unpack_rows(pack, shapes):
    flat = pack.reshape(-1)
    out, off = [], 0
    for s in shapes:
        n = _size(s)
        out.append(flat[off:off + n].reshape(s))
        off += n
    return out


def _full_weights(sh):
    cat = lambda a, axis: jnp.concatenate([a[s] for s in range(4)], axis=axis)
    w_in = cat(sh["w_in"], 2)
    z64 = jnp.zeros((2, D, 64), w_in.dtype)
    win_p = jnp.concatenate([w_in[..., :D], w_in[..., 1472:2496], w_in[..., 2496:], w_in[..., D:1472], z64], axis=2)
    w_uq = cat(sh["w_uq"], 1)
    wq_p = jnp.concatenate([w_uq, jnp.zeros((2, QR, NH, DH - DN - DR), w_uq.dtype)], axis=3).reshape(2, QR, NH * DH)
    w_ukv = cat(sh["w_ukv"], 3)
    wk_p = jnp.concatenate([w_ukv[..., :DN], jnp.zeros((2, KVR, NH, DH - DN), w_ukv.dtype)], axis=3).reshape(2, KVR, NH * DH)
    wv = w_ukv[..., DN:].reshape(2, KVR, NH * DV)
    return dict(win=win_p, wq=wq_p, wk=wk_p, wv=wv, wout=cat(sh["w_out"], 1), wfi=cat(sh["w_ffn_in"], 2),
                wfo=cat(sh["w_ffn_out"], 1))


def _shard_grads(g):
    dwin = g["win"]
    w_in = jnp.concatenate([dwin[:, :D], dwin[:, QKV0:QKV0 + 448], dwin[:, D:QKV0]], axis=1)
    split = lambda a, axis: jnp.split(a, 4, axis=axis)
    w_uq = g["wq"].reshape(QR, NH, DH)[..., :DN + DR]
    w_ukv = jnp.concatenate([g["wk"].reshape(KVR, NH, DH)[..., :DN], g["wv"].reshape(KVR, NH, DV)], axis=2)
    return [split(w_in, 1), split(w_uq, 0), split(w_ukv, 2), split(g["wout"], 0), split(g["wfi"], 1),
            split(g["wfo"], 0)]


def kernel(x, c, positions, w_ada, b_ada, w_in, conv_w, conv_b, lru_wa, lru_ba, lru_wx, lru_bx, lru_a_param, q_norm_g, kv_norm_g, w_uq, w_ukv, w_out, w_ffn_in, w_ffn_out, final_norm_g, loss_target, m_w_ada, m_b_ada, m_w_in, m_conv_w, m_conv_b, m_lru_wa, m_lru_ba, m_lru_wx, m_lru_bx, m_lru_a_param, m_q_norm_g, m_kv_norm_g, m_w_uq, m_w_ukv, m_w_out, m_w_ffn_in, m_w_ffn_out, m_final_norm_g, v_w_ada, v_b_ada, v_w_in, v_conv_w, v_conv_b, v_lru_wa, v_lru_ba, v_lru_wx, v_lru_bx, v_lru_a_param, v_q_norm_g, v_kv_norm_g, v_w_uq, v_w_ukv, v_w_out, v_w_ffn_in, v_w_ffn_out, v_final_norm_g):
    T = x.shape[1]
    tb = min(ROW_BLOCK, T)
    tq = min(ATTN_BLOCK, T)
    nq = T // tq
    mx, my, mc = _coords()
    chip = 2 * mx + my
    dev = 4 * mx + 2 * my + mc
    weights = dict(w_ada=w_ada, b_ada=b_ada, w_in=w_in, conv_w=conv_w, conv_b=conv_b, lru_wa=lru_wa, lru_ba=lru_ba,
                   lru_wx=lru_wx, lru_bx=lru_bx, lru_a_param=lru_a_param, q_norm_g=q_norm_g, kv_norm_g=kv_norm_g,
                   w_uq=w_uq, w_ukv=w_ukv, w_out=w_out, w_ffn_in=w_ffn_in, w_ffn_out=w_ffn_out,
                   final_norm_g=final_norm_g)
    mom = dict(w_ada=m_w_ada, b_ada=m_b_ada, w_in=m_w_in, conv_w=m_conv_w, conv_b=m_conv_b, lru_wa=m_lru_wa,
               lru_ba=m_lru_ba, lru_wx=m_lru_wx, lru_bx=m_lru_bx, lru_a_param=m_lru_a_param, q_norm_g=m_q_norm_g,
               kv_norm_g=m_kv_norm_g, w_uq=m_w_uq, w_ukv=m_w_ukv, w_out=m_w_out, w_ffn_in=m_w_ffn_in,
               w_ffn_out=m_w_ffn_out, final_norm_g=m_final_norm_g)
    var = dict(w_ada=v_w_ada, b_ada=v_b_ada, w_in=v_w_in, conv_w=v_conv_w, conv_b=v_conv_b, lru_wa=v_lru_wa,
               lru_ba=v_lru_ba, lru_wx=v_lru_wx, lru_bx=v_lru_bx, lru_a_param=v_lru_a_param, q_norm_g=v_q_norm_g,
               kv_norm_g=v_kv_norm_g, w_uq=v_w_uq, w_ukv=v_w_ukv, w_out=v_w_out, w_ffn_in=v_w_ffn_in,
               w_ffn_out=v_w_ffn_out, final_norm_g=v_final_norm_g)
    order = list(weights)

    ns_ada = w_ada.shape[2]
    c_all = _all_gather("gather_c", c, ALL_MASKS, True).reshape(8, D)
    b_sh = lax.dynamic_slice_in_dim(b_ada, chip * ns_ada, ns_ada, axis=1).reshape(2, 1, ns_ada)
    mod_sh = _ada_fwd("ada_fwd", c_all, w_ada, b_sh)
    mod_all = _all_gather("gather_mod", mod_sh.reshape(16, ns_ada), CHIP_MASKS, False)
    mod_mine = lax.dynamic_index_in_dim(mod_all.reshape(4, 2, 8, ns_ada), dev, axis=2, keepdims=False)
    mod = jnp.transpose(mod_mine, (1, 0, 2)).reshape(2, 6, 1, D)

    big_shapes = [weights[n].shape for n in BIG]
    big_rows = sum(_size(s) for s in big_shapes) // 1024
    wpack = _pack_rows([weights[n] for n in BIG], big_rows, bf16)
    wall = _gather_weights("gather_weights", wpack)
    per_chip = [_unpack_rows(wall[s], big_shapes) for s in range(4)]
    W = _full_weights({n: [per_chip[s][i] for s in range(4)] for i, n in enumerate(BIG)})
    wa_b = lru_wa.astype(bf16)
    wx_b = lru_wx.astype(bf16)

    pos = positions.reshape(T, 1)
    invf = ROPE_THETA ** (-jnp.arange(0, DR, 2, dtype=f32) / DR)
    invf_p = jnp.concatenate([jnp.zeros((DN,), f32), invf, invf, jnp.zeros((DH - DN - DR,), f32)]).reshape(1, DH)
    cosp, sap, sbp = _rope_tables("rope_tables", pos, invf_p, T, tb)

    ns_cw = conv_w.shape[2]
    cw_all = _all_gather("gather_conv_w", conv_w.reshape(2 * CONV, ns_cw), CHIP_MASKS, False)
    cw_full = jnp.transpose(cw_all, (1, 0, 2)).reshape(2, CONV, D)
    vec = lambda a, l: a[l].reshape(1, -1)

    xl = x.reshape(T, D)
    saved = []
    h1 = _norm_mod("norm_mod_0", xl, mod[0, 1], mod[0, 0], T, tb)
    for l in range(2):
        sh_m, sc_m, g_m, sh_f, sc_f, g_f = [mod[l, j] for j in range(6)]
        proj = _matmul(f"proj_{l}", h1, W["win"][l], "nn")
        u, r, ig, ya = _lru_fwd(f"lru_fwd_{l}", proj, pos, cw_full[l], vec(conv_b, l), wa_b[l], vec(lru_ba, l),
                                wx_b[l], vec(lru_bx, l), vec(lru_a_param, l), T, tb)
        qp, kp, vp, cq, ckv = _mla_prep(f"mla_prep_{l}", proj, cosp, sap, sbp, vec(q_norm_g, l), vec(kv_norm_g, l),
                                        W["wq"][l], W["wk"][l], W["wv"][l], T, tb)
        o, lse = _flash_fwd(f"flash_fwd_{l}", qp, kp, vp, T, tq)
        y = _gate_mix(f"gate_mix_{l}", proj, ya, o, T, tb)
        mo = _matmul(f"out_proj_{l}", y, W["wout"][l], "nn")
        x2, h2 = _resid_norm_mod(f"resid_norm_f_{l}", xl, g_m, mo, sc_f, sh_f, T, tb)
        gu = _matmul(f"ffn_in_{l}", h2, W["wfi"][l], "nn")
        act = _swiglu(f"swiglu_{l}", gu, T, tb)
        fo = _matmul(f"ffn_out_{l}", act, W["wfo"][l], "nn")
        saved.append(dict(x=xl, h1=h1, proj=proj, u=u, r=r, ig=ig, ya=ya, qp=qp, kp=kp, vp=vp, cq=cq, ckv=ckv, o=o,
                          lse=lse, y=y, mo=mo, x2=x2, h2=h2, gu=gu, act=act, fo=fo))
        if l == 0:
            xl, h1 = _resid_norm_mod("resid_norm_m_1", x2, g_f, fo, mod[1, 1], mod[1, 0], T, tb)

    dx, loss_part, dgfin = _final("final", saved[1]["x2"], mod[1, 5], saved[1]["fo"],
                                  final_norm_g.reshape(1, D), loss_target.reshape(T, D), T, tb)
    loss = lax.psum(loss_part[0, 0], ("x", "y", "c"))

    gl = {n: [None, None] for n in ("win", "wq", "wk", "wv", "wout", "wfi", "wfo", "conv_w", "conv_b", "lru_wa",
                                    "lru_ba", "lru_wx", "lru_bx", "lru_a_param", "q_norm_g", "kv_norm_g")}
    dmod = [None, None]
    for l in (1, 0):
        s = saved[l]
        sh_m, sc_m, g_m, sh_f, sc_f, g_f = [mod[l, j] for j in range(6)]
        dfo, dg_f = _resid_bwd(f"resid_bwd_f_{l}", dx, g_f, s["fo"], T, tb)
        dact = _matmul(f"d_act_{l}", dfo, W["wfo"][l], "nt")
        gl["wfo"][l] = _matmul(f"dw_ffn_out_{l}", s["act"], dfo, "tn", bf16)
        dgu = _swiglu_bwd(f"swiglu_bwd_{l}", dact, s["gu"], T, tb)
        dh2 = _matmul(f"d_h2_{l}", dgu, W["wfi"][l], "nt")
        gl["wfi"][l] = _matmul(f"dw_ffn_in_{l}", s["h2"], dgu, "tn", bf16)
        dx2, dsc_f, dsh_f = _norm_mod_bwd(f"norm_bwd_f_{l}", dh2, s["x2"], sc_f, dx, T, tb)
        dmo, dg_m = _resid_bwd(f"resid_bwd_m_{l}", dx2, g_m, s["mo"], T, tb)
        dy = _matmul(f"d_y_{l}", dmo, W["wout"][l], "nt")
        gl["wout"][l] = _matmul(f"dw_out_{l}", s["y"], dmo, "tn", bf16)
        dga, dgb, dya, do, dlt = _gate_mix_bwd(f"gate_mix_bwd_{l}", dy, s["proj"], s["ya"], s["o"], T, tb)
        lse_r = s["lse"].reshape(NH, nq, 1, tq)
        dl_r = jnp.transpose(dlt).reshape(NH, nq, 1, tq)
        dk, dv, dq = _flash_bwd(f"flash_bwd_{l}", s["qp"], s["kp"], s["vp"], do, lse_r, dl_r, T, tq)
        dqkv, dqp, dkp, dgq, dgkv = _mla_bwd(f"mla_bwd_{l}", dq, dk, dv, s["proj"], cosp, sap, sbp,
                                              vec(q_norm_g, l), vec(kv_norm_g, l), W["wq"][l], W["wk"][l], W["wv"][l],
                                              T, tb)
        gl["wq"][l] = _matmul(f"dw_uq_{l}", s["cq"], dqp, "tn", bf16)
        gl["wk"][l] = _matmul(f"dw_uk_{l}", s["ckv"], dkp, "tn", bf16)
        gl["wv"][l] = _matmul(f"dw_uv_{l}", s["ckv"], dv, "tn", bf16)
        dxl, dcw, dcb, dwa, dba, dwx, dbx, dsp = _lru_bwd(
            f"lru_bwd_{l}", dya, s["proj"], s["u"], s["r"], s["ig"], s["ya"], pos, cw_full[l], wa_b[l], wx_b[l],
            vec(lru_a_param, l), T, tb)
        dproj = jnp.concatenate([dxl, dga, dgb, dqkv], axis=1)
        dh1 = _matmul(f"d_h1_{l}", dproj, W["win"][l], "nt")
        gl["win"][l] = _matmul(f"dw_in_{l}", s["h1"], dproj, "tn", bf16)
        dx, dsc_m, dsh_m = _norm_mod_bwd(f"norm_bwd_m_{l}", dh1, s["x"], sc_m, dx2, T, tb)
        dmod[l] = jnp.concatenate([dsh_m, dsc_m, dg_m, dsh_f, dsc_f, dg_f], axis=1)
        gl["conv_w"][l], gl["conv_b"][l] = dcw, dcb[0]
        gl["lru_wa"][l], gl["lru_ba"][l], gl["lru_wx"][l], gl["lru_bx"][l] = dwa, dba[0], dwx, dbx[0]
        gl["lru_a_param"][l] = dsp[0]
        gl["q_norm_g"][l], gl["kv_norm_g"][l] = dgq[0], dgkv[0]
    grad_x = dx.reshape(1, T, D)
    gfull = {n: jnp.stack(gl[n]) for n in SMALL if n != "final_norm_g"}
    gfull["final_norm_g"] = dgfin[0]

    dmod_all = _all_gather("gather_dmod", jnp.concatenate(dmod, axis=1), ALL_MASKS, True).reshape(8, 2, 6, D)
    dmod_sum = _sum_slots("sum_dmod", dmod_all.reshape(8, 12, D)).reshape(2, 6 * D)
    dm_sh = lax.dynamic_slice_in_dim(jnp.transpose(dmod_all.reshape(8, 2, 6 * D), (1, 0, 2)), chip * ns_ada, ns_ada, axis=2)
    grads = {"w_ada": _ada_bwd("ada_bwd", jnp.transpose(c_all), dm_sh), "b_ada": dmod_sum}

    rh = big_rows // 2
    pieces = [_shard_grads({n: gl[n][l] for n in ("win", "wq", "wk", "wv", "wout", "wfi", "wfo")}) for l in range(2)]
    gbig = jnp.stack([_pack_rows([pieces[l][i][s] for i in range(len(BIG)) for l in range(2)], big_rows, bf16)
                      for s in range(4)]).reshape(4, 2, rh, 1024)
    rs = SMALL_ROWS // 8
    gsmall = _pack_rows([gfull[n] for n in SMALL], SMALL_ROWS, f32).reshape(4, 2, rs, 1024)
    idx = jnp.stack([chip, mc]).astype(jnp.int32)
    got_b, got_s = _swap_halves("rs_swap_halves", [gbig, gsmall])
    sum_b = _pair_sum("rs_pair_sum_big", gbig, got_b, idx, bf16)
    sum_s = _pair_sum("rs_pair_sum_small", gsmall, got_s, idx, f32)
    land_b, land_s = _scatter_chips("rs_scatter_chips", [sum_b, sum_s])
    tot_b = _chip_sum("rs_chip_sum_big", sum_b, land_b, idx, True)
    tot_s = _chip_sum("rs_chip_sum_small", sum_s, land_s, idx, False)
    big_mine = _join_halves("rs_join_halves", tot_b).reshape(big_rows, 1024)
    small_all = _all_gather("gather_small_grads", tot_s, ALL_MASKS, True).reshape(SMALL_ROWS, 1024)
    for n, gpiece in zip(BIG, _unpack_rows(big_mine, big_shapes)):
        grads[n] = gpiece
    for n, gs in zip(SMALL, _unpack_rows(small_all, [SMALL_SHAPES[n] for n in SMALL])):
        grads[n] = gs
    grads["conv_w"] = lax.dynamic_slice_in_dim(grads["conv_w"], chip * ns_cw, ns_cw, axis=2)

    deltas, new_m, new_v = {}, {}, {}
    for n in order:
        w = weights[n]
        view = (1, w.shape[0]) if w.ndim == 1 else (_size(w.shape[:-1]), w.shape[-1]) if w.shape[-1] >= 128 else (w.shape[0], _size(w.shape[1:]))
        d, m2, v2 = _adamw(f"adamw_{n}", w.reshape(view), grads[n].reshape(view), mom[n].reshape(view), var[n].reshape(view))
        deltas[n], new_m[n], new_v[n] = d.reshape(w.shape), m2.reshape(w.shape), v2.reshape(w.shape)
    return (loss, grad_x, *[grads[n] for n in order], *[deltas[n] for n in order], *[new_m[n] for n in order],
            *[new_v[n] for n in order])
```

```python
import functools

import jax
import jax.numpy as jnp
from jax import lax
from jax.experimental import pallas as pl
from jax.experimental.pallas import tpu as pltpu

f32 = jnp.float32
bf16 = jnp.bfloat16

D = 1024
NH = 8
DN = 128
DR = 64
DV = 128
DH = 256
QR = 256
KVR = 128
DFF = 2816
FH = DFF // 2
NB = 8
BW = 128
CONV = 4
PW = 3584
QKV0 = 3 * D
DIN = 3520
EPS = 1e-6
LRU_C = 8.0
ROPE_THETA = 10000.0
SCALE = (DN + DR) ** -0.5
LOG2E = 1.4426950408889634
LN2 = 0.6931471805599453
QSCALE = SCALE * LOG2E
NEG = -1e30
HPS = 2

ADAM_LR = 0.001
ADAM_B1 = 0.9
ADAM_B2 = 0.999
ADAM_EPS = 1e-08
ADAM_WD = 0.01
ADAM_STEP = 10

VMEM_LIMIT = 56 * 1024 * 1024
ROW_BLOCK = 256
ATTN_BLOCK = 512
MESH = pl.DeviceIdType.MESH
ANY = pl.BlockSpec(memory_space=pl.ANY)
CHIP_MASKS = ((1, 0, 0), (0, 1, 0), (1, 1, 0))
ALL_MASKS = ((0, 0, 1), (1, 0, 0), (0, 1, 0), (1, 1, 0), (1, 0, 1), (0, 1, 1), (1, 1, 1))


def _cparams(n_axes):
    return pltpu.CompilerParams(dimension_semantics=("arbitrary",) * n_axes, vmem_limit_bytes=VMEM_LIMIT)


def _pick(n, cands):
    for c in cands:
        if n % c == 0:
            return c
    return n


def _rowwise(name, body, T, tb, row_ins, vec_ins, row_outs, acc_outs=(), scratch=(), reverse=False):
    n = T // tb

    def rmap(i):
        return (n - 1 - i) if reverse else i

    in_specs = []
    for (_, w, cb) in row_ins:
        in_specs.append(pl.BlockSpec((tb, w), functools.partial(lambda i, cb: (rmap(i), cb), cb=cb)))
    for v in vec_ins:
        in_specs.append(pl.BlockSpec(v.shape, functools.partial(lambda i, nd: (0,) * nd, nd=v.ndim)))
    out_specs, out_shape = [], []
    for (w, dt) in row_outs:
        out_specs.append(pl.BlockSpec((tb, w), lambda i: (rmap(i), 0)))
        out_shape.append(jax.ShapeDtypeStruct((T, w), dt))
    for (s, dt) in acc_outs:
        out_specs.append(pl.BlockSpec(s, functools.partial(lambda i, nd: (0,) * nd, nd=len(s))))
        out_shape.append(jax.ShapeDtypeStruct(s, dt))

    def kern(*refs):
        body(*refs)

    return pl.pallas_call(
        kern, name=name, grid=(n,), in_specs=in_specs, out_specs=out_specs, out_shape=out_shape,
        scratch_shapes=list(scratch), compiler_params=_cparams(1),
    )(*[a for (a, _, _) in row_ins], *vec_ins)


def _matmul(name, a, b, mode, out_dtype=f32):
    if mode == "nn":
        (M, K), N = a.shape, b.shape[1]
    elif mode == "nt":
        (M, K), N = a.shape, b.shape[0]
    else:
        (K, M), N = a.shape, b.shape[1]
    wide = (1792, 1408, 1024, 512, 256, 128)
    tm = _pick(M, (512, 256, 128)) if mode != "tn" else _pick(M, (1024, 1408, 512, 256, 128))
    tn = _pick(N, wide)
    tk = _pick(K, wide) if mode != "tn" else _pick(K, (512, 256, 128))
    nk = K // tk
    dims = {"nn": (((1,), (0,)), ((), ())), "nt": (((1,), (1,)), ((), ())), "tn": (((0,), (0,)), ((), ()))}[mode]
    if mode == "tn":
        a_spec = pl.BlockSpec((tk, tm), lambda j, i, k: (k, i))
    else:
        a_spec = pl.BlockSpec((tm, tk), lambda j, i, k: (i, k))
    if mode == "nt":
        b_spec = pl.BlockSpec((tn, tk), lambda j, i, k: (j, k))
    else:
        b_spec = pl.BlockSpec((tk, tn), lambda j, i, k: (k, j))
    o_spec = pl.BlockSpec((tm, tn), lambda j, i, k: (i, j))
    use_acc = nk > 1 and out_dtype != f32

    def kern(a_ref, b_ref, o_ref, *scr):
        k = pl.program_id(2)
        d = lax.dot_general(a_ref[...].astype(bf16), b_ref[...].astype(bf16), dims, preferred_element_type=f32)
        if nk == 1:
            o_ref[...] = d.astype(out_dtype)
        else:
            acc = scr[0] if use_acc else o_ref

            @pl.when(k == 0)
            def _():
                acc[...] = d

            @pl.when(k > 0)
            def _():
                acc[...] += d

            if use_acc:
                @pl.when(k == nk - 1)
                def _():
                    o_ref[...] = acc[...].astype(out_dtype)

    return pl.pallas_call(
        kern, name=name, grid=(N // tn, M // tm, nk), in_specs=[a_spec, b_spec], out_specs=o_spec,
        out_shape=jax.ShapeDtypeStruct((M, N), out_dtype),
        scratch_shapes=[pltpu.VMEM((tm, tn), f32)] if use_acc else [],
        compiler_params=_cparams(3),
    )(a, b)


def _colsum(v):
    return jnp.sum(v, axis=0, keepdims=True)


def _init_acc(step, *refs):
    @pl.when(step == 0)
    def _():
        for r in refs:
            r[...] = jnp.zeros_like(r)


def _norm_mod(name, x, sc, sh, T, tb):
    def body(x_ref, sc_ref, sh_ref, h_ref):
        xv = x_ref[...]
        rstd = lax.rsqrt(jnp.mean(xv * xv, axis=-1, keepdims=True) + EPS)
        h_ref[...] = (xv * rstd * (1.0 + sc_ref[...]) + sh_ref[...]).astype(bf16)

    return _rowwise(name, body, T, tb, [(x, D, 0)], [sc, sh], [(D, bf16)])[0]


def _resid_norm_mod(name, x, g, m, sc, sh, T, tb):
    def body(x_ref, m_ref, g_ref, sc_ref, sh_ref, x2_ref, h_ref):
        xv = x_ref[...] + g_ref[...] * m_ref[...]
        x2_ref[...] = xv
        rstd = lax.rsqrt(jnp.mean(xv * xv, axis=-1, keepdims=True) + EPS)
        h_ref[...] = (xv * rstd * (1.0 + sc_ref[...]) + sh_ref[...]).astype(bf16)

    return _rowwise(name, body, T, tb, [(x, D, 0), (m, D, 0)], [g, sc, sh], [(D, f32), (D, bf16)])


def _norm_mod_bwd(name, dh, x, sc, dres, T, tb):
    def body(dh_ref, x_ref, dr_ref, sc_ref, dx_ref, dsc_ref, dsh_ref):
        _init_acc(pl.program_id(0), dsc_ref, dsh_ref)
        xv = x_ref[...]
        dhv = dh_ref[...]
        rstd = lax.rsqrt(jnp.mean(xv * xv, axis=-1, keepdims=True) + EPS)
        nv = xv * rstd
        dn = dhv * (1.0 + sc_ref[...])
        dx_ref[...] = dr_ref[...] + rstd * (dn - nv * jnp.mean(dn * nv, axis=-1, keepdims=True))
        dsc_ref[...] += _colsum(dhv * nv)
        dsh_ref[...] += _colsum(dhv)

    return _rowwise(name, body, T, tb, [(dh, D, 0), (x, D, 0), (dres, D, 0)], [sc], [(D, f32)],
                    [((1, D), f32), ((1, D), f32)])


def _resid_bwd(name, dx, g, m, T, tb):
    def body(dx_ref, m_ref, g_ref, dm_ref, dg_ref):
        _init_acc(pl.program_id(0), dg_ref)
        dxv = dx_ref[...]
        dm_ref[...] = (dxv * g_ref[...]).astype(bf16)
        dg_ref[...] += _colsum(dxv * m_ref[...])

    return _rowwise(name, body, T, tb, [(dx, D, 0), (m, D, 0)], [g], [(D, bf16)], [((1, D), f32)])


def _expm1_neg(y):
    poly = y * (1.0 + y * (0.5 + y * (1.0 / 6 + y * (1.0 / 24 + y * (1.0 / 120 + y * (1.0 / 720 + y * (1.0 / 5040)))))))
    return jnp.where(y > -0.25, poly, jnp.exp(y) - 1.0)


def _lru_gates(r, sp, reset):
    log_a = -LRU_C * r * sp
    a = jnp.where(reset, 0.0, jnp.exp(log_a))
    mult = jnp.where(reset, 1.0, jnp.sqrt(-_expm1_neg(2.0 * log_a)))
    return a, mult


def _block_dot(v, w_ref, dims):
    outs = [lax.dot_general(v[:, n * BW:(n + 1) * BW], w_ref[n], dims, preferred_element_type=f32) for n in range(NB)]
    return jnp.concatenate(outs, axis=1)


def _lru_fwd(name, proj, pos, cw, cb, wa, ba, wx, bx, ap, T, tb):
    nsteps = tb.bit_length() - 1

    def body(x_ref, pos_ref, cw_ref, cb_ref, wa_ref, ba_ref, wx_ref, bx_ref, ap_ref,
             u_ref, r_ref, i_ref, h_ref, prev_x, carry_h):
        _init_acc(pl.program_id(0), prev_x, carry_h)
        x = x_ref[...]
        px = prev_x[...]
        row = lax.broadcasted_iota(jnp.int32, (tb, 1), 0)
        u = x * cw_ref[CONV - 1:CONV, :] + cb_ref[...]
        for j in range(1, CONV):
            z = jnp.where(row >= tb - j, px, x)
            u = u + pltpu.roll(z, j, 0) * cw_ref[CONV - 1 - j:CONV - j, :]
        ub = u.astype(bf16)
        nn = (((1,), (0,)), ((), ()))
        r = jax.nn.sigmoid(_block_dot(ub, wa_ref, nn) + ba_ref[...])
        ig = jax.nn.sigmoid(_block_dot(ub, wx_ref, nn) + bx_ref[...])
        sp = jax.nn.softplus(-ap_ref[...])
        reset = pos_ref[...] == 0
        a, mult = _lru_gates(r, sp, reset)
        b = u * ig * mult
        sub = row & 7
        for s in (1, 2, 4):
            keep = sub >= s
            a_sh = jnp.where(keep, pltpu.roll(a, s, 0), 1.0)
            b_sh = jnp.where(keep, pltpu.roll(b, s, 0), 0.0)
            b = a * b_sh + b
            a = a * a_sh
        carry = carry_h[...]
        for g in range(tb // 8):
            rows = slice(8 * g, 8 * g + 8)
            hg = b[rows] + a[rows] * carry
            h_ref[rows, :] = hg
            carry = hg[7:8]
        carry_h[...] = carry
        u_ref[...] = u
        r_ref[...] = r
        i_ref[...] = ig
        prev_x[...] = x

    return _rowwise(name, body, T, tb, [(proj, D, 0), (pos, 1, 0)], [cw, cb, wa, ba, wx, bx, ap],
                    [(D, f32)] * 4, scratch=[pltpu.VMEM((tb, D), f32), pltpu.VMEM((1, D), f32)])


def _lru_bwd(name, dh, proj, u, r, ig, h, pos, cw, wa, wx, ap, T, tb):
    n = T // tb
    nsteps = tb.bit_length() - 1
    t8 = tb // 8

    def rmap(i):
        return n - 1 - i

    def body(dh_ref, x_ref, u_ref, r_ref, i_ref, h_ref, hp_ref, pos_ref, cw_ref, wa_ref, wx_ref, ap_ref,
             dx_ref, dcw_ref, dcb_ref, dwa_ref, dba_ref, dwx_ref, dbx_ref, dsp_ref, next_du, carry_g, carry_a):
        step = pl.program_id(0)
        _init_acc(step, dcw_ref, dcb_ref, dwa_ref, dba_ref, dwx_ref, dbx_ref, dsp_ref, next_du, carry_g, carry_a)
        blk = n - 1 - step
        x = x_ref[...]
        u = u_ref[...]
        r = r_ref[...]
        ig = i_ref[...]
        h = h_ref[...]
        row = lax.broadcasted_iota(jnp.int32, (tb, 1), 0)
        sp = jax.nn.softplus(-ap_ref[...])
        reset = pos_ref[...] == 0
        a, mult = _lru_gates(r, sp, reset)
        A = pltpu.roll(jnp.where(row == 0, carry_a[...], a), tb - 1, 0)
        B = dh_ref[...]
        sub = row & 7
        for s in (1, 2, 4):
            keep = sub < 8 - s
            a_sh = jnp.where(keep, pltpu.roll(A, tb - s, 0), 1.0)
            b_sh = jnp.where(keep, pltpu.roll(B, tb - s, 0), 0.0)
            B = B + A * b_sh
            A = A * a_sh
        carry = carry_g[...]
        parts = [None] * (tb // 8)
        for grp in reversed(range(tb // 8)):
            rows = slice(8 * grp, 8 * grp + 8)
            parts[grp] = B[rows] + A[rows] * carry
            carry = parts[grp][0:1]
        g = jnp.concatenate(parts, axis=0)
        carry_g[...] = carry
        carry_a[...] = a[0:1, :]
        h_last = jnp.where(blk == 0, 0.0, hp_ref[7:8, :])
        h_prev = pltpu.roll(jnp.where(row == tb - 1, h_last, h), 1, 0)
        da = g * h_prev
        gu = g * u
        dlog_a = jnp.where(reset, 0.0, da * a - gu * ig * (a * a) / mult)
        du = g * ig * mult
        di = gu * mult
        dr = dlog_a * (-LRU_C * sp)
        dsp_ref[...] += _colsum(dlog_a * (-LRU_C * r)) * (-jax.nn.sigmoid(-ap_ref[...]))
        dzr = dr * r * (1.0 - r)
        dzi = di * ig * (1.0 - ig)
        dzr_b = dzr.astype(bf16)
        dzi_b = dzi.astype(bf16)
        nt = (((1,), (1,)), ((), ()))
        du = du + _block_dot(dzr_b, wa_ref, nt) + _block_dot(dzi_b, wx_ref, nt)
        ub = u.astype(bf16)
        tn = (((0,), (0,)), ((), ()))
        for nb in range(NB):
            sl = slice(nb * BW, (nb + 1) * BW)
            dwa_ref[nb] += lax.dot_general(ub[:, sl], dzr_b[:, sl], tn, preferred_element_type=f32)
            dwx_ref[nb] += lax.dot_general(ub[:, sl], dzi_b[:, sl], tn, preferred_element_type=f32)
        dba_ref[...] += _colsum(dzr)
        dbx_ref[...] += _colsum(dzi)
        dcb_ref[...] += _colsum(du)
        ndu = next_du[...]
        dx = du * cw_ref[CONV - 1:CONV, :]
        dcw_ref[CONV - 1:CONV, :] += _colsum(x * du)
        for j in range(1, CONV):
            up = pltpu.roll(jnp.where(row < j, ndu, du), tb - j, 0)
            dx = dx + up * cw_ref[CONV - 1 - j:CONV - j, :]
            dcw_ref[CONV - 1 - j:CONV - j, :] += _colsum(x * up)
        dx_ref[...] = dx.astype(bf16)
        next_du[...] = du

    row_specs = [pl.BlockSpec((tb, D), lambda i: (rmap(i), 0)) for _ in range(6)]
    in_specs = row_specs + [
        pl.BlockSpec((8, D), lambda i: (jnp.maximum(rmap(i) * t8 - 1, 0), 0)),
        pl.BlockSpec((tb, 1), lambda i: (rmap(i), 0)),
        pl.BlockSpec((CONV, D), lambda i: (0, 0)),
        pl.BlockSpec((NB, BW, BW), lambda i: (0, 0, 0)),
        pl.BlockSpec((NB, BW, BW), lambda i: (0, 0, 0)),
        pl.BlockSpec((1, D), lambda i: (0, 0)),
    ]
    vec = lambda s: pl.BlockSpec(s, functools.partial(lambda i, nd: (0,) * nd, nd=len(s)))
    acc_shapes = [(CONV, D), (1, D), (NB, BW, BW), (1, D), (NB, BW, BW), (1, D), (1, D)]
    return pl.pallas_call(
        body, name=name, grid=(n,), in_specs=in_specs,
        out_specs=[pl.BlockSpec((tb, D), lambda i: (rmap(i), 0))] + [vec(s) for s in acc_shapes],
        out_shape=[jax.ShapeDtypeStruct((T, D), bf16)] + [jax.ShapeDtypeStruct(s, f32) for s in acc_shapes],
        scratch_shapes=[pltpu.VMEM((tb, D), f32), pltpu.VMEM((1, D), f32), pltpu.VMEM((1, D), f32)],
        compiler_params=_cparams(1),
    )(dh, proj, u, r, ig, h, h, pos, cw, wa, wx, ap)


def _rope_tables(name, pos, invf, T, tb):
    def body(pos_ref, invf_ref, cos_ref, sa_ref, sb_ref):
        ang = pos_ref[...].astype(f32) * invf_ref[...]
        lane = lax.broadcasted_iota(jnp.int32, (tb, DH), 1)
        first = (lane >= DN) & (lane < DN + DR // 2)
        second = (lane >= DN + DR // 2) & (lane < DN + DR)
        c = jnp.cos(ang)
        s = jnp.sin(ang)
        cos_ref[...] = jnp.where(lane < DN, 1.0, jnp.where(first | second, c, 0.0))
        sa_ref[...] = jnp.where(first, -s, 0.0)
        sb_ref[...] = jnp.where(second, s, 0.0)

    return _rowwise(name, body, T, tb, [(pos, 1, 0)], [invf], [(DH, f32)] * 3)


def _rope(z, cos8, sa8, sb8):
    w = z.shape[1]
    return z * cos8 + pltpu.roll(z, w - DR // 2, 1) * sa8 + pltpu.roll(z, DR // 2, 1) * sb8


def _rope_t(d, cos8, sa8, sb8):
    w = d.shape[1]
    return d * cos8 + pltpu.roll(d * sa8, DR // 2, 1) + pltpu.roll(d * sb8, w - DR // 2, 1)


def _mla_prep(name, proj, cosp, sap, sbp, gq, gkv, wq, wk, wv, T, tb):
    def body(qkv_ref, cos_ref, sa_ref, sb_ref, gq_ref, gkv_ref, wq_ref, wk_ref, wv_ref,
             q_ref, k_ref, v_ref, cq_ref, ckv_ref):
        qkv = qkv_ref[...]
        qd = qkv[:, :QR]
        kvd = qkv[:, QR:QR + KVR]
        slab = qkv[:, QR + KVR:]
        cq = (qd * lax.rsqrt(jnp.mean(qd * qd, axis=-1, keepdims=True) + EPS) * gq_ref[...]).astype(bf16)
        ckv = (kvd * lax.rsqrt(jnp.mean(kvd * kvd, axis=-1, keepdims=True) + EPS) * gkv_ref[...]).astype(bf16)
        cos8 = jnp.tile(cos_ref[...], (1, NH))
        sa8 = jnp.tile(sa_ref[...], (1, NH))
        sb8 = jnp.tile(sb_ref[...], (1, NH))
        qpre = jnp.dot(cq, wq_ref[...], preferred_element_type=f32)
        shared = jnp.concatenate([jnp.zeros((tb, DN), f32), slab], axis=1)
        kpre = jnp.dot(ckv, wk_ref[...], preferred_element_type=f32) + jnp.tile(shared, (1, NH))
        q_ref[...] = (_rope(qpre, cos8, sa8, sb8) * QSCALE).astype(bf16)
        k_ref[...] = _rope(kpre, cos8, sa8, sb8).astype(bf16)
        v_ref[...] = jnp.dot(ckv, wv_ref[...], preferred_element_type=f32).astype(bf16)
        cq_ref[...] = cq
        ckv_ref[...] = ckv

    return _rowwise(name, body, T, tb,
                    [(proj, 512, QKV0 // 512), (cosp, DH, 0), (sap, DH, 0), (sbp, DH, 0)],
                    [gq, gkv, wq, wk, wv],
                    [(NH * DH, bf16), (NH * DH, bf16), (NH * DV, bf16), (QR, bf16), (KVR, bf16)])


def _mla_bwd(name, dq, dk, dv, proj, cosp, sap, sbp, gq, gkv, wq, wk, wv, T, tb):
    def body(dq_ref, dk_ref, dv_ref, qkv_ref, cos_ref, sa_ref, sb_ref, gq_ref, gkv_ref, wq_ref, wk_ref, wv_ref,
             dqkv_ref, dqp_ref, dkp_ref, dgq_ref, dgkv_ref):
        _init_acc(pl.program_id(0), dgq_ref, dgkv_ref)
        cos8 = jnp.tile(cos_ref[...], (1, NH))
        sa8 = jnp.tile(sa_ref[...], (1, NH))
        sb8 = jnp.tile(sb_ref[...], (1, NH))
        dqp = _rope_t(dq_ref[...] * SCALE, cos8, sa8, sb8)
        dkp = _rope_t(dk_ref[...], cos8, sa8, sb8)
        dqp_b = dqp.astype(bf16)
        dkp_b = dkp.astype(bf16)
        dqp_ref[...] = dqp_b
        dkp_ref[...] = dkp_b
        nt = (((1,), (1,)), ((), ()))
        dcq = lax.dot_general(dqp_b, wq_ref[...], nt, preferred_element_type=f32)
        dckv = (lax.dot_general(dkp_b, wk_ref[...], nt, preferred_element_type=f32)
                + lax.dot_general(dv_ref[...], wv_ref[...], nt, preferred_element_type=f32))
        dshared = dkp[:, 0:DH]
        for hd in range(1, NH):
            dshared = dshared + dkp[:, hd * DH:(hd + 1) * DH]
        qkv = qkv_ref[...]
        qd = qkv[:, :QR]
        kvd = qkv[:, QR:QR + KVR]
        nq = qd * lax.rsqrt(jnp.mean(qd * qd, axis=-1, keepdims=True) + EPS)
        nkv = kvd * lax.rsqrt(jnp.mean(kvd * kvd, axis=-1, keepdims=True) + EPS)
        dgq_ref[...] += _colsum(dcq * nq)
        dgkv_ref[...] += _colsum(dckv * nkv)
        dnq = dcq * gq_ref[...]
        dnkv = dckv * gkv_ref[...]
        rq = lax.rsqrt(jnp.mean(qd * qd, axis=-1, keepdims=True) + EPS)
        rkv = lax.rsqrt(jnp.mean(kvd * kvd, axis=-1, keepdims=True) + EPS)
        dqd = rq * (dnq - nq * jnp.mean(dnq * nq, axis=-1, keepdims=True))
        dkvd = rkv * (dnkv - nkv * jnp.mean(dnkv * nkv, axis=-1, keepdims=True))
        dqkv_ref[...] = jnp.concatenate([dqd, dkvd, dshared[:, DN:]], axis=1).astype(bf16)

    return _rowwise(name, body, T, tb,
                    [(dq, NH * DH, 0), (dk, NH * DH, 0), (dv, NH * DV, 0), (proj, 512, QKV0 // 512),
                     (cosp, DH, 0), (sap, DH, 0), (sbp, DH, 0)],
                    [gq, gkv, wq, wk, wv],
                    [(512, bf16), (NH * DH, bf16), (NH * DH, bf16)],
                    [((1, QR), f32), ((1, KVR), f32)])


def _flash_fwd(name, q, k, v, T, tq, pack=None):
    nq = T // tq
    unroll = 2

    def body(q_ref, k_ref, v_ref, o_ref, lse_ref):
        i = pl.program_id(1)
        nt = (((1,), (1,)), ((), ()))
        qs = [q_ref[:, hh * DH:(hh + 1) * DH] for hh in range(HPS)]

        def one(hh, j, carry, masked):
            m, l, acc = carry
            off = pl.multiple_of(j * tq, tq)
            kj = k_ref[pl.ds(off, tq), hh * DH:(hh + 1) * DH]
            vj = v_ref[pl.ds(off, tq), hh * DV:(hh + 1) * DV]
            s = lax.dot_general(qs[hh], kj, nt, preferred_element_type=f32)
            if masked:
                rr = lax.broadcasted_iota(jnp.int32, (tq, tq), 0)
                cc = lax.broadcasted_iota(jnp.int32, (tq, tq), 1)
                s = jnp.where(cc <= rr, s, NEG)
            m_new = jnp.maximum(m, jnp.max(s, axis=-1, keepdims=True))
            alpha = jnp.exp2(m - m_new)
            p = jnp.exp2(s - m_new)
            l = alpha * l + jnp.sum(p, axis=-1, keepdims=True)
            acc = alpha * acc + jnp.dot(p.astype(bf16), vj, preferred_element_type=f32)
            return m_new, l, acc

        def step(j, carries, masked):
            return tuple(one(hh, j, carries[hh], masked) for hh in range(HPS))

        def pair(jj, carries):
            for u in range(unroll):
                carries = step(jj * unroll + u, carries, False)
            return carries

        init1 = (jnp.full((tq, 1), NEG, f32), jnp.zeros((tq, 1), f32), jnp.zeros((tq, DV), f32))
        carry = lax.fori_loop(0, i // unroll, pair, tuple(init1 for _ in range(HPS)))
        carry = lax.fori_loop((i // unroll) * unroll, i, lambda j, c: step(j, c, False), carry)
        carry = step(i, carry, True)
        for hh in range(HPS):
            m, l, acc = carry[hh]
            o_ref[:, hh * DV:(hh + 1) * DV] = acc / l
            lse_ref[hh] = m + jnp.log2(l)

    def body_with_gather(q_ref, k_ref, v_ref, w_ref, o_ref, lse_ref, wall_ref, *sems):
        h = pl.program_id(0)
        i = pl.program_id(1)

        @pl.when((h == 0) & (i == 0))
        def _():
            mine, first, _ = _gather_copies(w_ref, wall_ref, *sems)
            _gather_start(mine, first)

        @pl.when((h == ngrp // 2) & (i == 0))
        def _():
            _, first, passed = _gather_copies(w_ref, wall_ref, *sems)
            _gather_forward(first, passed)

        body(q_ref, k_ref, v_ref, o_ref, lse_ref)

        @pl.when((h == ngrp - 1) & (i == nq - 1))
        def _():
            _gather_finish(*_gather_copies(w_ref, wall_ref, *sems))

    ngrp = NH // HPS
    in_specs = [pl.BlockSpec((tq, HPS * DH), lambda h, i: (i, h)),
                pl.BlockSpec((T, HPS * DH), lambda h, i: (0, h), pipeline_mode=pl.Buffered(1)),
                pl.BlockSpec((T, HPS * DV), lambda h, i: (0, h), pipeline_mode=pl.Buffered(1))]
    out_specs = [pl.BlockSpec((tq, HPS * DV), lambda h, i: (i, h)), pl.BlockSpec((HPS, tq, 1), lambda h, i: (h, i, 0))]
    out_shape = [jax.ShapeDtypeStruct((T, NH * DV), f32), jax.ShapeDtypeStruct((NH, T, 1), f32)]
    if pack is None:
        return pl.pallas_call(body, name=name, grid=(ngrp, nq), in_specs=in_specs, out_specs=out_specs,
                              out_shape=out_shape, compiler_params=_cparams(2))(q, k, v)
    return pl.pallas_call(
        body_with_gather, name=name, grid=(ngrp, nq), in_specs=in_specs + [ANY], out_specs=out_specs + [ANY],
        out_shape=out_shape + [jax.ShapeDtypeStruct((4,) + pack.shape, pack.dtype)],
        scratch_shapes=_gather_sems(pack), compiler_params=_cparams(2),
    )(q, k, v, pack)


def _flash_bwd(name, q, k, v, do, lse, dl, T, tq):
    nq = T // tq

    def body(k_ref, v_ref, q_ref, do_ref, lse_ref, dl_ref, dk_ref, dv_ref, dq_ref):
        j = pl.program_id(1)

        @pl.when(j == 0)
        def _():
            dq_ref[...] = jnp.zeros_like(dq_ref)

        nt = (((1,), (1,)), ((), ()))
        tn = (((0,), (0,)), ((), ()))
        ks = [k_ref[:, hh * DH:(hh + 1) * DH] for hh in range(HPS)]
        vs = [v_ref[:, hh * DV:(hh + 1) * DV] for hh in range(HPS)]

        def one(hh, i, carry, masked):
            dk, dv = carry
            off = pl.multiple_of(i * tq, tq)
            qi = q_ref[pl.ds(off, tq), hh * DH:(hh + 1) * DH]
            doi = do_ref[pl.ds(off, tq), hh * DV:(hh + 1) * DV]
            st = lax.dot_general(ks[hh], qi, nt, preferred_element_type=f32)
            if masked:
                kr = lax.broadcasted_iota(jnp.int32, (tq, tq), 0)
                qc = lax.broadcasted_iota(jnp.int32, (tq, tq), 1)
                st = jnp.where(qc >= kr, st, NEG)
            pt = jnp.exp2(st - lse_ref[hh, i])
            dv = dv + jnp.dot(pt.astype(bf16), doi, preferred_element_type=f32)
            dpt = lax.dot_general(vs[hh], doi, nt, preferred_element_type=f32)
            dst = (pt * (dpt - dl_ref[hh, i])).astype(bf16)
            dk = dk + jnp.dot(dst, qi, preferred_element_type=f32)
            dq_ref[pl.ds(off, tq), hh * DH:(hh + 1) * DH] += lax.dot_general(dst, ks[hh], tn, preferred_element_type=f32)
            return dk, dv

        def step(i, carries, masked):
            return tuple(one(hh, i, carries[hh], masked) for hh in range(HPS))

        zero = tuple((jnp.zeros((tq, DH), f32), jnp.zeros((tq, DV), f32)) for _ in range(HPS))
        carry = step(j, zero, True)
        carry = lax.fori_loop(j + 1, nq, lambda i, c: step(i, c, False), carry)
        for hh in range(HPS):
            dk, dv = carry[hh]
            dk_ref[:, hh * DH:(hh + 1) * DH] = dk * LN2
            dv_ref[:, hh * DV:(hh + 1) * DV] = dv.astype(bf16)

    return pl.pallas_call(
        body, name=name, grid=(NH // HPS, nq),
        in_specs=[pl.BlockSpec((tq, HPS * DH), lambda h, j: (j, h)),
                  pl.BlockSpec((tq, HPS * DV), lambda h, j: (j, h)),
                  pl.BlockSpec((T, HPS * DH), lambda h, j: (0, h), pipeline_mode=pl.Buffered(1)),
                  pl.BlockSpec((T, HPS * DV), lambda h, j: (0, h), pipeline_mode=pl.Buffered(1)),
                  pl.BlockSpec((HPS, nq, 1, tq), lambda h, j: (h, 0, 0, 0)),
                  pl.BlockSpec((HPS, nq, 1, tq), lambda h, j: (h, 0, 0, 0))],
        out_specs=[pl.BlockSpec((tq, HPS * DH), lambda h, j: (j, h)),
                   pl.BlockSpec((tq, HPS * DV), lambda h, j: (j, h)),
                   pl.BlockSpec((T, HPS * DH), lambda h, j: (0, h), pipeline_mode=pl.Buffered(1))],
        out_shape=[jax.ShapeDtypeStruct((T, NH * DH), f32), jax.ShapeDtypeStruct((T, NH * DV), bf16),
                   jax.ShapeDtypeStruct((T, NH * DH), f32)],
        compiler_params=_cparams(2),
    )(k, v, q, do, lse, dl)


def _gate_mix(name, proj, ya, o, T, tb):
    def body(ga_ref, gb_ref, ya_ref, o_ref, y_ref):
        y_ref[...] = (jax.nn.sigmoid(ga_ref[...]) * ya_ref[...] + jax.nn.sigmoid(gb_ref[...]) * o_ref[...]).astype(bf16)

    return _rowwise(name, body, T, tb, [(proj, D, 1), (proj, D, 2), (ya, D, 0), (o, D, 0)], [], [(D, bf16)])[0]


def _gate_mix_bwd(name, dy, proj, ya, o, T, tb):
    def body(dy_ref, ga_ref, gb_ref, ya_ref, o_ref, dga_ref, dgb_ref, dya_ref, do_ref, dl_ref):
        dy = dy_ref[...]
        sa = jax.nn.sigmoid(ga_ref[...])
        sb = jax.nn.sigmoid(gb_ref[...])
        ov = o_ref[...]
        dga_ref[...] = (dy * ya_ref[...] * sa * (1.0 - sa)).astype(bf16)
        dgb_ref[...] = (dy * ov * sb * (1.0 - sb)).astype(bf16)
        dya_ref[...] = dy * sa
        do = dy * sb
        do_ref[...] = do.astype(bf16)
        prod = do * ov
        for hd in range(NH):
            dl_ref[:, hd:hd + 1] = jnp.sum(prod[:, hd * DV:(hd + 1) * DV], axis=-1, keepdims=True)

    return _rowwise(name, body, T, tb, [(dy, D, 0), (proj, D, 1), (proj, D, 2), (ya, D, 0), (o, D, 0)], [],
                    [(D, bf16), (D, bf16), (D, f32), (D, bf16), (NH, f32)])


def _ffn_in_swiglu(name, h, w, T):
    tm = _pick(T, (512, 256, 128))

    def body(h_ref, w_ref, z_ref, a_ref):
        z = jnp.dot(h_ref[...], w_ref[...], preferred_element_type=f32)
        g = z[:, :FH]
        z_ref[...] = z.astype(bf16)
        a_ref[...] = (g * jax.nn.sigmoid(g) * z[:, FH:]).astype(bf16)

    return pl.pallas_call(
        body, name=name, grid=(2, T // tm),
        in_specs=[pl.BlockSpec((tm, D), lambda j, i: (i, 0)), pl.BlockSpec((D, 2 * FH), lambda j, i: (0, j))],
        out_specs=[pl.BlockSpec((tm, 2 * FH), lambda j, i: (i, j)), pl.BlockSpec((tm, FH), lambda j, i: (i, j))],
        out_shape=[jax.ShapeDtypeStruct((T, 2 * DFF), bf16), jax.ShapeDtypeStruct((T, DFF), bf16)],
        compiler_params=_cparams(2),
    )(h, w)


def _dact_swiglu_bwd(name, dfo, w, z, T):
    tm = _pick(T, (512, 256, 128))

    def body(d_ref, w_ref, z_ref, o_ref):
        da = lax.dot_general(d_ref[...], w_ref[...], (((1,), (1,)), ((), ())), preferred_element_type=f32)
        g = z_ref[:, :FH].astype(f32)
        u = z_ref[:, FH:].astype(f32)
        sg = jax.nn.sigmoid(g)
        o_ref[:, :FH] = (da * u * sg * (1.0 + g * (1.0 - sg))).astype(bf16)
        o_ref[:, FH:] = (da * g * sg).astype(bf16)

    return pl.pallas_call(
        body, name=name, grid=(2, T // tm),
        in_specs=[pl.BlockSpec((tm, D), lambda j, i: (i, 0)), pl.BlockSpec((FH, D), lambda j, i: (j, 0)),
                  pl.BlockSpec((tm, 2 * FH), lambda j, i: (i, j))],
        out_specs=pl.BlockSpec((tm, 2 * FH), lambda j, i: (i, j)),
        out_shape=jax.ShapeDtypeStruct((T, 2 * DFF), bf16), compiler_params=_cparams(2),
    )(dfo, w, z)


def _final(name, x, g, m, gfin, tgt, T, tb):
    def body(x_ref, m_ref, t_ref, g_ref, gf_ref, dx_ref, loss_ref, dgf_ref):
        _init_acc(pl.program_id(0), loss_ref, dgf_ref)
        xv = x_ref[...] + g_ref[...] * m_ref[...]
        rstd = lax.rsqrt(jnp.mean(xv * xv, axis=-1, keepdims=True) + EPS)
        nv = xv * rstd
        err = nv * gf_ref[...] - t_ref[...]
        loss_ref[...] += 0.5 * jnp.sum(jnp.mean(err * err, axis=-1, keepdims=True), axis=0, keepdims=True)
        dy = err * (1.0 / D)
        dgf_ref[...] += _colsum(dy * nv)
        dn = dy * gf_ref[...]
        dx_ref[...] = rstd * (dn - nv * jnp.mean(dn * nv, axis=-1, keepdims=True))

    return _rowwise(name, body, T, tb, [(x, D, 0), (m, D, 0), (tgt, D, 0)], [g, gfin], [(D, f32)],
                    [((1, 1), f32), ((1, D), f32)])


def _adamw(name, w, g, m, v):
    R, C = w.shape
    tb = _pick(R, (256, 128, 64, 32, 16, 8))
    c1 = 1.0 - ADAM_B1 ** ADAM_STEP
    c2 = 1.0 - ADAM_B2 ** ADAM_STEP

    def body(w_ref, g_ref, m_ref, v_ref, d_ref, m2_ref, v2_ref):
        gv = g_ref[...]
        m2 = ADAM_B1 * m_ref[...] + (1.0 - ADAM_B1) * gv
        v2 = ADAM_B2 * v_ref[...] + (1.0 - ADAM_B2) * (gv * gv)
        m2_ref[...] = m2
        v2_ref[...] = v2
        d_ref[...] = -ADAM_LR * ((m2 / c1) / (jnp.sqrt(v2 / c2) + ADAM_EPS) + ADAM_WD * w_ref[...])

    return _rowwise(name, body, R, tb, [(w, C, 0), (g, C, 0), (m, C, 0), (v, C, 0)], [], [(C, f32)] * 3)


def _ada_fwd(name, c_all, w_ada, b_sh):
    nl, _, ns = w_ada.shape

    def body(c_ref, w_ref, b_ref, o_ref):
        o_ref[0] = jnp.dot(c_ref[...].astype(bf16), w_ref[0].astype(bf16), preferred_element_type=f32) + b_ref[0]

    return pl.pallas_call(
        body, name=name, grid=(nl,),
        in_specs=[pl.BlockSpec((8, D), lambda l: (0, 0)), pl.BlockSpec((1, D, ns), lambda l: (l, 0, 0)),
                  pl.BlockSpec((1, 1, ns), lambda l: (l, 0, 0))],
        out_specs=pl.BlockSpec((1, 8, ns), lambda l: (l, 0, 0)),
        out_shape=jax.ShapeDtypeStruct((nl, 8, ns), f32), compiler_params=_cparams(1),
    )(c_all, w_ada, b_sh)


def _ada_bwd(name, c_t, dm_sh):
    nl, _, ns = dm_sh.shape

    def body(c_ref, dm_ref, o_ref):
        acc = c_ref[:, 0:1] * dm_ref[0, 0:1, :]
        for d in range(1, 8):
            acc = acc + c_ref[:, d:d + 1] * dm_ref[0, d:d + 1, :]
        o_ref[0] = acc

    return pl.pallas_call(
        body, name=name, grid=(nl,),
        in_specs=[pl.BlockSpec((D, 8), lambda l: (0, 0)), pl.BlockSpec((1, 8, ns), lambda l: (l, 0, 0))],
        out_specs=pl.BlockSpec((1, D, ns), lambda l: (l, 0, 0)),
        out_shape=jax.ShapeDtypeStruct((nl, D, ns), f32), compiler_params=_cparams(1),
    )(c_t, dm_sh)


def _sum_slots(name, a):
    n, R, C = a.shape
    tb = _pick(R, (256, 128, 64, 32, 16, 8))

    def body(a_ref, o_ref):
        acc = a_ref[0].astype(f32)
        for s in range(1, n):
            acc = acc + a_ref[s].astype(f32)
        o_ref[...] = acc

    return pl.pallas_call(
        body, name=name, grid=(R // tb,), in_specs=[pl.BlockSpec((n, tb, C), lambda i: (0, i, 0))],
        out_specs=pl.BlockSpec((tb, C), lambda i: (i, 0)), out_shape=jax.ShapeDtypeStruct((R, C), f32),
        compiler_params=_cparams(1),
    )(a)


def _chunks(rows, dtype, want):
    unit = 16 if dtype == bf16 else 8
    for n in range(want, 0, -1):
        if rows % n == 0 and (rows // n) % unit == 0:
            return n, rows // n
    return 1, rows


def _my_chip():
    return 2 * lax.axis_index("x") + lax.axis_index("y")


def _pair_sum(name, a, got, out_dtype):
    n, _, rh, C = a.shape
    _, tb = _chunks(rh, bf16, 3)

    def body(a_ref, g_ref, o_ref):
        o_ref[...] = (a_ref[0].astype(f32) + g_ref[...].astype(f32)).astype(out_dtype)

    return pl.pallas_call(
        body, name=name, grid=(n, rh // tb),
        in_specs=[pl.BlockSpec((1, 1, tb, C), lambda s, i: (s, lax.axis_index("c"), i, 0)),
                  pl.BlockSpec((1, tb, C), lambda s, i: (s, i, 0))],
        out_specs=pl.BlockSpec((1, tb, C), lambda s, i: (s, i, 0)),
        out_shape=jax.ShapeDtypeStruct((n, rh, C), out_dtype), compiler_params=_cparams(2),
    )(a, got)


def _chip_sum(name, s, land, join):
    _, rh, C = s.shape
    _, tb = _chunks(rh, bf16, 3)

    def body(s_ref, l_ref, o_ref):
        acc = s_ref[0].astype(f32)
        for j in range(l_ref.shape[0]):
            acc = acc + l_ref[j].astype(f32)
        o_ref[...] = acc.reshape(o_ref.shape)

    if join:
        out_spec = pl.BlockSpec((1, tb, C), lambda i: (lax.axis_index("c"), i, 0))
        out_shape = jax.ShapeDtypeStruct((2, rh, C), f32)
    else:
        out_spec = pl.BlockSpec((tb, C), lambda i: (i, 0))
        out_shape = jax.ShapeDtypeStruct((rh, C), f32)
    return pl.pallas_call(
        body, name=name, grid=(rh // tb,),
        in_specs=[pl.BlockSpec((1, tb, C), lambda i: (_my_chip(), i, 0)),
                  pl.BlockSpec((land.shape[0], tb, C), lambda i: (0, i, 0))],
        out_specs=out_spec, out_shape=out_shape, compiler_params=_cparams(1),
    )(s, land)


def _coords():
    return lax.axis_index("x"), lax.axis_index("y"), lax.axis_index("c")


def _flip(v, m):
    return (1 - v) if m else v


def _peer(x, y, c, mask):
    return (_flip(x, mask[0]), _flip(y, mask[1]), _flip(c, mask[2]))


def _all_gather(name, arr, masks, whole_mesh):
    nslots = 8 if whole_mesh else 4

    def slot(x, y, c):
        return 4 * x + 2 * y + c if whole_mesh else 2 * x + y

    def body(a_ref, o_ref, send_sems, recv_sems, local_sem):
        x, y, c = _coords()
        mine = pltpu.make_async_copy(a_ref, o_ref.at[slot(x, y, c)], local_sem)
        mine.start()
        copies = []
        for k, mask in enumerate(masks):
            cp = pltpu.make_async_remote_copy(
                src_ref=a_ref, dst_ref=o_ref.at[slot(x, y, c)], send_sem=send_sems.at[k], recv_sem=recv_sems.at[k],
                device_id=_peer(x, y, c, mask), device_id_type=MESH)
            cp.start()
            copies.append(cp)
        for cp in copies:
            cp.wait()
        mine.wait()

    return pl.pallas_call(
        body, name=name, in_specs=[ANY], out_specs=ANY,
        out_shape=jax.ShapeDtypeStruct((nslots,) + arr.shape, arr.dtype),
        scratch_shapes=[pltpu.SemaphoreType.DMA((len(masks),)), pltpu.SemaphoreType.DMA((len(masks),)),
                        pltpu.SemaphoreType.DMA],
    )(arr)


def _gather_weights(name, pack):
    def body(w_ref, o_ref, *sems):
        mine, first, passed = _gather_copies(w_ref, o_ref, *sems)
        _gather_start(mine, first)
        _gather_forward(first, passed)
        _gather_finish(mine, first, passed)

    return pl.pallas_call(
        body, name=name, in_specs=[ANY], out_specs=ANY, out_shape=jax.ShapeDtypeStruct((4,) + pack.shape, pack.dtype),
        scratch_shapes=_gather_sems(pack),
    )(pack)


def _gather_sems(pack):
    nch, _ = _chunks(pack.shape[0] // 2, pack.dtype, 3)
    nloc, _ = _chunks(pack.shape[0], pack.dtype, 6)
    return [pltpu.SemaphoreType.DMA((len(CHIP_MASKS), nch))] * 4 + [pltpu.SemaphoreType.DMA((nloc,))]


def _gather_copies(w_ref, o_ref, sa, ra, sb, rb, local_sems):
    R = w_ref.shape[0]
    rh = R // 2
    nch, rows = _chunks(rh, w_ref.dtype, 3)
    nloc, lrows = _chunks(R, w_ref.dtype, 6)
    x, y, c = _coords()
    chip = 2 * x + y
    mine = []
    for k in range(nloc):
        part = pl.ds(k * lrows, lrows)
        mine.append(pltpu.make_async_copy(w_ref.at[part], o_ref.at[chip, part], local_sems.at[k]))
    first, passed = [], []
    for k, mask in enumerate(CHIP_MASKS):
        px, py, _ = _peer(x, y, c, mask)
        for ch in range(nch):
            part = pl.ds(c * rh + ch * rows, rows)
            first.append(pltpu.make_async_remote_copy(
                src_ref=w_ref.at[part], dst_ref=o_ref.at[chip, part], send_sem=sa.at[k, ch], recv_sem=ra.at[k, ch],
                device_id=(px, py, c), device_id_type=MESH))
            landed = o_ref.at[2 * px + py, part]
            passed.append(pltpu.make_async_remote_copy(
                src_ref=landed, dst_ref=landed, send_sem=sb.at[k, ch], recv_sem=rb.at[k, ch],
                device_id=(x, y, 1 - c), device_id_type=MESH))
    return mine, first, passed


def _gather_start(mine, first):
    for cp in mine + first:
        cp.start()


def _gather_forward(first, passed):
    for a, b in zip(first, passed):
        a.wait_recv()
        b.start()


def _gather_finish(mine, first, passed):
    for a, b in zip(first, passed):
        a.wait_send()
        b.wait()
    for cp in mine:
        cp.wait()


def _swap_halves(name, arrs):
    na = len(arrs)
    plan = []
    for a in arrs:
        nch, rows = _chunks(a.shape[2], a.dtype, 3)
        plan.append([(s, ch * rows, rows) for s in range(a.shape[0]) for ch in range(nch)])
    ncopies = sum(len(p) for p in plan)

    def body(*refs):
        a_refs, got_refs = refs[:na], refs[na:2 * na]
        send_sems, recv_sems = refs[2 * na:]
        x, y, c = _coords()
        cps = []
        for ai in range(na):
            for (s, r0, rows) in plan[ai]:
                k = len(cps)
                cp = pltpu.make_async_remote_copy(
                    src_ref=a_refs[ai].at[s, 1 - c, pl.ds(r0, rows)], dst_ref=got_refs[ai].at[s, pl.ds(r0, rows)],
                    send_sem=send_sems.at[k], recv_sem=recv_sems.at[k], device_id=(x, y, 1 - c), device_id_type=MESH)
                cp.start()
                cps.append(cp)
        for cp in cps:
            cp.wait()

    halves = [jax.ShapeDtypeStruct((a.shape[0],) + a.shape[2:], a.dtype) for a in arrs]
    return pl.pallas_call(
        body, name=name, in_specs=[ANY] * na, out_specs=[ANY] * na, out_shape=halves,
        scratch_shapes=[pltpu.SemaphoreType.DMA((ncopies,))] * 2,
    )(*arrs)


def _scatter_chips(name, arrs):
    na = len(arrs)
    n = len(CHIP_MASKS)

    def body(*refs):
        a_refs, o_refs = refs[:na], refs[na:2 * na]
        send_sems, recv_sems = refs[2 * na:]
        x, y, c = _coords()
        cps = []
        for k in range(na):
            for j, mask in enumerate(CHIP_MASKS):
                px, py, pc = _peer(x, y, c, mask)
                cp = pltpu.make_async_remote_copy(
                    src_ref=a_refs[k].at[2 * px + py], dst_ref=o_refs[k].at[j],
                    send_sem=send_sems.at[k, j], recv_sem=recv_sems.at[k, j],
                    device_id=(px, py, pc), device_id_type=MESH)
                cp.start()
                cps.append(cp)
        for cp in cps:
            cp.wait()

    return pl.pallas_call(
        body, name=name, in_specs=[ANY] * na, out_specs=[ANY] * na,
        out_shape=[jax.ShapeDtypeStruct((n,) + a.shape[1:], a.dtype) for a in arrs],
        scratch_shapes=[pltpu.SemaphoreType.DMA((na, n)), pltpu.SemaphoreType.DMA((na, n))],
    )(*arrs)


def _join_halves(name, buf):
    _, rh, C = buf.shape
    nch, rows = _chunks(rh, buf.dtype, 6)

    def body(b_ref, o_ref, send_sems, recv_sems):
        x, y, c = _coords()
        cps = []
        for k in range(nch):
            part = o_ref.at[c, pl.ds(k * rows, rows)]
            cp = pltpu.make_async_remote_copy(
                src_ref=part, dst_ref=part, send_sem=send_sems.at[k], recv_sem=recv_sems.at[k],
                device_id=(x, y, 1 - c), device_id_type=MESH)
            cp.start()
            cps.append(cp)
        for cp in cps:
            cp.wait()

    return pl.pallas_call(
        body, name=name, in_specs=[ANY], out_specs=ANY, out_shape=jax.ShapeDtypeStruct(buf.shape, buf.dtype),
        input_output_aliases={0: 0}, scratch_shapes=[pltpu.SemaphoreType.DMA((nch,))] * 2,
    )(buf)


BIG = ("w_in", "w_uq", "w_ukv", "w_out", "w_ffn_in", "w_ffn_out")
SMALL = ("conv_w", "conv_b", "lru_wa", "lru_ba", "lru_wx", "lru_bx", "lru_a_param", "q_norm_g", "kv_norm_g",
         "final_norm_g")
SMALL_SHAPES = {"conv_w": (2, CONV, D), "conv_b": (2, D), "lru_wa": (2, NB, BW, BW), "lru_ba": (2, NB, BW),
                "lru_wx": (2, NB, BW, BW), "lru_bx": (2, NB, BW), "lru_a_param": (2, D), "q_norm_g": (2, QR),
                "kv_norm_g": (2, KVR), "final_norm_g": (D,)}
SMALL_ROWS = 576
EARLY_ROWS = 1056
LATE_ROWS = 5856
FFN_CHIP_ORDER = (0, 2, 1, 3)


def _size(shape):
    n = 1
    for s in shape:
        n *= s
    return n


def _pack_rows(parts, rows, dtype):
    flat = jnp.concatenate([p.reshape(-1).astype(dtype) for p in parts])
    pad = rows * 1024 - flat.shape[0]
    if pad:
        flat = jnp.concatenate([flat, jnp.zeros((pad,), dtype)])
    return flat.reshape(rows, 1024)


def _unpack_rows(pack, shapes):
    flat = pack.reshape(-1)
    out, off = [], 0
    for s in shapes:
        n = _size(s)
        out.append(flat[off:off + n].reshape(s))
        off += n
    return out


def _inproj_weights(w_in_s, w_uq_s, w_ukv_s):
    w_in = jnp.concatenate(w_in_s, axis=2)
    nl = w_in.shape[0]
    z64 = jnp.zeros((nl, D, 64), w_in.dtype)
    win_p = jnp.concatenate([w_in[..., :D], w_in[..., 1472:2496], w_in[..., 2496:], w_in[..., D:1472], z64], axis=2)
    w_uq = jnp.concatenate(w_uq_s, axis=1)
    wq_p = jnp.concatenate([w_uq, jnp.zeros((nl, QR, NH, DH - DN - DR), w_uq.dtype)], axis=3).reshape(nl, QR, NH * DH)
    w_ukv = jnp.concatenate(w_ukv_s, axis=3)
    wk_p = jnp.concatenate([w_ukv[..., :DN], jnp.zeros((nl, KVR, NH, DH - DN), w_ukv.dtype)], axis=3).reshape(nl, KVR, NH * DH)
    wv = w_ukv[..., DN:].reshape(nl, KVR, NH * DV)
    return dict(win=win_p, wq=wq_p, wk=wk_p, wv=wv)


def _other_weights(w_out_s, w_ffn_in_s, w_ffn_out_s):
    wfi = jnp.concatenate([w_ffn_in_s[s] for s in FFN_CHIP_ORDER], axis=2)
    return dict(wout=jnp.concatenate(w_out_s, axis=1), wfi=wfi, wfo=jnp.concatenate(w_ffn_out_s, axis=1))


def _shard_grads(g):
    dwin = g["win"]
    w_in = jnp.concatenate([dwin[:, :D], dwin[:, QKV0:QKV0 + 448], dwin[:, D:QKV0]], axis=1)
    split = lambda a, axis: jnp.split(a, 4, axis=axis)
    w_uq = g["wq"].reshape(QR, NH, DH)[..., :DN + DR]
    w_ukv = jnp.concatenate([g["wk"].reshape(KVR, NH, DH)[..., :DN], g["wv"].reshape(KVR, NH, DV)], axis=2)
    wfi_tiles = split(g["wfi"], 1)
    wfi = [wfi_tiles[FFN_CHIP_ORDER.index(s)] for s in range(4)]
    return [split(w_in, 1), split(w_uq, 0), split(w_ukv, 2), split(g["wout"], 0), wfi, split(g["wfo"], 0)]


def kernel(x, c, positions, w_ada, b_ada, w_in, conv_w, conv_b, lru_wa, lru_ba, lru_wx, lru_bx, lru_a_param, q_norm_g, kv_norm_g, w_uq, w_ukv, w_out, w_ffn_in, w_ffn_out, final_norm_g, loss_target, m_w_ada, m_b_ada, m_w_in, m_conv_w, m_conv_b, m_lru_wa, m_lru_ba, m_lru_wx, m_lru_bx, m_lru_a_param, m_q_norm_g, m_kv_norm_g, m_w_uq, m_w_ukv, m_w_out, m_w_ffn_in, m_w_ffn_out, m_final_norm_g, v_w_ada, v_b_ada, v_w_in, v_conv_w, v_conv_b, v_lru_wa, v_lru_ba, v_lru_wx, v_lru_bx, v_lru_a_param, v_q_norm_g, v_kv_norm_g, v_w_uq, v_w_ukv, v_w_out, v_w_ffn_in, v_w_ffn_out, v_final_norm_g):
    T = x.shape[1]
    tb = min(ROW_BLOCK, T)
    tq = min(ATTN_BLOCK, T)
    nq = T // tq
    mx, my, mc = _coords()
    chip = 2 * mx + my
    dev = 4 * mx + 2 * my + mc
    weights = dict(w_ada=w_ada, b_ada=b_ada, w_in=w_in, conv_w=conv_w, conv_b=conv_b, lru_wa=lru_wa, lru_ba=lru_ba,
                   lru_wx=lru_wx, lru_bx=lru_bx, lru_a_param=lru_a_param, q_norm_g=q_norm_g, kv_norm_g=kv_norm_g,
                   w_uq=w_uq, w_ukv=w_ukv, w_out=w_out, w_ffn_in=w_ffn_in, w_ffn_out=w_ffn_out,
                   final_norm_g=final_norm_g)
    mom = dict(w_ada=m_w_ada, b_ada=m_b_ada, w_in=m_w_in, conv_w=m_conv_w, conv_b=m_conv_b, lru_wa=m_lru_wa,
               lru_ba=m_lru_ba, lru_wx=m_lru_wx, lru_bx=m_lru_bx, lru_a_param=m_lru_a_param, q_norm_g=m_q_norm_g,
               kv_norm_g=m_kv_norm_g, w_uq=m_w_uq, w_ukv=m_w_ukv, w_out=m_w_out, w_ffn_in=m_w_ffn_in,
               w_ffn_out=m_w_ffn_out, final_norm_g=m_final_norm_g)
    var = dict(w_ada=v_w_ada, b_ada=v_b_ada, w_in=v_w_in, conv_w=v_conv_w, conv_b=v_conv_b, lru_wa=v_lru_wa,
               lru_ba=v_lru_ba, lru_wx=v_lru_wx, lru_bx=v_lru_bx, lru_a_param=v_lru_a_param, q_norm_g=v_q_norm_g,
               kv_norm_g=v_kv_norm_g, w_uq=v_w_uq, w_ukv=v_w_ukv, w_out=v_w_out, w_ffn_in=v_w_ffn_in,
               w_ffn_out=v_w_ffn_out, final_norm_g=v_final_norm_g)
    order = list(weights)

    ns_ada = w_ada.shape[2]
    c_all = _all_gather("gather_c", c, ALL_MASKS, True).reshape(8, D)
    b_sh = lax.dynamic_slice_in_dim(b_ada, chip * ns_ada, ns_ada, axis=1).reshape(2, 1, ns_ada)
    mod_sh = _ada_fwd("ada_fwd", c_all, w_ada, b_sh)
    mod_all = _all_gather("gather_mod", mod_sh.reshape(16, ns_ada), CHIP_MASKS, False)
    mod_mine = lax.dynamic_index_in_dim(mod_all.reshape(4, 2, 8, ns_ada), dev, axis=2, keepdims=False)
    mod = jnp.transpose(mod_mine, (1, 0, 2)).reshape(2, 6, 1, D)

    big_shapes = [weights[n].shape for n in BIG]
    big_rows = sum(_size(s) for s in big_shapes) // 1024
    early = [w_in[:1], w_uq[:1], w_ukv[:1]]
    late = [w_in[1:], w_uq[1:], w_ukv[1:], w_out, w_ffn_in, w_ffn_out]
    early_all = _gather_weights("gather_weights_early", _pack_rows(early, EARLY_ROWS, bf16))
    late_pack = _pack_rows(late, LATE_ROWS, bf16)
    per_chip = [_unpack_rows(early_all[s], [a.shape for a in early]) for s in range(4)]
    W0 = _inproj_weights(*[[per_chip[s][i] for s in range(4)] for i in range(3)])
    W = {n: [W0[n][0], None] for n in W0}
    wa_b = lru_wa.astype(bf16)
    wx_b = lru_wx.astype(bf16)

    pos = positions.reshape(T, 1)
    invf = ROPE_THETA ** (-jnp.arange(0, DR, 2, dtype=f32) / DR)
    invf_p = jnp.concatenate([jnp.zeros((DN,), f32), invf, invf, jnp.zeros((DH - DN - DR,), f32)]).reshape(1, DH)
    cosp, sap, sbp = _rope_tables("rope_tables", pos, invf_p, T, tb)

    ns_cw = conv_w.shape[2]
    cw_all = _all_gather("gather_conv_w", conv_w.reshape(2 * CONV, ns_cw), CHIP_MASKS, False)
    cw_full = jnp.transpose(cw_all, (1, 0, 2)).reshape(2, CONV, D)
    vec = lambda a, l: a[l].reshape(1, -1)

    xl = x.reshape(T, D)
    saved = []
    h1 = _norm_mod("norm_mod_0", xl, mod[0, 1], mod[0, 0], T, tb)
    for l in range(2):
        sh_m, sc_m, g_m, sh_f, sc_f, g_f = [mod[l, j] for j in range(6)]
        proj = _matmul(f"proj_{l}", h1, W["win"][l], "nn")
        u, r, ig, ya = _lru_fwd(f"lru_fwd_{l}", proj, pos, cw_full[l], vec(conv_b, l), wa_b[l], vec(lru_ba, l),
                                wx_b[l], vec(lru_bx, l), vec(lru_a_param, l), T, tb)
        qp, kp, vp, cq, ckv = _mla_prep(f"mla_prep_{l}", proj, cosp, sap, sbp, vec(q_norm_g, l), vec(kv_norm_g, l),
                                        W["wq"][l], W["wk"][l], W["wv"][l], T, tb)
        if l == 0:
            o, lse, late_all = _flash_fwd("flash_fwd_0", qp, kp, vp, T, tq, late_pack)
            per_chip = [_unpack_rows(late_all[s], [a.shape for a in late]) for s in range(4)]
            W1 = _inproj_weights(*[[per_chip[s][i] for s in range(4)] for i in range(3)])
            for n in W1:
                W[n][1] = W1[n][0]
            W.update(_other_weights(*[[per_chip[s][i] for s in range(4)] for i in range(3, 6)]))
        else:
            o, lse = _flash_fwd("flash_fwd_1", qp, kp, vp, T, tq)
        y = _gate_mix(f"gate_mix_{l}", proj, ya, o, T, tb)
        mo = _matmul(f"out_proj_{l}", y, W["wout"][l], "nn")
        x2, h2 = _resid_norm_mod(f"resid_norm_f_{l}", xl, g_m, mo, sc_f, sh_f, T, tb)
        gu, act = _ffn_in_swiglu(f"ffn_in_{l}", h2, W["wfi"][l], T)
        fo = _matmul(f"ffn_out_{l}", act, W["wfo"][l], "nn")
        saved.append(dict(x=xl, h1=h1, proj=proj, u=u, r=r, ig=ig, ya=ya, qp=qp, kp=kp, vp=vp, cq=cq, ckv=ckv, o=o,
                          lse=lse, y=y, mo=mo, x2=x2, h2=h2, gu=gu, act=act, fo=fo))
        if l == 0:
            xl, h1 = _resid_norm_mod("resid_norm_m_1", x2, g_f, fo, mod[1, 1], mod[1, 0], T, tb)

    dx, loss_part, dgfin = _final("final", saved[1]["x2"], mod[1, 5], saved[1]["fo"],
                                  final_norm_g.reshape(1, D), loss_target.reshape(T, D), T, tb)
    loss = lax.psum(loss_part[0, 0], ("x", "y", "c"))

    gl = {n: [None, None] for n in ("win", "wq", "wk", "wv", "wout", "wfi", "wfo", "conv_w", "conv_b", "lru_wa",
                                    "lru_ba", "lru_wx", "lru_bx", "lru_a_param", "q_norm_g", "kv_norm_g")}
    dmod = [None, None]
    for l in (1, 0):
        s = saved[l]
        sh_m, sc_m, g_m, sh_f, sc_f, g_f = [mod[l, j] for j in range(6)]
        dfo, dg_f = _resid_bwd(f"resid_bwd_f_{l}", dx, g_f, s["fo"], T, tb)
        dgu = _dact_swiglu_bwd(f"d_act_{l}", dfo, W["wfo"][l], s["gu"], T)
        gl["wfo"][l] = _matmul(f"dw_ffn_out_{l}", s["act"], dfo, "tn", bf16)
        dh2 = _matmul(f"d_h2_{l}", dgu, W["wfi"][l], "nt")
        gl["wfi"][l] = _matmul(f"dw_ffn_in_{l}", s["h2"], dgu, "tn", bf16)
        dx2, dsc_f, dsh_f = _norm_mod_bwd(f"norm_bwd_f_{l}", dh2, s["x2"], sc_f, dx, T, tb)
        dmo, dg_m = _resid_bwd(f"resid_bwd_m_{l}", dx2, g_m, s["mo"], T, tb)
        dy = _matmul(f"d_y_{l}", dmo, W["wout"][l], "nt")
        gl["wout"][l] = _matmul(f"dw_out_{l}", s["y"], dmo, "tn", bf16)
        dga, dgb, dya, do, dlt = _gate_mix_bwd(f"gate_mix_bwd_{l}", dy, s["proj"], s["ya"], s["o"], T, tb)
        lse_r = s["lse"].reshape(NH, nq, 1, tq)
        dl_r = jnp.transpose(dlt).reshape(NH, nq, 1, tq)
        dk, dv, dq = _flash_bwd(f"flash_bwd_{l}", s["qp"], s["kp"], s["vp"], do, lse_r, dl_r, T, tq)
        dqkv, dqp, dkp, dgq, dgkv = _mla_bwd(f"mla_bwd_{l}", dq, dk, dv, s["proj"], cosp, sap, sbp,
                                              vec(q_norm_g, l), vec(kv_norm_g, l), W["wq"][l], W["wk"][l], W["wv"][l],
                                              T, tb)
        gl["wq"][l] = _matmul(f"dw_uq_{l}", s["cq"], dqp, "tn", bf16)
        gl["wk"][l] = _matmul(f"dw_uk_{l}", s["ckv"], dkp, "tn", bf16)
        gl["wv"][l] = _matmul(f"dw_uv_{l}", s["ckv"], dv, "tn", bf16)
        dxl, dcw, dcb, dwa, dba, dwx, dbx, dsp = _lru_bwd(
            f"lru_bwd_{l}", dya, s["proj"], s["u"], s["r"], s["ig"], s["ya"], pos, cw_full[l], wa_b[l], wx_b[l],
            vec(lru_a_param, l), T, tb)
        dproj = jnp.concatenate([dxl, dga, dgb, dqkv], axis=1)
        dh1 = _matmul(f"d_h1_{l}", dproj, W["win"][l], "nt")
        gl["win"][l] = _matmul(f"dw_in_{l}", s["h1"], dproj, "tn", bf16)
        dx, dsc_m, dsh_m = _norm_mod_bwd(f"norm_bwd_m_{l}", dh1, s["x"], sc_m, dx2, T, tb)
        dmod[l] = jnp.concatenate([dsh_m, dsc_m, dg_m, dsh_f, dsc_f, dg_f], axis=1)
        gl["conv_w"][l], gl["conv_b"][l] = dcw, dcb[0]
        gl["lru_wa"][l], gl["lru_ba"][l], gl["lru_wx"][l], gl["lru_bx"][l] = dwa, dba[0], dwx, dbx[0]
        gl["lru_a_param"][l] = dsp[0]
        gl["q_norm_g"][l], gl["kv_norm_g"][l] = dgq[0], dgkv[0]
    grad_x = dx.reshape(1, T, D)
    gfull = {n: jnp.stack(gl[n]) for n in SMALL if n != "final_norm_g"}
    gfull["final_norm_g"] = dgfin[0]

    dmod_all = _all_gather("gather_dmod", jnp.concatenate(dmod, axis=1), ALL_MASKS, True).reshape(8, 2, 6, D)
    dmod_sum = _sum_slots("sum_dmod", dmod_all.reshape(8, 12, D)).reshape(2, 6 * D)
    dm_sh = lax.dynamic_slice_in_dim(jnp.transpose(dmod_all.reshape(8, 2, 6 * D), (1, 0, 2)), chip * ns_ada, ns_ada, axis=2)
    grads = {"w_ada": _ada_bwd("ada_bwd", jnp.transpose(c_all), dm_sh), "b_ada": dmod_sum}

    rh = big_rows // 2
    pieces = [_shard_grads({n: gl[n][l] for n in ("win", "wq", "wk", "wv", "wout", "wfi", "wfo")}) for l in range(2)]
    gbig = jnp.stack([_pack_rows([pieces[l][i][s] for i in range(len(BIG)) for l in range(2)], big_rows, bf16)
                      for s in range(4)]).reshape(4, 2, rh, 1024)
    rs = SMALL_ROWS // 8
    gsmall = _pack_rows([gfull[n] for n in SMALL], SMALL_ROWS, f32).reshape(4, 2, rs, 1024)
    got_b, got_s = _swap_halves("rs_swap_halves", [gbig, gsmall])
    sum_b = _pair_sum("rs_pair_sum_big", gbig, got_b, bf16)
    sum_s = _pair_sum("rs_pair_sum_small", gsmall, got_s, f32)
    land_b, land_s = _scatter_chips("rs_scatter_chips", [sum_b, sum_s])
    tot_b = _chip_sum("rs_chip_sum_big", sum_b, land_b, True)
    tot_s = _chip_sum("rs_chip_sum_small", sum_s, land_s, False)
    big_mine = _join_halves("rs_join_halves", tot_b).reshape(big_rows, 1024)
    small_all = _all_gather("gather_small_grads", tot_s, ALL_MASKS, True).reshape(SMALL_ROWS, 1024)
    for n, gpiece in zip(BIG, _unpack_rows(big_mine, big_shapes)):
        grads[n] = gpiece
    for n, gs in zip(SMALL, _unpack_rows(small_all, [SMALL_SHAPES[n] for n in SMALL])):
        grads[n] = gs
    grads["conv_w"] = lax.dynamic_slice_in_dim(grads["conv_w"], chip * ns_cw, ns_cw, axis=2)

    deltas, new_m, new_v = {}, {}, {}
    for n in order:
        w = weights[n]
        view = (1, w.shape[0]) if w.ndim == 1 else (_size(w.shape[:-1]), w.shape[-1]) if w.shape[-1] >= 128 else (w.shape[0], _size(w.shape[1:]))
        d, m2, v2 = _adamw(f"adamw_{n}", w.reshape(view), grads[n].reshape(view), mom[n].reshape(view), var[n].reshape(view))
        deltas[n], new_m[n], new_v[n] = d.reshape(w.shape), m2.reshape(w.shape), v2.reshape(w.shape)
    return (loss, grad_x, *[grads[n] for n in order], *[deltas[n] for n in order], *[new_m[n] for n in order],
            *[new_v[n] for n in order])
```

```python
import functools

import jax
import jax.numpy as jnp
from jax import lax
from jax.experimental import pallas as pl
from jax.experimental.pallas import tpu as pltpu

f32 = jnp.float32
bf16 = jnp.bfloat16

D = 1024
NH = 8
DN = 128
DR = 64
DV = 128
DH = 256
QR = 256
KVR = 128
DFF = 2816
FH = DFF // 2
NB = 8
BW = 128
CONV = 4
PW = 3584
QKV0 = 3 * D
DIN = 3520
EPS = 1e-6
LRU_C = 8.0
ROPE_THETA = 10000.0
SCALE = (DN + DR) ** -0.5
LOG2E = 1.4426950408889634
LN2 = 0.6931471805599453
QSCALE = SCALE * LOG2E
NEG = -1e30
HPS = 2

ADAM_LR = 0.001
ADAM_B1 = 0.9
ADAM_B2 = 0.999
ADAM_EPS = 1e-08
ADAM_WD = 0.01
ADAM_STEP = 10

VMEM_LIMIT = 56 * 1024 * 1024
ROW_BLOCK = 256
ATTN_BLOCK = 512
MESH = pl.DeviceIdType.MESH
ANY = pl.BlockSpec(memory_space=pl.ANY)
CHIP_MASKS = ((1, 0, 0), (0, 1, 0), (1, 1, 0))
ALL_MASKS = ((0, 0, 1), (1, 0, 0), (0, 1, 0), (1, 1, 0), (1, 0, 1), (0, 1, 1), (1, 1, 1))


def _cparams(n_axes):
    return pltpu.CompilerParams(dimension_semantics=("arbitrary",) * n_axes, vmem_limit_bytes=VMEM_LIMIT)


def _pick(n, cands):
    for c in cands:
        if n % c == 0:
            return c
    return n


def _rowwise(name, body, T, tb, row_ins, vec_ins, row_outs, acc_outs=(), scratch=(), reverse=False):
    n = T // tb

    def rmap(i):
        return (n - 1 - i) if reverse else i

    in_specs = []
    for (_, w, cb) in row_ins:
        in_specs.append(pl.BlockSpec((tb, w), functools.partial(lambda i, cb: (rmap(i), cb), cb=cb)))
    for v in vec_ins:
        in_specs.append(pl.BlockSpec(v.shape, functools.partial(lambda i, nd: (0,) * nd, nd=v.ndim)))
    out_specs, out_shape = [], []
    for (w, dt) in row_outs:
        out_specs.append(pl.BlockSpec((tb, w), lambda i: (rmap(i), 0)))
        out_shape.append(jax.ShapeDtypeStruct((T, w), dt))
    for (s, dt) in acc_outs:
        out_specs.append(pl.BlockSpec(s, functools.partial(lambda i, nd: (0,) * nd, nd=len(s))))
        out_shape.append(jax.ShapeDtypeStruct(s, dt))

    def kern(*refs):
        body(*refs)

    return pl.pallas_call(
        kern, name=name, grid=(n,), in_specs=in_specs, out_specs=out_specs, out_shape=out_shape,
        scratch_shapes=list(scratch), compiler_params=_cparams(1),
    )(*[a for (a, _, _) in row_ins], *vec_ins)


def _matmul(name, a, b, mode, out_dtype=f32):
    if mode == "nn":
        (M, K), N = a.shape, b.shape[1]
    elif mode == "nt":
        (M, K), N = a.shape, b.shape[0]
    else:
        (K, M), N = a.shape, b.shape[1]
    wide = (1792, 1408, 1024, 512, 256, 128)
    tm = _pick(M, (512, 256, 128)) if mode != "tn" else _pick(M, (1024, 1408, 512, 256, 128))
    tn = _pick(N, wide)
    tk = _pick(K, wide) if mode != "tn" else _pick(K, (512, 256, 128))
    nk = K // tk
    dims = {"nn": (((1,), (0,)), ((), ())), "nt": (((1,), (1,)), ((), ())), "tn": (((0,), (0,)), ((), ()))}[mode]
    if mode == "tn":
        a_spec = pl.BlockSpec((tk, tm), lambda j, i, k: (k, i))
    else:
        a_spec = pl.BlockSpec((tm, tk), lambda j, i, k: (i, k))
    if mode == "nt":
        b_spec = pl.BlockSpec((tn, tk), lambda j, i, k: (j, k))
    else:
        b_spec = pl.BlockSpec((tk, tn), lambda j, i, k: (k, j))
    o_spec = pl.BlockSpec((tm, tn), lambda j, i, k: (i, j))
    use_acc = nk > 1 and out_dtype != f32

    def kern(a_ref, b_ref, o_ref, *scr):
        k = pl.program_id(2)
        d = lax.dot_general(a_ref[...].astype(bf16), b_ref[...].astype(bf16), dims, preferred_element_type=f32)
        if nk == 1:
            o_ref[...] = d.astype(out_dtype)
        else:
            acc = scr[0] if use_acc else o_ref

            @pl.when(k == 0)
            def _():
                acc[...] = d

            @pl.when(k > 0)
            def _():
                acc[...] += d

            if use_acc:
                @pl.when(k == nk - 1)
                def _():
                    o_ref[...] = acc[...].astype(out_dtype)

    return pl.pallas_call(
        kern, name=name, grid=(N // tn, M // tm, nk), in_specs=[a_spec, b_spec], out_specs=o_spec,
        out_shape=jax.ShapeDtypeStruct((M, N), out_dtype),
        scratch_shapes=[pltpu.VMEM((tm, tn), f32)] if use_acc else [],
        compiler_params=_cparams(3),
    )(a, b)


def _colsum(v):
    return jnp.sum(v, axis=0, keepdims=True)


def _init_acc(step, *refs):
    @pl.when(step == 0)
    def _():
        for r in refs:
            r[...] = jnp.zeros_like(r)


def _norm_mod(name, x, sc, sh, T, tb):
    def body(x_ref, sc_ref, sh_ref, h_ref):
        xv = x_ref[...]
        rstd = lax.rsqrt(jnp.mean(xv * xv, axis=-1, keepdims=True) + EPS)
        h_ref[...] = (xv * rstd * (1.0 + sc_ref[...]) + sh_ref[...]).astype(bf16)

    return _rowwise(name, body, T, tb, [(x, D, 0)], [sc, sh], [(D, bf16)])[0]


def _resid_norm_mod(name, x, g, m, sc, sh, T, tb):
    def body(x_ref, m_ref, g_ref, sc_ref, sh_ref, x2_ref, h_ref):
        xv = x_ref[...] + g_ref[...] * m_ref[...]
        x2_ref[...] = xv
        rstd = lax.rsqrt(jnp.mean(xv * xv, axis=-1, keepdims=True) + EPS)
        h_ref[...] = (xv * rstd * (1.0 + sc_ref[...]) + sh_ref[...]).astype(bf16)

    return _rowwise(name, body, T, tb, [(x, D, 0), (m, D, 0)], [g, sc, sh], [(D, f32), (D, bf16)])


def _norm_mod_bwd(name, dh, x, sc, dres, T, tb):
    def body(dh_ref, x_ref, dr_ref, sc_ref, dx_ref, dsc_ref, dsh_ref):
        _init_acc(pl.program_id(0), dsc_ref, dsh_ref)
        xv = x_ref[...]
        dhv = dh_ref[...]
        rstd = lax.rsqrt(jnp.mean(xv * xv, axis=-1, keepdims=True) + EPS)
        nv = xv * rstd
        dn = dhv * (1.0 + sc_ref[...])
        dx_ref[...] = dr_ref[...] + rstd * (dn - nv * jnp.mean(dn * nv, axis=-1, keepdims=True))
        dsc_ref[...] += _colsum(dhv * nv)
        dsh_ref[...] += _colsum(dhv)

    return _rowwise(name, body, T, tb, [(dh, D, 0), (x, D, 0), (dres, D, 0)], [sc], [(D, f32)],
                    [((1, D), f32), ((1, D), f32)])


def _resid_bwd(name, dx, g, m, T, tb):
    def body(dx_ref, m_ref, g_ref, dm_ref, dg_ref):
        _init_acc(pl.program_id(0), dg_ref)
        dxv = dx_ref[...]
        dm_ref[...] = (dxv * g_ref[...]).astype(bf16)
        dg_ref[...] += _colsum(dxv * m_ref[...])

    return _rowwise(name, body, T, tb, [(dx, D, 0), (m, D, 0)], [g], [(D, bf16)], [((1, D), f32)])


def _expm1_neg(y):
    poly = y * (1.0 + y * (0.5 + y * (1.0 / 6 + y * (1.0 / 24 + y * (1.0 / 120 + y * (1.0 / 720 + y * (1.0 / 5040)))))))
    return jnp.where(y > -0.25, poly, jnp.exp(y) - 1.0)


def _lru_gates(r, sp, reset):
    log_a = -LRU_C * r * sp
    a = jnp.where(reset, 0.0, jnp.exp(log_a))
    mult = jnp.where(reset, 1.0, jnp.sqrt(-_expm1_neg(2.0 * log_a)))
    return a, mult


def _block_dot(v, w_ref, dims):
    outs = [lax.dot_general(v[:, n * BW:(n + 1) * BW], w_ref[n], dims, preferred_element_type=f32) for n in range(NB)]
    return jnp.concatenate(outs, axis=1)


def _lru_fwd(name, proj, pos, cw, cb, wa, ba, wx, bx, ap, T, tb):
    nsteps = tb.bit_length() - 1

    def body(x_ref, pos_ref, cw_ref, cb_ref, wa_ref, ba_ref, wx_ref, bx_ref, ap_ref,
             u_ref, r_ref, i_ref, h_ref, prev_x, carry_h):
        _init_acc(pl.program_id(0), prev_x, carry_h)
        x = x_ref[...]
        px = prev_x[...]
        row = lax.broadcasted_iota(jnp.int32, (tb, 1), 0)
        u = x * cw_ref[CONV - 1:CONV, :] + cb_ref[...]
        for j in range(1, CONV):
            z = jnp.where(row >= tb - j, px, x)
            u = u + pltpu.roll(z, j, 0) * cw_ref[CONV - 1 - j:CONV - j, :]
        ub = u.astype(bf16)
        nn = (((1,), (0,)), ((), ()))
        r = jax.nn.sigmoid(_block_dot(ub, wa_ref, nn) + ba_ref[...])
        ig = jax.nn.sigmoid(_block_dot(ub, wx_ref, nn) + bx_ref[...])
        sp = jax.nn.softplus(-ap_ref[...])
        reset = pos_ref[...] == 0
        a, mult = _lru_gates(r, sp, reset)
        b = u * ig * mult
        sub = row & 7
        for s in (1, 2, 4):
            keep = sub >= s
            a_sh = jnp.where(keep, pltpu.roll(a, s, 0), 1.0)
            b_sh = jnp.where(keep, pltpu.roll(b, s, 0), 0.0)
            b = a * b_sh + b
            a = a * a_sh
        carry = carry_h[...]
        for g in range(tb // 8):
            rows = slice(8 * g, 8 * g + 8)
            hg = b[rows] + a[rows] * carry
            h_ref[rows, :] = hg
            carry = hg[7:8]
        carry_h[...] = carry
        u_ref[...] = u
        r_ref[...] = r
        i_ref[...] = ig
        prev_x[...] = x

    return _rowwise(name, body, T, tb, [(proj, D, 0), (pos, 1, 0)], [cw, cb, wa, ba, wx, bx, ap],
                    [(D, f32)] * 4, scratch=[pltpu.VMEM((tb, D), f32), pltpu.VMEM((1, D), f32)])


def _lru_bwd(name, dh, proj, u, r, ig, h, pos, cw, wa, wx, ap, T, tb):
    n = T // tb
    nsteps = tb.bit_length() - 1
    t8 = tb // 8

    def rmap(i):
        return n - 1 - i

    def body(dh_ref, x_ref, u_ref, r_ref, i_ref, h_ref, hp_ref, pos_ref, cw_ref, wa_ref, wx_ref, ap_ref,
             dx_ref, dcw_ref, dcb_ref, dwa_ref, dba_ref, dwx_ref, dbx_ref, dsp_ref, next_du, carry_g, carry_a):
        step = pl.program_id(0)
        _init_acc(step, dcw_ref, dcb_ref, dwa_ref, dba_ref, dwx_ref, dbx_ref, dsp_ref, next_du, carry_g, carry_a)
        blk = n - 1 - step
        x = x_ref[...]
        u = u_ref[...]
        r = r_ref[...]
        ig = i_ref[...]
        h = h_ref[...]
        row = lax.broadcasted_iota(jnp.int32, (tb, 1), 0)
        sp = jax.nn.softplus(-ap_ref[...])
        reset = pos_ref[...] == 0
        a, mult = _lru_gates(r, sp, reset)
        A = pltpu.roll(jnp.where(row == 0, carry_a[...], a), tb - 1, 0)
        B = dh_ref[...]
        sub = row & 7
        for s in (1, 2, 4):
            keep = sub < 8 - s
            a_sh = jnp.where(keep, pltpu.roll(A, tb - s, 0), 1.0)
            b_sh = jnp.where(keep, pltpu.roll(B, tb - s, 0), 0.0)
            B = B + A * b_sh
            A = A * a_sh
        carry = carry_g[...]
        parts = [None] * (tb // 8)
        for grp in reversed(range(tb // 8)):
            rows = slice(8 * grp, 8 * grp + 8)
            parts[grp] = B[rows] + A[rows] * carry
            carry = parts[grp][0:1]
        g = jnp.concatenate(parts, axis=0)
        carry_g[...] = carry
        carry_a[...] = a[0:1, :]
        h_last = jnp.where(blk == 0, 0.0, hp_ref[7:8, :])
        h_prev = pltpu.roll(jnp.where(row == tb - 1, h_last, h), 1, 0)
        da = g * h_prev
        gu = g * u
        dlog_a = jnp.where(reset, 0.0, da * a - gu * ig * (a * a) / mult)
        du = g * ig * mult
        di = gu * mult
        dr = dlog_a * (-LRU_C * sp)
        dsp_ref[...] += _colsum(dlog_a * (-LRU_C * r)) * (-jax.nn.sigmoid(-ap_ref[...]))
        dzr = dr * r * (1.0 - r)
        dzi = di * ig * (1.0 - ig)
        dzr_b = dzr.astype(bf16)
        dzi_b = dzi.astype(bf16)
        nt = (((1,), (1,)), ((), ()))
        du = du + _block_dot(dzr_b, wa_ref, nt) + _block_dot(dzi_b, wx_ref, nt)
        ub = u.astype(bf16)
        tn = (((0,), (0,)), ((), ()))
        for nb in range(NB):
            sl = slice(nb * BW, (nb + 1) * BW)
            dwa_ref[nb] += lax.dot_general(ub[:, sl], dzr_b[:, sl], tn, preferred_element_type=f32)
            dwx_ref[nb] += lax.dot_general(ub[:, sl], dzi_b[:, sl], tn, preferred_element_type=f32)
        dba_ref[...] += _colsum(dzr)
        dbx_ref[...] += _colsum(dzi)
        dcb_ref[...] += _colsum(du)
        ndu = next_du[...]
        dx = du * cw_ref[CONV - 1:CONV, :]
        dcw_ref[CONV - 1:CONV, :] += _colsum(x * du)
        for j in range(1, CONV):
            up = pltpu.roll(jnp.where(row < j, ndu, du), tb - j, 0)
            dx = dx + up * cw_ref[CONV - 1 - j:CONV - j, :]
            dcw_ref[CONV - 1 - j:CONV - j, :] += _colsum(x * up)
        dx_ref[...] = dx.astype(bf16)
        next_du[...] = du

    row_specs = [pl.BlockSpec((tb, D), lambda i: (rmap(i), 0)) for _ in range(6)]
    in_specs = row_specs + [
        pl.BlockSpec((8, D), lambda i: (jnp.maximum(rmap(i) * t8 - 1, 0), 0)),
        pl.BlockSpec((tb, 1), lambda i: (rmap(i), 0)),
        pl.BlockSpec((CONV, D), lambda i: (0, 0)),
        pl.BlockSpec((NB, BW, BW), lambda i: (0, 0, 0)),
        pl.BlockSpec((NB, BW, BW), lambda i: (0, 0, 0)),
        pl.BlockSpec((1, D), lambda i: (0, 0)),
    ]
    vec = lambda s: pl.BlockSpec(s, functools.partial(lambda i, nd: (0,) * nd, nd=len(s)))
    acc_shapes = [(CONV, D), (1, D), (NB, BW, BW), (1, D), (NB, BW, BW), (1, D), (1, D)]
    return pl.pallas_call(
        body, name=name, grid=(n,), in_specs=in_specs,
        out_specs=[pl.BlockSpec((tb, D), lambda i: (rmap(i), 0))] + [vec(s) for s in acc_shapes],
        out_shape=[jax.ShapeDtypeStruct((T, D), bf16)] + [jax.ShapeDtypeStruct(s, f32) for s in acc_shapes],
        scratch_shapes=[pltpu.VMEM((tb, D), f32), pltpu.VMEM((1, D), f32), pltpu.VMEM((1, D), f32)],
        compiler_params=_cparams(1),
    )(dh, proj, u, r, ig, h, h, pos, cw, wa, wx, ap)


def _rope_tables(name, pos, invf, T, tb):
    def body(pos_ref, invf_ref, cos_ref, sa_ref, sb_ref):
        ang = pos_ref[...].astype(f32) * invf_ref[...]
        lane = lax.broadcasted_iota(jnp.int32, (tb, DH), 1)
        first = (lane >= DN) & (lane < DN + DR // 2)
        second = (lane >= DN + DR // 2) & (lane < DN + DR)
        c = jnp.cos(ang)
        s = jnp.sin(ang)
        cos_ref[...] = jnp.where(lane < DN, 1.0, jnp.where(first | second, c, 0.0))
        sa_ref[...] = jnp.where(first, -s, 0.0)
        sb_ref[...] = jnp.where(second, s, 0.0)

    return _rowwise(name, body, T, tb, [(pos, 1, 0)], [invf], [(DH, f32)] * 3)


def _rope(z, cos8, sa8, sb8):
    w = z.shape[1]
    return z * cos8 + pltpu.roll(z, w - DR // 2, 1) * sa8 + pltpu.roll(z, DR // 2, 1) * sb8


def _rope_t(d, cos8, sa8, sb8):
    w = d.shape[1]
    return d * cos8 + pltpu.roll(d * sa8, DR // 2, 1) + pltpu.roll(d * sb8, w - DR // 2, 1)


def _mla_prep(name, proj, cosp, sap, sbp, gq, gkv, wq, wk, wv, T, tb):
    def body(qkv_ref, cos_ref, sa_ref, sb_ref, gq_ref, gkv_ref, wq_ref, wk_ref, wv_ref,
             q_ref, k_ref, v_ref, cq_ref, ckv_ref):
        qkv = qkv_ref[...]
        qd = qkv[:, :QR]
        kvd = qkv[:, QR:QR + KVR]
        slab = qkv[:, QR + KVR:]
        cq = (qd * lax.rsqrt(jnp.mean(qd * qd, axis=-1, keepdims=True) + EPS) * gq_ref[...]).astype(bf16)
        ckv = (kvd * lax.rsqrt(jnp.mean(kvd * kvd, axis=-1, keepdims=True) + EPS) * gkv_ref[...]).astype(bf16)
        cos8 = jnp.tile(cos_ref[...], (1, NH))
        sa8 = jnp.tile(sa_ref[...], (1, NH))
        sb8 = jnp.tile(sb_ref[...], (1, NH))
        qpre = jnp.dot(cq, wq_ref[...], preferred_element_type=f32)
        shared = jnp.concatenate([jnp.zeros((tb, DN), f32), slab], axis=1)
        kpre = jnp.dot(ckv, wk_ref[...], preferred_element_type=f32) + jnp.tile(shared, (1, NH))
        q_ref[...] = (_rope(qpre, cos8, sa8, sb8) * QSCALE).astype(bf16)
        k_ref[...] = _rope(kpre, cos8, sa8, sb8).astype(bf16)
        v_ref[...] = jnp.dot(ckv, wv_ref[...], preferred_element_type=f32).astype(bf16)
        cq_ref[...] = cq
        ckv_ref[...] = ckv

    return _rowwise(name, body, T, tb,
                    [(proj, 512, QKV0 // 512), (cosp, DH, 0), (sap, DH, 0), (sbp, DH, 0)],
                    [gq, gkv, wq, wk, wv],
                    [(NH * DH, bf16), (NH * DH, bf16), (NH * DV, bf16), (QR, bf16), (KVR, bf16)])


def _mla_bwd(name, dq, dk, dv, proj, cosp, sap, sbp, gq, gkv, wq, wk, wv, T, tb):
    def body(dq_ref, dk_ref, dv_ref, qkv_ref, cos_ref, sa_ref, sb_ref, gq_ref, gkv_ref, wq_ref, wk_ref, wv_ref,
             dqkv_ref, dqp_ref, dkp_ref, dgq_ref, dgkv_ref):
        _init_acc(pl.program_id(0), dgq_ref, dgkv_ref)
        cos8 = jnp.tile(cos_ref[...], (1, NH))
        sa8 = jnp.tile(sa_ref[...], (1, NH))
        sb8 = jnp.tile(sb_ref[...], (1, NH))
        dqp = _rope_t(dq_ref[...] * SCALE, cos8, sa8, sb8)
        dkp = _rope_t(dk_ref[...], cos8, sa8, sb8)
        dqp_b = dqp.astype(bf16)
        dkp_b = dkp.astype(bf16)
        dqp_ref[...] = dqp_b
        dkp_ref[...] = dkp_b
        nt = (((1,), (1,)), ((), ()))
        dcq = lax.dot_general(dqp_b, wq_ref[...], nt, preferred_element_type=f32)
        dckv = (lax.dot_general(dkp_b, wk_ref[...], nt, preferred_element_type=f32)
                + lax.dot_general(dv_ref[...], wv_ref[...], nt, preferred_element_type=f32))
        dshared = dkp[:, 0:DH]
        for hd in range(1, NH):
            dshared = dshared + dkp[:, hd * DH:(hd + 1) * DH]
        qkv = qkv_ref[...]
        qd = qkv[:, :QR]
        kvd = qkv[:, QR:QR + KVR]
        nq = qd * lax.rsqrt(jnp.mean(qd * qd, axis=-1, keepdims=True) + EPS)
        nkv = kvd * lax.rsqrt(jnp.mean(kvd * kvd, axis=-1, keepdims=True) + EPS)
        dgq_ref[...] += _colsum(dcq * nq)
        dgkv_ref[...] += _colsum(dckv * nkv)
        dnq = dcq * gq_ref[...]
        dnkv = dckv * gkv_ref[...]
        rq = lax.rsqrt(jnp.mean(qd * qd, axis=-1, keepdims=True) + EPS)
        rkv = lax.rsqrt(jnp.mean(kvd * kvd, axis=-1, keepdims=True) + EPS)
        dqd = rq * (dnq - nq * jnp.mean(dnq * nq, axis=-1, keepdims=True))
        dkvd = rkv * (dnkv - nkv * jnp.mean(dnkv * nkv, axis=-1, keepdims=True))
        dqkv_ref[...] = jnp.concatenate([dqd, dkvd, dshared[:, DN:]], axis=1).astype(bf16)

    return _rowwise(name, body, T, tb,
                    [(dq, NH * DH, 0), (dk, NH * DH, 0), (dv, NH * DV, 0), (proj, 512, QKV0 // 512),
                     (cosp, DH, 0), (sap, DH, 0), (sbp, DH, 0)],
                    [gq, gkv, wq, wk, wv],
                    [(512, bf16), (NH * DH, bf16), (NH * DH, bf16)],
                    [((1, QR), f32), ((1, KVR), f32)])


def _flash_fwd(name, q, k, v, T, tq, pack=None):
    nq = T // tq
    unroll = 2

    def body(q_ref, k_ref, v_ref, o_ref, lse_ref):
        i = pl.program_id(1)
        nt = (((1,), (1,)), ((), ()))
        qs = [q_ref[:, hh * DH:(hh + 1) * DH] for hh in range(HPS)]

        def one(hh, j, carry, masked):
            m, l, acc = carry
            off = pl.multiple_of(j * tq, tq)
            kj = k_ref[pl.ds(off, tq), hh * DH:(hh + 1) * DH]
            vj = v_ref[pl.ds(off, tq), hh * DV:(hh + 1) * DV]
            s = lax.dot_general(qs[hh], kj, nt, preferred_element_type=f32)
            if masked:
                rr = lax.broadcasted_iota(jnp.int32, (tq, tq), 0)
                cc = lax.broadcasted_iota(jnp.int32, (tq, tq), 1)
                s = jnp.where(cc <= rr, s, NEG)
            m_new = jnp.maximum(m, jnp.max(s, axis=-1, keepdims=True))
            alpha = jnp.exp2(m - m_new)
            p = jnp.exp2(s - m_new)
            l = alpha * l + jnp.sum(p, axis=-1, keepdims=True)
            acc = alpha * acc + jnp.dot(p.astype(bf16), vj, preferred_element_type=f32)
            return m_new, l, acc

        def step(j, carries, masked):
            return tuple(one(hh, j, carries[hh], masked) for hh in range(HPS))

        def pair(jj, carries):
            for u in range(unroll):
                carries = step(jj * unroll + u, carries, False)
            return carries

        init1 = (jnp.full((tq, 1), NEG, f32), jnp.zeros((tq, 1), f32), jnp.zeros((tq, DV), f32))
        carry = lax.fori_loop(0, i // unroll, pair, tuple(init1 for _ in range(HPS)))
        carry = lax.fori_loop((i // unroll) * unroll, i, lambda j, c: step(j, c, False), carry)
        carry = step(i, carry, True)
        for hh in range(HPS):
            m, l, acc = carry[hh]
            o_ref[:, hh * DV:(hh + 1) * DV] = acc / l
            lse_ref[hh] = m + jnp.log2(l)

    def body_with_gather(q_ref, k_ref, v_ref, *rest):
        nw = len(pack)
        w_refs, (o_ref, lse_ref), wall_refs, sems = rest[:nw], rest[nw:nw + 2], rest[nw + 2:2 * nw + 2], rest[2 * nw + 2:]
        h = pl.program_id(0)
        i = pl.program_id(1)

        @pl.when((h == 0) & (i == 0))
        def _():
            mine, first, _ = _gather_copies(w_refs, wall_refs, *sems)
            _gather_start(mine, first)

        @pl.when((h == ngrp // 2) & (i == 0))
        def _():
            _, first, passed = _gather_copies(w_refs, wall_refs, *sems)
            _gather_forward(first, passed)

        body(q_ref, k_ref, v_ref, o_ref, lse_ref)

        @pl.when((h == ngrp - 1) & (i == nq - 1))
        def _():
            _gather_finish(*_gather_copies(w_refs, wall_refs, *sems))

    ngrp = NH // HPS
    in_specs = [pl.BlockSpec((tq, HPS * DH), lambda h, i: (i, h)),
                pl.BlockSpec((T, HPS * DH), lambda h, i: (0, h), pipeline_mode=pl.Buffered(1)),
                pl.BlockSpec((T, HPS * DV), lambda h, i: (0, h), pipeline_mode=pl.Buffered(1))]
    out_specs = [pl.BlockSpec((tq, HPS * DV), lambda h, i: (i, h)), pl.BlockSpec((HPS, tq, 1), lambda h, i: (h, i, 0))]
    out_shape = [jax.ShapeDtypeStruct((T, NH * DV), f32), jax.ShapeDtypeStruct((NH, T, 1), f32)]
    if pack is None:
        return pl.pallas_call(body, name=name, grid=(ngrp, nq), in_specs=in_specs, out_specs=out_specs,
                              out_shape=out_shape, compiler_params=_cparams(2))(q, k, v)
    return pl.pallas_call(
        body_with_gather, name=name, grid=(ngrp, nq), in_specs=in_specs + [ANY] * len(pack),
        out_specs=out_specs + [ANY] * len(pack),
        out_shape=out_shape + [jax.ShapeDtypeStruct((4,) + a.shape, a.dtype) for a in pack],
        scratch_shapes=_gather_sems(pack), compiler_params=_cparams(2),
    )(q, k, v, *pack)


def _flash_bwd(name, q, k, v, do, lse, dl, T, tq):
    nq = T // tq

    def body(k_ref, v_ref, q_ref, do_ref, lse_ref, dl_ref, dk_ref, dv_ref, dq_ref):
        j = pl.program_id(1)

        @pl.when(j == 0)
        def _():
            dq_ref[...] = jnp.zeros_like(dq_ref)

        nt = (((1,), (1,)), ((), ()))
        tn = (((0,), (0,)), ((), ()))
        ks = [k_ref[:, hh * DH:(hh + 1) * DH] for hh in range(HPS)]
        vs = [v_ref[:, hh * DV:(hh + 1) * DV] for hh in range(HPS)]

        def one(hh, i, carry, masked):
            dk, dv = carry
            off = pl.multiple_of(i * tq, tq)
            qi = q_ref[pl.ds(off, tq), hh * DH:(hh + 1) * DH]
            doi = do_ref[pl.ds(off, tq), hh * DV:(hh + 1) * DV]
            st = lax.dot_general(ks[hh], qi, nt, preferred_element_type=f32)
            if masked:
                kr = lax.broadcasted_iota(jnp.int32, (tq, tq), 0)
                qc = lax.broadcasted_iota(jnp.int32, (tq, tq), 1)
                st = jnp.where(qc >= kr, st, NEG)
            pt = jnp.exp2(st - lse_ref[hh, i])
            dv = dv + jnp.dot(pt.astype(bf16), doi, preferred_element_type=f32)
            dpt = lax.dot_general(vs[hh], doi, nt, preferred_element_type=f32)
            dst = (pt * (dpt - dl_ref[hh, i])).astype(bf16)
            dk = dk + jnp.dot(dst, qi, preferred_element_type=f32)
            dq_ref[pl.ds(off, tq), hh * DH:(hh + 1) * DH] += lax.dot_general(dst, ks[hh], tn, preferred_element_type=f32)
            return dk, dv

        def step(i, carries, masked):
            return tuple(one(hh, i, carries[hh], masked) for hh in range(HPS))

        zero = tuple((jnp.zeros((tq, DH), f32), jnp.zeros((tq, DV), f32)) for _ in range(HPS))
        carry = step(j, zero, True)
        carry = lax.fori_loop(j + 1, nq, lambda i, c: step(i, c, False), carry)
        for hh in range(HPS):
            dk, dv = carry[hh]
            dk_ref[:, hh * DH:(hh + 1) * DH] = dk * LN2
            dv_ref[:, hh * DV:(hh + 1) * DV] = dv.astype(bf16)

    return pl.pallas_call(
        body, name=name, grid=(NH // HPS, nq),
        in_specs=[pl.BlockSpec((tq, HPS * DH), lambda h, j: (j, h)),
                  pl.BlockSpec((tq, HPS * DV), lambda h, j: (j, h)),
                  pl.BlockSpec((T, HPS * DH), lambda h, j: (0, h), pipeline_mode=pl.Buffered(1)),
                  pl.BlockSpec((T, HPS * DV), lambda h, j: (0, h), pipeline_mode=pl.Buffered(1)),
                  pl.BlockSpec((HPS, nq, 1, tq), lambda h, j: (h, 0, 0, 0)),
                  pl.BlockSpec((HPS, nq, 1, tq), lambda h, j: (h, 0, 0, 0))],
        out_specs=[pl.BlockSpec((tq, HPS * DH), lambda h, j: (j, h)),
                   pl.BlockSpec((tq, HPS * DV), lambda h, j: (j, h)),
                   pl.BlockSpec((T, HPS * DH), lambda h, j: (0, h), pipeline_mode=pl.Buffered(1))],
        out_shape=[jax.ShapeDtypeStruct((T, NH * DH), f32), jax.ShapeDtypeStruct((T, NH * DV), bf16),
                   jax.ShapeDtypeStruct((T, NH * DH), f32)],
        compiler_params=_cparams(2),
    )(k, v, q, do, lse, dl)


def _gate_mix(name, proj, ya, o, T, tb):
    def body(ga_ref, gb_ref, ya_ref, o_ref, y_ref):
        y_ref[...] = (jax.nn.sigmoid(ga_ref[...]) * ya_ref[...] + jax.nn.sigmoid(gb_ref[...]) * o_ref[...]).astype(bf16)

    return _rowwise(name, body, T, tb, [(proj, D, 1), (proj, D, 2), (ya, D, 0), (o, D, 0)], [], [(D, bf16)])[0]


def _gate_mix_bwd(name, dy, proj, ya, o, T, tb):
    def body(dy_ref, ga_ref, gb_ref, ya_ref, o_ref, dga_ref, dgb_ref, dya_ref, do_ref, dl_ref):
        dy = dy_ref[...]
        sa = jax.nn.sigmoid(ga_ref[...])
        sb = jax.nn.sigmoid(gb_ref[...])
        ov = o_ref[...]
        dga_ref[...] = (dy * ya_ref[...] * sa * (1.0 - sa)).astype(bf16)
        dgb_ref[...] = (dy * ov * sb * (1.0 - sb)).astype(bf16)
        dya_ref[...] = dy * sa
        do = dy * sb
        do_ref[...] = do.astype(bf16)
        prod = do * ov
        for hd in range(NH):
            dl_ref[:, hd:hd + 1] = jnp.sum(prod[:, hd * DV:(hd + 1) * DV], axis=-1, keepdims=True)

    return _rowwise(name, body, T, tb, [(dy, D, 0), (proj, D, 1), (proj, D, 2), (ya, D, 0), (o, D, 0)], [],
                    [(D, bf16), (D, bf16), (D, f32), (D, bf16), (NH, f32)])


def _ffn_in_swiglu(name, h, w, T):
    tm = _pick(T, (512, 256, 128))

    def body(h_ref, w_ref, z_ref, a_ref):
        z = jnp.dot(h_ref[...], w_ref[...], preferred_element_type=f32)
        g = z[:, :FH]
        z_ref[...] = z.astype(bf16)
        a_ref[...] = (g * jax.nn.sigmoid(g) * z[:, FH:]).astype(bf16)

    return pl.pallas_call(
        body, name=name, grid=(2, T // tm),
        in_specs=[pl.BlockSpec((tm, D), lambda j, i: (i, 0)), pl.BlockSpec((D, 2 * FH), lambda j, i: (0, j))],
        out_specs=[pl.BlockSpec((tm, 2 * FH), lambda j, i: (i, j)), pl.BlockSpec((tm, FH), lambda j, i: (i, j))],
        out_shape=[jax.ShapeDtypeStruct((T, 2 * DFF), bf16), jax.ShapeDtypeStruct((T, DFF), bf16)],
        compiler_params=_cparams(2),
    )(h, w)


def _dact_swiglu_bwd(name, dfo, w, z, T):
    tm = _pick(T, (512, 256, 128))

    def body(d_ref, w_ref, z_ref, o_ref):
        da = lax.dot_general(d_ref[...], w_ref[...], (((1,), (1,)), ((), ())), preferred_element_type=f32)
        g = z_ref[:, :FH].astype(f32)
        u = z_ref[:, FH:].astype(f32)
        sg = jax.nn.sigmoid(g)
        o_ref[:, :FH] = (da * u * sg * (1.0 + g * (1.0 - sg))).astype(bf16)
        o_ref[:, FH:] = (da * g * sg).astype(bf16)

    return pl.pallas_call(
        body, name=name, grid=(2, T // tm),
        in_specs=[pl.BlockSpec((tm, D), lambda j, i: (i, 0)), pl.BlockSpec((FH, D), lambda j, i: (j, 0)),
                  pl.BlockSpec((tm, 2 * FH), lambda j, i: (i, j))],
        out_specs=pl.BlockSpec((tm, 2 * FH), lambda j, i: (i, j)),
        out_shape=jax.ShapeDtypeStruct((T, 2 * DFF), bf16), compiler_params=_cparams(2),
    )(dfo, w, z)


def _final(name, x, g, m, gfin, tgt, T, tb):
    def body(x_ref, m_ref, t_ref, g_ref, gf_ref, dx_ref, loss_ref, dgf_ref):
        _init_acc(pl.program_id(0), loss_ref, dgf_ref)
        xv = x_ref[...] + g_ref[...] * m_ref[...]
        rstd = lax.rsqrt(jnp.mean(xv * xv, axis=-1, keepdims=True) + EPS)
        nv = xv * rstd
        err = nv * gf_ref[...] - t_ref[...]
        loss_ref[...] += 0.5 * jnp.sum(jnp.mean(err * err, axis=-1, keepdims=True), axis=0, keepdims=True)
        dy = err * (1.0 / D)
        dgf_ref[...] += _colsum(dy * nv)
        dn = dy * gf_ref[...]
        dx_ref[...] = rstd * (dn - nv * jnp.mean(dn * nv, axis=-1, keepdims=True))

    return _rowwise(name, body, T, tb, [(x, D, 0), (m, D, 0), (tgt, D, 0)], [g, gfin], [(D, f32)],
                    [((1, 1), f32), ((1, D), f32)])


def _adamw(name, w, g, m, v):
    R, C = w.shape
    tb = _pick(R, (256, 128, 64, 32, 16, 8))
    c1 = 1.0 - ADAM_B1 ** ADAM_STEP
    c2 = 1.0 - ADAM_B2 ** ADAM_STEP

    def body(w_ref, g_ref, m_ref, v_ref, d_ref, m2_ref, v2_ref):
        gv = g_ref[...]
        m2 = ADAM_B1 * m_ref[...] + (1.0 - ADAM_B1) * gv
        v2 = ADAM_B2 * v_ref[...] + (1.0 - ADAM_B2) * (gv * gv)
        m2_ref[...] = m2
        v2_ref[...] = v2
        d_ref[...] = -ADAM_LR * ((m2 / c1) / (jnp.sqrt(v2 / c2) + ADAM_EPS) + ADAM_WD * w_ref[...])

    return _rowwise(name, body, R, tb, [(w, C, 0), (g, C, 0), (m, C, 0), (v, C, 0)], [], [(C, f32)] * 3)


def _ada_fwd(name, c_all, w_ada, b_sh):
    nl, _, ns = w_ada.shape

    def body(c_ref, w_ref, b_ref, o_ref):
        o_ref[0] = jnp.dot(c_ref[...].astype(bf16), w_ref[0].astype(bf16), preferred_element_type=f32) + b_ref[0]

    return pl.pallas_call(
        body, name=name, grid=(nl,),
        in_specs=[pl.BlockSpec((8, D), lambda l: (0, 0)), pl.BlockSpec((1, D, ns), lambda l: (l, 0, 0)),
                  pl.BlockSpec((1, 1, ns), lambda l: (l, 0, 0))],
        out_specs=pl.BlockSpec((1, 8, ns), lambda l: (l, 0, 0)),
        out_shape=jax.ShapeDtypeStruct((nl, 8, ns), f32), compiler_params=_cparams(1),
    )(c_all, w_ada, b_sh)


def _ada_bwd(name, c_t, dm_sh):
    nl, _, ns = dm_sh.shape

    def body(c_ref, dm_ref, o_ref):
        acc = c_ref[:, 0:1] * dm_ref[0, 0:1, :]
        for d in range(1, 8):
            acc = acc + c_ref[:, d:d + 1] * dm_ref[0, d:d + 1, :]
        o_ref[0] = acc

    return pl.pallas_call(
        body, name=name, grid=(nl,),
        in_specs=[pl.BlockSpec((D, 8), lambda l: (0, 0)), pl.BlockSpec((1, 8, ns), lambda l: (l, 0, 0))],
        out_specs=pl.BlockSpec((1, D, ns), lambda l: (l, 0, 0)),
        out_shape=jax.ShapeDtypeStruct((nl, D, ns), f32), compiler_params=_cparams(1),
    )(c_t, dm_sh)


def _sum_slots(name, a):
    n, R, C = a.shape
    tb = _pick(R, (256, 128, 64, 32, 16, 8))

    def body(a_ref, o_ref):
        acc = a_ref[0].astype(f32)
        for s in range(1, n):
            acc = acc + a_ref[s].astype(f32)
        o_ref[...] = acc

    return pl.pallas_call(
        body, name=name, grid=(R // tb,), in_specs=[pl.BlockSpec((n, tb, C), lambda i: (0, i, 0))],
        out_specs=pl.BlockSpec((tb, C), lambda i: (i, 0)), out_shape=jax.ShapeDtypeStruct((R, C), f32),
        compiler_params=_cparams(1),
    )(a)


def _chunks(rows, dtype, want):
    unit = 16 if dtype == bf16 else 8
    for n in range(want, 0, -1):
        if rows % n == 0 and (rows // n) % unit == 0:
            return n, rows // n
    return 1, rows


def _my_chip():
    return 2 * lax.axis_index("x") + lax.axis_index("y")


def _pair_sum(name, arrs, gots):
    na = len(arrs)

    def body(*refs):
        for k in range(na):
            a_ref, g_ref, o_ref = refs[k], refs[na + k], refs[2 * na + k]
            o_ref[...] = (a_ref[...].astype(f32) + g_ref[...].astype(f32)).astype(o_ref.dtype)

    in_specs = [pl.BlockSpec((1, a.shape[1] // 4, a.shape[2]), lambda s, i: (s, 2 * lax.axis_index("c") + i, 0))
                for a in arrs]
    half_specs = [pl.BlockSpec((1, g.shape[1] // 2, g.shape[2]), lambda s, i: (s, i, 0)) for g in gots]
    return pl.pallas_call(
        body, name=name, grid=(4, 2), in_specs=in_specs + half_specs, out_specs=half_specs,
        out_shape=[jax.ShapeDtypeStruct(g.shape, a.dtype) for a, g in zip(arrs, gots)], compiler_params=_cparams(2),
    )(*arrs, *gots)


def _chip_sum(name, sums, lands, layer, bufs=None):
    na = len(sums)

    def body(*refs):
        for k in range(na):
            s_ref, l_ref, o_ref = refs[k], refs[na + k], refs[len(refs) - na + k]
            acc = s_ref[0].astype(f32)
            for j in range(l_ref.shape[0]):
                acc = acc + l_ref[j].astype(f32)
            o_ref[...] = acc.reshape(o_ref.shape)

    in_specs = [pl.BlockSpec((1, s.shape[1] // 2, s.shape[2]), lambda i: (_my_chip(), i, 0)) for s in sums]
    in_specs += [pl.BlockSpec((3, l.shape[1] // 2, l.shape[2]), lambda i: (0, i, 0)) for l in lands]
    if layer is None:
        out_specs = [pl.BlockSpec((s.shape[1] // 2, s.shape[2]), lambda i: (i, 0)) for s in sums]
        out_shape = [jax.ShapeDtypeStruct(s.shape[1:], f32) for s in sums]
    else:
        out_specs = [pl.BlockSpec((1, 1, s.shape[1] // 2, s.shape[2]), lambda i: (layer, lax.axis_index("c"), i, 0))
                     for s in sums]
        out_shape = [jax.ShapeDtypeStruct((2, 2) + s.shape[1:], f32) for s in sums]
    aliases = {}
    operands = list(sums) + list(lands)
    if bufs is not None:
        in_specs += [ANY] * na
        aliases = {2 * na + k: k for k in range(na)}
        operands += list(bufs)
    return pl.pallas_call(
        body, name=name, grid=(2,), in_specs=in_specs, out_specs=out_specs, out_shape=out_shape,
        input_output_aliases=aliases, compiler_params=_cparams(1),
    )(*operands)


def _coords():
    return lax.axis_index("x"), lax.axis_index("y"), lax.axis_index("c")


def _flip(v, m):
    return (1 - v) if m else v


def _peer(x, y, c, mask):
    return (_flip(x, mask[0]), _flip(y, mask[1]), _flip(c, mask[2]))


def _all_gather(name, arr, masks, whole_mesh):
    nslots = 8 if whole_mesh else 4

    def slot(x, y, c):
        return 4 * x + 2 * y + c if whole_mesh else 2 * x + y

    def body(a_ref, o_ref, send_sems, recv_sems, local_sem):
        x, y, c = _coords()
        mine = pltpu.make_async_copy(a_ref, o_ref.at[slot(x, y, c)], local_sem)
        mine.start()
        copies = []
        for k, mask in enumerate(masks):
            cp = pltpu.make_async_remote_copy(
                src_ref=a_ref, dst_ref=o_ref.at[slot(x, y, c)], send_sem=send_sems.at[k], recv_sem=recv_sems.at[k],
                device_id=_peer(x, y, c, mask), device_id_type=MESH)
            cp.start()
            copies.append(cp)
        for cp in copies:
            cp.wait()
        mine.wait()

    return pl.pallas_call(
        body, name=name, in_specs=[ANY], out_specs=ANY,
        out_shape=jax.ShapeDtypeStruct((nslots,) + arr.shape, arr.dtype),
        scratch_shapes=[pltpu.SemaphoreType.DMA((len(masks),)), pltpu.SemaphoreType.DMA((len(masks),)),
                        pltpu.SemaphoreType.DMA],
    )(arr)


def _gather_weights(name, arrs):
    na = len(arrs)

    def body(*refs):
        mine, first, passed = _gather_copies(refs[:na], refs[na:2 * na], *refs[2 * na:])
        _gather_start(mine, first)
        _gather_forward(first, passed)
        _gather_finish(mine, first, passed)

    return pl.pallas_call(
        body, name=name, in_specs=[ANY] * na, out_specs=[ANY] * na,
        out_shape=[jax.ShapeDtypeStruct((4,) + a.shape, a.dtype) for a in arrs], scratch_shapes=_gather_sems(arrs),
    )(*arrs)


def _gather_plan(shape, dtype):
    big = (shape[0] // 2) * shape[1] * 2 >= (2 << 20)
    return _chunks(shape[0] // 2, dtype, 3 if big else 1), _chunks(shape[0], dtype, 4 if big else 1)


def _gather_sems(arrs):
    plans = [_gather_plan(a.shape, a.dtype) for a in arrs]
    nremote = sum(len(CHIP_MASKS) * p[0][0] for p in plans)
    nlocal = sum(p[1][0] for p in plans)
    return [pltpu.SemaphoreType.DMA((nremote,))] * 4 + [pltpu.SemaphoreType.DMA((nlocal,))]


def _gather_copies(w_refs, o_refs, sa, ra, sb, rb, local_sems):
    x, y, c = _coords()
    chip = 2 * x + y
    mine, first, passed = [], [], []
    for w_ref, o_ref in zip(w_refs, o_refs):
        (nch, rows), (nloc, lrows) = _gather_plan(w_ref.shape, w_ref.dtype)
        rh = w_ref.shape[0] // 2
        for k in range(nloc):
            part = pl.ds(k * lrows, lrows)
            mine.append(pltpu.make_async_copy(w_ref.at[part], o_ref.at[chip, part], local_sems.at[len(mine)]))
        for mask in CHIP_MASKS:
            px, py, _ = _peer(x, y, c, mask)
            for ch in range(nch):
                k = len(first)
                part = pl.ds(c * rh + ch * rows, rows)
                first.append(pltpu.make_async_remote_copy(
                    src_ref=w_ref.at[part], dst_ref=o_ref.at[chip, part], send_sem=sa.at[k], recv_sem=ra.at[k],
                    device_id=(px, py, c), device_id_type=MESH))
                landed = o_ref.at[2 * px + py, part]
                passed.append(pltpu.make_async_remote_copy(
                    src_ref=landed, dst_ref=landed, send_sem=sb.at[k], recv_sem=rb.at[k],
                    device_id=(x, y, 1 - c), device_id_type=MESH))
    return mine, first, passed


def _gather_start(mine, first):
    for cp in mine + first:
        cp.start()


def _gather_forward(first, passed):
    for a, b in zip(first, passed):
        a.wait_recv()
        b.start()


def _gather_finish(mine, first, passed):
    for a, b in zip(first, passed):
        a.wait_send()
        b.wait()
    for cp in mine:
        cp.wait()


def _swap_halves(name, arrs):
    na = len(arrs)
    plan = []
    for a in arrs:
        h = a.shape[1] // 2
        nch, rows = _chunks(h, a.dtype, 2 if h * a.shape[2] * 2 >= (1 << 20) else 1)
        plan.append([(s, ch * rows, rows) for s in range(a.shape[0]) for ch in range(nch)])
    ncopies = sum(len(p) for p in plan)

    def body(*refs):
        a_refs, got_refs = refs[:na], refs[na:2 * na]
        send_sems, recv_sems = refs[2 * na:]
        x, y, c = _coords()
        cps = []
        for ai in range(na):
            h = arrs[ai].shape[1] // 2
            for (s, r0, rows) in plan[ai]:
                k = len(cps)
                cp = pltpu.make_async_remote_copy(
                    src_ref=a_refs[ai].at[s, pl.ds((1 - c) * h + r0, rows)], dst_ref=got_refs[ai].at[s, pl.ds(r0, rows)],
                    send_sem=send_sems.at[k], recv_sem=recv_sems.at[k], device_id=(x, y, 1 - c), device_id_type=MESH)
                cp.start()
                cps.append(cp)
        for cp in cps:
            cp.wait()

    halves = [jax.ShapeDtypeStruct((a.shape[0], a.shape[1] // 2, a.shape[2]), a.dtype) for a in arrs]
    return pl.pallas_call(
        body, name=name, in_specs=[ANY] * na, out_specs=[ANY] * na, out_shape=halves,
        scratch_shapes=[pltpu.SemaphoreType.DMA((ncopies,))] * 2,
    )(*arrs)


def _scatter_chips(name, arrs):
    na = len(arrs)
    n = len(CHIP_MASKS)

    def body(*refs):
        a_refs, o_refs = refs[:na], refs[na:2 * na]
        send_sems, recv_sems = refs[2 * na:]
        x, y, c = _coords()
        cps = []
        for k in range(na):
            for j, mask in enumerate(CHIP_MASKS):
                px, py, pc = _peer(x, y, c, mask)
                cp = pltpu.make_async_remote_copy(
                    src_ref=a_refs[k].at[2 * px + py], dst_ref=o_refs[k].at[j],
                    send_sem=send_sems.at[k, j], recv_sem=recv_sems.at[k, j],
                    device_id=(px, py, pc), device_id_type=MESH)
                cp.start()
                cps.append(cp)
        for cp in cps:
            cp.wait()

    return pl.pallas_call(
        body, name=name, in_specs=[ANY] * na, out_specs=[ANY] * na,
        out_shape=[jax.ShapeDtypeStruct((n,) + a.shape[1:], a.dtype) for a in arrs],
        scratch_shapes=[pltpu.SemaphoreType.DMA((na, n)), pltpu.SemaphoreType.DMA((na, n))],
    )(*arrs)


def _join_halves(name, bufs):
    na = len(bufs)

    def body(*refs):
        o_refs = refs[na:2 * na]
        send_sems, recv_sems = refs[2 * na:]
        x, y, c = _coords()
        cps = []
        for k in range(na):
            for l in range(2):
                part = o_refs[k].at[l, c]
                cp = pltpu.make_async_remote_copy(
                    src_ref=part, dst_ref=part, send_sem=send_sems.at[k, l], recv_sem=recv_sems.at[k, l],
                    device_id=(x, y, 1 - c), device_id_type=MESH)
                cp.start()
                cps.append(cp)
        for cp in cps:
            cp.wait()

    return pl.pallas_call(
        body, name=name, in_specs=[ANY] * na, out_specs=[ANY] * na,
        out_shape=[jax.ShapeDtypeStruct(b.shape, b.dtype) for b in bufs],
        input_output_aliases={k: k for k in range(na)}, scratch_shapes=[pltpu.SemaphoreType.DMA((na, 2))] * 2,
    )(*bufs)


BIG = ("w_in", "w_uq", "w_ukv", "w_out", "w_ffn_in", "w_ffn_out")
SMALL = ("conv_w", "conv_b", "lru_wa", "lru_ba", "lru_wx", "lru_bx", "lru_a_param", "q_norm_g", "kv_norm_g",
         "final_norm_g")
SMALL_SHAPES = {"conv_w": (2, CONV, D), "conv_b": (2, D), "lru_wa": (2, NB, BW, BW), "lru_ba": (2, NB, BW),
                "lru_wx": (2, NB, BW, BW), "lru_bx": (2, NB, BW), "lru_a_param": (2, D), "q_norm_g": (2, QR),
                "kv_norm_g": (2, KVR), "final_norm_g": (D,)}
SMALL_ROWS = 640
FFN_CHIP_ORDER = (0, 2, 1, 3)


def _size(shape):
    n = 1
    for s in shape:
        n *= s
    return n


def _pack_rows(parts, rows, dtype):
    flat = jnp.concatenate([p.reshape(-1).astype(dtype) for p in parts])
    pad = rows * 1024 - flat.shape[0]
    if pad:
        flat = jnp.concatenate([flat, jnp.zeros((pad,), dtype)])
    return flat.reshape(rows, 1024)


def _unpack_rows(pack, shapes):
    flat = pack.reshape(-1)
    out, off = [], 0
    for s in shapes:
        n = _size(s)
        out.append(flat[off:off + n].reshape(s))
        off += n
    return out


def _inproj_weights(g_in, g_uq, g_ukv):
    w_in = jnp.concatenate([g_in[s] for s in range(4)], axis=1)
    win_p = jnp.concatenate([w_in[:, :D], w_in[:, 1472:2496], w_in[:, 2496:], w_in[:, D:1472],
                             jnp.zeros((D, 64), w_in.dtype)], axis=1)
    w_uq = g_uq.reshape(QR, NH, DN + DR)
    wq_p = jnp.concatenate([w_uq, jnp.zeros((QR, NH, DH - DN - DR), w_uq.dtype)], axis=2).reshape(QR, NH * DH)
    w_ukv = jnp.transpose(g_ukv.reshape(4, KVR, NH, (DN + DV) // 4), (1, 2, 0, 3)).reshape(KVR, NH, DN + DV)
    wk_p = jnp.concatenate([w_ukv[..., :DN], jnp.zeros((KVR, NH, DH - DN), w_ukv.dtype)], axis=2).reshape(KVR, NH * DH)
    wv = w_ukv[..., DN:].reshape(KVR, NH * DV)
    return dict(win=win_p, wq=wq_p, wk=wk_p, wv=wv)


def _other_weights(g_out, g_fi, g_fo):
    ko, ki, kf = g_out.shape[1] // 2, g_fi.shape[1] // 2, g_fo.shape[1] // 2
    wout = [g_out[:, l * ko:(l + 1) * ko].reshape(4 * ko, D) for l in range(2)]
    wfi = [jnp.concatenate([g_fi[s, l * ki:(l + 1) * ki] for s in FFN_CHIP_ORDER], axis=1) for l in range(2)]
    wfo = [g_fo[:, l * kf:(l + 1) * kf].reshape(4 * kf, D) for l in range(2)]
    return dict(wout=wout, wfi=wfi, wfo=wfo)


def _shard_grads(g):
    dwin = g["win"]
    w_in = jnp.concatenate([dwin[:, :D], dwin[:, QKV0:QKV0 + 448], dwin[:, D:QKV0]], axis=1)
    w_in = jnp.transpose(w_in.reshape(D, 4, DIN // 4), (1, 0, 2))
    w_uq = g["wq"].reshape(QR, NH, DH)[..., :DN + DR].reshape(4, QR // 4, NH * (DN + DR))
    w_ukv = jnp.concatenate([g["wk"].reshape(KVR, NH, DH)[..., :DN], g["wv"].reshape(KVR, NH, DV)], axis=2)
    w_ukv = jnp.transpose(w_ukv.reshape(KVR, NH, 4, (DN + DV) // 4), (2, 0, 1, 3)).reshape(4, KVR, NH * (DN + DV) // 4)
    tiles = jnp.split(g["wfi"], 4, axis=1)
    wfi = jnp.stack([tiles[FFN_CHIP_ORDER.index(s)] for s in range(4)])
    return [w_in, w_uq, w_ukv, g["wout"].reshape(4, D // 4, D), wfi, g["wfo"].reshape(4, DFF // 4, D)]


def kernel(x, c, positions, w_ada, b_ada, w_in, conv_w, conv_b, lru_wa, lru_ba, lru_wx, lru_bx, lru_a_param, q_norm_g, kv_norm_g, w_uq, w_ukv, w_out, w_ffn_in, w_ffn_out, final_norm_g, loss_target, m_w_ada, m_b_ada, m_w_in, m_conv_w, m_conv_b, m_lru_wa, m_lru_ba, m_lru_wx, m_lru_bx, m_lru_a_param, m_q_norm_g, m_kv_norm_g, m_w_uq, m_w_ukv, m_w_out, m_w_ffn_in, m_w_ffn_out, m_final_norm_g, v_w_ada, v_b_ada, v_w_in, v_conv_w, v_conv_b, v_lru_wa, v_lru_ba, v_lru_wx, v_lru_bx, v_lru_a_param, v_q_norm_g, v_kv_norm_g, v_w_uq, v_w_ukv, v_w_out, v_w_ffn_in, v_w_ffn_out, v_final_norm_g):
    T = x.shape[1]
    tb = min(ROW_BLOCK, T)
    tq = min(ATTN_BLOCK, T)
    nq = T // tq
    mx, my, mc = _coords()
    chip = 2 * mx + my
    dev = 4 * mx + 2 * my + mc
    weights = dict(w_ada=w_ada, b_ada=b_ada, w_in=w_in, conv_w=conv_w, conv_b=conv_b, lru_wa=lru_wa, lru_ba=lru_ba,
                   lru_wx=lru_wx, lru_bx=lru_bx, lru_a_param=lru_a_param, q_norm_g=q_norm_g, kv_norm_g=kv_norm_g,
                   w_uq=w_uq, w_ukv=w_ukv, w_out=w_out, w_ffn_in=w_ffn_in, w_ffn_out=w_ffn_out,
                   final_norm_g=final_norm_g)
    mom = dict(w_ada=m_w_ada, b_ada=m_b_ada, w_in=m_w_in, conv_w=m_conv_w, conv_b=m_conv_b, lru_wa=m_lru_wa,
               lru_ba=m_lru_ba, lru_wx=m_lru_wx, lru_bx=m_lru_bx, lru_a_param=m_lru_a_param, q_norm_g=m_q_norm_g,
               kv_norm_g=m_kv_norm_g, w_uq=m_w_uq, w_ukv=m_w_ukv, w_out=m_w_out, w_ffn_in=m_w_ffn_in,
               w_ffn_out=m_w_ffn_out, final_norm_g=m_final_norm_g)
    var = dict(w_ada=v_w_ada, b_ada=v_b_ada, w_in=v_w_in, conv_w=v_conv_w, conv_b=v_conv_b, lru_wa=v_lru_wa,
               lru_ba=v_lru_ba, lru_wx=v_lru_wx, lru_bx=v_lru_bx, lru_a_param=v_lru_a_param, q_norm_g=v_q_norm_g,
               kv_norm_g=v_kv_norm_g, w_uq=v_w_uq, w_ukv=v_w_ukv, w_out=v_w_out, w_ffn_in=v_w_ffn_in,
               w_ffn_out=v_w_ffn_out, final_norm_g=v_final_norm_g)
    order = list(weights)

    ns_ada = w_ada.shape[2]
    c_all = _all_gather("gather_c", c, ALL_MASKS, True).reshape(8, D)
    b_sh = lax.dynamic_slice_in_dim(b_ada, chip * ns_ada, ns_ada, axis=1).reshape(2, 1, ns_ada)
    mod_sh = _ada_fwd("ada_fwd", c_all, w_ada, b_sh)
    mod_all = _all_gather("gather_mod", mod_sh.reshape(16, ns_ada), CHIP_MASKS, False)
    mod_mine = lax.dynamic_index_in_dim(mod_all.reshape(4, 2, 8, ns_ada), dev, axis=2, keepdims=False)
    mod = jnp.transpose(mod_mine, (1, 0, 2)).reshape(2, 6, 1, D)

    flat = lambda w: w.astype(bf16).reshape(w.shape[0], w.shape[1], -1)
    b_in, b_uq, b_ukv = flat(w_in), flat(w_uq), flat(w_ukv)
    late = [b_in[1], b_uq[1], b_ukv[1]] + [flat(w).reshape(-1, w.shape[-1]) for w in (w_out, w_ffn_in, w_ffn_out)]
    W0 = _inproj_weights(*_gather_weights("gather_weights_early", [b_in[0], b_uq[0], b_ukv[0]]))
    W = {n: [W0[n], None] for n in W0}
    wa_b = lru_wa.astype(bf16)
    wx_b = lru_wx.astype(bf16)

    pos = positions.reshape(T, 1)
    invf = ROPE_THETA ** (-jnp.arange(0, DR, 2, dtype=f32) / DR)
    invf_p = jnp.concatenate([jnp.zeros((DN,), f32), invf, invf, jnp.zeros((DH - DN - DR,), f32)]).reshape(1, DH)
    cosp, sap, sbp = _rope_tables("rope_tables", pos, invf_p, T, tb)

    ns_cw = conv_w.shape[2]
    cw_all = _all_gather("gather_conv_w", conv_w.reshape(2 * CONV, ns_cw), CHIP_MASKS, False)
    cw_full = jnp.transpose(cw_all, (1, 0, 2)).reshape(2, CONV, D)
    vec = lambda a, l: a[l].reshape(1, -1)

    xl = x.reshape(T, D)
    saved = []
    h1 = _norm_mod("norm_mod_0", xl, mod[0, 1], mod[0, 0], T, tb)
    for l in range(2):
        sh_m, sc_m, g_m, sh_f, sc_f, g_f = [mod[l, j] for j in range(6)]
        proj = _matmul(f"proj_{l}", h1, W["win"][l], "nn")
        u, r, ig, ya = _lru_fwd(f"lru_fwd_{l}", proj, pos, cw_full[l], vec(conv_b, l), wa_b[l], vec(lru_ba, l),
                                wx_b[l], vec(lru_bx, l), vec(lru_a_param, l), T, tb)
        qp, kp, vp, cq, ckv = _mla_prep(f"mla_prep_{l}", proj, cosp, sap, sbp, vec(q_norm_g, l), vec(kv_norm_g, l),
                                        W["wq"][l], W["wk"][l], W["wv"][l], T, tb)
        if l == 0:
            o, lse, *late_all = _flash_fwd("flash_fwd_0", qp, kp, vp, T, tq, late)
            W1 = _inproj_weights(*late_all[:3])
            for n in W1:
                W[n][1] = W1[n]
            W.update(_other_weights(*late_all[3:]))
        else:
            o, lse = _flash_fwd("flash_fwd_1", qp, kp, vp, T, tq)
        y = _gate_mix(f"gate_mix_{l}", proj, ya, o, T, tb)
        mo = _matmul(f"out_proj_{l}", y, W["wout"][l], "nn")
        x2, h2 = _resid_norm_mod(f"resid_norm_f_{l}", xl, g_m, mo, sc_f, sh_f, T, tb)
        gu, act = _ffn_in_swiglu(f"ffn_in_{l}", h2, W["wfi"][l], T)
        fo = _matmul(f"ffn_out_{l}", act, W["wfo"][l], "nn")
        saved.append(dict(x=xl, h1=h1, proj=proj, u=u, r=r, ig=ig, ya=ya, qp=qp, kp=kp, vp=vp, cq=cq, ckv=ckv, o=o,
                          lse=lse, y=y, mo=mo, x2=x2, h2=h2, gu=gu, act=act, fo=fo))
        if l == 0:
            xl, h1 = _resid_norm_mod("resid_norm_m_1", x2, g_f, fo, mod[1, 1], mod[1, 0], T, tb)

    dx, loss_part, dgfin = _final("final", saved[1]["x2"], mod[1, 5], saved[1]["fo"],
                                  final_norm_g.reshape(1, D), loss_target.reshape(T, D), T, tb)
    loss = lax.psum(loss_part[0, 0], ("x", "y", "c"))

    gl = {n: [None, None] for n in ("win", "wq", "wk", "wv", "wout", "wfi", "wfo", "conv_w", "conv_b", "lru_wa",
                                    "lru_ba", "lru_wx", "lru_bx", "lru_a_param", "q_norm_g", "kv_norm_g")}
    dmod = [None, None]

    def reduce_scatter(l, small, bufs):
        pieces = _shard_grads({n: gl[n][l] for n in ("win", "wq", "wk", "wv", "wout", "wfi", "wfo")})
        nb = len(pieces)
        arrs = pieces + ([small] if small is not None else [])
        sums = _pair_sum(f"rs_pair_sum_{l}", arrs, _swap_halves(f"rs_swap_halves_{l}", arrs))
        lands = _scatter_chips(f"rs_scatter_chips_{l}", sums)
        out = _chip_sum(f"rs_chip_sum_{l}", sums[:nb], lands[:nb], l, bufs)
        if small is None:
            return out, None
        return out, _chip_sum("rs_chip_sum_small", sums[nb:], lands[nb:], None)[0]

    for l in (1, 0):
        s = saved[l]
        sh_m, sc_m, g_m, sh_f, sc_f, g_f = [mod[l, j] for j in range(6)]
        dfo, dg_f = _resid_bwd(f"resid_bwd_f_{l}", dx, g_f, s["fo"], T, tb)
        dgu = _dact_swiglu_bwd(f"d_act_{l}", dfo, W["wfo"][l], s["gu"], T)
        gl["wfo"][l] = _matmul(f"dw_ffn_out_{l}", s["act"], dfo, "tn", bf16)
        dh2 = _matmul(f"d_h2_{l}", dgu, W["wfi"][l], "nt")
        gl["wfi"][l] = _matmul(f"dw_ffn_in_{l}", s["h2"], dgu, "tn", bf16)
        dx2, dsc_f, dsh_f = _norm_mod_bwd(f"norm_bwd_f_{l}", dh2, s["x2"], sc_f, dx, T, tb)
        dmo, dg_m = _resid_bwd(f"resid_bwd_m_{l}", dx2, g_m, s["mo"], T, tb)
        dy = _matmul(f"d_y_{l}", dmo, W["wout"][l], "nt")
        gl["wout"][l] = _matmul(f"dw_out_{l}", s["y"], dmo, "tn", bf16)
        dga, dgb, dya, do, dlt = _gate_mix_bwd(f"gate_mix_bwd_{l}", dy, s["proj"], s["ya"], s["o"], T, tb)
        lse_r = s["lse"].reshape(NH, nq, 1, tq)
        dl_r = jnp.transpose(dlt).reshape(NH, nq, 1, tq)
        dk, dv, dq = _flash_bwd(f"flash_bwd_{l}", s["qp"], s["kp"], s["vp"], do, lse_r, dl_r, T, tq)
        dqkv, dqp, dkp, dgq, dgkv = _mla_bwd(f"mla_bwd_{l}", dq, dk, dv, s["proj"], cosp, sap, sbp,
                                              vec(q_norm_g, l), vec(kv_norm_g, l), W["wq"][l], W["wk"][l], W["wv"][l],
                                              T, tb)
        gl["wq"][l] = _matmul(f"dw_uq_{l}", s["cq"], dqp, "tn", bf16)
        gl["wk"][l] = _matmul(f"dw_uk_{l}", s["ckv"], dkp, "tn", bf16)
        gl["wv"][l] = _matmul(f"dw_uv_{l}", s["ckv"], dv, "tn", bf16)
        dxl, dcw, dcb, dwa, dba, dwx, dbx, dsp = _lru_bwd(
            f"lru_bwd_{l}", dya, s["proj"], s["u"], s["r"], s["ig"], s["ya"], pos, cw_full[l], wa_b[l], wx_b[l],
            vec(lru_a_param, l), T, tb)
        dproj = jnp.concatenate([dxl, dga, dgb, dqkv], axis=1)
        dh1 = _matmul(f"d_h1_{l}", dproj, W["win"][l], "nt")
        gl["win"][l] = _matmul(f"dw_in_{l}", s["h1"], dproj, "tn", bf16)
        dx, dsc_m, dsh_m = _norm_mod_bwd(f"norm_bwd_m_{l}", dh1, s["x"], sc_m, dx2, T, tb)
        dmod[l] = jnp.concatenate([dsh_m, dsc_m, dg_m, dsh_f, dsc_f, dg_f], axis=1)
        gl["conv_w"][l], gl["conv_b"][l] = dcw, dcb[0]
        gl["lru_wa"][l], gl["lru_ba"][l], gl["lru_wx"][l], gl["lru_bx"][l] = dwa, dba[0], dwx, dbx[0]
        gl["lru_a_param"][l] = dsp[0]
        gl["q_norm_g"][l], gl["kv_norm_g"][l] = dgq[0], dgkv[0]
        if l == 1:
            rs_bufs, _ = reduce_scatter(1, None, None)
    grad_x = dx.reshape(1, T, D)
    gfull = {n: jnp.stack(gl[n]) for n in SMALL if n != "final_norm_g"}
    gfull["final_norm_g"] = dgfin[0]

    dmod_all = _all_gather("gather_dmod", jnp.concatenate(dmod, axis=1), ALL_MASKS, True).reshape(8, 2, 6, D)
    dmod_sum = _sum_slots("sum_dmod", dmod_all.reshape(8, 12, D)).reshape(2, 6 * D)
    dm_sh = lax.dynamic_slice_in_dim(jnp.transpose(dmod_all.reshape(8, 2, 6 * D), (1, 0, 2)), chip * ns_ada, ns_ada, axis=2)
    grads = {"w_ada": _ada_bwd("ada_bwd", jnp.transpose(c_all), dm_sh), "b_ada": dmod_sum}

    gsmall = _pack_rows([gfull[n] for n in SMALL], SMALL_ROWS, f32).reshape(4, SMALL_ROWS // 4, 1024)
    bufs, tot_s = reduce_scatter(0, gsmall, rs_bufs)
    bufs = _join_halves("rs_join_halves", bufs)
    small_all = _all_gather("gather_small_grads", tot_s, ALL_MASKS, True).reshape(SMALL_ROWS, 1024)
    for n, buf in zip(BIG, bufs):
        grads[n] = buf.reshape(weights[n].shape)
    for n, gs in zip(SMALL, _unpack_rows(small_all, [SMALL_SHAPES[n] for n in SMALL])):
        grads[n] = gs
    grads["conv_w"] = lax.dynamic_slice_in_dim(grads["conv_w"], chip * ns_cw, ns_cw, axis=2)

    deltas, new_m, new_v = {}, {}, {}
    for n in order:
        w = weights[n]
        view = (1, w.shape[0]) if w.ndim == 1 else (_size(w.shape[:-1]), w.shape[-1]) if w.shape[-1] >= 128 else (w.shape[0], _size(w.shape[1:]))
        d, m2, v2 = _adamw(f"adamw_{n}", w.reshape(view), grads[n].reshape(view), mom[n].reshape(view), var[n].reshape(view))
        deltas[n], new_m[n], new_v[n] = d.reshape(w.shape), m2.reshape(w.shape), v2.reshape(w.shape)
    return (loss, grad_x, *[grads[n] for n in order], *[deltas[n] for n in order], *[new_m[n] for n in order],
            *[new_v[n] for n in order])
```

```python
import functools

import jax
import jax.numpy as jnp
from jax import lax
from jax.experimental import pallas as pl
from jax.experimental.pallas import tpu as pltpu

f32 = jnp.float32
bf16 = jnp.bfloat16

D = 1024
NH = 8
DN = 128
DR = 64
DV = 128
DH = 256
QR = 256
KVR = 128
DFF = 2816
FH = DFF // 2
NB = 8
BW = 128
CONV = 4
PW = 3584
QKV0 = 3 * D
DIN = 3520
EPS = 1e-6
LRU_C = 8.0
ROPE_THETA = 10000.0
SCALE = (DN + DR) ** -0.5
LOG2E = 1.4426950408889634
LN2 = 0.6931471805599453
QSCALE = SCALE * LOG2E
NEG = -1e30
HPS = 2

ADAM_LR = 0.001
ADAM_B1 = 0.9
ADAM_B2 = 0.999
ADAM_EPS = 1e-08
ADAM_WD = 0.01
ADAM_STEP = 10

VMEM_LIMIT = 56 * 1024 * 1024
ROW_BLOCK = 256
ATTN_BLOCK = 512
MESH = pl.DeviceIdType.MESH
ANY = pl.BlockSpec(memory_space=pl.ANY)
CHIP_MASKS = ((1, 0, 0), (0, 1, 0), (1, 1, 0))
ALL_MASKS = ((0, 0, 1), (1, 0, 0), (0, 1, 0), (1, 1, 0), (1, 0, 1), (0, 1, 1), (1, 1, 1))


def _cparams(n_axes):
    return pltpu.CompilerParams(dimension_semantics=("arbitrary",) * n_axes, vmem_limit_bytes=VMEM_LIMIT)


def _pick(n, cands):
    for c in cands:
        if n % c == 0:
            return c
    return n


def _rowwise(name, body, T, tb, row_ins, vec_ins, row_outs, acc_outs=(), scratch=(), reverse=False):
    n = T // tb

    def rmap(i):
        return (n - 1 - i) if reverse else i

    in_specs = []
    for (_, w, cb) in row_ins:
        in_specs.append(pl.BlockSpec((tb, w), functools.partial(lambda i, cb: (rmap(i), cb), cb=cb)))
    for v in vec_ins:
        in_specs.append(pl.BlockSpec(v.shape, functools.partial(lambda i, nd: (0,) * nd, nd=v.ndim)))
    out_specs, out_shape = [], []
    for (w, dt) in row_outs:
        out_specs.append(pl.BlockSpec((tb, w), lambda i: (rmap(i), 0)))
        out_shape.append(jax.ShapeDtypeStruct((T, w), dt))
    for (s, dt) in acc_outs:
        out_specs.append(pl.BlockSpec(s, functools.partial(lambda i, nd: (0,) * nd, nd=len(s))))
        out_shape.append(jax.ShapeDtypeStruct(s, dt))

    def kern(*refs):
        body(*refs)

    return pl.pallas_call(
        kern, name=name, grid=(n,), in_specs=in_specs, out_specs=out_specs, out_shape=out_shape,
        scratch_shapes=list(scratch), compiler_params=_cparams(1),
    )(*[a for (a, _, _) in row_ins], *vec_ins)


def _matmul(name, a, b, mode, out_dtype=f32):
    if mode == "nn":
        (M, K), N = a.shape, b.shape[1]
    elif mode == "nt":
        (M, K), N = a.shape, b.shape[0]
    else:
        (K, M), N = a.shape, b.shape[1]
    wide = (1792, 1408, 1024, 512, 256, 128)
    tm = _pick(M, (512, 256, 128)) if mode != "tn" else _pick(M, (1024, 1408, 512, 256, 128))
    tn = _pick(N, wide)
    tk = K if mode != "tn" else _pick(K, (1024, 512, 256, 128))
    nk = K // tk
    dims = {"nn": (((1,), (0,)), ((), ())), "nt": (((1,), (1,)), ((), ())), "tn": (((0,), (0,)), ((), ()))}[mode]
    if mode == "tn":
        a_spec = pl.BlockSpec((tk, tm), lambda j, i, k: (k, i))
    else:
        a_spec = pl.BlockSpec((tm, tk), lambda j, i, k: (i, k))
    once = {"pipeline_mode": pl.Buffered(1)} if (mode != "tn" and N == tn) else {}
    if mode == "nt":
        b_spec = pl.BlockSpec((tn, tk), lambda j, i, k: (j, k), **once)
    else:
        b_spec = pl.BlockSpec((tk, tn), lambda j, i, k: (k, j), **once)
    o_spec = pl.BlockSpec((tm, tn), lambda j, i, k: (i, j))
    use_acc = nk > 1 and out_dtype != f32

    def kern(a_ref, b_ref, o_ref, *scr):
        k = pl.program_id(2)
        d = lax.dot_general(a_ref[...].astype(bf16), b_ref[...].astype(bf16), dims, preferred_element_type=f32)
        if nk == 1:
            o_ref[...] = d.astype(out_dtype)
        else:
            acc = scr[0] if use_acc else o_ref

            @pl.when(k == 0)
            def _():
                acc[...] = d

            @pl.when(k > 0)
            def _():
                acc[...] += d

            if use_acc:
                @pl.when(k == nk - 1)
                def _():
                    o_ref[...] = acc[...].astype(out_dtype)

    return pl.pallas_call(
        kern, name=name, grid=(N // tn, M // tm, nk), in_specs=[a_spec, b_spec], out_specs=o_spec,
        out_shape=jax.ShapeDtypeStruct((M, N), out_dtype),
        scratch_shapes=[pltpu.VMEM((tm, tn), f32)] if use_acc else [],
        compiler_params=_cparams(3),
    )(a, b)


def _colsum(v):
    return jnp.sum(v, axis=0, keepdims=True)


def _init_acc(step, *refs):
    @pl.when(step == 0)
    def _():
        for r in refs:
            r[...] = jnp.zeros_like(r)


def _norm_mod(name, x, sc, sh, T, tb):
    def body(x_ref, sc_ref, sh_ref, h_ref):
        xv = x_ref[...]
        rstd = lax.rsqrt(jnp.mean(xv * xv, axis=-1, keepdims=True) + EPS)
        h_ref[...] = (xv * rstd * (1.0 + sc_ref[...]) + sh_ref[...]).astype(bf16)

    return _rowwise(name, body, T, tb, [(x, D, 0)], [sc, sh], [(D, bf16)])[0]


def _resid_norm_mod(name, x, g, m, sc, sh, T, tb):
    def body(x_ref, m_ref, g_ref, sc_ref, sh_ref, x2_ref, h_ref):
        xv = x_ref[...] + g_ref[...] * m_ref[...]
        x2_ref[...] = xv
        rstd = lax.rsqrt(jnp.mean(xv * xv, axis=-1, keepdims=True) + EPS)
        h_ref[...] = (xv * rstd * (1.0 + sc_ref[...]) + sh_ref[...]).astype(bf16)

    return _rowwise(name, body, T, tb, [(x, D, 0), (m, D, 0)], [g, sc, sh], [(D, f32), (D, bf16)])


def _norm_mod_bwd(name, dh, x, sc, dres, T, tb, resid=None):
    def norm_part(dh_ref, x_ref, dr_ref, sc_ref, dx_ref, dsc_ref, dsh_ref):
        xv = x_ref[...]
        dhv = dh_ref[...]
        rstd = lax.rsqrt(jnp.mean(xv * xv, axis=-1, keepdims=True) + EPS)
        nv = xv * rstd
        dn = dhv * (1.0 + sc_ref[...])
        dxv = dr_ref[...] + rstd * (dn - nv * jnp.mean(dn * nv, axis=-1, keepdims=True))
        dx_ref[...] = dxv
        dsc_ref[...] += _colsum(dhv * nv)
        dsh_ref[...] += _colsum(dhv)
        return dxv

    if resid is None:
        def body(dh_ref, x_ref, dr_ref, sc_ref, dx_ref, dsc_ref, dsh_ref):
            _init_acc(pl.program_id(0), dsc_ref, dsh_ref)
            norm_part(dh_ref, x_ref, dr_ref, sc_ref, dx_ref, dsc_ref, dsh_ref)

        return _rowwise(name, body, T, tb, [(dh, D, 0), (x, D, 0), (dres, D, 0)], [sc], [(D, f32)],
                        [((1, D), f32)] * 2)

    def body_resid(dh_ref, x_ref, dr_ref, m_ref, sc_ref, g_ref, dx_ref, dm_ref, dsc_ref, dsh_ref, dg_ref):
        _init_acc(pl.program_id(0), dsc_ref, dsh_ref, dg_ref)
        dxv = norm_part(dh_ref, x_ref, dr_ref, sc_ref, dx_ref, dsc_ref, dsh_ref)
        dm_ref[...] = (dxv * g_ref[...]).astype(bf16)
        dg_ref[...] += _colsum(dxv * m_ref[...])

    g, m = resid
    return _rowwise(name, body_resid, T, tb, [(dh, D, 0), (x, D, 0), (dres, D, 0), (m, D, 0)], [sc, g],
                    [(D, f32), (D, bf16)], [((1, D), f32)] * 3)


def _expm1_neg(y):
    poly = y * (1.0 + y * (0.5 + y * (1.0 / 6 + y * (1.0 / 24 + y * (1.0 / 120 + y * (1.0 / 720 + y * (1.0 / 5040)))))))
    return jnp.where(y > -0.25, poly, jnp.exp(y) - 1.0)


def _lru_gates(r, sp, reset):
    log_a = -LRU_C * r * sp
    a = jnp.where(reset, 0.0, jnp.exp(log_a))
    mult = jnp.where(reset, 1.0, jnp.sqrt(-_expm1_neg(2.0 * log_a)))
    return a, mult


def _block_dot(v, w_ref, dims):
    outs = [lax.dot_general(v[:, n * BW:(n + 1) * BW], w_ref[n], dims, preferred_element_type=f32) for n in range(NB)]
    return jnp.concatenate(outs, axis=1)


def _lru_fwd(name, proj, pos, cw, cb, wa, ba, wx, bx, ap, T, tb):
    nsteps = tb.bit_length() - 1

    def body(x_ref, pos_ref, cw_ref, cb_ref, wa_ref, ba_ref, wx_ref, bx_ref, ap_ref,
             u_ref, r_ref, i_ref, h_ref, prev_x, carry_h):
        _init_acc(pl.program_id(0), prev_x, carry_h)
        x = x_ref[...]
        px = prev_x[...]
        row = lax.broadcasted_iota(jnp.int32, (tb, 1), 0)
        u = x * cw_ref[CONV - 1:CONV, :] + cb_ref[...]
        for j in range(1, CONV):
            z = jnp.where(row >= tb - j, px, x)
            u = u + pltpu.roll(z, j, 0) * cw_ref[CONV - 1 - j:CONV - j, :]
        ub = u.astype(bf16)
        nn = (((1,), (0,)), ((), ()))
        r = jax.nn.sigmoid(_block_dot(ub, wa_ref, nn) + ba_ref[...])
        ig = jax.nn.sigmoid(_block_dot(ub, wx_ref, nn) + bx_ref[...])
        sp = jax.nn.softplus(-ap_ref[...])
        reset = pos_ref[...] == 0
        a, mult = _lru_gates(r, sp, reset)
        b = u * ig * mult
        sub = row & 7
        for s in (1, 2, 4):
            keep = sub >= s
            a_sh = jnp.where(keep, pltpu.roll(a, s, 0), 1.0)
            b_sh = jnp.where(keep, pltpu.roll(b, s, 0), 0.0)
            b = a * b_sh + b
            a = a * a_sh
        carry = carry_h[...]
        for g in range(tb // 8):
            rows = slice(8 * g, 8 * g + 8)
            hg = b[rows] + a[rows] * carry
            h_ref[rows, :] = hg
            carry = hg[7:8]
        carry_h[...] = carry
        u_ref[...] = u
        r_ref[...] = r
        i_ref[...] = ig
        prev_x[...] = x

    return _rowwise(name, body, T, tb, [(proj, D, 0), (pos, 1, 0)], [cw, cb, wa, ba, wx, bx, ap],
                    [(D, f32)] * 4, scratch=[pltpu.VMEM((tb, D), f32), pltpu.VMEM((1, D), f32)])


def _lru_bwd(name, dh, proj, u, r, ig, h, pos, cw, wa, wx, ap, T, tb):
    n = T // tb
    nsteps = tb.bit_length() - 1
    t8 = tb // 8

    def rmap(i):
        return n - 1 - i

    def body(dh_ref, x_ref, u_ref, r_ref, i_ref, h_ref, hp_ref, pos_ref, cw_ref, wa_ref, wx_ref, ap_ref,
             dx_ref, dcw_ref, dcb_ref, dwa_ref, dba_ref, dwx_ref, dbx_ref, dsp_ref, next_du, carry_g, carry_a):
        step = pl.program_id(0)
        _init_acc(step, dcw_ref, dcb_ref, dwa_ref, dba_ref, dwx_ref, dbx_ref, dsp_ref, next_du, carry_g, carry_a)
        blk = n - 1 - step
        x = x_ref[...]
        u = u_ref[...]
        r = r_ref[...]
        ig = i_ref[...]
        h = h_ref[...]
        row = lax.broadcasted_iota(jnp.int32, (tb, 1), 0)
        sp = jax.nn.softplus(-ap_ref[...])
        reset = pos_ref[...] == 0
        a, mult = _lru_gates(r, sp, reset)
        A = pltpu.roll(jnp.where(row == 0, carry_a[...], a), tb - 1, 0)
        B = dh_ref[...]
        sub = row & 7
        for s in (1, 2, 4):
            keep = sub < 8 - s
            a_sh = jnp.where(keep, pltpu.roll(A, tb - s, 0), 1.0)
            b_sh = jnp.where(keep, pltpu.roll(B, tb - s, 0), 0.0)
            B = B + A * b_sh
            A = A * a_sh
        carry = carry_g[...]
        parts = [None] * (tb // 8)
        for grp in reversed(range(tb // 8)):
            rows = slice(8 * grp, 8 * grp + 8)
            parts[grp] = B[rows] + A[rows] * carry
            carry = parts[grp][0:1]
        g = jnp.concatenate(parts, axis=0)
        carry_g[...] = carry
        carry_a[...] = a[0:1, :]
        h_last = jnp.where(blk == 0, 0.0, hp_ref[7:8, :])
        h_prev = pltpu.roll(jnp.where(row == tb - 1, h_last, h), 1, 0)
        da = g * h_prev
        gu = g * u
        dlog_a = jnp.where(reset, 0.0, da * a - gu * ig * (a * a) / mult)
        du = g * ig * mult
        di = gu * mult
        dr = dlog_a * (-LRU_C * sp)
        dsp_ref[...] += _colsum(dlog_a * (-LRU_C * r)) * (-jax.nn.sigmoid(-ap_ref[...]))
        dzr = dr * r * (1.0 - r)
        dzi = di * ig * (1.0 - ig)
        dzr_b = dzr.astype(bf16)
        dzi_b = dzi.astype(bf16)
        nt = (((1,), (1,)), ((), ()))
        du = du + _block_dot(dzr_b, wa_ref, nt) + _block_dot(dzi_b, wx_ref, nt)
        ub = u.astype(bf16)
        tn = (((0,), (0,)), ((), ()))
        for nb in range(NB):
            sl = slice(nb * BW, (nb + 1) * BW)
            dwa_ref[nb] += lax.dot_general(ub[:, sl], dzr_b[:, sl], tn, preferred_element_type=f32)
            dwx_ref[nb] += lax.dot_general(ub[:, sl], dzi_b[:, sl], tn, preferred_element_type=f32)
        dba_ref[...] += _colsum(dzr)
        dbx_ref[...] += _colsum(dzi)
        dcb_ref[...] += _colsum(du)
        ndu = next_du[...]
        dx = du * cw_ref[CONV - 1:CONV, :]
        dcw_ref[CONV - 1:CONV, :] += _colsum(x * du)
        for j in range(1, CONV):
            up = pltpu.roll(jnp.where(row < j, ndu, du), tb - j, 0)
            dx = dx + up * cw_ref[CONV - 1 - j:CONV - j, :]
            dcw_ref[CONV - 1 - j:CONV - j, :] += _colsum(x * up)
        dx_ref[...] = dx.astype(bf16)
        next_du[...] = du

    row_specs = [pl.BlockSpec((tb, D), lambda i: (rmap(i), 0)) for _ in range(6)]
    in_specs = row_specs + [
        pl.BlockSpec((8, D), lambda i: (jnp.maximum(rmap(i) * t8 - 1, 0), 0)),
        pl.BlockSpec((tb, 1), lambda i: (rmap(i), 0)),
        pl.BlockSpec((CONV, D), lambda i: (0, 0)),
        pl.BlockSpec((NB, BW, BW), lambda i: (0, 0, 0)),
        pl.BlockSpec((NB, BW, BW), lambda i: (0, 0, 0)),
        pl.BlockSpec((1, D), lambda i: (0, 0)),
    ]
    vec = lambda s: pl.BlockSpec(s, functools.partial(lambda i, nd: (0,) * nd, nd=len(s)))
    acc_shapes = [(CONV, D), (1, D), (NB, BW, BW), (1, D), (NB, BW, BW), (1, D), (1, D)]
    return pl.pallas_call(
        body, name=name, grid=(n,), in_specs=in_specs,
        out_specs=[pl.BlockSpec((tb, D), lambda i: (rmap(i), 0))] + [vec(s) for s in acc_shapes],
        out_shape=[jax.ShapeDtypeStruct((T, D), bf16)] + [jax.ShapeDtypeStruct(s, f32) for s in acc_shapes],
        scratch_shapes=[pltpu.VMEM((tb, D), f32), pltpu.VMEM((1, D), f32), pltpu.VMEM((1, D), f32)],
        compiler_params=_cparams(1),
    )(dh, proj, u, r, ig, h, h, pos, cw, wa, wx, ap)


def _rope_tables(name, pos, invf, T, tb):
    def body(pos_ref, invf_ref, cos_ref, sa_ref, sb_ref):
        ang = pos_ref[...].astype(f32) * invf_ref[...]
        lane = lax.broadcasted_iota(jnp.int32, (tb, DH), 1)
        first = (lane >= DN) & (lane < DN + DR // 2)
        second = (lane >= DN + DR // 2) & (lane < DN + DR)
        c = jnp.cos(ang)
        s = jnp.sin(ang)
        cos_ref[...] = jnp.where(lane < DN, 1.0, jnp.where(first | second, c, 0.0))
        sa_ref[...] = jnp.where(first, -s, 0.0)
        sb_ref[...] = jnp.where(second, s, 0.0)

    return _rowwise(name, body, T, tb, [(pos, 1, 0)], [invf], [(DH, f32)] * 3)


def _rope(z, cos8, sa8, sb8):
    w = z.shape[1]
    return z * cos8 + pltpu.roll(z, w - DR // 2, 1) * sa8 + pltpu.roll(z, DR // 2, 1) * sb8


def _rope_t(d, cos8, sa8, sb8):
    w = d.shape[1]
    return d * cos8 + pltpu.roll(d * sa8, DR // 2, 1) + pltpu.roll(d * sb8, w - DR // 2, 1)


def _mla_prep(name, proj, cosp, sap, sbp, gq, gkv, wq, wk, wv, T, tb):
    def body(qkv_ref, cos_ref, sa_ref, sb_ref, gq_ref, gkv_ref, wq_ref, wk_ref, wv_ref,
             q_ref, k_ref, v_ref, cq_ref, ckv_ref):
        qkv = qkv_ref[...]
        qd = qkv[:, :QR]
        kvd = qkv[:, QR:QR + KVR]
        slab = qkv[:, QR + KVR:]
        cq = (qd * lax.rsqrt(jnp.mean(qd * qd, axis=-1, keepdims=True) + EPS) * gq_ref[...]).astype(bf16)
        ckv = (kvd * lax.rsqrt(jnp.mean(kvd * kvd, axis=-1, keepdims=True) + EPS) * gkv_ref[...]).astype(bf16)
        cos8 = jnp.tile(cos_ref[...], (1, NH))
        sa8 = jnp.tile(sa_ref[...], (1, NH))
        sb8 = jnp.tile(sb_ref[...], (1, NH))
        qpre = jnp.dot(cq, wq_ref[...], preferred_element_type=f32)
        shared = jnp.concatenate([jnp.zeros((tb, DN), f32), slab], axis=1)
        kpre = jnp.dot(ckv, wk_ref[...], preferred_element_type=f32) + jnp.tile(shared, (1, NH))
        q_ref[...] = (_rope(qpre, cos8, sa8, sb8) * QSCALE).astype(bf16)
        k_ref[...] = _rope(kpre, cos8, sa8, sb8).astype(bf16)
        v_ref[...] = jnp.dot(ckv, wv_ref[...], preferred_element_type=f32).astype(bf16)
        cq_ref[...] = cq
        ckv_ref[...] = ckv

    return _rowwise(name, body, T, tb,
                    [(proj, 512, QKV0 // 512), (cosp, DH, 0), (sap, DH, 0), (sbp, DH, 0)],
                    [gq, gkv, wq, wk, wv],
                    [(NH * DH, bf16), (NH * DH, bf16), (NH * DV, bf16), (QR, bf16), (KVR, bf16)])


def _mla_bwd(name, dq, dk, dv, proj, cosp, sap, sbp, gq, gkv, wq, wk, wv, T, tb):
    def body(dq_ref, dk_ref, dv_ref, qkv_ref, cos_ref, sa_ref, sb_ref, gq_ref, gkv_ref, wq_ref, wk_ref, wv_ref,
             dqkv_ref, dqp_ref, dkp_ref, dgq_ref, dgkv_ref):
        _init_acc(pl.program_id(0), dgq_ref, dgkv_ref)
        cos8 = jnp.tile(cos_ref[...], (1, NH))
        sa8 = jnp.tile(sa_ref[...], (1, NH))
        sb8 = jnp.tile(sb_ref[...], (1, NH))
        dqp = _rope_t(dq_ref[...] * SCALE, cos8, sa8, sb8)
        dkp = _rope_t(dk_ref[...], cos8, sa8, sb8)
        dqp_b = dqp.astype(bf16)
        dkp_b = dkp.astype(bf16)
        dqp_ref[...] = dqp_b
        dkp_ref[...] = dkp_b
        nt = (((1,), (1,)), ((), ()))
        dcq = lax.dot_general(dqp_b, wq_ref[...], nt, preferred_element_type=f32)
        dckv = (lax.dot_general(dkp_b, wk_ref[...], nt, preferred_element_type=f32)
                + lax.dot_general(dv_ref[...], wv_ref[...], nt, preferred_element_type=f32))
        dshared = dkp[:, 0:DH]
        for hd in range(1, NH):
            dshared = dshared + dkp[:, hd * DH:(hd + 1) * DH]
        qkv = qkv_ref[...]
        qd = qkv[:, :QR]
        kvd = qkv[:, QR:QR + KVR]
        nq = qd * lax.rsqrt(jnp.mean(qd * qd, axis=-1, keepdims=True) + EPS)
        nkv = kvd * lax.rsqrt(jnp.mean(kvd * kvd, axis=-1, keepdims=True) + EPS)
        dgq_ref[...] += _colsum(dcq * nq)
        dgkv_ref[...] += _colsum(dckv * nkv)
        dnq = dcq * gq_ref[...]
        dnkv = dckv * gkv_ref[...]
        rq = lax.rsqrt(jnp.mean(qd * qd, axis=-1, keepdims=True) + EPS)
        rkv = lax.rsqrt(jnp.mean(kvd * kvd, axis=-1, keepdims=True) + EPS)
        dqd = rq * (dnq - nq * jnp.mean(dnq * nq, axis=-1, keepdims=True))
        dkvd = rkv * (dnkv - nkv * jnp.mean(dnkv * nkv, axis=-1, keepdims=True))
        dqkv_ref[...] = jnp.concatenate([dqd, dkvd, dshared[:, DN:]], axis=1).astype(bf16)

    return _rowwise(name, body, T, tb,
                    [(dq, NH * DH, 0), (dk, NH * DH, 0), (dv, NH * DV, 0), (proj, 512, QKV0 // 512),
                     (cosp, DH, 0), (sap, DH, 0), (sbp, DH, 0)],
                    [gq, gkv, wq, wk, wv],
                    [(512, bf16), (NH * DH, bf16), (NH * DH, bf16)],
                    [((1, QR), f32), ((1, KVR), f32)])


def _flash_fwd(name, q, k, v, T, tq, pack=None):
    nq = T // tq
    unroll = 2

    def body(q_ref, k_ref, v_ref, o_ref, lse_ref):
        i = pl.program_id(1)
        nt = (((1,), (1,)), ((), ()))
        qs = [q_ref[:, hh * DH:(hh + 1) * DH] for hh in range(HPS)]

        def one(hh, j, carry, masked):
            m, l, acc = carry
            off = pl.multiple_of(j * tq, tq)
            kj = k_ref[pl.ds(off, tq), hh * DH:(hh + 1) * DH]
            vj = v_ref[pl.ds(off, tq), hh * DV:(hh + 1) * DV]
            s = lax.dot_general(qs[hh], kj, nt, preferred_element_type=f32)
            if masked:
                rr = lax.broadcasted_iota(jnp.int32, (tq, tq), 0)
                cc = lax.broadcasted_iota(jnp.int32, (tq, tq), 1)
                s = jnp.where(cc <= rr, s, NEG)
            m_new = jnp.maximum(m, jnp.max(s, axis=-1, keepdims=True))
            alpha = jnp.exp2(m - m_new)
            p = jnp.exp2(s - m_new)
            l = alpha * l + jnp.sum(p, axis=-1, keepdims=True)
            acc = alpha * acc + jnp.dot(p.astype(bf16), vj, preferred_element_type=f32)
            return m_new, l, acc

        def step(j, carries, masked):
            return tuple(one(hh, j, carries[hh], masked) for hh in range(HPS))

        def pair(jj, carries):
            for u in range(unroll):
                carries = step(jj * unroll + u, carries, False)
            return carries

        init1 = (jnp.full((tq, 1), NEG, f32), jnp.zeros((tq, 1), f32), jnp.zeros((tq, DV), f32))
        carry = lax.fori_loop(0, i // unroll, pair, tuple(init1 for _ in range(HPS)))
        carry = lax.fori_loop((i // unroll) * unroll, i, lambda j, c: step(j, c, False), carry)
        carry = step(i, carry, True)
        for hh in range(HPS):
            m, l, acc = carry[hh]
            o_ref[:, hh * DV:(hh + 1) * DV] = acc / l
            lse_ref[hh] = m + jnp.log2(l)

    def body_with_gather(q_ref, k_ref, v_ref, *rest):
        nw = len(pack)
        w_refs, (o_ref, lse_ref), wall_refs, sems = rest[:nw], rest[nw:nw + 2], rest[nw + 2:2 * nw + 2], rest[2 * nw + 2:]
        h = pl.program_id(0)
        i = pl.program_id(1)

        @pl.when((h == 0) & (i == 0))
        def _():
            mine, first, _ = _gather_copies(w_refs, wall_refs, *sems)
            _gather_start(mine, first)

        @pl.when((h == ngrp // 2) & (i == 0))
        def _():
            _, first, passed = _gather_copies(w_refs, wall_refs, *sems)
            _gather_forward(first, passed)

        body(q_ref, k_ref, v_ref, o_ref, lse_ref)

        @pl.when((h == ngrp - 1) & (i == nq - 1))
        def _():
            _gather_finish(*_gather_copies(w_refs, wall_refs, *sems))

    ngrp = NH // HPS
    in_specs = [pl.BlockSpec((tq, HPS * DH), lambda h, i: (i, h)),
                pl.BlockSpec((T, HPS * DH), lambda h, i: (0, h), pipeline_mode=pl.Buffered(1)),
                pl.BlockSpec((T, HPS * DV), lambda h, i: (0, h), pipeline_mode=pl.Buffered(1))]
    out_specs = [pl.BlockSpec((tq, HPS * DV), lambda h, i: (i, h)), pl.BlockSpec((HPS, tq, 1), lambda h, i: (h, i, 0))]
    out_shape = [jax.ShapeDtypeStruct((T, NH * DV), f32), jax.ShapeDtypeStruct((NH, T, 1), f32)]
    if pack is None:
        return pl.pallas_call(body, name=name, grid=(ngrp, nq), in_specs=in_specs, out_specs=out_specs,
                              out_shape=out_shape, compiler_params=_cparams(2))(q, k, v)
    return pl.pallas_call(
        body_with_gather, name=name, grid=(ngrp, nq), in_specs=in_specs + [ANY] * len(pack),
        out_specs=out_specs + [ANY] * len(pack),
        out_shape=out_shape + [jax.ShapeDtypeStruct((4,) + a.shape, a.dtype) for a in pack],
        scratch_shapes=_gather_sems(pack), compiler_params=_cparams(2),
    )(q, k, v, *pack)


def _flash_bwd(name, q, k, v, do, lse, dl, T, tq, scatter=None):
    nq = T // tq

    def body(k_ref, v_ref, q_ref, do_ref, lse_ref, dl_ref, dk_ref, dv_ref, dq_ref):
        j = pl.program_id(1)

        @pl.when(j == 0)
        def _():
            dq_ref[...] = jnp.zeros_like(dq_ref)

        nt = (((1,), (1,)), ((), ()))
        tn = (((0,), (0,)), ((), ()))
        ks = [k_ref[:, hh * DH:(hh + 1) * DH] for hh in range(HPS)]
        vs = [v_ref[:, hh * DV:(hh + 1) * DV] for hh in range(HPS)]

        def one(hh, i, carry, masked):
            dk, dv = carry
            off = pl.multiple_of(i * tq, tq)
            qi = q_ref[pl.ds(off, tq), hh * DH:(hh + 1) * DH]
            doi = do_ref[pl.ds(off, tq), hh * DV:(hh + 1) * DV]
            st = lax.dot_general(ks[hh], qi, nt, preferred_element_type=f32)
            if masked:
                kr = lax.broadcasted_iota(jnp.int32, (tq, tq), 0)
                qc = lax.broadcasted_iota(jnp.int32, (tq, tq), 1)
                st = jnp.where(qc >= kr, st, NEG)
            pt = jnp.exp2(st - lse_ref[hh, i])
            dv = dv + jnp.dot(pt.astype(bf16), doi, preferred_element_type=f32)
            dpt = lax.dot_general(vs[hh], doi, nt, preferred_element_type=f32)
            dst = (pt * (dpt - dl_ref[hh, i])).astype(bf16)
            dk = dk + jnp.dot(dst, qi, preferred_element_type=f32)
            dq_ref[pl.ds(off, tq), hh * DH:(hh + 1) * DH] += lax.dot_general(dst, ks[hh], tn, preferred_element_type=f32)
            return dk, dv

        def step(i, carries, masked):
            return tuple(one(hh, i, carries[hh], masked) for hh in range(HPS))

        zero = tuple((jnp.zeros((tq, DH), f32), jnp.zeros((tq, DV), f32)) for _ in range(HPS))
        carry = step(j, zero, True)
        carry = lax.fori_loop(j + 1, nq, lambda i, c: step(i, c, False), carry)
        for hh in range(HPS):
            dk, dv = carry[hh]
            dk_ref[:, hh * DH:(hh + 1) * DH] = dk * LN2
            dv_ref[:, hh * DV:(hh + 1) * DV] = dv.astype(bf16)

    def body_with_scatter(*refs):
        ns = len(scatter)
        ins, a_refs, outs = refs[:6], refs[6:6 + ns], refs[6 + ns:9 + ns]
        o_refs, sems = refs[9 + ns:9 + 2 * ns], refs[9 + 2 * ns:]
        h = pl.program_id(0)
        j = pl.program_id(1)

        @pl.when((h == 0) & (j == 0))
        def _():
            for cp in _scatter_copies(a_refs, o_refs, *sems):
                cp.start()

        body(*ins, *outs)

        @pl.when((h == ngrp - 1) & (j == nq - 1))
        def _():
            for cp in _scatter_copies(a_refs, o_refs, *sems):
                cp.wait()

    ngrp = NH // HPS
    in_specs = [pl.BlockSpec((tq, HPS * DH), lambda h, j: (j, h)),
                pl.BlockSpec((tq, HPS * DV), lambda h, j: (j, h)),
                pl.BlockSpec((T, HPS * DH), lambda h, j: (0, h), pipeline_mode=pl.Buffered(1)),
                pl.BlockSpec((T, HPS * DV), lambda h, j: (0, h), pipeline_mode=pl.Buffered(1)),
                pl.BlockSpec((HPS, nq, 1, tq), lambda h, j: (h, 0, 0, 0)),
                pl.BlockSpec((HPS, nq, 1, tq), lambda h, j: (h, 0, 0, 0))]
    out_specs = [pl.BlockSpec((tq, HPS * DH), lambda h, j: (j, h)),
                 pl.BlockSpec((tq, HPS * DV), lambda h, j: (j, h)),
                 pl.BlockSpec((T, HPS * DH), lambda h, j: (0, h), pipeline_mode=pl.Buffered(1))]
    out_shape = [jax.ShapeDtypeStruct((T, NH * DH), f32), jax.ShapeDtypeStruct((T, NH * DV), bf16),
                 jax.ShapeDtypeStruct((T, NH * DH), f32)]
    if scatter is None:
        return pl.pallas_call(body, name=name, grid=(ngrp, nq), in_specs=in_specs, out_specs=out_specs,
                              out_shape=out_shape, compiler_params=_cparams(2))(k, v, q, do, lse, dl)
    ns = len(scatter)
    return pl.pallas_call(
        body_with_scatter, name=name, grid=(ngrp, nq), in_specs=in_specs + [ANY] * ns,
        out_specs=out_specs + [ANY] * ns, out_shape=out_shape + _scatter_shapes(scatter),
        scratch_shapes=_scatter_sems(scatter), compiler_params=_cparams(2),
    )(k, v, q, do, lse, dl, *scatter)


def _gate_mix(name, proj, ya, o, T, tb):
    def body(ga_ref, gb_ref, ya_ref, o_ref, y_ref):
        y_ref[...] = (jax.nn.sigmoid(ga_ref[...]) * ya_ref[...] + jax.nn.sigmoid(gb_ref[...]) * o_ref[...]).astype(bf16)

    return _rowwise(name, body, T, tb, [(proj, D, 1), (proj, D, 2), (ya, D, 0), (o, D, 0)], [], [(D, bf16)])[0]


def _gate_mix_bwd(name, dy, proj, ya, o, T, tb):
    def body(dy_ref, ga_ref, gb_ref, ya_ref, o_ref, dga_ref, dgb_ref, dya_ref, do_ref, dl_ref):
        dy = dy_ref[...]
        sa = jax.nn.sigmoid(ga_ref[...])
        sb = jax.nn.sigmoid(gb_ref[...])
        ov = o_ref[...]
        dga_ref[...] = (dy * ya_ref[...] * sa * (1.0 - sa)).astype(bf16)
        dgb_ref[...] = (dy * ov * sb * (1.0 - sb)).astype(bf16)
        dya_ref[...] = dy * sa
        do = dy * sb
        do_ref[...] = do.astype(bf16)
        prod = do * ov
        for hd in range(NH):
            dl_ref[:, hd:hd + 1] = jnp.sum(prod[:, hd * DV:(hd + 1) * DV], axis=-1, keepdims=True)

    return _rowwise(name, body, T, tb, [(dy, D, 0), (proj, D, 1), (proj, D, 2), (ya, D, 0), (o, D, 0)], [],
                    [(D, bf16), (D, bf16), (D, f32), (D, bf16), (NH, f32)])


def _ffn_in_swiglu(name, h, w, T):
    tm = _pick(T, (512, 256, 128))

    def body(h_ref, w_ref, z_ref, a_ref):
        z = jnp.dot(h_ref[...], w_ref[...], preferred_element_type=f32)
        g = z[:, :FH]
        z_ref[...] = z.astype(bf16)
        a_ref[...] = (g * jax.nn.sigmoid(g) * z[:, FH:]).astype(bf16)

    return pl.pallas_call(
        body, name=name, grid=(2, T // tm),
        in_specs=[pl.BlockSpec((tm, D), lambda j, i: (i, 0)), pl.BlockSpec((D, 2 * FH), lambda j, i: (0, j))],
        out_specs=[pl.BlockSpec((tm, 2 * FH), lambda j, i: (i, j)), pl.BlockSpec((tm, FH), lambda j, i: (i, j))],
        out_shape=[jax.ShapeDtypeStruct((T, 2 * DFF), bf16), jax.ShapeDtypeStruct((T, DFF), bf16)],
        compiler_params=_cparams(2),
    )(h, w)


def _dact_swiglu_bwd(name, dfo, w, z, T):
    tm = _pick(T, (512, 256, 128))

    def body(d_ref, w_ref, z_ref, o_ref):
        da = lax.dot_general(d_ref[...], w_ref[...], (((1,), (1,)), ((), ())), preferred_element_type=f32)
        g = z_ref[:, :FH].astype(f32)
        u = z_ref[:, FH:].astype(f32)
        sg = jax.nn.sigmoid(g)
        o_ref[:, :FH] = (da * u * sg * (1.0 + g * (1.0 - sg))).astype(bf16)
        o_ref[:, FH:] = (da * g * sg).astype(bf16)

    return pl.pallas_call(
        body, name=name, grid=(2, T // tm),
        in_specs=[pl.BlockSpec((tm, D), lambda j, i: (i, 0)), pl.BlockSpec((FH, D), lambda j, i: (j, 0)),
                  pl.BlockSpec((tm, 2 * FH), lambda j, i: (i, j))],
        out_specs=pl.BlockSpec((tm, 2 * FH), lambda j, i: (i, j)),
        out_shape=jax.ShapeDtypeStruct((T, 2 * DFF), bf16), compiler_params=_cparams(2),
    )(dfo, w, z)


def _final(name, x, g, m, gfin, tgt, T, tb):
    def body(x_ref, m_ref, t_ref, g_ref, gf_ref, dx_ref, dm_ref, loss_ref, dgf_ref, dg_ref):
        _init_acc(pl.program_id(0), loss_ref, dgf_ref, dg_ref)
        mv = m_ref[...]
        xv = x_ref[...] + g_ref[...] * mv
        rstd = lax.rsqrt(jnp.mean(xv * xv, axis=-1, keepdims=True) + EPS)
        nv = xv * rstd
        err = nv * gf_ref[...] - t_ref[...]
        loss_ref[...] += 0.5 * jnp.sum(jnp.mean(err * err, axis=-1, keepdims=True), axis=0, keepdims=True)
        dy = err * (1.0 / D)
        dgf_ref[...] += _colsum(dy * nv)
        dn = dy * gf_ref[...]
        dxv = rstd * (dn - nv * jnp.mean(dn * nv, axis=-1, keepdims=True))
        dx_ref[...] = dxv
        dm_ref[...] = (dxv * g_ref[...]).astype(bf16)
        dg_ref[...] += _colsum(dxv * mv)

    return _rowwise(name, body, T, tb, [(x, D, 0), (m, D, 0), (tgt, D, 0)], [g, gfin], [(D, f32), (D, bf16)],
                    [((1, 1), f32), ((1, D), f32), ((1, D), f32)])


def _adamw(name, w, g, m, v):
    R, C = w.shape
    tb = _pick(R, (256, 128, 64, 32, 16, 8))
    c1 = 1.0 - ADAM_B1 ** ADAM_STEP
    c2 = 1.0 - ADAM_B2 ** ADAM_STEP

    def body(w_ref, g_ref, m_ref, v_ref, d_ref, m2_ref, v2_ref):
        gv = g_ref[...]
        m2 = ADAM_B1 * m_ref[...] + (1.0 - ADAM_B1) * gv
        v2 = ADAM_B2 * v_ref[...] + (1.0 - ADAM_B2) * (gv * gv)
        m2_ref[...] = m2
        v2_ref[...] = v2
        d_ref[...] = -ADAM_LR * ((m2 / c1) / (jnp.sqrt(v2 / c2) + ADAM_EPS) + ADAM_WD * w_ref[...])

    return _rowwise(name, body, R, tb, [(w, C, 0), (g, C, 0), (m, C, 0), (v, C, 0)], [], [(C, f32)] * 3)


def _ada_fwd(name, c_all, w_ada, b_sh):
    nl, _, ns = w_ada.shape

    def body(c_ref, w_ref, b_ref, o_ref):
        o_ref[0] = jnp.dot(c_ref[...].astype(bf16), w_ref[0].astype(bf16), preferred_element_type=f32) + b_ref[0]

    return pl.pallas_call(
        body, name=name, grid=(nl,),
        in_specs=[pl.BlockSpec((8, D), lambda l: (0, 0)), pl.BlockSpec((1, D, ns), lambda l: (l, 0, 0)),
                  pl.BlockSpec((1, 1, ns), lambda l: (l, 0, 0))],
        out_specs=pl.BlockSpec((1, 8, ns), lambda l: (l, 0, 0)),
        out_shape=jax.ShapeDtypeStruct((nl, 8, ns), f32), compiler_params=_cparams(1),
    )(c_all, w_ada, b_sh)


def _ada_bwd(name, c_t, dm_sh):
    nl, _, ns = dm_sh.shape

    def body(c_ref, dm_ref, o_ref):
        acc = c_ref[:, 0:1] * dm_ref[0, 0:1, :]
        for d in range(1, 8):
            acc = acc + c_ref[:, d:d + 1] * dm_ref[0, d:d + 1, :]
        o_ref[0] = acc

    return pl.pallas_call(
        body, name=name, grid=(nl,),
        in_specs=[pl.BlockSpec((D, 8), lambda l: (0, 0)), pl.BlockSpec((1, 8, ns), lambda l: (l, 0, 0))],
        out_specs=pl.BlockSpec((1, D, ns), lambda l: (l, 0, 0)),
        out_shape=jax.ShapeDtypeStruct((nl, D, ns), f32), compiler_params=_cparams(1),
    )(c_t, dm_sh)


def _sum_slots(name, a):
    n, R, C = a.shape
    tb = _pick(R, (256, 128, 64, 32, 16, 8))

    def body(a_ref, o_ref):
        acc = a_ref[0].astype(f32)
        for s in range(1, n):
            acc = acc + a_ref[s].astype(f32)
        o_ref[...] = acc

    return pl.pallas_call(
        body, name=name, grid=(R // tb,), in_specs=[pl.BlockSpec((n, tb, C), lambda i: (0, i, 0))],
        out_specs=pl.BlockSpec((tb, C), lambda i: (i, 0)), out_shape=jax.ShapeDtypeStruct((R, C), f32),
        compiler_params=_cparams(1),
    )(a)


def _chunks(rows, dtype, want):
    unit = 16 if dtype == bf16 else 8
    for n in range(want, 0, -1):
        if rows % n == 0 and (rows // n) % unit == 0:
            return n, rows // n
    return 1, rows


def _my_chip():
    return 2 * lax.axis_index("x") + lax.axis_index("y")


def _pair_sum(name, arrs, gots):
    na = len(arrs)

    def body(*refs):
        for k in range(na):
            a_ref, g_ref, o_ref = refs[k], refs[na + k], refs[2 * na + k]
            o_ref[...] = (a_ref[...].astype(f32) + g_ref[...].astype(f32)).astype(o_ref.dtype)

    in_specs = [pl.BlockSpec((1, a.shape[1] // 4, a.shape[2]), lambda s, i: (s, 2 * lax.axis_index("c") + i, 0))
                for a in arrs]
    half_specs = [pl.BlockSpec((1, g.shape[1] // 2, g.shape[2]), lambda s, i: (s, i, 0)) for g in gots]
    return pl.pallas_call(
        body, name=name, grid=(4, 2), in_specs=in_specs + half_specs, out_specs=half_specs,
        out_shape=[jax.ShapeDtypeStruct(g.shape, a.dtype) for a, g in zip(arrs, gots)], compiler_params=_cparams(2),
    )(*arrs, *gots)


def _chip_sum(name, sums, lands, layer, bufs=None):
    na = len(sums)

    def body(*refs):
        for k in range(na):
            s_ref, l_ref, o_ref = refs[k], refs[na + k], refs[len(refs) - na + k]
            acc = s_ref[0].astype(f32)
            for j in range(l_ref.shape[0]):
                acc = acc + l_ref[j].astype(f32)
            o_ref[...] = acc.reshape(o_ref.shape)

    in_specs = [pl.BlockSpec((1, s.shape[1] // 2, s.shape[2]), lambda i: (_my_chip(), i, 0)) for s in sums]
    in_specs += [pl.BlockSpec((3, l.shape[1] // 2, l.shape[2]), lambda i: (0, i, 0)) for l in lands]
    if layer is None:
        out_specs = [pl.BlockSpec((s.shape[1] // 2, s.shape[2]), lambda i: (i, 0)) for s in sums]
        out_shape = [jax.ShapeDtypeStruct(s.shape[1:], f32) for s in sums]
    else:
        out_specs = [pl.BlockSpec((1, 1, s.shape[1] // 2, s.shape[2]), lambda i: (layer, lax.axis_index("c"), i, 0))
                     for s in sums]
        out_shape = [jax.ShapeDtypeStruct((2, 2) + s.shape[1:], f32) for s in sums]
    aliases = {}
    operands = list(sums) + list(lands)
    if bufs is not None:
        in_specs += [ANY] * na
        aliases = {2 * na + k: k for k in range(na)}
        operands += list(bufs)
    return pl.pallas_call(
        body, name=name, grid=(2,), in_specs=in_specs, out_specs=out_specs, out_shape=out_shape,
        input_output_aliases=aliases, compiler_params=_cparams(1),
    )(*operands)


def _coords():
    return lax.axis_index("x"), lax.axis_index("y"), lax.axis_index("c")


def _flip(v, m):
    return (1 - v) if m else v


def _peer(x, y, c, mask):
    return (_flip(x, mask[0]), _flip(y, mask[1]), _flip(c, mask[2]))


def _all_gather(name, arr, masks, whole_mesh):
    nslots = 8 if whole_mesh else 4

    def slot(x, y, c):
        return 4 * x + 2 * y + c if whole_mesh else 2 * x + y

    def body(a_ref, o_ref, send_sems, recv_sems, local_sem):
        x, y, c = _coords()
        mine = pltpu.make_async_copy(a_ref, o_ref.at[slot(x, y, c)], local_sem)
        mine.start()
        copies = []
        for k, mask in enumerate(masks):
            cp = pltpu.make_async_remote_copy(
                src_ref=a_ref, dst_ref=o_ref.at[slot(x, y, c)], send_sem=send_sems.at[k], recv_sem=recv_sems.at[k],
                device_id=_peer(x, y, c, mask), device_id_type=MESH)
            cp.start()
            copies.append(cp)
        for cp in copies:
            cp.wait()
        mine.wait()

    return pl.pallas_call(
        body, name=name, in_specs=[ANY], out_specs=ANY,
        out_shape=jax.ShapeDtypeStruct((nslots,) + arr.shape, arr.dtype),
        scratch_shapes=[pltpu.SemaphoreType.DMA((len(masks),)), pltpu.SemaphoreType.DMA((len(masks),)),
                        pltpu.SemaphoreType.DMA],
    )(arr)


def _gather_weights(name, arrs):
    na = len(arrs)

    def body(*refs):
        mine, first, passed = _gather_copies(refs[:na], refs[na:2 * na], *refs[2 * na:])
        _gather_start(mine, first)
        _gather_forward(first, passed)
        _gather_finish(mine, first, passed)

    return pl.pallas_call(
        body, name=name, in_specs=[ANY] * na, out_specs=[ANY] * na,
        out_shape=[jax.ShapeDtypeStruct((4,) + a.shape, a.dtype) for a in arrs], scratch_shapes=_gather_sems(arrs),
    )(*arrs)


def _gather_plan(shape, dtype):
    big = (shape[0] // 2) * shape[1] * 2 >= (2 << 20)
    return _chunks(shape[0] // 2, dtype, 3 if big else 1), _chunks(shape[0], dtype, 4 if big else 1)


def _gather_sems(arrs):
    plans = [_gather_plan(a.shape, a.dtype) for a in arrs]
    nremote = sum(len(CHIP_MASKS) * p[0][0] for p in plans)
    nlocal = sum(p[1][0] for p in plans)
    return [pltpu.SemaphoreType.DMA((nremote,))] * 4 + [pltpu.SemaphoreType.DMA((nlocal,))]


def _gather_copies(w_refs, o_refs, sa, ra, sb, rb, local_sems):
    x, y, c = _coords()
    chip = 2 * x + y
    mine, first, passed = [], [], []
    for w_ref, o_ref in zip(w_refs, o_refs):
        (nch, rows), (nloc, lrows) = _gather_plan(w_ref.shape, w_ref.dtype)
        rh = w_ref.shape[0] // 2
        for k in range(nloc):
            part = pl.ds(k * lrows, lrows)
            mine.append(pltpu.make_async_copy(w_ref.at[part], o_ref.at[chip, part], local_sems.at[len(mine)]))
        for mask in CHIP_MASKS:
            px, py, _ = _peer(x, y, c, mask)
            for ch in range(nch):
                k = len(first)
                part = pl.ds(c * rh + ch * rows, rows)
                first.append(pltpu.make_async_remote_copy(
                    src_ref=w_ref.at[part], dst_ref=o_ref.at[chip, part], send_sem=sa.at[k], recv_sem=ra.at[k],
                    device_id=(px, py, c), device_id_type=MESH))
                landed = o_ref.at[2 * px + py, part]
                passed.append(pltpu.make_async_remote_copy(
                    src_ref=landed, dst_ref=landed, send_sem=sb.at[k], recv_sem=rb.at[k],
                    device_id=(x, y, 1 - c), device_id_type=MESH))
    return mine, first, passed


def _gather_start(mine, first):
    for cp in mine + first:
        cp.start()


def _gather_forward(first, passed):
    for a, b in zip(first, passed):
        a.wait_recv()
        b.start()


def _gather_finish(mine, first, passed):
    for a, b in zip(first, passed):
        a.wait_send()
        b.wait()
    for cp in mine:
        cp.wait()


def _swap_halves(name, arrs):
    na = len(arrs)
    plan = []
    for a in arrs:
        h = a.shape[1] // 2
        nch, rows = _chunks(h, a.dtype, 2 if h * a.shape[2] * 2 >= (1 << 20) else 1)
        plan.append([(s, ch * rows, rows) for s in range(a.shape[0]) for ch in range(nch)])
    ncopies = sum(len(p) for p in plan)

    def body(*refs):
        a_refs, got_refs = refs[:na], refs[na:2 * na]
        send_sems, recv_sems = refs[2 * na:]
        x, y, c = _coords()
        cps = []
        for ai in range(na):
            h = arrs[ai].shape[1] // 2
            for (s, r0, rows) in plan[ai]:
                k = len(cps)
                cp = pltpu.make_async_remote_copy(
                    src_ref=a_refs[ai].at[s, pl.ds((1 - c) * h + r0, rows)], dst_ref=got_refs[ai].at[s, pl.ds(r0, rows)],
                    send_sem=send_sems.at[k], recv_sem=recv_sems.at[k], device_id=(x, y, 1 - c), device_id_type=MESH)
                cp.start()
                cps.append(cp)
        for cp in cps:
            cp.wait()

    halves = [jax.ShapeDtypeStruct((a.shape[0], a.shape[1] // 2, a.shape[2]), a.dtype) for a in arrs]
    return pl.pallas_call(
        body, name=name, in_specs=[ANY] * na, out_specs=[ANY] * na, out_shape=halves,
        scratch_shapes=[pltpu.SemaphoreType.DMA((ncopies,))] * 2,
    )(*arrs)


def _scatter_chips(name, arrs):
    na = len(arrs)

    def body(*refs):
        cps = _scatter_copies(refs[:na], refs[na:2 * na], *refs[2 * na:])
        for cp in cps:
            cp.start()
        for cp in cps:
            cp.wait()

    return pl.pallas_call(
        body, name=name, in_specs=[ANY] * na, out_specs=[ANY] * na,
        out_shape=_scatter_shapes(arrs), scratch_shapes=_scatter_sems(arrs),
    )(*arrs)


def _scatter_shapes(arrs):
    return [jax.ShapeDtypeStruct((len(CHIP_MASKS),) + a.shape[1:], a.dtype) for a in arrs]


def _scatter_sems(arrs):
    return [pltpu.SemaphoreType.DMA((len(arrs), len(CHIP_MASKS)))] * 2


def _scatter_copies(a_refs, o_refs, send_sems, recv_sems):
    x, y, c = _coords()
    cps = []
    for k in range(len(a_refs)):
        for j, mask in enumerate(CHIP_MASKS):
            px, py, pc = _peer(x, y, c, mask)
            cps.append(pltpu.make_async_remote_copy(
                src_ref=a_refs[k].at[2 * px + py], dst_ref=o_refs[k].at[j],
                send_sem=send_sems.at[k, j], recv_sem=recv_sems.at[k, j],
                device_id=(px, py, pc), device_id_type=MESH))
    return cps


def _join_halves(name, bufs):
    na = len(bufs)

    def body(*refs):
        o_refs = refs[na:2 * na]
        send_sems, recv_sems = refs[2 * na:]
        x, y, c = _coords()
        cps = []
        for k in range(na):
            for l in range(2):
                part = o_refs[k].at[l, c]
                cp = pltpu.make_async_remote_copy(
                    src_ref=part, dst_ref=part, send_sem=send_sems.at[k, l], recv_sem=recv_sems.at[k, l],
                    device_id=(x, y, 1 - c), device_id_type=MESH)
                cp.start()
                cps.append(cp)
        for cp in cps:
            cp.wait()

    return pl.pallas_call(
        body, name=name, in_specs=[ANY] * na, out_specs=[ANY] * na,
        out_shape=[jax.ShapeDtypeStruct(b.shape, b.dtype) for b in bufs],
        input_output_aliases={k: k for k in range(na)}, scratch_shapes=[pltpu.SemaphoreType.DMA((na, 2))] * 2,
    )(*bufs)


BIG = ("w_in", "w_uq", "w_ukv", "w_out", "w_ffn_in", "w_ffn_out")
SMALL = ("conv_w", "conv_b", "lru_wa", "lru_ba", "lru_wx", "lru_bx", "lru_a_param", "q_norm_g", "kv_norm_g",
         "final_norm_g")
SMALL_SHAPES = {"conv_w": (2, CONV, D), "conv_b": (2, D), "lru_wa": (2, NB, BW, BW), "lru_ba": (2, NB, BW),
                "lru_wx": (2, NB, BW, BW), "lru_bx": (2, NB, BW), "lru_a_param": (2, D), "q_norm_g": (2, QR),
                "kv_norm_g": (2, KVR), "final_norm_g": (D,)}
SMALL_ROWS = 640
FFN_CHIP_ORDER = (0, 2, 1, 3)


def _size(shape):
    n = 1
    for s in shape:
        n *= s
    return n


def _pack_rows(parts, rows, dtype):
    flat = jnp.concatenate([p.reshape(-1).astype(dtype) for p in parts])
    pad = rows * 1024 - flat.shape[0]
    if pad:
        flat = jnp.concatenate([flat, jnp.zeros((pad,), dtype)])
    return flat.reshape(rows, 1024)


def _unpack_rows(pack, shapes):
    flat = pack.reshape(-1)
    out, off = [], 0
    for s in shapes:
        n = _size(s)
        out.append(flat[off:off + n].reshape(s))
        off += n
    return out


def _inproj_weights(g_in, g_uq, g_ukv):
    w_in = jnp.concatenate([g_in[s] for s in range(4)], axis=1)
    win_p = jnp.concatenate([w_in[:, :D], w_in[:, 1472:2496], w_in[:, 2496:], w_in[:, D:1472],
                             jnp.zeros((D, 64), w_in.dtype)], axis=1)
    w_uq = g_uq.reshape(QR, NH, DN + DR)
    wq_p = jnp.concatenate([w_uq, jnp.zeros((QR, NH, DH - DN - DR), w_uq.dtype)], axis=2).reshape(QR, NH * DH)
    w_ukv = jnp.transpose(g_ukv.reshape(4, KVR, NH, (DN + DV) // 4), (1, 2, 0, 3)).reshape(KVR, NH, DN + DV)
    wk_p = jnp.concatenate([w_ukv[..., :DN], jnp.zeros((KVR, NH, DH - DN), w_ukv.dtype)], axis=2).reshape(KVR, NH * DH)
    wv = w_ukv[..., DN:].reshape(KVR, NH * DV)
    return dict(win=win_p, wq=wq_p, wk=wk_p, wv=wv)


def _other_weights(g_out, g_fi, g_fo):
    ko, ki, kf = g_out.shape[1] // 2, g_fi.shape[1] // 2, g_fo.shape[1] // 2
    wout = [g_out[:, l * ko:(l + 1) * ko].reshape(4 * ko, D) for l in range(2)]
    wfi = [jnp.concatenate([g_fi[s, l * ki:(l + 1) * ki] for s in FFN_CHIP_ORDER], axis=1) for l in range(2)]
    wfo = [g_fo[:, l * kf:(l + 1) * kf].reshape(4 * kf, D) for l in range(2)]
    return dict(wout=wout, wfi=wfi, wfo=wfo)


def _shard_grads(g):
    dwin = g["win"]
    w_in = jnp.concatenate([dwin[:, :D], dwin[:, QKV0:QKV0 + 448], dwin[:, D:QKV0]], axis=1)
    w_in = jnp.transpose(w_in.reshape(D, 4, DIN // 4), (1, 0, 2))
    w_uq = g["wq"].reshape(QR, NH, DH)[..., :DN + DR].reshape(4, QR // 4, NH * (DN + DR))
    w_ukv = jnp.concatenate([g["wk"].reshape(KVR, NH, DH)[..., :DN], g["wv"].reshape(KVR, NH, DV)], axis=2)
    w_ukv = jnp.transpose(w_ukv.reshape(KVR, NH, 4, (DN + DV) // 4), (2, 0, 1, 3)).reshape(4, KVR, NH * (DN + DV) // 4)
    tiles = jnp.split(g["wfi"], 4, axis=1)
    wfi = jnp.stack([tiles[FFN_CHIP_ORDER.index(s)] for s in range(4)])
    return [w_in, w_uq, w_ukv, g["wout"].reshape(4, D // 4, D), wfi, g["wfo"].reshape(4, DFF // 4, D)]


def kernel(x, c, positions, w_ada, b_ada, w_in, conv_w, conv_b, lru_wa, lru_ba, lru_wx, lru_bx, lru_a_param, q_norm_g, kv_norm_g, w_uq, w_ukv, w_out, w_ffn_in, w_ffn_out, final_norm_g, loss_target, m_w_ada, m_b_ada, m_w_in, m_conv_w, m_conv_b, m_lru_wa, m_lru_ba, m_lru_wx, m_lru_bx, m_lru_a_param, m_q_norm_g, m_kv_norm_g, m_w_uq, m_w_ukv, m_w_out, m_w_ffn_in, m_w_ffn_out, m_final_norm_g, v_w_ada, v_b_ada, v_w_in, v_conv_w, v_conv_b, v_lru_wa, v_lru_ba, v_lru_wx, v_lru_bx, v_lru_a_param, v_q_norm_g, v_kv_norm_g, v_w_uq, v_w_ukv, v_w_out, v_w_ffn_in, v_w_ffn_out, v_final_norm_g):
    T = x.shape[1]
    tb = min(ROW_BLOCK, T)
    tq = min(ATTN_BLOCK, T)
    nq = T // tq
    mx, my, mc = _coords()
    chip = 2 * mx + my
    dev = 4 * mx + 2 * my + mc
    weights = dict(w_ada=w_ada, b_ada=b_ada, w_in=w_in, conv_w=conv_w, conv_b=conv_b, lru_wa=lru_wa, lru_ba=lru_ba,
                   lru_wx=lru_wx, lru_bx=lru_bx, lru_a_param=lru_a_param, q_norm_g=q_norm_g, kv_norm_g=kv_norm_g,
                   w_uq=w_uq, w_ukv=w_ukv, w_out=w_out, w_ffn_in=w_ffn_in, w_ffn_out=w_ffn_out,
                   final_norm_g=final_norm_g)
    mom = dict(w_ada=m_w_ada, b_ada=m_b_ada, w_in=m_w_in, conv_w=m_conv_w, conv_b=m_conv_b, lru_wa=m_lru_wa,
               lru_ba=m_lru_ba, lru_wx=m_lru_wx, lru_bx=m_lru_bx, lru_a_param=m_lru_a_param, q_norm_g=m_q_norm_g,
               kv_norm_g=m_kv_norm_g, w_uq=m_w_uq, w_ukv=m_w_ukv, w_out=m_w_out, w_ffn_in=m_w_ffn_in,
               w_ffn_out=m_w_ffn_out, final_norm_g=m_final_norm_g)
    var = dict(w_ada=v_w_ada, b_ada=v_b_ada, w_in=v_w_in, conv_w=v_conv_w, conv_b=v_conv_b, lru_wa=v_lru_wa,
               lru_ba=v_lru_ba, lru_wx=v_lru_wx, lru_bx=v_lru_bx, lru_a_param=v_lru_a_param, q_norm_g=v_q_norm_g,
               kv_norm_g=v_kv_norm_g, w_uq=v_w_uq, w_ukv=v_w_ukv, w_out=v_w_out, w_ffn_in=v_w_ffn_in,
               w_ffn_out=v_w_ffn_out, final_norm_g=v_final_norm_g)
    order = list(weights)

    ns_ada = w_ada.shape[2]
    c_all = _all_gather("gather_c", c, ALL_MASKS, True).reshape(8, D)
    b_sh = lax.dynamic_slice_in_dim(b_ada, chip * ns_ada, ns_ada, axis=1).reshape(2, 1, ns_ada)
    mod_sh = _ada_fwd("ada_fwd", c_all, w_ada, b_sh)
    mod_all = _all_gather("gather_mod", mod_sh.reshape(16, ns_ada), CHIP_MASKS, False)
    mod_mine = lax.dynamic_index_in_dim(mod_all.reshape(4, 2, 8, ns_ada), dev, axis=2, keepdims=False)
    mod = jnp.transpose(mod_mine, (1, 0, 2)).reshape(2, 6, 1, D)

    flat = lambda w: w.astype(bf16).reshape(w.shape[0], w.shape[1], -1)
    b_in, b_uq, b_ukv = flat(w_in), flat(w_uq), flat(w_ukv)
    late = [b_in[1], b_uq[1], b_ukv[1]] + [flat(w).reshape(-1, w.shape[-1]) for w in (w_out, w_ffn_in, w_ffn_out)]
    W0 = _inproj_weights(*_gather_weights("gather_weights_early", [b_in[0], b_uq[0], b_ukv[0]]))
    W = {n: [W0[n], None] for n in W0}
    wa_b = lru_wa.astype(bf16)
    wx_b = lru_wx.astype(bf16)

    pos = positions.reshape(T, 1)
    invf = ROPE_THETA ** (-jnp.arange(0, DR, 2, dtype=f32) / DR)
    invf_p = jnp.concatenate([jnp.zeros((DN,), f32), invf, invf, jnp.zeros((DH - DN - DR,), f32)]).reshape(1, DH)
    cosp, sap, sbp = _rope_tables("rope_tables", pos, invf_p, T, tb)

    ns_cw = conv_w.shape[2]
    cw_all = _all_gather("gather_conv_w", conv_w.reshape(2 * CONV, ns_cw), CHIP_MASKS, False)
    cw_full = jnp.transpose(cw_all, (1, 0, 2)).reshape(2, CONV, D)
    vec = lambda a, l: a[l].reshape(1, -1)

    xl = x.reshape(T, D)
    saved = []
    h1 = _norm_mod("norm_mod_0", xl, mod[0, 1], mod[0, 0], T, tb)
    for l in range(2):
        sh_m, sc_m, g_m, sh_f, sc_f, g_f = [mod[l, j] for j in range(6)]
        proj = _matmul(f"proj_{l}", h1, W["win"][l], "nn")
        u, r, ig, ya = _lru_fwd(f"lru_fwd_{l}", proj, pos, cw_full[l], vec(conv_b, l), wa_b[l], vec(lru_ba, l),
                                wx_b[l], vec(lru_bx, l), vec(lru_a_param, l), T, tb)
        qp, kp, vp, cq, ckv = _mla_prep(f"mla_prep_{l}", proj, cosp, sap, sbp, vec(q_norm_g, l), vec(kv_norm_g, l),
                                        W["wq"][l], W["wk"][l], W["wv"][l], T, tb)
        if l == 0:
            o, lse, *late_all = _flash_fwd("flash_fwd_0", qp, kp, vp, T, tq, late)
            W1 = _inproj_weights(*late_all[:3])
            for n in W1:
                W[n][1] = W1[n]
            W.update(_other_weights(*late_all[3:]))
        else:
            o, lse = _flash_fwd("flash_fwd_1", qp, kp, vp, T, tq)
        y = _gate_mix(f"gate_mix_{l}", proj, ya, o, T, tb)
        mo = _matmul(f"out_proj_{l}", y, W["wout"][l], "nn")
        x2, h2 = _resid_norm_mod(f"resid_norm_f_{l}", xl, g_m, mo, sc_f, sh_f, T, tb)
        gu, act = _ffn_in_swiglu(f"ffn_in_{l}", h2, W["wfi"][l], T)
        fo = _matmul(f"ffn_out_{l}", act, W["wfo"][l], "nn")
        saved.append(dict(x=xl, h1=h1, proj=proj, u=u, r=r, ig=ig, ya=ya, qp=qp, kp=kp, vp=vp, cq=cq, ckv=ckv, o=o,
                          lse=lse, y=y, mo=mo, x2=x2, h2=h2, gu=gu, act=act, fo=fo))
        if l == 0:
            xl, h1 = _resid_norm_mod("resid_norm_m_1", x2, g_f, fo, mod[1, 1], mod[1, 0], T, tb)

    dx, dfo, loss_part, dgfin, dg_f = _final("final", saved[1]["x2"], mod[1, 5], saved[1]["fo"],
                                             final_norm_g.reshape(1, D), loss_target.reshape(T, D), T, tb)
    loss = lax.psum(loss_part[0, 0], ("x", "y", "c"))

    gl = {n: [None, None] for n in ("win", "wq", "wk", "wv", "wout", "wfi", "wfo", "conv_w", "conv_b", "lru_wa",
                                    "lru_ba", "lru_wx", "lru_bx", "lru_a_param", "q_norm_g", "kv_norm_g")}
    dmod = [None, None]

    def pair_sums(l, small):
        arrs = _shard_grads({n: gl[n][l] for n in ("win", "wq", "wk", "wv", "wout", "wfi", "wfo")})
        arrs += [small] if small is not None else []
        return _pair_sum(f"rs_pair_sum_{l}", arrs, _swap_halves(f"rs_swap_halves_{l}", arrs))

    def chip_sums(l, sums, lands, bufs):
        nb = len(BIG)
        out = _chip_sum(f"rs_chip_sum_{l}", sums[:nb], lands[:nb], l, bufs)
        small = _chip_sum("rs_chip_sum_small", sums[nb:], lands[nb:], None)[0] if len(sums) > nb else None
        return out, small

    for l in (1, 0):
        s = saved[l]
        sh_m, sc_m, g_m, sh_f, sc_f, g_f = [mod[l, j] for j in range(6)]
        dgu = _dact_swiglu_bwd(f"d_act_{l}", dfo, W["wfo"][l], s["gu"], T)
        gl["wfo"][l] = _matmul(f"dw_ffn_out_{l}", s["act"], dfo, "tn", bf16)
        dh2 = _matmul(f"d_h2_{l}", dgu, W["wfi"][l], "nt")
        gl["wfi"][l] = _matmul(f"dw_ffn_in_{l}", s["h2"], dgu, "tn", bf16)
        dx2, dmo, dsc_f, dsh_f, dg_m = _norm_mod_bwd(f"norm_bwd_f_{l}", dh2, s["x2"], sc_f, dx, T, tb, (g_m, s["mo"]))
        dy = _matmul(f"d_y_{l}", dmo, W["wout"][l], "nt")
        gl["wout"][l] = _matmul(f"dw_out_{l}", s["y"], dmo, "tn", bf16)
        dga, dgb, dya, do, dlt = _gate_mix_bwd(f"gate_mix_bwd_{l}", dy, s["proj"], s["ya"], s["o"], T, tb)
        lse_r = s["lse"].reshape(NH, nq, 1, tq)
        dl_r = jnp.transpose(dlt).reshape(NH, nq, 1, tq)
        if l == 0:
            dk, dv, dq, *lands1 = _flash_bwd("flash_bwd_0", s["qp"], s["kp"], s["vp"], do, lse_r, dl_r, T, tq, sums1)
            rs_bufs, _ = chip_sums(1, sums1, lands1, None)
        else:
            dk, dv, dq = _flash_bwd("flash_bwd_1", s["qp"], s["kp"], s["vp"], do, lse_r, dl_r, T, tq)
        dqkv, dqp, dkp, dgq, dgkv = _mla_bwd(f"mla_bwd_{l}", dq, dk, dv, s["proj"], cosp, sap, sbp,
                                              vec(q_norm_g, l), vec(kv_norm_g, l), W["wq"][l], W["wk"][l], W["wv"][l],
                                              T, tb)
        gl["wq"][l] = _matmul(f"dw_uq_{l}", s["cq"], dqp, "tn", bf16)
        gl["wk"][l] = _matmul(f"dw_uk_{l}", s["ckv"], dkp, "tn", bf16)
        gl["wv"][l] = _matmul(f"dw_uv_{l}", s["ckv"], dv, "tn", bf16)
        dxl, dcw, dcb, dwa, dba, dwx, dbx, dsp = _lru_bwd(
            f"lru_bwd_{l}", dya, s["proj"], s["u"], s["r"], s["ig"], s["ya"], pos, cw_full[l], wa_b[l], wx_b[l],
            vec(lru_a_param, l), T, tb)
        dproj = jnp.concatenate([dxl, dga, dgb, dqkv], axis=1)
        dh1 = _matmul(f"d_h1_{l}", dproj, W["win"][l], "nt")
        gl["win"][l] = _matmul(f"dw_in_{l}", s["h1"], dproj, "tn", bf16)
        if l == 1:
            dx, dfo_below, dsc_m, dsh_m, dg_f_below = _norm_mod_bwd(
                "norm_bwd_m_1", dh1, s["x"], sc_m, dx2, T, tb, (mod[0, 5], saved[0]["fo"]))
        else:
            dx, dsc_m, dsh_m = _norm_mod_bwd("norm_bwd_m_0", dh1, s["x"], sc_m, dx2, T, tb)
        dmod[l] = jnp.concatenate([dsh_m, dsc_m, dg_m, dsh_f, dsc_f, dg_f], axis=1)
        gl["conv_w"][l], gl["conv_b"][l] = dcw, dcb[0]
        gl["lru_wa"][l], gl["lru_ba"][l], gl["lru_wx"][l], gl["lru_bx"][l] = dwa, dba[0], dwx, dbx[0]
        gl["lru_a_param"][l] = dsp[0]
        gl["q_norm_g"][l], gl["kv_norm_g"][l] = dgq[0], dgkv[0]
        if l == 1:
            sums1 = pair_sums(1, None)
            dfo, dg_f = dfo_below, dg_f_below
    grad_x = dx.reshape(1, T, D)
    gfull = {n: jnp.stack(gl[n]) for n in SMALL if n != "final_norm_g"}
    gfull["final_norm_g"] = dgfin[0]

    dmod_all = _all_gather("gather_dmod", jnp.concatenate(dmod, axis=1), ALL_MASKS, True).reshape(8, 2, 6, D)
    dmod_sum = _sum_slots("sum_dmod", dmod_all.reshape(8, 12, D)).reshape(2, 6 * D)
    dm_sh = lax.dynamic_slice_in_dim(jnp.transpose(dmod_all.reshape(8, 2, 6 * D), (1, 0, 2)), chip * ns_ada, ns_ada, axis=2)
    grads = {"w_ada": _ada_bwd("ada_bwd", jnp.transpose(c_all), dm_sh), "b_ada": dmod_sum}

    gsmall = _pack_rows([gfull[n] for n in SMALL], SMALL_ROWS, f32).reshape(4, SMALL_ROWS // 4, 1024)
    sums0 = pair_sums(0, gsmall)
    bufs, tot_s = chip_sums(0, sums0, _scatter_chips("rs_scatter_chips_0", sums0), rs_bufs)
    bufs = _join_halves("rs_join_halves", bufs)
    small_all = _all_gather("gather_small_grads", tot_s, ALL_MASKS, True).reshape(SMALL_ROWS, 1024)
    for n, buf in zip(BIG, bufs):
        grads[n] = buf.reshape(weights[n].shape)
    for n, gs in zip(SMALL, _unpack_rows(small_all, [SMALL_SHAPES[n] for n in SMALL])):
        grads[n] = gs
    grads["conv_w"] = lax.dynamic_slice_in_dim(grads["conv_w"], chip * ns_cw, ns_cw, axis=2)

    deltas, new_m, new_v = {}, {}, {}
    for n in order:
        w = weights[n]
        view = (1, w.shape[0]) if w.ndim == 1 else (_size(w.shape[:-1]), w.shape[-1]) if w.shape[-1] >= 128 else (w.shape[0], _size(w.shape[1:]))
        d, m2, v2 = _adamw(f"adamw_{n}", w.reshape(view), grads[n].reshape(view), mom[n].reshape(view), var[n].reshape(view))
        deltas[n], new_m[n], new_v[n] = d.reshape(w.shape), m2.reshape(w.shape), v2.reshape(w.shape)
    return (loss, grad_x, *[grads[n] for n in order], *[deltas[n] for n in order], *[new_m[n] for n in order],
            *[new_v[n] for n in order])
```

```python
import functools

import jax
import jax.numpy as jnp
from jax import lax
from jax.experimental import pallas as pl
from jax.experimental.pallas import tpu as pltpu

f32 = jnp.float32
bf16 = jnp.bfloat16

D = 1024
NH = 8
DN = 128
DR = 64
DV = 128
DH = 256
QR = 256
KVR = 128
DFF = 2816
FH = DFF // 2
NB = 8
BW = 128
CONV = 4
PW = 3584
QKV0 = 3 * D
DIN = 3520
EPS = 1e-6
LRU_C = 8.0
ROPE_THETA = 10000.0
SCALE = (DN + DR) ** -0.5
LOG2E = 1.4426950408889634
LN2 = 0.6931471805599453
QSCALE = SCALE * LOG2E
NEG = -1e30
HPS = 2

ADAM_LR = 0.001
ADAM_B1 = 0.9
ADAM_B2 = 0.999
ADAM_EPS = 1e-08
ADAM_WD = 0.01
ADAM_STEP = 10

VMEM_LIMIT = 56 * 1024 * 1024
ROW_BLOCK = 256
ATTN_BLOCK = 512
MESH = pl.DeviceIdType.MESH
ANY = pl.BlockSpec(memory_space=pl.ANY)
CHIP_MASKS = ((1, 0, 0), (0, 1, 0), (1, 1, 0))
ALL_MASKS = ((0, 0, 1), (1, 0, 0), (0, 1, 0), (1, 1, 0), (1, 0, 1), (0, 1, 1), (1, 1, 1))


def _cparams(n_axes):
    return pltpu.CompilerParams(dimension_semantics=("arbitrary",) * n_axes, vmem_limit_bytes=VMEM_LIMIT)


def _pick(n, cands):
    for c in cands:
        if n % c == 0:
            return c
    return n


def _rowwise(name, body, T, tb, row_ins, vec_ins, row_outs, acc_outs=(), scratch=(), reverse=False):
    n = T // tb

    def rmap(i):
        return (n - 1 - i) if reverse else i

    in_specs = []
    for (_, w, cb) in row_ins:
        in_specs.append(pl.BlockSpec((tb, w), functools.partial(lambda i, cb: (rmap(i), cb), cb=cb)))
    for v in vec_ins:
        in_specs.append(pl.BlockSpec(v.shape, functools.partial(lambda i, nd: (0,) * nd, nd=v.ndim)))
    out_specs, out_shape = [], []
    for (w, dt) in row_outs:
        out_specs.append(pl.BlockSpec((tb, w), lambda i: (rmap(i), 0)))
        out_shape.append(jax.ShapeDtypeStruct((T, w), dt))
    for (s, dt) in acc_outs:
        out_specs.append(pl.BlockSpec(s, functools.partial(lambda i, nd: (0,) * nd, nd=len(s))))
        out_shape.append(jax.ShapeDtypeStruct(s, dt))

    def kern(*refs):
        body(*refs)

    return pl.pallas_call(
        kern, name=name, grid=(n,), in_specs=in_specs, out_specs=out_specs, out_shape=out_shape,
        scratch_shapes=list(scratch), compiler_params=_cparams(1),
    )(*[a for (a, _, _) in row_ins], *vec_ins)


def _matmul(name, a, b, mode, out_dtype=f32):
    if mode == "nn":
        (M, K), N = a.shape, b.shape[1]
    elif mode == "nt":
        (M, K), N = a.shape, b.shape[0]
    else:
        (K, M), N = a.shape, b.shape[1]
    wide = (1792, 1408, 1024, 512, 256, 128)
    tm = _pick(M, (512, 256, 128)) if mode != "tn" else _pick(M, (1024, 1408, 512, 256, 128))
    tn = _pick(N, wide)
    tk = K if mode != "tn" else _pick(K, (1024, 512, 256, 128))
    nk = K // tk
    dims = {"nn": (((1,), (0,)), ((), ())), "nt": (((1,), (1,)), ((), ())), "tn": (((0,), (0,)), ((), ()))}[mode]
    if mode == "tn":
        a_spec = pl.BlockSpec((tk, tm), lambda j, i, k: (k, i))
    else:
        a_spec = pl.BlockSpec((tm, tk), lambda j, i, k: (i, k))
    once = {"pipeline_mode": pl.Buffered(1)} if (mode != "tn" and N == tn) else {}
    if mode == "nt":
        b_spec = pl.BlockSpec((tn, tk), lambda j, i, k: (j, k), **once)
    else:
        b_spec = pl.BlockSpec((tk, tn), lambda j, i, k: (k, j), **once)
    o_spec = pl.BlockSpec((tm, tn), lambda j, i, k: (i, j))
    use_acc = nk > 1 and out_dtype != f32

    def kern(a_ref, b_ref, o_ref, *scr):
        k = pl.program_id(2)
        d = lax.dot_general(a_ref[...].astype(bf16), b_ref[...].astype(bf16), dims, preferred_element_type=f32)
        if nk == 1:
            o_ref[...] = d.astype(out_dtype)
        else:
            acc = scr[0] if use_acc else o_ref

            @pl.when(k == 0)
            def _():
                acc[...] = d

            @pl.when(k > 0)
            def _():
                acc[...] += d

            if use_acc:
                @pl.when(k == nk - 1)
                def _():
                    o_ref[...] = acc[...].astype(out_dtype)

    return pl.pallas_call(
        kern, name=name, grid=(N // tn, M // tm, nk), in_specs=[a_spec, b_spec], out_specs=o_spec,
        out_shape=jax.ShapeDtypeStruct((M, N), out_dtype),
        scratch_shapes=[pltpu.VMEM((tm, tn), f32)] if use_acc else [],
        compiler_params=_cparams(3),
    )(a, b)


def _colsum(v):
    return jnp.sum(v, axis=0, keepdims=True)


def _init_acc(step, *refs):
    @pl.when(step == 0)
    def _():
        for r in refs:
            r[...] = jnp.zeros_like(r)


def _norm_mod(name, x, sc, sh, T, tb):
    def body(x_ref, sc_ref, sh_ref, h_ref):
        xv = x_ref[...]
        rstd = lax.rsqrt(jnp.mean(xv * xv, axis=-1, keepdims=True) + EPS)
        h_ref[...] = (xv * rstd * (1.0 + sc_ref[...]) + sh_ref[...]).astype(bf16)

    return _rowwise(name, body, T, tb, [(x, D, 0)], [sc, sh], [(D, bf16)])[0]


def _resid_norm_mod(name, x, g, m, sc, sh, T, tb):
    def body(x_ref, m_ref, g_ref, sc_ref, sh_ref, x2_ref, h_ref):
        xv = x_ref[...] + g_ref[...] * m_ref[...]
        x2_ref[...] = xv
        rstd = lax.rsqrt(jnp.mean(xv * xv, axis=-1, keepdims=True) + EPS)
        h_ref[...] = (xv * rstd * (1.0 + sc_ref[...]) + sh_ref[...]).astype(bf16)

    return _rowwise(name, body, T, tb, [(x, D, 0), (m, D, 0)], [g, sc, sh], [(D, f32), (D, bf16)])


def _norm_mod_bwd(name, dh, x, sc, dres, T, tb, resid=None):
    def norm_part(dh_ref, x_ref, dr_ref, sc_ref, dx_ref, dsc_ref, dsh_ref):
        xv = x_ref[...]
        dhv = dh_ref[...]
        rstd = lax.rsqrt(jnp.mean(xv * xv, axis=-1, keepdims=True) + EPS)
        nv = xv * rstd
        dn = dhv * (1.0 + sc_ref[...])
        dxv = dr_ref[...] + rstd * (dn - nv * jnp.mean(dn * nv, axis=-1, keepdims=True))
        dx_ref[...] = dxv
        dsc_ref[...] += _colsum(dhv * nv)
        dsh_ref[...] += _colsum(dhv)
        return dxv

    if resid is None:
        def body(dh_ref, x_ref, dr_ref, sc_ref, dx_ref, dsc_ref, dsh_ref):
            _init_acc(pl.program_id(0), dsc_ref, dsh_ref)
            norm_part(dh_ref, x_ref, dr_ref, sc_ref, dx_ref, dsc_ref, dsh_ref)

        return _rowwise(name, body, T, tb, [(dh, D, 0), (x, D, 0), (dres, D, 0)], [sc], [(D, f32)],
                        [((1, D), f32)] * 2)

    def body_resid(dh_ref, x_ref, dr_ref, m_ref, sc_ref, g_ref, dx_ref, dm_ref, dsc_ref, dsh_ref, dg_ref):
        _init_acc(pl.program_id(0), dsc_ref, dsh_ref, dg_ref)
        dxv = norm_part(dh_ref, x_ref, dr_ref, sc_ref, dx_ref, dsc_ref, dsh_ref)
        dm_ref[...] = (dxv * g_ref[...]).astype(bf16)
        dg_ref[...] += _colsum(dxv * m_ref[...])

    g, m = resid
    return _rowwise(name, body_resid, T, tb, [(dh, D, 0), (x, D, 0), (dres, D, 0), (m, D, 0)], [sc, g],
                    [(D, f32), (D, bf16)], [((1, D), f32)] * 3)


def _expm1_neg(y):
    poly = y * (1.0 + y * (0.5 + y * (1.0 / 6 + y * (1.0 / 24))))
    return jnp.where(y > -0.05, poly, jnp.exp(y) - 1.0)


def _lru_gates(r, sp, reset):
    log_a = -LRU_C * r * sp
    a = jnp.where(reset, 0.0, jnp.exp(log_a))
    mult = jnp.where(reset, 1.0, jnp.sqrt(-_expm1_neg(2.0 * log_a)))
    return a, mult


def _block_dot(v, w_ref, dims):
    outs = [lax.dot_general(v[:, n * BW:(n + 1) * BW], w_ref[n], dims, preferred_element_type=f32) for n in range(NB)]
    return jnp.concatenate(outs, axis=1)


def _lru_fwd(name, proj, pos, cw, cb, wa, ba, wx, bx, ap, T, tb):
    nsteps = tb.bit_length() - 1

    def body(x_ref, pos_ref, cw_ref, cb_ref, wa_ref, ba_ref, wx_ref, bx_ref, ap_ref,
             u_ref, r_ref, i_ref, h_ref, prev_x, carry_h):
        _init_acc(pl.program_id(0), prev_x, carry_h)
        x = x_ref[...]
        px = prev_x[...]
        row = lax.broadcasted_iota(jnp.int32, (tb, 1), 0)
        u = x * cw_ref[CONV - 1:CONV, :] + cb_ref[...]
        for j in range(1, CONV):
            z = jnp.where(row >= tb - j, px, x)
            u = u + pltpu.roll(z, j, 0) * cw_ref[CONV - 1 - j:CONV - j, :]
        ub = u.astype(bf16)
        nn = (((1,), (0,)), ((), ()))
        r = jax.nn.sigmoid(_block_dot(ub, wa_ref, nn) + ba_ref[...])
        ig = jax.nn.sigmoid(_block_dot(ub, wx_ref, nn) + bx_ref[...])
        sp = jax.nn.softplus(-ap_ref[...])
        reset = pos_ref[...] == 0
        a, mult = _lru_gates(r, sp, reset)
        b = u * ig * mult
        sub = row & 7
        for s in (1, 2, 4):
            keep = sub >= s
            a_sh = jnp.where(keep, pltpu.roll(a, s, 0), 1.0)
            b_sh = jnp.where(keep, pltpu.roll(b, s, 0), 0.0)
            b = a * b_sh + b
            a = a * a_sh
        carry = carry_h[...]
        for g in range(tb // 8):
            rows = slice(8 * g, 8 * g + 8)
            hg = b[rows] + a[rows] * carry
            h_ref[rows, :] = hg
            carry = hg[7:8]
        carry_h[...] = carry
        u_ref[...] = u
        r_ref[...] = r
        i_ref[...] = ig
        prev_x[...] = x

    return _rowwise(name, body, T, tb, [(proj, D, 0), (pos, 1, 0)], [cw, cb, wa, ba, wx, bx, ap],
                    [(D, f32)] * 4, scratch=[pltpu.VMEM((tb, D), f32), pltpu.VMEM((1, D), f32)])


def _lru_bwd(name, dh, proj, u, r, ig, h, pos, cw, wa, wx, ap, T, tb):
    n = T // tb
    nsteps = tb.bit_length() - 1
    t8 = tb // 8

    def rmap(i):
        return n - 1 - i

    def body(dh_ref, x_ref, u_ref, r_ref, i_ref, h_ref, hp_ref, pos_ref, cw_ref, wa_ref, wx_ref, ap_ref,
             dx_ref, dcw_ref, dcb_ref, dwa_ref, dba_ref, dwx_ref, dbx_ref, dsp_ref, next_du, carry_g, carry_a):
        step = pl.program_id(0)
        _init_acc(step, dcw_ref, dcb_ref, dwa_ref, dba_ref, dwx_ref, dbx_ref, dsp_ref, next_du, carry_g, carry_a)
        blk = n - 1 - step
        x = x_ref[...]
        u = u_ref[...]
        r = r_ref[...]
        ig = i_ref[...]
        h = h_ref[...]
        row = lax.broadcasted_iota(jnp.int32, (tb, 1), 0)
        sp = jax.nn.softplus(-ap_ref[...])
        reset = pos_ref[...] == 0
        a, mult = _lru_gates(r, sp, reset)
        A = pltpu.roll(jnp.where(row == 0, carry_a[...], a), tb - 1, 0)
        B = dh_ref[...]
        sub = row & 7
        for s in (1, 2, 4):
            keep = sub < 8 - s
            a_sh = jnp.where(keep, pltpu.roll(A, tb - s, 0), 1.0)
            b_sh = jnp.where(keep, pltpu.roll(B, tb - s, 0), 0.0)
            B = B + A * b_sh
            A = A * a_sh
        carry = carry_g[...]
        parts = [None] * (tb // 8)
        for grp in reversed(range(tb // 8)):
            rows = slice(8 * grp, 8 * grp + 8)
            parts[grp] = B[rows] + A[rows] * carry
            carry = parts[grp][0:1]
        g = jnp.concatenate(parts, axis=0)
        carry_g[...] = carry
        carry_a[...] = a[0:1, :]
        h_last = jnp.where(blk == 0, 0.0, hp_ref[7:8, :])
        h_prev = pltpu.roll(jnp.where(row == tb - 1, h_last, h), 1, 0)
        da = g * h_prev
        gu = g * u
        dlog_a = jnp.where(reset, 0.0, da * a - gu * ig * (a * a) / mult)
        du = g * ig * mult
        di = gu * mult
        dr = dlog_a * (-LRU_C * sp)
        dsp_ref[...] += _colsum(dlog_a * (-LRU_C * r)) * (-jax.nn.sigmoid(-ap_ref[...]))
        dzr = dr * r * (1.0 - r)
        dzi = di * ig * (1.0 - ig)
        dzr_b = dzr.astype(bf16)
        dzi_b = dzi.astype(bf16)
        nt = (((1,), (1,)), ((), ()))
        du = du + _block_dot(dzr_b, wa_ref, nt) + _block_dot(dzi_b, wx_ref, nt)
        ub = u.astype(bf16)
        tn = (((0,), (0,)), ((), ()))
        for nb in range(NB):
            sl = slice(nb * BW, (nb + 1) * BW)
            dwa_ref[nb] += lax.dot_general(ub[:, sl], dzr_b[:, sl], tn, preferred_element_type=f32)
            dwx_ref[nb] += lax.dot_general(ub[:, sl], dzi_b[:, sl], tn, preferred_element_type=f32)
        dba_ref[...] += _colsum(dzr)
        dbx_ref[...] += _colsum(dzi)
        dcb_ref[...] += _colsum(du)
        ndu = next_du[...]
        dx = du * cw_ref[CONV - 1:CONV, :]
        dcw_ref[CONV - 1:CONV, :] += _colsum(x * du)
        for j in range(1, CONV):
            up = pltpu.roll(jnp.where(row < j, ndu, du), tb - j, 0)
            dx = dx + up * cw_ref[CONV - 1 - j:CONV - j, :]
            dcw_ref[CONV - 1 - j:CONV - j, :] += _colsum(x * up)
        dx_ref[...] = dx.astype(bf16)
        next_du[...] = du

    row_specs = [pl.BlockSpec((tb, D), lambda i: (rmap(i), 0)) for _ in range(6)]
    in_specs = row_specs + [
        pl.BlockSpec((8, D), lambda i: (jnp.maximum(rmap(i) * t8 - 1, 0), 0)),
        pl.BlockSpec((tb, 1), lambda i: (rmap(i), 0)),
        pl.BlockSpec((CONV, D), lambda i: (0, 0)),
        pl.BlockSpec((NB, BW, BW), lambda i: (0, 0, 0)),
        pl.BlockSpec((NB, BW, BW), lambda i: (0, 0, 0)),
        pl.BlockSpec((1, D), lambda i: (0, 0)),
    ]
    vec = lambda s: pl.BlockSpec(s, functools.partial(lambda i, nd: (0,) * nd, nd=len(s)))
    acc_shapes = [(CONV, D), (1, D), (NB, BW, BW), (1, D), (NB, BW, BW), (1, D), (1, D)]
    return pl.pallas_call(
        body, name=name, grid=(n,), in_specs=in_specs,
        out_specs=[pl.BlockSpec((tb, D), lambda i: (rmap(i), 0))] + [vec(s) for s in acc_shapes],
        out_shape=[jax.ShapeDtypeStruct((T, D), bf16)] + [jax.ShapeDtypeStruct(s, f32) for s in acc_shapes],
        scratch_shapes=[pltpu.VMEM((tb, D), f32), pltpu.VMEM((1, D), f32), pltpu.VMEM((1, D), f32)],
        compiler_params=_cparams(1),
    )(dh, proj, u, r, ig, h, h, pos, cw, wa, wx, ap)


def _rope_tables(name, pos, invf, T, tb):
    def body(pos_ref, invf_ref, cos_ref, sa_ref, sb_ref):
        ang = pos_ref[...].astype(f32) * invf_ref[...]
        lane = lax.broadcasted_iota(jnp.int32, (tb, DH), 1)
        first = (lane >= DN) & (lane < DN + DR // 2)
        second = (lane >= DN + DR // 2) & (lane < DN + DR)
        c = jnp.cos(ang)
        s = jnp.sin(ang)
        cos_ref[...] = jnp.where(lane < DN, 1.0, jnp.where(first | second, c, 0.0))
        sa_ref[...] = jnp.where(first, -s, 0.0)
        sb_ref[...] = jnp.where(second, s, 0.0)

    return _rowwise(name, body, T, tb, [(pos, 1, 0)], [invf], [(DH, f32)] * 3)


def _rope_heads(z, cos, sa, sb, transposed=False, shared=None):
    half = DR // 2
    parts = []
    for hd in range(NH):
        hi = z[:, hd * DH + DN:(hd + 1) * DH]
        if shared is not None:
            hi = hi + shared
        if transposed:
            hi = hi * cos + pltpu.roll(hi * sa, half, 1) + pltpu.roll(hi * sb, DN - half, 1)
        else:
            hi = hi * cos + pltpu.roll(hi, DN - half, 1) * sa + pltpu.roll(hi, half, 1) * sb
        parts += [z[:, hd * DH:hd * DH + DN], hi]
    return parts


def _mla_prep(name, proj, cosp, sap, sbp, gq, gkv, wq, wk, wv, T, tb):
    def body(qkv_ref, cos_ref, sa_ref, sb_ref, gq_ref, gkv_ref, wq_ref, wk_ref, wv_ref,
             q_ref, k_ref, v_ref, cq_ref, ckv_ref):
        qkv = qkv_ref[...]
        qd = qkv[:, :QR]
        kvd = qkv[:, QR:QR + KVR]
        slab = qkv[:, QR + KVR:]
        nq = qd * lax.rsqrt(jnp.mean(qd * qd, axis=-1, keepdims=True) + EPS)
        cq = (nq * gq_ref[...]).astype(bf16)
        cq_scaled = (nq * (gq_ref[...] * QSCALE)).astype(bf16)
        ckv = (kvd * lax.rsqrt(jnp.mean(kvd * kvd, axis=-1, keepdims=True) + EPS) * gkv_ref[...]).astype(bf16)
        cos, sa, sb = cos_ref[:, DN:], sa_ref[:, DN:], sb_ref[:, DN:]
        qpre = jnp.dot(cq_scaled, wq_ref[...], preferred_element_type=f32)
        kpre = jnp.dot(ckv, wk_ref[...], preferred_element_type=f32)
        q_ref[...] = jnp.concatenate(_rope_heads(qpre, cos, sa, sb), axis=1).astype(bf16)
        k_ref[...] = jnp.concatenate(_rope_heads(kpre, cos, sa, sb, shared=slab), axis=1).astype(bf16)
        v_ref[...] = jnp.dot(ckv, wv_ref[...], preferred_element_type=f32).astype(bf16)
        cq_ref[...] = cq
        ckv_ref[...] = ckv

    return _rowwise(name, body, T, tb,
                    [(proj, 512, QKV0 // 512), (cosp, DH, 0), (sap, DH, 0), (sbp, DH, 0)],
                    [gq, gkv, wq, wk, wv],
                    [(NH * DH, bf16), (NH * DH, bf16), (NH * DV, bf16), (QR, bf16), (KVR, bf16)])


def _mla_bwd(name, dq, dk, dv, proj, cosp, sap, sbp, gq, gkv, wq, wk, wv, T, tb):
    def body(dq_ref, dk_ref, dv_ref, qkv_ref, cos_ref, sa_ref, sb_ref, gq_ref, gkv_ref, wq_ref, wk_ref, wv_ref,
             dqkv_ref, dqp_ref, dkp_ref, dgq_ref, dgkv_ref):
        _init_acc(pl.program_id(0), dgq_ref, dgkv_ref)
        cos, sa, sb = cos_ref[:, DN:], sa_ref[:, DN:], sb_ref[:, DN:]
        dqp_b = jnp.concatenate(_rope_heads(dq_ref[...] * SCALE, cos, sa, sb, transposed=True), axis=1).astype(bf16)
        dk_parts = _rope_heads(dk_ref[...], cos, sa, sb, transposed=True)
        dkp_b = jnp.concatenate(dk_parts, axis=1).astype(bf16)
        dqp_ref[...] = dqp_b
        dkp_ref[...] = dkp_b
        nt = (((1,), (1,)), ((), ()))
        dcq = lax.dot_general(dqp_b, wq_ref[...], nt, preferred_element_type=f32)
        dckv = (lax.dot_general(dkp_b, wk_ref[...], nt, preferred_element_type=f32)
                + lax.dot_general(dv_ref[...], wv_ref[...], nt, preferred_element_type=f32))
        dslab = dk_parts[1]
        for hd in range(1, NH):
            dslab = dslab + dk_parts[2 * hd + 1]
        qkv = qkv_ref[...]
        qd = qkv[:, :QR]
        kvd = qkv[:, QR:QR + KVR]
        nq = qd * lax.rsqrt(jnp.mean(qd * qd, axis=-1, keepdims=True) + EPS)
        nkv = kvd * lax.rsqrt(jnp.mean(kvd * kvd, axis=-1, keepdims=True) + EPS)
        dgq_ref[...] += _colsum(dcq * nq)
        dgkv_ref[...] += _colsum(dckv * nkv)
        dnq = dcq * gq_ref[...]
        dnkv = dckv * gkv_ref[...]
        rq = lax.rsqrt(jnp.mean(qd * qd, axis=-1, keepdims=True) + EPS)
        rkv = lax.rsqrt(jnp.mean(kvd * kvd, axis=-1, keepdims=True) + EPS)
        dqd = rq * (dnq - nq * jnp.mean(dnq * nq, axis=-1, keepdims=True))
        dkvd = rkv * (dnkv - nkv * jnp.mean(dnkv * nkv, axis=-1, keepdims=True))
        dqkv_ref[...] = jnp.concatenate([dqd, dkvd, dslab], axis=1).astype(bf16)

    return _rowwise(name, body, T, tb,
                    [(dq, NH * DH, 0), (dk, NH * DH, 0), (dv, NH * DV, 0), (proj, 512, QKV0 // 512),
                     (cosp, DH, 0), (sap, DH, 0), (sbp, DH, 0)],
                    [gq, gkv, wq, wk, wv],
                    [(512, bf16), (NH * DH, bf16), (NH * DH, bf16)],
                    [((1, QR), f32), ((1, KVR), f32)])


def _flash_fwd(name, q, k, v, T, tq, pack=None):
    nq = T // tq
    unroll = 2

    def body(q_ref, k_ref, v_ref, o_ref, lse_ref):
        i = pl.program_id(1)
        nt = (((1,), (1,)), ((), ()))
        qs = [q_ref[:, hh * DH:(hh + 1) * DH] for hh in range(HPS)]

        def one(hh, j, carry, masked):
            m, l, acc = carry
            off = pl.multiple_of(j * tq, tq)
            kj = k_ref[pl.ds(off, tq), hh * DH:(hh + 1) * DH]
            vj = v_ref[pl.ds(off, tq), hh * DV:(hh + 1) * DV]
            s = lax.dot_general(qs[hh], kj, nt, preferred_element_type=f32)
            if masked:
                rr = lax.broadcasted_iota(jnp.int32, (tq, tq), 0)
                cc = lax.broadcasted_iota(jnp.int32, (tq, tq), 1)
                s = jnp.where(cc <= rr, s, NEG)
            m_new = jnp.maximum(m, jnp.max(s, axis=-1, keepdims=True))
            alpha = jnp.exp2(m - m_new)
            p = jnp.exp2(s - m_new)
            l = alpha * l + jnp.sum(p, axis=-1, keepdims=True)
            acc = alpha * acc + jnp.dot(p.astype(bf16), vj, preferred_element_type=f32)
            return m_new, l, acc

        def step(j, carries, masked):
            return tuple(one(hh, j, carries[hh], masked) for hh in range(HPS))

        def pair(jj, carries):
            for u in range(unroll):
                carries = step(jj * unroll + u, carries, False)
            return carries

        init1 = (jnp.full((tq, 1), NEG, f32), jnp.zeros((tq, 1), f32), jnp.zeros((tq, DV), f32))
        carry = lax.fori_loop(0, i // unroll, pair, tuple(init1 for _ in range(HPS)))
        carry = lax.fori_loop((i // unroll) * unroll, i, lambda j, c: step(j, c, False), carry)
        carry = step(i, carry, True)
        for hh in range(HPS):
            m, l, acc = carry[hh]
            o_ref[:, hh * DV:(hh + 1) * DV] = acc / l
            lse_ref[hh] = m + jnp.log2(l)

    def body_with_gather(q_ref, k_ref, v_ref, *rest):
        nw = len(pack)
        w_refs, (o_ref, lse_ref), wall_refs, sems = rest[:nw], rest[nw:nw + 2], rest[nw + 2:2 * nw + 2], rest[2 * nw + 2:]
        h = pl.program_id(0)
        i = pl.program_id(1)

        @pl.when((h == 0) & (i == 0))
        def _():
            mine, first, _ = _gather_copies(w_refs, wall_refs, *sems)
            _gather_start(mine, first)

        @pl.when((h == ngrp // 2) & (i == 0))
        def _():
            _, first, passed = _gather_copies(w_refs, wall_refs, *sems)
            _gather_forward(first, passed)

        body(q_ref, k_ref, v_ref, o_ref, lse_ref)

        @pl.when((h == ngrp - 1) & (i == nq - 1))
        def _():
            _gather_finish(*_gather_copies(w_refs, wall_refs, *sems))

    ngrp = NH // HPS
    in_specs = [pl.BlockSpec((tq, HPS * DH), lambda h, i: (i, h)),
                pl.BlockSpec((T, HPS * DH), lambda h, i: (0, h), pipeline_mode=pl.Buffered(1)),
                pl.BlockSpec((T, HPS * DV), lambda h, i: (0, h), pipeline_mode=pl.Buffered(1))]
    out_specs = [pl.BlockSpec((tq, HPS * DV), lambda h, i: (i, h)), pl.BlockSpec((HPS, tq, 1), lambda h, i: (h, i, 0))]
    out_shape = [jax.ShapeDtypeStruct((T, NH * DV), f32), jax.ShapeDtypeStruct((NH, T, 1), f32)]
    if pack is None:
        return pl.pallas_call(body, name=name, grid=(ngrp, nq), in_specs=in_specs, out_specs=out_specs,
                              out_shape=out_shape, compiler_params=_cparams(2))(q, k, v)
    return pl.pallas_call(
        body_with_gather, name=name, grid=(ngrp, nq), in_specs=in_specs + [ANY] * len(pack),
        out_specs=out_specs + [ANY] * len(pack),
        out_shape=out_shape + [jax.ShapeDtypeStruct((4,) + a.shape, a.dtype) for a in pack],
        scratch_shapes=_gather_sems(pack), compiler_params=_cparams(2),
    )(q, k, v, *pack)


def _flash_bwd(name, q, k, v, do, lse, dl, T, tq, scatter=None):
    nq = T // tq

    def body(k_ref, v_ref, q_ref, do_ref, lse_ref, dl_ref, dk_ref, dv_ref, dq_ref):
        j = pl.program_id(1)

        @pl.when(j == 0)
        def _():
            dq_ref[...] = jnp.zeros_like(dq_ref)

        nt = (((1,), (1,)), ((), ()))
        tn = (((0,), (0,)), ((), ()))
        ks = [k_ref[:, hh * DH:(hh + 1) * DH] for hh in range(HPS)]
        vs = [v_ref[:, hh * DV:(hh + 1) * DV] for hh in range(HPS)]

        def one(hh, i, carry, masked):
            dk, dv = carry
            off = pl.multiple_of(i * tq, tq)
            qi = q_ref[pl.ds(off, tq), hh * DH:(hh + 1) * DH]
            doi = do_ref[pl.ds(off, tq), hh * DV:(hh + 1) * DV]
            st = lax.dot_general(ks[hh], qi, nt, preferred_element_type=f32)
            if masked:
                kr = lax.broadcasted_iota(jnp.int32, (tq, tq), 0)
                qc = lax.broadcasted_iota(jnp.int32, (tq, tq), 1)
                st = jnp.where(qc >= kr, st, NEG)
            pt = jnp.exp2(st - lse_ref[hh, i])
            dv = dv + jnp.dot(pt.astype(bf16), doi, preferred_element_type=f32)
            dpt = lax.dot_general(vs[hh], doi, nt, preferred_element_type=f32)
            dst = (pt * (dpt - dl_ref[hh, i])).astype(bf16)
            dk = dk + jnp.dot(dst, qi, preferred_element_type=f32)
            dq_ref[pl.ds(off, tq), hh * DH:(hh + 1) * DH] += lax.dot_general(dst, ks[hh], tn, preferred_element_type=f32)
            return dk, dv

        def step(i, carries, masked):
            return tuple(one(hh, i, carries[hh], masked) for hh in range(HPS))

        zero = tuple((jnp.zeros((tq, DH), f32), jnp.zeros((tq, DV), f32)) for _ in range(HPS))
        carry = step(j, zero, True)
        carry = lax.fori_loop(j + 1, nq, lambda i, c: step(i, c, False), carry)
        for hh in range(HPS):
            dk, dv = carry[hh]
            dk_ref[:, hh * DH:(hh + 1) * DH] = dk * LN2
            dv_ref[:, hh * DV:(hh + 1) * DV] = dv.astype(bf16)

    def body_with_scatter(*refs):
        ns = len(scatter)
        ins, a_refs, outs = refs[:6], refs[6:6 + ns], refs[6 + ns:9 + ns]
        o_refs, sems = refs[9 + ns:9 + 2 * ns], refs[9 + 2 * ns:]
        h = pl.program_id(0)
        j = pl.program_id(1)

        @pl.when((h == 0) & (j == 0))
        def _():
            for cp in _scatter_copies(a_refs, o_refs, *sems):
                cp.start()

        body(*ins, *outs)

        @pl.when((h == ngrp - 1) & (j == nq - 1))
        def _():
            for cp in _scatter_copies(a_refs, o_refs, *sems):
                cp.wait()

    ngrp = NH // HPS
    in_specs = [pl.BlockSpec((tq, HPS * DH), lambda h, j: (j, h)),
                pl.BlockSpec((tq, HPS * DV), lambda h, j: (j, h)),
                pl.BlockSpec((T, HPS * DH), lambda h, j: (0, h), pipeline_mode=pl.Buffered(1)),
                pl.BlockSpec((T, HPS * DV), lambda h, j: (0, h), pipeline_mode=pl.Buffered(1)),
                pl.BlockSpec((HPS, nq, 1, tq), lambda h, j: (h, 0, 0, 0)),
                pl.BlockSpec((HPS, nq, 1, tq), lambda h, j: (h, 0, 0, 0))]
    out_specs = [pl.BlockSpec((tq, HPS * DH), lambda h, j: (j, h)),
                 pl.BlockSpec((tq, HPS * DV), lambda h, j: (j, h)),
                 pl.BlockSpec((T, HPS * DH), lambda h, j: (0, h), pipeline_mode=pl.Buffered(1))]
    out_shape = [jax.ShapeDtypeStruct((T, NH * DH), f32), jax.ShapeDtypeStruct((T, NH * DV), bf16),
                 jax.ShapeDtypeStruct((T, NH * DH), f32)]
    if scatter is None:
        return pl.pallas_call(body, name=name, grid=(ngrp, nq), in_specs=in_specs, out_specs=out_specs,
                              out_shape=out_shape, compiler_params=_cparams(2))(k, v, q, do, lse, dl)
    ns = len(scatter)
    return pl.pallas_call(
        body_with_scatter, name=name, grid=(ngrp, nq), in_specs=in_specs + [ANY] * ns,
        out_specs=out_specs + [ANY] * ns, out_shape=out_shape + _scatter_shapes(scatter),
        scratch_shapes=_scatter_sems(scatter), compiler_params=_cparams(2),
    )(k, v, q, do, lse, dl, *scatter)


def _gate_mix(name, proj, ya, o, T, tb):
    def body(ga_ref, gb_ref, ya_ref, o_ref, y_ref):
        y_ref[...] = (jax.nn.sigmoid(ga_ref[...]) * ya_ref[...] + jax.nn.sigmoid(gb_ref[...]) * o_ref[...]).astype(bf16)

    return _rowwise(name, body, T, tb, [(proj, D, 1), (proj, D, 2), (ya, D, 0), (o, D, 0)], [], [(D, bf16)])[0]


def _gate_mix_bwd(name, dy, proj, ya, o, T, tb):
    def body(dy_ref, ga_ref, gb_ref, ya_ref, o_ref, dga_ref, dgb_ref, dya_ref, do_ref, dl_ref):
        dy = dy_ref[...]
        sa = jax.nn.sigmoid(ga_ref[...])
        sb = jax.nn.sigmoid(gb_ref[...])
        ov = o_ref[...]
        dga_ref[...] = (dy * ya_ref[...] * sa * (1.0 - sa)).astype(bf16)
        dgb_ref[...] = (dy * ov * sb * (1.0 - sb)).astype(bf16)
        dya_ref[...] = dy * sa
        do = dy * sb
        do_ref[...] = do.astype(bf16)
        prod = do * ov
        for hd in range(NH):
            dl_ref[:, hd:hd + 1] = jnp.sum(prod[:, hd * DV:(hd + 1) * DV], axis=-1, keepdims=True)

    return _rowwise(name, body, T, tb, [(dy, D, 0), (proj, D, 1), (proj, D, 2), (ya, D, 0), (o, D, 0)], [],
                    [(D, bf16), (D, bf16), (D, f32), (D, bf16), (NH, f32)])


def _ffn_in_swiglu(name, h, w, T):
    tm = _pick(T, (512, 256, 128))

    def body(h_ref, w_ref, z_ref, a_ref):
        z = jnp.dot(h_ref[...], w_ref[...], preferred_element_type=f32)
        g = z[:, :FH]
        z_ref[...] = z.astype(bf16)
        a_ref[...] = (g * jax.nn.sigmoid(g) * z[:, FH:]).astype(bf16)

    return pl.pallas_call(
        body, name=name, grid=(2, T // tm),
        in_specs=[pl.BlockSpec((tm, D), lambda j, i: (i, 0)), pl.BlockSpec((D, 2 * FH), lambda j, i: (0, j))],
        out_specs=[pl.BlockSpec((tm, 2 * FH), lambda j, i: (i, j)), pl.BlockSpec((tm, FH), lambda j, i: (i, j))],
        out_shape=[jax.ShapeDtypeStruct((T, 2 * DFF), bf16), jax.ShapeDtypeStruct((T, DFF), bf16)],
        compiler_params=_cparams(2),
    )(h, w)


def _dact_swiglu_bwd(name, dfo, w, z, T):
    tm = _pick(T, (512, 256, 128))

    def body(d_ref, w_ref, z_ref, o_ref):
        da = lax.dot_general(d_ref[...], w_ref[...], (((1,), (1,)), ((), ())), preferred_element_type=f32)
        g = z_ref[:, :FH].astype(f32)
        u = z_ref[:, FH:].astype(f32)
        sg = jax.nn.sigmoid(g)
        o_ref[:, :FH] = (da * u * sg * (1.0 + g * (1.0 - sg))).astype(bf16)
        o_ref[:, FH:] = (da * g * sg).astype(bf16)

    return pl.pallas_call(
        body, name=name, grid=(2, T // tm),
        in_specs=[pl.BlockSpec((tm, D), lambda j, i: (i, 0)), pl.BlockSpec((FH, D), lambda j, i: (j, 0)),
                  pl.BlockSpec((tm, 2 * FH), lambda j, i: (i, j))],
        out_specs=pl.BlockSpec((tm, 2 * FH), lambda j, i: (i, j)),
        out_shape=jax.ShapeDtypeStruct((T, 2 * DFF), bf16), compiler_params=_cparams(2),
    )(dfo, w, z)


def _final(name, x, g, m, gfin, tgt, T, tb):
    def body(x_ref, m_ref, t_ref, g_ref, gf_ref, dx_ref, dm_ref, loss_ref, dgf_ref, dg_ref):
        _init_acc(pl.program_id(0), loss_ref, dgf_ref, dg_ref)
        mv = m_ref[...]
        xv = x_ref[...] + g_ref[...] * mv
        rstd = lax.rsqrt(jnp.mean(xv * xv, axis=-1, keepdims=True) + EPS)
        nv = xv * rstd
        err = nv * gf_ref[...] - t_ref[...]
        loss_ref[...] += 0.5 * jnp.sum(jnp.mean(err * err, axis=-1, keepdims=True), axis=0, keepdims=True)
        dy = err * (1.0 / D)
        dgf_ref[...] += _colsum(dy * nv)
        dn = dy * gf_ref[...]
        dxv = rstd * (dn - nv * jnp.mean(dn * nv, axis=-1, keepdims=True))
        dx_ref[...] = dxv
        dm_ref[...] = (dxv * g_ref[...]).astype(bf16)
        dg_ref[...] += _colsum(dxv * mv)

    return _rowwise(name, body, T, tb, [(x, D, 0), (m, D, 0), (tgt, D, 0)], [g, gfin], [(D, f32), (D, bf16)],
                    [((1, 1), f32), ((1, D), f32), ((1, D), f32)])


def _adamw(name, w, g, m, v):
    R, C = w.shape
    tb = _pick(R, (256, 128, 64, 32, 16, 8))
    c1 = 1.0 - ADAM_B1 ** ADAM_STEP
    c2 = 1.0 - ADAM_B2 ** ADAM_STEP

    def body(w_ref, g_ref, m_ref, v_ref, d_ref, m2_ref, v2_ref):
        gv = g_ref[...]
        m2 = ADAM_B1 * m_ref[...] + (1.0 - ADAM_B1) * gv
        v2 = ADAM_B2 * v_ref[...] + (1.0 - ADAM_B2) * (gv * gv)
        m2_ref[...] = m2
        v2_ref[...] = v2
        d_ref[...] = -ADAM_LR * ((m2 / c1) / (jnp.sqrt(v2 / c2) + ADAM_EPS) + ADAM_WD * w_ref[...])

    return _rowwise(name, body, R, tb, [(w, C, 0), (g, C, 0), (m, C, 0), (v, C, 0)], [], [(C, f32)] * 3)


def _ada_fwd(name, c_all, w_ada, b_sh):
    nl, _, ns = w_ada.shape

    def body(c_ref, w_ref, b_ref, o_ref):
        o_ref[0] = jnp.dot(c_ref[...].astype(bf16), w_ref[0].astype(bf16), preferred_element_type=f32) + b_ref[0]

    return pl.pallas_call(
        body, name=name, grid=(nl,),
        in_specs=[pl.BlockSpec((8, D), lambda l: (0, 0)), pl.BlockSpec((1, D, ns), lambda l: (l, 0, 0)),
                  pl.BlockSpec((1, 1, ns), lambda l: (l, 0, 0))],
        out_specs=pl.BlockSpec((1, 8, ns), lambda l: (l, 0, 0)),
        out_shape=jax.ShapeDtypeStruct((nl, 8, ns), f32), compiler_params=_cparams(1),
    )(c_all, w_ada, b_sh)


def _ada_bwd(name, c_t, dm_sh):
    nl, _, ns = dm_sh.shape

    def body(c_ref, dm_ref, o_ref):
        acc = c_ref[:, 0:1] * dm_ref[0, 0:1, :]
        for d in range(1, 8):
            acc = acc + c_ref[:, d:d + 1] * dm_ref[0, d:d + 1, :]
        o_ref[0] = acc

    return pl.pallas_call(
        body, name=name, grid=(nl,),
        in_specs=[pl.BlockSpec((D, 8), lambda l: (0, 0)), pl.BlockSpec((1, 8, ns), lambda l: (l, 0, 0))],
        out_specs=pl.BlockSpec((1, D, ns), lambda l: (l, 0, 0)),
        out_shape=jax.ShapeDtypeStruct((nl, D, ns), f32), compiler_params=_cparams(1),
    )(c_t, dm_sh)


def _sum_slots(name, a):
    n, R, C = a.shape
    tb = _pick(R, (256, 128, 64, 32, 16, 8))

    def body(a_ref, o_ref):
        acc = a_ref[0].astype(f32)
        for s in range(1, n):
            acc = acc + a_ref[s].astype(f32)
        o_ref[...] = acc

    return pl.pallas_call(
        body, name=name, grid=(R // tb,), in_specs=[pl.BlockSpec((n, tb, C), lambda i: (0, i, 0))],
        out_specs=pl.BlockSpec((tb, C), lambda i: (i, 0)), out_shape=jax.ShapeDtypeStruct((R, C), f32),
        compiler_params=_cparams(1),
    )(a)


def _chunks(rows, dtype, want):
    unit = 16 if dtype == bf16 else 8
    for n in range(want, 0, -1):
        if rows % n == 0 and (rows // n) % unit == 0:
            return n, rows // n
    return 1, rows


def _my_chip():
    return 2 * lax.axis_index("x") + lax.axis_index("y")


def _pair_sum(name, arrs, gots):
    na = len(arrs)

    def body(*refs):
        for k in range(na):
            a_ref, g_ref, o_ref = refs[k], refs[na + k], refs[2 * na + k]
            o_ref[...] = (a_ref[...].astype(f32) + g_ref[...].astype(f32)).astype(o_ref.dtype)

    in_specs = [pl.BlockSpec((1, a.shape[1] // 4, a.shape[2]), lambda s, i: (s, 2 * lax.axis_index("c") + i, 0))
                for a in arrs]
    half_specs = [pl.BlockSpec((1, g.shape[1] // 2, g.shape[2]), lambda s, i: (s, i, 0)) for g in gots]
    return pl.pallas_call(
        body, name=name, grid=(4, 2), in_specs=in_specs + half_specs, out_specs=half_specs,
        out_shape=[jax.ShapeDtypeStruct(g.shape, a.dtype) for a, g in zip(arrs, gots)], compiler_params=_cparams(2),
    )(*arrs, *gots)


def _chip_sum(name, sums, lands, layer, bufs=None):
    na = len(sums)

    def body(*refs):
        for k in range(na):
            s_ref, l_ref, o_ref = refs[k], refs[na + k], refs[len(refs) - na + k]
            acc = s_ref[0].astype(f32)
            for j in range(l_ref.shape[0]):
                acc = acc + l_ref[j].astype(f32)
            o_ref[...] = acc.reshape(o_ref.shape)

    in_specs = [pl.BlockSpec((1, s.shape[1] // 2, s.shape[2]), lambda i: (_my_chip(), i, 0)) for s in sums]
    in_specs += [pl.BlockSpec((3, l.shape[1] // 2, l.shape[2]), lambda i: (0, i, 0)) for l in lands]
    if layer is None:
        out_specs = [pl.BlockSpec((s.shape[1] // 2, s.shape[2]), lambda i: (i, 0)) for s in sums]
        out_shape = [jax.ShapeDtypeStruct(s.shape[1:], f32) for s in sums]
    else:
        out_specs = [pl.BlockSpec((1, 1, s.shape[1] // 2, s.shape[2]), lambda i: (layer, lax.axis_index("c"), i, 0))
                     for s in sums]
        out_shape = [jax.ShapeDtypeStruct((2, 2) + s.shape[1:], f32) for s in sums]
    aliases = {}
    operands = list(sums) + list(lands)
    if bufs is not None:
        in_specs += [ANY] * na
        aliases = {2 * na + k: k for k in range(na)}
        operands += list(bufs)
    return pl.pallas_call(
        body, name=name, grid=(2,), in_specs=in_specs, out_specs=out_specs, out_shape=out_shape,
        input_output_aliases=aliases, compiler_params=_cparams(1),
    )(*operands)


def _coords():
    return lax.axis_index("x"), lax.axis_index("y"), lax.axis_index("c")


def _flip(v, m):
    return (1 - v) if m else v


def _peer(x, y, c, mask):
    return (_flip(x, mask[0]), _flip(y, mask[1]), _flip(c, mask[2]))


def _all_gather(name, arr, masks, whole_mesh):
    nslots = 8 if whole_mesh else 4

    def slot(x, y, c):
        return 4 * x + 2 * y + c if whole_mesh else 2 * x + y

    def body(a_ref, o_ref, send_sems, recv_sems, local_sem):
        x, y, c = _coords()
        mine = pltpu.make_async_copy(a_ref, o_ref.at[slot(x, y, c)], local_sem)
        mine.start()
        copies = []
        for k, mask in enumerate(masks):
            cp = pltpu.make_async_remote_copy(
                src_ref=a_ref, dst_ref=o_ref.at[slot(x, y, c)], send_sem=send_sems.at[k], recv_sem=recv_sems.at[k],
                device_id=_peer(x, y, c, mask), device_id_type=MESH)
            cp.start()
            copies.append(cp)
        for cp in copies:
            cp.wait()
        mine.wait()

    return pl.pallas_call(
        body, name=name, in_specs=[ANY], out_specs=ANY,
        out_shape=jax.ShapeDtypeStruct((nslots,) + arr.shape, arr.dtype),
        scratch_shapes=[pltpu.SemaphoreType.DMA((len(masks),)), pltpu.SemaphoreType.DMA((len(masks),)),
                        pltpu.SemaphoreType.DMA],
    )(arr)


def _gather_weights(name, arrs):
    na = len(arrs)

    def body(*refs):
        mine, first, passed = _gather_copies(refs[:na], refs[na:2 * na], *refs[2 * na:])
        _gather_start(mine, first)
        _gather_forward(first, passed)
        _gather_finish(mine, first, passed)

    return pl.pallas_call(
        body, name=name, in_specs=[ANY] * na, out_specs=[ANY] * na,
        out_shape=[jax.ShapeDtypeStruct((4,) + a.shape, a.dtype) for a in arrs], scratch_shapes=_gather_sems(arrs),
    )(*arrs)


def _gather_plan(shape, dtype):
    big = (shape[0] // 2) * shape[1] * 2 >= (2 << 20)
    return _chunks(shape[0] // 2, dtype, 3 if big else 1), _chunks(shape[0], dtype, 4 if big else 1)


def _gather_sems(arrs):
    plans = [_gather_plan(a.shape, a.dtype) for a in arrs]
    nremote = sum(len(CHIP_MASKS) * p[0][0] for p in plans)
    nlocal = sum(p[1][0] for p in plans)
    return [pltpu.SemaphoreType.DMA((nremote,))] * 4 + [pltpu.SemaphoreType.DMA((nlocal,))]


def _gather_copies(w_refs, o_refs, sa, ra, sb, rb, local_sems):
    x, y, c = _coords()
    chip = 2 * x + y
    mine, first, passed = [], [], []
    for w_ref, o_ref in zip(w_refs, o_refs):
        (nch, rows), (nloc, lrows) = _gather_plan(w_ref.shape, w_ref.dtype)
        rh = w_ref.shape[0] // 2
        for k in range(nloc):
            part = pl.ds(k * lrows, lrows)
            mine.append(pltpu.make_async_copy(w_ref.at[part], o_ref.at[chip, part], local_sems.at[len(mine)]))
        for mask in CHIP_MASKS:
            px, py, _ = _peer(x, y, c, mask)
            for ch in range(nch):
                k = len(first)
                part = pl.ds(c * rh + ch * rows, rows)
                first.append(pltpu.make_async_remote_copy(
                    src_ref=w_ref.at[part], dst_ref=o_ref.at[chip, part], send_sem=sa.at[k], recv_sem=ra.at[k],
                    device_id=(px, py, c), device_id_type=MESH))
                landed = o_ref.at[2 * px + py, part]
                passed.append(pltpu.make_async_remote_copy(
                    src_ref=landed, dst_ref=landed, send_sem=sb.at[k], recv_sem=rb.at[k],
                    device_id=(x, y, 1 - c), device_id_type=MESH))
    return mine, first, passed


def _gather_start(mine, first):
    for cp in mine + first:
        cp.start()


def _gather_forward(first, passed):
    for a, b in zip(first, passed):
        a.wait_recv()
        b.start()


def _gather_finish(mine, first, passed):
    for a, b in zip(first, passed):
        a.wait_send()
        b.wait()
    for cp in mine:
        cp.wait()


def _swap_halves(name, arrs):
    na = len(arrs)
    plan = []
    for a in arrs:
        h = a.shape[1] // 2
        nch, rows = _chunks(h, a.dtype, 2 if h * a.shape[2] * 2 >= (1 << 20) else 1)
        plan.append([(s, ch * rows, rows) for s in range(a.shape[0]) for ch in range(nch)])
    ncopies = sum(len(p) for p in plan)

    def body(*refs):
        a_refs, got_refs = refs[:na], refs[na:2 * na]
        send_sems, recv_sems = refs[2 * na:]
        x, y, c = _coords()
        cps = []
        for ai in range(na):
            h = arrs[ai].shape[1] // 2
            for (s, r0, rows) in plan[ai]:
                k = len(cps)
                cp = pltpu.make_async_remote_copy(
                    src_ref=a_refs[ai].at[s, pl.ds((1 - c) * h + r0, rows)], dst_ref=got_refs[ai].at[s, pl.ds(r0, rows)],
                    send_sem=send_sems.at[k], recv_sem=recv_sems.at[k], device_id=(x, y, 1 - c), device_id_type=MESH)
                cp.start()
                cps.append(cp)
        for cp in cps:
            cp.wait()

    halves = [jax.ShapeDtypeStruct((a.shape[0], a.shape[1] // 2, a.shape[2]), a.dtype) for a in arrs]
    return pl.pallas_call(
        body, name=name, in_specs=[ANY] * na, out_specs=[ANY] * na, out_shape=halves,
        scratch_shapes=[pltpu.SemaphoreType.DMA((ncopies,))] * 2,
    )(*arrs)


def _scatter_chips(name, arrs):
    na = len(arrs)

    def body(*refs):
        cps = _scatter_copies(refs[:na], refs[na:2 * na], *refs[2 * na:])
        for cp in cps:
            cp.start()
        for cp in cps:
            cp.wait()

    return pl.pallas_call(
        body, name=name, in_specs=[ANY] * na, out_specs=[ANY] * na,
        out_shape=_scatter_shapes(arrs), scratch_shapes=_scatter_sems(arrs),
    )(*arrs)


def _scatter_shapes(arrs):
    return [jax.ShapeDtypeStruct((len(CHIP_MASKS),) + a.shape[1:], a.dtype) for a in arrs]


def _scatter_sems(arrs):
    return [pltpu.SemaphoreType.DMA((len(arrs), len(CHIP_MASKS)))] * 2


def _scatter_copies(a_refs, o_refs, send_sems, recv_sems):
    x, y, c = _coords()
    cps = []
    for k in range(len(a_refs)):
        for j, mask in enumerate(CHIP_MASKS):
            px, py, pc = _peer(x, y, c, mask)
            cps.append(pltpu.make_async_remote_copy(
                src_ref=a_refs[k].at[2 * px + py], dst_ref=o_refs[k].at[j],
                send_sem=send_sems.at[k, j], recv_sem=recv_sems.at[k, j],
                device_id=(px, py, pc), device_id_type=MESH))
    return cps


def _join_halves(name, bufs):
    na = len(bufs)

    def body(*refs):
        o_refs = refs[na:2 * na]
        send_sems, recv_sems = refs[2 * na:]
        x, y, c = _coords()
        cps = []
        for k in range(na):
            for l in range(2):
                part = o_refs[k].at[l, c]
                cp = pltpu.make_async_remote_copy(
                    src_ref=part, dst_ref=part, send_sem=send_sems.at[k, l], recv_sem=recv_sems.at[k, l],
                    device_id=(x, y, 1 - c), device_id_type=MESH)
                cp.start()
                cps.append(cp)
        for cp in cps:
            cp.wait()

    return pl.pallas_call(
        body, name=name, in_specs=[ANY] * na, out_specs=[ANY] * na,
        out_shape=[jax.ShapeDtypeStruct(b.shape, b.dtype) for b in bufs],
        input_output_aliases={k: k for k in range(na)}, scratch_shapes=[pltpu.SemaphoreType.DMA((na, 2))] * 2,
    )(*bufs)


BIG = ("w_in", "w_uq", "w_ukv", "w_out", "w_ffn_in", "w_ffn_out")
SMALL = ("conv_w", "conv_b", "lru_wa", "lru_ba", "lru_wx", "lru_bx", "lru_a_param", "q_norm_g", "kv_norm_g",
         "final_norm_g")
SMALL_SHAPES = {"conv_w": (2, CONV, D), "conv_b": (2, D), "lru_wa": (2, NB, BW, BW), "lru_ba": (2, NB, BW),
                "lru_wx": (2, NB, BW, BW), "lru_bx": (2, NB, BW), "lru_a_param": (2, D), "q_norm_g": (2, QR),
                "kv_norm_g": (2, KVR), "final_norm_g": (D,)}
SMALL_ROWS = 640
FFN_CHIP_ORDER = (0, 2, 1, 3)


def _size(shape):
    n = 1
    for s in shape:
        n *= s
    return n


def _pack_rows(parts, rows, dtype):
    flat = jnp.concatenate([p.reshape(-1).astype(dtype) for p in parts])
    pad = rows * 1024 - flat.shape[0]
    if pad:
        flat = jnp.concatenate([flat, jnp.zeros((pad,), dtype)])
    return flat.reshape(rows, 1024)


def _unpack_rows(pack, shapes):
    flat = pack.reshape(-1)
    out, off = [], 0
    for s in shapes:
        n = _size(s)
        out.append(flat[off:off + n].reshape(s))
        off += n
    return out


def _inproj_weights(g_in, g_uq, g_ukv):
    w_in = jnp.concatenate([g_in[s] for s in range(4)], axis=1)
    win_p = jnp.concatenate([w_in[:, :D], w_in[:, 1472:2496], w_in[:, 2496:], w_in[:, D:1472],
                             jnp.zeros((D, 64), w_in.dtype)], axis=1)
    w_uq = g_uq.reshape(QR, NH, DN + DR)
    wq_p = jnp.concatenate([w_uq, jnp.zeros((QR, NH, DH - DN - DR), w_uq.dtype)], axis=2).reshape(QR, NH * DH)
    w_ukv = jnp.transpose(g_ukv.reshape(4, KVR, NH, (DN + DV) // 4), (1, 2, 0, 3)).reshape(KVR, NH, DN + DV)
    wk_p = jnp.concatenate([w_ukv[..., :DN], jnp.zeros((KVR, NH, DH - DN), w_ukv.dtype)], axis=2).reshape(KVR, NH * DH)
    wv = w_ukv[..., DN:].reshape(KVR, NH * DV)
    return dict(win=win_p, wq=wq_p, wk=wk_p, wv=wv)


def _other_weights(g_out, g_fi, g_fo):
    ko, ki, kf = g_out.shape[1] // 2, g_fi.shape[1] // 2, g_fo.shape[1] // 2
    wout = [g_out[:, l * ko:(l + 1) * ko].reshape(4 * ko, D) for l in range(2)]
    wfi = [jnp.concatenate([g_fi[s, l * ki:(l + 1) * ki] for s in FFN_CHIP_ORDER], axis=1) for l in range(2)]
    wfo = [g_fo[:, l * kf:(l + 1) * kf].reshape(4 * kf, D) for l in range(2)]
    return dict(wout=wout, wfi=wfi, wfo=wfo)


def _shard_grads_inproj(dwin, dwq, dwk, dwv):
    w_in = jnp.concatenate([dwin[:, :D], dwin[:, QKV0:QKV0 + 448], dwin[:, D:QKV0]], axis=1)
    w_in = jnp.transpose(w_in.reshape(D, 4, DIN // 4), (1, 0, 2))
    w_uq = dwq.reshape(QR, NH, DH)[..., :DN + DR].reshape(4, QR // 4, NH * (DN + DR))
    w_ukv = jnp.concatenate([dwk.reshape(KVR, NH, DH)[..., :DN], dwv.reshape(KVR, NH, DV)], axis=2)
    w_ukv = jnp.transpose(w_ukv.reshape(KVR, NH, 4, (DN + DV) // 4), (2, 0, 1, 3)).reshape(4, KVR, NH * (DN + DV) // 4)
    return [w_in, w_uq, w_ukv]


def _shard_grads_rest(dwout, dwfi, dwfo):
    tiles = jnp.split(dwfi, 4, axis=1)
    wfi = jnp.stack([tiles[FFN_CHIP_ORDER.index(s)] for s in range(4)])
    return [dwout.reshape(4, D // 4, D), wfi, dwfo.reshape(4, DFF // 4, D)]


def kernel(x, c, positions, w_ada, b_ada, w_in, conv_w, conv_b, lru_wa, lru_ba, lru_wx, lru_bx, lru_a_param, q_norm_g, kv_norm_g, w_uq, w_ukv, w_out, w_ffn_in, w_ffn_out, final_norm_g, loss_target, m_w_ada, m_b_ada, m_w_in, m_conv_w, m_conv_b, m_lru_wa, m_lru_ba, m_lru_wx, m_lru_bx, m_lru_a_param, m_q_norm_g, m_kv_norm_g, m_w_uq, m_w_ukv, m_w_out, m_w_ffn_in, m_w_ffn_out, m_final_norm_g, v_w_ada, v_b_ada, v_w_in, v_conv_w, v_conv_b, v_lru_wa, v_lru_ba, v_lru_wx, v_lru_bx, v_lru_a_param, v_q_norm_g, v_kv_norm_g, v_w_uq, v_w_ukv, v_w_out, v_w_ffn_in, v_w_ffn_out, v_final_norm_g):
    T = x.shape[1]
    tb = min(ROW_BLOCK, T)
    tq = min(ATTN_BLOCK, T)
    nq = T // tq
    mx, my, mc = _coords()
    chip = 2 * mx + my
    dev = 4 * mx + 2 * my + mc
    weights = dict(w_ada=w_ada, b_ada=b_ada, w_in=w_in, conv_w=conv_w, conv_b=conv_b, lru_wa=lru_wa, lru_ba=lru_ba,
                   lru_wx=lru_wx, lru_bx=lru_bx, lru_a_param=lru_a_param, q_norm_g=q_norm_g, kv_norm_g=kv_norm_g,
                   w_uq=w_uq, w_ukv=w_ukv, w_out=w_out, w_ffn_in=w_ffn_in, w_ffn_out=w_ffn_out,
                   final_norm_g=final_norm_g)
    mom = dict(w_ada=m_w_ada, b_ada=m_b_ada, w_in=m_w_in, conv_w=m_conv_w, conv_b=m_conv_b, lru_wa=m_lru_wa,
               lru_ba=m_lru_ba, lru_wx=m_lru_wx, lru_bx=m_lru_bx, lru_a_param=m_lru_a_param, q_norm_g=m_q_norm_g,
               kv_norm_g=m_kv_norm_g, w_uq=m_w_uq, w_ukv=m_w_ukv, w_out=m_w_out, w_ffn_in=m_w_ffn_in,
               w_ffn_out=m_w_ffn_out, final_norm_g=m_final_norm_g)
    var = dict(w_ada=v_w_ada, b_ada=v_b_ada, w_in=v_w_in, conv_w=v_conv_w, conv_b=v_conv_b, lru_wa=v_lru_wa,
               lru_ba=v_lru_ba, lru_wx=v_lru_wx, lru_bx=v_lru_bx, lru_a_param=v_lru_a_param, q_norm_g=v_q_norm_g,
               kv_norm_g=v_kv_norm_g, w_uq=v_w_uq, w_ukv=v_w_ukv, w_out=v_w_out, w_ffn_in=v_w_ffn_in,
               w_ffn_out=v_w_ffn_out, final_norm_g=v_final_norm_g)
    order = list(weights)

    ns_ada = w_ada.shape[2]
    c_all = _all_gather("gather_c", c, ALL_MASKS, True).reshape(8, D)
    b_sh = lax.dynamic_slice_in_dim(b_ada, chip * ns_ada, ns_ada, axis=1).reshape(2, 1, ns_ada)
    mod_sh = _ada_fwd("ada_fwd", c_all, w_ada, b_sh)
    mod_all = _all_gather("gather_mod", mod_sh.reshape(16, ns_ada), CHIP_MASKS, False)
    mod_mine = lax.dynamic_index_in_dim(mod_all.reshape(4, 2, 8, ns_ada), dev, axis=2, keepdims=False)
    mod = jnp.transpose(mod_mine, (1, 0, 2)).reshape(2, 6, 1, D)

    flat = lambda w: w.astype(bf16).reshape(w.shape[0], w.shape[1], -1)
    b_in, b_uq, b_ukv = flat(w_in), flat(w_uq), flat(w_ukv)
    late = [b_in[1], b_uq[1], b_ukv[1]] + [flat(w).reshape(-1, w.shape[-1]) for w in (w_out, w_ffn_in, w_ffn_out)]
    W0 = _inproj_weights(*_gather_weights("gather_weights_early", [b_in[0], b_uq[0], b_ukv[0]]))
    W = {n: [W0[n], None] for n in W0}
    wa_b = lru_wa.astype(bf16)
    wx_b = lru_wx.astype(bf16)

    pos = positions.reshape(T, 1)
    invf = ROPE_THETA ** (-jnp.arange(0, DR, 2, dtype=f32) / DR)
    invf_p = jnp.concatenate([jnp.zeros((DN,), f32), invf, invf, jnp.zeros((DH - DN - DR,), f32)]).reshape(1, DH)
    cosp, sap, sbp = _rope_tables("rope_tables", pos, invf_p, T, tb)

    ns_cw = conv_w.shape[2]
    cw_all = _all_gather("gather_conv_w", conv_w.reshape(2 * CONV, ns_cw), CHIP_MASKS, False)
    cw_full = jnp.transpose(cw_all, (1, 0, 2)).reshape(2, CONV, D)
    vec = lambda a, l: a[l].reshape(1, -1)

    xl = x.reshape(T, D)
    saved = []
    h1 = _norm_mod("norm_mod_0", xl, mod[0, 1], mod[0, 0], T, tb)
    for l in range(2):
        sh_m, sc_m, g_m, sh_f, sc_f, g_f = [mod[l, j] for j in range(6)]
        proj = _matmul(f"proj_{l}", h1, W["win"][l], "nn")
        u, r, ig, ya = _lru_fwd(f"lru_fwd_{l}", proj, pos, cw_full[l], vec(conv_b, l), wa_b[l], vec(lru_ba, l),
                                wx_b[l], vec(lru_bx, l), vec(lru_a_param, l), T, tb)
        qp, kp, vp, cq, ckv = _mla_prep(f"mla_prep_{l}", proj, cosp, sap, sbp, vec(q_norm_g, l), vec(kv_norm_g, l),
                                        W["wq"][l], W["wk"][l], W["wv"][l], T, tb)
        if l == 0:
            o, lse, *late_all = _flash_fwd("flash_fwd_0", qp, kp, vp, T, tq, late)
            W1 = _inproj_weights(*late_all[:3])
            for n in W1:
                W[n][1] = W1[n]
            W.update(_other_weights(*late_all[3:]))
        else:
            o, lse = _flash_fwd("flash_fwd_1", qp, kp, vp, T, tq)
        y = _gate_mix(f"gate_mix_{l}", proj, ya, o, T, tb)
        mo = _matmul(f"out_proj_{l}", y, W["wout"][l], "nn")
        x2, h2 = _resid_norm_mod(f"resid_norm_f_{l}", xl, g_m, mo, sc_f, sh_f, T, tb)
        gu, act = _ffn_in_swiglu(f"ffn_in_{l}", h2, W["wfi"][l], T)
        fo = _matmul(f"ffn_out_{l}", act, W["wfo"][l], "nn")
        saved.append(dict(x=xl, h1=h1, proj=proj, u=u, r=r, ig=ig, ya=ya, qp=qp, kp=kp, vp=vp, cq=cq, ckv=ckv, o=o,
                          lse=lse, y=y, mo=mo, x2=x2, h2=h2, gu=gu, act=act, fo=fo))
        if l == 0:
            xl, h1 = _resid_norm_mod("resid_norm_m_1", x2, g_f, fo, mod[1, 1], mod[1, 0], T, tb)

    dx, dfo, loss_part, dgfin, dg_f = _final("final", saved[1]["x2"], mod[1, 5], saved[1]["fo"],
                                             final_norm_g.reshape(1, D), loss_target.reshape(T, D), T, tb)
    loss = lax.psum(loss_part[0, 0], ("x", "y", "c"))

    gl = {n: [None, None] for n in ("win", "wq", "wk", "wv", "wout", "wfi", "wfo", "conv_w", "conv_b", "lru_wa",
                                    "lru_ba", "lru_wx", "lru_bx", "lru_a_param", "q_norm_g", "kv_norm_g")}
    dmod = [None, None]

    def pair_sums(tag, arrs):
        return list(_pair_sum(f"rs_pair_sum_{tag}", arrs, _swap_halves(f"rs_swap_halves_{tag}", arrs)))

    def inproj_pieces(l):
        return _shard_grads_inproj(gl["win"][l], gl["wq"][l], gl["wk"][l], gl["wv"][l])

    def rest_pieces(l):
        return _shard_grads_rest(gl["wout"][l], gl["wfi"][l], gl["wfo"][l])

    for l in (1, 0):
        s = saved[l]
        sh_m, sc_m, g_m, sh_f, sc_f, g_f = [mod[l, j] for j in range(6)]
        dgu = _dact_swiglu_bwd(f"d_act_{l}", dfo, W["wfo"][l], s["gu"], T)
        gl["wfo"][l] = _matmul(f"dw_ffn_out_{l}", s["act"], dfo, "tn", bf16)
        dh2 = _matmul(f"d_h2_{l}", dgu, W["wfi"][l], "nt")
        gl["wfi"][l] = _matmul(f"dw_ffn_in_{l}", s["h2"], dgu, "tn", bf16)
        dx2, dmo, dsc_f, dsh_f, dg_m = _norm_mod_bwd(f"norm_bwd_f_{l}", dh2, s["x2"], sc_f, dx, T, tb, (g_m, s["mo"]))
        dy = _matmul(f"d_y_{l}", dmo, W["wout"][l], "nt")
        gl["wout"][l] = _matmul(f"dw_out_{l}", s["y"], dmo, "tn", bf16)
        dga, dgb, dya, do, dlt = _gate_mix_bwd(f"gate_mix_bwd_{l}", dy, s["proj"], s["ya"], s["o"], T, tb)
        lse_r = s["lse"].reshape(NH, nq, 1, tq)
        dl_r = jnp.transpose(dlt).reshape(NH, nq, 1, tq)
        if l == 0:
            sums0a = pair_sums("0a", rest_pieces(0))
            dk, dv, dq, *lands = _flash_bwd("flash_bwd_0", s["qp"], s["kp"], s["vp"], do, lse_r, dl_r, T, tq,
                                            sums1 + sums0a)
            rs_bufs = list(_chip_sum("rs_chip_sum_1", sums1, lands[:6], 1))
            rs_bufs[3:] = _chip_sum("rs_chip_sum_0a", sums0a, lands[6:], 0, rs_bufs[3:])
        else:
            dk, dv, dq = _flash_bwd("flash_bwd_1", s["qp"], s["kp"], s["vp"], do, lse_r, dl_r, T, tq)
        dqkv, dqp, dkp, dgq, dgkv = _mla_bwd(f"mla_bwd_{l}", dq, dk, dv, s["proj"], cosp, sap, sbp,
                                              vec(q_norm_g, l), vec(kv_norm_g, l), W["wq"][l], W["wk"][l], W["wv"][l],
                                              T, tb)
        gl["wq"][l] = _matmul(f"dw_uq_{l}", s["cq"], dqp, "tn", bf16)
        gl["wk"][l] = _matmul(f"dw_uk_{l}", s["ckv"], dkp, "tn", bf16)
        gl["wv"][l] = _matmul(f"dw_uv_{l}", s["ckv"], dv, "tn", bf16)
        dxl, dcw, dcb, dwa, dba, dwx, dbx, dsp = _lru_bwd(
            f"lru_bwd_{l}", dya, s["proj"], s["u"], s["r"], s["ig"], s["ya"], pos, cw_full[l], wa_b[l], wx_b[l],
            vec(lru_a_param, l), T, tb)
        dproj = jnp.concatenate([dxl, dga, dgb, dqkv], axis=1)
        dh1 = _matmul(f"d_h1_{l}", dproj, W["win"][l], "nt")
        gl["win"][l] = _matmul(f"dw_in_{l}", s["h1"], dproj, "tn", bf16)
        if l == 1:
            dx, dfo_below, dsc_m, dsh_m, dg_f_below = _norm_mod_bwd(
                "norm_bwd_m_1", dh1, s["x"], sc_m, dx2, T, tb, (mod[0, 5], saved[0]["fo"]))
        else:
            dx, dsc_m, dsh_m = _norm_mod_bwd("norm_bwd_m_0", dh1, s["x"], sc_m, dx2, T, tb)
        dmod[l] = jnp.concatenate([dsh_m, dsc_m, dg_m, dsh_f, dsc_f, dg_f], axis=1)
        gl["conv_w"][l], gl["conv_b"][l] = dcw, dcb[0]
        gl["lru_wa"][l], gl["lru_ba"][l], gl["lru_wx"][l], gl["lru_bx"][l] = dwa, dba[0], dwx, dbx[0]
        gl["lru_a_param"][l] = dsp[0]
        gl["q_norm_g"][l], gl["kv_norm_g"][l] = dgq[0], dgkv[0]
        if l == 1:
            sums1 = pair_sums("1", inproj_pieces(1) + rest_pieces(1))
            dfo, dg_f = dfo_below, dg_f_below
    grad_x = dx.reshape(1, T, D)
    gfull = {n: jnp.stack(gl[n]) for n in SMALL if n != "final_norm_g"}
    gfull["final_norm_g"] = dgfin[0]

    dmod_all = _all_gather("gather_dmod", jnp.concatenate(dmod, axis=1), ALL_MASKS, True).reshape(8, 2, 6, D)
    dmod_sum = _sum_slots("sum_dmod", dmod_all.reshape(8, 12, D)).reshape(2, 6 * D)
    dm_sh = lax.dynamic_slice_in_dim(jnp.transpose(dmod_all.reshape(8, 2, 6 * D), (1, 0, 2)), chip * ns_ada, ns_ada, axis=2)
    grads = {"w_ada": _ada_bwd("ada_bwd", jnp.transpose(c_all), dm_sh), "b_ada": dmod_sum}

    gsmall = _pack_rows([gfull[n] for n in SMALL], SMALL_ROWS, f32).reshape(4, SMALL_ROWS // 4, 1024)
    sums0b = pair_sums("0b", inproj_pieces(0) + [gsmall])
    lands0b = _scatter_chips("rs_scatter_chips_0b", sums0b)
    rs_bufs[:3] = _chip_sum("rs_chip_sum_0b", sums0b[:3], lands0b[:3], 0, rs_bufs[:3])
    tot_s = _chip_sum("rs_chip_sum_small", sums0b[3:], lands0b[3:], None)[0]
    bufs = _join_halves("rs_join_halves", rs_bufs)
    small_all = _all_gather("gather_small_grads", tot_s, ALL_MASKS, True).reshape(SMALL_ROWS, 1024)
    for n, buf in zip(BIG, bufs):
        grads[n] = buf.reshape(weights[n].shape)
    for n, gs in zip(SMALL, _unpack_rows(small_all, [SMALL_SHAPES[n] for n in SMALL])):
        grads[n] = gs
    grads["conv_w"] = lax.dynamic_slice_in_dim(grads["conv_w"], chip * ns_cw, ns_cw, axis=2)

    deltas, new_m, new_v = {}, {}, {}
    for n in order:
        w = weights[n]
        view = (1, w.shape[0]) if w.ndim == 1 else (_size(w.shape[:-1]), w.shape[-1]) if w.shape[-1] >= 128 else (w.shape[0], _size(w.shape[1:]))
        d, m2, v2 = _adamw(f"adamw_{n}", w.reshape(view), grads[n].reshape(view), mom[n].reshape(view), var[n].reshape(view))
        deltas[n], new_m[n], new_v[n] = d.reshape(w.shape), m2.reshape(w.shape), v2.reshape(w.shape)
    return (loss, grad_x, *[grads[n] for n in order], *[deltas[n] for n in order], *[new_m[n] for n in order],
            *[new_v[n] for n in order])
```

```python
import functools

import jax
import jax.numpy as jnp
from jax import lax
from jax.experimental import pallas as pl
from jax.experimental.pallas import tpu as pltpu

f32 = jnp.float32
bf16 = jnp.bfloat16

D = 1024
NH = 8
DN = 128
DR = 64
DV = 128
DH = 256
QR = 256
KVR = 128
DFF = 2816
FH = DFF // 2
NB = 8
BW = 128
CONV = 4
PW = 3584
QKV0 = 3 * D
DIN = 3520
EPS = 1e-6
LRU_C = 8.0
ROPE_THETA = 10000.0
SCALE = (DN + DR) ** -0.5
LOG2E = 1.4426950408889634
LN2 = 0.6931471805599453
QSCALE = SCALE * LOG2E
NEG = -1e30
HPS = 2
HPS_BWD = 1

ADAM_LR = 0.001
ADAM_B1 = 0.9
ADAM_B2 = 0.999
ADAM_EPS = 1e-08
ADAM_WD = 0.01
ADAM_STEP = 10

VMEM_LIMIT = 56 * 1024 * 1024
ROW_BLOCK = 256
ATTN_BLOCK = 1024
MESH = pl.DeviceIdType.MESH
ANY = pl.BlockSpec(memory_space=pl.ANY)
CHIP_MASKS = ((1, 0, 0), (0, 1, 0), (1, 1, 0))
ALL_MASKS = ((0, 0, 1), (1, 0, 0), (0, 1, 0), (1, 1, 0), (1, 0, 1), (0, 1, 1), (1, 1, 1))


def _cparams(n_axes):
    return pltpu.CompilerParams(dimension_semantics=("arbitrary",) * n_axes, vmem_limit_bytes=VMEM_LIMIT)


def _pick(n, cands):
    for c in cands:
        if n % c == 0:
            return c
    return n


def _rowwise(name, body, T, tb, row_ins, vec_ins, row_outs, acc_outs=(), scratch=(), reverse=False):
    n = T // tb

    def rmap(i):
        return (n - 1 - i) if reverse else i

    in_specs = []
    for (_, w, cb) in row_ins:
        in_specs.append(pl.BlockSpec((tb, w), functools.partial(lambda i, cb: (rmap(i), cb), cb=cb)))
    for v in vec_ins:
        in_specs.append(pl.BlockSpec(v.shape, functools.partial(lambda i, nd: (0,) * nd, nd=v.ndim)))
    out_specs, out_shape = [], []
    for (w, dt) in row_outs:
        out_specs.append(pl.BlockSpec((tb, w), lambda i: (rmap(i), 0)))
        out_shape.append(jax.ShapeDtypeStruct((T, w), dt))
    for (s, dt) in acc_outs:
        out_specs.append(pl.BlockSpec(s, functools.partial(lambda i, nd: (0,) * nd, nd=len(s))))
        out_shape.append(jax.ShapeDtypeStruct(s, dt))

    def kern(*refs):
        body(*refs)

    return pl.pallas_call(
        kern, name=name, grid=(n,), in_specs=in_specs, out_specs=out_specs, out_shape=out_shape,
        scratch_shapes=list(scratch), compiler_params=_cparams(1),
    )(*[a for (a, _, _) in row_ins], *vec_ins)


def _matmul(name, a, b, mode, out_dtype=f32):
    if mode == "nn":
        (M, K), N = a.shape, b.shape[1]
    elif mode == "nt":
        (M, K), N = a.shape, b.shape[0]
    else:
        (K, M), N = a.shape, b.shape[1]
    wide = (1792, 1408, 1024, 512, 256, 128)
    tm = _pick(M, (512, 256, 128)) if mode != "tn" else _pick(M, (1024, 1408, 512, 256, 128))
    tn = _pick(N, wide)
    tk = K if mode != "tn" else _pick(K, (1024, 512, 256, 128))
    nk = K // tk
    dims = {"nn": (((1,), (0,)), ((), ())), "nt": (((1,), (1,)), ((), ())), "tn": (((0,), (0,)), ((), ()))}[mode]
    if mode == "tn":
        a_spec = pl.BlockSpec((tk, tm), lambda j, i, k: (k, i))
    else:
        a_spec = pl.BlockSpec((tm, tk), lambda j, i, k: (i, k))
    once = {"pipeline_mode": pl.Buffered(1)} if (mode != "tn" and N == tn) else {}
    if mode == "nt":
        b_spec = pl.BlockSpec((tn, tk), lambda j, i, k: (j, k), **once)
    else:
        b_spec = pl.BlockSpec((tk, tn), lambda j, i, k: (k, j), **once)
    o_spec = pl.BlockSpec((tm, tn), lambda j, i, k: (i, j))
    use_acc = nk > 1 and out_dtype != f32

    def kern(a_ref, b_ref, o_ref, *scr):
        k = pl.program_id(2)
        d = lax.dot_general(a_ref[...].astype(bf16), b_ref[...].astype(bf16), dims, preferred_element_type=f32)
        if nk == 1:
            o_ref[...] = d.astype(out_dtype)
        else:
            acc = scr[0] if use_acc else o_ref

            @pl.when(k == 0)
            def _():
                acc[...] = d

            @pl.when(k > 0)
            def _():
                acc[...] += d

            if use_acc:
                @pl.when(k == nk - 1)
                def _():
                    o_ref[...] = acc[...].astype(out_dtype)

    return pl.pallas_call(
        kern, name=name, grid=(N // tn, M // tm, nk), in_specs=[a_spec, b_spec], out_specs=o_spec,
        out_shape=jax.ShapeDtypeStruct((M, N), out_dtype),
        scratch_shapes=[pltpu.VMEM((tm, tn), f32)] if use_acc else [],
        compiler_params=_cparams(3),
    )(a, b)


def _colsum(v):
    return jnp.sum(v, axis=0, keepdims=True)


def _init_acc(step, *refs):
    @pl.when(step == 0)
    def _():
        for r in refs:
            r[...] = jnp.zeros_like(r)


def _norm_mod(name, x, sc, sh, T, tb):
    def body(x_ref, sc_ref, sh_ref, h_ref):
        xv = x_ref[...]
        rstd = lax.rsqrt(jnp.mean(xv * xv, axis=-1, keepdims=True) + EPS)
        h_ref[...] = (xv * rstd * (1.0 + sc_ref[...]) + sh_ref[...]).astype(bf16)

    return _rowwise(name, body, T, tb, [(x, D, 0)], [sc, sh], [(D, bf16)])[0]


def _resid_norm_mod(name, x, g, m, sc, sh, T, tb):
    def body(x_ref, m_ref, g_ref, sc_ref, sh_ref, x2_ref, h_ref):
        xv = x_ref[...] + g_ref[...] * m_ref[...]
        x2_ref[...] = xv
        rstd = lax.rsqrt(jnp.mean(xv * xv, axis=-1, keepdims=True) + EPS)
        h_ref[...] = (xv * rstd * (1.0 + sc_ref[...]) + sh_ref[...]).astype(bf16)

    return _rowwise(name, body, T, tb, [(x, D, 0), (m, D, 0)], [g, sc, sh], [(D, f32), (D, bf16)])


def _norm_mod_bwd(name, dh, x, sc, dres, T, tb, resid=None):
    def norm_part(dh_ref, x_ref, dr_ref, sc_ref, dx_ref, dsc_ref, dsh_ref):
        xv = x_ref[...]
        dhv = dh_ref[...]
        rstd = lax.rsqrt(jnp.mean(xv * xv, axis=-1, keepdims=True) + EPS)
        nv = xv * rstd
        dn = dhv * (1.0 + sc_ref[...])
        dxv = dr_ref[...] + rstd * (dn - nv * jnp.mean(dn * nv, axis=-1, keepdims=True))
        dx_ref[...] = dxv
        dsc_ref[...] += _colsum(dhv * nv)
        dsh_ref[...] += _colsum(dhv)
        return dxv

    if resid is None:
        def body(dh_ref, x_ref, dr_ref, sc_ref, dx_ref, dsc_ref, dsh_ref):
            _init_acc(pl.program_id(0), dsc_ref, dsh_ref)
            norm_part(dh_ref, x_ref, dr_ref, sc_ref, dx_ref, dsc_ref, dsh_ref)

        return _rowwise(name, body, T, tb, [(dh, D, 0), (x, D, 0), (dres, D, 0)], [sc], [(D, f32)],
                        [((1, D), f32)] * 2)

    def body_resid(dh_ref, x_ref, dr_ref, m_ref, sc_ref, g_ref, dx_ref, dm_ref, dsc_ref, dsh_ref, dg_ref):
        _init_acc(pl.program_id(0), dsc_ref, dsh_ref, dg_ref)
        dxv = norm_part(dh_ref, x_ref, dr_ref, sc_ref, dx_ref, dsc_ref, dsh_ref)
        dm_ref[...] = (dxv * g_ref[...]).astype(bf16)
        dg_ref[...] += _colsum(dxv * m_ref[...])

    g, m = resid
    return _rowwise(name, body_resid, T, tb, [(dh, D, 0), (x, D, 0), (dres, D, 0), (m, D, 0)], [sc, g],
                    [(D, f32), (D, bf16)], [((1, D), f32)] * 3)


def _expm1_neg(y):
    poly = y * (1.0 + y * (0.5 + y * (1.0 / 6 + y * (1.0 / 24))))
    return jnp.where(y > -0.05, poly, jnp.exp(y) - 1.0)


def _lru_gates(r, sp, reset):
    log_a = -LRU_C * r * sp
    a = jnp.where(reset, 0.0, jnp.exp(log_a))
    mult = jnp.where(reset, 1.0, jnp.sqrt(-_expm1_neg(2.0 * log_a)))
    return a, mult


def _block_dot(v, w_ref, dims):
    outs = [lax.dot_general(v[:, n * BW:(n + 1) * BW], w_ref[n], dims, preferred_element_type=f32) for n in range(NB)]
    return jnp.concatenate(outs, axis=1)


def _lru_fwd(name, proj, pos, cw, cb, wa, ba, wx, bx, ap, T, tb):
    nsteps = tb.bit_length() - 1

    def body(x_ref, pos_ref, cw_ref, cb_ref, wa_ref, ba_ref, wx_ref, bx_ref, ap_ref,
             u_ref, r_ref, i_ref, h_ref, prev_x, carry_h):
        _init_acc(pl.program_id(0), prev_x, carry_h)
        x = x_ref[...]
        px = prev_x[...]
        row = lax.broadcasted_iota(jnp.int32, (tb, 1), 0)
        u = x * cw_ref[CONV - 1:CONV, :] + cb_ref[...]
        for j in range(1, CONV):
            z = jnp.where(row >= tb - j, px, x)
            u = u + pltpu.roll(z, j, 0) * cw_ref[CONV - 1 - j:CONV - j, :]
        ub = u.astype(bf16)
        nn = (((1,), (0,)), ((), ()))
        r = jax.nn.sigmoid(_block_dot(ub, wa_ref, nn) + ba_ref[...])
        ig = jax.nn.sigmoid(_block_dot(ub, wx_ref, nn) + bx_ref[...])
        sp = jax.nn.softplus(-ap_ref[...])
        reset = pos_ref[...] == 0
        a, mult = _lru_gates(r, sp, reset)
        b = u * ig * mult
        sub = row & 7
        for s in (1, 2, 4):
            keep = sub >= s
            a_sh = jnp.where(keep, pltpu.roll(a, s, 0), 1.0)
            b_sh = jnp.where(keep, pltpu.roll(b, s, 0), 0.0)
            b = a * b_sh + b
            a = a * a_sh
        carry = carry_h[...]
        for g in range(tb // 8):
            rows = slice(8 * g, 8 * g + 8)
            hg = b[rows] + a[rows] * carry
            h_ref[rows, :] = hg
            carry = hg[7:8]
        carry_h[...] = carry
        u_ref[...] = u
        r_ref[...] = r
        i_ref[...] = ig
        prev_x[...] = x

    return _rowwise(name, body, T, tb, [(proj, D, 0), (pos, 1, 0)], [cw, cb, wa, ba, wx, bx, ap],
                    [(D, f32)] * 4, scratch=[pltpu.VMEM((tb, D), f32), pltpu.VMEM((1, D), f32)])


def _lru_bwd(name, dh, proj, u, r, ig, h, pos, cw, wa, wx, ap, T, tb):
    n = T // tb
    nsteps = tb.bit_length() - 1
    t8 = tb // 8

    def rmap(i):
        return n - 1 - i

    def body(dh_ref, x_ref, u_ref, r_ref, i_ref, h_ref, hp_ref, pos_ref, cw_ref, wa_ref, wx_ref, ap_ref,
             dx_ref, dcw_ref, dcb_ref, dwa_ref, dba_ref, dwx_ref, dbx_ref, dsp_ref, next_du, carry_g, carry_a):
        step = pl.program_id(0)
        _init_acc(step, dcw_ref, dcb_ref, dwa_ref, dba_ref, dwx_ref, dbx_ref, dsp_ref, next_du, carry_g, carry_a)
        blk = n - 1 - step
        x = x_ref[...]
        u = u_ref[...]
        r = r_ref[...]
        ig = i_ref[...]
        h = h_ref[...]
        row = lax.broadcasted_iota(jnp.int32, (tb, 1), 0)
        sp = jax.nn.softplus(-ap_ref[...])
        reset = pos_ref[...] == 0
        a, mult = _lru_gates(r, sp, reset)
        A = pltpu.roll(jnp.where(row == 0, carry_a[...], a), tb - 1, 0)
        B = dh_ref[...]
        sub = row & 7
        for s in (1, 2, 4):
            keep = sub < 8 - s
            a_sh = jnp.where(keep, pltpu.roll(A, tb - s, 0), 1.0)
            b_sh = jnp.where(keep, pltpu.roll(B, tb - s, 0), 0.0)
            B = B + A * b_sh
            A = A * a_sh
        carry = carry_g[...]
        parts = [None] * (tb // 8)
        for grp in reversed(range(tb // 8)):
            rows = slice(8 * grp, 8 * grp + 8)
            parts[grp] = B[rows] + A[rows] * carry
            carry = parts[grp][0:1]
        g = jnp.concatenate(parts, axis=0)
        carry_g[...] = carry
        carry_a[...] = a[0:1, :]
        h_last = jnp.where(blk == 0, 0.0, hp_ref[7:8, :])
        h_prev = pltpu.roll(jnp.where(row == tb - 1, h_last, h), 1, 0)
        da = g * h_prev
        gu = g * u
        dlog_a = jnp.where(reset, 0.0, da * a - gu * ig * (a * a) / mult)
        du = g * ig * mult
        di = gu * mult
        dr = dlog_a * (-LRU_C * sp)
        dsp_ref[...] += _colsum(dlog_a * (-LRU_C * r)) * (-jax.nn.sigmoid(-ap_ref[...]))
        dzr = dr * r * (1.0 - r)
        dzi = di * ig * (1.0 - ig)
        dzr_b = dzr.astype(bf16)
        dzi_b = dzi.astype(bf16)
        nt = (((1,), (1,)), ((), ()))
        du = du + _block_dot(dzr_b, wa_ref, nt) + _block_dot(dzi_b, wx_ref, nt)
        ub = u.astype(bf16)
        tn = (((0,), (0,)), ((), ()))
        for nb in range(NB):
            sl = slice(nb * BW, (nb + 1) * BW)
            dwa_ref[nb] += lax.dot_general(ub[:, sl], dzr_b[:, sl], tn, preferred_element_type=f32)
            dwx_ref[nb] += lax.dot_general(ub[:, sl], dzi_b[:, sl], tn, preferred_element_type=f32)
        dba_ref[...] += _colsum(dzr)
        dbx_ref[...] += _colsum(dzi)
        dcb_ref[...] += _colsum(du)
        ndu = next_du[...]
        dx = du * cw_ref[CONV - 1:CONV, :]
        dcw_ref[CONV - 1:CONV, :] += _colsum(x * du)
        for j in range(1, CONV):
            up = pltpu.roll(jnp.where(row < j, ndu, du), tb - j, 0)
            dx = dx + up * cw_ref[CONV - 1 - j:CONV - j, :]
            dcw_ref[CONV - 1 - j:CONV - j, :] += _colsum(x * up)
        dx_ref[...] = dx.astype(bf16)
        next_du[...] = du

    row_specs = [pl.BlockSpec((tb, D), lambda i: (rmap(i), 0)) for _ in range(6)]
    in_specs = row_specs + [
        pl.BlockSpec((8, D), lambda i: (jnp.maximum(rmap(i) * t8 - 1, 0), 0)),
        pl.BlockSpec((tb, 1), lambda i: (rmap(i), 0)),
        pl.BlockSpec((CONV, D), lambda i: (0, 0)),
        pl.BlockSpec((NB, BW, BW), lambda i: (0, 0, 0)),
        pl.BlockSpec((NB, BW, BW), lambda i: (0, 0, 0)),
        pl.BlockSpec((1, D), lambda i: (0, 0)),
    ]
    vec = lambda s: pl.BlockSpec(s, functools.partial(lambda i, nd: (0,) * nd, nd=len(s)))
    acc_shapes = [(CONV, D), (1, D), (NB, BW, BW), (1, D), (NB, BW, BW), (1, D), (1, D)]
    return pl.pallas_call(
        body, name=name, grid=(n,), in_specs=in_specs,
        out_specs=[pl.BlockSpec((tb, D), lambda i: (rmap(i), 0))] + [vec(s) for s in acc_shapes],
        out_shape=[jax.ShapeDtypeStruct((T, D), bf16)] + [jax.ShapeDtypeStruct(s, f32) for s in acc_shapes],
        scratch_shapes=[pltpu.VMEM((tb, D), f32), pltpu.VMEM((1, D), f32), pltpu.VMEM((1, D), f32)],
        compiler_params=_cparams(1),
    )(dh, proj, u, r, ig, h, h, pos, cw, wa, wx, ap)


def _rope_tables(name, pos, invf, T, tb):
    def body(pos_ref, invf_ref, cos_ref, sa_ref, sb_ref):
        ang = pos_ref[...].astype(f32) * invf_ref[...]
        lane = lax.broadcasted_iota(jnp.int32, (tb, DH), 1)
        first = (lane >= DN) & (lane < DN + DR // 2)
        second = (lane >= DN + DR // 2) & (lane < DN + DR)
        c = jnp.cos(ang)
        s = jnp.sin(ang)
        cos_ref[...] = jnp.where(lane < DN, 1.0, jnp.where(first | second, c, 0.0))
        sa_ref[...] = jnp.where(first, -s, 0.0)
        sb_ref[...] = jnp.where(second, s, 0.0)

    return _rowwise(name, body, T, tb, [(pos, 1, 0)], [invf], [(DH, f32)] * 3)


def _rope_heads(z, cos, sa, sb, transposed=False, shared=None):
    half = DR // 2
    parts = []
    for hd in range(NH):
        hi = z[:, hd * DH + DN:(hd + 1) * DH]
        if shared is not None:
            hi = hi + shared
        if transposed:
            hi = hi * cos + pltpu.roll(hi * sa, half, 1) + pltpu.roll(hi * sb, DN - half, 1)
        else:
            hi = hi * cos + pltpu.roll(hi, DN - half, 1) * sa + pltpu.roll(hi, half, 1) * sb
        parts += [z[:, hd * DH:hd * DH + DN], hi]
    return parts


def _mla_prep(name, proj, cosp, sap, sbp, gq, gkv, wq, wk, wv, T, tb):
    def body(qkv_ref, cos_ref, sa_ref, sb_ref, gq_ref, gkv_ref, wq_ref, wk_ref, wv_ref,
             q_ref, k_ref, v_ref, cq_ref, ckv_ref):
        qkv = qkv_ref[...]
        qd = qkv[:, :QR]
        kvd = qkv[:, QR:QR + KVR]
        slab = qkv[:, QR + KVR:]
        nq = qd * lax.rsqrt(jnp.mean(qd * qd, axis=-1, keepdims=True) + EPS)
        cq = (nq * gq_ref[...]).astype(bf16)
        cq_scaled = (nq * (gq_ref[...] * QSCALE)).astype(bf16)
        ckv = (kvd * lax.rsqrt(jnp.mean(kvd * kvd, axis=-1, keepdims=True) + EPS) * gkv_ref[...]).astype(bf16)
        cos, sa, sb = cos_ref[:, DN:], sa_ref[:, DN:], sb_ref[:, DN:]
        qpre = jnp.dot(cq_scaled, wq_ref[...], preferred_element_type=f32)
        kpre = jnp.dot(ckv, wk_ref[...], preferred_element_type=f32)
        q_ref[...] = jnp.concatenate(_rope_heads(qpre, cos, sa, sb), axis=1).astype(bf16)
        k_ref[...] = jnp.concatenate(_rope_heads(kpre, cos, sa, sb, shared=slab), axis=1).astype(bf16)
        v_ref[...] = jnp.dot(ckv, wv_ref[...], preferred_element_type=f32).astype(bf16)
        cq_ref[...] = cq
        ckv_ref[...] = ckv

    return _rowwise(name, body, T, tb,
                    [(proj, 512, QKV0 // 512), (cosp, DH, 0), (sap, DH, 0), (sbp, DH, 0)],
                    [gq, gkv, wq, wk, wv],
                    [(NH * DH, bf16), (NH * DH, bf16), (NH * DV, bf16), (QR, bf16), (KVR, bf16)])


def _mla_bwd(name, dq, dk, dv, proj, cosp, sap, sbp, gq, gkv, wq, wk, wv, T, tb):
    def body(dq_ref, dk_ref, dv_ref, qkv_ref, cos_ref, sa_ref, sb_ref, gq_ref, gkv_ref, wq_ref, wk_ref, wv_ref,
             dqkv_ref, dqp_ref, dkp_ref, dgq_ref, dgkv_ref):
        _init_acc(pl.program_id(0), dgq_ref, dgkv_ref)
        cos, sa, sb = cos_ref[:, DN:], sa_ref[:, DN:], sb_ref[:, DN:]
        dqp_b = jnp.concatenate(_rope_heads(dq_ref[...] * SCALE, cos, sa, sb, transposed=True), axis=1).astype(bf16)
        dk_parts = _rope_heads(dk_ref[...], cos, sa, sb, transposed=True)
        dkp_b = jnp.concatenate(dk_parts, axis=1).astype(bf16)
        dqp_ref[...] = dqp_b
        dkp_ref[...] = dkp_b
        nt = (((1,), (1,)), ((), ()))
        dcq = lax.dot_general(dqp_b, wq_ref[...], nt, preferred_element_type=f32)
        dckv = (lax.dot_general(dkp_b, wk_ref[...], nt, preferred_element_type=f32)
                + lax.dot_general(dv_ref[...], wv_ref[...], nt, preferred_element_type=f32))
        dslab = dk_parts[1]
        for hd in range(1, NH):
            dslab = dslab + dk_parts[2 * hd + 1]
        qkv = qkv_ref[...]
        qd = qkv[:, :QR]
        kvd = qkv[:, QR:QR + KVR]
        nq = qd * lax.rsqrt(jnp.mean(qd * qd, axis=-1, keepdims=True) + EPS)
        nkv = kvd * lax.rsqrt(jnp.mean(kvd * kvd, axis=-1, keepdims=True) + EPS)
        dgq_ref[...] += _colsum(dcq * nq)
        dgkv_ref[...] += _colsum(dckv * nkv)
        dnq = dcq * gq_ref[...]
        dnkv = dckv * gkv_ref[...]
        rq = lax.rsqrt(jnp.mean(qd * qd, axis=-1, keepdims=True) + EPS)
        rkv = lax.rsqrt(jnp.mean(kvd * kvd, axis=-1, keepdims=True) + EPS)
        dqd = rq * (dnq - nq * jnp.mean(dnq * nq, axis=-1, keepdims=True))
        dkvd = rkv * (dnkv - nkv * jnp.mean(dnkv * nkv, axis=-1, keepdims=True))
        dqkv_ref[...] = jnp.concatenate([dqd, dkvd, dslab], axis=1).astype(bf16)

    return _rowwise(name, body, T, tb,
                    [(dq, NH * DH, 0), (dk, NH * DH, 0), (dv, NH * DV, 0), (proj, 512, QKV0 // 512),
                     (cosp, DH, 0), (sap, DH, 0), (sbp, DH, 0)],
                    [gq, gkv, wq, wk, wv],
                    [(512, bf16), (NH * DH, bf16), (NH * DH, bf16)],
                    [((1, QR), f32), ((1, KVR), f32)])


def _flash_fwd(name, q, k, v, T, tq, pack=None):
    nq = T // tq
    unroll = 2

    def body(q_ref, k_ref, v_ref, o_ref, lse_ref):
        i = pl.program_id(1)
        nt = (((1,), (1,)), ((), ()))
        qs = [q_ref[:, hh * DH:(hh + 1) * DH] for hh in range(HPS)]

        def one(hh, j, carry, masked):
            m, l, acc = carry
            off = pl.multiple_of(j * tq, tq)
            kj = k_ref[pl.ds(off, tq), hh * DH:(hh + 1) * DH]
            vj = v_ref[pl.ds(off, tq), hh * DV:(hh + 1) * DV]
            s = lax.dot_general(qs[hh], kj, nt, preferred_element_type=f32)
            if masked:
                rr = lax.broadcasted_iota(jnp.int32, (tq, tq), 0)
                cc = lax.broadcasted_iota(jnp.int32, (tq, tq), 1)
                s = jnp.where(cc <= rr, s, NEG)
            m_new = jnp.maximum(m, jnp.max(s, axis=-1, keepdims=True))
            alpha = jnp.exp2(m - m_new)
            p = jnp.exp2(s - m_new)
            l = alpha * l + jnp.sum(p, axis=-1, keepdims=True)
            acc = alpha * acc + jnp.dot(p.astype(bf16), vj, preferred_element_type=f32)
            return m_new, l, acc

        def step(j, carries, masked):
            return tuple(one(hh, j, carries[hh], masked) for hh in range(HPS))

        def pair(jj, carries):
            for u in range(unroll):
                carries = step(jj * unroll + u, carries, False)
            return carries

        init1 = (jnp.full((tq, 1), NEG, f32), jnp.zeros((tq, 1), f32), jnp.zeros((tq, DV), f32))
        carry = lax.fori_loop(0, i // unroll, pair, tuple(init1 for _ in range(HPS)))
        carry = lax.fori_loop((i // unroll) * unroll, i, lambda j, c: step(j, c, False), carry)
        carry = step(i, carry, True)
        for hh in range(HPS):
            m, l, acc = carry[hh]
            o_ref[:, hh * DV:(hh + 1) * DV] = acc / l
            lse_ref[hh] = m + jnp.log2(l)

    def body_with_gather(q_ref, k_ref, v_ref, *rest):
        nw = len(pack)
        w_refs, (o_ref, lse_ref), wall_refs, sems = rest[:nw], rest[nw:nw + 2], rest[nw + 2:2 * nw + 2], rest[2 * nw + 2:]
        h = pl.program_id(0)
        i = pl.program_id(1)

        @pl.when((h == 0) & (i == 0))
        def _():
            mine, first, _ = _gather_copies(w_refs, wall_refs, *sems)
            _gather_start(mine, first)

        @pl.when((h == ngrp // 2) & (i == 0))
        def _():
            _, first, passed = _gather_copies(w_refs, wall_refs, *sems)
            _gather_forward(first, passed)

        body(q_ref, k_ref, v_ref, o_ref, lse_ref)

        @pl.when((h == ngrp - 1) & (i == nq - 1))
        def _():
            _gather_finish(*_gather_copies(w_refs, wall_refs, *sems))

    ngrp = NH // HPS
    in_specs = [pl.BlockSpec((tq, HPS * DH), lambda h, i: (i, h)),
                pl.BlockSpec((T, HPS * DH), lambda h, i: (0, h), pipeline_mode=pl.Buffered(1)),
                pl.BlockSpec((T, HPS * DV), lambda h, i: (0, h), pipeline_mode=pl.Buffered(1))]
    out_specs = [pl.BlockSpec((tq, HPS * DV), lambda h, i: (i, h)), pl.BlockSpec((HPS, tq, 1), lambda h, i: (h, i, 0))]
    out_shape = [jax.ShapeDtypeStruct((T, NH * DV), f32), jax.ShapeDtypeStruct((NH, T, 1), f32)]
    if pack is None:
        return pl.pallas_call(body, name=name, grid=(ngrp, nq), in_specs=in_specs, out_specs=out_specs,
                              out_shape=out_shape, compiler_params=_cparams(2))(q, k, v)
    return pl.pallas_call(
        body_with_gather, name=name, grid=(ngrp, nq), in_specs=in_specs + [ANY] * len(pack),
        out_specs=out_specs + [ANY] * len(pack),
        out_shape=out_shape + [jax.ShapeDtypeStruct((4,) + a.shape, a.dtype) for a in pack],
        scratch_shapes=_gather_sems(pack), compiler_params=_cparams(2),
    )(q, k, v, *pack)


def _flash_bwd(name, q, k, v, do, lse, dl, T, tq, scatter=None):
    HPS = HPS_BWD
    nq = T // tq

    def body(k_ref, v_ref, q_ref, do_ref, lse_ref, dl_ref, dk_ref, dv_ref, dq_ref):
        j = pl.program_id(1)

        @pl.when(j == 0)
        def _():
            dq_ref[...] = jnp.zeros_like(dq_ref)

        nt = (((1,), (1,)), ((), ()))
        tn = (((0,), (0,)), ((), ()))
        ks = [k_ref[:, hh * DH:(hh + 1) * DH] for hh in range(HPS)]
        vs = [v_ref[:, hh * DV:(hh + 1) * DV] for hh in range(HPS)]

        def one(hh, i, carry, masked):
            dk, dv = carry
            off = pl.multiple_of(i * tq, tq)
            qi = q_ref[pl.ds(off, tq), hh * DH:(hh + 1) * DH]
            doi = do_ref[pl.ds(off, tq), hh * DV:(hh + 1) * DV]
            st = lax.dot_general(ks[hh], qi, nt, preferred_element_type=f32)
            if masked:
                kr = lax.broadcasted_iota(jnp.int32, (tq, tq), 0)
                qc = lax.broadcasted_iota(jnp.int32, (tq, tq), 1)
                st = jnp.where(qc >= kr, st, NEG)
            dpt = lax.dot_general(vs[hh], doi, nt, preferred_element_type=f32)
            pt = jnp.exp2(st - lse_ref[hh, i])
            dst = (pt * (dpt - dl_ref[hh, i])).astype(bf16)
            dv = dv + jnp.dot(pt.astype(bf16), doi, preferred_element_type=f32)
            dk = dk + jnp.dot(dst, qi, preferred_element_type=f32)
            dq_ref[pl.ds(off, tq), hh * DH:(hh + 1) * DH] += lax.dot_general(dst, ks[hh], tn, preferred_element_type=f32)
            return dk, dv

        def step(i, carries, masked):
            return tuple(one(hh, i, carries[hh], masked) for hh in range(HPS))

        zero = tuple((jnp.zeros((tq, DH), f32), jnp.zeros((tq, DV), f32)) for _ in range(HPS))
        carry = step(j, zero, True)
        carry = lax.fori_loop(j + 1, nq, lambda i, c: step(i, c, False), carry)
        for hh in range(HPS):
            dk, dv = carry[hh]
            dk_ref[:, hh * DH:(hh + 1) * DH] = dk * LN2
            dv_ref[:, hh * DV:(hh + 1) * DV] = dv.astype(bf16)

    def body_with_scatter(*refs):
        ns = len(scatter)
        ins, a_refs, outs = refs[:6], refs[6:6 + ns], refs[6 + ns:9 + ns]
        o_refs, sems = refs[9 + ns:9 + 2 * ns], refs[9 + 2 * ns:]
        h = pl.program_id(0)
        j = pl.program_id(1)

        @pl.when((h == 0) & (j == 0))
        def _():
            for cp in _scatter_copies(a_refs, o_refs, *sems):
                cp.start()

        body(*ins, *outs)

        @pl.when((h == ngrp - 1) & (j == nq - 1))
        def _():
            for cp in _scatter_copies(a_refs, o_refs, *sems):
                cp.wait()

    ngrp = NH // HPS
    in_specs = [pl.BlockSpec((tq, HPS * DH), lambda h, j: (j, h)),
                pl.BlockSpec((tq, HPS * DV), lambda h, j: (j, h)),
                pl.BlockSpec((T, HPS * DH), lambda h, j: (0, h), pipeline_mode=pl.Buffered(1)),
                pl.BlockSpec((T, HPS * DV), lambda h, j: (0, h), pipeline_mode=pl.Buffered(1)),
                pl.BlockSpec((HPS, nq, 1, tq), lambda h, j: (h, 0, 0, 0)),
                pl.BlockSpec((HPS, nq, 1, tq), lambda h, j: (h, 0, 0, 0))]
    out_specs = [pl.BlockSpec((tq, HPS * DH), lambda h, j: (j, h)),
                 pl.BlockSpec((tq, HPS * DV), lambda h, j: (j, h)),
                 pl.BlockSpec((T, HPS * DH), lambda h, j: (0, h), pipeline_mode=pl.Buffered(1))]
    out_shape = [jax.ShapeDtypeStruct((T, NH * DH), f32), jax.ShapeDtypeStruct((T, NH * DV), bf16),
                 jax.ShapeDtypeStruct((T, NH * DH), f32)]
    if scatter is None:
        return pl.pallas_call(body, name=name, grid=(ngrp, nq), in_specs=in_specs, out_specs=out_specs,
                              out_shape=out_shape, compiler_params=_cparams(2))(k, v, q, do, lse, dl)
    ns = len(scatter)
    return pl.pallas_call(
        body_with_scatter, name=name, grid=(ngrp, nq), in_specs=in_specs + [ANY] * ns,
        out_specs=out_specs + [ANY] * ns, out_shape=out_shape + _scatter_shapes(scatter),
        scratch_shapes=_scatter_sems(scatter), compiler_params=_cparams(2),
    )(k, v, q, do, lse, dl, *scatter)


def _gate_mix(name, proj, ya, o, T, tb):
    def body(ga_ref, gb_ref, ya_ref, o_ref, y_ref):
        y_ref[...] = (jax.nn.sigmoid(ga_ref[...]) * ya_ref[...] + jax.nn.sigmoid(gb_ref[...]) * o_ref[...]).astype(bf16)

    return _rowwise(name, body, T, tb, [(proj, D, 1), (proj, D, 2), (ya, D, 0), (o, D, 0)], [], [(D, bf16)])[0]


def _gate_mix_bwd(name, dy, proj, ya, o, T, tb):
    def body(dy_ref, ga_ref, gb_ref, ya_ref, o_ref, dga_ref, dgb_ref, dya_ref, do_ref, dl_ref):
        dy = dy_ref[...]
        sa = jax.nn.sigmoid(ga_ref[...])
        sb = jax.nn.sigmoid(gb_ref[...])
        ov = o_ref[...]
        dga_ref[...] = (dy * ya_ref[...] * sa * (1.0 - sa)).astype(bf16)
        dgb_ref[...] = (dy * ov * sb * (1.0 - sb)).astype(bf16)
        dya_ref[...] = dy * sa
        do = dy * sb
        do_ref[...] = do.astype(bf16)
        prod = do * ov
        for hd in range(NH):
            dl_ref[:, hd:hd + 1] = jnp.sum(prod[:, hd * DV:(hd + 1) * DV], axis=-1, keepdims=True)

    return _rowwise(name, body, T, tb, [(dy, D, 0), (proj, D, 1), (proj, D, 2), (ya, D, 0), (o, D, 0)], [],
                    [(D, bf16), (D, bf16), (D, f32), (D, bf16), (NH, f32)])


def _ffn_in_swiglu(name, h, w, T):
    tm = _pick(T, (512, 256, 128))

    def body(h_ref, w_ref, z_ref, a_ref):
        z = jnp.dot(h_ref[...], w_ref[...], preferred_element_type=f32)
        g = z[:, :FH]
        z_ref[...] = z.astype(bf16)
        a_ref[...] = (g * jax.nn.sigmoid(g) * z[:, FH:]).astype(bf16)

    return pl.pallas_call(
        body, name=name, grid=(2, T // tm),
        in_specs=[pl.BlockSpec((tm, D), lambda j, i: (i, 0)), pl.BlockSpec((D, 2 * FH), lambda j, i: (0, j))],
        out_specs=[pl.BlockSpec((tm, 2 * FH), lambda j, i: (i, j)), pl.BlockSpec((tm, FH), lambda j, i: (i, j))],
        out_shape=[jax.ShapeDtypeStruct((T, 2 * DFF), bf16), jax.ShapeDtypeStruct((T, DFF), bf16)],
        compiler_params=_cparams(2),
    )(h, w)


def _dact_swiglu_bwd(name, dfo, w, z, T):
    tm = _pick(T, (512, 256, 128))

    def body(d_ref, w_ref, z_ref, o_ref):
        da = lax.dot_general(d_ref[...], w_ref[...], (((1,), (1,)), ((), ())), preferred_element_type=f32)
        g = z_ref[:, :FH].astype(f32)
        u = z_ref[:, FH:].astype(f32)
        sg = jax.nn.sigmoid(g)
        o_ref[:, :FH] = (da * u * sg * (1.0 + g * (1.0 - sg))).astype(bf16)
        o_ref[:, FH:] = (da * g * sg).astype(bf16)

    return pl.pallas_call(
        body, name=name, grid=(2, T // tm),
        in_specs=[pl.BlockSpec((tm, D), lambda j, i: (i, 0)), pl.BlockSpec((FH, D), lambda j, i: (j, 0)),
                  pl.BlockSpec((tm, 2 * FH), lambda j, i: (i, j))],
        out_specs=pl.BlockSpec((tm, 2 * FH), lambda j, i: (i, j)),
        out_shape=jax.ShapeDtypeStruct((T, 2 * DFF), bf16), compiler_params=_cparams(2),
    )(dfo, w, z)


def _final(name, x, g, m, gfin, tgt, T, tb):
    def body(x_ref, m_ref, t_ref, g_ref, gf_ref, dx_ref, dm_ref, loss_ref, dgf_ref, dg_ref):
        _init_acc(pl.program_id(0), loss_ref, dgf_ref, dg_ref)
        mv = m_ref[...]
        xv = x_ref[...] + g_ref[...] * mv
        rstd = lax.rsqrt(jnp.mean(xv * xv, axis=-1, keepdims=True) + EPS)
        nv = xv * rstd
        err = nv * gf_ref[...] - t_ref[...]
        loss_ref[...] += 0.5 * jnp.sum(jnp.mean(err * err, axis=-1, keepdims=True), axis=0, keepdims=True)
        dy = err * (1.0 / D)
        dgf_ref[...] += _colsum(dy * nv)
        dn = dy * gf_ref[...]
        dxv = rstd * (dn - nv * jnp.mean(dn * nv, axis=-1, keepdims=True))
        dx_ref[...] = dxv
        dm_ref[...] = (dxv * g_ref[...]).astype(bf16)
        dg_ref[...] += _colsum(dxv * mv)

    return _rowwise(name, body, T, tb, [(x, D, 0), (m, D, 0), (tgt, D, 0)], [g, gfin], [(D, f32), (D, bf16)],
                    [((1, 1), f32), ((1, D), f32), ((1, D), f32)])


def _adamw(name, w, g, m, v):
    R, C = w.shape
    tb = _pick(R, (256, 128, 64, 32, 16, 8))
    c1 = 1.0 - ADAM_B1 ** ADAM_STEP
    c2 = 1.0 - ADAM_B2 ** ADAM_STEP

    def body(w_ref, g_ref, m_ref, v_ref, d_ref, m2_ref, v2_ref):
        gv = g_ref[...]
        m2 = ADAM_B1 * m_ref[...] + (1.0 - ADAM_B1) * gv
        v2 = ADAM_B2 * v_ref[...] + (1.0 - ADAM_B2) * (gv * gv)
        m2_ref[...] = m2
        v2_ref[...] = v2
        d_ref[...] = -ADAM_LR * ((m2 / c1) / (jnp.sqrt(v2 / c2) + ADAM_EPS) + ADAM_WD * w_ref[...])

    return _rowwise(name, body, R, tb, [(w, C, 0), (g, C, 0), (m, C, 0), (v, C, 0)], [], [(C, f32)] * 3)


def _ada_fwd(name, c_all, w_ada, b_sh):
    nl, _, ns = w_ada.shape

    def body(c_ref, w_ref, b_ref, o_ref):
        o_ref[0] = jnp.dot(c_ref[...].astype(bf16), w_ref[0].astype(bf16), preferred_element_type=f32) + b_ref[0]

    return pl.pallas_call(
        body, name=name, grid=(nl,),
        in_specs=[pl.BlockSpec((8, D), lambda l: (0, 0)), pl.BlockSpec((1, D, ns), lambda l: (l, 0, 0)),
                  pl.BlockSpec((1, 1, ns), lambda l: (l, 0, 0))],
        out_specs=pl.BlockSpec((1, 8, ns), lambda l: (l, 0, 0)),
        out_shape=jax.ShapeDtypeStruct((nl, 8, ns), f32), compiler_params=_cparams(1),
    )(c_all, w_ada, b_sh)


def _ada_bwd(name, c_t, dm_sh):
    nl, _, ns = dm_sh.shape

    def body(c_ref, dm_ref, o_ref):
        acc = c_ref[:, 0:1] * dm_ref[0, 0:1, :]
        for d in range(1, 8):
            acc = acc + c_ref[:, d:d + 1] * dm_ref[0, d:d + 1, :]
        o_ref[0] = acc

    return pl.pallas_call(
        body, name=name, grid=(nl,),
        in_specs=[pl.BlockSpec((D, 8), lambda l: (0, 0)), pl.BlockSpec((1, 8, ns), lambda l: (l, 0, 0))],
        out_specs=pl.BlockSpec((1, D, ns), lambda l: (l, 0, 0)),
        out_shape=jax.ShapeDtypeStruct((nl, D, ns), f32), compiler_params=_cparams(1),
    )(c_t, dm_sh)


def _sum_slots(name, a):
    n, R, C = a.shape
    tb = _pick(R, (256, 128, 64, 32, 16, 8))

    def body(a_ref, o_ref):
        acc = a_ref[0].astype(f32)
        for s in range(1, n):
            acc = acc + a_ref[s].astype(f32)
        o_ref[...] = acc

    return pl.pallas_call(
        body, name=name, grid=(R // tb,), in_specs=[pl.BlockSpec((n, tb, C), lambda i: (0, i, 0))],
        out_specs=pl.BlockSpec((tb, C), lambda i: (i, 0)), out_shape=jax.ShapeDtypeStruct((R, C), f32),
        compiler_params=_cparams(1),
    )(a)


def _chunks(rows, dtype, want):
    unit = 16 if dtype == bf16 else 8
    for n in range(want, 0, -1):
        if rows % n == 0 and (rows // n) % unit == 0:
            return n, rows // n
    return 1, rows


def _my_chip():
    return 2 * lax.axis_index("x") + lax.axis_index("y")


def _pair_sum(name, arrs, gots):
    na = len(arrs)

    def body(*refs):
        for k in range(na):
            a_ref, g_ref, o_ref = refs[k], refs[na + k], refs[2 * na + k]
            o_ref[...] = (a_ref[...].astype(f32) + g_ref[...].astype(f32)).astype(o_ref.dtype)

    in_specs = [pl.BlockSpec((1, a.shape[1] // 4, a.shape[2]), lambda s, i: (s, 2 * lax.axis_index("c") + i, 0))
                for a in arrs]
    half_specs = [pl.BlockSpec((1, g.shape[1] // 2, g.shape[2]), lambda s, i: (s, i, 0)) for g in gots]
    return pl.pallas_call(
        body, name=name, grid=(4, 2), in_specs=in_specs + half_specs, out_specs=half_specs,
        out_shape=[jax.ShapeDtypeStruct(g.shape, a.dtype) for a, g in zip(arrs, gots)], compiler_params=_cparams(2),
    )(*arrs, *gots)


def _chip_sum(name, sums, lands, layer, bufs=None):
    na = len(sums)

    def body(*refs):
        for k in range(na):
            s_ref, l_ref, o_ref = refs[k], refs[na + k], refs[len(refs) - na + k]
            acc = s_ref[0].astype(f32)
            for j in range(l_ref.shape[0]):
                acc = acc + l_ref[j].astype(f32)
            o_ref[...] = acc.reshape(o_ref.shape)

    in_specs = [pl.BlockSpec((1, s.shape[1] // 2, s.shape[2]), lambda i: (_my_chip(), i, 0)) for s in sums]
    in_specs += [pl.BlockSpec((3, l.shape[1] // 2, l.shape[2]), lambda i: (0, i, 0)) for l in lands]
    if layer is None:
        out_specs = [pl.BlockSpec((s.shape[1] // 2, s.shape[2]), lambda i: (i, 0)) for s in sums]
        out_shape = [jax.ShapeDtypeStruct(s.shape[1:], f32) for s in sums]
    else:
        out_specs = [pl.BlockSpec((1, 1, s.shape[1] // 2, s.shape[2]), lambda i: (layer, lax.axis_index("c"), i, 0))
                     for s in sums]
        out_shape = [jax.ShapeDtypeStruct((2, 2) + s.shape[1:], f32) for s in sums]
    aliases = {}
    operands = list(sums) + list(lands)
    if bufs is not None:
        in_specs += [ANY] * na
        aliases = {2 * na + k: k for k in range(na)}
        operands += list(bufs)
    return pl.pallas_call(
        body, name=name, grid=(2,), in_specs=in_specs, out_specs=out_specs, out_shape=out_shape,
        input_output_aliases=aliases, compiler_params=_cparams(1),
    )(*operands)


def _coords():
    return lax.axis_index("x"), lax.axis_index("y"), lax.axis_index("c")


def _flip(v, m):
    return (1 - v) if m else v


def _peer(x, y, c, mask):
    return (_flip(x, mask[0]), _flip(y, mask[1]), _flip(c, mask[2]))


def _all_gather(name, arr, masks, whole_mesh):
    nslots = 8 if whole_mesh else 4

    def slot(x, y, c):
        return 4 * x + 2 * y + c if whole_mesh else 2 * x + y

    def body(a_ref, o_ref, send_sems, recv_sems, local_sem):
        x, y, c = _coords()
        mine = pltpu.make_async_copy(a_ref, o_ref.at[slot(x, y, c)], local_sem)
        mine.start()
        copies = []
        for k, mask in enumerate(masks):
            cp = pltpu.make_async_remote_copy(
                src_ref=a_ref, dst_ref=o_ref.at[slot(x, y, c)], send_sem=send_sems.at[k], recv_sem=recv_sems.at[k],
                device_id=_peer(x, y, c, mask), device_id_type=MESH)
            cp.start()
            copies.append(cp)
        for cp in copies:
            cp.wait()
        mine.wait()

    return pl.pallas_call(
        body, name=name, in_specs=[ANY], out_specs=ANY,
        out_shape=jax.ShapeDtypeStruct((nslots,) + arr.shape, arr.dtype),
        scratch_shapes=[pltpu.SemaphoreType.DMA((len(masks),)), pltpu.SemaphoreType.DMA((len(masks),)),
                        pltpu.SemaphoreType.DMA],
    )(arr)


def _gather_weights(name, arrs):
    na = len(arrs)

    def body(*refs):
        mine, first, passed = _gather_copies(refs[:na], refs[na:2 * na], *refs[2 * na:])
        _gather_start(mine, first)
        _gather_forward(first, passed)
        _gather_finish(mine, first, passed)

    return pl.pallas_call(
        body, name=name, in_specs=[ANY] * na, out_specs=[ANY] * na,
        out_shape=[jax.ShapeDtypeStruct((4,) + a.shape, a.dtype) for a in arrs], scratch_shapes=_gather_sems(arrs),
    )(*arrs)


def _gather_plan(shape, dtype):
    big = (shape[0] // 2) * shape[1] * 2 >= (2 << 20)
    return _chunks(shape[0] // 2, dtype, 3 if big else 1), _chunks(shape[0], dtype, 4 if big else 1)


def _gather_sems(arrs):
    plans = [_gather_plan(a.shape, a.dtype) for a in arrs]
    nremote = sum(len(CHIP_MASKS) * p[0][0] for p in plans)
    nlocal = sum(p[1][0] for p in plans)
    return [pltpu.SemaphoreType.DMA((nremote,))] * 4 + [pltpu.SemaphoreType.DMA((nlocal,))]


def _gather_copies(w_refs, o_refs, sa, ra, sb, rb, local_sems):
    x, y, c = _coords()
    chip = 2 * x + y
    mine, first, passed = [], [], []
    for w_ref, o_ref in zip(w_refs, o_refs):
        (nch, rows), (nloc, lrows) = _gather_plan(w_ref.shape, w_ref.dtype)
        rh = w_ref.shape[0] // 2
        for k in range(nloc):
            part = pl.ds(k * lrows, lrows)
            mine.append(pltpu.make_async_copy(w_ref.at[part], o_ref.at[chip, part], local_sems.at[len(mine)]))
        for mask in CHIP_MASKS:
            px, py, _ = _peer(x, y, c, mask)
            for ch in range(nch):
                k = len(first)
                part = pl.ds(c * rh + ch * rows, rows)
                first.append(pltpu.make_async_remote_copy(
                    src_ref=w_ref.at[part], dst_ref=o_ref.at[chip, part], send_sem=sa.at[k], recv_sem=ra.at[k],
                    device_id=(px, py, c), device_id_type=MESH))
                landed = o_ref.at[2 * px + py, part]
                passed.append(pltpu.make_async_remote_copy(
                    src_ref=landed, dst_ref=landed, send_sem=sb.at[k], recv_sem=rb.at[k],
                    device_id=(x, y, 1 - c), device_id_type=MESH))
    return mine, first, passed


def _gather_start(mine, first):
    for cp in mine + first:
        cp.start()


def _gather_forward(first, passed):
    for a, b in zip(first, passed):
        a.wait_recv()
        b.start()


def _gather_finish(mine, first, passed):
    for a, b in zip(first, passed):
        a.wait_send()
        b.wait()
    for cp in mine:
        cp.wait()


def _swap_halves(name, arrs):
    na = len(arrs)
    plan = []
    for a in arrs:
        h = a.shape[1] // 2
        nch, rows = _chunks(h, a.dtype, 2 if h * a.shape[2] * 2 >= (1 << 20) else 1)
        plan.append([(s, ch * rows, rows) for s in range(a.shape[0]) for ch in range(nch)])
    ncopies = sum(len(p) for p in plan)

    def body(*refs):
        a_refs, got_refs = refs[:na], refs[na:2 * na]
        send_sems, recv_sems = refs[2 * na:]
        x, y, c = _coords()
        cps = []
        for ai in range(na):
            h = arrs[ai].shape[1] // 2
            for (s, r0, rows) in plan[ai]:
                k = len(cps)
                cp = pltpu.make_async_remote_copy(
                    src_ref=a_refs[ai].at[s, pl.ds((1 - c) * h + r0, rows)], dst_ref=got_refs[ai].at[s, pl.ds(r0, rows)],
                    send_sem=send_sems.at[k], recv_sem=recv_sems.at[k], device_id=(x, y, 1 - c), device_id_type=MESH)
                cp.start()
                cps.append(cp)
        for cp in cps:
            cp.wait()

    halves = [jax.ShapeDtypeStruct((a.shape[0], a.shape[1] // 2, a.shape[2]), a.dtype) for a in arrs]
    return pl.pallas_call(
        body, name=name, in_specs=[ANY] * na, out_specs=[ANY] * na, out_shape=halves,
        scratch_shapes=[pltpu.SemaphoreType.DMA((ncopies,))] * 2,
    )(*arrs)


def _scatter_chips(name, arrs):
    na = len(arrs)

    def body(*refs):
        cps = _scatter_copies(refs[:na], refs[na:2 * na], *refs[2 * na:])
        for cp in cps:
            cp.start()
        for cp in cps:
            cp.wait()

    return pl.pallas_call(
        body, name=name, in_specs=[ANY] * na, out_specs=[ANY] * na,
        out_shape=_scatter_shapes(arrs), scratch_shapes=_scatter_sems(arrs),
    )(*arrs)


def _scatter_shapes(arrs):
    return [jax.ShapeDtypeStruct((len(CHIP_MASKS),) + a.shape[1:], a.dtype) for a in arrs]


def _scatter_sems(arrs):
    return [pltpu.SemaphoreType.DMA((len(arrs), len(CHIP_MASKS)))] * 2


def _scatter_copies(a_refs, o_refs, send_sems, recv_sems):
    x, y, c = _coords()
    cps = []
    for k in range(len(a_refs)):
        for j, mask in enumerate(CHIP_MASKS):
            px, py, pc = _peer(x, y, c, mask)
            cps.append(pltpu.make_async_remote_copy(
                src_ref=a_refs[k].at[2 * px + py], dst_ref=o_refs[k].at[j],
                send_sem=send_sems.at[k, j], recv_sem=recv_sems.at[k, j],
                device_id=(px, py, pc), device_id_type=MESH))
    return cps


def _join_halves(name, bufs):
    na = len(bufs)

    def body(*refs):
        o_refs = refs[na:2 * na]
        send_sems, recv_sems = refs[2 * na:]
        x, y, c = _coords()
        cps = []
        for k in range(na):
            for l in range(2):
                part = o_refs[k].at[l, c]
                cp = pltpu.make_async_remote_copy(
                    src_ref=part, dst_ref=part, send_sem=send_sems.at[k, l], recv_sem=recv_sems.at[k, l],
                    device_id=(x, y, 1 - c), device_id_type=MESH)
                cp.start()
                cps.append(cp)
        for cp in cps:
            cp.wait()

    return pl.pallas_call(
        body, name=name, in_specs=[ANY] * na, out_specs=[ANY] * na,
        out_shape=[jax.ShapeDtypeStruct(b.shape, b.dtype) for b in bufs],
        input_output_aliases={k: k for k in range(na)}, scratch_shapes=[pltpu.SemaphoreType.DMA((na, 2))] * 2,
    )(*bufs)


BIG = ("w_in", "w_uq", "w_ukv", "w_out", "w_ffn_in", "w_ffn_out")
SMALL = ("conv_w", "conv_b", "lru_wa", "lru_ba", "lru_wx", "lru_bx", "lru_a_param", "q_norm_g", "kv_norm_g",
         "final_norm_g")
SMALL_SHAPES = {"conv_w": (2, CONV, D), "conv_b": (2, D), "lru_wa": (2, NB, BW, BW), "lru_ba": (2, NB, BW),
                "lru_wx": (2, NB, BW, BW), "lru_bx": (2, NB, BW), "lru_a_param": (2, D), "q_norm_g": (2, QR),
                "kv_norm_g": (2, KVR), "final_norm_g": (D,)}
SMALL_ROWS = 640
FFN_CHIP_ORDER = (0, 2, 1, 3)


def _size(shape):
    n = 1
    for s in shape:
        n *= s
    return n


def _pack_rows(parts, rows, dtype):
    flat = jnp.concatenate([p.reshape(-1).astype(dtype) for p in parts])
    pad = rows * 1024 - flat.shape[0]
    if pad:
        flat = jnp.concatenate([flat, jnp.zeros((pad,), dtype)])
    return flat.reshape(rows, 1024)


def _unpack_rows(pack, shapes):
    flat = pack.reshape(-1)
    out, off = [], 0
    for s in shapes:
        n = _size(s)
        out.append(flat[off:off + n].reshape(s))
        off += n
    return out


def _inproj_weights(g_in, g_uq, g_ukv):
    w_in = jnp.concatenate([g_in[s] for s in range(4)], axis=1)
    win_p = jnp.concatenate([w_in[:, :D], w_in[:, 1472:2496], w_in[:, 2496:], w_in[:, D:1472],
                             jnp.zeros((D, 64), w_in.dtype)], axis=1)
    w_uq = g_uq.reshape(QR, NH, DN + DR)
    wq_p = jnp.concatenate([w_uq, jnp.zeros((QR, NH, DH - DN - DR), w_uq.dtype)], axis=2).reshape(QR, NH * DH)
    w_ukv = jnp.transpose(g_ukv.reshape(4, KVR, NH, (DN + DV) // 4), (1, 2, 0, 3)).reshape(KVR, NH, DN + DV)
    wk_p = jnp.concatenate([w_ukv[..., :DN], jnp.zeros((KVR, NH, DH - DN), w_ukv.dtype)], axis=2).reshape(KVR, NH * DH)
    wv = w_ukv[..., DN:].reshape(KVR, NH * DV)
    return dict(win=win_p, wq=wq_p, wk=wk_p, wv=wv)


def _other_weights(g_out, g_fi, g_fo):
    ko, ki, kf = g_out.shape[1] // 2, g_fi.shape[1] // 2, g_fo.shape[1] // 2
    wout = [g_out[:, l * ko:(l + 1) * ko].reshape(4 * ko, D) for l in range(2)]
    wfi = [jnp.concatenate([g_fi[s, l * ki:(l + 1) * ki] for s in FFN_CHIP_ORDER], axis=1) for l in range(2)]
    wfo = [g_fo[:, l * kf:(l + 1) * kf].reshape(4 * kf, D) for l in range(2)]
    return dict(wout=wout, wfi=wfi, wfo=wfo)


def _shard_grads_inproj(dwin, dwq, dwk, dwv):
    w_in = jnp.concatenate([dwin[:, :D], dwin[:, QKV0:QKV0 + 448], dwin[:, D:QKV0]], axis=1)
    w_in = jnp.transpose(w_in.reshape(D, 4, DIN // 4), (1, 0, 2))
    w_uq = dwq.reshape(QR, NH, DH)[..., :DN + DR].reshape(4, QR // 4, NH * (DN + DR))
    w_ukv = jnp.concatenate([dwk.reshape(KVR, NH, DH)[..., :DN], dwv.reshape(KVR, NH, DV)], axis=2)
    w_ukv = jnp.transpose(w_ukv.reshape(KVR, NH, 4, (DN + DV) // 4), (2, 0, 1, 3)).reshape(4, KVR, NH * (DN + DV) // 4)
    return [w_in, w_uq, w_ukv]


def _shard_grads_rest(dwout, dwfi, dwfo):
    tiles = jnp.split(dwfi, 4, axis=1)
    wfi = jnp.stack([tiles[FFN_CHIP_ORDER.index(s)] for s in range(4)])
    return [dwout.reshape(4, D // 4, D), wfi, dwfo.reshape(4, DFF // 4, D)]


def kernel(x, c, positions, w_ada, b_ada, w_in, conv_w, conv_b, lru_wa, lru_ba, lru_wx, lru_bx, lru_a_param, q_norm_g, kv_norm_g, w_uq, w_ukv, w_out, w_ffn_in, w_ffn_out, final_norm_g, loss_target, m_w_ada, m_b_ada, m_w_in, m_conv_w, m_conv_b, m_lru_wa, m_lru_ba, m_lru_wx, m_lru_bx, m_lru_a_param, m_q_norm_g, m_kv_norm_g, m_w_uq, m_w_ukv, m_w_out, m_w_ffn_in, m_w_ffn_out, m_final_norm_g, v_w_ada, v_b_ada, v_w_in, v_conv_w, v_conv_b, v_lru_wa, v_lru_ba, v_lru_wx, v_lru_bx, v_lru_a_param, v_q_norm_g, v_kv_norm_g, v_w_uq, v_w_ukv, v_w_out, v_w_ffn_in, v_w_ffn_out, v_final_norm_g):
    T = x.shape[1]
    tb = min(ROW_BLOCK, T)
    tq = min(ATTN_BLOCK, T)
    nq = T // tq
    mx, my, mc = _coords()
    chip = 2 * mx + my
    dev = 4 * mx + 2 * my + mc
    weights = dict(w_ada=w_ada, b_ada=b_ada, w_in=w_in, conv_w=conv_w, conv_b=conv_b, lru_wa=lru_wa, lru_ba=lru_ba,
                   lru_wx=lru_wx, lru_bx=lru_bx, lru_a_param=lru_a_param, q_norm_g=q_norm_g, kv_norm_g=kv_norm_g,
                   w_uq=w_uq, w_ukv=w_ukv, w_out=w_out, w_ffn_in=w_ffn_in, w_ffn_out=w_ffn_out,
                   final_norm_g=final_norm_g)
    mom = dict(w_ada=m_w_ada, b_ada=m_b_ada, w_in=m_w_in, conv_w=m_conv_w, conv_b=m_conv_b, lru_wa=m_lru_wa,
               lru_ba=m_lru_ba, lru_wx=m_lru_wx, lru_bx=m_lru_bx, lru_a_param=m_lru_a_param, q_norm_g=m_q_norm_g,
               kv_norm_g=m_kv_norm_g, w_uq=m_w_uq, w_ukv=m_w_ukv, w_out=m_w_out, w_ffn_in=m_w_ffn_in,
               w_ffn_out=m_w_ffn_out, final_norm_g=m_final_norm_g)
    var = dict(w_ada=v_w_ada, b_ada=v_b_ada, w_in=v_w_in, conv_w=v_conv_w, conv_b=v_conv_b, lru_wa=v_lru_wa,
               lru_ba=v_lru_ba, lru_wx=v_lru_wx, lru_bx=v_lru_bx, lru_a_param=v_lru_a_param, q_norm_g=v_q_norm_g,
               kv_norm_g=v_kv_norm_g, w_uq=v_w_uq, w_ukv=v_w_ukv, w_out=v_w_out, w_ffn_in=v_w_ffn_in,
               w_ffn_out=v_w_ffn_out, final_norm_g=v_final_norm_g)
    order = list(weights)

    ns_ada = w_ada.shape[2]
    c_all = _all_gather("gather_c", c, ALL_MASKS, True).reshape(8, D)
    b_sh = lax.dynamic_slice_in_dim(b_ada, chip * ns_ada, ns_ada, axis=1).reshape(2, 1, ns_ada)
    mod_sh = _ada_fwd("ada_fwd", c_all, w_ada, b_sh)
    mod_all = _all_gather("gather_mod", mod_sh.reshape(16, ns_ada), CHIP_MASKS, False)
    mod_mine = lax.dynamic_index_in_dim(mod_all.reshape(4, 2, 8, ns_ada), dev, axis=2, keepdims=False)
    mod = jnp.transpose(mod_mine, (1, 0, 2)).reshape(2, 6, 1, D)

    flat = lambda w: w.astype(bf16).reshape(w.shape[0], w.shape[1], -1)
    b_in, b_uq, b_ukv = flat(w_in), flat(w_uq), flat(w_ukv)
    late = [b_in[1], b_uq[1], b_ukv[1]] + [flat(w).reshape(-1, w.shape[-1]) for w in (w_out, w_ffn_in, w_ffn_out)]
    W0 = _inproj_weights(*_gather_weights("gather_weights_early", [b_in[0], b_uq[0], b_ukv[0]]))
    W = {n: [W0[n], None] for n in W0}
    wa_b = lru_wa.astype(bf16)
    wx_b = lru_wx.astype(bf16)

    pos = positions.reshape(T, 1)
    invf = ROPE_THETA ** (-jnp.arange(0, DR, 2, dtype=f32) / DR)
    invf_p = jnp.concatenate([jnp.zeros((DN,), f32), invf, invf, jnp.zeros((DH - DN - DR,), f32)]).reshape(1, DH)
    cosp, sap, sbp = _rope_tables("rope_tables", pos, invf_p, T, tb)

    ns_cw = conv_w.shape[2]
    cw_all = _all_gather("gather_conv_w", conv_w.reshape(2 * CONV, ns_cw), CHIP_MASKS, False)
    cw_full = jnp.transpose(cw_all, (1, 0, 2)).reshape(2, CONV, D)
    vec = lambda a, l: a[l].reshape(1, -1)

    xl = x.reshape(T, D)
    saved = []
    h1 = _norm_mod("norm_mod_0", xl, mod[0, 1], mod[0, 0], T, tb)
    for l in range(2):
        sh_m, sc_m, g_m, sh_f, sc_f, g_f = [mod[l, j] for j in range(6)]
        proj = _matmul(f"proj_{l}", h1, W["win"][l], "nn")
        u, r, ig, ya = _lru_fwd(f"lru_fwd_{l}", proj, pos, cw_full[l], vec(conv_b, l), wa_b[l], vec(lru_ba, l),
                                wx_b[l], vec(lru_bx, l), vec(lru_a_param, l), T, tb)
        qp, kp, vp, cq, ckv = _mla_prep(f"mla_prep_{l}", proj, cosp, sap, sbp, vec(q_norm_g, l), vec(kv_norm_g, l),
                                        W["wq"][l], W["wk"][l], W["wv"][l], T, tb)
        if l == 0:
            o, lse, *late_all = _flash_fwd("flash_fwd_0", qp, kp, vp, T, tq, late)
            W1 = _inproj_weights(*late_all[:3])
            for n in W1:
                W[n][1] = W1[n]
            W.update(_other_weights(*late_all[3:]))
        else:
            o, lse = _flash_fwd("flash_fwd_1", qp, kp, vp, T, tq)
        y = _gate_mix(f"gate_mix_{l}", proj, ya, o, T, tb)
        mo = _matmul(f"out_proj_{l}", y, W["wout"][l], "nn")
        x2, h2 = _resid_norm_mod(f"resid_norm_f_{l}", xl, g_m, mo, sc_f, sh_f, T, tb)
        gu, act = _ffn_in_swiglu(f"ffn_in_{l}", h2, W["wfi"][l], T)
        fo = _matmul(f"ffn_out_{l}", act, W["wfo"][l], "nn")
        saved.append(dict(x=xl, h1=h1, proj=proj, u=u, r=r, ig=ig, ya=ya, qp=qp, kp=kp, vp=vp, cq=cq, ckv=ckv, o=o,
                          lse=lse, y=y, mo=mo, x2=x2, h2=h2, gu=gu, act=act, fo=fo))
        if l == 0:
            xl, h1 = _resid_norm_mod("resid_norm_m_1", x2, g_f, fo, mod[1, 1], mod[1, 0], T, tb)

    dx, dfo, loss_part, dgfin, dg_f = _final("final", saved[1]["x2"], mod[1, 5], saved[1]["fo"],
                                             final_norm_g.reshape(1, D), loss_target.reshape(T, D), T, tb)
    loss = lax.psum(loss_part[0, 0], ("x", "y", "c"))

    gl = {n: [None, None] for n in ("win", "wq", "wk", "wv", "wout", "wfi", "wfo", "conv_w", "conv_b", "lru_wa",
                                    "lru_ba", "lru_wx", "lru_bx", "lru_a_param", "q_norm_g", "kv_norm_g")}
    dmod = [None, None]

    def pair_sums(tag, arrs):
        return list(_pair_sum(f"rs_pair_sum_{tag}", arrs, _swap_halves(f"rs_swap_halves_{tag}", arrs)))

    def inproj_pieces(l):
        return _shard_grads_inproj(gl["win"][l], gl["wq"][l], gl["wk"][l], gl["wv"][l])

    def rest_pieces(l):
        return _shard_grads_rest(gl["wout"][l], gl["wfi"][l], gl["wfo"][l])

    for l in (1, 0):
        s = saved[l]
        sh_m, sc_m, g_m, sh_f, sc_f, g_f = [mod[l, j] for j in range(6)]
        dgu = _dact_swiglu_bwd(f"d_act_{l}", dfo, W["wfo"][l], s["gu"], T)
        gl["wfo"][l] = _matmul(f"dw_ffn_out_{l}", s["act"], dfo, "tn", bf16)
        dh2 = _matmul(f"d_h2_{l}", dgu, W["wfi"][l], "nt")
        gl["wfi"][l] = _matmul(f"dw_ffn_in_{l}", s["h2"], dgu, "tn", bf16)
        dx2, dmo, dsc_f, dsh_f, dg_m = _norm_mod_bwd(f"norm_bwd_f_{l}", dh2, s["x2"], sc_f, dx, T, tb, (g_m, s["mo"]))
        dy = _matmul(f"d_y_{l}", dmo, W["wout"][l], "nt")
        gl["wout"][l] = _matmul(f"dw_out_{l}", s["y"], dmo, "tn", bf16)
        dga, dgb, dya, do, dlt = _gate_mix_bwd(f"gate_mix_bwd_{l}", dy, s["proj"], s["ya"], s["o"], T, tb)
        lse_r = s["lse"].reshape(NH, nq, 1, tq)
        dl_r = jnp.transpose(dlt).reshape(NH, nq, 1, tq)
        if l == 0:
            sums0a = pair_sums("0a", rest_pieces(0))
            dk, dv, dq, *lands = _flash_bwd("flash_bwd_0", s["qp"], s["kp"], s["vp"], do, lse_r, dl_r, T, tq,
                                            sums1 + sums0a)
            rs_bufs = list(_chip_sum("rs_chip_sum_1", sums1, lands[:6], 1))
            rs_bufs[3:] = _chip_sum("rs_chip_sum_0a", sums0a, lands[6:], 0, rs_bufs[3:])
        else:
            dk, dv, dq = _flash_bwd("flash_bwd_1", s["qp"], s["kp"], s["vp"], do, lse_r, dl_r, T, tq)
        dqkv, dqp, dkp, dgq, dgkv = _mla_bwd(f"mla_bwd_{l}", dq, dk, dv, s["proj"], cosp, sap, sbp,
                                              vec(q_norm_g, l), vec(kv_norm_g, l), W["wq"][l], W["wk"][l], W["wv"][l],
                                              T, tb)
        gl["wq"][l] = _matmul(f"dw_uq_{l}", s["cq"], dqp, "tn", bf16)
        gl["wk"][l] = _matmul(f"dw_uk_{l}", s["ckv"], dkp, "tn", bf16)
        gl["wv"][l] = _matmul(f"dw_uv_{l}", s["ckv"], dv, "tn", bf16)
        dxl, dcw, dcb, dwa, dba, dwx, dbx, dsp = _lru_bwd(
            f"lru_bwd_{l}", dya, s["proj"], s["u"], s["r"], s["ig"], s["ya"], pos, cw_full[l], wa_b[l], wx_b[l],
            vec(lru_a_param, l), T, tb)
        dproj = jnp.concatenate([dxl, dga, dgb, dqkv], axis=1)
        dh1 = _matmul(f"d_h1_{l}", dproj, W["win"][l], "nt")
        gl["win"][l] = _matmul(f"dw_in_{l}", s["h1"], dproj, "tn", bf16)
        if l == 1:
            dx, dfo_below, dsc_m, dsh_m, dg_f_below = _norm_mod_bwd(
                "norm_bwd_m_1", dh1, s["x"], sc_m, dx2, T, tb, (mod[0, 5], saved[0]["fo"]))
        else:
            dx, dsc_m, dsh_m = _norm_mod_bwd("norm_bwd_m_0", dh1, s["x"], sc_m, dx2, T, tb)
        dmod[l] = jnp.concatenate([dsh_m, dsc_m, dg_m, dsh_f, dsc_f, dg_f], axis=1)
        gl["conv_w"][l], gl["conv_b"][l] = dcw, dcb[0]
        gl["lru_wa"][l], gl["lru_ba"][l], gl["lru_wx"][l], gl["lru_bx"][l] = dwa, dba[0], dwx, dbx[0]
        gl["lru_a_param"][l] = dsp[0]
        gl["q_norm_g"][l], gl["kv_norm_g"][l] = dgq[0], dgkv[0]
        if l == 1:
            sums1 = pair_sums("1", inproj_pieces(1) + rest_pieces(1))
            dfo, dg_f = dfo_below, dg_f_below
    grad_x = dx.reshape(1, T, D)
    gfull = {n: jnp.stack(gl[n]) for n in SMALL if n != "final_norm_g"}
    gfull["final_norm_g"] = dgfin[0]

    dmod_all = _all_gather("gather_dmod", jnp.concatenate(dmod, axis=1), ALL_MASKS, True).reshape(8, 2, 6, D)
    dmod_sum = _sum_slots("sum_dmod", dmod_all.reshape(8, 12, D)).reshape(2, 6 * D)
    dm_sh = lax.dynamic_slice_in_dim(jnp.transpose(dmod_all.reshape(8, 2, 6 * D), (1, 0, 2)), chip * ns_ada, ns_ada, axis=2)
    grads = {"w_ada": _ada_bwd("ada_bwd", jnp.transpose(c_all), dm_sh), "b_ada": dmod_sum}

    gsmall = _pack_rows([gfull[n] for n in SMALL], SMALL_ROWS, f32).reshape(4, SMALL_ROWS // 4, 1024)
    sums0b = pair_sums("0b", inproj_pieces(0) + [gsmall])
    lands0b = _scatter_chips("rs_scatter_chips_0b", sums0b)
    rs_bufs[:3] = _chip_sum("rs_chip_sum_0b", sums0b[:3], lands0b[:3], 0, rs_bufs[:3])
    tot_s = _chip_sum("rs_chip_sum_small", sums0b[3:], lands0b[3:], None)[0]
    bufs = _join_halves("rs_join_halves", rs_bufs)
    small_all = _all_gather("gather_small_grads", tot_s, ALL_MASKS, True).reshape(SMALL_ROWS, 1024)
    for n, buf in zip(BIG, bufs):
        grads[n] = buf.reshape(weights[n].shape)
    for n, gs in zip(SMALL, _unpack_rows(small_all, [SMALL_SHAPES[n] for n in SMALL])):
        grads[n] = gs
    grads["conv_w"] = lax.dynamic_slice_in_dim(grads["conv_w"], chip * ns_cw, ns_cw, axis=2)

    deltas, new_m, new_v = {}, {}, {}
    for n in order:
        w = weights[n]
        view = (1, w.shape[0]) if w.ndim == 1 else (_size(w.shape[:-1]), w.shape[-1]) if w.shape[-1] >= 128 else (w.shape[0], _size(w.shape[1:]))
        d, m2, v2 = _adamw(f"adamw_{n}", w.reshape(view), grads[n].reshape(view), mom[n].reshape(view), var[n].reshape(view))
        deltas[n], new_m[n], new_v[n] = d.reshape(w.shape), m2.reshape(w.shape), v2.reshape(w.shape)
    return (loss, grad_x, *[grads[n] for n in order], *[deltas[n] for n in order], *[new_m[n] for n in order],
            *[new_v[n] for n in order])
```

```python
import functools

import jax
import jax.numpy as jnp
from jax import lax
from jax.experimental import pallas as pl
from jax.experimental.pallas import tpu as pltpu

f32 = jnp.float32
bf16 = jnp.bfloat16

D = 1024
NH = 8
DN = 128
DR = 64
DV = 128
DH = 256
QR = 256
KVR = 128
DFF = 2816
FH = DFF // 2
NB = 8
BW = 128
CONV = 4
PW = 3584
QKV0 = 3 * D
DIN = 3520
EPS = 1e-6
LRU_C = 8.0
ROPE_THETA = 10000.0
SCALE = (DN + DR) ** -0.5
LOG2E = 1.4426950408889634
LN2 = 0.6931471805599453
QSCALE = SCALE * LOG2E
NEG = -1e30
HPS = 2
HPS_BWD = 1

ADAM_LR = 0.001
ADAM_B1 = 0.9
ADAM_B2 = 0.999
ADAM_EPS = 1e-08
ADAM_WD = 0.01
ADAM_STEP = 10

VMEM_LIMIT = 56 * 1024 * 1024
ROW_BLOCK = 256
ATTN_BLOCK = 1024
MESH = pl.DeviceIdType.MESH
ANY = pl.BlockSpec(memory_space=pl.ANY)
CHIP_MASKS = ((1, 0, 0), (0, 1, 0), (1, 1, 0))
ALL_MASKS = ((0, 0, 1), (1, 0, 0), (0, 1, 0), (1, 1, 0), (1, 0, 1), (0, 1, 1), (1, 1, 1))


def _cparams(n_axes):
    return pltpu.CompilerParams(dimension_semantics=("arbitrary",) * n_axes, vmem_limit_bytes=VMEM_LIMIT)


def _pick(n, cands):
    for c in cands:
        if n % c == 0:
            return c
    return n


def _rowwise(name, body, T, tb, row_ins, vec_ins, row_outs, acc_outs=(), scratch=(), reverse=False):
    n = T // tb

    def rmap(i):
        return (n - 1 - i) if reverse else i

    in_specs = []
    for (_, w, cb) in row_ins:
        in_specs.append(pl.BlockSpec((tb, w), functools.partial(lambda i, cb: (rmap(i), cb), cb=cb)))
    for v in vec_ins:
        in_specs.append(pl.BlockSpec(v.shape, functools.partial(lambda i, nd: (0,) * nd, nd=v.ndim)))
    out_specs, out_shape = [], []
    for (w, dt) in row_outs:
        out_specs.append(pl.BlockSpec((tb, w), lambda i: (rmap(i), 0)))
        out_shape.append(jax.ShapeDtypeStruct((T, w), dt))
    for (s, dt) in acc_outs:
        out_specs.append(pl.BlockSpec(s, functools.partial(lambda i, nd: (0,) * nd, nd=len(s))))
        out_shape.append(jax.ShapeDtypeStruct(s, dt))

    def kern(*refs):
        body(*refs)

    return pl.pallas_call(
        kern, name=name, grid=(n,), in_specs=in_specs, out_specs=out_specs, out_shape=out_shape,
        scratch_shapes=list(scratch), compiler_params=_cparams(1),
    )(*[a for (a, _, _) in row_ins], *vec_ins)


def _matmul(name, a, b, mode, out_dtype=f32, scatter=None):
    if mode == "nn":
        (M, K), N = a.shape, b.shape[1]
    elif mode == "nt":
        (M, K), N = a.shape, b.shape[0]
    else:
        (K, M), N = a.shape, b.shape[1]
    wide = (1792, 1408, 1024, 512, 256, 128)
    tm = _pick(M, (512, 256, 128)) if mode != "tn" else _pick(M, (1024, 1408, 512, 256, 128))
    tn = _pick(N, wide)
    tk = K if mode != "tn" else _pick(K, (1024, 512, 256, 128))
    nk = K // tk
    dims = {"nn": (((1,), (0,)), ((), ())), "nt": (((1,), (1,)), ((), ())), "tn": (((0,), (0,)), ((), ()))}[mode]
    if mode == "tn":
        a_spec = pl.BlockSpec((tk, tm), lambda j, i, k: (k, i))
    else:
        a_spec = pl.BlockSpec((tm, tk), lambda j, i, k: (i, k))
    once = {"pipeline_mode": pl.Buffered(1)} if (mode != "tn" and N == tn) else {}
    if mode == "nt":
        b_spec = pl.BlockSpec((tn, tk), lambda j, i, k: (j, k), **once)
    else:
        b_spec = pl.BlockSpec((tk, tn), lambda j, i, k: (k, j), **once)
    o_spec = pl.BlockSpec((tm, tn), lambda j, i, k: (i, j))
    use_acc = nk > 1 and out_dtype != f32

    def kern(a_ref, b_ref, o_ref, *scr):
        k = pl.program_id(2)
        d = lax.dot_general(a_ref[...].astype(bf16), b_ref[...].astype(bf16), dims, preferred_element_type=f32)
        if nk == 1:
            o_ref[...] = d.astype(out_dtype)
        else:
            acc = scr[0] if use_acc else o_ref

            @pl.when(k == 0)
            def _():
                acc[...] = d

            @pl.when(k > 0)
            def _():
                acc[...] += d

            if use_acc:
                @pl.when(k == nk - 1)
                def _():
                    o_ref[...] = acc[...].astype(out_dtype)

    grid = (N // tn, M // tm, nk)
    scratch = [pltpu.VMEM((tm, tn), f32)] if use_acc else []
    if scatter is None:
        return pl.pallas_call(
            kern, name=name, grid=grid, in_specs=[a_spec, b_spec], out_specs=o_spec,
            out_shape=jax.ShapeDtypeStruct((M, N), out_dtype), scratch_shapes=scratch, compiler_params=_cparams(3),
        )(a, b)

    ns = len(scatter)

    def kern_with_scatter(*refs):
        a_ref, b_ref, s_in, o_ref, s_out = refs[0], refs[1], refs[2:2 + ns], refs[2 + ns], refs[3 + ns:3 + 2 * ns]
        scr, sems = refs[3 + 2 * ns:3 + 2 * ns + len(scratch)], refs[3 + 2 * ns + len(scratch):]
        at = [pl.program_id(d) for d in range(3)]

        @pl.when((at[0] == 0) & (at[1] == 0) & (at[2] == 0))
        def _():
            for cp in _scatter_copies(s_in, s_out, *sems):
                cp.start()

        kern(a_ref, b_ref, o_ref, *scr)

        @pl.when((at[0] == grid[0] - 1) & (at[1] == grid[1] - 1) & (at[2] == grid[2] - 1))
        def _():
            for cp in _scatter_copies(s_in, s_out, *sems):
                cp.wait()

    return pl.pallas_call(
        kern_with_scatter, name=name, grid=grid, in_specs=[a_spec, b_spec] + [ANY] * ns,
        out_specs=[o_spec] + [ANY] * ns,
        out_shape=[jax.ShapeDtypeStruct((M, N), out_dtype)] + _scatter_shapes(scatter),
        scratch_shapes=scratch + _scatter_sems(scatter), compiler_params=_cparams(3),
    )(a, b, *scatter)


def _colsum(v):
    return jnp.sum(v, axis=0, keepdims=True)


def _init_acc(step, *refs):
    @pl.when(step == 0)
    def _():
        for r in refs:
            r[...] = jnp.zeros_like(r)


def _norm_mod(name, x, sc, sh, T, tb):
    def body(x_ref, sc_ref, sh_ref, h_ref):
        xv = x_ref[...]
        rstd = lax.rsqrt(jnp.mean(xv * xv, axis=-1, keepdims=True) + EPS)
        h_ref[...] = (xv * rstd * (1.0 + sc_ref[...]) + sh_ref[...]).astype(bf16)

    return _rowwise(name, body, T, tb, [(x, D, 0)], [sc, sh], [(D, bf16)])[0]


def _resid_norm_mod(name, x, g, m, sc, sh, T, tb):
    def body(x_ref, m_ref, g_ref, sc_ref, sh_ref, x2_ref, h_ref):
        xv = x_ref[...] + g_ref[...] * m_ref[...]
        x2_ref[...] = xv
        rstd = lax.rsqrt(jnp.mean(xv * xv, axis=-1, keepdims=True) + EPS)
        h_ref[...] = (xv * rstd * (1.0 + sc_ref[...]) + sh_ref[...]).astype(bf16)

    return _rowwise(name, body, T, tb, [(x, D, 0), (m, D, 0)], [g, sc, sh], [(D, f32), (D, bf16)])


def _norm_mod_bwd(name, dh, x, sc, dres, T, tb, resid=None):
    def norm_part(dh_ref, x_ref, dr_ref, sc_ref, dx_ref, dsc_ref, dsh_ref):
        xv = x_ref[...]
        dhv = dh_ref[...]
        rstd = lax.rsqrt(jnp.mean(xv * xv, axis=-1, keepdims=True) + EPS)
        nv = xv * rstd
        dn = dhv * (1.0 + sc_ref[...])
        dxv = dr_ref[...] + rstd * (dn - nv * jnp.mean(dn * nv, axis=-1, keepdims=True))
        dx_ref[...] = dxv
        dsc_ref[...] += _colsum(dhv * nv)
        dsh_ref[...] += _colsum(dhv)
        return dxv

    if resid is None:
        def body(dh_ref, x_ref, dr_ref, sc_ref, dx_ref, dsc_ref, dsh_ref):
            _init_acc(pl.program_id(0), dsc_ref, dsh_ref)
            norm_part(dh_ref, x_ref, dr_ref, sc_ref, dx_ref, dsc_ref, dsh_ref)

        return _rowwise(name, body, T, tb, [(dh, D, 0), (x, D, 0), (dres, D, 0)], [sc], [(D, f32)],
                        [((1, D), f32)] * 2)

    def body_resid(dh_ref, x_ref, dr_ref, m_ref, sc_ref, g_ref, dx_ref, dm_ref, dsc_ref, dsh_ref, dg_ref):
        _init_acc(pl.program_id(0), dsc_ref, dsh_ref, dg_ref)
        dxv = norm_part(dh_ref, x_ref, dr_ref, sc_ref, dx_ref, dsc_ref, dsh_ref)
        dm_ref[...] = (dxv * g_ref[...]).astype(bf16)
        dg_ref[...] += _colsum(dxv * m_ref[...])

    g, m = resid
    return _rowwise(name, body_resid, T, tb, [(dh, D, 0), (x, D, 0), (dres, D, 0), (m, D, 0)], [sc, g],
                    [(D, f32), (D, bf16)], [((1, D), f32)] * 3)


def _expm1_neg(y):
    poly = y * (1.0 + y * (0.5 + y * (1.0 / 6 + y * (1.0 / 24))))
    return jnp.where(y > -0.05, poly, jnp.exp(y) - 1.0)


def _lru_gates(r, sp, reset):
    log_a = -LRU_C * r * sp
    a = jnp.where(reset, 0.0, jnp.exp(log_a))
    mult = jnp.where(reset, 1.0, jnp.sqrt(-_expm1_neg(2.0 * log_a)))
    return a, mult


def _block_dot(v, w_ref, dims):
    outs = [lax.dot_general(v[:, n * BW:(n + 1) * BW], w_ref[n], dims, preferred_element_type=f32) for n in range(NB)]
    return jnp.concatenate(outs, axis=1)


def _lru_fwd(name, proj, pos, cw, cb, wa, ba, wx, bx, ap, T, tb):
    nsteps = tb.bit_length() - 1

    def body(x_ref, pos_ref, cw_ref, cb_ref, wa_ref, ba_ref, wx_ref, bx_ref, ap_ref,
             u_ref, r_ref, i_ref, h_ref, prev_x, carry_h):
        _init_acc(pl.program_id(0), prev_x, carry_h)
        x = x_ref[...].astype(f32)
        px = prev_x[...]
        row = lax.broadcasted_iota(jnp.int32, (tb, 1), 0)
        u = x * cw_ref[CONV - 1:CONV, :] + cb_ref[...]
        for j in range(1, CONV):
            z = jnp.where(row >= tb - j, px, x)
            u = u + pltpu.roll(z, j, 0) * cw_ref[CONV - 1 - j:CONV - j, :]
        ub = u.astype(bf16)
        nn = (((1,), (0,)), ((), ()))
        r = jax.nn.sigmoid(_block_dot(ub, wa_ref, nn) + ba_ref[...])
        ig = jax.nn.sigmoid(_block_dot(ub, wx_ref, nn) + bx_ref[...])
        sp = jax.nn.softplus(-ap_ref[...])
        reset = pos_ref[...] == 0
        a, mult = _lru_gates(r, sp, reset)
        b = u * ig * mult
        sub = row & 7
        for s in (1, 2, 4):
            keep = sub >= s
            a_sh = jnp.where(keep, pltpu.roll(a, s, 0), 1.0)
            b_sh = jnp.where(keep, pltpu.roll(b, s, 0), 0.0)
            b = a * b_sh + b
            a = a * a_sh
        carry = carry_h[...]
        for g in range(tb // 8):
            rows = slice(8 * g, 8 * g + 8)
            hg = b[rows] + a[rows] * carry
            h_ref[rows, :] = hg
            carry = hg[7:8]
        carry_h[...] = carry
        u_ref[...] = u
        r_ref[...] = r
        i_ref[...] = ig
        prev_x[...] = x

    return _rowwise(name, body, T, tb, [(proj, D, 0), (pos, 1, 0)], [cw, cb, wa, ba, wx, bx, ap],
                    [(D, f32)] * 4, scratch=[pltpu.VMEM((tb, D), f32), pltpu.VMEM((1, D), f32)])


def _lru_bwd(name, dh, proj, u, r, ig, h, pos, cw, wa, wx, ap, T, tb):
    n = T // tb
    nsteps = tb.bit_length() - 1
    t8 = tb // 8

    def rmap(i):
        return n - 1 - i

    def body(dh_ref, x_ref, u_ref, r_ref, i_ref, h_ref, hp_ref, pos_ref, cw_ref, wa_ref, wx_ref, ap_ref,
             dx_ref, dcw_ref, dcb_ref, dwa_ref, dba_ref, dwx_ref, dbx_ref, dsp_ref, next_du, carry_g, carry_a):
        step = pl.program_id(0)
        _init_acc(step, dcw_ref, dcb_ref, dwa_ref, dba_ref, dwx_ref, dbx_ref, dsp_ref, next_du, carry_g, carry_a)
        blk = n - 1 - step
        x = x_ref[...].astype(f32)
        u = u_ref[...]
        r = r_ref[...]
        ig = i_ref[...]
        h = h_ref[...]
        row = lax.broadcasted_iota(jnp.int32, (tb, 1), 0)
        sp = jax.nn.softplus(-ap_ref[...])
        reset = pos_ref[...] == 0
        a, mult = _lru_gates(r, sp, reset)
        A = pltpu.roll(jnp.where(row == 0, carry_a[...], a), tb - 1, 0)
        B = dh_ref[...]
        sub = row & 7
        for s in (1, 2, 4):
            keep = sub < 8 - s
            a_sh = jnp.where(keep, pltpu.roll(A, tb - s, 0), 1.0)
            b_sh = jnp.where(keep, pltpu.roll(B, tb - s, 0), 0.0)
            B = B + A * b_sh
            A = A * a_sh
        carry = carry_g[...]
        parts = [None] * (tb // 8)
        for grp in reversed(range(tb // 8)):
            rows = slice(8 * grp, 8 * grp + 8)
            parts[grp] = B[rows] + A[rows] * carry
            carry = parts[grp][0:1]
        g = jnp.concatenate(parts, axis=0)
        carry_g[...] = carry
        carry_a[...] = a[0:1, :]
        h_last = jnp.where(blk == 0, 0.0, hp_ref[7:8, :])
        h_prev = pltpu.roll(jnp.where(row == tb - 1, h_last, h), 1, 0)
        da = g * h_prev
        gu = g * u
        dlog_a = jnp.where(reset, 0.0, da * a - gu * ig * (a * a) / mult)
        du = g * ig * mult
        di = gu * mult
        dr = dlog_a * (-LRU_C * sp)
        dsp_ref[...] += _colsum(dlog_a * (-LRU_C * r)) * (-jax.nn.sigmoid(-ap_ref[...]))
        dzr = dr * r * (1.0 - r)
        dzi = di * ig * (1.0 - ig)
        dzr_b = dzr.astype(bf16)
        dzi_b = dzi.astype(bf16)
        nt = (((1,), (1,)), ((), ()))
        du = du + _block_dot(dzr_b, wa_ref, nt) + _block_dot(dzi_b, wx_ref, nt)
        ub = u.astype(bf16)
        tn = (((0,), (0,)), ((), ()))
        for nb in range(NB):
            sl = slice(nb * BW, (nb + 1) * BW)
            dwa_ref[nb] += lax.dot_general(ub[:, sl], dzr_b[:, sl], tn, preferred_element_type=f32)
            dwx_ref[nb] += lax.dot_general(ub[:, sl], dzi_b[:, sl], tn, preferred_element_type=f32)
        dba_ref[...] += _colsum(dzr)
        dbx_ref[...] += _colsum(dzi)
        dcb_ref[...] += _colsum(du)
        ndu = next_du[...]
        dx = du * cw_ref[CONV - 1:CONV, :]
        dcw_ref[CONV - 1:CONV, :] += _colsum(x * du)
        for j in range(1, CONV):
            up = pltpu.roll(jnp.where(row < j, ndu, du), tb - j, 0)
            dx = dx + up * cw_ref[CONV - 1 - j:CONV - j, :]
            dcw_ref[CONV - 1 - j:CONV - j, :] += _colsum(x * up)
        dx_ref[...] = dx.astype(bf16)
        next_du[...] = du

    row_specs = [pl.BlockSpec((tb, D), lambda i: (rmap(i), 0)) for _ in range(6)]
    in_specs = row_specs + [
        pl.BlockSpec((8, D), lambda i: (jnp.maximum(rmap(i) * t8 - 1, 0), 0)),
        pl.BlockSpec((tb, 1), lambda i: (rmap(i), 0)),
        pl.BlockSpec((CONV, D), lambda i: (0, 0)),
        pl.BlockSpec((NB, BW, BW), lambda i: (0, 0, 0)),
        pl.BlockSpec((NB, BW, BW), lambda i: (0, 0, 0)),
        pl.BlockSpec((1, D), lambda i: (0, 0)),
    ]
    vec = lambda s: pl.BlockSpec(s, functools.partial(lambda i, nd: (0,) * nd, nd=len(s)))
    acc_shapes = [(CONV, D), (1, D), (NB, BW, BW), (1, D), (NB, BW, BW), (1, D), (1, D)]
    return pl.pallas_call(
        body, name=name, grid=(n,), in_specs=in_specs,
        out_specs=[pl.BlockSpec((tb, D), lambda i: (rmap(i), 0))] + [vec(s) for s in acc_shapes],
        out_shape=[jax.ShapeDtypeStruct((T, D), bf16)] + [jax.ShapeDtypeStruct(s, f32) for s in acc_shapes],
        scratch_shapes=[pltpu.VMEM((tb, D), f32), pltpu.VMEM((1, D), f32), pltpu.VMEM((1, D), f32)],
        compiler_params=_cparams(1),
    )(dh, proj, u, r, ig, h, h, pos, cw, wa, wx, ap)


def _rope_tables(name, pos, invf, T, tb):
    def body(pos_ref, invf_ref, cos_ref, sa_ref, sb_ref):
        ang = pos_ref[...].astype(f32) * invf_ref[...]
        lane = lax.broadcasted_iota(jnp.int32, (tb, DH), 1)
        first = (lane >= DN) & (lane < DN + DR // 2)
        second = (lane >= DN + DR // 2) & (lane < DN + DR)
        c = jnp.cos(ang)
        s = jnp.sin(ang)
        cos_ref[...] = jnp.where(lane < DN, 1.0, jnp.where(first | second, c, 0.0))
        sa_ref[...] = jnp.where(first, -s, 0.0)
        sb_ref[...] = jnp.where(second, s, 0.0)

    return _rowwise(name, body, T, tb, [(pos, 1, 0)], [invf], [(DH, f32)] * 3)


def _rope_heads(z, cos, sa, sb, transposed=False, shared=None):
    half = DR // 2
    parts = []
    for hd in range(NH):
        hi = z[:, hd * DH + DN:(hd + 1) * DH]
        if shared is not None:
            hi = hi + shared
        if transposed:
            hi = hi * cos + pltpu.roll(hi * sa, half, 1) + pltpu.roll(hi * sb, DN - half, 1)
        else:
            hi = hi * cos + pltpu.roll(hi, DN - half, 1) * sa + pltpu.roll(hi, half, 1) * sb
        parts += [z[:, hd * DH:hd * DH + DN], hi]
    return parts


def _mla_prep(name, proj, cosp, sap, sbp, gq, gkv, wq, wk, wv, T, tb):
    def body(qkv_ref, cos_ref, sa_ref, sb_ref, gq_ref, gkv_ref, wq_ref, wk_ref, wv_ref,
             q_ref, k_ref, v_ref, cq_ref, ckv_ref):
        qkv = qkv_ref[...].astype(f32)
        qd = qkv[:, :QR]
        kvd = qkv[:, QR:QR + KVR]
        slab = qkv[:, QR + KVR:]
        nq = qd * lax.rsqrt(jnp.mean(qd * qd, axis=-1, keepdims=True) + EPS)
        cq = (nq * gq_ref[...]).astype(bf16)
        cq_scaled = (nq * (gq_ref[...] * QSCALE)).astype(bf16)
        ckv = (kvd * lax.rsqrt(jnp.mean(kvd * kvd, axis=-1, keepdims=True) + EPS) * gkv_ref[...]).astype(bf16)
        cos, sa, sb = cos_ref[:, DN:], sa_ref[:, DN:], sb_ref[:, DN:]
        qpre = jnp.dot(cq_scaled, wq_ref[...], preferred_element_type=f32)
        kpre = jnp.dot(ckv, wk_ref[...], preferred_element_type=f32)
        q_ref[...] = jnp.concatenate(_rope_heads(qpre, cos, sa, sb), axis=1).astype(bf16)
        k_ref[...] = jnp.concatenate(_rope_heads(kpre, cos, sa, sb, shared=slab), axis=1).astype(bf16)
        v_ref[...] = jnp.dot(ckv, wv_ref[...], preferred_element_type=f32).astype(bf16)
        cq_ref[...] = cq
        ckv_ref[...] = ckv

    return _rowwise(name, body, T, tb,
                    [(proj, 512, QKV0 // 512), (cosp, DH, 0), (sap, DH, 0), (sbp, DH, 0)],
                    [gq, gkv, wq, wk, wv],
                    [(NH * DH, bf16), (NH * DH, bf16), (NH * DV, bf16), (QR, bf16), (KVR, bf16)])


def _mla_bwd(name, dq, dk, dv, proj, cosp, sap, sbp, gq, gkv, wq, wk, wv, T, tb):
    def body(dq_ref, dk_ref, dv_ref, qkv_ref, cos_ref, sa_ref, sb_ref, gq_ref, gkv_ref, wq_ref, wk_ref, wv_ref,
             dqkv_ref, dqp_ref, dkp_ref, dgq_ref, dgkv_ref):
        _init_acc(pl.program_id(0), dgq_ref, dgkv_ref)
        cos, sa, sb = cos_ref[:, DN:], sa_ref[:, DN:], sb_ref[:, DN:]
        dqp_b = jnp.concatenate(_rope_heads(dq_ref[...] * SCALE, cos, sa, sb, transposed=True), axis=1).astype(bf16)
        dk_parts = _rope_heads(dk_ref[...], cos, sa, sb, transposed=True)
        dkp_b = jnp.concatenate(dk_parts, axis=1).astype(bf16)
        dqp_ref[...] = dqp_b
        dkp_ref[...] = dkp_b
        nt = (((1,), (1,)), ((), ()))
        dcq = lax.dot_general(dqp_b, wq_ref[...], nt, preferred_element_type=f32)
        dckv = (lax.dot_general(dkp_b, wk_ref[...], nt, preferred_element_type=f32)
                + lax.dot_general(dv_ref[...], wv_ref[...], nt, preferred_element_type=f32))
        dslab = dk_parts[1]
        for hd in range(1, NH):
            dslab = dslab + dk_parts[2 * hd + 1]
        qkv = qkv_ref[...].astype(f32)
        qd = qkv[:, :QR]
        kvd = qkv[:, QR:QR + KVR]
        nq = qd * lax.rsqrt(jnp.mean(qd * qd, axis=-1, keepdims=True) + EPS)
        nkv = kvd * lax.rsqrt(jnp.mean(kvd * kvd, axis=-1, keepdims=True) + EPS)
        dgq_ref[...] += _colsum(dcq * nq)
        dgkv_ref[...] += _colsum(dckv * nkv)
        dnq = dcq * gq_ref[...]
        dnkv = dckv * gkv_ref[...]
        rq = lax.rsqrt(jnp.mean(qd * qd, axis=-1, keepdims=True) + EPS)
        rkv = lax.rsqrt(jnp.mean(kvd * kvd, axis=-1, keepdims=True) + EPS)
        dqd = rq * (dnq - nq * jnp.mean(dnq * nq, axis=-1, keepdims=True))
        dkvd = rkv * (dnkv - nkv * jnp.mean(dnkv * nkv, axis=-1, keepdims=True))
        dqkv_ref[...] = jnp.concatenate([dqd, dkvd, dslab], axis=1).astype(bf16)

    return _rowwise(name, body, T, tb,
                    [(dq, NH * DH, 0), (dk, NH * DH, 0), (dv, NH * DV, 0), (proj, 512, QKV0 // 512),
                     (cosp, DH, 0), (sap, DH, 0), (sbp, DH, 0)],
                    [gq, gkv, wq, wk, wv],
                    [(512, bf16), (NH * DH, bf16), (NH * DH, bf16)],
                    [((1, QR), f32), ((1, KVR), f32)])


def _flash_fwd(name, q, k, v, T, tq, pack=None):
    nq = T // tq
    unroll = 2

    def body(q_ref, k_ref, v_ref, o_ref, lse_ref):
        i = pl.program_id(1)
        nt = (((1,), (1,)), ((), ()))
        qs = [q_ref[:, hh * DH:(hh + 1) * DH] for hh in range(HPS)]

        def one(hh, j, carry, masked):
            m, l, acc = carry
            off = pl.multiple_of(j * tq, tq)
            kj = k_ref[pl.ds(off, tq), hh * DH:(hh + 1) * DH]
            vj = v_ref[pl.ds(off, tq), hh * DV:(hh + 1) * DV]
            s = lax.dot_general(qs[hh], kj, nt, preferred_element_type=f32)
            if masked:
                rr = lax.broadcasted_iota(jnp.int32, (tq, tq), 0)
                cc = lax.broadcasted_iota(jnp.int32, (tq, tq), 1)
                s = jnp.where(cc <= rr, s, NEG)
            m_new = jnp.maximum(m, jnp.max(s, axis=-1, keepdims=True))
            alpha = jnp.exp2(m - m_new)
            p = jnp.exp2(s - m_new)
            l = alpha * l + jnp.sum(p, axis=-1, keepdims=True)
            acc = alpha * acc + jnp.dot(p.astype(bf16), vj, preferred_element_type=f32)
            return m_new, l, acc

        def step(j, carries, masked):
            return tuple(one(hh, j, carries[hh], masked) for hh in range(HPS))

        def pair(jj, carries):
            for u in range(unroll):
                carries = step(jj * unroll + u, carries, False)
            return carries

        init1 = (jnp.full((tq, 1), NEG, f32), jnp.zeros((tq, 1), f32), jnp.zeros((tq, DV), f32))
        carry = lax.fori_loop(0, i // unroll, pair, tuple(init1 for _ in range(HPS)))
        carry = lax.fori_loop((i // unroll) * unroll, i, lambda j, c: step(j, c, False), carry)
        carry = step(i, carry, True)
        for hh in range(HPS):
            m, l, acc = carry[hh]
            o_ref[:, hh * DV:(hh + 1) * DV] = (acc / l).astype(bf16)
            lse_ref[hh] = m + jnp.log2(l)

    def body_with_gather(q_ref, k_ref, v_ref, *rest):
        nw = len(pack)
        w_refs, (o_ref, lse_ref), wall_refs, sems = rest[:nw], rest[nw:nw + 2], rest[nw + 2:2 * nw + 2], rest[2 * nw + 2:]
        h = pl.program_id(0)
        i = pl.program_id(1)

        @pl.when((h == 0) & (i == 0))
        def _():
            mine, first, _ = _gather_copies(w_refs, wall_refs, *sems)
            _gather_start(mine, first)

        @pl.when((h == ngrp // 2) & (i == 0))
        def _():
            _, first, passed = _gather_copies(w_refs, wall_refs, *sems)
            _gather_forward(first, passed)

        body(q_ref, k_ref, v_ref, o_ref, lse_ref)

        @pl.when((h == ngrp - 1) & (i == nq - 1))
        def _():
            _gather_finish(*_gather_copies(w_refs, wall_refs, *sems))

    ngrp = NH // HPS
    in_specs = [pl.BlockSpec((tq, HPS * DH), lambda h, i: (i, h)),
                pl.BlockSpec((T, HPS * DH), lambda h, i: (0, h), pipeline_mode=pl.Buffered(1)),
                pl.BlockSpec((T, HPS * DV), lambda h, i: (0, h), pipeline_mode=pl.Buffered(1))]
    out_specs = [pl.BlockSpec((tq, HPS * DV), lambda h, i: (i, h)), pl.BlockSpec((HPS, tq, 1), lambda h, i: (h, i, 0))]
    out_shape = [jax.ShapeDtypeStruct((T, NH * DV), bf16), jax.ShapeDtypeStruct((NH, T, 1), f32)]
    if pack is None:
        return pl.pallas_call(body, name=name, grid=(ngrp, nq), in_specs=in_specs, out_specs=out_specs,
                              out_shape=out_shape, compiler_params=_cparams(2))(q, k, v)
    return pl.pallas_call(
        body_with_gather, name=name, grid=(ngrp, nq), in_specs=in_specs + [ANY] * len(pack),
        out_specs=out_specs + [ANY] * len(pack),
        out_shape=out_shape + [jax.ShapeDtypeStruct((4,) + a.shape, a.dtype) for a in pack],
        scratch_shapes=_gather_sems(pack), compiler_params=_cparams(2),
    )(q, k, v, *pack)


def _flash_bwd(name, q, k, v, do, lse, dl, T, tq, scatter=None):
    HPS = HPS_BWD
    nq = T // tq

    def body(k_ref, v_ref, q_ref, do_ref, lse_ref, dl_ref, dk_ref, dv_ref, dq_ref):
        j = pl.program_id(1)

        @pl.when(j == 0)
        def _():
            dq_ref[...] = jnp.zeros_like(dq_ref)

        nt = (((1,), (1,)), ((), ()))
        tn = (((0,), (0,)), ((), ()))
        ks = [k_ref[:, hh * DH:(hh + 1) * DH] for hh in range(HPS)]
        vs = [v_ref[:, hh * DV:(hh + 1) * DV] for hh in range(HPS)]

        def one(hh, i, carry, masked):
            dk, dv = carry
            off = pl.multiple_of(i * tq, tq)
            qi = q_ref[pl.ds(off, tq), hh * DH:(hh + 1) * DH]
            doi = do_ref[pl.ds(off, tq), hh * DV:(hh + 1) * DV]
            st = lax.dot_general(ks[hh], qi, nt, preferred_element_type=f32)
            if masked:
                kr = lax.broadcasted_iota(jnp.int32, (tq, tq), 0)
                qc = lax.broadcasted_iota(jnp.int32, (tq, tq), 1)
                st = jnp.where(qc >= kr, st, NEG)
            dpt = lax.dot_general(vs[hh], doi, nt, preferred_element_type=f32)
            pt = jnp.exp2(st - lse_ref[hh, i])
            dst = (pt * (dpt - dl_ref[hh, i])).astype(bf16)
            dv = dv + jnp.dot(pt.astype(bf16), doi, preferred_element_type=f32)
            dk = dk + jnp.dot(dst, qi, preferred_element_type=f32)
            dq_ref[pl.ds(off, tq), hh * DH:(hh + 1) * DH] += lax.dot_general(dst, ks[hh], tn, preferred_element_type=f32)
            return dk, dv

        def step(i, carries, masked):
            return tuple(one(hh, i, carries[hh], masked) for hh in range(HPS))

        zero = tuple((jnp.zeros((tq, DH), f32), jnp.zeros((tq, DV), f32)) for _ in range(HPS))
        carry = step(j, zero, True)
        carry = lax.fori_loop(j + 1, nq, lambda i, c: step(i, c, False), carry)
        for hh in range(HPS):
            dk, dv = carry[hh]
            dk_ref[:, hh * DH:(hh + 1) * DH] = dk * LN2
            dv_ref[:, hh * DV:(hh + 1) * DV] = dv.astype(bf16)

    def body_with_scatter(*refs):
        ns = len(scatter)
        ins, a_refs, outs = refs[:6], refs[6:6 + ns], refs[6 + ns:9 + ns]
        o_refs, sems = refs[9 + ns:9 + 2 * ns], refs[9 + 2 * ns:]
        h = pl.program_id(0)
        j = pl.program_id(1)

        @pl.when((h == 0) & (j == 0))
        def _():
            for cp in _scatter_copies(a_refs, o_refs, *sems):
                cp.start()

        body(*ins, *outs)

        @pl.when((h == ngrp - 1) & (j == nq - 1))
        def _():
            for cp in _scatter_copies(a_refs, o_refs, *sems):
                cp.wait()

    ngrp = NH // HPS
    in_specs = [pl.BlockSpec((tq, HPS * DH), lambda h, j: (j, h)),
                pl.BlockSpec((tq, HPS * DV), lambda h, j: (j, h)),
                pl.BlockSpec((T, HPS * DH), lambda h, j: (0, h), pipeline_mode=pl.Buffered(1)),
                pl.BlockSpec((T, HPS * DV), lambda h, j: (0, h), pipeline_mode=pl.Buffered(1)),
                pl.BlockSpec((HPS, nq, 1, tq), lambda h, j: (h, 0, 0, 0)),
                pl.BlockSpec((HPS, nq, 1, tq), lambda h, j: (h, 0, 0, 0))]
    out_specs = [pl.BlockSpec((tq, HPS * DH), lambda h, j: (j, h)),
                 pl.BlockSpec((tq, HPS * DV), lambda h, j: (j, h)),
                 pl.BlockSpec((T, HPS * DH), lambda h, j: (0, h), pipeline_mode=pl.Buffered(1))]
    out_shape = [jax.ShapeDtypeStruct((T, NH * DH), f32), jax.ShapeDtypeStruct((T, NH * DV), bf16),
                 jax.ShapeDtypeStruct((T, NH * DH), f32)]
    if scatter is None:
        return pl.pallas_call(body, name=name, grid=(ngrp, nq), in_specs=in_specs, out_specs=out_specs,
                              out_shape=out_shape, compiler_params=_cparams(2))(k, v, q, do, lse, dl)
    ns = len(scatter)
    return pl.pallas_call(
        body_with_scatter, name=name, grid=(ngrp, nq), in_specs=in_specs + [ANY] * ns,
        out_specs=out_specs + [ANY] * ns, out_shape=out_shape + _scatter_shapes(scatter),
        scratch_shapes=_scatter_sems(scatter), compiler_params=_cparams(2),
    )(k, v, q, do, lse, dl, *scatter)


def _gate_mix(name, proj, ya, o, T, tb):
    def body(ga_ref, gb_ref, ya_ref, o_ref, y_ref):
        y_ref[...] = (jax.nn.sigmoid(ga_ref[...].astype(f32)) * ya_ref[...]
                      + jax.nn.sigmoid(gb_ref[...].astype(f32)) * o_ref[...].astype(f32)).astype(bf16)

    return _rowwise(name, body, T, tb, [(proj, D, 1), (proj, D, 2), (ya, D, 0), (o, D, 0)], [], [(D, bf16)])[0]


def _gate_mix_bwd(name, dy, proj, ya, o, T, tb):
    def body(dy_ref, ga_ref, gb_ref, ya_ref, o_ref, dga_ref, dgb_ref, dya_ref, do_ref, dl_ref):
        dy = dy_ref[...]
        sa = jax.nn.sigmoid(ga_ref[...].astype(f32))
        sb = jax.nn.sigmoid(gb_ref[...].astype(f32))
        ov = o_ref[...].astype(f32)
        dga_ref[...] = (dy * ya_ref[...] * sa * (1.0 - sa)).astype(bf16)
        dgb_ref[...] = (dy * ov * sb * (1.0 - sb)).astype(bf16)
        dya_ref[...] = dy * sa
        do = dy * sb
        do_ref[...] = do.astype(bf16)
        prod = do * ov
        for hd in range(NH):
            dl_ref[:, hd:hd + 1] = jnp.sum(prod[:, hd * DV:(hd + 1) * DV], axis=-1, keepdims=True)

    return _rowwise(name, body, T, tb, [(dy, D, 0), (proj, D, 1), (proj, D, 2), (ya, D, 0), (o, D, 0)], [],
                    [(D, bf16), (D, bf16), (D, f32), (D, bf16), (NH, f32)])


def _ffn_in_swiglu(name, h, w, T):
    tm = _pick(T, (512, 256, 128))

    def body(h_ref, w_ref, z_ref, a_ref):
        z = jnp.dot(h_ref[...], w_ref[...], preferred_element_type=f32)
        g = z[:, :FH]
        z_ref[...] = z.astype(bf16)
        a_ref[...] = (g * jax.nn.sigmoid(g) * z[:, FH:]).astype(bf16)

    return pl.pallas_call(
        body, name=name, grid=(2, T // tm),
        in_specs=[pl.BlockSpec((tm, D), lambda j, i: (i, 0)), pl.BlockSpec((D, 2 * FH), lambda j, i: (0, j))],
        out_specs=[pl.BlockSpec((tm, 2 * FH), lambda j, i: (i, j)), pl.BlockSpec((tm, FH), lambda j, i: (i, j))],
        out_shape=[jax.ShapeDtypeStruct((T, 2 * DFF), bf16), jax.ShapeDtypeStruct((T, DFF), bf16)],
        compiler_params=_cparams(2),
    )(h, w)


def _dact_swiglu_bwd(name, dfo, w, z, T):
    tm = _pick(T, (512, 256, 128))

    def body(d_ref, w_ref, z_ref, o_ref):
        da = lax.dot_general(d_ref[...], w_ref[...], (((1,), (1,)), ((), ())), preferred_element_type=f32)
        g = z_ref[:, :FH].astype(f32)
        u = z_ref[:, FH:].astype(f32)
        sg = jax.nn.sigmoid(g)
        o_ref[:, :FH] = (da * u * sg * (1.0 + g * (1.0 - sg))).astype(bf16)
        o_ref[:, FH:] = (da * g * sg).astype(bf16)

    return pl.pallas_call(
        body, name=name, grid=(2, T // tm),
        in_specs=[pl.BlockSpec((tm, D), lambda j, i: (i, 0)), pl.BlockSpec((FH, D), lambda j, i: (j, 0)),
                  pl.BlockSpec((tm, 2 * FH), lambda j, i: (i, j))],
        out_specs=pl.BlockSpec((tm, 2 * FH), lambda j, i: (i, j)),
        out_shape=jax.ShapeDtypeStruct((T, 2 * DFF), bf16), compiler_params=_cparams(2),
    )(dfo, w, z)


def _final(name, x, g, m, gfin, tgt, T, tb):
    def body(x_ref, m_ref, t_ref, g_ref, gf_ref, dx_ref, dm_ref, loss_ref, dgf_ref, dg_ref):
        _init_acc(pl.program_id(0), loss_ref, dgf_ref, dg_ref)
        mv = m_ref[...]
        xv = x_ref[...] + g_ref[...] * mv
        rstd = lax.rsqrt(jnp.mean(xv * xv, axis=-1, keepdims=True) + EPS)
        nv = xv * rstd
        err = nv * gf_ref[...] - t_ref[...]
        loss_ref[...] += 0.5 * jnp.sum(jnp.mean(err * err, axis=-1, keepdims=True), axis=0, keepdims=True)
        dy = err * (1.0 / D)
        dgf_ref[...] += _colsum(dy * nv)
        dn = dy * gf_ref[...]
        dxv = rstd * (dn - nv * jnp.mean(dn * nv, axis=-1, keepdims=True))
        dx_ref[...] = dxv
        dm_ref[...] = (dxv * g_ref[...]).astype(bf16)
        dg_ref[...] += _colsum(dxv * mv)

    return _rowwise(name, body, T, tb, [(x, D, 0), (m, D, 0), (tgt, D, 0)], [g, gfin], [(D, f32), (D, bf16)],
                    [((1, 1), f32), ((1, D), f32), ((1, D), f32)])


def _adamw(name, w, g, m, v):
    R, C = w.shape
    tb = _pick(R, (256, 128, 64, 32, 16, 8))
    c1 = 1.0 - ADAM_B1 ** ADAM_STEP
    c2 = 1.0 - ADAM_B2 ** ADAM_STEP

    def body(w_ref, g_ref, m_ref, v_ref, d_ref, m2_ref, v2_ref):
        gv = g_ref[...]
        m2 = ADAM_B1 * m_ref[...] + (1.0 - ADAM_B1) * gv
        v2 = ADAM_B2 * v_ref[...] + (1.0 - ADAM_B2) * (gv * gv)
        m2_ref[...] = m2
        v2_ref[...] = v2
        d_ref[...] = -ADAM_LR * ((m2 / c1) / (jnp.sqrt(v2 / c2) + ADAM_EPS) + ADAM_WD * w_ref[...])

    return _rowwise(name, body, R, tb, [(w, C, 0), (g, C, 0), (m, C, 0), (v, C, 0)], [], [(C, f32)] * 3)


def _ada_fwd(name, c_all, w_ada, b_sh):
    nl, _, ns = w_ada.shape

    def body(c_ref, w_ref, b_ref, o_ref):
        o_ref[0] = jnp.dot(c_ref[...].astype(bf16), w_ref[0].astype(bf16), preferred_element_type=f32) + b_ref[0]

    return pl.pallas_call(
        body, name=name, grid=(nl,),
        in_specs=[pl.BlockSpec((8, D), lambda l: (0, 0)), pl.BlockSpec((1, D, ns), lambda l: (l, 0, 0)),
                  pl.BlockSpec((1, 1, ns), lambda l: (l, 0, 0))],
        out_specs=pl.BlockSpec((1, 8, ns), lambda l: (l, 0, 0)),
        out_shape=jax.ShapeDtypeStruct((nl, 8, ns), f32), compiler_params=_cparams(1),
    )(c_all, w_ada, b_sh)


def _ada_bwd(name, c_t, dm_sh):
    nl, _, ns = dm_sh.shape

    def body(c_ref, dm_ref, o_ref):
        acc = c_ref[:, 0:1] * dm_ref[0, 0:1, :]
        for d in range(1, 8):
            acc = acc + c_ref[:, d:d + 1] * dm_ref[0, d:d + 1, :]
        o_ref[0] = acc

    return pl.pallas_call(
        body, name=name, grid=(nl,),
        in_specs=[pl.BlockSpec((D, 8), lambda l: (0, 0)), pl.BlockSpec((1, 8, ns), lambda l: (l, 0, 0))],
        out_specs=pl.BlockSpec((1, D, ns), lambda l: (l, 0, 0)),
        out_shape=jax.ShapeDtypeStruct((nl, D, ns), f32), compiler_params=_cparams(1),
    )(c_t, dm_sh)


def _sum_slots(name, a):
    n, R, C = a.shape
    tb = _pick(R, (256, 128, 64, 32, 16, 8))

    def body(a_ref, o_ref):
        acc = a_ref[0].astype(f32)
        for s in range(1, n):
            acc = acc + a_ref[s].astype(f32)
        o_ref[...] = acc

    return pl.pallas_call(
        body, name=name, grid=(R // tb,), in_specs=[pl.BlockSpec((n, tb, C), lambda i: (0, i, 0))],
        out_specs=pl.BlockSpec((tb, C), lambda i: (i, 0)), out_shape=jax.ShapeDtypeStruct((R, C), f32),
        compiler_params=_cparams(1),
    )(a)


def _chunks(rows, dtype, want):
    unit = 16 if dtype == bf16 else 8
    for n in range(want, 0, -1):
        if rows % n == 0 and (rows // n) % unit == 0:
            return n, rows // n
    return 1, rows


def _my_chip():
    return 2 * lax.axis_index("x") + lax.axis_index("y")


def _pair_sum(name, arrs, gots):
    na = len(arrs)

    def body(*refs):
        for k in range(na):
            a_ref, g_ref, o_ref = refs[k], refs[na + k], refs[2 * na + k]
            o_ref[...] = (a_ref[...].astype(f32) + g_ref[...].astype(f32)).astype(o_ref.dtype)

    in_specs = [pl.BlockSpec((1, a.shape[1] // 4, a.shape[2]), lambda s, i: (s, 2 * lax.axis_index("c") + i, 0))
                for a in arrs]
    half_specs = [pl.BlockSpec((1, g.shape[1] // 2, g.shape[2]), lambda s, i: (s, i, 0)) for g in gots]
    return pl.pallas_call(
        body, name=name, grid=(4, 2), in_specs=in_specs + half_specs, out_specs=half_specs,
        out_shape=[jax.ShapeDtypeStruct(g.shape, a.dtype) for a, g in zip(arrs, gots)], compiler_params=_cparams(2),
    )(*arrs, *gots)


def _chip_sum(name, sums, lands, layer, bufs=None):
    na = len(sums)

    def body(*refs):
        for k in range(na):
            s_ref, l_ref, o_ref = refs[k], refs[na + k], refs[len(refs) - na + k]
            acc = s_ref[0].astype(f32)
            for j in range(l_ref.shape[0]):
                acc = acc + l_ref[j].astype(f32)
            o_ref[...] = acc.reshape(o_ref.shape)

    in_specs = [pl.BlockSpec((1, s.shape[1] // 2, s.shape[2]), lambda i: (_my_chip(), i, 0)) for s in sums]
    in_specs += [pl.BlockSpec((3, l.shape[1] // 2, l.shape[2]), lambda i: (0, i, 0)) for l in lands]
    if layer is None:
        out_specs = [pl.BlockSpec((s.shape[1] // 2, s.shape[2]), lambda i: (i, 0)) for s in sums]
        out_shape = [jax.ShapeDtypeStruct(s.shape[1:], f32) for s in sums]
    else:
        out_specs = [pl.BlockSpec((1, 1, s.shape[1] // 2, s.shape[2]), lambda i: (layer, lax.axis_index("c"), i, 0))
                     for s in sums]
        out_shape = [jax.ShapeDtypeStruct((2, 2) + s.shape[1:], f32) for s in sums]
    aliases = {}
    operands = list(sums) + list(lands)
    if bufs is not None:
        in_specs += [ANY] * na
        aliases = {2 * na + k: k for k in range(na)}
        operands += list(bufs)
    return pl.pallas_call(
        body, name=name, grid=(2,), in_specs=in_specs, out_specs=out_specs, out_shape=out_shape,
        input_output_aliases=aliases, compiler_params=_cparams(1),
    )(*operands)


def _coords():
    return lax.axis_index("x"), lax.axis_index("y"), lax.axis_index("c")


def _flip(v, m):
    return (1 - v) if m else v


def _peer(x, y, c, mask):
    return (_flip(x, mask[0]), _flip(y, mask[1]), _flip(c, mask[2]))


def _all_gather(name, arr, masks, whole_mesh):
    nslots = 8 if whole_mesh else 4

    def slot(x, y, c):
        return 4 * x + 2 * y + c if whole_mesh else 2 * x + y

    def body(a_ref, o_ref, send_sems, recv_sems, local_sem):
        x, y, c = _coords()
        mine = pltpu.make_async_copy(a_ref, o_ref.at[slot(x, y, c)], local_sem)
        mine.start()
        copies = []
        for k, mask in enumerate(masks):
            cp = pltpu.make_async_remote_copy(
                src_ref=a_ref, dst_ref=o_ref.at[slot(x, y, c)], send_sem=send_sems.at[k], recv_sem=recv_sems.at[k],
                device_id=_peer(x, y, c, mask), device_id_type=MESH)
            cp.start()
            copies.append(cp)
        for cp in copies:
            cp.wait()
        mine.wait()

    return pl.pallas_call(
        body, name=name, in_specs=[ANY], out_specs=ANY,
        out_shape=jax.ShapeDtypeStruct((nslots,) + arr.shape, arr.dtype),
        scratch_shapes=[pltpu.SemaphoreType.DMA((len(masks),)), pltpu.SemaphoreType.DMA((len(masks),)),
                        pltpu.SemaphoreType.DMA],
    )(arr)


def _gather_weights(name, arrs):
    na = len(arrs)

    def body(*refs):
        mine, first, passed = _gather_copies(refs[:na], refs[na:2 * na], *refs[2 * na:])
        _gather_start(mine, first)
        _gather_forward(first, passed)
        _gather_finish(mine, first, passed)

    return pl.pallas_call(
        body, name=name, in_specs=[ANY] * na, out_specs=[ANY] * na,
        out_shape=[jax.ShapeDtypeStruct((4,) + a.shape, a.dtype) for a in arrs], scratch_shapes=_gather_sems(arrs),
    )(*arrs)


def _gather_plan(shape, dtype):
    big = (shape[0] // 2) * shape[1] * 2 >= (2 << 20)
    return _chunks(shape[0] // 2, dtype, 3 if big else 1), _chunks(shape[0], dtype, 4 if big else 1)


def _gather_sems(arrs):
    plans = [_gather_plan(a.shape, a.dtype) for a in arrs]
    nremote = sum(len(CHIP_MASKS) * p[0][0] for p in plans)
    nlocal = sum(p[1][0] for p in plans)
    return [pltpu.SemaphoreType.DMA((nremote,))] * 4 + [pltpu.SemaphoreType.DMA((nlocal,))]


def _gather_copies(w_refs, o_refs, sa, ra, sb, rb, local_sems):
    x, y, c = _coords()
    chip = 2 * x + y
    mine, first, passed = [], [], []
    for w_ref, o_ref in zip(w_refs, o_refs):
        (nch, rows), (nloc, lrows) = _gather_plan(w_ref.shape, w_ref.dtype)
        rh = w_ref.shape[0] // 2
        for k in range(nloc):
            part = pl.ds(k * lrows, lrows)
            mine.append(pltpu.make_async_copy(w_ref.at[part], o_ref.at[chip, part], local_sems.at[len(mine)]))
        for mask in CHIP_MASKS:
            px, py, _ = _peer(x, y, c, mask)
            for ch in range(nch):
                k = len(first)
                part = pl.ds(c * rh + ch * rows, rows)
                first.append(pltpu.make_async_remote_copy(
                    src_ref=w_ref.at[part], dst_ref=o_ref.at[chip, part], send_sem=sa.at[k], recv_sem=ra.at[k],
                    device_id=(px, py, c), device_id_type=MESH))
                landed = o_ref.at[2 * px + py, part]
                passed.append(pltpu.make_async_remote_copy(
                    src_ref=landed, dst_ref=landed, send_sem=sb.at[k], recv_sem=rb.at[k],
                    device_id=(x, y, 1 - c), device_id_type=MESH))
    return mine, first, passed


def _gather_start(mine, first):
    for cp in mine + first:
        cp.start()


def _gather_forward(first, passed):
    for a, b in zip(first, passed):
        a.wait_recv()
        b.start()


def _gather_finish(mine, first, passed):
    for a, b in zip(first, passed):
        a.wait_send()
        b.wait()
    for cp in mine:
        cp.wait()


def _swap_halves(name, arrs):
    na = len(arrs)
    plan = []
    for a in arrs:
        h = a.shape[1] // 2
        nch, rows = _chunks(h, a.dtype, 2 if h * a.shape[2] * 2 >= (1 << 20) else 1)
        plan.append([(s, ch * rows, rows) for s in range(a.shape[0]) for ch in range(nch)])
    ncopies = sum(len(p) for p in plan)

    def body(*refs):
        a_refs, got_refs = refs[:na], refs[na:2 * na]
        send_sems, recv_sems = refs[2 * na:]
        x, y, c = _coords()
        cps = []
        for ai in range(na):
            h = arrs[ai].shape[1] // 2
            for (s, r0, rows) in plan[ai]:
                k = len(cps)
                cp = pltpu.make_async_remote_copy(
                    src_ref=a_refs[ai].at[s, pl.ds((1 - c) * h + r0, rows)], dst_ref=got_refs[ai].at[s, pl.ds(r0, rows)],
                    send_sem=send_sems.at[k], recv_sem=recv_sems.at[k], device_id=(x, y, 1 - c), device_id_type=MESH)
                cp.start()
                cps.append(cp)
        for cp in cps:
            cp.wait()

    halves = [jax.ShapeDtypeStruct((a.shape[0], a.shape[1] // 2, a.shape[2]), a.dtype) for a in arrs]
    return pl.pallas_call(
        body, name=name, in_specs=[ANY] * na, out_specs=[ANY] * na, out_shape=halves,
        scratch_shapes=[pltpu.SemaphoreType.DMA((ncopies,))] * 2,
    )(*arrs)


def _scatter_chips(name, arrs):
    na = len(arrs)

    def body(*refs):
        cps = _scatter_copies(refs[:na], refs[na:2 * na], *refs[2 * na:])
        for cp in cps:
            cp.start()
        for cp in cps:
            cp.wait()

    return pl.pallas_call(
        body, name=name, in_specs=[ANY] * na, out_specs=[ANY] * na,
        out_shape=_scatter_shapes(arrs), scratch_shapes=_scatter_sems(arrs),
    )(*arrs)


def _scatter_shapes(arrs):
    return [jax.ShapeDtypeStruct((len(CHIP_MASKS),) + a.shape[1:], a.dtype) for a in arrs]


def _scatter_sems(arrs):
    return [pltpu.SemaphoreType.DMA((len(arrs), len(CHIP_MASKS)))] * 2


def _scatter_copies(a_refs, o_refs, send_sems, recv_sems):
    x, y, c = _coords()
    cps = []
    for k in range(len(a_refs)):
        for j, mask in enumerate(CHIP_MASKS):
            px, py, pc = _peer(x, y, c, mask)
            cps.append(pltpu.make_async_remote_copy(
                src_ref=a_refs[k].at[2 * px + py], dst_ref=o_refs[k].at[j],
                send_sem=send_sems.at[k, j], recv_sem=recv_sems.at[k, j],
                device_id=(px, py, pc), device_id_type=MESH))
    return cps


def _join_halves(name, bufs):
    na = len(bufs)

    def body(*refs):
        o_refs = refs[na:2 * na]
        send_sems, recv_sems = refs[2 * na:]
        x, y, c = _coords()
        cps = []
        for k in range(na):
            for l in range(2):
                part = o_refs[k].at[l, c]
                cp = pltpu.make_async_remote_copy(
                    src_ref=part, dst_ref=part, send_sem=send_sems.at[k, l], recv_sem=recv_sems.at[k, l],
                    device_id=(x, y, 1 - c), device_id_type=MESH)
                cp.start()
                cps.append(cp)
        for cp in cps:
            cp.wait()

    return pl.pallas_call(
        body, name=name, in_specs=[ANY] * na, out_specs=[ANY] * na,
        out_shape=[jax.ShapeDtypeStruct(b.shape, b.dtype) for b in bufs],
        input_output_aliases={k: k for k in range(na)}, scratch_shapes=[pltpu.SemaphoreType.DMA((na, 2))] * 2,
    )(*bufs)


BIG = ("w_in", "w_uq", "w_ukv", "w_out", "w_ffn_in", "w_ffn_out")
SMALL = ("conv_w", "conv_b", "lru_wa", "lru_ba", "lru_wx", "lru_bx", "lru_a_param", "q_norm_g", "kv_norm_g",
         "final_norm_g")
SMALL_SHAPES = {"conv_w": (2, CONV, D), "conv_b": (2, D), "lru_wa": (2, NB, BW, BW), "lru_ba": (2, NB, BW),
                "lru_wx": (2, NB, BW, BW), "lru_bx": (2, NB, BW), "lru_a_param": (2, D), "q_norm_g": (2, QR),
                "kv_norm_g": (2, KVR), "final_norm_g": (D,)}
SMALL_ROWS = 640
FFN_CHIP_ORDER = (0, 2, 1, 3)


def _size(shape):
    n = 1
    for s in shape:
        n *= s
    return n


def _pack_rows(parts, rows, dtype):
    flat = jnp.concatenate([p.reshape(-1).astype(dtype) for p in parts])
    pad = rows * 1024 - flat.shape[0]
    if pad:
        flat = jnp.concatenate([flat, jnp.zeros((pad,), dtype)])
    return flat.reshape(rows, 1024)


def _unpack_rows(pack, shapes):
    flat = pack.reshape(-1)
    out, off = [], 0
    for s in shapes:
        n = _size(s)
        out.append(flat[off:off + n].reshape(s))
        off += n
    return out


def _inproj_weights(g_in, g_uq, g_ukv):
    w_in = jnp.concatenate([g_in[s] for s in range(4)], axis=1)
    win_p = jnp.concatenate([w_in[:, :D], w_in[:, 1472:2496], w_in[:, 2496:], w_in[:, D:1472],
                             jnp.zeros((D, 64), w_in.dtype)], axis=1)
    w_uq = g_uq.reshape(QR, NH, DN + DR)
    wq_p = jnp.concatenate([w_uq, jnp.zeros((QR, NH, DH - DN - DR), w_uq.dtype)], axis=2).reshape(QR, NH * DH)
    w_ukv = jnp.transpose(g_ukv.reshape(4, KVR, NH, (DN + DV) // 4), (1, 2, 0, 3)).reshape(KVR, NH, DN + DV)
    wk_p = jnp.concatenate([w_ukv[..., :DN], jnp.zeros((KVR, NH, DH - DN), w_ukv.dtype)], axis=2).reshape(KVR, NH * DH)
    wv = w_ukv[..., DN:].reshape(KVR, NH * DV)
    return dict(win=win_p, wq=wq_p, wk=wk_p, wv=wv)


def _other_weights(g_out, g_fi, g_fo):
    ko, ki, kf = g_out.shape[1] // 2, g_fi.shape[1] // 2, g_fo.shape[1] // 2
    wout = [g_out[:, l * ko:(l + 1) * ko].reshape(4 * ko, D) for l in range(2)]
    wfi = [jnp.concatenate([g_fi[s, l * ki:(l + 1) * ki] for s in FFN_CHIP_ORDER], axis=1) for l in range(2)]
    wfo = [g_fo[:, l * kf:(l + 1) * kf].reshape(4 * kf, D) for l in range(2)]
    return dict(wout=wout, wfi=wfi, wfo=wfo)


def _shard_grads_inproj(dwin, dwq, dwk, dwv):
    w_in = jnp.concatenate([dwin[:, :D], dwin[:, QKV0:QKV0 + 448], dwin[:, D:QKV0]], axis=1)
    w_in = jnp.transpose(w_in.reshape(D, 4, DIN // 4), (1, 0, 2))
    w_uq = dwq.reshape(QR, NH, DH)[..., :DN + DR].reshape(4, QR // 4, NH * (DN + DR))
    w_ukv = jnp.concatenate([dwk.reshape(KVR, NH, DH)[..., :DN], dwv.reshape(KVR, NH, DV)], axis=2)
    w_ukv = jnp.transpose(w_ukv.reshape(KVR, NH, 4, (DN + DV) // 4), (2, 0, 1, 3)).reshape(4, KVR, NH * (DN + DV) // 4)
    return [w_in, w_uq, w_ukv]


def _shard_grads_rest(dwout, dwfi, dwfo):
    tiles = jnp.split(dwfi, 4, axis=1)
    wfi = jnp.stack([tiles[FFN_CHIP_ORDER.index(s)] for s in range(4)])
    return [dwout.reshape(4, D // 4, D), wfi, dwfo.reshape(4, DFF // 4, D)]


def kernel(x, c, positions, w_ada, b_ada, w_in, conv_w, conv_b, lru_wa, lru_ba, lru_wx, lru_bx, lru_a_param, q_norm_g, kv_norm_g, w_uq, w_ukv, w_out, w_ffn_in, w_ffn_out, final_norm_g, loss_target, m_w_ada, m_b_ada, m_w_in, m_conv_w, m_conv_b, m_lru_wa, m_lru_ba, m_lru_wx, m_lru_bx, m_lru_a_param, m_q_norm_g, m_kv_norm_g, m_w_uq, m_w_ukv, m_w_out, m_w_ffn_in, m_w_ffn_out, m_final_norm_g, v_w_ada, v_b_ada, v_w_in, v_conv_w, v_conv_b, v_lru_wa, v_lru_ba, v_lru_wx, v_lru_bx, v_lru_a_param, v_q_norm_g, v_kv_norm_g, v_w_uq, v_w_ukv, v_w_out, v_w_ffn_in, v_w_ffn_out, v_final_norm_g):
    T = x.shape[1]
    tb = min(ROW_BLOCK, T)
    tq = min(ATTN_BLOCK, T)
    nq = T // tq
    mx, my, mc = _coords()
    chip = 2 * mx + my
    dev = 4 * mx + 2 * my + mc
    weights = dict(w_ada=w_ada, b_ada=b_ada, w_in=w_in, conv_w=conv_w, conv_b=conv_b, lru_wa=lru_wa, lru_ba=lru_ba,
                   lru_wx=lru_wx, lru_bx=lru_bx, lru_a_param=lru_a_param, q_norm_g=q_norm_g, kv_norm_g=kv_norm_g,
                   w_uq=w_uq, w_ukv=w_ukv, w_out=w_out, w_ffn_in=w_ffn_in, w_ffn_out=w_ffn_out,
                   final_norm_g=final_norm_g)
    mom = dict(w_ada=m_w_ada, b_ada=m_b_ada, w_in=m_w_in, conv_w=m_conv_w, conv_b=m_conv_b, lru_wa=m_lru_wa,
               lru_ba=m_lru_ba, lru_wx=m_lru_wx, lru_bx=m_lru_bx, lru_a_param=m_lru_a_param, q_norm_g=m_q_norm_g,
               kv_norm_g=m_kv_norm_g, w_uq=m_w_uq, w_ukv=m_w_ukv, w_out=m_w_out, w_ffn_in=m_w_ffn_in,
               w_ffn_out=m_w_ffn_out, final_norm_g=m_final_norm_g)
    var = dict(w_ada=v_w_ada, b_ada=v_b_ada, w_in=v_w_in, conv_w=v_conv_w, conv_b=v_conv_b, lru_wa=v_lru_wa,
               lru_ba=v_lru_ba, lru_wx=v_lru_wx, lru_bx=v_lru_bx, lru_a_param=v_lru_a_param, q_norm_g=v_q_norm_g,
               kv_norm_g=v_kv_norm_g, w_uq=v_w_uq, w_ukv=v_w_ukv, w_out=v_w_out, w_ffn_in=v_w_ffn_in,
               w_ffn_out=v_w_ffn_out, final_norm_g=v_final_norm_g)
    order = list(weights)

    ns_ada = w_ada.shape[2]
    c_all = _all_gather("gather_c", c, ALL_MASKS, True).reshape(8, D)
    b_sh = lax.dynamic_slice_in_dim(b_ada, chip * ns_ada, ns_ada, axis=1).reshape(2, 1, ns_ada)
    mod_sh = _ada_fwd("ada_fwd", c_all, w_ada, b_sh)
    mod_all = _all_gather("gather_mod", mod_sh.reshape(16, ns_ada), CHIP_MASKS, False)
    mod_mine = lax.dynamic_index_in_dim(mod_all.reshape(4, 2, 8, ns_ada), dev, axis=2, keepdims=False)
    mod = jnp.transpose(mod_mine, (1, 0, 2)).reshape(2, 6, 1, D)

    flat = lambda w: w.astype(bf16).reshape(w.shape[0], w.shape[1], -1)
    b_in, b_uq, b_ukv = flat(w_in), flat(w_uq), flat(w_ukv)
    late = [b_in[1], b_uq[1], b_ukv[1]] + [flat(w).reshape(-1, w.shape[-1]) for w in (w_out, w_ffn_in, w_ffn_out)]
    W0 = _inproj_weights(*_gather_weights("gather_weights_early", [b_in[0], b_uq[0], b_ukv[0]]))
    W = {n: [W0[n], None] for n in W0}
    wa_b = lru_wa.astype(bf16)
    wx_b = lru_wx.astype(bf16)

    pos = positions.reshape(T, 1)
    invf = ROPE_THETA ** (-jnp.arange(0, DR, 2, dtype=f32) / DR)
    invf_p = jnp.concatenate([jnp.zeros((DN,), f32), invf, invf, jnp.zeros((DH - DN - DR,), f32)]).reshape(1, DH)
    cosp, sap, sbp = _rope_tables("rope_tables", pos, invf_p, T, tb)

    ns_cw = conv_w.shape[2]
    cw_all = _all_gather("gather_conv_w", conv_w.reshape(2 * CONV, ns_cw), CHIP_MASKS, False)
    cw_full = jnp.transpose(cw_all, (1, 0, 2)).reshape(2, CONV, D)
    vec = lambda a, l: a[l].reshape(1, -1)

    xl = x.reshape(T, D)
    saved = []
    h1 = _norm_mod("norm_mod_0", xl, mod[0, 1], mod[0, 0], T, tb)
    for l in range(2):
        sh_m, sc_m, g_m, sh_f, sc_f, g_f = [mod[l, j] for j in range(6)]
        proj = _matmul(f"proj_{l}", h1, W["win"][l], "nn", bf16)
        u, r, ig, ya = _lru_fwd(f"lru_fwd_{l}", proj, pos, cw_full[l], vec(conv_b, l), wa_b[l], vec(lru_ba, l),
                                wx_b[l], vec(lru_bx, l), vec(lru_a_param, l), T, tb)
        qp, kp, vp, cq, ckv = _mla_prep(f"mla_prep_{l}", proj, cosp, sap, sbp, vec(q_norm_g, l), vec(kv_norm_g, l),
                                        W["wq"][l], W["wk"][l], W["wv"][l], T, tb)
        if l == 0:
            o, lse, *late_all = _flash_fwd("flash_fwd_0", qp, kp, vp, T, tq, late)
            W1 = _inproj_weights(*late_all[:3])
            for n in W1:
                W[n][1] = W1[n]
            W.update(_other_weights(*late_all[3:]))
        else:
            o, lse = _flash_fwd("flash_fwd_1", qp, kp, vp, T, tq)
        y = _gate_mix(f"gate_mix_{l}", proj, ya, o, T, tb)
        mo = _matmul(f"out_proj_{l}", y, W["wout"][l], "nn")
        x2, h2 = _resid_norm_mod(f"resid_norm_f_{l}", xl, g_m, mo, sc_f, sh_f, T, tb)
        gu, act = _ffn_in_swiglu(f"ffn_in_{l}", h2, W["wfi"][l], T)
        fo = _matmul(f"ffn_out_{l}", act, W["wfo"][l], "nn")
        saved.append(dict(x=xl, h1=h1, proj=proj, u=u, r=r, ig=ig, ya=ya, qp=qp, kp=kp, vp=vp, cq=cq, ckv=ckv, o=o,
                          lse=lse, y=y, mo=mo, x2=x2, h2=h2, gu=gu, act=act, fo=fo))
        if l == 0:
            xl, h1 = _resid_norm_mod("resid_norm_m_1", x2, g_f, fo, mod[1, 1], mod[1, 0], T, tb)

    dx, dfo, loss_part, dgfin, dg_f = _final("final", saved[1]["x2"], mod[1, 5], saved[1]["fo"],
                                             final_norm_g.reshape(1, D), loss_target.reshape(T, D), T, tb)
    loss = lax.psum(loss_part[0, 0], ("x", "y", "c"))

    gl = {n: [None, None] for n in ("win", "wq", "wk", "wv", "wout", "wfi", "wfo", "conv_w", "conv_b", "lru_wa",
                                    "lru_ba", "lru_wx", "lru_bx", "lru_a_param", "q_norm_g", "kv_norm_g")}
    dmod = [None, None]

    def pair_sums(tag, arrs):
        return list(_pair_sum(f"rs_pair_sum_{tag}", arrs, _swap_halves(f"rs_swap_halves_{tag}", arrs)))

    def inproj_pieces(l):
        return _shard_grads_inproj(gl["win"][l], gl["wq"][l], gl["wk"][l], gl["wv"][l])

    def rest_pieces(l):
        return _shard_grads_rest(gl["wout"][l], gl["wfi"][l], gl["wfo"][l])

    for l in (1, 0):
        s = saved[l]
        sh_m, sc_m, g_m, sh_f, sc_f, g_f = [mod[l, j] for j in range(6)]
        dgu = _dact_swiglu_bwd(f"d_act_{l}", dfo, W["wfo"][l], s["gu"], T)
        gl["wfo"][l] = _matmul(f"dw_ffn_out_{l}", s["act"], dfo, "tn", bf16)
        dh2 = _matmul(f"d_h2_{l}", dgu, W["wfi"][l], "nt")
        gl["wfi"][l] = _matmul(f"dw_ffn_in_{l}", s["h2"], dgu, "tn", bf16)
        dx2, dmo, dsc_f, dsh_f, dg_m = _norm_mod_bwd(f"norm_bwd_f_{l}", dh2, s["x2"], sc_f, dx, T, tb, (g_m, s["mo"]))
        dy = _matmul(f"d_y_{l}", dmo, W["wout"][l], "nt")
        gl["wout"][l] = _matmul(f"dw_out_{l}", s["y"], dmo, "tn", bf16)
        dga, dgb, dya, do, dlt = _gate_mix_bwd(f"gate_mix_bwd_{l}", dy, s["proj"], s["ya"], s["o"], T, tb)
        lse_r = s["lse"].reshape(NH, nq, 1, tq)
        dl_r = jnp.transpose(dlt).reshape(NH, nq, 1, tq)
        if l == 0:
            sums0a = pair_sums("0a", rest_pieces(0))
            dk, dv, dq, *lands = _flash_bwd("flash_bwd_0", s["qp"], s["kp"], s["vp"], do, lse_r, dl_r, T, tq,
                                            sums1 + sums0a)
            rs_bufs = list(_chip_sum("rs_chip_sum_1", sums1, lands[:6], 1))
            rs_bufs[3:] = _chip_sum("rs_chip_sum_0a", sums0a, lands[6:], 0, rs_bufs[3:])
        else:
            dk, dv, dq = _flash_bwd("flash_bwd_1", s["qp"], s["kp"], s["vp"], do, lse_r, dl_r, T, tq)
        dqkv, dqp, dkp, dgq, dgkv = _mla_bwd(f"mla_bwd_{l}", dq, dk, dv, s["proj"], cosp, sap, sbp,
                                              vec(q_norm_g, l), vec(kv_norm_g, l), W["wq"][l], W["wk"][l], W["wv"][l],
                                              T, tb)
        gl["wq"][l] = _matmul(f"dw_uq_{l}", s["cq"], dqp, "tn", bf16)
        gl["wk"][l] = _matmul(f"dw_uk_{l}", s["ckv"], dkp, "tn", bf16)
        gl["wv"][l] = _matmul(f"dw_uv_{l}", s["ckv"], dv, "tn", bf16)
        dxl, dcw, dcb, dwa, dba, dwx, dbx, dsp = _lru_bwd(
            f"lru_bwd_{l}", dya, s["proj"], s["u"], s["r"], s["ig"], s["ya"], pos, cw_full[l], wa_b[l], wx_b[l],
            vec(lru_a_param, l), T, tb)
        dproj = jnp.concatenate([dxl, dga, dgb, dqkv], axis=1)
        gl["win"][l] = _matmul(f"dw_in_{l}", s["h1"], dproj, "tn", bf16)
        gl["conv_w"][l], gl["conv_b"][l] = dcw, dcb[0]
        gl["lru_wa"][l], gl["lru_ba"][l], gl["lru_wx"][l], gl["lru_bx"][l] = dwa, dba[0], dwx, dbx[0]
        gl["lru_a_param"][l] = dsp[0]
        gl["q_norm_g"][l], gl["kv_norm_g"][l] = dgq[0], dgkv[0]
        if l == 1:
            dh1 = _matmul("d_h1_1", dproj, W["win"][l], "nt")
            dx, dfo_below, dsc_m, dsh_m, dg_f_below = _norm_mod_bwd(
                "norm_bwd_m_1", dh1, s["x"], sc_m, dx2, T, tb, (mod[0, 5], saved[0]["fo"]))
            sums1 = pair_sums("1", inproj_pieces(1) + rest_pieces(1))
        else:
            gfull = {n: jnp.stack(gl[n]) for n in SMALL if n != "final_norm_g"}
            gfull["final_norm_g"] = dgfin[0]
            gsmall = _pack_rows([gfull[n] for n in SMALL], SMALL_ROWS, f32).reshape(4, SMALL_ROWS // 4, 1024)
            sums0b = pair_sums("0b", inproj_pieces(0) + [gsmall])
            dh1, *lands0b = _matmul("d_h1_0", dproj, W["win"][l], "nt", scatter=sums0b)
            dx, dsc_m, dsh_m = _norm_mod_bwd("norm_bwd_m_0", dh1, s["x"], sc_m, dx2, T, tb)
        dmod[l] = jnp.concatenate([dsh_m, dsc_m, dg_m, dsh_f, dsc_f, dg_f], axis=1)
        if l == 1:
            dfo, dg_f = dfo_below, dg_f_below
    grad_x = dx.reshape(1, T, D)

    dmod_all = _all_gather("gather_dmod", jnp.concatenate(dmod, axis=1), ALL_MASKS, True).reshape(8, 2, 6, D)
    dmod_sum = _sum_slots("sum_dmod", dmod_all.reshape(8, 12, D)).reshape(2, 6 * D)
    dm_sh = lax.dynamic_slice_in_dim(jnp.transpose(dmod_all.reshape(8, 2, 6 * D), (1, 0, 2)), chip * ns_ada, ns_ada, axis=2)
    grads = {"w_ada": _ada_bwd("ada_bwd", jnp.transpose(c_all), dm_sh), "b_ada": dmod_sum}

    rs_bufs[:3] = _chip_sum("rs_chip_sum_0b", sums0b[:3], lands0b[:3], 0, rs_bufs[:3])
    tot_s = _chip_sum("rs_chip_sum_small", sums0b[3:], lands0b[3:], None)[0]
    bufs = _join_halves("rs_join_halves", rs_bufs)
    small_all = _all_gather("gather_small_grads", tot_s, ALL_MASKS, True).reshape(SMALL_ROWS, 1024)
    for n, buf in zip(BIG, bufs):
        grads[n] = buf.reshape(weights[n].shape)
    for n, gs in zip(SMALL, _unpack_rows(small_all, [SMALL_SHAPES[n] for n in SMALL])):
        grads[n] = gs
    grads["conv_w"] = lax.dynamic_slice_in_dim(grads["conv_w"], chip * ns_cw, ns_cw, axis=2)

    deltas, new_m, new_v = {}, {}, {}
    for n in order:
        w = weights[n]
        view = (1, w.shape[0]) if w.ndim == 1 else (_size(w.shape[:-1]), w.shape[-1]) if w.shape[-1] >= 128 else (w.shape[0], _size(w.shape[1:]))
        d, m2, v2 = _adamw(f"adamw_{n}", w.reshape(view), grads[n].reshape(view), mom[n].reshape(view), var[n].reshape(view))
        deltas[n], new_m[n], new_v[n] = d.reshape(w.shape), m2.reshape(w.shape), v2.reshape(w.shape)
    return (loss, grad_x, *[grads[n] for n in order], *[deltas[n] for n in order], *[new_m[n] for n in order],
            *[new_v[n] for n in order])
```

```python
import functools

import jax
import jax.numpy as jnp
from jax import lax
from jax.experimental import pallas as pl
from jax.experimental.pallas import tpu as pltpu

f32 = jnp.float32
bf16 = jnp.bfloat16

D = 1024
NH = 8
DN = 128
DR = 64
DV = 128
DH = 256
QR = 256
KVR = 128
DFF = 2816
FH = DFF // 2
NB = 8
BW = 128
CONV = 4
PW = 3584
QKV0 = 3 * D
DIN = 3520
EPS = 1e-6
LRU_C = 8.0
ROPE_THETA = 10000.0
SCALE = (DN + DR) ** -0.5
LOG2E = 1.4426950408889634
LN2 = 0.6931471805599453
QSCALE = SCALE * LOG2E
NEG = -1e30
HPS = 2
HPS_BWD = 2

ADAM_LR = 0.001
ADAM_B1 = 0.9
ADAM_B2 = 0.999
ADAM_EPS = 1e-08
ADAM_WD = 0.01
ADAM_STEP = 10

VMEM_LIMIT = 56 * 1024 * 1024
ROW_BLOCK = 256
ATTN_BLOCK = 1024
MESH = pl.DeviceIdType.MESH
ANY = pl.BlockSpec(memory_space=pl.ANY)
CHIP_MASKS = ((1, 0, 0), (0, 1, 0), (1, 1, 0))
ALL_MASKS = ((0, 0, 1), (1, 0, 0), (0, 1, 0), (1, 1, 0), (1, 0, 1), (0, 1, 1), (1, 1, 1))


def _cparams(n_axes):
    return pltpu.CompilerParams(dimension_semantics=("arbitrary",) * n_axes, vmem_limit_bytes=VMEM_LIMIT)


def _pick(n, cands):
    for c in cands:
        if n % c == 0:
            return c
    return n


def _rowwise(name, body, T, tb, row_ins, vec_ins, row_outs, acc_outs=(), scratch=(), reverse=False):
    n = T // tb

    def rmap(i):
        return (n - 1 - i) if reverse else i

    in_specs = []
    for (_, w, cb) in row_ins:
        in_specs.append(pl.BlockSpec((tb, w), functools.partial(lambda i, cb: (rmap(i), cb), cb=cb)))
    for v in vec_ins:
        in_specs.append(pl.BlockSpec(v.shape, functools.partial(lambda i, nd: (0,) * nd, nd=v.ndim)))
    out_specs, out_shape = [], []
    for (w, dt) in row_outs:
        out_specs.append(pl.BlockSpec((tb, w), lambda i: (rmap(i), 0)))
        out_shape.append(jax.ShapeDtypeStruct((T, w), dt))
    for (s, dt) in acc_outs:
        out_specs.append(pl.BlockSpec(s, functools.partial(lambda i, nd: (0,) * nd, nd=len(s))))
        out_shape.append(jax.ShapeDtypeStruct(s, dt))

    def kern(*refs):
        body(*refs)

    return pl.pallas_call(
        kern, name=name, grid=(n,), in_specs=in_specs, out_specs=out_specs, out_shape=out_shape,
        scratch_shapes=list(scratch), compiler_params=_cparams(1),
    )(*[a for (a, _, _) in row_ins], *vec_ins)


def _matmul(name, a, b, mode, out_dtype=f32, scatter=None):
    if mode == "nn":
        (M, K), N = a.shape, b.shape[1]
    elif mode == "nt":
        (M, K), N = a.shape, b.shape[0]
    else:
        (K, M), N = a.shape, b.shape[1]
    wide = (1792, 1408, 1024, 512, 256, 128)
    tm = _pick(M, (512, 256, 128)) if mode != "tn" else _pick(M, (1024, 1408, 512, 256, 128))
    tn = _pick(N, wide)
    tk = K if mode != "tn" else _pick(K, (1024, 512, 256, 128))
    nk = K // tk
    dims = {"nn": (((1,), (0,)), ((), ())), "nt": (((1,), (1,)), ((), ())), "tn": (((0,), (0,)), ((), ()))}[mode]
    if mode == "tn":
        a_spec = pl.BlockSpec((tk, tm), lambda j, i, k: (k, i))
    else:
        a_spec = pl.BlockSpec((tm, tk), lambda j, i, k: (i, k))
    once = {"pipeline_mode": pl.Buffered(1)} if (mode != "tn" and N == tn) else {}
    if mode == "nt":
        b_spec = pl.BlockSpec((tn, tk), lambda j, i, k: (j, k), **once)
    else:
        b_spec = pl.BlockSpec((tk, tn), lambda j, i, k: (k, j), **once)
    o_spec = pl.BlockSpec((tm, tn), lambda j, i, k: (i, j))
    use_acc = nk > 1 and out_dtype != f32

    def kern(a_ref, b_ref, o_ref, *scr):
        k = pl.program_id(2)
        d = lax.dot_general(a_ref[...].astype(bf16), b_ref[...].astype(bf16), dims, preferred_element_type=f32)
        if nk == 1:
            o_ref[...] = d.astype(out_dtype)
        else:
            acc = scr[0] if use_acc else o_ref

            @pl.when(k == 0)
            def _():
                acc[...] = d

            @pl.when(k > 0)
            def _():
                acc[...] += d

            if use_acc:
                @pl.when(k == nk - 1)
                def _():
                    o_ref[...] = acc[...].astype(out_dtype)

    grid = (N // tn, M // tm, nk)
    scratch = [pltpu.VMEM((tm, tn), f32)] if use_acc else []
    if scatter is None:
        return pl.pallas_call(
            kern, name=name, grid=grid, in_specs=[a_spec, b_spec], out_specs=o_spec,
            out_shape=jax.ShapeDtypeStruct((M, N), out_dtype), scratch_shapes=scratch, compiler_params=_cparams(3),
        )(a, b)

    ns = len(scatter)

    def kern_with_scatter(*refs):
        a_ref, b_ref, s_in, o_ref, s_out = refs[0], refs[1], refs[2:2 + ns], refs[2 + ns], refs[3 + ns:3 + 2 * ns]
        scr, sems = refs[3 + 2 * ns:3 + 2 * ns + len(scratch)], refs[3 + 2 * ns + len(scratch):]
        at = [pl.program_id(d) for d in range(3)]

        @pl.when((at[0] == 0) & (at[1] == 0) & (at[2] == 0))
        def _():
            for cp in _scatter_copies(s_in, s_out, *sems):
                cp.start()

        kern(a_ref, b_ref, o_ref, *scr)

        @pl.when((at[0] == grid[0] - 1) & (at[1] == grid[1] - 1) & (at[2] == grid[2] - 1))
        def _():
            for cp in _scatter_copies(s_in, s_out, *sems):
                cp.wait()

    return pl.pallas_call(
        kern_with_scatter, name=name, grid=grid, in_specs=[a_spec, b_spec] + [ANY] * ns,
        out_specs=[o_spec] + [ANY] * ns,
        out_shape=[jax.ShapeDtypeStruct((M, N), out_dtype)] + _scatter_shapes(scatter),
        scratch_shapes=scratch + _scatter_sems(scatter), compiler_params=_cparams(3),
    )(a, b, *scatter)


def _colsum(v):
    return jnp.sum(v, axis=0, keepdims=True)


def _init_acc(step, *refs):
    @pl.when(step == 0)
    def _():
        for r in refs:
            r[...] = jnp.zeros_like(r)


def _norm_mod(name, x, sc, sh, T, tb):
    def body(x_ref, sc_ref, sh_ref, h_ref):
        xv = x_ref[...]
        rstd = lax.rsqrt(jnp.mean(xv * xv, axis=-1, keepdims=True) + EPS)
        h_ref[...] = (xv * rstd * (1.0 + sc_ref[...]) + sh_ref[...]).astype(bf16)

    return _rowwise(name, body, T, tb, [(x, D, 0)], [sc, sh], [(D, bf16)])[0]


def _resid_norm_mod(name, x, g, m, sc, sh, T, tb):
    def body(x_ref, m_ref, g_ref, sc_ref, sh_ref, x2_ref, h_ref):
        xv = x_ref[...] + g_ref[...] * m_ref[...]
        x2_ref[...] = xv
        rstd = lax.rsqrt(jnp.mean(xv * xv, axis=-1, keepdims=True) + EPS)
        h_ref[...] = (xv * rstd * (1.0 + sc_ref[...]) + sh_ref[...]).astype(bf16)

    return _rowwise(name, body, T, tb, [(x, D, 0), (m, D, 0)], [g, sc, sh], [(D, f32), (D, bf16)])


def _norm_mod_bwd(name, dh, x, sc, dres, T, tb, resid=None):
    def norm_part(dh_ref, x_ref, dr_ref, sc_ref, dx_ref, dsc_ref, dsh_ref):
        xv = x_ref[...]
        dhv = dh_ref[...]
        rstd = lax.rsqrt(jnp.mean(xv * xv, axis=-1, keepdims=True) + EPS)
        nv = xv * rstd
        dn = dhv * (1.0 + sc_ref[...])
        dxv = dr_ref[...] + rstd * (dn - nv * jnp.mean(dn * nv, axis=-1, keepdims=True))
        dx_ref[...] = dxv
        dsc_ref[...] += _colsum(dhv * nv)
        dsh_ref[...] += _colsum(dhv)
        return dxv

    if resid is None:
        def body(dh_ref, x_ref, dr_ref, sc_ref, dx_ref, dsc_ref, dsh_ref):
            _init_acc(pl.program_id(0), dsc_ref, dsh_ref)
            norm_part(dh_ref, x_ref, dr_ref, sc_ref, dx_ref, dsc_ref, dsh_ref)

        return _rowwise(name, body, T, tb, [(dh, D, 0), (x, D, 0), (dres, D, 0)], [sc], [(D, f32)],
                        [((1, D), f32)] * 2)

    def body_resid(dh_ref, x_ref, dr_ref, m_ref, sc_ref, g_ref, dx_ref, dm_ref, dsc_ref, dsh_ref, dg_ref):
        _init_acc(pl.program_id(0), dsc_ref, dsh_ref, dg_ref)
        dxv = norm_part(dh_ref, x_ref, dr_ref, sc_ref, dx_ref, dsc_ref, dsh_ref)
        dm_ref[...] = (dxv * g_ref[...]).astype(bf16)
        dg_ref[...] += _colsum(dxv * m_ref[...])

    g, m = resid
    return _rowwise(name, body_resid, T, tb, [(dh, D, 0), (x, D, 0), (dres, D, 0), (m, D, 0)], [sc, g],
                    [(D, f32), (D, bf16)], [((1, D), f32)] * 3)


def _expm1_neg(y):
    poly = y * (1.0 + y * (0.5 + y * (1.0 / 6 + y * (1.0 / 24))))
    return jnp.where(y > -0.05, poly, jnp.exp(y) - 1.0)


def _lru_gates(r, sp, reset):
    log_a = -LRU_C * r * sp
    a = jnp.where(reset, 0.0, jnp.exp(log_a))
    mult = jnp.where(reset, 1.0, jnp.sqrt(-_expm1_neg(2.0 * log_a)))
    return a, mult


def _block_dot(v, w_ref, dims):
    outs = [lax.dot_general(v[:, n * BW:(n + 1) * BW], w_ref[n], dims, preferred_element_type=f32) for n in range(NB)]
    return jnp.concatenate(outs, axis=1)


def _lru_fwd(name, proj, pos, cw, cb, wa, ba, wx, bx, ap, T, tb):
    nsteps = tb.bit_length() - 1

    def body(x_ref, pos_ref, cw_ref, cb_ref, wa_ref, ba_ref, wx_ref, bx_ref, ap_ref,
             u_ref, r_ref, i_ref, h_ref, prev_x, carry_h):
        _init_acc(pl.program_id(0), prev_x, carry_h)
        x = x_ref[...].astype(f32)
        px = prev_x[...]
        row = lax.broadcasted_iota(jnp.int32, (tb, 1), 0)
        u = x * cw_ref[CONV - 1:CONV, :] + cb_ref[...]
        for j in range(1, CONV):
            z = jnp.where(row >= tb - j, px, x)
            u = u + pltpu.roll(z, j, 0) * cw_ref[CONV - 1 - j:CONV - j, :]
        ub = u.astype(bf16)
        nn = (((1,), (0,)), ((), ()))
        r = jax.nn.sigmoid(_block_dot(ub, wa_ref, nn) + ba_ref[...])
        ig = jax.nn.sigmoid(_block_dot(ub, wx_ref, nn) + bx_ref[...])
        sp = jax.nn.softplus(-ap_ref[...])
        reset = pos_ref[...] == 0
        a, mult = _lru_gates(r, sp, reset)
        b = u * ig * mult
        sub = row & 7
        for s in (1, 2, 4):
            keep = sub >= s
            a_sh = jnp.where(keep, pltpu.roll(a, s, 0), 1.0)
            b_sh = jnp.where(keep, pltpu.roll(b, s, 0), 0.0)
            b = a * b_sh + b
            a = a * a_sh
        carry = carry_h[...]
        for g in range(tb // 8):
            rows = slice(8 * g, 8 * g + 8)
            hg = b[rows] + a[rows] * carry
            h_ref[rows, :] = hg
            carry = hg[7:8]
        carry_h[...] = carry
        u_ref[...] = u
        r_ref[...] = r
        i_ref[...] = ig
        prev_x[...] = x

    return _rowwise(name, body, T, tb, [(proj, D, 0), (pos, 1, 0)], [cw, cb, wa, ba, wx, bx, ap],
                    [(D, f32)] * 4, scratch=[pltpu.VMEM((tb, D), f32), pltpu.VMEM((1, D), f32)])


def _lru_bwd(name, dh, proj, u, r, ig, h, pos, cw, wa, wx, ap, T, tb):
    n = T // tb
    nsteps = tb.bit_length() - 1
    t8 = tb // 8

    def rmap(i):
        return n - 1 - i

    def body(dh_ref, x_ref, u_ref, r_ref, i_ref, h_ref, hp_ref, pos_ref, cw_ref, wa_ref, wx_ref, ap_ref,
             dx_ref, dcw_ref, dcb_ref, dwa_ref, dba_ref, dwx_ref, dbx_ref, dsp_ref, next_du, carry_g, carry_a):
        step = pl.program_id(0)
        _init_acc(step, dcw_ref, dcb_ref, dwa_ref, dba_ref, dwx_ref, dbx_ref, dsp_ref, next_du, carry_g, carry_a)
        blk = n - 1 - step
        x = x_ref[...].astype(f32)
        u = u_ref[...]
        r = r_ref[...]
        ig = i_ref[...]
        h = h_ref[...]
        row = lax.broadcasted_iota(jnp.int32, (tb, 1), 0)
        sp = jax.nn.softplus(-ap_ref[...])
        reset = pos_ref[...] == 0
        a, mult = _lru_gates(r, sp, reset)
        A = pltpu.roll(jnp.where(row == 0, carry_a[...], a), tb - 1, 0)
        B = dh_ref[...]
        sub = row & 7
        for s in (1, 2, 4):
            keep = sub < 8 - s
            a_sh = jnp.where(keep, pltpu.roll(A, tb - s, 0), 1.0)
            b_sh = jnp.where(keep, pltpu.roll(B, tb - s, 0), 0.0)
            B = B + A * b_sh
            A = A * a_sh
        carry = carry_g[...]
        parts = [None] * (tb // 8)
        for grp in reversed(range(tb // 8)):
            rows = slice(8 * grp, 8 * grp + 8)
            parts[grp] = B[rows] + A[rows] * carry
            carry = parts[grp][0:1]
        g = jnp.concatenate(parts, axis=0)
        carry_g[...] = carry
        carry_a[...] = a[0:1, :]
        h_last = jnp.where(blk == 0, 0.0, hp_ref[7:8, :])
        h_prev = pltpu.roll(jnp.where(row == tb - 1, h_last, h), 1, 0)
        da = g * h_prev
        gu = g * u
        dlog_a = jnp.where(reset, 0.0, da * a - gu * ig * (a * a) / mult)
        du = g * ig * mult
        di = gu * mult
        dr = dlog_a * (-LRU_C * sp)
        dsp_ref[...] += _colsum(dlog_a * (-LRU_C * r)) * (-jax.nn.sigmoid(-ap_ref[...]))
        dzr = dr * r * (1.0 - r)
        dzi = di * ig * (1.0 - ig)
        dzr_b = dzr.astype(bf16)
        dzi_b = dzi.astype(bf16)
        nt = (((1,), (1,)), ((), ()))
        du = du + _block_dot(dzr_b, wa_ref, nt) + _block_dot(dzi_b, wx_ref, nt)
        ub = u.astype(bf16)
        tn = (((0,), (0,)), ((), ()))
        for nb in range(NB):
            sl = slice(nb * BW, (nb + 1) * BW)
            dwa_ref[nb] += lax.dot_general(ub[:, sl], dzr_b[:, sl], tn, preferred_element_type=f32)
            dwx_ref[nb] += lax.dot_general(ub[:, sl], dzi_b[:, sl], tn, preferred_element_type=f32)
        dba_ref[...] += _colsum(dzr)
        dbx_ref[...] += _colsum(dzi)
        dcb_ref[...] += _colsum(du)
        ndu = next_du[...]
        dx = du * cw_ref[CONV - 1:CONV, :]
        dcw_ref[CONV - 1:CONV, :] += _colsum(x * du)
        for j in range(1, CONV):
            up = pltpu.roll(jnp.where(row < j, ndu, du), tb - j, 0)
            dx = dx + up * cw_ref[CONV - 1 - j:CONV - j, :]
            dcw_ref[CONV - 1 - j:CONV - j, :] += _colsum(x * up)
        dx_ref[...] = dx.astype(bf16)
        next_du[...] = du

    row_specs = [pl.BlockSpec((tb, D), lambda i: (rmap(i), 0)) for _ in range(6)]
    in_specs = row_specs + [
        pl.BlockSpec((8, D), lambda i: (jnp.maximum(rmap(i) * t8 - 1, 0), 0)),
        pl.BlockSpec((tb, 1), lambda i: (rmap(i), 0)),
        pl.BlockSpec((CONV, D), lambda i: (0, 0)),
        pl.BlockSpec((NB, BW, BW), lambda i: (0, 0, 0)),
        pl.BlockSpec((NB, BW, BW), lambda i: (0, 0, 0)),
        pl.BlockSpec((1, D), lambda i: (0, 0)),
    ]
    vec = lambda s: pl.BlockSpec(s, functools.partial(lambda i, nd: (0,) * nd, nd=len(s)))
    acc_shapes = [(CONV, D), (1, D), (NB, BW, BW), (1, D), (NB, BW, BW), (1, D), (1, D)]
    return pl.pallas_call(
        body, name=name, grid=(n,), in_specs=in_specs,
        out_specs=[pl.BlockSpec((tb, D), lambda i: (rmap(i), 0))] + [vec(s) for s in acc_shapes],
        out_shape=[jax.ShapeDtypeStruct((T, D), bf16)] + [jax.ShapeDtypeStruct(s, f32) for s in acc_shapes],
        scratch_shapes=[pltpu.VMEM((tb, D), f32), pltpu.VMEM((1, D), f32), pltpu.VMEM((1, D), f32)],
        compiler_params=_cparams(1),
    )(dh, proj, u, r, ig, h, h, pos, cw, wa, wx, ap)


def _rope_tables(name, pos, invf, T, tb):
    def body(pos_ref, invf_ref, cos_ref, sa_ref, sb_ref):
        ang = pos_ref[...].astype(f32) * invf_ref[...]
        lane = lax.broadcasted_iota(jnp.int32, (tb, DH), 1)
        first = (lane >= DN) & (lane < DN + DR // 2)
        second = (lane >= DN + DR // 2) & (lane < DN + DR)
        c = jnp.cos(ang)
        s = jnp.sin(ang)
        cos_ref[...] = jnp.where(lane < DN, 1.0, jnp.where(first | second, c, 0.0))
        sa_ref[...] = jnp.where(first, -s, 0.0)
        sb_ref[...] = jnp.where(second, s, 0.0)

    return _rowwise(name, body, T, tb, [(pos, 1, 0)], [invf], [(DH, f32)] * 3)


def _rope_heads(z, cos, sa, sb, transposed=False, shared=None):
    half = DR // 2
    parts = []
    for hd in range(NH):
        hi = z[:, hd * DH + DN:(hd + 1) * DH]
        if shared is not None:
            hi = hi + shared
        if transposed:
            hi = hi * cos + pltpu.roll(hi * sa, half, 1) + pltpu.roll(hi * sb, DN - half, 1)
        else:
            hi = hi * cos + pltpu.roll(hi, DN - half, 1) * sa + pltpu.roll(hi, half, 1) * sb
        parts += [z[:, hd * DH:hd * DH + DN], hi]
    return parts


def _mla_prep(name, proj, cosp, sap, sbp, gq, gkv, wq, wk, wv, T, tb):
    def body(qkv_ref, cos_ref, sa_ref, sb_ref, gq_ref, gkv_ref, wq_ref, wk_ref, wv_ref,
             q_ref, k_ref, v_ref, cq_ref, ckv_ref):
        qkv = qkv_ref[...].astype(f32)
        qd = qkv[:, :QR]
        kvd = qkv[:, QR:QR + KVR]
        slab = qkv[:, QR + KVR:]
        nq = qd * lax.rsqrt(jnp.mean(qd * qd, axis=-1, keepdims=True) + EPS)
        cq = (nq * gq_ref[...]).astype(bf16)
        cq_scaled = (nq * (gq_ref[...] * QSCALE)).astype(bf16)
        ckv = (kvd * lax.rsqrt(jnp.mean(kvd * kvd, axis=-1, keepdims=True) + EPS) * gkv_ref[...]).astype(bf16)
        cos, sa, sb = cos_ref[:, DN:], sa_ref[:, DN:], sb_ref[:, DN:]
        qpre = jnp.dot(cq_scaled, wq_ref[...], preferred_element_type=f32)
        kpre = jnp.dot(ckv, wk_ref[...], preferred_element_type=f32)
        q_ref[...] = jnp.concatenate(_rope_heads(qpre, cos, sa, sb), axis=1).astype(bf16)
        k_ref[...] = jnp.concatenate(_rope_heads(kpre, cos, sa, sb, shared=slab), axis=1).astype(bf16)
        v_ref[...] = jnp.dot(ckv, wv_ref[...], preferred_element_type=f32).astype(bf16)
        cq_ref[...] = cq
        ckv_ref[...] = ckv

    return _rowwise(name, body, T, tb,
                    [(proj, 512, QKV0 // 512), (cosp, DH, 0), (sap, DH, 0), (sbp, DH, 0)],
                    [gq, gkv, wq, wk, wv],
                    [(NH * DH, bf16), (NH * DH, bf16), (NH * DV, bf16), (QR, bf16), (KVR, bf16)])


def _mla_bwd(name, dq, dk, dv, proj, cosp, sap, sbp, gq, gkv, wq, wk, wv, T, tb):
    def body(dq_ref, dk_ref, dv_ref, qkv_ref, cos_ref, sa_ref, sb_ref, gq_ref, gkv_ref, wq_ref, wk_ref, wv_ref,
             dqkv_ref, dqp_ref, dkp_ref, dgq_ref, dgkv_ref):
        _init_acc(pl.program_id(0), dgq_ref, dgkv_ref)
        cos, sa, sb = cos_ref[:, DN:], sa_ref[:, DN:], sb_ref[:, DN:]
        dqp_b = jnp.concatenate(_rope_heads(dq_ref[...] * SCALE, cos, sa, sb, transposed=True), axis=1).astype(bf16)
        dk_parts = _rope_heads(dk_ref[...], cos, sa, sb, transposed=True)
        dkp_b = jnp.concatenate(dk_parts, axis=1).astype(bf16)
        dqp_ref[...] = dqp_b
        dkp_ref[...] = dkp_b
        nt = (((1,), (1,)), ((), ()))
        dcq = lax.dot_general(dqp_b, wq_ref[...], nt, preferred_element_type=f32)
        dckv = (lax.dot_general(dkp_b, wk_ref[...], nt, preferred_element_type=f32)
                + lax.dot_general(dv_ref[...], wv_ref[...], nt, preferred_element_type=f32))
        dslab = dk_parts[1]
        for hd in range(1, NH):
            dslab = dslab + dk_parts[2 * hd + 1]
        qkv = qkv_ref[...].astype(f32)
        qd = qkv[:, :QR]
        kvd = qkv[:, QR:QR + KVR]
        nq = qd * lax.rsqrt(jnp.mean(qd * qd, axis=-1, keepdims=True) + EPS)
        nkv = kvd * lax.rsqrt(jnp.mean(kvd * kvd, axis=-1, keepdims=True) + EPS)
        dgq_ref[...] += _colsum(dcq * nq)
        dgkv_ref[...] += _colsum(dckv * nkv)
        dnq = dcq * gq_ref[...]
        dnkv = dckv * gkv_ref[...]
        rq = lax.rsqrt(jnp.mean(qd * qd, axis=-1, keepdims=True) + EPS)
        rkv = lax.rsqrt(jnp.mean(kvd * kvd, axis=-1, keepdims=True) + EPS)
        dqd = rq * (dnq - nq * jnp.mean(dnq * nq, axis=-1, keepdims=True))
        dkvd = rkv * (dnkv - nkv * jnp.mean(dnkv * nkv, axis=-1, keepdims=True))
        dqkv_ref[...] = jnp.concatenate([dqd, dkvd, dslab], axis=1).astype(bf16)

    return _rowwise(name, body, T, tb,
                    [(dq, NH * DH, 0), (dk, NH * DH, 0), (dv, NH * DV, 0), (proj, 512, QKV0 // 512),
                     (cosp, DH, 0), (sap, DH, 0), (sbp, DH, 0)],
                    [gq, gkv, wq, wk, wv],
                    [(512, bf16), (NH * DH, bf16), (NH * DH, bf16)],
                    [((1, QR), f32), ((1, KVR), f32)])


def _flash_fwd(name, q, k, v, T, tq, pack=None):
    nq = T // tq
    unroll = 2

    def body(q_ref, k_ref, v_ref, o_ref, lse_ref):
        i = pl.program_id(1)
        nt = (((1,), (1,)), ((), ()))
        qs = [q_ref[:, hh * DH:(hh + 1) * DH] for hh in range(HPS)]

        def one(hh, j, carry, masked):
            m, l, acc = carry
            off = pl.multiple_of(j * tq, tq)
            kj = k_ref[pl.ds(off, tq), hh * DH:(hh + 1) * DH]
            vj = v_ref[pl.ds(off, tq), hh * DV:(hh + 1) * DV]
            s = lax.dot_general(qs[hh], kj, nt, preferred_element_type=f32)
            if masked:
                rr = lax.broadcasted_iota(jnp.int32, (tq, tq), 0)
                cc = lax.broadcasted_iota(jnp.int32, (tq, tq), 1)
                s = jnp.where(cc <= rr, s, NEG)
            m_new = jnp.maximum(m, jnp.max(s, axis=-1, keepdims=True))
            alpha = jnp.exp2(m - m_new)
            p = jnp.exp2(s - m_new)
            l = alpha * l + jnp.sum(p, axis=-1, keepdims=True)
            acc = alpha * acc + jnp.dot(p.astype(bf16), vj, preferred_element_type=f32)
            return m_new, l, acc

        def step(j, carries, masked):
            return tuple(one(hh, j, carries[hh], masked) for hh in range(HPS))

        def pair(jj, carries):
            for u in range(unroll):
                carries = step(jj * unroll + u, carries, False)
            return carries

        init1 = (jnp.full((tq, 1), NEG, f32), jnp.zeros((tq, 1), f32), jnp.zeros((tq, DV), f32))
        carry = lax.fori_loop(0, i // unroll, pair, tuple(init1 for _ in range(HPS)))
        carry = lax.fori_loop((i // unroll) * unroll, i, lambda j, c: step(j, c, False), carry)
        carry = step(i, carry, True)
        for hh in range(HPS):
            m, l, acc = carry[hh]
            o_ref[:, hh * DV:(hh + 1) * DV] = (acc / l).astype(bf16)
            lse_ref[hh] = m + jnp.log2(l)

    def body_with_gather(q_ref, k_ref, v_ref, *rest):
        nw = len(pack)
        w_refs, (o_ref, lse_ref), wall_refs, sems = rest[:nw], rest[nw:nw + 2], rest[nw + 2:2 * nw + 2], rest[2 * nw + 2:]
        h = pl.program_id(0)
        i = pl.program_id(1)

        @pl.when((h == 0) & (i == 0))
        def _():
            mine, first, _ = _gather_copies(w_refs, wall_refs, *sems)
            _gather_start(mine, first)

        @pl.when((h == ngrp // 2) & (i == 0))
        def _():
            _, first, passed = _gather_copies(w_refs, wall_refs, *sems)
            _gather_forward(first, passed)

        body(q_ref, k_ref, v_ref, o_ref, lse_ref)

        @pl.when((h == ngrp - 1) & (i == nq - 1))
        def _():
            _gather_finish(*_gather_copies(w_refs, wall_refs, *sems))

    ngrp = NH // HPS
    in_specs = [pl.BlockSpec((tq, HPS * DH), lambda h, i: (i, h)),
                pl.BlockSpec((T, HPS * DH), lambda h, i: (0, h), pipeline_mode=pl.Buffered(1)),
                pl.BlockSpec((T, HPS * DV), lambda h, i: (0, h), pipeline_mode=pl.Buffered(1))]
    out_specs = [pl.BlockSpec((tq, HPS * DV), lambda h, i: (i, h)), pl.BlockSpec((HPS, tq, 1), lambda h, i: (h, i, 0))]
    out_shape = [jax.ShapeDtypeStruct((T, NH * DV), bf16), jax.ShapeDtypeStruct((NH, T, 1), f32)]
    if pack is None:
        return pl.pallas_call(body, name=name, grid=(ngrp, nq), in_specs=in_specs, out_specs=out_specs,
                              out_shape=out_shape, compiler_params=_cparams(2))(q, k, v)
    return pl.pallas_call(
        body_with_gather, name=name, grid=(ngrp, nq), in_specs=in_specs + [ANY] * len(pack),
        out_specs=out_specs + [ANY] * len(pack),
        out_shape=out_shape + [jax.ShapeDtypeStruct((4,) + a.shape, a.dtype) for a in pack],
        scratch_shapes=_gather_sems(pack), compiler_params=_cparams(2),
    )(q, k, v, *pack)


def _flash_bwd(name, q, k, v, do, lse, dl, T, tq, scatter=None):
    HPS = HPS_BWD
    nq = T // tq

    def body(k_ref, v_ref, q_ref, do_ref, lse_ref, dl_ref, dk_ref, dv_ref, dq_ref):
        j = pl.program_id(1)

        @pl.when(j == 0)
        def _():
            dq_ref[...] = jnp.zeros_like(dq_ref)

        nt = (((1,), (1,)), ((), ()))
        tn = (((0,), (0,)), ((), ()))
        ks = [k_ref[:, hh * DH:(hh + 1) * DH] for hh in range(HPS)]
        vs = [v_ref[:, hh * DV:(hh + 1) * DV] for hh in range(HPS)]

        def one(hh, i, carry, masked):
            dk, dv = carry
            off = pl.multiple_of(i * tq, tq)
            qi = q_ref[pl.ds(off, tq), hh * DH:(hh + 1) * DH]
            doi = do_ref[pl.ds(off, tq), hh * DV:(hh + 1) * DV]
            st = lax.dot_general(ks[hh], qi, nt, preferred_element_type=f32)
            if masked:
                kr = lax.broadcasted_iota(jnp.int32, (tq, tq), 0)
                qc = lax.broadcasted_iota(jnp.int32, (tq, tq), 1)
                st = jnp.where(qc >= kr, st, NEG)
            dpt = lax.dot_general(vs[hh], doi, nt, preferred_element_type=f32)
            pt = jnp.exp2(st - lse_ref[hh, i])
            dst = (pt * (dpt - dl_ref[hh, i])).astype(bf16)
            dv = dv + jnp.dot(pt.astype(bf16), doi, preferred_element_type=f32)
            dk = dk + jnp.dot(dst, qi, preferred_element_type=f32)
            dq_ref[pl.ds(off, tq), hh * DH:(hh + 1) * DH] += lax.dot_general(dst, ks[hh], tn, preferred_element_type=f32)
            return dk, dv

        def step(i, carries, masked):
            return tuple(one(hh, i, carries[hh], masked) for hh in range(HPS))

        zero = tuple((jnp.zeros((tq, DH), f32), jnp.zeros((tq, DV), f32)) for _ in range(HPS))
        carry = step(j, zero, True)
        carry = lax.fori_loop(j + 1, nq, lambda i, c: step(i, c, False), carry)
        for hh in range(HPS):
            dk, dv = carry[hh]
            dk_ref[:, hh * DH:(hh + 1) * DH] = dk * LN2
            dv_ref[:, hh * DV:(hh + 1) * DV] = dv.astype(bf16)

    def body_with_scatter(*refs):
        ns = len(scatter)
        ins, a_refs, outs = refs[:6], refs[6:6 + ns], refs[6 + ns:9 + ns]
        o_refs, sems = refs[9 + ns:9 + 2 * ns], refs[9 + 2 * ns:]
        h = pl.program_id(0)
        j = pl.program_id(1)

        @pl.when((h == 0) & (j == 0))
        def _():
            for cp in _scatter_copies(a_refs, o_refs, *sems):
                cp.start()

        body(*ins, *outs)

        @pl.when((h == ngrp - 1) & (j == nq - 1))
        def _():
            for cp in _scatter_copies(a_refs, o_refs, *sems):
                cp.wait()

    ngrp = NH // HPS
    in_specs = [pl.BlockSpec((tq, HPS * DH), lambda h, j: (j, h)),
                pl.BlockSpec((tq, HPS * DV), lambda h, j: (j, h)),
                pl.BlockSpec((T, HPS * DH), lambda h, j: (0, h), pipeline_mode=pl.Buffered(1)),
                pl.BlockSpec((T, HPS * DV), lambda h, j: (0, h), pipeline_mode=pl.Buffered(1)),
                pl.BlockSpec((HPS, nq, 1, tq), lambda h, j: (h, 0, 0, 0)),
                pl.BlockSpec((HPS, nq, 1, tq), lambda h, j: (h, 0, 0, 0))]
    out_specs = [pl.BlockSpec((tq, HPS * DH), lambda h, j: (j, h)),
                 pl.BlockSpec((tq, HPS * DV), lambda h, j: (j, h)),
                 pl.BlockSpec((T, HPS * DH), lambda h, j: (0, h), pipeline_mode=pl.Buffered(1))]
    out_shape = [jax.ShapeDtypeStruct((T, NH * DH), f32), jax.ShapeDtypeStruct((T, NH * DV), bf16),
                 jax.ShapeDtypeStruct((T, NH * DH), f32)]
    if scatter is None:
        return pl.pallas_call(body, name=name, grid=(ngrp, nq), in_specs=in_specs, out_specs=out_specs,
                              out_shape=out_shape, compiler_params=_cparams(2))(k, v, q, do, lse, dl)
    ns = len(scatter)
    return pl.pallas_call(
        body_with_scatter, name=name, grid=(ngrp, nq), in_specs=in_specs + [ANY] * ns,
        out_specs=out_specs + [ANY] * ns, out_shape=out_shape + _scatter_shapes(scatter),
        scratch_shapes=_scatter_sems(scatter), compiler_params=_cparams(2),
    )(k, v, q, do, lse, dl, *scatter)


def _gate_mix(name, proj, ya, o, T, tb):
    def body(ga_ref, gb_ref, ya_ref, o_ref, y_ref):
        y_ref[...] = (jax.nn.sigmoid(ga_ref[...].astype(f32)) * ya_ref[...]
                      + jax.nn.sigmoid(gb_ref[...].astype(f32)) * o_ref[...].astype(f32)).astype(bf16)

    return _rowwise(name, body, T, tb, [(proj, D, 1), (proj, D, 2), (ya, D, 0), (o, D, 0)], [], [(D, bf16)])[0]


def _gate_mix_bwd(name, dy, proj, ya, o, T, tb):
    def body(dy_ref, ga_ref, gb_ref, ya_ref, o_ref, dga_ref, dgb_ref, dya_ref, do_ref, dl_ref):
        dy = dy_ref[...]
        sa = jax.nn.sigmoid(ga_ref[...].astype(f32))
        sb = jax.nn.sigmoid(gb_ref[...].astype(f32))
        ov = o_ref[...].astype(f32)
        dga_ref[...] = (dy * ya_ref[...] * sa * (1.0 - sa)).astype(bf16)
        dgb_ref[...] = (dy * ov * sb * (1.0 - sb)).astype(bf16)
        dya_ref[...] = dy * sa
        do = dy * sb
        do_ref[...] = do.astype(bf16)
        prod = do * ov
        for hd in range(NH):
            dl_ref[:, hd:hd + 1] = jnp.sum(prod[:, hd * DV:(hd + 1) * DV], axis=-1, keepdims=True)

    return _rowwise(name, body, T, tb, [(dy, D, 0), (proj, D, 1), (proj, D, 2), (ya, D, 0), (o, D, 0)], [],
                    [(D, bf16), (D, bf16), (D, f32), (D, bf16), (NH, f32)])


def _ffn_in_swiglu(name, h, w, T):
    tm = _pick(T, (512, 256, 128))

    def body(h_ref, w_ref, z_ref, a_ref):
        z = jnp.dot(h_ref[...], w_ref[...], preferred_element_type=f32)
        g = z[:, :FH]
        z_ref[...] = z.astype(bf16)
        a_ref[...] = (g * jax.nn.sigmoid(g) * z[:, FH:]).astype(bf16)

    return pl.pallas_call(
        body, name=name, grid=(2, T // tm),
        in_specs=[pl.BlockSpec((tm, D), lambda j, i: (i, 0)), pl.BlockSpec((D, 2 * FH), lambda j, i: (0, j))],
        out_specs=[pl.BlockSpec((tm, 2 * FH), lambda j, i: (i, j)), pl.BlockSpec((tm, FH), lambda j, i: (i, j))],
        out_shape=[jax.ShapeDtypeStruct((T, 2 * DFF), bf16), jax.ShapeDtypeStruct((T, DFF), bf16)],
        compiler_params=_cparams(2),
    )(h, w)


def _dact_swiglu_bwd(name, dfo, w, z, T):
    tm = _pick(T, (512, 256, 128))

    def body(d_ref, w_ref, z_ref, o_ref):
        da = lax.dot_general(d_ref[...], w_ref[...], (((1,), (1,)), ((), ())), preferred_element_type=f32)
        g = z_ref[:, :FH].astype(f32)
        u = z_ref[:, FH:].astype(f32)
        sg = jax.nn.sigmoid(g)
        o_ref[:, :FH] = (da * u * sg * (1.0 + g * (1.0 - sg))).astype(bf16)
        o_ref[:, FH:] = (da * g * sg).astype(bf16)

    return pl.pallas_call(
        body, name=name, grid=(2, T // tm),
        in_specs=[pl.BlockSpec((tm, D), lambda j, i: (i, 0)), pl.BlockSpec((FH, D), lambda j, i: (j, 0)),
                  pl.BlockSpec((tm, 2 * FH), lambda j, i: (i, j))],
        out_specs=pl.BlockSpec((tm, 2 * FH), lambda j, i: (i, j)),
        out_shape=jax.ShapeDtypeStruct((T, 2 * DFF), bf16), compiler_params=_cparams(2),
    )(dfo, w, z)


def _final(name, x, g, m, gfin, tgt, T, tb):
    def body(x_ref, m_ref, t_ref, g_ref, gf_ref, dx_ref, dm_ref, loss_ref, dgf_ref, dg_ref):
        _init_acc(pl.program_id(0), loss_ref, dgf_ref, dg_ref)
        mv = m_ref[...]
        xv = x_ref[...] + g_ref[...] * mv
        rstd = lax.rsqrt(jnp.mean(xv * xv, axis=-1, keepdims=True) + EPS)
        nv = xv * rstd
        err = nv * gf_ref[...] - t_ref[...]
        loss_ref[...] += 0.5 * jnp.sum(jnp.mean(err * err, axis=-1, keepdims=True), axis=0, keepdims=True)
        dy = err * (1.0 / D)
        dgf_ref[...] += _colsum(dy * nv)
        dn = dy * gf_ref[...]
        dxv = rstd * (dn - nv * jnp.mean(dn * nv, axis=-1, keepdims=True))
        dx_ref[...] = dxv
        dm_ref[...] = (dxv * g_ref[...]).astype(bf16)
        dg_ref[...] += _colsum(dxv * mv)

    return _rowwise(name, body, T, tb, [(x, D, 0), (m, D, 0), (tgt, D, 0)], [g, gfin], [(D, f32), (D, bf16)],
                    [((1, 1), f32), ((1, D), f32), ((1, D), f32)])


def _adamw(name, w, g, m, v):
    R, C = w.shape
    tb = _pick(R, (256, 128, 64, 32, 16, 8))
    c1 = 1.0 - ADAM_B1 ** ADAM_STEP
    c2 = 1.0 - ADAM_B2 ** ADAM_STEP

    def body(w_ref, g_ref, m_ref, v_ref, d_ref, m2_ref, v2_ref):
        gv = g_ref[...]
        m2 = ADAM_B1 * m_ref[...] + (1.0 - ADAM_B1) * gv
        v2 = ADAM_B2 * v_ref[...] + (1.0 - ADAM_B2) * (gv * gv)
        m2_ref[...] = m2
        v2_ref[...] = v2
        d_ref[...] = -ADAM_LR * ((m2 / c1) / (jnp.sqrt(v2 / c2) + ADAM_EPS) + ADAM_WD * w_ref[...])

    return _rowwise(name, body, R, tb, [(w, C, 0), (g, C, 0), (m, C, 0), (v, C, 0)], [], [(C, f32)] * 3)


def _ada_fwd(name, c_all, w_ada, b_sh):
    nl, _, ns = w_ada.shape

    def body(c_ref, w_ref, b_ref, o_ref):
        o_ref[0] = jnp.dot(c_ref[...].astype(bf16), w_ref[0].astype(bf16), preferred_element_type=f32) + b_ref[0]

    return pl.pallas_call(
        body, name=name, grid=(nl,),
        in_specs=[pl.BlockSpec((8, D), lambda l: (0, 0)), pl.BlockSpec((1, D, ns), lambda l: (l, 0, 0)),
                  pl.BlockSpec((1, 1, ns), lambda l: (l, 0, 0))],
        out_specs=pl.BlockSpec((1, 8, ns), lambda l: (l, 0, 0)),
        out_shape=jax.ShapeDtypeStruct((nl, 8, ns), f32), compiler_params=_cparams(1),
    )(c_all, w_ada, b_sh)


def _ada_bwd(name, c_t, dm_sh):
    nl, _, ns = dm_sh.shape

    def body(c_ref, dm_ref, o_ref):
        acc = c_ref[:, 0:1] * dm_ref[0, 0:1, :]
        for d in range(1, 8):
            acc = acc + c_ref[:, d:d + 1] * dm_ref[0, d:d + 1, :]
        o_ref[0] = acc

    return pl.pallas_call(
        body, name=name, grid=(nl,),
        in_specs=[pl.BlockSpec((D, 8), lambda l: (0, 0)), pl.BlockSpec((1, 8, ns), lambda l: (l, 0, 0))],
        out_specs=pl.BlockSpec((1, D, ns), lambda l: (l, 0, 0)),
        out_shape=jax.ShapeDtypeStruct((nl, D, ns), f32), compiler_params=_cparams(1),
    )(c_t, dm_sh)


def _sum_slots(name, a):
    n, R, C = a.shape
    tb = _pick(R, (256, 128, 64, 32, 16, 8))

    def body(a_ref, o_ref):
        acc = a_ref[0].astype(f32)
        for s in range(1, n):
            acc = acc + a_ref[s].astype(f32)
        o_ref[...] = acc

    return pl.pallas_call(
        body, name=name, grid=(R // tb,), in_specs=[pl.BlockSpec((n, tb, C), lambda i: (0, i, 0))],
        out_specs=pl.BlockSpec((tb, C), lambda i: (i, 0)), out_shape=jax.ShapeDtypeStruct((R, C), f32),
        compiler_params=_cparams(1),
    )(a)


def _chunks(rows, dtype, want):
    unit = 16 if dtype == bf16 else 8
    for n in range(want, 0, -1):
        if rows % n == 0 and (rows // n) % unit == 0:
            return n, rows // n
    return 1, rows


def _my_chip():
    return 2 * lax.axis_index("x") + lax.axis_index("y")


def _pair_sum(name, arrs, gots):
    na = len(arrs)

    def body(*refs):
        for k in range(na):
            a_ref, g_ref, o_ref = refs[k], refs[na + k], refs[2 * na + k]
            o_ref[...] = (a_ref[...].astype(f32) + g_ref[...].astype(f32)).astype(o_ref.dtype)

    in_specs = [pl.BlockSpec((1, a.shape[1] // 4, a.shape[2]), lambda s, i: (s, 2 * lax.axis_index("c") + i, 0))
                for a in arrs]
    half_specs = [pl.BlockSpec((1, g.shape[1] // 2, g.shape[2]), lambda s, i: (s, i, 0)) for g in gots]
    return pl.pallas_call(
        body, name=name, grid=(4, 2), in_specs=in_specs + half_specs, out_specs=half_specs,
        out_shape=[jax.ShapeDtypeStruct(g.shape, a.dtype) for a, g in zip(arrs, gots)], compiler_params=_cparams(2),
    )(*arrs, *gots)


def _chip_sum(name, sums, lands, layer, bufs=None):
    na = len(sums)

    def body(*refs):
        for k in range(na):
            s_ref, l_ref, o_ref = refs[k], refs[na + k], refs[len(refs) - na + k]
            acc = s_ref[0].astype(f32)
            for j in range(l_ref.shape[0]):
                acc = acc + l_ref[j].astype(f32)
            o_ref[...] = acc.reshape(o_ref.shape)

    in_specs = [pl.BlockSpec((1, s.shape[1] // 2, s.shape[2]), lambda i: (_my_chip(), i, 0)) for s in sums]
    in_specs += [pl.BlockSpec((3, l.shape[1] // 2, l.shape[2]), lambda i: (0, i, 0)) for l in lands]
    if layer is None:
        out_specs = [pl.BlockSpec((s.shape[1] // 2, s.shape[2]), lambda i: (i, 0)) for s in sums]
        out_shape = [jax.ShapeDtypeStruct(s.shape[1:], f32) for s in sums]
    else:
        out_specs = [pl.BlockSpec((1, 1, s.shape[1] // 2, s.shape[2]), lambda i: (layer, lax.axis_index("c"), i, 0))
                     for s in sums]
        out_shape = [jax.ShapeDtypeStruct((2, 2) + s.shape[1:], f32) for s in sums]
    aliases = {}
    operands = list(sums) + list(lands)
    if bufs is not None:
        in_specs += [ANY] * na
        aliases = {2 * na + k: k for k in range(na)}
        operands += list(bufs)
    return pl.pallas_call(
        body, name=name, grid=(2,), in_specs=in_specs, out_specs=out_specs, out_shape=out_shape,
        input_output_aliases=aliases, compiler_params=_cparams(1),
    )(*operands)


def _coords():
    return lax.axis_index("x"), lax.axis_index("y"), lax.axis_index("c")


def _flip(v, m):
    return (1 - v) if m else v


def _peer(x, y, c, mask):
    return (_flip(x, mask[0]), _flip(y, mask[1]), _flip(c, mask[2]))


def _all_gather(name, arr, masks, whole_mesh):
    nslots = 8 if whole_mesh else 4

    def slot(x, y, c):
        return 4 * x + 2 * y + c if whole_mesh else 2 * x + y

    def body(a_ref, o_ref, send_sems, recv_sems, local_sem):
        x, y, c = _coords()
        mine = pltpu.make_async_copy(a_ref, o_ref.at[slot(x, y, c)], local_sem)
        mine.start()
        copies = []
        for k, mask in enumerate(masks):
            cp = pltpu.make_async_remote_copy(
                src_ref=a_ref, dst_ref=o_ref.at[slot(x, y, c)], send_sem=send_sems.at[k], recv_sem=recv_sems.at[k],
                device_id=_peer(x, y, c, mask), device_id_type=MESH)
            cp.start()
            copies.append(cp)
        for cp in copies:
            cp.wait()
        mine.wait()

    return pl.pallas_call(
        body, name=name, in_specs=[ANY], out_specs=ANY,
        out_shape=jax.ShapeDtypeStruct((nslots,) + arr.shape, arr.dtype),
        scratch_shapes=[pltpu.SemaphoreType.DMA((len(masks),)), pltpu.SemaphoreType.DMA((len(masks),)),
                        pltpu.SemaphoreType.DMA],
    )(arr)


def _gather_weights(name, arrs):
    na = len(arrs)

    def body(*refs):
        mine, first, passed = _gather_copies(refs[:na], refs[na:2 * na], *refs[2 * na:])
        _gather_start(mine, first)
        _gather_forward(first, passed)
        _gather_finish(mine, first, passed)

    return pl.pallas_call(
        body, name=name, in_specs=[ANY] * na, out_specs=[ANY] * na,
        out_shape=[jax.ShapeDtypeStruct((4,) + a.shape, a.dtype) for a in arrs], scratch_shapes=_gather_sems(arrs),
    )(*arrs)


def _gather_plan(shape, dtype):
    big = (shape[0] // 2) * shape[1] * 2 >= (2 << 20)
    return _chunks(shape[0] // 2, dtype, 3 if big else 1), _chunks(shape[0], dtype, 4 if big else 1)


def _gather_sems(arrs):
    plans = [_gather_plan(a.shape, a.dtype) for a in arrs]
    nremote = sum(len(CHIP_MASKS) * p[0][0] for p in plans)
    nlocal = sum(p[1][0] for p in plans)
    return [pltpu.SemaphoreType.DMA((nremote,))] * 4 + [pltpu.SemaphoreType.DMA((nlocal,))]


def _gather_copies(w_refs, o_refs, sa, ra, sb, rb, local_sems):
    x, y, c = _coords()
    chip = 2 * x + y
    mine, first, passed = [], [], []
    for w_ref, o_ref in zip(w_refs, o_refs):
        (nch, rows), (nloc, lrows) = _gather_plan(w_ref.shape, w_ref.dtype)
        rh = w_ref.shape[0] // 2
        for k in range(nloc):
            part = pl.ds(k * lrows, lrows)
            mine.append(pltpu.make_async_copy(w_ref.at[part], o_ref.at[chip, part], local_sems.at[len(mine)]))
        for mask in CHIP_MASKS:
            px, py, _ = _peer(x, y, c, mask)
            for ch in range(nch):
                k = len(first)
                part = pl.ds(c * rh + ch * rows, rows)
                first.append(pltpu.make_async_remote_copy(
                    src_ref=w_ref.at[part], dst_ref=o_ref.at[chip, part], send_sem=sa.at[k], recv_sem=ra.at[k],
                    device_id=(px, py, c), device_id_type=MESH))
                landed = o_ref.at[2 * px + py, part]
                passed.append(pltpu.make_async_remote_copy(
                    src_ref=landed, dst_ref=landed, send_sem=sb.at[k], recv_sem=rb.at[k],
                    device_id=(x, y, 1 - c), device_id_type=MESH))
    return mine, first, passed


def _gather_start(mine, first):
    for cp in mine + first:
        cp.start()


def _gather_forward(first, passed):
    for a, b in zip(first, passed):
        a.wait_recv()
        b.start()


def _gather_finish(mine, first, passed):
    for a, b in zip(first, passed):
        a.wait_send()
        b.wait()
    for cp in mine:
        cp.wait()


def _swap_halves(name, arrs):
    na = len(arrs)
    plan = []
    for a in arrs:
        h = a.shape[1] // 2
        nch, rows = _chunks(h, a.dtype, 2 if h * a.shape[2] * 2 >= (1 << 20) else 1)
        plan.append([(s, ch * rows, rows) for s in range(a.shape[0]) for ch in range(nch)])
    ncopies = sum(len(p) for p in plan)

    def body(*refs):
        a_refs, got_refs = refs[:na], refs[na:2 * na]
        send_sems, recv_sems = refs[2 * na:]
        x, y, c = _coords()
        cps = []
        for ai in range(na):
            h = arrs[ai].shape[1] // 2
            for (s, r0, rows) in plan[ai]:
                k = len(cps)
                cp = pltpu.make_async_remote_copy(
                    src_ref=a_refs[ai].at[s, pl.ds((1 - c) * h + r0, rows)], dst_ref=got_refs[ai].at[s, pl.ds(r0, rows)],
                    send_sem=send_sems.at[k], recv_sem=recv_sems.at[k], device_id=(x, y, 1 - c), device_id_type=MESH)
                cp.start()
                cps.append(cp)
        for cp in cps:
            cp.wait()

    halves = [jax.ShapeDtypeStruct((a.shape[0], a.shape[1] // 2, a.shape[2]), a.dtype) for a in arrs]
    return pl.pallas_call(
        body, name=name, in_specs=[ANY] * na, out_specs=[ANY] * na, out_shape=halves,
        scratch_shapes=[pltpu.SemaphoreType.DMA((ncopies,))] * 2,
    )(*arrs)


def _scatter_chips(name, arrs):
    na = len(arrs)

    def body(*refs):
        cps = _scatter_copies(refs[:na], refs[na:2 * na], *refs[2 * na:])
        for cp in cps:
            cp.start()
        for cp in cps:
            cp.wait()

    return pl.pallas_call(
        body, name=name, in_specs=[ANY] * na, out_specs=[ANY] * na,
        out_shape=_scatter_shapes(arrs), scratch_shapes=_scatter_sems(arrs),
    )(*arrs)


def _scatter_shapes(arrs):
    return [jax.ShapeDtypeStruct((len(CHIP_MASKS),) + a.shape[1:], a.dtype) for a in arrs]


def _scatter_sems(arrs):
    return [pltpu.SemaphoreType.DMA((len(arrs), len(CHIP_MASKS)))] * 2


def _scatter_copies(a_refs, o_refs, send_sems, recv_sems):
    x, y, c = _coords()
    cps = []
    for k in range(len(a_refs)):
        for j, mask in enumerate(CHIP_MASKS):
            px, py, pc = _peer(x, y, c, mask)
            cps.append(pltpu.make_async_remote_copy(
                src_ref=a_refs[k].at[2 * px + py], dst_ref=o_refs[k].at[j],
                send_sem=send_sems.at[k, j], recv_sem=recv_sems.at[k, j],
                device_id=(px, py, pc), device_id_type=MESH))
    return cps


def _join_halves(name, bufs, small):
    na = len(bufs)
    nm = len(ALL_MASKS)

    def body(*refs):
        small_ref, o_refs, all_ref = refs[na], refs[na + 1:2 * na + 1], refs[2 * na + 1]
        send_sems, recv_sems, gs_sems, gr_sems, local_sem = refs[2 * na + 2:]
        x, y, c = _coords()
        cps = []
        for k in range(na):
            for l in range(2):
                part = o_refs[k].at[l, c]
                cps.append(pltpu.make_async_remote_copy(
                    src_ref=part, dst_ref=part, send_sem=send_sems.at[k, l], recv_sem=recv_sems.at[k, l],
                    device_id=(x, y, 1 - c), device_id_type=MESH))
        slot = all_ref.at[4 * x + 2 * y + c]
        for k, mask in enumerate(ALL_MASKS):
            cps.append(pltpu.make_async_remote_copy(
                src_ref=small_ref, dst_ref=slot, send_sem=gs_sems.at[k], recv_sem=gr_sems.at[k],
                device_id=_peer(x, y, c, mask), device_id_type=MESH))
        mine = pltpu.make_async_copy(small_ref, slot, local_sem)
        mine.start()
        for cp in cps:
            cp.start()
        for cp in cps:
            cp.wait()
        mine.wait()

    return pl.pallas_call(
        body, name=name, in_specs=[ANY] * (na + 1), out_specs=[ANY] * (na + 1),
        out_shape=[jax.ShapeDtypeStruct(b.shape, b.dtype) for b in bufs]
        + [jax.ShapeDtypeStruct((8,) + small.shape, small.dtype)],
        input_output_aliases={k: k for k in range(na)},
        scratch_shapes=[pltpu.SemaphoreType.DMA((na, 2))] * 2 + [pltpu.SemaphoreType.DMA((nm,))] * 2
        + [pltpu.SemaphoreType.DMA],
    )(*bufs, small)


BIG = ("w_in", "w_uq", "w_ukv", "w_out", "w_ffn_in", "w_ffn_out")
SMALL = ("conv_w", "conv_b", "lru_wa", "lru_ba", "lru_wx", "lru_bx", "lru_a_param", "q_norm_g", "kv_norm_g",
         "final_norm_g")
SMALL_SHAPES = {"conv_w": (2, CONV, D), "conv_b": (2, D), "lru_wa": (2, NB, BW, BW), "lru_ba": (2, NB, BW),
                "lru_wx": (2, NB, BW, BW), "lru_bx": (2, NB, BW), "lru_a_param": (2, D), "q_norm_g": (2, QR),
                "kv_norm_g": (2, KVR), "final_norm_g": (D,)}
SMALL_ROWS = 640
FFN_CHIP_ORDER = (0, 2, 1, 3)


def _size(shape):
    n = 1
    for s in shape:
        n *= s
    return n


def _pack_rows(parts, rows, dtype):
    flat = jnp.concatenate([p.reshape(-1).astype(dtype) for p in parts])
    pad = rows * 1024 - flat.shape[0]
    if pad:
        flat = jnp.concatenate([flat, jnp.zeros((pad,), dtype)])
    return flat.reshape(rows, 1024)


def _unpack_rows(pack, shapes):
    flat = pack.reshape(-1)
    out, off = [], 0
    for s in shapes:
        n = _size(s)
        out.append(flat[off:off + n].reshape(s))
        off += n
    return out


def _inproj_weights(g_in, g_uq, g_ukv):
    w_in = jnp.concatenate([g_in[s] for s in range(4)], axis=1)
    win_p = jnp.concatenate([w_in[:, :D], w_in[:, 1472:2496], w_in[:, 2496:], w_in[:, D:1472],
                             jnp.zeros((D, 64), w_in.dtype)], axis=1)
    w_uq = g_uq.reshape(QR, NH, DN + DR)
    wq_p = jnp.concatenate([w_uq, jnp.zeros((QR, NH, DH - DN - DR), w_uq.dtype)], axis=2).reshape(QR, NH * DH)
    w_ukv = jnp.transpose(g_ukv.reshape(4, KVR, NH, (DN + DV) // 4), (1, 2, 0, 3)).reshape(KVR, NH, DN + DV)
    wk_p = jnp.concatenate([w_ukv[..., :DN], jnp.zeros((KVR, NH, DH - DN), w_ukv.dtype)], axis=2).reshape(KVR, NH * DH)
    wv = w_ukv[..., DN:].reshape(KVR, NH * DV)
    return dict(win=win_p, wq=wq_p, wk=wk_p, wv=wv)


def _other_weights(g_out, g_fi, g_fo):
    ko, ki, kf = g_out.shape[1] // 2, g_fi.shape[1] // 2, g_fo.shape[1] // 2
    wout = [g_out[:, l * ko:(l + 1) * ko].reshape(4 * ko, D) for l in range(2)]
    wfi = [jnp.concatenate([g_fi[s, l * ki:(l + 1) * ki] for s in FFN_CHIP_ORDER], axis=1) for l in range(2)]
    wfo = [g_fo[:, l * kf:(l + 1) * kf].reshape(4 * kf, D) for l in range(2)]
    return dict(wout=wout, wfi=wfi, wfo=wfo)


def _shard_grads_inproj(dwin, dwq, dwk, dwv):
    w_in = jnp.concatenate([dwin[:, :D], dwin[:, QKV0:QKV0 + 448], dwin[:, D:QKV0]], axis=1)
    w_in = jnp.transpose(w_in.reshape(D, 4, DIN // 4), (1, 0, 2))
    w_uq = dwq.reshape(QR, NH, DH)[..., :DN + DR].reshape(4, QR // 4, NH * (DN + DR))
    w_ukv = jnp.concatenate([dwk.reshape(KVR, NH, DH)[..., :DN], dwv.reshape(KVR, NH, DV)], axis=2)
    w_ukv = jnp.transpose(w_ukv.reshape(KVR, NH, 4, (DN + DV) // 4), (2, 0, 1, 3)).reshape(4, KVR, NH * (DN + DV) // 4)
    return [w_in, w_uq, w_ukv]


def _shard_grads_rest(dwout, dwfi, dwfo):
    tiles = jnp.split(dwfi, 4, axis=1)
    wfi = jnp.stack([tiles[FFN_CHIP_ORDER.index(s)] for s in range(4)])
    return [dwout.reshape(4, D // 4, D), wfi, dwfo.reshape(4, DFF // 4, D)]


def kernel(x, c, positions, w_ada, b_ada, w_in, conv_w, conv_b, lru_wa, lru_ba, lru_wx, lru_bx, lru_a_param, q_norm_g, kv_norm_g, w_uq, w_ukv, w_out, w_ffn_in, w_ffn_out, final_norm_g, loss_target, m_w_ada, m_b_ada, m_w_in, m_conv_w, m_conv_b, m_lru_wa, m_lru_ba, m_lru_wx, m_lru_bx, m_lru_a_param, m_q_norm_g, m_kv_norm_g, m_w_uq, m_w_ukv, m_w_out, m_w_ffn_in, m_w_ffn_out, m_final_norm_g, v_w_ada, v_b_ada, v_w_in, v_conv_w, v_conv_b, v_lru_wa, v_lru_ba, v_lru_wx, v_lru_bx, v_lru_a_param, v_q_norm_g, v_kv_norm_g, v_w_uq, v_w_ukv, v_w_out, v_w_ffn_in, v_w_ffn_out, v_final_norm_g):
    T = x.shape[1]
    tb = min(ROW_BLOCK, T)
    tq = min(ATTN_BLOCK, T)
    nq = T // tq
    mx, my, mc = _coords()
    chip = 2 * mx + my
    dev = 4 * mx + 2 * my + mc
    weights = dict(w_ada=w_ada, b_ada=b_ada, w_in=w_in, conv_w=conv_w, conv_b=conv_b, lru_wa=lru_wa, lru_ba=lru_ba,
                   lru_wx=lru_wx, lru_bx=lru_bx, lru_a_param=lru_a_param, q_norm_g=q_norm_g, kv_norm_g=kv_norm_g,
                   w_uq=w_uq, w_ukv=w_ukv, w_out=w_out, w_ffn_in=w_ffn_in, w_ffn_out=w_ffn_out,
                   final_norm_g=final_norm_g)
    mom = dict(w_ada=m_w_ada, b_ada=m_b_ada, w_in=m_w_in, conv_w=m_conv_w, conv_b=m_conv_b, lru_wa=m_lru_wa,
               lru_ba=m_lru_ba, lru_wx=m_lru_wx, lru_bx=m_lru_bx, lru_a_param=m_lru_a_param, q_norm_g=m_q_norm_g,
               kv_norm_g=m_kv_norm_g, w_uq=m_w_uq, w_ukv=m_w_ukv, w_out=m_w_out, w_ffn_in=m_w_ffn_in,
               w_ffn_out=m_w_ffn_out, final_norm_g=m_final_norm_g)
    var = dict(w_ada=v_w_ada, b_ada=v_b_ada, w_in=v_w_in, conv_w=v_conv_w, conv_b=v_conv_b, lru_wa=v_lru_wa,
               lru_ba=v_lru_ba, lru_wx=v_lru_wx, lru_bx=v_lru_bx, lru_a_param=v_lru_a_param, q_norm_g=v_q_norm_g,
               kv_norm_g=v_kv_norm_g, w_uq=v_w_uq, w_ukv=v_w_ukv, w_out=v_w_out, w_ffn_in=v_w_ffn_in,
               w_ffn_out=v_w_ffn_out, final_norm_g=v_final_norm_g)
    order = list(weights)

    ns_ada = w_ada.shape[2]
    c_all = _all_gather("gather_c", c, ALL_MASKS, True).reshape(8, D)
    b_sh = lax.dynamic_slice_in_dim(b_ada, chip * ns_ada, ns_ada, axis=1).reshape(2, 1, ns_ada)
    mod_sh = _ada_fwd("ada_fwd", c_all, w_ada, b_sh)
    mod_all = _all_gather("gather_mod", mod_sh.reshape(16, ns_ada), CHIP_MASKS, False)
    mod_mine = lax.dynamic_index_in_dim(mod_all.reshape(4, 2, 8, ns_ada), dev, axis=2, keepdims=False)
    mod = jnp.transpose(mod_mine, (1, 0, 2)).reshape(2, 6, 1, D)

    flat = lambda w: w.astype(bf16).reshape(w.shape[0], w.shape[1], -1)
    b_in, b_uq, b_ukv = flat(w_in), flat(w_uq), flat(w_ukv)
    late = [b_in[1], b_uq[1], b_ukv[1]] + [flat(w).reshape(-1, w.shape[-1]) for w in (w_out, w_ffn_in, w_ffn_out)]
    W0 = _inproj_weights(*_gather_weights("gather_weights_early", [b_in[0], b_uq[0], b_ukv[0]]))
    W = {n: [W0[n], None] for n in W0}
    wa_b = lru_wa.astype(bf16)
    wx_b = lru_wx.astype(bf16)

    pos = positions.reshape(T, 1)
    invf = ROPE_THETA ** (-jnp.arange(0, DR, 2, dtype=f32) / DR)
    invf_p = jnp.concatenate([jnp.zeros((DN,), f32), invf, invf, jnp.zeros((DH - DN - DR,), f32)]).reshape(1, DH)
    cosp, sap, sbp = _rope_tables("rope_tables", pos, invf_p, T, tb)

    ns_cw = conv_w.shape[2]
    cw_all = _all_gather("gather_conv_w", conv_w.reshape(2 * CONV, ns_cw), CHIP_MASKS, False)
    cw_full = jnp.transpose(cw_all, (1, 0, 2)).reshape(2, CONV, D)
    vec = lambda a, l: a[l].reshape(1, -1)

    xl = x.reshape(T, D)
    saved = []
    h1 = _norm_mod("norm_mod_0", xl, mod[0, 1], mod[0, 0], T, tb)
    for l in range(2):
        sh_m, sc_m, g_m, sh_f, sc_f, g_f = [mod[l, j] for j in range(6)]
        proj = _matmul(f"proj_{l}", h1, W["win"][l], "nn", bf16)
        u, r, ig, ya = _lru_fwd(f"lru_fwd_{l}", proj, pos, cw_full[l], vec(conv_b, l), wa_b[l], vec(lru_ba, l),
                                wx_b[l], vec(lru_bx, l), vec(lru_a_param, l), T, tb)
        qp, kp, vp, cq, ckv = _mla_prep(f"mla_prep_{l}", proj, cosp, sap, sbp, vec(q_norm_g, l), vec(kv_norm_g, l),
                                        W["wq"][l], W["wk"][l], W["wv"][l], T, tb)
        if l == 0:
            o, lse, *late_all = _flash_fwd("flash_fwd_0", qp, kp, vp, T, tq, late)
            W1 = _inproj_weights(*late_all[:3])
            for n in W1:
                W[n][1] = W1[n]
            W.update(_other_weights(*late_all[3:]))
        else:
            o, lse = _flash_fwd("flash_fwd_1", qp, kp, vp, T, tq)
        y = _gate_mix(f"gate_mix_{l}", proj, ya, o, T, tb)
        mo = _matmul(f"out_proj_{l}", y, W["wout"][l], "nn")
        x2, h2 = _resid_norm_mod(f"resid_norm_f_{l}", xl, g_m, mo, sc_f, sh_f, T, tb)
        gu, act = _ffn_in_swiglu(f"ffn_in_{l}", h2, W["wfi"][l], T)
        fo = _matmul(f"ffn_out_{l}", act, W["wfo"][l], "nn")
        saved.append(dict(x=xl, h1=h1, proj=proj, u=u, r=r, ig=ig, ya=ya, qp=qp, kp=kp, vp=vp, cq=cq, ckv=ckv, o=o,
                          lse=lse, y=y, mo=mo, x2=x2, h2=h2, gu=gu, act=act, fo=fo))
        if l == 0:
            xl, h1 = _resid_norm_mod("resid_norm_m_1", x2, g_f, fo, mod[1, 1], mod[1, 0], T, tb)

    dx, dfo, loss_part, dgfin, dg_f = _final("final", saved[1]["x2"], mod[1, 5], saved[1]["fo"],
                                             final_norm_g.reshape(1, D), loss_target.reshape(T, D), T, tb)
    loss = lax.psum(loss_part[0, 0], ("x", "y", "c"))

    gl = {n: [None, None] for n in ("win", "wq", "wk", "wv", "wout", "wfi", "wfo", "conv_w", "conv_b", "lru_wa",
                                    "lru_ba", "lru_wx", "lru_bx", "lru_a_param", "q_norm_g", "kv_norm_g")}
    dmod = [None, None]

    def pair_sums(tag, arrs):
        return list(_pair_sum(f"rs_pair_sum_{tag}", arrs, _swap_halves(f"rs_swap_halves_{tag}", arrs)))

    def inproj_pieces(l):
        return _shard_grads_inproj(gl["win"][l], gl["wq"][l], gl["wk"][l], gl["wv"][l])

    def rest_pieces(l):
        return _shard_grads_rest(gl["wout"][l], gl["wfi"][l], gl["wfo"][l])

    for l in (1, 0):
        s = saved[l]
        sh_m, sc_m, g_m, sh_f, sc_f, g_f = [mod[l, j] for j in range(6)]
        dgu = _dact_swiglu_bwd(f"d_act_{l}", dfo, W["wfo"][l], s["gu"], T)
        gl["wfo"][l] = _matmul(f"dw_ffn_out_{l}", s["act"], dfo, "tn", bf16)
        dh2 = _matmul(f"d_h2_{l}", dgu, W["wfi"][l], "nt")
        gl["wfi"][l] = _matmul(f"dw_ffn_in_{l}", s["h2"], dgu, "tn", bf16)
        dx2, dmo, dsc_f, dsh_f, dg_m = _norm_mod_bwd(f"norm_bwd_f_{l}", dh2, s["x2"], sc_f, dx, T, tb, (g_m, s["mo"]))
        dy = _matmul(f"d_y_{l}", dmo, W["wout"][l], "nt")
        gl["wout"][l] = _matmul(f"dw_out_{l}", s["y"], dmo, "tn", bf16)
        dga, dgb, dya, do, dlt = _gate_mix_bwd(f"gate_mix_bwd_{l}", dy, s["proj"], s["ya"], s["o"], T, tb)
        lse_r = s["lse"].reshape(NH, nq, 1, tq)
        dl_r = jnp.transpose(dlt).reshape(NH, nq, 1, tq)
        if l == 0:
            sums0a = pair_sums("0a", rest_pieces(0))
            dk, dv, dq, *lands = _flash_bwd("flash_bwd_0", s["qp"], s["kp"], s["vp"], do, lse_r, dl_r, T, tq,
                                            sums1 + sums0a)
            rs_bufs = list(_chip_sum("rs_chip_sum_1", sums1, lands[:6], 1))
            rs_bufs[3:] = _chip_sum("rs_chip_sum_0a", sums0a, lands[6:], 0, rs_bufs[3:])
        else:
            dk, dv, dq = _flash_bwd("flash_bwd_1", s["qp"], s["kp"], s["vp"], do, lse_r, dl_r, T, tq)
        dqkv, dqp, dkp, dgq, dgkv = _mla_bwd(f"mla_bwd_{l}", dq, dk, dv, s["proj"], cosp, sap, sbp,
                                              vec(q_norm_g, l), vec(kv_norm_g, l), W["wq"][l], W["wk"][l], W["wv"][l],
                                              T, tb)
        gl["wq"][l] = _matmul(f"dw_uq_{l}", s["cq"], dqp, "tn", bf16)
        gl["wk"][l] = _matmul(f"dw_uk_{l}", s["ckv"], dkp, "tn", bf16)
        gl["wv"][l] = _matmul(f"dw_uv_{l}", s["ckv"], dv, "tn", bf16)
        dxl, dcw, dcb, dwa, dba, dwx, dbx, dsp = _lru_bwd(
            f"lru_bwd_{l}", dya, s["proj"], s["u"], s["r"], s["ig"], s["ya"], pos, cw_full[l], wa_b[l], wx_b[l],
            vec(lru_a_param, l), T, tb)
        dproj = jnp.concatenate([dxl, dga, dgb, dqkv], axis=1)
        gl["win"][l] = _matmul(f"dw_in_{l}", s["h1"], dproj, "tn", bf16)
        gl["conv_w"][l], gl["conv_b"][l] = dcw, dcb[0]
        gl["lru_wa"][l], gl["lru_ba"][l], gl["lru_wx"][l], gl["lru_bx"][l] = dwa, dba[0], dwx, dbx[0]
        gl["lru_a_param"][l] = dsp[0]
        gl["q_norm_g"][l], gl["kv_norm_g"][l] = dgq[0], dgkv[0]
        if l == 1:
            dh1 = _matmul("d_h1_1", dproj, W["win"][l], "nt")
            dx, dfo_below, dsc_m, dsh_m, dg_f_below = _norm_mod_bwd(
                "norm_bwd_m_1", dh1, s["x"], sc_m, dx2, T, tb, (mod[0, 5], saved[0]["fo"]))
            sums1 = pair_sums("1", inproj_pieces(1) + rest_pieces(1))
        else:
            gfull = {n: jnp.stack(gl[n]) for n in SMALL if n != "final_norm_g"}
            gfull["final_norm_g"] = dgfin[0]
            gsmall = _pack_rows([gfull[n] for n in SMALL], SMALL_ROWS, f32).reshape(4, SMALL_ROWS // 4, 1024)
            sums0b = pair_sums("0b", inproj_pieces(0) + [gsmall])
            dh1, *lands0b = _matmul("d_h1_0", dproj, W["win"][l], "nt", scatter=sums0b)
            dx, dsc_m, dsh_m = _norm_mod_bwd("norm_bwd_m_0", dh1, s["x"], sc_m, dx2, T, tb)
        dmod[l] = jnp.concatenate([dsh_m, dsc_m, dg_m, dsh_f, dsc_f, dg_f], axis=1)
        if l == 1:
            dfo, dg_f = dfo_below, dg_f_below
    grad_x = dx.reshape(1, T, D)

    dmod_all = _all_gather("gather_dmod", jnp.concatenate(dmod, axis=1), ALL_MASKS, True).reshape(8, 2, 6, D)
    dmod_sum = _sum_slots("sum_dmod", dmod_all.reshape(8, 12, D)).reshape(2, 6 * D)
    dm_sh = lax.dynamic_slice_in_dim(jnp.transpose(dmod_all.reshape(8, 2, 6 * D), (1, 0, 2)), chip * ns_ada, ns_ada, axis=2)
    grads = {"w_ada": _ada_bwd("ada_bwd", jnp.transpose(c_all), dm_sh), "b_ada": dmod_sum}

    rs_bufs[:3] = _chip_sum("rs_chip_sum_0b", sums0b[:3], lands0b[:3], 0, rs_bufs[:3])
    tot_s = _chip_sum("rs_chip_sum_small", sums0b[3:], lands0b[3:], None)[0]
    *bufs, small_all = _join_halves("rs_join_halves", rs_bufs, tot_s)
    small_all = small_all.reshape(SMALL_ROWS, 1024)
    for n, buf in zip(BIG, bufs):
        grads[n] = buf.reshape(weights[n].shape)
    for n, gs in zip(SMALL, _unpack_rows(small_all, [SMALL_SHAPES[n] for n in SMALL])):
        grads[n] = gs
    grads["conv_w"] = lax.dynamic_slice_in_dim(grads["conv_w"], chip * ns_cw, ns_cw, axis=2)

    deltas, new_m, new_v = {}, {}, {}
    for n in order:
        w = weights[n]
        view = (1, w.shape[0]) if w.ndim == 1 else (_size(w.shape[:-1]), w.shape[-1]) if w.shape[-1] >= 128 else (w.shape[0], _size(w.shape[1:]))
        d, m2, v2 = _adamw(f"adamw_{n}", w.reshape(view), grads[n].reshape(view), mom[n].reshape(view), var[n].reshape(view))
        deltas[n], new_m[n], new_v[n] = d.reshape(w.shape), m2.reshape(w.shape), v2.reshape(w.shape)
    return (loss, grad_x, *[grads[n] for n in order], *[deltas[n] for n in order], *[new_m[n] for n in order],
            *[new_v[n] for n in order])
```

```python
import functools

import jax
import jax.numpy as jnp
from jax import lax
from jax.experimental import pallas as pl
from jax.experimental.pallas import tpu as pltpu

f32 = jnp.float32
bf16 = jnp.bfloat16

D = 1024
NH = 8
DN = 128
DR = 64
DV = 128
DH = 256
QR = 256
KVR = 128
DFF = 2816
FH = DFF // 2
FFN_SPLIT = 768
NB = 8
BW = 128
CONV = 4
PW = 3584
QKV0 = 3 * D
DIN = 3520
EPS = 1e-6
LRU_C = 8.0
ROPE_THETA = 10000.0
SCALE = (DN + DR) ** -0.5
LOG2E = 1.4426950408889634
LN2 = 0.6931471805599453
QSCALE = SCALE * LOG2E
NEG = -1e30
HPS = 2
HPS_BWD = 2

ADAM_LR = 0.001
ADAM_B1 = 0.9
ADAM_B2 = 0.999
ADAM_EPS = 1e-08
ADAM_WD = 0.01
ADAM_STEP = 10

VMEM_LIMIT = 56 * 1024 * 1024
ROW_BLOCK = 256
ATTN_BLOCK = 1024
MESH = pl.DeviceIdType.MESH
ANY = pl.BlockSpec(memory_space=pl.ANY)
CHIP_MASKS = ((1, 0, 0), (0, 1, 0), (1, 1, 0))
ALL_MASKS = ((0, 0, 1), (1, 0, 0), (0, 1, 0), (1, 1, 0), (1, 0, 1), (0, 1, 1), (1, 1, 1))


def _cparams(n_axes):
    return pltpu.CompilerParams(dimension_semantics=("arbitrary",) * n_axes, vmem_limit_bytes=VMEM_LIMIT)


def _pick(n, cands):
    for c in cands:
        if n % c == 0:
            return c
    return n


def _rowwise(name, body, T, tb, row_ins, vec_ins, row_outs, acc_outs=(), scratch=(), reverse=False):
    n = T // tb

    def rmap(i):
        return (n - 1 - i) if reverse else i

    in_specs = []
    for (_, w, cb) in row_ins:
        in_specs.append(pl.BlockSpec((tb, w), functools.partial(lambda i, cb: (rmap(i), cb), cb=cb)))
    for v in vec_ins:
        in_specs.append(pl.BlockSpec(v.shape, functools.partial(lambda i, nd: (0,) * nd, nd=v.ndim)))
    out_specs, out_shape = [], []
    for (w, dt) in row_outs:
        out_specs.append(pl.BlockSpec((tb, w), lambda i: (rmap(i), 0)))
        out_shape.append(jax.ShapeDtypeStruct((T, w), dt))
    for (s, dt) in acc_outs:
        out_specs.append(pl.BlockSpec(s, functools.partial(lambda i, nd: (0,) * nd, nd=len(s))))
        out_shape.append(jax.ShapeDtypeStruct(s, dt))

    def kern(*refs):
        body(*refs)

    return pl.pallas_call(
        kern, name=name, grid=(n,), in_specs=in_specs, out_specs=out_specs, out_shape=out_shape,
        scratch_shapes=list(scratch), compiler_params=_cparams(1),
    )(*[a for (a, _, _) in row_ins], *vec_ins)


def _matmul(name, a, b, mode, out_dtype=f32, scatter=None):
    if mode == "nn":
        (M, K), N = a.shape, b.shape[1]
    elif mode == "nt":
        (M, K), N = a.shape, b.shape[0]
    else:
        (K, M), N = a.shape, b.shape[1]
    wide = (1792, 1408, 1024, 512, 256, 128)
    tm = _pick(M, (512, 256, 128)) if mode != "tn" else _pick(M, (1024, 1408, 512, 256, 128))
    tn = _pick(N, wide)
    tk = K if mode != "tn" else _pick(K, (2048, 1024, 512, 256, 128))
    nk = K // tk
    dims = {"nn": (((1,), (0,)), ((), ())), "nt": (((1,), (1,)), ((), ())), "tn": (((0,), (0,)), ((), ()))}[mode]
    if mode == "tn":
        a_spec = pl.BlockSpec((tk, tm), lambda j, i, k: (k, i))
    else:
        a_spec = pl.BlockSpec((tm, tk), lambda j, i, k: (i, k))
    once = {"pipeline_mode": pl.Buffered(1)} if (mode != "tn" and N == tn) else {}
    if mode == "nt":
        b_spec = pl.BlockSpec((tn, tk), lambda j, i, k: (j, k), **once)
    else:
        b_spec = pl.BlockSpec((tk, tn), lambda j, i, k: (k, j), **once)
    o_spec = pl.BlockSpec((tm, tn), lambda j, i, k: (i, j))
    use_acc = nk > 1 and out_dtype != f32

    def kern(a_ref, b_ref, o_ref, *scr):
        k = pl.program_id(2)
        d = lax.dot_general(a_ref[...].astype(bf16), b_ref[...].astype(bf16), dims, preferred_element_type=f32)
        if nk == 1:
            o_ref[...] = d.astype(out_dtype)
        else:
            acc = scr[0] if use_acc else o_ref

            @pl.when(k == 0)
            def _():
                acc[...] = d

            @pl.when(k > 0)
            def _():
                acc[...] += d

            if use_acc:
                @pl.when(k == nk - 1)
                def _():
                    o_ref[...] = acc[...].astype(out_dtype)

    grid = (N // tn, M // tm, nk)
    scratch = [pltpu.VMEM((tm, tn), f32)] if use_acc else []
    if scatter is None:
        return pl.pallas_call(
            kern, name=name, grid=grid, in_specs=[a_spec, b_spec], out_specs=o_spec,
            out_shape=jax.ShapeDtypeStruct((M, N), out_dtype), scratch_shapes=scratch, compiler_params=_cparams(3),
        )(a, b)

    ns = len(scatter)

    def kern_with_scatter(*refs):
        a_ref, b_ref, s_in, o_ref, s_out = refs[0], refs[1], refs[2:2 + ns], refs[2 + ns], refs[3 + ns:3 + 2 * ns]
        scr, sems = refs[3 + 2 * ns:3 + 2 * ns + len(scratch)], refs[3 + 2 * ns + len(scratch):]
        at = [pl.program_id(d) for d in range(3)]

        @pl.when((at[0] == 0) & (at[1] == 0) & (at[2] == 0))
        def _():
            for cp in _scatter_copies(s_in, s_out, *sems):
                cp.start()

        kern(a_ref, b_ref, o_ref, *scr)

        @pl.when((at[0] == grid[0] - 1) & (at[1] == grid[1] - 1) & (at[2] == grid[2] - 1))
        def _():
            for cp in _scatter_copies(s_in, s_out, *sems):
                cp.wait()

    return pl.pallas_call(
        kern_with_scatter, name=name, grid=grid, in_specs=[a_spec, b_spec] + [ANY] * ns,
        out_specs=[o_spec] + [ANY] * ns,
        out_shape=[jax.ShapeDtypeStruct((M, N), out_dtype)] + _scatter_shapes(scatter),
        scratch_shapes=scratch + _scatter_sems(scatter), compiler_params=_cparams(3),
    )(a, b, *scatter)


def _colsum(v):
    return jnp.sum(v, axis=0, keepdims=True)


def _init_acc(step, *refs):
    @pl.when(step == 0)
    def _():
        for r in refs:
            r[...] = jnp.zeros_like(r)


def _norm_mod(name, x, sc, sh, T, tb):
    def body(x_ref, sc_ref, sh_ref, h_ref):
        xv = x_ref[...]
        rstd = lax.rsqrt(jnp.mean(xv * xv, axis=-1, keepdims=True) + EPS)
        h_ref[...] = (xv * rstd * (1.0 + sc_ref[...]) + sh_ref[...]).astype(bf16)

    return _rowwise(name, body, T, tb, [(x, D, 0)], [sc, sh], [(D, bf16)])[0]


def _resid_norm_mod(name, x, g, m, sc, sh, T, tb):
    def body(x_ref, m_ref, g_ref, sc_ref, sh_ref, x2_ref, h_ref):
        xv = x_ref[...] + g_ref[...] * m_ref[...]
        x2_ref[...] = xv
        rstd = lax.rsqrt(jnp.mean(xv * xv, axis=-1, keepdims=True) + EPS)
        h_ref[...] = (xv * rstd * (1.0 + sc_ref[...]) + sh_ref[...]).astype(bf16)

    return _rowwise(name, body, T, tb, [(x, D, 0), (m, D, 0)], [g, sc, sh], [(D, f32), (D, bf16)])


def _norm_mod_bwd(name, dh, x, sc, dres, T, tb, resid=None):
    def norm_part(dh_ref, x_ref, dr_ref, sc_ref, dx_ref, dsc_ref, dsh_ref):
        xv = x_ref[...]
        dhv = dh_ref[...]
        rstd = lax.rsqrt(jnp.mean(xv * xv, axis=-1, keepdims=True) + EPS)
        nv = xv * rstd
        dn = dhv * (1.0 + sc_ref[...])
        dxv = dr_ref[...] + rstd * (dn - nv * jnp.mean(dn * nv, axis=-1, keepdims=True))
        dx_ref[...] = dxv
        dsc_ref[...] += _colsum(dhv * nv)
        dsh_ref[...] += _colsum(dhv)
        return dxv

    if resid is None:
        def body(dh_ref, x_ref, dr_ref, sc_ref, dx_ref, dsc_ref, dsh_ref):
            _init_acc(pl.program_id(0), dsc_ref, dsh_ref)
            norm_part(dh_ref, x_ref, dr_ref, sc_ref, dx_ref, dsc_ref, dsh_ref)

        return _rowwise(name, body, T, tb, [(dh, D, 0), (x, D, 0), (dres, D, 0)], [sc], [(D, f32)],
                        [((1, D), f32)] * 2)

    def body_resid(dh_ref, x_ref, dr_ref, m_ref, sc_ref, g_ref, dx_ref, dm_ref, dsc_ref, dsh_ref, dg_ref):
        _init_acc(pl.program_id(0), dsc_ref, dsh_ref, dg_ref)
        dxv = norm_part(dh_ref, x_ref, dr_ref, sc_ref, dx_ref, dsc_ref, dsh_ref)
        dm_ref[...] = (dxv * g_ref[...]).astype(bf16)
        dg_ref[...] += _colsum(dxv * m_ref[...])

    g, m = resid
    return _rowwise(name, body_resid, T, tb, [(dh, D, 0), (x, D, 0), (dres, D, 0), (m, D, 0)], [sc, g],
                    [(D, f32), (D, bf16)], [((1, D), f32)] * 3)


def _expm1_neg(y):
    poly = y * (1.0 + y * (0.5 + y * (1.0 / 6 + y * (1.0 / 24))))
    return jnp.where(y > -0.05, poly, jnp.exp(y) - 1.0)


def _lru_gates(r, sp, reset):
    log_a = -LRU_C * r * sp
    a = jnp.where(reset, 0.0, jnp.exp(log_a))
    mult = jnp.where(reset, 1.0, jnp.sqrt(-_expm1_neg(2.0 * log_a)))
    return a, mult


def _block_dot(v, w_ref, dims):
    outs = [lax.dot_general(v[:, n * BW:(n + 1) * BW], w_ref[n], dims, preferred_element_type=f32) for n in range(NB)]
    return jnp.concatenate(outs, axis=1)


def _lru_fwd(name, proj, pos, cw, cb, wa, ba, wx, bx, ap, T, tb):
    def body(x_ref, pos_ref, cw_ref, cb_ref, wa_ref, ba_ref, wx_ref, bx_ref, ap_ref,
             u_ref, r_ref, i_ref, h_ref, prev_x, carry_h):
        _init_acc(pl.program_id(0), prev_x, carry_h)
        x = x_ref[...].astype(f32)
        px = prev_x[...]
        row = lax.broadcasted_iota(jnp.int32, (tb, 1), 0)
        u = x * cw_ref[CONV - 1:CONV, :] + cb_ref[...]
        for j in range(1, CONV):
            z = jnp.where(row >= tb - j, px, x)
            u = u + pltpu.roll(z, j, 0) * cw_ref[CONV - 1 - j:CONV - j, :]
        ub = u.astype(bf16)
        nn = (((1,), (0,)), ((), ()))
        r = jax.nn.sigmoid(_block_dot(ub, wa_ref, nn) + ba_ref[...])
        ig = jax.nn.sigmoid(_block_dot(ub, wx_ref, nn) + bx_ref[...])
        sp = jax.nn.softplus(-ap_ref[...])
        reset = pos_ref[...] == 0
        a, mult = _lru_gates(r, sp, reset)
        b = u * ig * mult
        sub = row & 7
        for s in (1, 2, 4):
            keep = sub >= s
            a_sh = jnp.where(keep, pltpu.roll(a, s, 0), 1.0)
            b_sh = jnp.where(keep, pltpu.roll(b, s, 0), 0.0)
            b = a * b_sh + b
            a = a * a_sh
        carry = carry_h[...]
        for g in range(tb // 8):
            rows = slice(8 * g, 8 * g + 8)
            hg = b[rows] + a[rows] * carry
            h_ref[rows, :] = hg
            carry = hg[7:8]
        carry_h[...] = carry
        u_ref[...] = u
        r_ref[...] = r
        i_ref[...] = ig
        prev_x[...] = x

    return _rowwise(name, body, T, tb, [(proj, D, 0), (pos, 1, 0)], [cw, cb, wa, ba, wx, bx, ap],
                    [(D, f32)] * 4, scratch=[pltpu.VMEM((tb, D), f32), pltpu.VMEM((1, D), f32)])


def _lru_bwd(name, dh, proj, u, r, ig, h, pos, cw, wa, wx, ap, T, tb):
    n = T // tb
    t8 = tb // 8

    def rmap(i):
        return n - 1 - i

    def body(dh_ref, x_ref, u_ref, r_ref, i_ref, h_ref, hp_ref, pos_ref, cw_ref, wa_ref, wx_ref, ap_ref,
             dx_ref, dcw_ref, dcb_ref, dwa_ref, dba_ref, dwx_ref, dbx_ref, dsp_ref, next_du, carry_g, carry_a):
        step = pl.program_id(0)
        _init_acc(step, dcw_ref, dcb_ref, dwa_ref, dba_ref, dwx_ref, dbx_ref, dsp_ref, next_du, carry_g, carry_a)
        blk = n - 1 - step
        x = x_ref[...].astype(f32)
        u = u_ref[...]
        r = r_ref[...]
        ig = i_ref[...]
        h = h_ref[...]
        row = lax.broadcasted_iota(jnp.int32, (tb, 1), 0)
        sp = jax.nn.softplus(-ap_ref[...])
        reset = pos_ref[...] == 0
        a, mult = _lru_gates(r, sp, reset)
        A = pltpu.roll(jnp.where(row == 0, carry_a[...], a), tb - 1, 0)
        B = dh_ref[...]
        sub = row & 7
        for s in (1, 2, 4):
            keep = sub < 8 - s
            a_sh = jnp.where(keep, pltpu.roll(A, tb - s, 0), 1.0)
            b_sh = jnp.where(keep, pltpu.roll(B, tb - s, 0), 0.0)
            B = B + A * b_sh
            A = A * a_sh
        carry = carry_g[...]
        parts = [None] * (tb // 8)
        for grp in reversed(range(tb // 8)):
            rows = slice(8 * grp, 8 * grp + 8)
            parts[grp] = B[rows] + A[rows] * carry
            carry = parts[grp][0:1]
        g = jnp.concatenate(parts, axis=0)
        carry_g[...] = carry
        carry_a[...] = a[0:1, :]
        h_last = jnp.where(blk == 0, 0.0, hp_ref[7:8, :])
        h_prev = pltpu.roll(jnp.where(row == tb - 1, h_last, h), 1, 0)
        da = g * h_prev
        gu = g * u
        dlog_a = jnp.where(reset, 0.0, da * a - gu * ig * (a * a) / mult)
        du = g * ig * mult
        di = gu * mult
        dr = dlog_a * (-LRU_C * sp)
        dsp_ref[...] += _colsum(dlog_a * (-LRU_C * r)) * (-jax.nn.sigmoid(-ap_ref[...]))
        dzr = dr * r * (1.0 - r)
        dzi = di * ig * (1.0 - ig)
        dzr_b = dzr.astype(bf16)
        dzi_b = dzi.astype(bf16)
        nt = (((1,), (1,)), ((), ()))
        du = du + _block_dot(dzr_b, wa_ref, nt) + _block_dot(dzi_b, wx_ref, nt)
        ub = u.astype(bf16)
        tn = (((0,), (0,)), ((), ()))
        for nb in range(NB):
            sl = slice(nb * BW, (nb + 1) * BW)
            dwa_ref[nb] += lax.dot_general(ub[:, sl], dzr_b[:, sl], tn, preferred_element_type=f32)
            dwx_ref[nb] += lax.dot_general(ub[:, sl], dzi_b[:, sl], tn, preferred_element_type=f32)
        dba_ref[...] += _colsum(dzr)
        dbx_ref[...] += _colsum(dzi)
        dcb_ref[...] += _colsum(du)
        ndu = next_du[...]
        dx = du * cw_ref[CONV - 1:CONV, :]
        dcw_ref[CONV - 1:CONV, :] += _colsum(x * du)
        for j in range(1, CONV):
            up = pltpu.roll(jnp.where(row < j, ndu, du), tb - j, 0)
            dx = dx + up * cw_ref[CONV - 1 - j:CONV - j, :]
            dcw_ref[CONV - 1 - j:CONV - j, :] += _colsum(x * up)
        dx_ref[...] = dx.astype(bf16)
        next_du[...] = du

    row_specs = [pl.BlockSpec((tb, D), lambda i: (rmap(i), 0)) for _ in range(6)]
    in_specs = row_specs + [
        pl.BlockSpec((8, D), lambda i: (jnp.maximum(rmap(i) * t8 - 1, 0), 0)),
        pl.BlockSpec((tb, 1), lambda i: (rmap(i), 0)),
        pl.BlockSpec((CONV, D), lambda i: (0, 0)),
        pl.BlockSpec((NB, BW, BW), lambda i: (0, 0, 0)),
        pl.BlockSpec((NB, BW, BW), lambda i: (0, 0, 0)),
        pl.BlockSpec((1, D), lambda i: (0, 0)),
    ]
    vec = lambda s: pl.BlockSpec(s, functools.partial(lambda i, nd: (0,) * nd, nd=len(s)))
    acc_shapes = [(CONV, D), (1, D), (NB, BW, BW), (1, D), (NB, BW, BW), (1, D), (1, D)]
    return pl.pallas_call(
        body, name=name, grid=(n,), in_specs=in_specs,
        out_specs=[pl.BlockSpec((tb, D), lambda i: (rmap(i), 0))] + [vec(s) for s in acc_shapes],
        out_shape=[jax.ShapeDtypeStruct((T, D), bf16)] + [jax.ShapeDtypeStruct(s, f32) for s in acc_shapes],
        scratch_shapes=[pltpu.VMEM((tb, D), f32), pltpu.VMEM((1, D), f32), pltpu.VMEM((1, D), f32)],
        compiler_params=_cparams(1),
    )(dh, proj, u, r, ig, h, h, pos, cw, wa, wx, ap)


def _rope_tables(name, pos, invf, T, tb):
    def body(pos_ref, invf_ref, cos_ref, sa_ref, sb_ref):
        ang = pos_ref[...].astype(f32) * invf_ref[...]
        lane = lax.broadcasted_iota(jnp.int32, (tb, DH), 1)
        first = (lane >= DN) & (lane < DN + DR // 2)
        second = (lane >= DN + DR // 2) & (lane < DN + DR)
        c = jnp.cos(ang)
        s = jnp.sin(ang)
        cos_ref[...] = jnp.where(lane < DN, 1.0, jnp.where(first | second, c, 0.0))
        sa_ref[...] = jnp.where(first, -s, 0.0)
        sb_ref[...] = jnp.where(second, s, 0.0)

    return _rowwise(name, body, T, tb, [(pos, 1, 0)], [invf], [(DH, f32)] * 3)


def _rope_heads(z, cos, sa, sb, transposed=False, shared=None):
    half = DR // 2
    parts = []
    for hd in range(NH):
        hi = z[:, hd * DH + DN:(hd + 1) * DH]
        if shared is not None:
            hi = hi + shared
        if transposed:
            hi = hi * cos + pltpu.roll(hi * sa, half, 1) + pltpu.roll(hi * sb, DN - half, 1)
        else:
            hi = hi * cos + pltpu.roll(hi, DN - half, 1) * sa + pltpu.roll(hi, half, 1) * sb
        parts += [z[:, hd * DH:hd * DH + DN], hi]
    return parts


def _mla_prep(name, proj, cosp, sap, sbp, gq, gkv, wq, wk, wv, T, tb):
    def body(qkv_ref, cos_ref, sa_ref, sb_ref, gq_ref, gkv_ref, wq_ref, wk_ref, wv_ref,
             q_ref, k_ref, v_ref, cq_ref, ckv_ref):
        qkv = qkv_ref[...].astype(f32)
        qd = qkv[:, :QR]
        kvd = qkv[:, QR:QR + KVR]
        slab = qkv[:, QR + KVR:]
        nq = qd * lax.rsqrt(jnp.mean(qd * qd, axis=-1, keepdims=True) + EPS)
        cq = (nq * gq_ref[...]).astype(bf16)
        cq_scaled = (nq * (gq_ref[...] * QSCALE)).astype(bf16)
        ckv = (kvd * lax.rsqrt(jnp.mean(kvd * kvd, axis=-1, keepdims=True) + EPS) * gkv_ref[...]).astype(bf16)
        cos, sa, sb = cos_ref[:, DN:], sa_ref[:, DN:], sb_ref[:, DN:]
        qpre = jnp.dot(cq_scaled, wq_ref[...], preferred_element_type=f32)
        kpre = jnp.dot(ckv, wk_ref[...], preferred_element_type=f32)
        q_ref[...] = jnp.concatenate(_rope_heads(qpre, cos, sa, sb), axis=1).astype(bf16)
        k_ref[...] = jnp.concatenate(_rope_heads(kpre, cos, sa, sb, shared=slab), axis=1).astype(bf16)
        v_ref[...] = jnp.dot(ckv, wv_ref[...], preferred_element_type=f32).astype(bf16)
        cq_ref[...] = cq
        ckv_ref[...] = ckv

    return _rowwise(name, body, T, tb,
                    [(proj, 512, QKV0 // 512), (cosp, DH, 0), (sap, DH, 0), (sbp, DH, 0)],
                    [gq, gkv, wq, wk, wv],
                    [(NH * DH, bf16), (NH * DH, bf16), (NH * DV, bf16), (QR, bf16), (KVR, bf16)])


def _mla_bwd(name, dq, dk, dv, proj, cosp, sap, sbp, gq, gkv, wq, wk, wv, T, tb):
    def body(dq_ref, dk_ref, dv_ref, qkv_ref, cos_ref, sa_ref, sb_ref, gq_ref, gkv_ref, wq_ref, wk_ref, wv_ref,
             dqkv_ref, dqp_ref, dkp_ref, dgq_ref, dgkv_ref):
        _init_acc(pl.program_id(0), dgq_ref, dgkv_ref)
        cos, sa, sb = cos_ref[:, DN:], sa_ref[:, DN:], sb_ref[:, DN:]
        dqp_b = jnp.concatenate(_rope_heads(dq_ref[...] * SCALE, cos, sa, sb, transposed=True), axis=1).astype(bf16)
        dk_parts = _rope_heads(dk_ref[...], cos, sa, sb, transposed=True)
        dkp_b = jnp.concatenate(dk_parts, axis=1).astype(bf16)
        dqp_ref[...] = dqp_b
        dkp_ref[...] = dkp_b
        nt = (((1,), (1,)), ((), ()))
        dcq = lax.dot_general(dqp_b, wq_ref[...], nt, preferred_element_type=f32)
        dckv = (lax.dot_general(dkp_b, wk_ref[...], nt, preferred_element_type=f32)
                + lax.dot_general(dv_ref[...], wv_ref[...], nt, preferred_element_type=f32))
        dslab = dk_parts[1]
        for hd in range(1, NH):
            dslab = dslab + dk_parts[2 * hd + 1]
        qkv = qkv_ref[...].astype(f32)
        qd = qkv[:, :QR]
        kvd = qkv[:, QR:QR + KVR]
        nq = qd * lax.rsqrt(jnp.mean(qd * qd, axis=-1, keepdims=True) + EPS)
        nkv = kvd * lax.rsqrt(jnp.mean(kvd * kvd, axis=-1, keepdims=True) + EPS)
        dgq_ref[...] += _colsum(dcq * nq)
        dgkv_ref[...] += _colsum(dckv * nkv)
        dnq = dcq * gq_ref[...]
        dnkv = dckv * gkv_ref[...]
        rq = lax.rsqrt(jnp.mean(qd * qd, axis=-1, keepdims=True) + EPS)
        rkv = lax.rsqrt(jnp.mean(kvd * kvd, axis=-1, keepdims=True) + EPS)
        dqd = rq * (dnq - nq * jnp.mean(dnq * nq, axis=-1, keepdims=True))
        dkvd = rkv * (dnkv - nkv * jnp.mean(dnkv * nkv, axis=-1, keepdims=True))
        dqkv_ref[...] = jnp.concatenate([dqd, dkvd, dslab], axis=1).astype(bf16)

    return _rowwise(name, body, T, tb,
                    [(dq, NH * DH, 0), (dk, NH * DH, 0), (dv, NH * DV, 0), (proj, 512, QKV0 // 512),
                     (cosp, DH, 0), (sap, DH, 0), (sbp, DH, 0)],
                    [gq, gkv, wq, wk, wv],
                    [(512, bf16), (NH * DH, bf16), (NH * DH, bf16)],
                    [((1, QR), f32), ((1, KVR), f32)])


def _flash_fwd(name, q, k, v, T, tq, pack=None):
    nq = T // tq
    unroll = 2

    def body(q_ref, k_ref, v_ref, o_ref, lse_ref):
        i = pl.program_id(1)
        nt = (((1,), (1,)), ((), ()))
        qs = [q_ref[:, hh * DH:(hh + 1) * DH] for hh in range(HPS)]

        def one(hh, j, carry, masked):
            m, l, acc = carry
            off = pl.multiple_of(j * tq, tq)
            kj = k_ref[pl.ds(off, tq), hh * DH:(hh + 1) * DH]
            vj = v_ref[pl.ds(off, tq), hh * DV:(hh + 1) * DV]
            s = lax.dot_general(qs[hh], kj, nt, preferred_element_type=f32)
            if masked:
                rr = lax.broadcasted_iota(jnp.int32, (tq, tq), 0)
                cc = lax.broadcasted_iota(jnp.int32, (tq, tq), 1)
                s = jnp.where(cc <= rr, s, NEG)
            m_new = jnp.maximum(m, jnp.max(s, axis=-1, keepdims=True))
            alpha = jnp.exp2(m - m_new)
            p = jnp.exp2(s - m_new)
            l = alpha * l + jnp.sum(p, axis=-1, keepdims=True)
            acc = alpha * acc + jnp.dot(p.astype(bf16), vj, preferred_element_type=f32)
            return m_new, l, acc

        def step(j, carries, masked):
            return tuple(one(hh, j, carries[hh], masked) for hh in range(HPS))

        def pair(jj, carries):
            for u in range(unroll):
                carries = step(jj * unroll + u, carries, False)
            return carries

        init1 = (jnp.full((tq, 1), NEG, f32), jnp.zeros((tq, 1), f32), jnp.zeros((tq, DV), f32))
        carry = lax.fori_loop(0, i // unroll, pair, tuple(init1 for _ in range(HPS)))
        carry = lax.fori_loop((i // unroll) * unroll, i, lambda j, c: step(j, c, False), carry)
        carry = step(i, carry, True)
        for hh in range(HPS):
            m, l, acc = carry[hh]
            o_ref[:, hh * DV:(hh + 1) * DV] = (acc / l).astype(bf16)
            lse_ref[hh] = m + jnp.log2(l)

    def body_with_gather(q_ref, k_ref, v_ref, *rest):
        nw = len(pack)
        w_refs, (o_ref, lse_ref), wall_refs, sems = rest[:nw], rest[nw:nw + 2], rest[nw + 2:2 * nw + 2], rest[2 * nw + 2:]
        h = pl.program_id(0)
        i = pl.program_id(1)

        @pl.when((h == 0) & (i == 0))
        def _():
            mine, first, _ = _gather_copies(w_refs, wall_refs, *sems)
            _gather_start(mine, first)

        @pl.when((h == ngrp // 2) & (i == 0))
        def _():
            _, first, passed = _gather_copies(w_refs, wall_refs, *sems)
            _gather_forward(first, passed)

        body(q_ref, k_ref, v_ref, o_ref, lse_ref)

        @pl.when((h == ngrp - 1) & (i == nq - 1))
        def _():
            _gather_finish(*_gather_copies(w_refs, wall_refs, *sems))

    ngrp = NH // HPS
    in_specs = [pl.BlockSpec((tq, HPS * DH), lambda h, i: (i, h)),
                pl.BlockSpec((T, HPS * DH), lambda h, i: (0, h), pipeline_mode=pl.Buffered(1)),
                pl.BlockSpec((T, HPS * DV), lambda h, i: (0, h), pipeline_mode=pl.Buffered(1))]
    out_specs = [pl.BlockSpec((tq, HPS * DV), lambda h, i: (i, h)), pl.BlockSpec((HPS, tq, 1), lambda h, i: (h, i, 0))]
    out_shape = [jax.ShapeDtypeStruct((T, NH * DV), bf16), jax.ShapeDtypeStruct((NH, T, 1), f32)]
    if pack is None:
        return pl.pallas_call(body, name=name, grid=(ngrp, nq), in_specs=in_specs, out_specs=out_specs,
                              out_shape=out_shape, compiler_params=_cparams(2))(q, k, v)
    return pl.pallas_call(
        body_with_gather, name=name, grid=(ngrp, nq), in_specs=in_specs + [ANY] * len(pack),
        out_specs=out_specs + [ANY] * len(pack),
        out_shape=out_shape + [jax.ShapeDtypeStruct((4,) + a.shape, a.dtype) for a in pack],
        scratch_shapes=_gather_sems(pack), compiler_params=_cparams(2),
    )(q, k, v, *pack)


def _flash_bwd(name, q, k, v, do, lse, dl, T, tq, scatter=None):
    HPS = HPS_BWD
    nq = T // tq

    def body(k_ref, v_ref, q_ref, do_ref, lse_ref, dl_ref, dk_ref, dv_ref, dq_ref):
        j = pl.program_id(1)

        @pl.when(j == 0)
        def _():
            dq_ref[...] = jnp.zeros_like(dq_ref)

        nt = (((1,), (1,)), ((), ()))
        tn = (((0,), (0,)), ((), ()))
        ks = [k_ref[:, hh * DH:(hh + 1) * DH] for hh in range(HPS)]
        vs = [v_ref[:, hh * DV:(hh + 1) * DV] for hh in range(HPS)]

        def one(hh, i, carry, masked):
            dk, dv = carry
            off = pl.multiple_of(i * tq, tq)
            qi = q_ref[pl.ds(off, tq), hh * DH:(hh + 1) * DH]
            doi = do_ref[pl.ds(off, tq), hh * DV:(hh + 1) * DV]
            st = lax.dot_general(ks[hh], qi, nt, preferred_element_type=f32)
            if masked:
                kr = lax.broadcasted_iota(jnp.int32, (tq, tq), 0)
                qc = lax.broadcasted_iota(jnp.int32, (tq, tq), 1)
                st = jnp.where(qc >= kr, st, NEG)
            dpt = lax.dot_general(vs[hh], doi, nt, preferred_element_type=f32)
            pt = jnp.exp2(st - lse_ref[hh, i])
            dst = (pt * (dpt - dl_ref[hh, i])).astype(bf16)
            dv = dv + jnp.dot(pt.astype(bf16), doi, preferred_element_type=f32)
            dk = dk + jnp.dot(dst, qi, preferred_element_type=f32)
            dq_ref[pl.ds(off, tq), hh * DH:(hh + 1) * DH] += lax.dot_general(dst, ks[hh], tn, preferred_element_type=f32)
            return dk, dv

        def step(i, carries, masked):
            return tuple(one(hh, i, carries[hh], masked) for hh in range(HPS))

        zero = tuple((jnp.zeros((tq, DH), f32), jnp.zeros((tq, DV), f32)) for _ in range(HPS))
        carry = step(j, zero, True)
        carry = lax.fori_loop(j + 1, nq, lambda i, c: step(i, c, False), carry)
        for hh in range(HPS):
            dk, dv = carry[hh]
            dk_ref[:, hh * DH:(hh + 1) * DH] = dk * LN2
            dv_ref[:, hh * DV:(hh + 1) * DV] = dv.astype(bf16)

    def body_with_scatter(*refs):
        ns = len(scatter)
        ins, a_refs, outs = refs[:6], refs[6:6 + ns], refs[6 + ns:9 + ns]
        o_refs, sems = refs[9 + ns:9 + 2 * ns], refs[9 + 2 * ns:]
        h = pl.program_id(0)
        j = pl.program_id(1)

        @pl.when((h == 0) & (j == 0))
        def _():
            for cp in _scatter_copies(a_refs, o_refs, *sems):
                cp.start()

        body(*ins, *outs)

        @pl.when((h == ngrp - 1) & (j == nq - 1))
        def _():
            for cp in _scatter_copies(a_refs, o_refs, *sems):
                cp.wait()

    ngrp = NH // HPS
    in_specs = [pl.BlockSpec((tq, HPS * DH), lambda h, j: (j, h)),
                pl.BlockSpec((tq, HPS * DV), lambda h, j: (j, h)),
                pl.BlockSpec((T, HPS * DH), lambda h, j: (0, h), pipeline_mode=pl.Buffered(1)),
                pl.BlockSpec((T, HPS * DV), lambda h, j: (0, h), pipeline_mode=pl.Buffered(1)),
                pl.BlockSpec((HPS, nq, 1, tq), lambda h, j: (h, 0, 0, 0)),
                pl.BlockSpec((HPS, nq, 1, tq), lambda h, j: (h, 0, 0, 0))]
    out_specs = [pl.BlockSpec((tq, HPS * DH), lambda h, j: (j, h)),
                 pl.BlockSpec((tq, HPS * DV), lambda h, j: (j, h)),
                 pl.BlockSpec((T, HPS * DH), lambda h, j: (0, h), pipeline_mode=pl.Buffered(1))]
    out_shape = [jax.ShapeDtypeStruct((T, NH * DH), f32), jax.ShapeDtypeStruct((T, NH * DV), bf16),
                 jax.ShapeDtypeStruct((T, NH * DH), f32)]
    if scatter is None:
        return pl.pallas_call(body, name=name, grid=(ngrp, nq), in_specs=in_specs, out_specs=out_specs,
                              out_shape=out_shape, compiler_params=_cparams(2))(k, v, q, do, lse, dl)
    ns = len(scatter)
    return pl.pallas_call(
        body_with_scatter, name=name, grid=(ngrp, nq), in_specs=in_specs + [ANY] * ns,
        out_specs=out_specs + [ANY] * ns, out_shape=out_shape + _scatter_shapes(scatter),
        scratch_shapes=_scatter_sems(scatter), compiler_params=_cparams(2),
    )(k, v, q, do, lse, dl, *scatter)


def _gate_mix(name, proj, ya, o, T, tb):
    def body(ga_ref, gb_ref, ya_ref, o_ref, y_ref):
        y_ref[...] = (jax.nn.sigmoid(ga_ref[...].astype(f32)) * ya_ref[...]
                      + jax.nn.sigmoid(gb_ref[...].astype(f32)) * o_ref[...].astype(f32)).astype(bf16)

    return _rowwise(name, body, T, tb, [(proj, D, 1), (proj, D, 2), (ya, D, 0), (o, D, 0)], [], [(D, bf16)])[0]


def _gate_mix_bwd(name, dy, proj, ya, o, T, tb):
    def body(dy_ref, ga_ref, gb_ref, ya_ref, o_ref, dga_ref, dgb_ref, dya_ref, do_ref, dl_ref):
        dy = dy_ref[...]
        sa = jax.nn.sigmoid(ga_ref[...].astype(f32))
        sb = jax.nn.sigmoid(gb_ref[...].astype(f32))
        ov = o_ref[...].astype(f32)
        dga_ref[...] = (dy * ya_ref[...] * sa * (1.0 - sa)).astype(bf16)
        dgb_ref[...] = (dy * ov * sb * (1.0 - sb)).astype(bf16)
        dya_ref[...] = dy * sa
        do = dy * sb
        do_ref[...] = do.astype(bf16)
        prod = do * ov
        for hd in range(NH):
            dl_ref[:, hd:hd + 1] = jnp.sum(prod[:, hd * DV:(hd + 1) * DV], axis=-1, keepdims=True)

    return _rowwise(name, body, T, tb, [(dy, D, 0), (proj, D, 1), (proj, D, 2), (ya, D, 0), (o, D, 0)], [],
                    [(D, bf16), (D, bf16), (D, f32), (D, bf16), (NH, f32)])


def _ffn_in_swiglu(name, h, w, T):
    tm = _pick(T, (512, 256, 128))

    def body(h_ref, w_ref, z_ref, a_ref):
        z = jnp.dot(h_ref[...], w_ref[...], preferred_element_type=f32)
        g = z[:, :FH]
        z_ref[...] = z.astype(bf16)
        a_ref[...] = (g * jax.nn.sigmoid(g) * z[:, FH:]).astype(bf16)

    return pl.pallas_call(
        body, name=name, grid=(2, T // tm),
        in_specs=[pl.BlockSpec((tm, D), lambda j, i: (i, 0)), pl.BlockSpec((D, 2 * FH), lambda j, i: (0, j))],
        out_specs=[pl.BlockSpec((tm, 2 * FH), lambda j, i: (i, j)), pl.BlockSpec((tm, FH), lambda j, i: (i, j))],
        out_shape=[jax.ShapeDtypeStruct((T, 2 * DFF), bf16), jax.ShapeDtypeStruct((T, DFF), bf16)],
        compiler_params=_cparams(2),
    )(h, w)


def _dact_swiglu_bwd(name, dfo, w, z, T):
    tm = _pick(T, (512, 256, 128))

    def body(d_ref, w_ref, z_ref, o_ref):
        for c0, c1 in ((0, FFN_SPLIT), (FFN_SPLIT, FH)):
            da = lax.dot_general(d_ref[...], w_ref[c0:c1, :], (((1,), (1,)), ((), ())), preferred_element_type=f32)
            g = z_ref[:, c0:c1].astype(f32)
            u = z_ref[:, FH + c0:FH + c1].astype(f32)
            sg = jax.nn.sigmoid(g)
            o_ref[:, c0:c1] = (da * u * sg * (1.0 + g * (1.0 - sg))).astype(bf16)
            o_ref[:, FH + c0:FH + c1] = (da * g * sg).astype(bf16)

    return pl.pallas_call(
        body, name=name, grid=(2, T // tm),
        in_specs=[pl.BlockSpec((tm, D), lambda j, i: (i, 0)), pl.BlockSpec((FH, D), lambda j, i: (j, 0)),
                  pl.BlockSpec((tm, 2 * FH), lambda j, i: (i, j))],
        out_specs=pl.BlockSpec((tm, 2 * FH), lambda j, i: (i, j)),
        out_shape=jax.ShapeDtypeStruct((T, 2 * DFF), bf16), compiler_params=_cparams(2),
    )(dfo, w, z)


def _final(name, x, g, m, gfin, tgt, T, tb):
    def body(x_ref, m_ref, t_ref, g_ref, gf_ref, dx_ref, dm_ref, loss_ref, dgf_ref, dg_ref):
        _init_acc(pl.program_id(0), loss_ref, dgf_ref, dg_ref)
        mv = m_ref[...]
        xv = x_ref[...] + g_ref[...] * mv
        rstd = lax.rsqrt(jnp.mean(xv * xv, axis=-1, keepdims=True) + EPS)
        nv = xv * rstd
        err = nv * gf_ref[...] - t_ref[...]
        loss_ref[...] += 0.5 * jnp.sum(jnp.mean(err * err, axis=-1, keepdims=True), axis=0, keepdims=True)
        dy = err * (1.0 / D)
        dgf_ref[...] += _colsum(dy * nv)
        dn = dy * gf_ref[...]
        dxv = rstd * (dn - nv * jnp.mean(dn * nv, axis=-1, keepdims=True))
        dx_ref[...] = dxv
        dm_ref[...] = (dxv * g_ref[...]).astype(bf16)
        dg_ref[...] += _colsum(dxv * mv)

    return _rowwise(name, body, T, tb, [(x, D, 0), (m, D, 0), (tgt, D, 0)], [g, gfin], [(D, f32), (D, bf16)],
                    [((1, 1), f32), ((1, D), f32), ((1, D), f32)])


def _adamw(name, w, g, m, v):
    R, C = w.shape
    tb = _pick(R, (256, 128, 64, 32, 16, 8))
    c1 = 1.0 - ADAM_B1 ** ADAM_STEP
    c2 = 1.0 - ADAM_B2 ** ADAM_STEP

    def body(w_ref, g_ref, m_ref, v_ref, d_ref, m2_ref, v2_ref):
        gv = g_ref[...]
        m2 = ADAM_B1 * m_ref[...] + (1.0 - ADAM_B1) * gv
        v2 = ADAM_B2 * v_ref[...] + (1.0 - ADAM_B2) * (gv * gv)
        m2_ref[...] = m2
        v2_ref[...] = v2
        d_ref[...] = -ADAM_LR * ((m2 / c1) / (jnp.sqrt(v2 / c2) + ADAM_EPS) + ADAM_WD * w_ref[...])

    return _rowwise(name, body, R, tb, [(w, C, 0), (g, C, 0), (m, C, 0), (v, C, 0)], [], [(C, f32)] * 3)


def _ada_fwd(name, c_all, w_ada, b_sh):
    nl, _, ns = w_ada.shape

    def body(c_ref, w_ref, b_ref, o_ref):
        o_ref[0] = jnp.dot(c_ref[...].astype(bf16), w_ref[0].astype(bf16), preferred_element_type=f32) + b_ref[0]

    return pl.pallas_call(
        body, name=name, grid=(nl,),
        in_specs=[pl.BlockSpec((8, D), lambda l: (0, 0)), pl.BlockSpec((1, D, ns), lambda l: (l, 0, 0)),
                  pl.BlockSpec((1, 1, ns), lambda l: (l, 0, 0))],
        out_specs=pl.BlockSpec((1, 8, ns), lambda l: (l, 0, 0)),
        out_shape=jax.ShapeDtypeStruct((nl, 8, ns), f32), compiler_params=_cparams(1),
    )(c_all, w_ada, b_sh)


def _ada_bwd(name, c_t, dm_sh):
    nl, _, ns = dm_sh.shape

    def body(c_ref, dm_ref, o_ref):
        acc = c_ref[:, 0:1] * dm_ref[0, 0:1, :]
        for d in range(1, 8):
            acc = acc + c_ref[:, d:d + 1] * dm_ref[0, d:d + 1, :]
        o_ref[0] = acc

    return pl.pallas_call(
        body, name=name, grid=(nl,),
        in_specs=[pl.BlockSpec((D, 8), lambda l: (0, 0)), pl.BlockSpec((1, 8, ns), lambda l: (l, 0, 0))],
        out_specs=pl.BlockSpec((1, D, ns), lambda l: (l, 0, 0)),
        out_shape=jax.ShapeDtypeStruct((nl, D, ns), f32), compiler_params=_cparams(1),
    )(c_t, dm_sh)


def _sum_slots(name, a):
    n, R, C = a.shape
    tb = _pick(R, (256, 128, 64, 32, 16, 8))

    def body(a_ref, o_ref):
        acc = a_ref[0].astype(f32)
        for s in range(1, n):
            acc = acc + a_ref[s].astype(f32)
        o_ref[...] = acc

    return pl.pallas_call(
        body, name=name, grid=(R // tb,), in_specs=[pl.BlockSpec((n, tb, C), lambda i: (0, i, 0))],
        out_specs=pl.BlockSpec((tb, C), lambda i: (i, 0)), out_shape=jax.ShapeDtypeStruct((R, C), f32),
        compiler_params=_cparams(1),
    )(a)


def _chunks(rows, dtype, want):
    unit = 16 if dtype == bf16 else 8
    for n in range(want, 0, -1):
        if rows % n == 0 and (rows // n) % unit == 0:
            return n, rows // n
    return 1, rows


def _my_chip():
    return 2 * lax.axis_index("x") + lax.axis_index("y")


def _pair_sum(name, arrs, gots):
    na = len(arrs)

    def body(*refs):
        for k in range(na):
            a_ref, g_ref, o_ref = refs[k], refs[na + k], refs[2 * na + k]
            o_ref[...] = (a_ref[...].astype(f32) + g_ref[...].astype(f32)).astype(o_ref.dtype)

    in_specs = [pl.BlockSpec((1, a.shape[1] // 4, a.shape[2]), lambda s, i: (s, 2 * lax.axis_index("c") + i, 0))
                for a in arrs]
    half_specs = [pl.BlockSpec((1, g.shape[1] // 2, g.shape[2]), lambda s, i: (s, i, 0)) for g in gots]
    return pl.pallas_call(
        body, name=name, grid=(4, 2), in_specs=in_specs + half_specs, out_specs=half_specs,
        out_shape=[jax.ShapeDtypeStruct(g.shape, a.dtype) for a, g in zip(arrs, gots)], compiler_params=_cparams(2),
    )(*arrs, *gots)


def _chip_sum(name, sums, lands, layer, bufs=None):
    na = len(sums)

    def body(*refs):
        for k in range(na):
            s_ref, l_ref, o_ref = refs[k], refs[na + k], refs[len(refs) - na + k]
            acc = s_ref[0].astype(f32)
            for j in range(l_ref.shape[0]):
                acc = acc + l_ref[j].astype(f32)
            o_ref[...] = acc.reshape(o_ref.shape)

    in_specs = [pl.BlockSpec((1, s.shape[1] // 2, s.shape[2]), lambda i: (_my_chip(), i, 0)) for s in sums]
    in_specs += [pl.BlockSpec((3, l.shape[1] // 2, l.shape[2]), lambda i: (0, i, 0)) for l in lands]
    if layer is None:
        out_specs = [pl.BlockSpec((s.shape[1] // 2, s.shape[2]), lambda i: (i, 0)) for s in sums]
        out_shape = [jax.ShapeDtypeStruct(s.shape[1:], f32) for s in sums]
    else:
        out_specs = [pl.BlockSpec((1, 1, s.shape[1] // 2, s.shape[2]), lambda i: (layer, lax.axis_index("c"), i, 0))
                     for s in sums]
        out_shape = [jax.ShapeDtypeStruct((2, 2) + s.shape[1:], f32) for s in sums]
    aliases = {}
    operands = list(sums) + list(lands)
    if bufs is not None:
        in_specs += [ANY] * na
        aliases = {2 * na + k: k for k in range(na)}
        operands += list(bufs)
    return pl.pallas_call(
        body, name=name, grid=(2,), in_specs=in_specs, out_specs=out_specs, out_shape=out_shape,
        input_output_aliases=aliases, compiler_params=_cparams(1),
    )(*operands)


def _coords():
    return lax.axis_index("x"), lax.axis_index("y"), lax.axis_index("c")


def _flip(v, m):
    return (1 - v) if m else v


def _peer(x, y, c, mask):
    return (_flip(x, mask[0]), _flip(y, mask[1]), _flip(c, mask[2]))


def _all_gather(name, arr, masks, whole_mesh):
    nslots = 8 if whole_mesh else 4

    def slot(x, y, c):
        return 4 * x + 2 * y + c if whole_mesh else 2 * x + y

    def body(a_ref, o_ref, send_sems, recv_sems, local_sem):
        x, y, c = _coords()
        mine = pltpu.make_async_copy(a_ref, o_ref.at[slot(x, y, c)], local_sem)
        mine.start()
        copies = []
        for k, mask in enumerate(masks):
            cp = pltpu.make_async_remote_copy(
                src_ref=a_ref, dst_ref=o_ref.at[slot(x, y, c)], send_sem=send_sems.at[k], recv_sem=recv_sems.at[k],
                device_id=_peer(x, y, c, mask), device_id_type=MESH)
            cp.start()
            copies.append(cp)
        for cp in copies:
            cp.wait()
        mine.wait()

    return pl.pallas_call(
        body, name=name, in_specs=[ANY], out_specs=ANY,
        out_shape=jax.ShapeDtypeStruct((nslots,) + arr.shape, arr.dtype),
        scratch_shapes=[pltpu.SemaphoreType.DMA((len(masks),)), pltpu.SemaphoreType.DMA((len(masks),)),
                        pltpu.SemaphoreType.DMA],
    )(arr)


def _gather_weights(name, arrs):
    na = len(arrs)

    def body(*refs):
        mine, first, passed = _gather_copies(refs[:na], refs[na:2 * na], *refs[2 * na:])
        _gather_start(mine, first)
        _gather_forward(first, passed)
        _gather_finish(mine, first, passed)

    return pl.pallas_call(
        body, name=name, in_specs=[ANY] * na, out_specs=[ANY] * na,
        out_shape=[jax.ShapeDtypeStruct((4,) + a.shape, a.dtype) for a in arrs], scratch_shapes=_gather_sems(arrs),
    )(*arrs)


def _gather_plan(shape, dtype):
    big = (shape[0] // 2) * shape[1] * 2 >= (2 << 20)
    return _chunks(shape[0] // 2, dtype, 3 if big else 1), _chunks(shape[0], dtype, 4 if big else 1)


def _gather_sems(arrs):
    plans = [_gather_plan(a.shape, a.dtype) for a in arrs]
    nremote = sum(len(CHIP_MASKS) * p[0][0] for p in plans)
    nlocal = sum(p[1][0] for p in plans)
    return [pltpu.SemaphoreType.DMA((nremote,))] * 4 + [pltpu.SemaphoreType.DMA((nlocal,))]


def _gather_copies(w_refs, o_refs, sa, ra, sb, rb, local_sems):
    x, y, c = _coords()
    chip = 2 * x + y
    mine, first, passed = [], [], []
    for w_ref, o_ref in zip(w_refs, o_refs):
        (nch, rows), (nloc, lrows) = _gather_plan(w_ref.shape, w_ref.dtype)
        rh = w_ref.shape[0] // 2
        for k in range(nloc):
            part = pl.ds(k * lrows, lrows)
            mine.append(pltpu.make_async_copy(w_ref.at[part], o_ref.at[chip, part], local_sems.at[len(mine)]))
        for mask in CHIP_MASKS:
            px, py, _ = _peer(x, y, c, mask)
            for ch in range(nch):
                k = len(first)
                part = pl.ds(c * rh + ch * rows, rows)
                first.append(pltpu.make_async_remote_copy(
                    src_ref=w_ref.at[part], dst_ref=o_ref.at[chip, part], send_sem=sa.at[k], recv_sem=ra.at[k],
                    device_id=(px, py, c), device_id_type=MESH))
                landed = o_ref.at[2 * px + py, part]
                passed.append(pltpu.make_async_remote_copy(
                    src_ref=landed, dst_ref=landed, send_sem=sb.at[k], recv_sem=rb.at[k],
                    device_id=(x, y, 1 - c), device_id_type=MESH))
    return mine, first, passed


def _gather_start(mine, first):
    for cp in mine + first:
        cp.start()


def _gather_forward(first, passed):
    for a, b in zip(first, passed):
        a.wait_recv()
        b.start()


def _gather_finish(mine, first, passed):
    for a, b in zip(first, passed):
        a.wait_send()
        b.wait()
    for cp in mine:
        cp.wait()


def _swap_halves(name, arrs):
    na = len(arrs)
    plan = []
    for a in arrs:
        h = a.shape[1] // 2
        nch, rows = _chunks(h, a.dtype, 2 if h * a.shape[2] * 2 >= (1 << 20) else 1)
        plan.append([(s, ch * rows, rows) for s in range(a.shape[0]) for ch in range(nch)])
    ncopies = sum(len(p) for p in plan)

    def body(*refs):
        a_refs, got_refs = refs[:na], refs[na:2 * na]
        send_sems, recv_sems = refs[2 * na:]
        x, y, c = _coords()
        cps = []
        for ai in range(na):
            h = arrs[ai].shape[1] // 2
            for (s, r0, rows) in plan[ai]:
                k = len(cps)
                cp = pltpu.make_async_remote_copy(
                    src_ref=a_refs[ai].at[s, pl.ds((1 - c) * h + r0, rows)], dst_ref=got_refs[ai].at[s, pl.ds(r0, rows)],
                    send_sem=send_sems.at[k], recv_sem=recv_sems.at[k], device_id=(x, y, 1 - c), device_id_type=MESH)
                cp.start()
                cps.append(cp)
        for cp in cps:
            cp.wait()

    halves = [jax.ShapeDtypeStruct((a.shape[0], a.shape[1] // 2, a.shape[2]), a.dtype) for a in arrs]
    return pl.pallas_call(
        body, name=name, in_specs=[ANY] * na, out_specs=[ANY] * na, out_shape=halves,
        scratch_shapes=[pltpu.SemaphoreType.DMA((ncopies,))] * 2,
    )(*arrs)


def _scatter_shapes(arrs):
    return [jax.ShapeDtypeStruct((len(CHIP_MASKS),) + a.shape[1:], a.dtype) for a in arrs]


def _scatter_sems(arrs):
    return [pltpu.SemaphoreType.DMA((len(arrs), len(CHIP_MASKS)))] * 2


def _scatter_copies(a_refs, o_refs, send_sems, recv_sems):
    x, y, c = _coords()
    cps = []
    for k in range(len(a_refs)):
        for j, mask in enumerate(CHIP_MASKS):
            px, py, pc = _peer(x, y, c, mask)
            cps.append(pltpu.make_async_remote_copy(
                src_ref=a_refs[k].at[2 * px + py], dst_ref=o_refs[k].at[j],
                send_sem=send_sems.at[k, j], recv_sem=recv_sems.at[k, j],
                device_id=(px, py, pc), device_id_type=MESH))
    return cps


def _join_halves(name, bufs, small):
    na = len(bufs)
    nm = len(ALL_MASKS)

    def body(*refs):
        small_ref, o_refs, all_ref = refs[na], refs[na + 1:2 * na + 1], refs[2 * na + 1]
        send_sems, recv_sems, gs_sems, gr_sems, local_sem = refs[2 * na + 2:]
        x, y, c = _coords()
        cps = []
        for k in range(na):
            for l in range(2):
                part = o_refs[k].at[l, c]
                cps.append(pltpu.make_async_remote_copy(
                    src_ref=part, dst_ref=part, send_sem=send_sems.at[k, l], recv_sem=recv_sems.at[k, l],
                    device_id=(x, y, 1 - c), device_id_type=MESH))
        slot = all_ref.at[4 * x + 2 * y + c]
        for k, mask in enumerate(ALL_MASKS):
            cps.append(pltpu.make_async_remote_copy(
                src_ref=small_ref, dst_ref=slot, send_sem=gs_sems.at[k], recv_sem=gr_sems.at[k],
                device_id=_peer(x, y, c, mask), device_id_type=MESH))
        mine = pltpu.make_async_copy(small_ref, slot, local_sem)
        mine.start()
        for cp in cps:
            cp.start()
        for cp in cps:
            cp.wait()
        mine.wait()

    return pl.pallas_call(
        body, name=name, in_specs=[ANY] * (na + 1), out_specs=[ANY] * (na + 1),
        out_shape=[jax.ShapeDtypeStruct(b.shape, b.dtype) for b in bufs]
        + [jax.ShapeDtypeStruct((8,) + small.shape, small.dtype)],
        input_output_aliases={k: k for k in range(na)},
        scratch_shapes=[pltpu.SemaphoreType.DMA((na, 2))] * 2 + [pltpu.SemaphoreType.DMA((nm,))] * 2
        + [pltpu.SemaphoreType.DMA],
    )(*bufs, small)


BIG = ("w_in", "w_uq", "w_ukv", "w_out", "w_ffn_in", "w_ffn_out")
SMALL = ("conv_w", "conv_b", "lru_wa", "lru_ba", "lru_wx", "lru_bx", "lru_a_param", "q_norm_g", "kv_norm_g",
         "final_norm_g")
SMALL_SHAPES = {"conv_w": (2, CONV, D), "conv_b": (2, D), "lru_wa": (2, NB, BW, BW), "lru_ba": (2, NB, BW),
                "lru_wx": (2, NB, BW, BW), "lru_bx": (2, NB, BW), "lru_a_param": (2, D), "q_norm_g": (2, QR),
                "kv_norm_g": (2, KVR), "final_norm_g": (D,)}
SMALL_ROWS = 640
FFN_CHIP_ORDER = (0, 2, 1, 3)


def _size(shape):
    n = 1
    for s in shape:
        n *= s
    return n


def _pack_rows(parts, rows, dtype):
    flat = jnp.concatenate([p.reshape(-1).astype(dtype) for p in parts])
    pad = rows * 1024 - flat.shape[0]
    if pad:
        flat = jnp.concatenate([flat, jnp.zeros((pad,), dtype)])
    return flat.reshape(rows, 1024)


def _unpack_rows(pack, shapes):
    flat = pack.reshape(-1)
    out, off = [], 0
    for s in shapes:
        n = _size(s)
        out.append(flat[off:off + n].reshape(s))
        off += n
    return out


def _inproj_weights(g_in, g_uq, g_ukv):
    w_in = jnp.concatenate([g_in[s] for s in range(4)], axis=1)
    win_p = jnp.concatenate([w_in[:, :D], w_in[:, 1472:2496], w_in[:, 2496:], w_in[:, D:1472],
                             jnp.zeros((D, 64), w_in.dtype)], axis=1)
    w_uq = g_uq.reshape(QR, NH, DN + DR)
    wq_p = jnp.concatenate([w_uq, jnp.zeros((QR, NH, DH - DN - DR), w_uq.dtype)], axis=2).reshape(QR, NH * DH)
    w_ukv = jnp.transpose(g_ukv.reshape(4, KVR, NH, (DN + DV) // 4), (1, 2, 0, 3)).reshape(KVR, NH, DN + DV)
    wk_p = jnp.concatenate([w_ukv[..., :DN], jnp.zeros((KVR, NH, DH - DN), w_ukv.dtype)], axis=2).reshape(KVR, NH * DH)
    wv = w_ukv[..., DN:].reshape(KVR, NH * DV)
    return dict(win=win_p, wq=wq_p, wk=wk_p, wv=wv)


def _other_weights(g_out, g_fi, g_fo):
    ko, ki, kf = g_out.shape[1] // 2, g_fi.shape[1] // 2, g_fo.shape[1] // 2
    wout = [g_out[:, l * ko:(l + 1) * ko].reshape(4 * ko, D) for l in range(2)]
    wfi = [jnp.concatenate([g_fi[s, l * ki:(l + 1) * ki] for s in FFN_CHIP_ORDER], axis=1) for l in range(2)]
    wfo = [g_fo[:, l * kf:(l + 1) * kf].reshape(4 * kf, D) for l in range(2)]
    return dict(wout=wout, wfi=wfi, wfo=wfo)


def _shard_grads_inproj(dwin, dwq, dwk, dwv):
    w_in = jnp.concatenate([dwin[:, :D], dwin[:, QKV0:QKV0 + 448], dwin[:, D:QKV0]], axis=1)
    w_in = jnp.transpose(w_in.reshape(D, 4, DIN // 4), (1, 0, 2))
    w_uq = dwq.reshape(QR, NH, DH)[..., :DN + DR].reshape(4, QR // 4, NH * (DN + DR))
    w_ukv = jnp.concatenate([dwk.reshape(KVR, NH, DH)[..., :DN], dwv.reshape(KVR, NH, DV)], axis=2)
    w_ukv = jnp.transpose(w_ukv.reshape(KVR, NH, 4, (DN + DV) // 4), (2, 0, 1, 3)).reshape(4, KVR, NH * (DN + DV) // 4)
    return [w_in, w_uq, w_ukv]


def _shard_grads_rest(dwout, dwfi, dwfo):
    tiles = jnp.split(dwfi, 4, axis=1)
    wfi = jnp.stack([tiles[FFN_CHIP_ORDER.index(s)] for s in range(4)])
    return [dwout.reshape(4, D // 4, D), wfi, dwfo.reshape(4, DFF // 4, D)]


def kernel(x, c, positions, w_ada, b_ada, w_in, conv_w, conv_b, lru_wa, lru_ba, lru_wx, lru_bx, lru_a_param, q_norm_g, kv_norm_g, w_uq, w_ukv, w_out, w_ffn_in, w_ffn_out, final_norm_g, loss_target, m_w_ada, m_b_ada, m_w_in, m_conv_w, m_conv_b, m_lru_wa, m_lru_ba, m_lru_wx, m_lru_bx, m_lru_a_param, m_q_norm_g, m_kv_norm_g, m_w_uq, m_w_ukv, m_w_out, m_w_ffn_in, m_w_ffn_out, m_final_norm_g, v_w_ada, v_b_ada, v_w_in, v_conv_w, v_conv_b, v_lru_wa, v_lru_ba, v_lru_wx, v_lru_bx, v_lru_a_param, v_q_norm_g, v_kv_norm_g, v_w_uq, v_w_ukv, v_w_out, v_w_ffn_in, v_w_ffn_out, v_final_norm_g):
    T = x.shape[1]
    tb = min(ROW_BLOCK, T)
    tq = min(ATTN_BLOCK, T)
    nq = T // tq
    mx, my, mc = _coords()
    chip = 2 * mx + my
    dev = 4 * mx + 2 * my + mc
    weights = dict(w_ada=w_ada, b_ada=b_ada, w_in=w_in, conv_w=conv_w, conv_b=conv_b, lru_wa=lru_wa, lru_ba=lru_ba,
                   lru_wx=lru_wx, lru_bx=lru_bx, lru_a_param=lru_a_param, q_norm_g=q_norm_g, kv_norm_g=kv_norm_g,
                   w_uq=w_uq, w_ukv=w_ukv, w_out=w_out, w_ffn_in=w_ffn_in, w_ffn_out=w_ffn_out,
                   final_norm_g=final_norm_g)
    mom = dict(w_ada=m_w_ada, b_ada=m_b_ada, w_in=m_w_in, conv_w=m_conv_w, conv_b=m_conv_b, lru_wa=m_lru_wa,
               lru_ba=m_lru_ba, lru_wx=m_lru_wx, lru_bx=m_lru_bx, lru_a_param=m_lru_a_param, q_norm_g=m_q_norm_g,
               kv_norm_g=m_kv_norm_g, w_uq=m_w_uq, w_ukv=m_w_ukv, w_out=m_w_out, w_ffn_in=m_w_ffn_in,
               w_ffn_out=m_w_ffn_out, final_norm_g=m_final_norm_g)
    var = dict(w_ada=v_w_ada, b_ada=v_b_ada, w_in=v_w_in, conv_w=v_conv_w, conv_b=v_conv_b, lru_wa=v_lru_wa,
               lru_ba=v_lru_ba, lru_wx=v_lru_wx, lru_bx=v_lru_bx, lru_a_param=v_lru_a_param, q_norm_g=v_q_norm_g,
               kv_norm_g=v_kv_norm_g, w_uq=v_w_uq, w_ukv=v_w_ukv, w_out=v_w_out, w_ffn_in=v_w_ffn_in,
               w_ffn_out=v_w_ffn_out, final_norm_g=v_final_norm_g)
    order = list(weights)

    ns_ada = w_ada.shape[2]
    c_all = _all_gather("gather_c", c, ALL_MASKS, True).reshape(8, D)
    b_sh = lax.dynamic_slice_in_dim(b_ada, chip * ns_ada, ns_ada, axis=1).reshape(2, 1, ns_ada)
    mod_sh = _ada_fwd("ada_fwd", c_all, w_ada, b_sh)
    mod_all = _all_gather("gather_mod", mod_sh.reshape(16, ns_ada), CHIP_MASKS, False)
    mod_mine = lax.dynamic_index_in_dim(mod_all.reshape(4, 2, 8, ns_ada), dev, axis=2, keepdims=False)
    mod = jnp.transpose(mod_mine, (1, 0, 2)).reshape(2, 6, 1, D)

    flat = lambda w: w.astype(bf16).reshape(w.shape[0], w.shape[1], -1)
    b_in, b_uq, b_ukv = flat(w_in), flat(w_uq), flat(w_ukv)
    late = [b_in[1], b_uq[1], b_ukv[1]] + [flat(w).reshape(-1, w.shape[-1]) for w in (w_out, w_ffn_in, w_ffn_out)]
    W0 = _inproj_weights(*_gather_weights("gather_weights_early", [b_in[0], b_uq[0], b_ukv[0]]))
    W = {n: [W0[n], None] for n in W0}
    wa_b = lru_wa.astype(bf16)
    wx_b = lru_wx.astype(bf16)

    pos = positions.reshape(T, 1)
    invf = ROPE_THETA ** (-jnp.arange(0, DR, 2, dtype=f32) / DR)
    invf_p = jnp.concatenate([jnp.zeros((DN,), f32), invf, invf, jnp.zeros((DH - DN - DR,), f32)]).reshape(1, DH)
    cosp, sap, sbp = _rope_tables("rope_tables", pos, invf_p, T, tb)

    ns_cw = conv_w.shape[2]
    cw_all = _all_gather("gather_conv_w", conv_w.reshape(2 * CONV, ns_cw), CHIP_MASKS, False)
    cw_full = jnp.transpose(cw_all, (1, 0, 2)).reshape(2, CONV, D)
    vec = lambda a, l: a[l].reshape(1, -1)

    xl = x.reshape(T, D)
    saved = []
    h1 = _norm_mod("norm_mod_0", xl, mod[0, 1], mod[0, 0], T, tb)
    for l in range(2):
        sh_m, sc_m, g_m, sh_f, sc_f, g_f = [mod[l, j] for j in range(6)]
        proj = _matmul(f"proj_{l}", h1, W["win"][l], "nn", bf16)
        u, r, ig, ya = _lru_fwd(f"lru_fwd_{l}", proj, pos, cw_full[l], vec(conv_b, l), wa_b[l], vec(lru_ba, l),
                                wx_b[l], vec(lru_bx, l), vec(lru_a_param, l), T, tb)
        qp, kp, vp, cq, ckv = _mla_prep(f"mla_prep_{l}", proj, cosp, sap, sbp, vec(q_norm_g, l), vec(kv_norm_g, l),
                                        W["wq"][l], W["wk"][l], W["wv"][l], T, tb)
        if l == 0:
            o, lse, *late_all = _flash_fwd("flash_fwd_0", qp, kp, vp, T, tq, late)
            W1 = _inproj_weights(*late_all[:3])
            for n in W1:
                W[n][1] = W1[n]
            W.update(_other_weights(*late_all[3:]))
        else:
            o, lse = _flash_fwd("flash_fwd_1", qp, kp, vp, T, tq)
        y = _gate_mix(f"gate_mix_{l}", proj, ya, o, T, tb)
        mo = _matmul(f"out_proj_{l}", y, W["wout"][l], "nn")
        x2, h2 = _resid_norm_mod(f"resid_norm_f_{l}", xl, g_m, mo, sc_f, sh_f, T, tb)
        gu, act = _ffn_in_swiglu(f"ffn_in_{l}", h2, W["wfi"][l], T)
        fo = _matmul(f"ffn_out_{l}", act, W["wfo"][l], "nn")
        saved.append(dict(x=xl, h1=h1, proj=proj, u=u, r=r, ig=ig, ya=ya, qp=qp, kp=kp, vp=vp, cq=cq, ckv=ckv, o=o,
                          lse=lse, y=y, mo=mo, x2=x2, h2=h2, gu=gu, act=act, fo=fo))
        if l == 0:
            xl, h1 = _resid_norm_mod("resid_norm_m_1", x2, g_f, fo, mod[1, 1], mod[1, 0], T, tb)

    dx, dfo, loss_part, dgfin, dg_f = _final("final", saved[1]["x2"], mod[1, 5], saved[1]["fo"],
                                             final_norm_g.reshape(1, D), loss_target.reshape(T, D), T, tb)
    loss = lax.psum(loss_part[0, 0], ("x", "y", "c"))

    gl = {n: [None, None] for n in ("win", "wq", "wk", "wv", "wout", "wfi", "wfo", "conv_w", "conv_b", "lru_wa",
                                    "lru_ba", "lru_wx", "lru_bx", "lru_a_param", "q_norm_g", "kv_norm_g")}
    dmod = [None, None]

    def pair_sums(tag, arrs):
        return list(_pair_sum(f"rs_pair_sum_{tag}", arrs, _swap_halves(f"rs_swap_halves_{tag}", arrs)))

    def inproj_pieces(l):
        return _shard_grads_inproj(gl["win"][l], gl["wq"][l], gl["wk"][l], gl["wv"][l])

    def rest_pieces(l):
        return _shard_grads_rest(gl["wout"][l], gl["wfi"][l], gl["wfo"][l])

    for l in (1, 0):
        s = saved[l]
        sh_m, sc_m, g_m, sh_f, sc_f, g_f = [mod[l, j] for j in range(6)]
        dgu = _dact_swiglu_bwd(f"d_act_{l}", dfo, W["wfo"][l], s["gu"], T)
        gl["wfo"][l] = _matmul(f"dw_ffn_out_{l}", s["act"], dfo, "tn", bf16)
        dh2 = _matmul(f"d_h2_{l}", dgu, W["wfi"][l], "nt")
        gl["wfi"][l] = _matmul(f"dw_ffn_in_{l}", s["h2"], dgu, "tn", bf16)
        dx2, dmo, dsc_f, dsh_f, dg_m = _norm_mod_bwd(f"norm_bwd_f_{l}", dh2, s["x2"], sc_f, dx, T, tb, (g_m, s["mo"]))
        dy = _matmul(f"d_y_{l}", dmo, W["wout"][l], "nt")
        gl["wout"][l] = _matmul(f"dw_out_{l}", s["y"], dmo, "tn", bf16)
        dga, dgb, dya, do, dlt = _gate_mix_bwd(f"gate_mix_bwd_{l}", dy, s["proj"], s["ya"], s["o"], T, tb)
        lse_r = s["lse"].reshape(NH, nq, 1, tq)
        dl_r = jnp.transpose(dlt).reshape(NH, nq, 1, tq)
        if l == 0:
            sums0a = pair_sums("0a", rest_pieces(0))
            dk, dv, dq, *lands = _flash_bwd("flash_bwd_0", s["qp"], s["kp"], s["vp"], do, lse_r, dl_r, T, tq,
                                            sums1 + sums0a)
            rs_bufs = list(_chip_sum("rs_chip_sum_1", sums1, lands[:6], 1))
            rs_bufs[3:] = _chip_sum("rs_chip_sum_0a", sums0a, lands[6:], 0, rs_bufs[3:])
        else:
            dk, dv, dq = _flash_bwd("flash_bwd_1", s["qp"], s["kp"], s["vp"], do, lse_r, dl_r, T, tq)
        dqkv, dqp, dkp, dgq, dgkv = _mla_bwd(f"mla_bwd_{l}", dq, dk, dv, s["proj"], cosp, sap, sbp,
                                              vec(q_norm_g, l), vec(kv_norm_g, l), W["wq"][l], W["wk"][l], W["wv"][l],
                                              T, tb)
        gl["wq"][l] = _matmul(f"dw_uq_{l}", s["cq"], dqp, "tn", bf16)
        gl["wk"][l] = _matmul(f"dw_uk_{l}", s["ckv"], dkp, "tn", bf16)
        gl["wv"][l] = _matmul(f"dw_uv_{l}", s["ckv"], dv, "tn", bf16)
        dxl, dcw, dcb, dwa, dba, dwx, dbx, dsp = _lru_bwd(
            f"lru_bwd_{l}", dya, s["proj"], s["u"], s["r"], s["ig"], s["ya"], pos, cw_full[l], wa_b[l], wx_b[l],
            vec(lru_a_param, l), T, tb)
        dproj = jnp.concatenate([dxl, dga, dgb, dqkv], axis=1)
        gl["win"][l] = _matmul(f"dw_in_{l}", s["h1"], dproj, "tn", bf16)
        gl["conv_w"][l], gl["conv_b"][l] = dcw, dcb[0]
        gl["lru_wa"][l], gl["lru_ba"][l], gl["lru_wx"][l], gl["lru_bx"][l] = dwa, dba[0], dwx, dbx[0]
        gl["lru_a_param"][l] = dsp[0]
        gl["q_norm_g"][l], gl["kv_norm_g"][l] = dgq[0], dgkv[0]
        if l == 1:
            dh1 = _matmul("d_h1_1", dproj, W["win"][l], "nt")
            dx, dfo_below, dsc_m, dsh_m, dg_f_below = _norm_mod_bwd(
                "norm_bwd_m_1", dh1, s["x"], sc_m, dx2, T, tb, (mod[0, 5], saved[0]["fo"]))
            sums1 = pair_sums("1", inproj_pieces(1) + rest_pieces(1))
        else:
            gfull = {n: jnp.stack(gl[n]) for n in SMALL if n != "final_norm_g"}
            gfull["final_norm_g"] = dgfin[0]
            gsmall = _pack_rows([gfull[n] for n in SMALL], SMALL_ROWS, f32).reshape(4, SMALL_ROWS // 4, 1024)
            sums0b = pair_sums("0b", inproj_pieces(0) + [gsmall])
            dh1, *lands0b = _matmul("d_h1_0", dproj, W["win"][l], "nt", scatter=sums0b)
            dx, dsc_m, dsh_m = _norm_mod_bwd("norm_bwd_m_0", dh1, s["x"], sc_m, dx2, T, tb)
        dmod[l] = jnp.concatenate([dsh_m, dsc_m, dg_m, dsh_f, dsc_f, dg_f], axis=1)
        if l == 1:
            dfo, dg_f = dfo_below, dg_f_below
    grad_x = dx.reshape(1, T, D)

    dmod_all = _all_gather("gather_dmod", jnp.concatenate(dmod, axis=1), ALL_MASKS, True).reshape(8, 2, 6, D)
    dmod_sum = _sum_slots("sum_dmod", dmod_all.reshape(8, 12, D)).reshape(2, 6 * D)
    dm_sh = lax.dynamic_slice_in_dim(jnp.transpose(dmod_all.reshape(8, 2, 6 * D), (1, 0, 2)), chip * ns_ada, ns_ada, axis=2)
    grads = {"w_ada": _ada_bwd("ada_bwd", jnp.transpose(c_all), dm_sh), "b_ada": dmod_sum}

    rs_bufs[:3] = _chip_sum("rs_chip_sum_0b", sums0b[:3], lands0b[:3], 0, rs_bufs[:3])
    tot_s = _chip_sum("rs_chip_sum_small", sums0b[3:], lands0b[3:], None)[0]
    *bufs, small_all = _join_halves("rs_join_halves", rs_bufs, tot_s)
    small_all = small_all.reshape(SMALL_ROWS, 1024)
    for n, buf in zip(BIG, bufs):
        grads[n] = buf.reshape(weights[n].shape)
    for n, gs in zip(SMALL, _unpack_rows(small_all, [SMALL_SHAPES[n] for n in SMALL])):
        grads[n] = gs
    grads["conv_w"] = lax.dynamic_slice_in_dim(grads["conv_w"], chip * ns_cw, ns_cw, axis=2)

    deltas, new_m, new_v = {}, {}, {}
    for n in order:
        w = weights[n]
        view = (1, w.shape[0]) if w.ndim == 1 else (_size(w.shape[:-1]), w.shape[-1]) if w.shape[-1] >= 128 else (w.shape[0], _size(w.shape[1:]))
        d, m2, v2 = _adamw(f"adamw_{n}", w.reshape(view), grads[n].reshape(view), mom[n].reshape(view), var[n].reshape(view))
        deltas[n], new_m[n], new_v[n] = d.reshape(w.shape), m2.reshape(w.shape), v2.reshape(w.shape)
    return (loss, grad_x, *[grads[n] for n in order], *[deltas[n] for n in order], *[new_m[n] for n in order],
            *[new_v[n] for n in order])
```

```python
import functools

import jax
import jax.numpy as jnp
from jax import lax
from jax.experimental import pallas as pl
from jax.experimental.pallas import tpu as pltpu

f32 = jnp.float32
bf16 = jnp.bfloat16

D = 1024
NH = 8
DN = 128
DR = 64
DV = 128
DH = 256
QR = 256
KVR = 128
DFF = 2816
FH = DFF // 2
FFN_SPLIT = 768
NB = 8
BW = 128
CONV = 4
PW = 3584
QKV0 = 3 * D
DIN = 3520
EPS = 1e-6
LRU_C = 8.0
ROPE_THETA = 10000.0
SCALE = (DN + DR) ** -0.5
LOG2E = 1.4426950408889634
LN2 = 0.6931471805599453
QSCALE = SCALE * LOG2E
NEG = -1e30
HPS = 2
HPS_BWD = 2

ADAM_LR = 0.001
ADAM_B1 = 0.9
ADAM_B2 = 0.999
ADAM_EPS = 1e-08
ADAM_WD = 0.01
ADAM_STEP = 10

VMEM_LIMIT = 56 * 1024 * 1024
ROW_BLOCK = 256
ATTN_BLOCK = 1024
MESH = pl.DeviceIdType.MESH
ANY = pl.BlockSpec(memory_space=pl.ANY)
CHIP_MASKS = ((1, 0, 0), (0, 1, 0), (1, 1, 0))
ALL_MASKS = ((0, 0, 1), (1, 0, 0), (0, 1, 0), (1, 1, 0), (1, 0, 1), (0, 1, 1), (1, 1, 1))


def _cparams(n_axes):
    return pltpu.CompilerParams(dimension_semantics=("arbitrary",) * n_axes, vmem_limit_bytes=VMEM_LIMIT)


def _pick(n, cands):
    for c in cands:
        if n % c == 0:
            return c
    return n


def _rowwise(name, body, T, tb, row_ins, vec_ins, row_outs, acc_outs=(), scratch=(), reverse=False):
    n = T // tb

    def rmap(i):
        return (n - 1 - i) if reverse else i

    in_specs = []
    for (_, w, cb) in row_ins:
        in_specs.append(pl.BlockSpec((tb, w), functools.partial(lambda i, cb: (rmap(i), cb), cb=cb)))
    for v in vec_ins:
        in_specs.append(pl.BlockSpec(v.shape, functools.partial(lambda i, nd: (0,) * nd, nd=v.ndim)))
    out_specs, out_shape = [], []
    for (w, dt) in row_outs:
        out_specs.append(pl.BlockSpec((tb, w), lambda i: (rmap(i), 0)))
        out_shape.append(jax.ShapeDtypeStruct((T, w), dt))
    for (s, dt) in acc_outs:
        out_specs.append(pl.BlockSpec(s, functools.partial(lambda i, nd: (0,) * nd, nd=len(s))))
        out_shape.append(jax.ShapeDtypeStruct(s, dt))

    def kern(*refs):
        body(*refs)

    return pl.pallas_call(
        kern, name=name, grid=(n,), in_specs=in_specs, out_specs=out_specs, out_shape=out_shape,
        scratch_shapes=list(scratch), compiler_params=_cparams(1),
    )(*[a for (a, _, _) in row_ins], *vec_ins)


def _matmul(name, a, b, mode, out_dtype=f32, scatter=None):
    if mode == "nn":
        (M, K), N = a.shape, b.shape[1]
    elif mode == "nt":
        (M, K), N = a.shape, b.shape[0]
    else:
        (K, M), N = a.shape, b.shape[1]
    wide = (1792, 1408, 1024, 512, 256, 128)
    if mode == "tn":
        tm = _pick(M, (1024, 1408, 512, 256, 128))
    else:
        tm = _pick(M, (1024, 512, 256, 128) if K <= 3584 else (512, 256, 128))
    tn = _pick(N, wide)
    tk = K if mode != "tn" else _pick(K, (2048, 1024, 512, 256, 128))
    nk = K // tk
    dims = {"nn": (((1,), (0,)), ((), ())), "nt": (((1,), (1,)), ((), ())), "tn": (((0,), (0,)), ((), ()))}[mode]
    if mode == "tn":
        a_spec = pl.BlockSpec((tk, tm), lambda j, i, k: (k, i))
    else:
        a_spec = pl.BlockSpec((tm, tk), lambda j, i, k: (i, k))
    once = {"pipeline_mode": pl.Buffered(1)} if (mode != "tn" and N == tn) else {}
    if mode == "nt":
        b_spec = pl.BlockSpec((tn, tk), lambda j, i, k: (j, k), **once)
    else:
        b_spec = pl.BlockSpec((tk, tn), lambda j, i, k: (k, j), **once)
    o_spec = pl.BlockSpec((tm, tn), lambda j, i, k: (i, j))
    use_acc = nk > 1 and out_dtype != f32

    def kern(a_ref, b_ref, o_ref, *scr):
        k = pl.program_id(2)
        d = lax.dot_general(a_ref[...].astype(bf16), b_ref[...].astype(bf16), dims, preferred_element_type=f32)
        if nk == 1:
            o_ref[...] = d.astype(out_dtype)
        else:
            acc = scr[0] if use_acc else o_ref

            @pl.when(k == 0)
            def _():
                acc[...] = d

            @pl.when(k > 0)
            def _():
                acc[...] += d

            if use_acc:
                @pl.when(k == nk - 1)
                def _():
                    o_ref[...] = acc[...].astype(out_dtype)

    grid = (N // tn, M // tm, nk)
    scratch = [pltpu.VMEM((tm, tn), f32)] if use_acc else []
    if scatter is None:
        return pl.pallas_call(
            kern, name=name, grid=grid, in_specs=[a_spec, b_spec], out_specs=o_spec,
            out_shape=jax.ShapeDtypeStruct((M, N), out_dtype), scratch_shapes=scratch, compiler_params=_cparams(3),
        )(a, b)

    ns = len(scatter)

    def kern_with_scatter(*refs):
        a_ref, b_ref, s_in, o_ref, s_out = refs[0], refs[1], refs[2:2 + ns], refs[2 + ns], refs[3 + ns:3 + 2 * ns]
        scr, sems = refs[3 + 2 * ns:3 + 2 * ns + len(scratch)], refs[3 + 2 * ns + len(scratch):]
        at = [pl.program_id(d) for d in range(3)]

        @pl.when((at[0] == 0) & (at[1] == 0) & (at[2] == 0))
        def _():
            for cp in _scatter_copies(s_in, s_out, *sems):
                cp.start()

        kern(a_ref, b_ref, o_ref, *scr)

        @pl.when((at[0] == grid[0] - 1) & (at[1] == grid[1] - 1) & (at[2] == grid[2] - 1))
        def _():
            for cp in _scatter_copies(s_in, s_out, *sems):
                cp.wait()

    return pl.pallas_call(
        kern_with_scatter, name=name, grid=grid, in_specs=[a_spec, b_spec] + [ANY] * ns,
        out_specs=[o_spec] + [ANY] * ns,
        out_shape=[jax.ShapeDtypeStruct((M, N), out_dtype)] + _scatter_shapes(scatter),
        scratch_shapes=scratch + _scatter_sems(scatter), compiler_params=_cparams(3),
    )(a, b, *scatter)


def _colsum(v):
    return jnp.sum(v, axis=0, keepdims=True)


def _init_acc(step, *refs):
    @pl.when(step == 0)
    def _():
        for r in refs:
            r[...] = jnp.zeros_like(r)


def _norm_mod(name, x, sc, sh, T, tb):
    def body(x_ref, sc_ref, sh_ref, h_ref):
        xv = x_ref[...]
        rstd = lax.rsqrt(jnp.mean(xv * xv, axis=-1, keepdims=True) + EPS)
        h_ref[...] = (xv * rstd * (1.0 + sc_ref[...]) + sh_ref[...]).astype(bf16)

    return _rowwise(name, body, T, tb, [(x, D, 0)], [sc, sh], [(D, bf16)])[0]


def _resid_norm_mod(name, x, g, m, sc, sh, T, tb):
    def body(x_ref, m_ref, g_ref, sc_ref, sh_ref, x2_ref, h_ref):
        xv = x_ref[...] + g_ref[...] * m_ref[...]
        x2_ref[...] = xv
        rstd = lax.rsqrt(jnp.mean(xv * xv, axis=-1, keepdims=True) + EPS)
        h_ref[...] = (xv * rstd * (1.0 + sc_ref[...]) + sh_ref[...]).astype(bf16)

    return _rowwise(name, body, T, tb, [(x, D, 0), (m, D, 0)], [g, sc, sh], [(D, f32), (D, bf16)])


def _norm_mod_bwd(name, dh, x, sc, dres, T, tb, resid=None):
    def norm_part(dh_ref, x_ref, dr_ref, sc_ref, dx_ref, dsc_ref, dsh_ref):
        xv = x_ref[...]
        dhv = dh_ref[...]
        rstd = lax.rsqrt(jnp.mean(xv * xv, axis=-1, keepdims=True) + EPS)
        nv = xv * rstd
        dn = dhv * (1.0 + sc_ref[...])
        dxv = dr_ref[...] + rstd * (dn - nv * jnp.mean(dn * nv, axis=-1, keepdims=True))
        dx_ref[...] = dxv
        dsc_ref[...] += _colsum(dhv * nv)
        dsh_ref[...] += _colsum(dhv)
        return dxv

    if resid is None:
        def body(dh_ref, x_ref, dr_ref, sc_ref, dx_ref, dsc_ref, dsh_ref):
            _init_acc(pl.program_id(0), dsc_ref, dsh_ref)
            norm_part(dh_ref, x_ref, dr_ref, sc_ref, dx_ref, dsc_ref, dsh_ref)

        return _rowwise(name, body, T, tb, [(dh, D, 0), (x, D, 0), (dres, D, 0)], [sc], [(D, f32)],
                        [((1, D), f32)] * 2)

    def body_resid(dh_ref, x_ref, dr_ref, m_ref, sc_ref, g_ref, dx_ref, dm_ref, dsc_ref, dsh_ref, dg_ref):
        _init_acc(pl.program_id(0), dsc_ref, dsh_ref, dg_ref)
        dxv = norm_part(dh_ref, x_ref, dr_ref, sc_ref, dx_ref, dsc_ref, dsh_ref)
        dm_ref[...] = (dxv * g_ref[...]).astype(bf16)
        dg_ref[...] += _colsum(dxv * m_ref[...])

    g, m = resid
    return _rowwise(name, body_resid, T, tb, [(dh, D, 0), (x, D, 0), (dres, D, 0), (m, D, 0)], [sc, g],
                    [(D, f32), (D, bf16)], [((1, D), f32)] * 3)


def _expm1_neg(y):
    poly = y * (1.0 + y * (0.5 + y * (1.0 / 6 + y * (1.0 / 24))))
    return jnp.where(y > -0.05, poly, jnp.exp(y) - 1.0)


def _lru_gates(r, sp, reset):
    log_a = -LRU_C * r * sp
    a = jnp.where(reset, 0.0, jnp.exp(log_a))
    mult = jnp.where(reset, 1.0, jnp.sqrt(-_expm1_neg(2.0 * log_a)))
    return a, mult


def _block_dot(v, w_ref, dims):
    outs = [lax.dot_general(v[:, n * BW:(n + 1) * BW], w_ref[n], dims, preferred_element_type=f32) for n in range(NB)]
    return jnp.concatenate(outs, axis=1)


def _lru_fwd(name, proj, pos, cw, cb, wa, ba, wx, bx, ap, T, tb):
    def body(x_ref, pos_ref, cw_ref, cb_ref, wa_ref, ba_ref, wx_ref, bx_ref, ap_ref,
             u_ref, r_ref, i_ref, h_ref, prev_x, carry_h):
        _init_acc(pl.program_id(0), prev_x, carry_h)
        x = x_ref[...].astype(f32)
        px = prev_x[...]
        row = lax.broadcasted_iota(jnp.int32, (tb, 1), 0)
        u = x * cw_ref[CONV - 1:CONV, :] + cb_ref[...]
        for j in range(1, CONV):
            z = jnp.where(row >= tb - j, px, x)
            u = u + pltpu.roll(z, j, 0) * cw_ref[CONV - 1 - j:CONV - j, :]
        ub = u.astype(bf16)
        nn = (((1,), (0,)), ((), ()))
        r = jax.nn.sigmoid(_block_dot(ub, wa_ref, nn) + ba_ref[...])
        ig = jax.nn.sigmoid(_block_dot(ub, wx_ref, nn) + bx_ref[...])
        sp = jax.nn.softplus(-ap_ref[...])
        reset = pos_ref[...] == 0
        a, mult = _lru_gates(r, sp, reset)
        b = u * ig * mult
        sub = row & 7
        for s in (1, 2, 4):
            keep = sub >= s
            a_sh = jnp.where(keep, pltpu.roll(a, s, 0), 1.0)
            b_sh = jnp.where(keep, pltpu.roll(b, s, 0), 0.0)
            b = a * b_sh + b
            a = a * a_sh
        carry = carry_h[...]
        for g in range(tb // 8):
            rows = slice(8 * g, 8 * g + 8)
            hg = b[rows] + a[rows] * carry
            h_ref[rows, :] = hg
            carry = hg[7:8]
        carry_h[...] = carry
        u_ref[...] = u
        r_ref[...] = r
        i_ref[...] = ig
        prev_x[...] = x

    return _rowwise(name, body, T, tb, [(proj, D, 0), (pos, 1, 0)], [cw, cb, wa, ba, wx, bx, ap],
                    [(D, f32)] * 4, scratch=[pltpu.VMEM((tb, D), f32), pltpu.VMEM((1, D), f32)])


def _lru_bwd(name, dh, proj, u, r, ig, h, pos, cw, wa, wx, ap, T, tb):
    n = T // tb
    t8 = tb // 8

    def rmap(i):
        return n - 1 - i

    def body(dh_ref, x_ref, u_ref, r_ref, i_ref, h_ref, hp_ref, pos_ref, cw_ref, wa_ref, wx_ref, ap_ref,
             dx_ref, dcw_ref, dcb_ref, dwa_ref, dba_ref, dwx_ref, dbx_ref, dsp_ref, next_du, carry_g, carry_a):
        step = pl.program_id(0)
        _init_acc(step, dcw_ref, dcb_ref, dwa_ref, dba_ref, dwx_ref, dbx_ref, dsp_ref, next_du, carry_g, carry_a)
        blk = n - 1 - step
        x = x_ref[...].astype(f32)
        u = u_ref[...]
        r = r_ref[...]
        ig = i_ref[...]
        h = h_ref[...]
        row = lax.broadcasted_iota(jnp.int32, (tb, 1), 0)
        sp = jax.nn.softplus(-ap_ref[...])
        reset = pos_ref[...] == 0
        a, mult = _lru_gates(r, sp, reset)
        A = pltpu.roll(jnp.where(row == 0, carry_a[...], a), tb - 1, 0)
        B = dh_ref[...]
        sub = row & 7
        for s in (1, 2, 4):
            keep = sub < 8 - s
            a_sh = jnp.where(keep, pltpu.roll(A, tb - s, 0), 1.0)
            b_sh = jnp.where(keep, pltpu.roll(B, tb - s, 0), 0.0)
            B = B + A * b_sh
            A = A * a_sh
        carry = carry_g[...]
        parts = [None] * (tb // 8)
        for grp in reversed(range(tb // 8)):
            rows = slice(8 * grp, 8 * grp + 8)
            parts[grp] = B[rows] + A[rows] * carry
            carry = parts[grp][0:1]
        g = jnp.concatenate(parts, axis=0)
        carry_g[...] = carry
        carry_a[...] = a[0:1, :]
        h_last = jnp.where(blk == 0, 0.0, hp_ref[7:8, :])
        h_prev = pltpu.roll(jnp.where(row == tb - 1, h_last, h), 1, 0)
        da = g * h_prev
        gu = g * u
        dlog_a = jnp.where(reset, 0.0, da * a - gu * ig * (a * a) / mult)
        du = g * ig * mult
        di = gu * mult
        dr = dlog_a * (-LRU_C * sp)
        dsp_ref[...] += _colsum(dlog_a * (-LRU_C * r)) * (-jax.nn.sigmoid(-ap_ref[...]))
        dzr = dr * r * (1.0 - r)
        dzi = di * ig * (1.0 - ig)
        dzr_b = dzr.astype(bf16)
        dzi_b = dzi.astype(bf16)
        nt = (((1,), (1,)), ((), ()))
        du = du + _block_dot(dzr_b, wa_ref, nt) + _block_dot(dzi_b, wx_ref, nt)
        ub = u.astype(bf16)
        tn = (((0,), (0,)), ((), ()))
        for nb in range(NB):
            sl = slice(nb * BW, (nb + 1) * BW)
            dwa_ref[nb] += lax.dot_general(ub[:, sl], dzr_b[:, sl], tn, preferred_element_type=f32)
            dwx_ref[nb] += lax.dot_general(ub[:, sl], dzi_b[:, sl], tn, preferred_element_type=f32)
        dba_ref[...] += _colsum(dzr)
        dbx_ref[...] += _colsum(dzi)
        dcb_ref[...] += _colsum(du)
        ndu = next_du[...]
        dx = du * cw_ref[CONV - 1:CONV, :]
        dcw_ref[CONV - 1:CONV, :] += _colsum(x * du)
        for j in range(1, CONV):
            up = pltpu.roll(jnp.where(row < j, ndu, du), tb - j, 0)
            dx = dx + up * cw_ref[CONV - 1 - j:CONV - j, :]
            dcw_ref[CONV - 1 - j:CONV - j, :] += _colsum(x * up)
        dx_ref[...] = dx.astype(bf16)
        next_du[...] = du

    row_specs = [pl.BlockSpec((tb, D), lambda i: (rmap(i), 0)) for _ in range(6)]
    in_specs = row_specs + [
        pl.BlockSpec((8, D), lambda i: (jnp.maximum(rmap(i) * t8 - 1, 0), 0)),
        pl.BlockSpec((tb, 1), lambda i: (rmap(i), 0)),
        pl.BlockSpec((CONV, D), lambda i: (0, 0)),
        pl.BlockSpec((NB, BW, BW), lambda i: (0, 0, 0)),
        pl.BlockSpec((NB, BW, BW), lambda i: (0, 0, 0)),
        pl.BlockSpec((1, D), lambda i: (0, 0)),
    ]
    vec = lambda s: pl.BlockSpec(s, functools.partial(lambda i, nd: (0,) * nd, nd=len(s)))
    acc_shapes = [(CONV, D), (1, D), (NB, BW, BW), (1, D), (NB, BW, BW), (1, D), (1, D)]
    return pl.pallas_call(
        body, name=name, grid=(n,), in_specs=in_specs,
        out_specs=[pl.BlockSpec((tb, D), lambda i: (rmap(i), 0))] + [vec(s) for s in acc_shapes],
        out_shape=[jax.ShapeDtypeStruct((T, D), bf16)] + [jax.ShapeDtypeStruct(s, f32) for s in acc_shapes],
        scratch_shapes=[pltpu.VMEM((tb, D), f32), pltpu.VMEM((1, D), f32), pltpu.VMEM((1, D), f32)],
        compiler_params=_cparams(1),
    )(dh, proj, u, r, ig, h, h, pos, cw, wa, wx, ap)


def _rope_tables(name, pos, invf, T, tb):
    def body(pos_ref, invf_ref, cos_ref, sa_ref, sb_ref):
        ang = pos_ref[...].astype(f32) * invf_ref[...]
        lane = lax.broadcasted_iota(jnp.int32, (tb, DH), 1)
        first = (lane >= DN) & (lane < DN + DR // 2)
        second = (lane >= DN + DR // 2) & (lane < DN + DR)
        c = jnp.cos(ang)
        s = jnp.sin(ang)
        cos_ref[...] = jnp.where(lane < DN, 1.0, jnp.where(first | second, c, 0.0))
        sa_ref[...] = jnp.where(first, -s, 0.0)
        sb_ref[...] = jnp.where(second, s, 0.0)

    return _rowwise(name, body, T, tb, [(pos, 1, 0)], [invf], [(DH, f32)] * 3)


def _rope_heads(z, cos, sa, sb, transposed=False, shared=None):
    half = DR // 2
    parts = []
    for hd in range(NH):
        hi = z[:, hd * DH + DN:(hd + 1) * DH]
        if shared is not None:
            hi = hi + shared
        if transposed:
            hi = hi * cos + pltpu.roll(hi * sa, half, 1) + pltpu.roll(hi * sb, DN - half, 1)
        else:
            hi = hi * cos + pltpu.roll(hi, DN - half, 1) * sa + pltpu.roll(hi, half, 1) * sb
        parts += [z[:, hd * DH:hd * DH + DN], hi]
    return parts


def _mla_prep(name, proj, cosp, sap, sbp, gq, gkv, wq, wk, wv, T, tb):
    def body(qkv_ref, cos_ref, sa_ref, sb_ref, gq_ref, gkv_ref, wq_ref, wk_ref, wv_ref,
             q_ref, k_ref, v_ref, cq_ref, ckv_ref):
        qkv = qkv_ref[...].astype(f32)
        qd = qkv[:, :QR]
        kvd = qkv[:, QR:QR + KVR]
        slab = qkv[:, QR + KVR:]
        nq = qd * lax.rsqrt(jnp.mean(qd * qd, axis=-1, keepdims=True) + EPS)
        cq = (nq * gq_ref[...]).astype(bf16)
        cq_scaled = (nq * (gq_ref[...] * QSCALE)).astype(bf16)
        ckv = (kvd * lax.rsqrt(jnp.mean(kvd * kvd, axis=-1, keepdims=True) + EPS) * gkv_ref[...]).astype(bf16)
        cos, sa, sb = cos_ref[:, DN:], sa_ref[:, DN:], sb_ref[:, DN:]
        qpre = jnp.dot(cq_scaled, wq_ref[...], preferred_element_type=f32)
        kpre = jnp.dot(ckv, wk_ref[...], preferred_element_type=f32)
        q_ref[...] = jnp.concatenate(_rope_heads(qpre, cos, sa, sb), axis=1).astype(bf16)
        k_ref[...] = jnp.concatenate(_rope_heads(kpre, cos, sa, sb, shared=slab), axis=1).astype(bf16)
        v_ref[...] = jnp.dot(ckv, wv_ref[...], preferred_element_type=f32).astype(bf16)
        cq_ref[...] = cq
        ckv_ref[...] = ckv

    return _rowwise(name, body, T, tb,
                    [(proj, 512, QKV0 // 512), (cosp, DH, 0), (sap, DH, 0), (sbp, DH, 0)],
                    [gq, gkv, wq, wk, wv],
                    [(NH * DH, bf16), (NH * DH, bf16), (NH * DV, bf16), (QR, bf16), (KVR, bf16)])


def _mla_bwd(name, dq, dk, dv, proj, cosp, sap, sbp, gq, gkv, wq, wk, wv, T, tb):
    def body(dq_ref, dk_ref, dv_ref, qkv_ref, cos_ref, sa_ref, sb_ref, gq_ref, gkv_ref, wq_ref, wk_ref, wv_ref,
             dqkv_ref, dqp_ref, dkp_ref, dgq_ref, dgkv_ref):
        _init_acc(pl.program_id(0), dgq_ref, dgkv_ref)
        cos, sa, sb = cos_ref[:, DN:], sa_ref[:, DN:], sb_ref[:, DN:]
        dqp_b = jnp.concatenate(_rope_heads(dq_ref[...] * SCALE, cos, sa, sb, transposed=True), axis=1).astype(bf16)
        dk_parts = _rope_heads(dk_ref[...], cos, sa, sb, transposed=True)
        dkp_b = jnp.concatenate(dk_parts, axis=1).astype(bf16)
        dqp_ref[...] = dqp_b
        dkp_ref[...] = dkp_b
        nt = (((1,), (1,)), ((), ()))
        dcq = lax.dot_general(dqp_b, wq_ref[...], nt, preferred_element_type=f32)
        dckv = (lax.dot_general(dkp_b, wk_ref[...], nt, preferred_element_type=f32)
                + lax.dot_general(dv_ref[...], wv_ref[...], nt, preferred_element_type=f32))
        dslab = dk_parts[1]
        for hd in range(1, NH):
            dslab = dslab + dk_parts[2 * hd + 1]
        qkv = qkv_ref[...].astype(f32)
        qd = qkv[:, :QR]
        kvd = qkv[:, QR:QR + KVR]
        nq = qd * lax.rsqrt(jnp.mean(qd * qd, axis=-1, keepdims=True) + EPS)
        nkv = kvd * lax.rsqrt(jnp.mean(kvd * kvd, axis=-1, keepdims=True) + EPS)
        dgq_ref[...] += _colsum(dcq * nq)
        dgkv_ref[...] += _colsum(dckv * nkv)
        dnq = dcq * gq_ref[...]
        dnkv = dckv * gkv_ref[...]
        rq = lax.rsqrt(jnp.mean(qd * qd, axis=-1, keepdims=True) + EPS)
        rkv = lax.rsqrt(jnp.mean(kvd * kvd, axis=-1, keepdims=True) + EPS)
        dqd = rq * (dnq - nq * jnp.mean(dnq * nq, axis=-1, keepdims=True))
        dkvd = rkv * (dnkv - nkv * jnp.mean(dnkv * nkv, axis=-1, keepdims=True))
        dqkv_ref[...] = jnp.concatenate([dqd, dkvd, dslab], axis=1).astype(bf16)

    return _rowwise(name, body, T, tb,
                    [(dq, NH * DH, 0), (dk, NH * DH, 0), (dv, NH * DV, 0), (proj, 512, QKV0 // 512),
                     (cosp, DH, 0), (sap, DH, 0), (sbp, DH, 0)],
                    [gq, gkv, wq, wk, wv],
                    [(512, bf16), (NH * DH, bf16), (NH * DH, bf16)],
                    [((1, QR), f32), ((1, KVR), f32)])


def _flash_fwd(name, q, k, v, T, tq, pack=None):
    nq = T // tq
    unroll = 2

    def body(q_ref, k_ref, v_ref, o_ref, lse_ref):
        i = pl.program_id(1)
        nt = (((1,), (1,)), ((), ()))
        qs = [q_ref[:, hh * DH:(hh + 1) * DH] for hh in range(HPS)]

        def one(hh, j, carry, masked):
            m, l, acc = carry
            off = pl.multiple_of(j * tq, tq)
            kj = k_ref[pl.ds(off, tq), hh * DH:(hh + 1) * DH]
            vj = v_ref[pl.ds(off, tq), hh * DV:(hh + 1) * DV]
            s = lax.dot_general(qs[hh], kj, nt, preferred_element_type=f32)
            if masked:
                rr = lax.broadcasted_iota(jnp.int32, (tq, tq), 0)
                cc = lax.broadcasted_iota(jnp.int32, (tq, tq), 1)
                s = jnp.where(cc <= rr, s, NEG)
            m_new = jnp.maximum(m, jnp.max(s, axis=-1, keepdims=True))
            alpha = jnp.exp2(m - m_new)
            p = jnp.exp2(s - m_new)
            l = alpha * l + jnp.sum(p, axis=-1, keepdims=True)
            acc = alpha * acc + jnp.dot(p.astype(bf16), vj, preferred_element_type=f32)
            return m_new, l, acc

        def step(j, carries, masked):
            return tuple(one(hh, j, carries[hh], masked) for hh in range(HPS))

        def pair(jj, carries):
            for u in range(unroll):
                carries = step(jj * unroll + u, carries, False)
            return carries

        init1 = (jnp.full((tq, 1), NEG, f32), jnp.zeros((tq, 1), f32), jnp.zeros((tq, DV), f32))
        carry = lax.fori_loop(0, i // unroll, pair, tuple(init1 for _ in range(HPS)))
        carry = lax.fori_loop((i // unroll) * unroll, i, lambda j, c: step(j, c, False), carry)
        carry = step(i, carry, True)
        for hh in range(HPS):
            m, l, acc = carry[hh]
            o_ref[:, hh * DV:(hh + 1) * DV] = (acc / l).astype(bf16)
            lse_ref[hh] = m + jnp.log2(l)

    def body_with_gather(q_ref, k_ref, v_ref, *rest):
        nw = len(pack)
        w_refs, (o_ref, lse_ref), wall_refs, sems = rest[:nw], rest[nw:nw + 2], rest[nw + 2:2 * nw + 2], rest[2 * nw + 2:]
        h = pl.program_id(0)
        i = pl.program_id(1)

        @pl.when((h == 0) & (i == 0))
        def _():
            mine, first, _ = _gather_copies(w_refs, wall_refs, *sems)
            _gather_start(mine, first)

        @pl.when((h == ngrp // 2) & (i == 0))
        def _():
            _, first, passed = _gather_copies(w_refs, wall_refs, *sems)
            _gather_forward(first, passed)

        body(q_ref, k_ref, v_ref, o_ref, lse_ref)

        @pl.when((h == ngrp - 1) & (i == nq - 1))
        def _():
            _gather_finish(*_gather_copies(w_refs, wall_refs, *sems))

    ngrp = NH // HPS
    in_specs = [pl.BlockSpec((tq, HPS * DH), lambda h, i: (i, h)),
                pl.BlockSpec((T, HPS * DH), lambda h, i: (0, h), pipeline_mode=pl.Buffered(1)),
                pl.BlockSpec((T, HPS * DV), lambda h, i: (0, h), pipeline_mode=pl.Buffered(1))]
    out_specs = [pl.BlockSpec((tq, HPS * DV), lambda h, i: (i, h)), pl.BlockSpec((HPS, tq, 1), lambda h, i: (h, i, 0))]
    out_shape = [jax.ShapeDtypeStruct((T, NH * DV), bf16), jax.ShapeDtypeStruct((NH, T, 1), f32)]
    if pack is None:
        return pl.pallas_call(body, name=name, grid=(ngrp, nq), in_specs=in_specs, out_specs=out_specs,
                              out_shape=out_shape, compiler_params=_cparams(2))(q, k, v)
    return pl.pallas_call(
        body_with_gather, name=name, grid=(ngrp, nq), in_specs=in_specs + [ANY] * len(pack),
        out_specs=out_specs + [ANY] * len(pack),
        out_shape=out_shape + [jax.ShapeDtypeStruct((4,) + a.shape, a.dtype) for a in pack],
        scratch_shapes=_gather_sems(pack), compiler_params=_cparams(2),
    )(q, k, v, *pack)


def _flash_bwd(name, q, k, v, do, lse, dl, T, tq, scatter=None):
    HPS = HPS_BWD
    nq = T // tq

    def body(k_ref, v_ref, q_ref, do_ref, lse_ref, dl_ref, dk_ref, dv_ref, dq_ref):
        j = pl.program_id(1)

        @pl.when(j == 0)
        def _():
            dq_ref[...] = jnp.zeros_like(dq_ref)

        nt = (((1,), (1,)), ((), ()))
        tn = (((0,), (0,)), ((), ()))
        ks = [k_ref[:, hh * DH:(hh + 1) * DH] for hh in range(HPS)]
        vs = [v_ref[:, hh * DV:(hh + 1) * DV] for hh in range(HPS)]

        def one(hh, i, carry, masked):
            dk, dv = carry
            off = pl.multiple_of(i * tq, tq)
            qi = q_ref[pl.ds(off, tq), hh * DH:(hh + 1) * DH]
            doi = do_ref[pl.ds(off, tq), hh * DV:(hh + 1) * DV]
            st = lax.dot_general(ks[hh], qi, nt, preferred_element_type=f32)
            if masked:
                kr = lax.broadcasted_iota(jnp.int32, (tq, tq), 0)
                qc = lax.broadcasted_iota(jnp.int32, (tq, tq), 1)
                st = jnp.where(qc >= kr, st, NEG)
            dpt = lax.dot_general(vs[hh], doi, nt, preferred_element_type=f32)
            pt = jnp.exp2(st - lse_ref[hh, i])
            dst = (pt * (dpt - dl_ref[hh, i])).astype(bf16)
            dv = dv + jnp.dot(pt.astype(bf16), doi, preferred_element_type=f32)
            dk = dk + jnp.dot(dst, qi, preferred_element_type=f32)
            dq_ref[pl.ds(off, tq), hh * DH:(hh + 1) * DH] += lax.dot_general(dst, ks[hh], tn, preferred_element_type=f32)
            return dk, dv

        def step(i, carries, masked):
            return tuple(one(hh, i, carries[hh], masked) for hh in range(HPS))

        zero = tuple((jnp.zeros((tq, DH), f32), jnp.zeros((tq, DV), f32)) for _ in range(HPS))
        carry = step(j, zero, True)
        carry = lax.fori_loop(j + 1, nq, lambda i, c: step(i, c, False), carry)
        for hh in range(HPS):
            dk, dv = carry[hh]
            dk_ref[:, hh * DH:(hh + 1) * DH] = dk * LN2
            dv_ref[:, hh * DV:(hh + 1) * DV] = dv.astype(bf16)

    def body_with_scatter(*refs):
        ns = len(scatter)
        ins, a_refs, outs = refs[:6], refs[6:6 + ns], refs[6 + ns:9 + ns]
        o_refs, sems = refs[9 + ns:9 + 2 * ns], refs[9 + 2 * ns:]
        h = pl.program_id(0)
        j = pl.program_id(1)

        @pl.when((h == 0) & (j == 0))
        def _():
            for cp in _scatter_copies(a_refs, o_refs, *sems):
                cp.start()

        body(*ins, *outs)

        @pl.when((h == ngrp - 1) & (j == nq - 1))
        def _():
            for cp in _scatter_copies(a_refs, o_refs, *sems):
                cp.wait()

    ngrp = NH // HPS
    in_specs = [pl.BlockSpec((tq, HPS * DH), lambda h, j: (j, h)),
                pl.BlockSpec((tq, HPS * DV), lambda h, j: (j, h)),
                pl.BlockSpec((T, HPS * DH), lambda h, j: (0, h), pipeline_mode=pl.Buffered(1)),
                pl.BlockSpec((T, HPS * DV), lambda h, j: (0, h), pipeline_mode=pl.Buffered(1)),
                pl.BlockSpec((HPS, nq, 1, tq), lambda h, j: (h, 0, 0, 0)),
                pl.BlockSpec((HPS, nq, 1, tq), lambda h, j: (h, 0, 0, 0))]
    out_specs = [pl.BlockSpec((tq, HPS * DH), lambda h, j: (j, h)),
                 pl.BlockSpec((tq, HPS * DV), lambda h, j: (j, h)),
                 pl.BlockSpec((T, HPS * DH), lambda h, j: (0, h), pipeline_mode=pl.Buffered(1))]
    out_shape = [jax.ShapeDtypeStruct((T, NH * DH), f32), jax.ShapeDtypeStruct((T, NH * DV), bf16),
                 jax.ShapeDtypeStruct((T, NH * DH), f32)]
    if scatter is None:
        return pl.pallas_call(body, name=name, grid=(ngrp, nq), in_specs=in_specs, out_specs=out_specs,
                              out_shape=out_shape, compiler_params=_cparams(2))(k, v, q, do, lse, dl)
    ns = len(scatter)
    return pl.pallas_call(
        body_with_scatter, name=name, grid=(ngrp, nq), in_specs=in_specs + [ANY] * ns,
        out_specs=out_specs + [ANY] * ns, out_shape=out_shape + _scatter_shapes(scatter),
        scratch_shapes=_scatter_sems(scatter), compiler_params=_cparams(2),
    )(k, v, q, do, lse, dl, *scatter)


def _gate_mix(name, proj, ya, o, T, tb):
    def body(ga_ref, gb_ref, ya_ref, o_ref, y_ref):
        y_ref[...] = (jax.nn.sigmoid(ga_ref[...].astype(f32)) * ya_ref[...]
                      + jax.nn.sigmoid(gb_ref[...].astype(f32)) * o_ref[...].astype(f32)).astype(bf16)

    return _rowwise(name, body, T, tb, [(proj, D, 1), (proj, D, 2), (ya, D, 0), (o, D, 0)], [], [(D, bf16)])[0]


def _gate_mix_bwd(name, dy, proj, ya, o, T, tb):
    def body(dy_ref, ga_ref, gb_ref, ya_ref, o_ref, dga_ref, dgb_ref, dya_ref, do_ref, dl_ref):
        dy = dy_ref[...]
        sa = jax.nn.sigmoid(ga_ref[...].astype(f32))
        sb = jax.nn.sigmoid(gb_ref[...].astype(f32))
        ov = o_ref[...].astype(f32)
        dga_ref[...] = (dy * ya_ref[...] * sa * (1.0 - sa)).astype(bf16)
        dgb_ref[...] = (dy * ov * sb * (1.0 - sb)).astype(bf16)
        dya_ref[...] = dy * sa
        do = dy * sb
        do_ref[...] = do.astype(bf16)
        prod = do * ov
        for hd in range(NH):
            dl_ref[:, hd:hd + 1] = jnp.sum(prod[:, hd * DV:(hd + 1) * DV], axis=-1, keepdims=True)

    return _rowwise(name, body, T, tb, [(dy, D, 0), (proj, D, 1), (proj, D, 2), (ya, D, 0), (o, D, 0)], [],
                    [(D, bf16), (D, bf16), (D, f32), (D, bf16), (NH, f32)])


def _ffn_in_swiglu(name, h, w, T):
    tm = _pick(T, (512, 256, 128))

    def body(h_ref, w_ref, z_ref, a_ref):
        z = jnp.dot(h_ref[...], w_ref[...], preferred_element_type=f32)
        g = z[:, :FH]
        z_ref[...] = z.astype(bf16)
        a_ref[...] = (g * jax.nn.sigmoid(g) * z[:, FH:]).astype(bf16)

    return pl.pallas_call(
        body, name=name, grid=(2, T // tm),
        in_specs=[pl.BlockSpec((tm, D), lambda j, i: (i, 0)), pl.BlockSpec((D, 2 * FH), lambda j, i: (0, j))],
        out_specs=[pl.BlockSpec((tm, 2 * FH), lambda j, i: (i, j)), pl.BlockSpec((tm, FH), lambda j, i: (i, j))],
        out_shape=[jax.ShapeDtypeStruct((T, 2 * DFF), bf16), jax.ShapeDtypeStruct((T, DFF), bf16)],
        compiler_params=_cparams(2),
    )(h, w)


def _dact_swiglu_bwd(name, dfo, w, z, T):
    tm = _pick(T, (512, 256, 128))

    def body(d_ref, w_ref, z_ref, o_ref):
        for c0, c1 in ((0, FFN_SPLIT), (FFN_SPLIT, FH)):
            da = lax.dot_general(d_ref[...], w_ref[c0:c1, :], (((1,), (1,)), ((), ())), preferred_element_type=f32)
            g = z_ref[:, c0:c1].astype(f32)
            u = z_ref[:, FH + c0:FH + c1].astype(f32)
            sg = jax.nn.sigmoid(g)
            o_ref[:, c0:c1] = (da * u * sg * (1.0 + g * (1.0 - sg))).astype(bf16)
            o_ref[:, FH + c0:FH + c1] = (da * g * sg).astype(bf16)

    return pl.pallas_call(
        body, name=name, grid=(2, T // tm),
        in_specs=[pl.BlockSpec((tm, D), lambda j, i: (i, 0)), pl.BlockSpec((FH, D), lambda j, i: (j, 0)),
                  pl.BlockSpec((tm, 2 * FH), lambda j, i: (i, j))],
        out_specs=pl.BlockSpec((tm, 2 * FH), lambda j, i: (i, j)),
        out_shape=jax.ShapeDtypeStruct((T, 2 * DFF), bf16), compiler_params=_cparams(2),
    )(dfo, w, z)


def _final(name, x, g, m, gfin, tgt, T, tb):
    def body(x_ref, m_ref, t_ref, g_ref, gf_ref, dx_ref, dm_ref, loss_ref, dgf_ref, dg_ref):
        _init_acc(pl.program_id(0), loss_ref, dgf_ref, dg_ref)
        mv = m_ref[...]
        xv = x_ref[...] + g_ref[...] * mv
        rstd = lax.rsqrt(jnp.mean(xv * xv, axis=-1, keepdims=True) + EPS)
        nv = xv * rstd
        err = nv * gf_ref[...] - t_ref[...]
        loss_ref[...] += 0.5 * jnp.sum(jnp.mean(err * err, axis=-1, keepdims=True), axis=0, keepdims=True)
        dy = err * (1.0 / D)
        dgf_ref[...] += _colsum(dy * nv)
        dn = dy * gf_ref[...]
        dxv = rstd * (dn - nv * jnp.mean(dn * nv, axis=-1, keepdims=True))
        dx_ref[...] = dxv
        dm_ref[...] = (dxv * g_ref[...]).astype(bf16)
        dg_ref[...] += _colsum(dxv * mv)

    return _rowwise(name, body, T, tb, [(x, D, 0), (m, D, 0), (tgt, D, 0)], [g, gfin], [(D, f32), (D, bf16)],
                    [((1, 1), f32), ((1, D), f32), ((1, D), f32)])


def _adamw(name, w, g, m, v):
    R, C = w.shape
    tb = _pick(R, (256, 128, 64, 32, 16, 8))
    c1 = 1.0 - ADAM_B1 ** ADAM_STEP
    c2 = 1.0 - ADAM_B2 ** ADAM_STEP

    def body(w_ref, g_ref, m_ref, v_ref, d_ref, m2_ref, v2_ref):
        gv = g_ref[...]
        m2 = ADAM_B1 * m_ref[...] + (1.0 - ADAM_B1) * gv
        v2 = ADAM_B2 * v_ref[...] + (1.0 - ADAM_B2) * (gv * gv)
        m2_ref[...] = m2
        v2_ref[...] = v2
        d_ref[...] = -ADAM_LR * ((m2 / c1) / (jnp.sqrt(v2 / c2) + ADAM_EPS) + ADAM_WD * w_ref[...])

    return _rowwise(name, body, R, tb, [(w, C, 0), (g, C, 0), (m, C, 0), (v, C, 0)], [], [(C, f32)] * 3)


def _ada_fwd(name, c_all, w_ada, b_sh):
    nl, _, ns = w_ada.shape

    def body(c_ref, w_ref, b_ref, o_ref):
        o_ref[0] = jnp.dot(c_ref[...].astype(bf16), w_ref[0].astype(bf16), preferred_element_type=f32) + b_ref[0]

    return pl.pallas_call(
        body, name=name, grid=(nl,),
        in_specs=[pl.BlockSpec((8, D), lambda l: (0, 0)), pl.BlockSpec((1, D, ns), lambda l: (l, 0, 0)),
                  pl.BlockSpec((1, 1, ns), lambda l: (l, 0, 0))],
        out_specs=pl.BlockSpec((1, 8, ns), lambda l: (l, 0, 0)),
        out_shape=jax.ShapeDtypeStruct((nl, 8, ns), f32), compiler_params=_cparams(1),
    )(c_all, w_ada, b_sh)


def _ada_bwd(name, c_t, dm_sh):
    nl, _, ns = dm_sh.shape

    def body(c_ref, dm_ref, o_ref):
        acc = c_ref[:, 0:1] * dm_ref[0, 0:1, :]
        for d in range(1, 8):
            acc = acc + c_ref[:, d:d + 1] * dm_ref[0, d:d + 1, :]
        o_ref[0] = acc

    return pl.pallas_call(
        body, name=name, grid=(nl,),
        in_specs=[pl.BlockSpec((D, 8), lambda l: (0, 0)), pl.BlockSpec((1, 8, ns), lambda l: (l, 0, 0))],
        out_specs=pl.BlockSpec((1, D, ns), lambda l: (l, 0, 0)),
        out_shape=jax.ShapeDtypeStruct((nl, D, ns), f32), compiler_params=_cparams(1),
    )(c_t, dm_sh)


def _sum_slots(name, a):
    n, R, C = a.shape
    tb = _pick(R, (256, 128, 64, 32, 16, 8))

    def body(a_ref, o_ref):
        acc = a_ref[0].astype(f32)
        for s in range(1, n):
            acc = acc + a_ref[s].astype(f32)
        o_ref[...] = acc

    return pl.pallas_call(
        body, name=name, grid=(R // tb,), in_specs=[pl.BlockSpec((n, tb, C), lambda i: (0, i, 0))],
        out_specs=pl.BlockSpec((tb, C), lambda i: (i, 0)), out_shape=jax.ShapeDtypeStruct((R, C), f32),
        compiler_params=_cparams(1),
    )(a)


def _chunks(rows, dtype, want):
    unit = 16 if dtype == bf16 else 8
    for n in range(want, 0, -1):
        if rows % n == 0 and (rows // n) % unit == 0:
            return n, rows // n
    return 1, rows


def _my_chip():
    return 2 * lax.axis_index("x") + lax.axis_index("y")


def _pair_sum(name, arrs, gots):
    na = len(arrs)

    def body(*refs):
        for k in range(na):
            a_ref, g_ref, o_ref = refs[k], refs[na + k], refs[2 * na + k]
            o_ref[...] = (a_ref[...].astype(f32) + g_ref[...].astype(f32)).astype(o_ref.dtype)

    in_specs = [pl.BlockSpec((1, a.shape[1] // 4, a.shape[2]), lambda s, i: (s, 2 * lax.axis_index("c") + i, 0))
                for a in arrs]
    half_specs = [pl.BlockSpec((1, g.shape[1] // 2, g.shape[2]), lambda s, i: (s, i, 0)) for g in gots]
    return pl.pallas_call(
        body, name=name, grid=(4, 2), in_specs=in_specs + half_specs, out_specs=half_specs,
        out_shape=[jax.ShapeDtypeStruct(g.shape, a.dtype) for a, g in zip(arrs, gots)], compiler_params=_cparams(2),
    )(*arrs, *gots)


def _chip_sum(name, sums, lands, layer, bufs=None):
    na = len(sums)

    def body(*refs):
        for k in range(na):
            s_ref, l_ref, o_ref = refs[k], refs[na + k], refs[len(refs) - na + k]
            acc = s_ref[0].astype(f32)
            for j in range(l_ref.shape[0]):
                acc = acc + l_ref[j].astype(f32)
            o_ref[...] = acc.reshape(o_ref.shape)

    in_specs = [pl.BlockSpec((1, s.shape[1] // 2, s.shape[2]), lambda i: (_my_chip(), i, 0)) for s in sums]
    in_specs += [pl.BlockSpec((3, l.shape[1] // 2, l.shape[2]), lambda i: (0, i, 0)) for l in lands]
    if layer is None:
        out_specs = [pl.BlockSpec((s.shape[1] // 2, s.shape[2]), lambda i: (i, 0)) for s in sums]
        out_shape = [jax.ShapeDtypeStruct(s.shape[1:], f32) for s in sums]
    else:
        out_specs = [pl.BlockSpec((1, 1, s.shape[1] // 2, s.shape[2]), lambda i: (layer, lax.axis_index("c"), i, 0))
                     for s in sums]
        out_shape = [jax.ShapeDtypeStruct((2, 2) + s.shape[1:], f32) for s in sums]
    aliases = {}
    operands = list(sums) + list(lands)
    if bufs is not None:
        in_specs += [ANY] * na
        aliases = {2 * na + k: k for k in range(na)}
        operands += list(bufs)
    return pl.pallas_call(
        body, name=name, grid=(2,), in_specs=in_specs, out_specs=out_specs, out_shape=out_shape,
        input_output_aliases=aliases, compiler_params=_cparams(1),
    )(*operands)


def _coords():
    return lax.axis_index("x"), lax.axis_index("y"), lax.axis_index("c")


def _flip(v, m):
    return (1 - v) if m else v


def _peer(x, y, c, mask):
    return (_flip(x, mask[0]), _flip(y, mask[1]), _flip(c, mask[2]))


def _all_gather(name, arr, masks, whole_mesh):
    nslots = 8 if whole_mesh else 4

    def slot(x, y, c):
        return 4 * x + 2 * y + c if whole_mesh else 2 * x + y

    def body(a_ref, o_ref, send_sems, recv_sems, local_sem):
        x, y, c = _coords()
        mine = pltpu.make_async_copy(a_ref, o_ref.at[slot(x, y, c)], local_sem)
        mine.start()
        copies = []
        for k, mask in enumerate(masks):
            cp = pltpu.make_async_remote_copy(
                src_ref=a_ref, dst_ref=o_ref.at[slot(x, y, c)], send_sem=send_sems.at[k], recv_sem=recv_sems.at[k],
                device_id=_peer(x, y, c, mask), device_id_type=MESH)
            cp.start()
            copies.append(cp)
        for cp in copies:
            cp.wait()
        mine.wait()

    return pl.pallas_call(
        body, name=name, in_specs=[ANY], out_specs=ANY,
        out_shape=jax.ShapeDtypeStruct((nslots,) + arr.shape, arr.dtype),
        scratch_shapes=[pltpu.SemaphoreType.DMA((len(masks),)), pltpu.SemaphoreType.DMA((len(masks),)),
                        pltpu.SemaphoreType.DMA],
    )(arr)


def _gather_weights(name, arrs):
    na = len(arrs)

    def body(*refs):
        mine, first, passed = _gather_copies(refs[:na], refs[na:2 * na], *refs[2 * na:])
        _gather_start(mine, first)
        _gather_forward(first, passed)
        _gather_finish(mine, first, passed)

    return pl.pallas_call(
        body, name=name, in_specs=[ANY] * na, out_specs=[ANY] * na,
        out_shape=[jax.ShapeDtypeStruct((4,) + a.shape, a.dtype) for a in arrs], scratch_shapes=_gather_sems(arrs),
    )(*arrs)


def _gather_plan(shape, dtype):
    big = (shape[0] // 2) * shape[1] * 2 >= (2 << 20)
    return _chunks(shape[0] // 2, dtype, 3 if big else 1), _chunks(shape[0], dtype, 4 if big else 1)


def _gather_sems(arrs):
    plans = [_gather_plan(a.shape, a.dtype) for a in arrs]
    nremote = sum(len(CHIP_MASKS) * p[0][0] for p in plans)
    nlocal = sum(p[1][0] for p in plans)
    return [pltpu.SemaphoreType.DMA((nremote,))] * 4 + [pltpu.SemaphoreType.DMA((nlocal,))]


def _gather_copies(w_refs, o_refs, sa, ra, sb, rb, local_sems):
    x, y, c = _coords()
    chip = 2 * x + y
    mine, first, passed = [], [], []
    for w_ref, o_ref in zip(w_refs, o_refs):
        (nch, rows), (nloc, lrows) = _gather_plan(w_ref.shape, w_ref.dtype)
        rh = w_ref.shape[0] // 2
        for k in range(nloc):
            part = pl.ds(k * lrows, lrows)
            mine.append(pltpu.make_async_copy(w_ref.at[part], o_ref.at[chip, part], local_sems.at[len(mine)]))
        for mask in CHIP_MASKS:
            px, py, _ = _peer(x, y, c, mask)
            for ch in range(nch):
                k = len(first)
                part = pl.ds(c * rh + ch * rows, rows)
                first.append(pltpu.make_async_remote_copy(
                    src_ref=w_ref.at[part], dst_ref=o_ref.at[chip, part], send_sem=sa.at[k], recv_sem=ra.at[k],
                    device_id=(px, py, c), device_id_type=MESH))
                landed = o_ref.at[2 * px + py, part]
                passed.append(pltpu.make_async_remote_copy(
                    src_ref=landed, dst_ref=landed, send_sem=sb.at[k], recv_sem=rb.at[k],
                    device_id=(x, y, 1 - c), device_id_type=MESH))
    return mine, first, passed


def _gather_start(mine, first):
    for cp in mine + first:
        cp.start()


def _gather_forward(first, passed):
    for a, b in zip(first, passed):
        a.wait_recv()
        b.start()


def _gather_finish(mine, first, passed):
    for a, b in zip(first, passed):
        a.wait_send()
        b.wait()
    for cp in mine:
        cp.wait()


def _swap_halves(name, arrs):
    na = len(arrs)
    plan = []
    for a in arrs:
        h = a.shape[1] // 2
        nch, rows = _chunks(h, a.dtype, 2 if h * a.shape[2] * 2 >= (1 << 20) else 1)
        plan.append([(s, ch * rows, rows) for s in range(a.shape[0]) for ch in range(nch)])
    ncopies = sum(len(p) for p in plan)

    def body(*refs):
        a_refs, got_refs = refs[:na], refs[na:2 * na]
        send_sems, recv_sems = refs[2 * na:]
        x, y, c = _coords()
        cps = []
        for ai in range(na):
            h = arrs[ai].shape[1] // 2
            for (s, r0, rows) in plan[ai]:
                k = len(cps)
                cp = pltpu.make_async_remote_copy(
                    src_ref=a_refs[ai].at[s, pl.ds((1 - c) * h + r0, rows)], dst_ref=got_refs[ai].at[s, pl.ds(r0, rows)],
                    send_sem=send_sems.at[k], recv_sem=recv_sems.at[k], device_id=(x, y, 1 - c), device_id_type=MESH)
                cp.start()
                cps.append(cp)
        for cp in cps:
            cp.wait()

    halves = [jax.ShapeDtypeStruct((a.shape[0], a.shape[1] // 2, a.shape[2]), a.dtype) for a in arrs]
    return pl.pallas_call(
        body, name=name, in_specs=[ANY] * na, out_specs=[ANY] * na, out_shape=halves,
        scratch_shapes=[pltpu.SemaphoreType.DMA((ncopies,))] * 2,
    )(*arrs)


def _scatter_shapes(arrs):
    return [jax.ShapeDtypeStruct((len(CHIP_MASKS),) + a.shape[1:], a.dtype) for a in arrs]


def _scatter_sems(arrs):
    return [pltpu.SemaphoreType.DMA((len(arrs), len(CHIP_MASKS)))] * 2


def _scatter_copies(a_refs, o_refs, send_sems, recv_sems):
    x, y, c = _coords()
    cps = []
    for k in range(len(a_refs)):
        for j, mask in enumerate(CHIP_MASKS):
            px, py, pc = _peer(x, y, c, mask)
            cps.append(pltpu.make_async_remote_copy(
                src_ref=a_refs[k].at[2 * px + py], dst_ref=o_refs[k].at[j],
                send_sem=send_sems.at[k, j], recv_sem=recv_sems.at[k, j],
                device_id=(px, py, pc), device_id_type=MESH))
    return cps


def _join_halves(name, bufs, small):
    na = len(bufs)
    nm = len(ALL_MASKS)

    def body(*refs):
        small_ref, o_refs, all_ref = refs[na], refs[na + 1:2 * na + 1], refs[2 * na + 1]
        send_sems, recv_sems, gs_sems, gr_sems, local_sem = refs[2 * na + 2:]
        x, y, c = _coords()
        cps = []
        for k in range(na):
            for l in range(2):
                part = o_refs[k].at[l, c]
                cps.append(pltpu.make_async_remote_copy(
                    src_ref=part, dst_ref=part, send_sem=send_sems.at[k, l], recv_sem=recv_sems.at[k, l],
                    device_id=(x, y, 1 - c), device_id_type=MESH))
        slot = all_ref.at[4 * x + 2 * y + c]
        for k, mask in enumerate(ALL_MASKS):
            cps.append(pltpu.make_async_remote_copy(
                src_ref=small_ref, dst_ref=slot, send_sem=gs_sems.at[k], recv_sem=gr_sems.at[k],
                device_id=_peer(x, y, c, mask), device_id_type=MESH))
        mine = pltpu.make_async_copy(small_ref, slot, local_sem)
        mine.start()
        for cp in cps:
            cp.start()
        for cp in cps:
            cp.wait()
        mine.wait()

    return pl.pallas_call(
        body, name=name, in_specs=[ANY] * (na + 1), out_specs=[ANY] * (na + 1),
        out_shape=[jax.ShapeDtypeStruct(b.shape, b.dtype) for b in bufs]
        + [jax.ShapeDtypeStruct((8,) + small.shape, small.dtype)],
        input_output_aliases={k: k for k in range(na)},
        scratch_shapes=[pltpu.SemaphoreType.DMA((na, 2))] * 2 + [pltpu.SemaphoreType.DMA((nm,))] * 2
        + [pltpu.SemaphoreType.DMA],
    )(*bufs, small)


BIG = ("w_in", "w_uq", "w_ukv", "w_out", "w_ffn_in", "w_ffn_out")
SMALL = ("conv_w", "conv_b", "lru_wa", "lru_ba", "lru_wx", "lru_bx", "lru_a_param", "q_norm_g", "kv_norm_g",
         "final_norm_g")
SMALL_SHAPES = {"conv_w": (2, CONV, D), "conv_b": (2, D), "lru_wa": (2, NB, BW, BW), "lru_ba": (2, NB, BW),
                "lru_wx": (2, NB, BW, BW), "lru_bx": (2, NB, BW), "lru_a_param": (2, D), "q_norm_g": (2, QR),
                "kv_norm_g": (2, KVR), "final_norm_g": (D,)}
SMALL_ROWS = 640
FFN_CHIP_ORDER = (0, 2, 1, 3)


def _size(shape):
    n = 1
    for s in shape:
        n *= s
    return n


def _pack_rows(parts, rows, dtype):
    flat = jnp.concatenate([p.reshape(-1).astype(dtype) for p in parts])
    pad = rows * 1024 - flat.shape[0]
    if pad:
        flat = jnp.concatenate([flat, jnp.zeros((pad,), dtype)])
    return flat.reshape(rows, 1024)


def _unpack_rows(pack, shapes):
    flat = pack.reshape(-1)
    out, off = [], 0
    for s in shapes:
        n = _size(s)
        out.append(flat[off:off + n].reshape(s))
        off += n
    return out


def _inproj_weights(g_in, g_uq, g_ukv):
    w_in = jnp.concatenate([g_in[s] for s in range(4)], axis=1)
    win_p = jnp.concatenate([w_in[:, :D], w_in[:, 1472:2496], w_in[:, 2496:], w_in[:, D:1472],
                             jnp.zeros((D, 64), w_in.dtype)], axis=1)
    w_uq = g_uq.reshape(QR, NH, DN + DR)
    wq_p = jnp.concatenate([w_uq, jnp.zeros((QR, NH, DH - DN - DR), w_uq.dtype)], axis=2).reshape(QR, NH * DH)
    w_ukv = jnp.transpose(g_ukv.reshape(4, KVR, NH, (DN + DV) // 4), (1, 2, 0, 3)).reshape(KVR, NH, DN + DV)
    wk_p = jnp.concatenate([w_ukv[..., :DN], jnp.zeros((KVR, NH, DH - DN), w_ukv.dtype)], axis=2).reshape(KVR, NH * DH)
    wv = w_ukv[..., DN:].reshape(KVR, NH * DV)
    return dict(win=win_p, wq=wq_p, wk=wk_p, wv=wv)


def _other_weights(g_out, g_fi, g_fo):
    ko, ki, kf = g_out.shape[1] // 2, g_fi.shape[1] // 2, g_fo.shape[1] // 2
    wout = [g_out[:, l * ko:(l + 1) * ko].reshape(4 * ko, D) for l in range(2)]
    wfi = [jnp.concatenate([g_fi[s, l * ki:(l + 1) * ki] for s in FFN_CHIP_ORDER], axis=1) for l in range(2)]
    wfo = [g_fo[:, l * kf:(l + 1) * kf].reshape(4 * kf, D) for l in range(2)]
    return dict(wout=wout, wfi=wfi, wfo=wfo)


def _shard_grads_inproj(dwin, dwq, dwk, dwv):
    w_in = jnp.concatenate([dwin[:, :D], dwin[:, QKV0:QKV0 + 448], dwin[:, D:QKV0]], axis=1)
    w_in = jnp.transpose(w_in.reshape(D, 4, DIN // 4), (1, 0, 2))
    w_uq = dwq.reshape(QR, NH, DH)[..., :DN + DR].reshape(4, QR // 4, NH * (DN + DR))
    w_ukv = jnp.concatenate([dwk.reshape(KVR, NH, DH)[..., :DN], dwv.reshape(KVR, NH, DV)], axis=2)
    w_ukv = jnp.transpose(w_ukv.reshape(KVR, NH, 4, (DN + DV) // 4), (2, 0, 1, 3)).reshape(4, KVR, NH * (DN + DV) // 4)
    return [w_in, w_uq, w_ukv]


def _shard_grads_rest(dwout, dwfi, dwfo):
    tiles = jnp.split(dwfi, 4, axis=1)
    wfi = jnp.stack([tiles[FFN_CHIP_ORDER.index(s)] for s in range(4)])
    return [dwout.reshape(4, D // 4, D), wfi, dwfo.reshape(4, DFF // 4, D)]


def kernel(x, c, positions, w_ada, b_ada, w_in, conv_w, conv_b, lru_wa, lru_ba, lru_wx, lru_bx, lru_a_param, q_norm_g, kv_norm_g, w_uq, w_ukv, w_out, w_ffn_in, w_ffn_out, final_norm_g, loss_target, m_w_ada, m_b_ada, m_w_in, m_conv_w, m_conv_b, m_lru_wa, m_lru_ba, m_lru_wx, m_lru_bx, m_lru_a_param, m_q_norm_g, m_kv_norm_g, m_w_uq, m_w_ukv, m_w_out, m_w_ffn_in, m_w_ffn_out, m_final_norm_g, v_w_ada, v_b_ada, v_w_in, v_conv_w, v_conv_b, v_lru_wa, v_lru_ba, v_lru_wx, v_lru_bx, v_lru_a_param, v_q_norm_g, v_kv_norm_g, v_w_uq, v_w_ukv, v_w_out, v_w_ffn_in, v_w_ffn_out, v_final_norm_g):
    T = x.shape[1]
    tb = min(ROW_BLOCK, T)
    tq = min(ATTN_BLOCK, T)
    nq = T // tq
    mx, my, mc = _coords()
    chip = 2 * mx + my
    dev = 4 * mx + 2 * my + mc
    weights = dict(w_ada=w_ada, b_ada=b_ada, w_in=w_in, conv_w=conv_w, conv_b=conv_b, lru_wa=lru_wa, lru_ba=lru_ba,
                   lru_wx=lru_wx, lru_bx=lru_bx, lru_a_param=lru_a_param, q_norm_g=q_norm_g, kv_norm_g=kv_norm_g,
                   w_uq=w_uq, w_ukv=w_ukv, w_out=w_out, w_ffn_in=w_ffn_in, w_ffn_out=w_ffn_out,
                   final_norm_g=final_norm_g)
    mom = dict(w_ada=m_w_ada, b_ada=m_b_ada, w_in=m_w_in, conv_w=m_conv_w, conv_b=m_conv_b, lru_wa=m_lru_wa,
               lru_ba=m_lru_ba, lru_wx=m_lru_wx, lru_bx=m_lru_bx, lru_a_param=m_lru_a_param, q_norm_g=m_q_norm_g,
               kv_norm_g=m_kv_norm_g, w_uq=m_w_uq, w_ukv=m_w_ukv, w_out=m_w_out, w_ffn_in=m_w_ffn_in,
               w_ffn_out=m_w_ffn_out, final_norm_g=m_final_norm_g)
    var = dict(w_ada=v_w_ada, b_ada=v_b_ada, w_in=v_w_in, conv_w=v_conv_w, conv_b=v_conv_b, lru_wa=v_lru_wa,
               lru_ba=v_lru_ba, lru_wx=v_lru_wx, lru_bx=v_lru_bx, lru_a_param=v_lru_a_param, q_norm_g=v_q_norm_g,
               kv_norm_g=v_kv_norm_g, w_uq=v_w_uq, w_ukv=v_w_ukv, w_out=v_w_out, w_ffn_in=v_w_ffn_in,
               w_ffn_out=v_w_ffn_out, final_norm_g=v_final_norm_g)
    order = list(weights)

    ns_ada = w_ada.shape[2]
    c_all = _all_gather("gather_c", c, ALL_MASKS, True).reshape(8, D)
    b_sh = lax.dynamic_slice_in_dim(b_ada, chip * ns_ada, ns_ada, axis=1).reshape(2, 1, ns_ada)
    mod_sh = _ada_fwd("ada_fwd", c_all, w_ada, b_sh)
    mod_all = _all_gather("gather_mod", mod_sh.reshape(16, ns_ada), CHIP_MASKS, False)
    mod_mine = lax.dynamic_index_in_dim(mod_all.reshape(4, 2, 8, ns_ada), dev, axis=2, keepdims=False)
    mod = jnp.transpose(mod_mine, (1, 0, 2)).reshape(2, 6, 1, D)

    flat = lambda w: w.astype(bf16).reshape(w.shape[0], w.shape[1], -1)
    b_in, b_uq, b_ukv = flat(w_in), flat(w_uq), flat(w_ukv)
    late = [b_in[1], b_uq[1], b_ukv[1]] + [flat(w).reshape(-1, w.shape[-1]) for w in (w_out, w_ffn_in, w_ffn_out)]
    W0 = _inproj_weights(*_gather_weights("gather_weights_early", [b_in[0], b_uq[0], b_ukv[0]]))
    W = {n: [W0[n], None] for n in W0}
    wa_b = lru_wa.astype(bf16)
    wx_b = lru_wx.astype(bf16)

    pos = positions.reshape(T, 1)
    invf = ROPE_THETA ** (-jnp.arange(0, DR, 2, dtype=f32) / DR)
    invf_p = jnp.concatenate([jnp.zeros((DN,), f32), invf, invf, jnp.zeros((DH - DN - DR,), f32)]).reshape(1, DH)
    cosp, sap, sbp = _rope_tables("rope_tables", pos, invf_p, T, tb)

    ns_cw = conv_w.shape[2]
    cw_all = _all_gather("gather_conv_w", conv_w.reshape(2 * CONV, ns_cw), CHIP_MASKS, False)
    cw_full = jnp.transpose(cw_all, (1, 0, 2)).reshape(2, CONV, D)
    vec = lambda a, l: a[l].reshape(1, -1)

    xl = x.reshape(T, D)
    saved = []
    h1 = _norm_mod("norm_mod_0", xl, mod[0, 1], mod[0, 0], T, tb)
    for l in range(2):
        sh_m, sc_m, g_m, sh_f, sc_f, g_f = [mod[l, j] for j in range(6)]
        proj = _matmul(f"proj_{l}", h1, W["win"][l], "nn", bf16)
        u, r, ig, ya = _lru_fwd(f"lru_fwd_{l}", proj, pos, cw_full[l], vec(conv_b, l), wa_b[l], vec(lru_ba, l),
                                wx_b[l], vec(lru_bx, l), vec(lru_a_param, l), T, tb)
        qp, kp, vp, cq, ckv = _mla_prep(f"mla_prep_{l}", proj, cosp, sap, sbp, vec(q_norm_g, l), vec(kv_norm_g, l),
                                        W["wq"][l], W["wk"][l], W["wv"][l], T, tb)
        if l == 0:
            o, lse, *late_all = _flash_fwd("flash_fwd_0", qp, kp, vp, T, tq, late)
            W1 = _inproj_weights(*late_all[:3])
            for n in W1:
                W[n][1] = W1[n]
            W.update(_other_weights(*late_all[3:]))
        else:
            o, lse = _flash_fwd("flash_fwd_1", qp, kp, vp, T, tq)
        y = _gate_mix(f"gate_mix_{l}", proj, ya, o, T, tb)
        mo = _matmul(f"out_proj_{l}", y, W["wout"][l], "nn")
        x2, h2 = _resid_norm_mod(f"resid_norm_f_{l}", xl, g_m, mo, sc_f, sh_f, T, tb)
        gu, act = _ffn_in_swiglu(f"ffn_in_{l}", h2, W["wfi"][l], T)
        fo = _matmul(f"ffn_out_{l}", act, W["wfo"][l], "nn")
        saved.append(dict(x=xl, h1=h1, proj=proj, u=u, r=r, ig=ig, ya=ya, qp=qp, kp=kp, vp=vp, cq=cq, ckv=ckv, o=o,
                          lse=lse, y=y, mo=mo, x2=x2, h2=h2, gu=gu, act=act, fo=fo))
        if l == 0:
            xl, h1 = _resid_norm_mod("resid_norm_m_1", x2, g_f, fo, mod[1, 1], mod[1, 0], T, tb)

    dx, dfo, loss_part, dgfin, dg_f = _final("final", saved[1]["x2"], mod[1, 5], saved[1]["fo"],
                                             final_norm_g.reshape(1, D), loss_target.reshape(T, D), T, tb)
    loss = lax.psum(loss_part[0, 0], ("x", "y", "c"))

    gl = {n: [None, None] for n in ("win", "wq", "wk", "wv", "wout", "wfi", "wfo", "conv_w", "conv_b", "lru_wa",
                                    "lru_ba", "lru_wx", "lru_bx", "lru_a_param", "q_norm_g", "kv_norm_g")}
    dmod = [None, None]

    def pair_sums(tag, arrs):
        return list(_pair_sum(f"rs_pair_sum_{tag}", arrs, _swap_halves(f"rs_swap_halves_{tag}", arrs)))

    def inproj_pieces(l):
        return _shard_grads_inproj(gl["win"][l], gl["wq"][l], gl["wk"][l], gl["wv"][l])

    def rest_pieces(l):
        return _shard_grads_rest(gl["wout"][l], gl["wfi"][l], gl["wfo"][l])

    for l in (1, 0):
        s = saved[l]
        sh_m, sc_m, g_m, sh_f, sc_f, g_f = [mod[l, j] for j in range(6)]
        dgu = _dact_swiglu_bwd(f"d_act_{l}", dfo, W["wfo"][l], s["gu"], T)
        gl["wfo"][l] = _matmul(f"dw_ffn_out_{l}", s["act"], dfo, "tn", bf16)
        dh2 = _matmul(f"d_h2_{l}", dgu, W["wfi"][l], "nt")
        gl["wfi"][l] = _matmul(f"dw_ffn_in_{l}", s["h2"], dgu, "tn", bf16)
        dx2, dmo, dsc_f, dsh_f, dg_m = _norm_mod_bwd(f"norm_bwd_f_{l}", dh2, s["x2"], sc_f, dx, T, tb, (g_m, s["mo"]))
        dy = _matmul(f"d_y_{l}", dmo, W["wout"][l], "nt")
        gl["wout"][l] = _matmul(f"dw_out_{l}", s["y"], dmo, "tn", bf16)
        dga, dgb, dya, do, dlt = _gate_mix_bwd(f"gate_mix_bwd_{l}", dy, s["proj"], s["ya"], s["o"], T, tb)
        lse_r = s["lse"].reshape(NH, nq, 1, tq)
        dl_r = jnp.transpose(dlt).reshape(NH, nq, 1, tq)
        if l == 0:
            sums0a = pair_sums("0a", rest_pieces(0))
            dk, dv, dq, *lands = _flash_bwd("flash_bwd_0", s["qp"], s["kp"], s["vp"], do, lse_r, dl_r, T, tq,
                                            sums1 + sums0a)
            rs_bufs = list(_chip_sum("rs_chip_sum_1", sums1, lands[:6], 1))
            rs_bufs[3:] = _chip_sum("rs_chip_sum_0a", sums0a, lands[6:], 0, rs_bufs[3:])
        else:
            dk, dv, dq = _flash_bwd("flash_bwd_1", s["qp"], s["kp"], s["vp"], do, lse_r, dl_r, T, tq)
        dqkv, dqp, dkp, dgq, dgkv = _mla_bwd(f"mla_bwd_{l}", dq, dk, dv, s["proj"], cosp, sap, sbp,
                                              vec(q_norm_g, l), vec(kv_norm_g, l), W["wq"][l], W["wk"][l], W["wv"][l],
                                              T, tb)
        gl["wq"][l] = _matmul(f"dw_uq_{l}", s["cq"], dqp, "tn", bf16)
        gl["wk"][l] = _matmul(f"dw_uk_{l}", s["ckv"], dkp, "tn", bf16)
        gl["wv"][l] = _matmul(f"dw_uv_{l}", s["ckv"], dv, "tn", bf16)
        dxl, dcw, dcb, dwa, dba, dwx, dbx, dsp = _lru_bwd(
            f"lru_bwd_{l}", dya, s["proj"], s["u"], s["r"], s["ig"], s["ya"], pos, cw_full[l], wa_b[l], wx_b[l],
            vec(lru_a_param, l), T, tb)
        dproj = jnp.concatenate([dxl, dga, dgb, dqkv], axis=1)
        gl["win"][l] = _matmul(f"dw_in_{l}", s["h1"], dproj, "tn", bf16)
        gl["conv_w"][l], gl["conv_b"][l] = dcw, dcb[0]
        gl["lru_wa"][l], gl["lru_ba"][l], gl["lru_wx"][l], gl["lru_bx"][l] = dwa, dba[0], dwx, dbx[0]
        gl["lru_a_param"][l] = dsp[0]
        gl["q_norm_g"][l], gl["kv_norm_g"][l] = dgq[0], dgkv[0]
        if l == 1:
            dh1 = _matmul("d_h1_1", dproj, W["win"][l], "nt")
            dx, dfo_below, dsc_m, dsh_m, dg_f_below = _norm_mod_bwd(
                "norm_bwd_m_1", dh1, s["x"], sc_m, dx2, T, tb, (mod[0, 5], saved[0]["fo"]))
            sums1 = pair_sums("1", inproj_pieces(1) + rest_pieces(1))
        else:
            gfull = {n: jnp.stack(gl[n]) for n in SMALL if n != "final_norm_g"}
            gfull["final_norm_g"] = dgfin[0]
            gsmall = _pack_rows([gfull[n] for n in SMALL], SMALL_ROWS, f32).reshape(4, SMALL_ROWS // 4, 1024)
            sums0b = pair_sums("0b", inproj_pieces(0) + [gsmall])
            dh1, *lands0b = _matmul("d_h1_0", dproj, W["win"][l], "nt", scatter=sums0b)
            dx, dsc_m, dsh_m = _norm_mod_bwd("norm_bwd_m_0", dh1, s["x"], sc_m, dx2, T, tb)
        dmod[l] = jnp.concatenate([dsh_m, dsc_m, dg_m, dsh_f, dsc_f, dg_f], axis=1)
        if l == 1:
            dfo, dg_f = dfo_below, dg_f_below
    grad_x = dx.reshape(1, T, D)

    dmod_all = _all_gather("gather_dmod", jnp.concatenate(dmod, axis=1), ALL_MASKS, True).reshape(8, 2, 6, D)
    dmod_sum = _sum_slots("sum_dmod", dmod_all.reshape(8, 12, D)).reshape(2, 6 * D)
    dm_sh = lax.dynamic_slice_in_dim(jnp.transpose(dmod_all.reshape(8, 2, 6 * D), (1, 0, 2)), chip * ns_ada, ns_ada, axis=2)
    grads = {"w_ada": _ada_bwd("ada_bwd", jnp.transpose(c_all), dm_sh), "b_ada": dmod_sum}

    rs_bufs[:3] = _chip_sum("rs_chip_sum_0b", sums0b[:3], lands0b[:3], 0, rs_bufs[:3])
    tot_s = _chip_sum("rs_chip_sum_small", sums0b[3:], lands0b[3:], None)[0]
    *bufs, small_all = _join_halves("rs_join_halves", rs_bufs, tot_s)
    small_all = small_all.reshape(SMALL_ROWS, 1024)
    for n, buf in zip(BIG, bufs):
        grads[n] = buf.reshape(weights[n].shape)
    for n, gs in zip(SMALL, _unpack_rows(small_all, [SMALL_SHAPES[n] for n in SMALL])):
        grads[n] = gs
    grads["conv_w"] = lax.dynamic_slice_in_dim(grads["conv_w"], chip * ns_cw, ns_cw, axis=2)

    deltas, new_m, new_v = {}, {}, {}
    for n in order:
        w = weights[n]
        view = (1, w.shape[0]) if w.ndim == 1 else (_size(w.shape[:-1]), w.shape[-1]) if w.shape[-1] >= 128 else (w.shape[0], _size(w.shape[1:]))
        d, m2, v2 = _adamw(f"adamw_{n}", w.reshape(view), grads[n].reshape(view), mom[n].reshape(view), var[n].reshape(view))
        deltas[n], new_m[n], new_v[n] = d.reshape(w.shape), m2.reshape(w.shape), v2.reshape(w.shape)
    return (loss, grad_x, *[grads[n] for n in order], *[deltas[n] for n in order], *[new_m[n] for n in order],
            *[new_v[n] for n in order])
```

```python
import functools

import jax
import jax.numpy as jnp
from jax import lax
from jax.experimental import pallas as pl
from jax.experimental.pallas import tpu as pltpu

f32 = jnp.float32
bf16 = jnp.bfloat16

D = 1024
NH = 8
DN = 128
DR = 64
DV = 128
DH = 256
QR = 256
KVR = 128
DFF = 2816
FH = DFF // 2
FFN_SPLIT = 768
NB = 8
BW = 128
CONV = 4
PW = 3584
QKV0 = 3 * D
DIN = 3520
EPS = 1e-6
LRU_C = 8.0
ROPE_THETA = 10000.0
SCALE = (DN + DR) ** -0.5
LOG2E = 1.4426950408889634
LN2 = 0.6931471805599453
QSCALE = SCALE * LOG2E
NEG = -1e30
HPS = 2
HPS_BWD = 2

ADAM_LR = 0.001
ADAM_B1 = 0.9
ADAM_B2 = 0.999
ADAM_EPS = 1e-08
ADAM_WD = 0.01
ADAM_STEP = 10

VMEM_LIMIT = 56 * 1024 * 1024
ROW_BLOCK = 256
ATTN_BLOCK = 1024
MESH = pl.DeviceIdType.MESH
ANY = pl.BlockSpec(memory_space=pl.ANY)
CHIP_MASKS = ((1, 0, 0), (0, 1, 0), (1, 1, 0))
ALL_MASKS = ((0, 0, 1), (1, 0, 0), (0, 1, 0), (1, 1, 0), (1, 0, 1), (0, 1, 1), (1, 1, 1))


def _cparams(n_axes):
    return pltpu.CompilerParams(dimension_semantics=("arbitrary",) * n_axes, vmem_limit_bytes=VMEM_LIMIT)


def _pick(n, cands):
    for c in cands:
        if n % c == 0:
            return c
    return n


def _rowwise(name, body, T, tb, row_ins, vec_ins, row_outs, acc_outs=(), scratch=(), reverse=False):
    n = T // tb

    def rmap(i):
        return (n - 1 - i) if reverse else i

    in_specs = []
    for (_, w, cb) in row_ins:
        in_specs.append(pl.BlockSpec((tb, w), functools.partial(lambda i, cb: (rmap(i), cb), cb=cb)))
    for v in vec_ins:
        in_specs.append(pl.BlockSpec(v.shape, functools.partial(lambda i, nd: (0,) * nd, nd=v.ndim)))
    out_specs, out_shape = [], []
    for (w, dt) in row_outs:
        out_specs.append(pl.BlockSpec((tb, w), lambda i: (rmap(i), 0)))
        out_shape.append(jax.ShapeDtypeStruct((T, w), dt))
    for (s, dt) in acc_outs:
        out_specs.append(pl.BlockSpec(s, functools.partial(lambda i, nd: (0,) * nd, nd=len(s))))
        out_shape.append(jax.ShapeDtypeStruct(s, dt))

    def kern(*refs):
        body(*refs)

    return pl.pallas_call(
        kern, name=name, grid=(n,), in_specs=in_specs, out_specs=out_specs, out_shape=out_shape,
        scratch_shapes=list(scratch), compiler_params=_cparams(1),
    )(*[a for (a, _, _) in row_ins], *vec_ins)


def _matmul(name, a, b, mode, out_dtype=f32, scatter=None):
    if mode == "nn":
        (M, K), N = a.shape, b.shape[1]
    elif mode == "nt":
        (M, K), N = a.shape, b.shape[0]
    else:
        (K, M), N = a.shape, b.shape[1]
    wide = (1792, 1408, 1024, 512, 256, 128)
    if mode == "tn":
        tm = _pick(M, (1024, 1408, 512, 256, 128))
    else:
        tm = _pick(M, (1024, 512, 256, 128) if K <= 3584 else (512, 256, 128))
    tn = _pick(N, wide)
    tk = K if mode != "tn" else _pick(K, (2048, 1024, 512, 256, 128))
    nk = K // tk
    dims = {"nn": (((1,), (0,)), ((), ())), "nt": (((1,), (1,)), ((), ())), "tn": (((0,), (0,)), ((), ()))}[mode]
    if mode == "tn":
        a_spec = pl.BlockSpec((tk, tm), lambda j, i, k: (k, i))
    else:
        a_spec = pl.BlockSpec((tm, tk), lambda j, i, k: (i, k))
    once = {"pipeline_mode": pl.Buffered(1)} if (mode != "tn" and N == tn) else {}
    if mode == "nt":
        b_spec = pl.BlockSpec((tn, tk), lambda j, i, k: (j, k), **once)
    else:
        b_spec = pl.BlockSpec((tk, tn), lambda j, i, k: (k, j), **once)
    o_spec = pl.BlockSpec((tm, tn), lambda j, i, k: (i, j))
    use_acc = nk > 1 and out_dtype != f32

    def kern(a_ref, b_ref, o_ref, *scr):
        k = pl.program_id(2)
        d = lax.dot_general(a_ref[...].astype(bf16), b_ref[...].astype(bf16), dims, preferred_element_type=f32)
        if nk == 1:
            o_ref[...] = d.astype(out_dtype)
        else:
            acc = scr[0] if use_acc else o_ref

            @pl.when(k == 0)
            def _():
                acc[...] = d

            @pl.when(k > 0)
            def _():
                acc[...] += d

            if use_acc:
                @pl.when(k == nk - 1)
                def _():
                    o_ref[...] = acc[...].astype(out_dtype)

    grid = (N // tn, M // tm, nk)
    scratch = [pltpu.VMEM((tm, tn), f32)] if use_acc else []
    if scatter is None:
        return pl.pallas_call(
            kern, name=name, grid=grid, in_specs=[a_spec, b_spec], out_specs=o_spec,
            out_shape=jax.ShapeDtypeStruct((M, N), out_dtype), scratch_shapes=scratch, compiler_params=_cparams(3),
        )(a, b)

    ns = len(scatter)

    def kern_with_scatter(*refs):
        a_ref, b_ref, s_in, o_ref, s_out = refs[0], refs[1], refs[2:2 + ns], refs[2 + ns], refs[3 + ns:3 + 2 * ns]
        scr, sems = refs[3 + 2 * ns:3 + 2 * ns + len(scratch)], refs[3 + 2 * ns + len(scratch):]
        at = [pl.program_id(d) for d in range(3)]

        @pl.when((at[0] == 0) & (at[1] == 0) & (at[2] == 0))
        def _():
            for cp in _scatter_copies(s_in, s_out, *sems):
                cp.start()

        kern(a_ref, b_ref, o_ref, *scr)

        @pl.when((at[0] == grid[0] - 1) & (at[1] == grid[1] - 1) & (at[2] == grid[2] - 1))
        def _():
            for cp in _scatter_copies(s_in, s_out, *sems):
                cp.wait()

    return pl.pallas_call(
        kern_with_scatter, name=name, grid=grid, in_specs=[a_spec, b_spec] + [ANY] * ns,
        out_specs=[o_spec] + [ANY] * ns,
        out_shape=[jax.ShapeDtypeStruct((M, N), out_dtype)] + _scatter_shapes(scatter),
        scratch_shapes=scratch + _scatter_sems(scatter), compiler_params=_cparams(3),
    )(a, b, *scatter)


def _colsum(v):
    return jnp.sum(v, axis=0, keepdims=True)


def _init_acc(step, *refs):
    @pl.when(step == 0)
    def _():
        for r in refs:
            r[...] = jnp.zeros_like(r)


def _norm_mod(name, x, sc, sh, T, tb):
    def body(x_ref, sc_ref, sh_ref, h_ref):
        xv = x_ref[...]
        rstd = lax.rsqrt(jnp.mean(xv * xv, axis=-1, keepdims=True) + EPS)
        h_ref[...] = (xv * rstd * (1.0 + sc_ref[...]) + sh_ref[...]).astype(bf16)

    return _rowwise(name, body, T, tb, [(x, D, 0)], [sc, sh], [(D, bf16)])[0]


def _resid_norm_mod(name, x, g, m, sc, sh, T, tb):
    def body(x_ref, m_ref, g_ref, sc_ref, sh_ref, x2_ref, h_ref):
        xv = x_ref[...] + g_ref[...] * m_ref[...]
        x2_ref[...] = xv
        rstd = lax.rsqrt(jnp.mean(xv * xv, axis=-1, keepdims=True) + EPS)
        h_ref[...] = (xv * rstd * (1.0 + sc_ref[...]) + sh_ref[...]).astype(bf16)

    return _rowwise(name, body, T, tb, [(x, D, 0), (m, D, 0)], [g, sc, sh], [(D, f32), (D, bf16)])


def _norm_mod_bwd(name, dh, x, sc, dres, T, tb, resid=None):
    def norm_part(dh_ref, x_ref, dr_ref, sc_ref, dx_ref, dsc_ref, dsh_ref):
        xv = x_ref[...]
        dhv = dh_ref[...]
        rstd = lax.rsqrt(jnp.mean(xv * xv, axis=-1, keepdims=True) + EPS)
        nv = xv * rstd
        dn = dhv * (1.0 + sc_ref[...])
        dxv = dr_ref[...] + rstd * (dn - nv * jnp.mean(dn * nv, axis=-1, keepdims=True))
        dx_ref[...] = dxv
        dsc_ref[...] += _colsum(dhv * nv)
        dsh_ref[...] += _colsum(dhv)
        return dxv

    if resid is None:
        def body(dh_ref, x_ref, dr_ref, sc_ref, dx_ref, dsc_ref, dsh_ref):
            _init_acc(pl.program_id(0), dsc_ref, dsh_ref)
            norm_part(dh_ref, x_ref, dr_ref, sc_ref, dx_ref, dsc_ref, dsh_ref)

        return _rowwise(name, body, T, tb, [(dh, D, 0), (x, D, 0), (dres, D, 0)], [sc], [(D, f32)],
                        [((1, D), f32)] * 2)

    def body_resid(dh_ref, x_ref, dr_ref, m_ref, sc_ref, g_ref, dx_ref, dm_ref, dsc_ref, dsh_ref, dg_ref):
        _init_acc(pl.program_id(0), dsc_ref, dsh_ref, dg_ref)
        dxv = norm_part(dh_ref, x_ref, dr_ref, sc_ref, dx_ref, dsc_ref, dsh_ref)
        dm_ref[...] = (dxv * g_ref[...]).astype(bf16)
        dg_ref[...] += _colsum(dxv * m_ref[...])

    g, m = resid
    return _rowwise(name, body_resid, T, tb, [(dh, D, 0), (x, D, 0), (dres, D, 0), (m, D, 0)], [sc, g],
                    [(D, f32), (D, bf16)], [((1, D), f32)] * 3)


def _expm1_neg(y):
    poly = y * (1.0 + y * (0.5 + y * (1.0 / 6 + y * (1.0 / 24))))
    return jnp.where(y > -0.05, poly, jnp.exp(y) - 1.0)


def _lru_gates(r, sp, reset):
    log_a = -LRU_C * r * sp
    a = jnp.where(reset, 0.0, jnp.exp(log_a))
    mult = jnp.where(reset, 1.0, jnp.sqrt(-_expm1_neg(2.0 * log_a)))
    return a, mult


def _block_dot(v, w_ref, dims):
    outs = [lax.dot_general(v[:, n * BW:(n + 1) * BW], w_ref[n], dims, preferred_element_type=f32) for n in range(NB)]
    return jnp.concatenate(outs, axis=1)


def _lru_fwd(name, proj, pos, cw, cb, wa, ba, wx, bx, ap, T, tb):
    def body(x_ref, pos_ref, cw_ref, cb_ref, wa_ref, ba_ref, wx_ref, bx_ref, ap_ref,
             u_ref, r_ref, i_ref, h_ref, prev_x, carry_h):
        _init_acc(pl.program_id(0), prev_x, carry_h)
        x = x_ref[...].astype(f32)
        px = prev_x[...]
        row = lax.broadcasted_iota(jnp.int32, (tb, 1), 0)
        u = x * cw_ref[CONV - 1:CONV, :] + cb_ref[...]
        for j in range(1, CONV):
            z = jnp.where(row >= tb - j, px, x)
            u = u + pltpu.roll(z, j, 0) * cw_ref[CONV - 1 - j:CONV - j, :]
        ub = u.astype(bf16)
        nn = (((1,), (0,)), ((), ()))
        r = jax.nn.sigmoid(_block_dot(ub, wa_ref, nn) + ba_ref[...])
        ig = jax.nn.sigmoid(_block_dot(ub, wx_ref, nn) + bx_ref[...])
        sp = jax.nn.softplus(-ap_ref[...])
        reset = pos_ref[...] == 0
        a, mult = _lru_gates(r, sp, reset)
        b = u * ig * mult
        sub = row & 7
        for s in (1, 2, 4):
            keep = sub >= s
            a_sh = jnp.where(keep, pltpu.roll(a, s, 0), 1.0)
            b_sh = jnp.where(keep, pltpu.roll(b, s, 0), 0.0)
            b = a * b_sh + b
            a = a * a_sh
        carry = carry_h[...]
        for g in range(tb // 8):
            rows = slice(8 * g, 8 * g + 8)
            hg = b[rows] + a[rows] * carry
            h_ref[rows, :] = hg
            carry = hg[7:8]
        carry_h[...] = carry
        u_ref[...] = u
        r_ref[...] = r
        i_ref[...] = ig
        prev_x[...] = x

    return _rowwise(name, body, T, tb, [(proj, D, 0), (pos, 1, 0)], [cw, cb, wa, ba, wx, bx, ap],
                    [(D, f32)] * 4, scratch=[pltpu.VMEM((tb, D), f32), pltpu.VMEM((1, D), f32)])


def _lru_bwd(name, dh, proj, u, r, ig, h, pos, cw, wa, wx, ap, T, tb):
    n = T // tb
    t8 = tb // 8

    def rmap(i):
        return n - 1 - i

    def body(dh_ref, x_ref, u_ref, r_ref, i_ref, h_ref, hp_ref, pos_ref, cw_ref, wa_ref, wx_ref, ap_ref,
             dx_ref, dcw_ref, dcb_ref, dwa_ref, dba_ref, dwx_ref, dbx_ref, dsp_ref, next_du, carry_g, carry_a):
        step = pl.program_id(0)
        _init_acc(step, dcw_ref, dcb_ref, dwa_ref, dba_ref, dwx_ref, dbx_ref, dsp_ref, next_du, carry_g, carry_a)
        blk = n - 1 - step
        x = x_ref[...].astype(f32)
        u = u_ref[...]
        r = r_ref[...]
        ig = i_ref[...]
        h = h_ref[...]
        row = lax.broadcasted_iota(jnp.int32, (tb, 1), 0)
        sp = jax.nn.softplus(-ap_ref[...])
        reset = pos_ref[...] == 0
        a, mult = _lru_gates(r, sp, reset)
        A = pltpu.roll(jnp.where(row == 0, carry_a[...], a), tb - 1, 0)
        B = dh_ref[...]
        sub = row & 7
        for s in (1, 2, 4):
            keep = sub < 8 - s
            a_sh = jnp.where(keep, pltpu.roll(A, tb - s, 0), 1.0)
            b_sh = jnp.where(keep, pltpu.roll(B, tb - s, 0), 0.0)
            B = B + A * b_sh
            A = A * a_sh
        carry = carry_g[...]
        parts = [None] * (tb // 8)
        for grp in reversed(range(tb // 8)):
            rows = slice(8 * grp, 8 * grp + 8)
            parts[grp] = B[rows] + A[rows] * carry
            carry = parts[grp][0:1]
        g = jnp.concatenate(parts, axis=0)
        carry_g[...] = carry
        carry_a[...] = a[0:1, :]
        h_last = jnp.where(blk == 0, 0.0, hp_ref[7:8, :])
        h_prev = pltpu.roll(jnp.where(row == tb - 1, h_last, h), 1, 0)
        da = g * h_prev
        gu = g * u
        dlog_a = jnp.where(reset, 0.0, da * a - gu * ig * (a * a) / mult)
        du = g * ig * mult
        di = gu * mult
        dr = dlog_a * (-LRU_C * sp)
        dsp_ref[...] += _colsum(dlog_a * (-LRU_C * r)) * (-jax.nn.sigmoid(-ap_ref[...]))
        dzr = dr * r * (1.0 - r)
        dzi = di * ig * (1.0 - ig)
        dzr_b = dzr.astype(bf16)
        dzi_b = dzi.astype(bf16)
        nt = (((1,), (1,)), ((), ()))
        du = du + _block_dot(dzr_b, wa_ref, nt) + _block_dot(dzi_b, wx_ref, nt)
        ub = u.astype(bf16)
        tn = (((0,), (0,)), ((), ()))
        for nb in range(NB):
            sl = slice(nb * BW, (nb + 1) * BW)
            dwa_ref[nb] += lax.dot_general(ub[:, sl], dzr_b[:, sl], tn, preferred_element_type=f32)
            dwx_ref[nb] += lax.dot_general(ub[:, sl], dzi_b[:, sl], tn, preferred_element_type=f32)
        dba_ref[...] += _colsum(dzr)
        dbx_ref[...] += _colsum(dzi)
        dcb_ref[...] += _colsum(du)
        ndu = next_du[...]
        dx = du * cw_ref[CONV - 1:CONV, :]
        dcw_ref[CONV - 1:CONV, :] += _colsum(x * du)
        for j in range(1, CONV):
            up = pltpu.roll(jnp.where(row < j, ndu, du), tb - j, 0)
            dx = dx + up * cw_ref[CONV - 1 - j:CONV - j, :]
            dcw_ref[CONV - 1 - j:CONV - j, :] += _colsum(x * up)
        dx_ref[...] = dx.astype(bf16)
        next_du[...] = du

    row_specs = [pl.BlockSpec((tb, D), lambda i: (rmap(i), 0)) for _ in range(6)]
    in_specs = row_specs + [
        pl.BlockSpec((8, D), lambda i: (jnp.maximum(rmap(i) * t8 - 1, 0), 0)),
        pl.BlockSpec((tb, 1), lambda i: (rmap(i), 0)),
        pl.BlockSpec((CONV, D), lambda i: (0, 0)),
        pl.BlockSpec((NB, BW, BW), lambda i: (0, 0, 0)),
        pl.BlockSpec((NB, BW, BW), lambda i: (0, 0, 0)),
        pl.BlockSpec((1, D), lambda i: (0, 0)),
    ]
    vec = lambda s: pl.BlockSpec(s, functools.partial(lambda i, nd: (0,) * nd, nd=len(s)))
    acc_shapes = [(CONV, D), (1, D), (NB, BW, BW), (1, D), (NB, BW, BW), (1, D), (1, D)]
    return pl.pallas_call(
        body, name=name, grid=(n,), in_specs=in_specs,
        out_specs=[pl.BlockSpec((tb, D), lambda i: (rmap(i), 0))] + [vec(s) for s in acc_shapes],
        out_shape=[jax.ShapeDtypeStruct((T, D), bf16)] + [jax.ShapeDtypeStruct(s, f32) for s in acc_shapes],
        scratch_shapes=[pltpu.VMEM((tb, D), f32), pltpu.VMEM((1, D), f32), pltpu.VMEM((1, D), f32)],
        compiler_params=_cparams(1),
    )(dh, proj, u, r, ig, h, h, pos, cw, wa, wx, ap)


def _rope_tables(name, pos, invf, T, tb, pack=None):
    def body(pos_ref, invf_ref, cos_ref, sa_ref, sb_ref):
        ang = pos_ref[...].astype(f32) * invf_ref[...]
        lane = lax.broadcasted_iota(jnp.int32, (tb, DH), 1)
        first = (lane >= DN) & (lane < DN + DR // 2)
        second = (lane >= DN + DR // 2) & (lane < DN + DR)
        c = jnp.cos(ang)
        s = jnp.sin(ang)
        cos_ref[...] = jnp.where(lane < DN, 1.0, jnp.where(first | second, c, 0.0))
        sa_ref[...] = jnp.where(first, -s, 0.0)
        sb_ref[...] = jnp.where(second, s, 0.0)

    if pack is None:
        return _rowwise(name, body, T, tb, [(pos, 1, 0)], [invf], [(DH, f32)] * 3)

    nw = len(pack)
    n = T // tb

    def body_with_gather(pos_ref, invf_ref, *rest):
        w_refs, tabs, wall_refs, sems = rest[:nw], rest[nw:nw + 3], rest[nw + 3:2 * nw + 3], rest[2 * nw + 3:]
        i = pl.program_id(0)

        @pl.when(i == 0)
        def _():
            mine, first, _ = _gather_copies(w_refs, wall_refs, *sems)
            _gather_start(mine, first)

        body(pos_ref, invf_ref, *tabs)

        @pl.when(i == n - 1)
        def _():
            mine, first, passed = _gather_copies(w_refs, wall_refs, *sems)
            _gather_forward(first, passed)
            _gather_finish(mine, first, passed)

    return pl.pallas_call(
        body_with_gather, name=name, grid=(n,),
        in_specs=[pl.BlockSpec((tb, 1), lambda i: (i, 0)), pl.BlockSpec(invf.shape, lambda i: (0, 0))] + [ANY] * nw,
        out_specs=[pl.BlockSpec((tb, DH), lambda i: (i, 0))] * 3 + [ANY] * nw,
        out_shape=[jax.ShapeDtypeStruct((T, DH), f32)] * 3 + [jax.ShapeDtypeStruct((4,) + a.shape, a.dtype) for a in pack],
        scratch_shapes=_gather_sems(pack), compiler_params=_cparams(1),
    )(pos, invf, *pack)


def _rope_heads(z, cos, sa, sb, transposed=False, shared=None):
    half = DR // 2
    parts = []
    for hd in range(NH):
        hi = z[:, hd * DH + DN:(hd + 1) * DH]
        if shared is not None:
            hi = hi + shared
        if transposed:
            hi = hi * cos + pltpu.roll(hi * sa, half, 1) + pltpu.roll(hi * sb, DN - half, 1)
        else:
            hi = hi * cos + pltpu.roll(hi, DN - half, 1) * sa + pltpu.roll(hi, half, 1) * sb
        parts += [z[:, hd * DH:hd * DH + DN], hi]
    return parts


def _mla_prep(name, proj, cosp, sap, sbp, gq, gkv, wq, wk, wv, T, tb):
    def body(qkv_ref, cos_ref, sa_ref, sb_ref, gq_ref, gkv_ref, wq_ref, wk_ref, wv_ref,
             q_ref, k_ref, v_ref, cq_ref, ckv_ref):
        qkv = qkv_ref[...].astype(f32)
        qd = qkv[:, :QR]
        kvd = qkv[:, QR:QR + KVR]
        slab = qkv[:, QR + KVR:]
        nq = qd * lax.rsqrt(jnp.mean(qd * qd, axis=-1, keepdims=True) + EPS)
        cq = (nq * gq_ref[...]).astype(bf16)
        cq_scaled = (nq * (gq_ref[...] * QSCALE)).astype(bf16)
        ckv = (kvd * lax.rsqrt(jnp.mean(kvd * kvd, axis=-1, keepdims=True) + EPS) * gkv_ref[...]).astype(bf16)
        cos, sa, sb = cos_ref[:, DN:], sa_ref[:, DN:], sb_ref[:, DN:]
        qpre = jnp.dot(cq_scaled, wq_ref[...], preferred_element_type=f32)
        kpre = jnp.dot(ckv, wk_ref[...], preferred_element_type=f32)
        q_ref[...] = jnp.concatenate(_rope_heads(qpre, cos, sa, sb), axis=1).astype(bf16)
        k_ref[...] = jnp.concatenate(_rope_heads(kpre, cos, sa, sb, shared=slab), axis=1).astype(bf16)
        v_ref[...] = jnp.dot(ckv, wv_ref[...], preferred_element_type=f32).astype(bf16)
        cq_ref[...] = cq
        ckv_ref[...] = ckv

    return _rowwise(name, body, T, tb,
                    [(proj, 512, QKV0 // 512), (cosp, DH, 0), (sap, DH, 0), (sbp, DH, 0)],
                    [gq, gkv, wq, wk, wv],
                    [(NH * DH, bf16), (NH * DH, bf16), (NH * DV, bf16), (QR, bf16), (KVR, bf16)])


def _mla_bwd(name, dq, dk, dv, proj, cosp, sap, sbp, gq, gkv, wq, wk, wv, T, tb):
    def body(dq_ref, dk_ref, dv_ref, qkv_ref, cos_ref, sa_ref, sb_ref, gq_ref, gkv_ref, wq_ref, wk_ref, wv_ref,
             dqkv_ref, dqp_ref, dkp_ref, dgq_ref, dgkv_ref):
        _init_acc(pl.program_id(0), dgq_ref, dgkv_ref)
        cos, sa, sb = cos_ref[:, DN:], sa_ref[:, DN:], sb_ref[:, DN:]
        dqp_b = jnp.concatenate(_rope_heads(dq_ref[...] * SCALE, cos, sa, sb, transposed=True), axis=1).astype(bf16)
        dk_parts = _rope_heads(dk_ref[...], cos, sa, sb, transposed=True)
        dkp_b = jnp.concatenate(dk_parts, axis=1).astype(bf16)
        dqp_ref[...] = dqp_b
        dkp_ref[...] = dkp_b
        nt = (((1,), (1,)), ((), ()))
        dcq = lax.dot_general(dqp_b, wq_ref[...], nt, preferred_element_type=f32)
        dckv = (lax.dot_general(dkp_b, wk_ref[...], nt, preferred_element_type=f32)
                + lax.dot_general(dv_ref[...], wv_ref[...], nt, preferred_element_type=f32))
        dslab = dk_parts[1]
        for hd in range(1, NH):
            dslab = dslab + dk_parts[2 * hd + 1]
        qkv = qkv_ref[...].astype(f32)
        qd = qkv[:, :QR]
        kvd = qkv[:, QR:QR + KVR]
        nq = qd * lax.rsqrt(jnp.mean(qd * qd, axis=-1, keepdims=True) + EPS)
        nkv = kvd * lax.rsqrt(jnp.mean(kvd * kvd, axis=-1, keepdims=True) + EPS)
        dgq_ref[...] += _colsum(dcq * nq)
        dgkv_ref[...] += _colsum(dckv * nkv)
        dnq = dcq * gq_ref[...]
        dnkv = dckv * gkv_ref[...]
        rq = lax.rsqrt(jnp.mean(qd * qd, axis=-1, keepdims=True) + EPS)
        rkv = lax.rsqrt(jnp.mean(kvd * kvd, axis=-1, keepdims=True) + EPS)
        dqd = rq * (dnq - nq * jnp.mean(dnq * nq, axis=-1, keepdims=True))
        dkvd = rkv * (dnkv - nkv * jnp.mean(dnkv * nkv, axis=-1, keepdims=True))
        dqkv_ref[...] = jnp.concatenate([dqd, dkvd, dslab], axis=1).astype(bf16)

    return _rowwise(name, body, T, tb,
                    [(dq, NH * DH, 0), (dk, NH * DH, 0), (dv, NH * DV, 0), (proj, 512, QKV0 // 512),
                     (cosp, DH, 0), (sap, DH, 0), (sbp, DH, 0)],
                    [gq, gkv, wq, wk, wv],
                    [(512, bf16), (NH * DH, bf16), (NH * DH, bf16)],
                    [((1, QR), f32), ((1, KVR), f32)])


def _flash_fwd(name, q, k, v, T, tq, pack=None):
    nq = T // tq
    unroll = 2

    def body(q_ref, k_ref, v_ref, o_ref, lse_ref):
        i = pl.program_id(1)
        nt = (((1,), (1,)), ((), ()))
        qs = [q_ref[:, hh * DH:(hh + 1) * DH] for hh in range(HPS)]

        def one(hh, j, carry, masked):
            m, l, acc = carry
            off = pl.multiple_of(j * tq, tq)
            kj = k_ref[pl.ds(off, tq), hh * DH:(hh + 1) * DH]
            vj = v_ref[pl.ds(off, tq), hh * DV:(hh + 1) * DV]
            s = lax.dot_general(qs[hh], kj, nt, preferred_element_type=f32)
            if masked:
                rr = lax.broadcasted_iota(jnp.int32, (tq, tq), 0)
                cc = lax.broadcasted_iota(jnp.int32, (tq, tq), 1)
                s = jnp.where(cc <= rr, s, NEG)
            m_new = jnp.maximum(m, jnp.max(s, axis=-1, keepdims=True))
            alpha = jnp.exp2(m - m_new)
            p = jnp.exp2(s - m_new)
            l = alpha * l + jnp.sum(p, axis=-1, keepdims=True)
            acc = alpha * acc + jnp.dot(p.astype(bf16), vj, preferred_element_type=f32)
            return m_new, l, acc

        def step(j, carries, masked):
            return tuple(one(hh, j, carries[hh], masked) for hh in range(HPS))

        def pair(jj, carries):
            for u in range(unroll):
                carries = step(jj * unroll + u, carries, False)
            return carries

        init1 = (jnp.full((tq, 1), NEG, f32), jnp.zeros((tq, 1), f32), jnp.zeros((tq, DV), f32))
        carry = lax.fori_loop(0, i // unroll, pair, tuple(init1 for _ in range(HPS)))
        carry = lax.fori_loop((i // unroll) * unroll, i, lambda j, c: step(j, c, False), carry)
        carry = step(i, carry, True)
        for hh in range(HPS):
            m, l, acc = carry[hh]
            o_ref[:, hh * DV:(hh + 1) * DV] = (acc / l).astype(bf16)
            lse_ref[hh] = m + jnp.log2(l)

    def body_with_gather(q_ref, k_ref, v_ref, *rest):
        nw = len(pack)
        w_refs, (o_ref, lse_ref), wall_refs, sems = rest[:nw], rest[nw:nw + 2], rest[nw + 2:2 * nw + 2], rest[2 * nw + 2:]
        h = pl.program_id(0)
        i = pl.program_id(1)

        @pl.when((h == 0) & (i == 0))
        def _():
            mine, first, _ = _gather_copies(w_refs, wall_refs, *sems)
            _gather_start(mine, first)

        @pl.when((h == ngrp // 2) & (i == 0))
        def _():
            _, first, passed = _gather_copies(w_refs, wall_refs, *sems)
            _gather_forward(first, passed)

        body(q_ref, k_ref, v_ref, o_ref, lse_ref)

        @pl.when((h == ngrp - 1) & (i == nq - 1))
        def _():
            _gather_finish(*_gather_copies(w_refs, wall_refs, *sems))

    ngrp = NH // HPS
    in_specs = [pl.BlockSpec((tq, HPS * DH), lambda h, i: (i, h)),
                pl.BlockSpec((T, HPS * DH), lambda h, i: (0, h), pipeline_mode=pl.Buffered(1)),
                pl.BlockSpec((T, HPS * DV), lambda h, i: (0, h), pipeline_mode=pl.Buffered(1))]
    out_specs = [pl.BlockSpec((tq, HPS * DV), lambda h, i: (i, h)), pl.BlockSpec((HPS, tq, 1), lambda h, i: (h, i, 0))]
    out_shape = [jax.ShapeDtypeStruct((T, NH * DV), bf16), jax.ShapeDtypeStruct((NH, T, 1), f32)]
    if pack is None:
        return pl.pallas_call(body, name=name, grid=(ngrp, nq), in_specs=in_specs, out_specs=out_specs,
                              out_shape=out_shape, compiler_params=_cparams(2))(q, k, v)
    return pl.pallas_call(
        body_with_gather, name=name, grid=(ngrp, nq), in_specs=in_specs + [ANY] * len(pack),
        out_specs=out_specs + [ANY] * len(pack),
        out_shape=out_shape + [jax.ShapeDtypeStruct((4,) + a.shape, a.dtype) for a in pack],
        scratch_shapes=_gather_sems(pack), compiler_params=_cparams(2),
    )(q, k, v, *pack)


def _flash_bwd(name, q, k, v, do, lse, dl, T, tq, scatter=None):
    HPS = HPS_BWD
    nq = T // tq

    def body(k_ref, v_ref, q_ref, do_ref, lse_ref, dl_ref, dk_ref, dv_ref, dq_ref):
        j = pl.program_id(1)

        @pl.when(j == 0)
        def _():
            dq_ref[...] = jnp.zeros_like(dq_ref)

        nt = (((1,), (1,)), ((), ()))
        tn = (((0,), (0,)), ((), ()))
        ks = [k_ref[:, hh * DH:(hh + 1) * DH] for hh in range(HPS)]
        vs = [v_ref[:, hh * DV:(hh + 1) * DV] for hh in range(HPS)]

        def one(hh, i, carry, masked):
            dk, dv = carry
            off = pl.multiple_of(i * tq, tq)
            qi = q_ref[pl.ds(off, tq), hh * DH:(hh + 1) * DH]
            doi = do_ref[pl.ds(off, tq), hh * DV:(hh + 1) * DV]
            st = lax.dot_general(ks[hh], qi, nt, preferred_element_type=f32)
            if masked:
                kr = lax.broadcasted_iota(jnp.int32, (tq, tq), 0)
                qc = lax.broadcasted_iota(jnp.int32, (tq, tq), 1)
                st = jnp.where(qc >= kr, st, NEG)
            dpt = lax.dot_general(vs[hh], doi, nt, preferred_element_type=f32)
            pt = jnp.exp2(st - lse_ref[hh, i])
            dst = (pt * (dpt - dl_ref[hh, i])).astype(bf16)
            dv = dv + jnp.dot(pt.astype(bf16), doi, preferred_element_type=f32)
            dk = dk + jnp.dot(dst, qi, preferred_element_type=f32)
            dq_ref[pl.ds(off, tq), hh * DH:(hh + 1) * DH] += lax.dot_general(dst, ks[hh], tn, preferred_element_type=f32)
            return dk, dv

        def step(i, carries, masked):
            return tuple(one(hh, i, carries[hh], masked) for hh in range(HPS))

        zero = tuple((jnp.zeros((tq, DH), f32), jnp.zeros((tq, DV), f32)) for _ in range(HPS))
        carry = step(j, zero, True)
        carry = lax.fori_loop(j + 1, nq, lambda i, c: step(i, c, False), carry)
        for hh in range(HPS):
            dk, dv = carry[hh]
            dk_ref[:, hh * DH:(hh + 1) * DH] = dk * LN2
            dv_ref[:, hh * DV:(hh + 1) * DV] = dv.astype(bf16)

    def body_with_scatter(*refs):
        ns = len(scatter)
        ins, a_refs, outs = refs[:6], refs[6:6 + ns], refs[6 + ns:9 + ns]
        o_refs, sems = refs[9 + ns:9 + 2 * ns], refs[9 + 2 * ns:]
        h = pl.program_id(0)
        j = pl.program_id(1)

        @pl.when((h == 0) & (j == 0))
        def _():
            for cp in _scatter_copies(a_refs, o_refs, *sems):
                cp.start()

        body(*ins, *outs)

        @pl.when((h == ngrp - 1) & (j == nq - 1))
        def _():
            for cp in _scatter_copies(a_refs, o_refs, *sems):
                cp.wait()

    ngrp = NH // HPS
    in_specs = [pl.BlockSpec((tq, HPS * DH), lambda h, j: (j, h)),
                pl.BlockSpec((tq, HPS * DV), lambda h, j: (j, h)),
                pl.BlockSpec((T, HPS * DH), lambda h, j: (0, h), pipeline_mode=pl.Buffered(1)),
                pl.BlockSpec((T, HPS * DV), lambda h, j: (0, h), pipeline_mode=pl.Buffered(1)),
                pl.BlockSpec((HPS, nq, 1, tq), lambda h, j: (h, 0, 0, 0)),
                pl.BlockSpec((HPS, nq, 1, tq), lambda h, j: (h, 0, 0, 0))]
    out_specs = [pl.BlockSpec((tq, HPS * DH), lambda h, j: (j, h)),
                 pl.BlockSpec((tq, HPS * DV), lambda h, j: (j, h)),
                 pl.BlockSpec((T, HPS * DH), lambda h, j: (0, h), pipeline_mode=pl.Buffered(1))]
    out_shape = [jax.ShapeDtypeStruct((T, NH * DH), f32), jax.ShapeDtypeStruct((T, NH * DV), bf16),
                 jax.ShapeDtypeStruct((T, NH * DH), f32)]
    if scatter is None:
        return pl.pallas_call(body, name=name, grid=(ngrp, nq), in_specs=in_specs, out_specs=out_specs,
                              out_shape=out_shape, compiler_params=_cparams(2))(k, v, q, do, lse, dl)
    ns = len(scatter)
    return pl.pallas_call(
        body_with_scatter, name=name, grid=(ngrp, nq), in_specs=in_specs + [ANY] * ns,
        out_specs=out_specs + [ANY] * ns, out_shape=out_shape + _scatter_shapes(scatter),
        scratch_shapes=_scatter_sems(scatter), compiler_params=_cparams(2),
    )(k, v, q, do, lse, dl, *scatter)


def _gate_mix(name, proj, ya, o, T, tb):
    def body(ga_ref, gb_ref, ya_ref, o_ref, y_ref):
        y_ref[...] = (jax.nn.sigmoid(ga_ref[...].astype(f32)) * ya_ref[...]
                      + jax.nn.sigmoid(gb_ref[...].astype(f32)) * o_ref[...].astype(f32)).astype(bf16)

    return _rowwise(name, body, T, tb, [(proj, D, 1), (proj, D, 2), (ya, D, 0), (o, D, 0)], [], [(D, bf16)])[0]


def _gate_mix_bwd(name, dy, proj, ya, o, T, tb):
    def body(dy_ref, ga_ref, gb_ref, ya_ref, o_ref, dga_ref, dgb_ref, dya_ref, do_ref, dl_ref):
        dy = dy_ref[...]
        sa = jax.nn.sigmoid(ga_ref[...].astype(f32))
        sb = jax.nn.sigmoid(gb_ref[...].astype(f32))
        ov = o_ref[...].astype(f32)
        dga_ref[...] = (dy * ya_ref[...] * sa * (1.0 - sa)).astype(bf16)
        dgb_ref[...] = (dy * ov * sb * (1.0 - sb)).astype(bf16)
        dya_ref[...] = dy * sa
        do = dy * sb
        do_ref[...] = do.astype(bf16)
        prod = do * ov
        for hd in range(NH):
            dl_ref[:, hd:hd + 1] = jnp.sum(prod[:, hd * DV:(hd + 1) * DV], axis=-1, keepdims=True)

    return _rowwise(name, body, T, tb, [(dy, D, 0), (proj, D, 1), (proj, D, 2), (ya, D, 0), (o, D, 0)], [],
                    [(D, bf16), (D, bf16), (D, f32), (D, bf16), (NH, f32)])


def _ffn_in_swiglu(name, h, w, T):
    tm = _pick(T, (512, 256, 128))

    def body(h_ref, w_ref, z_ref, a_ref):
        z = jnp.dot(h_ref[...], w_ref[...], preferred_element_type=f32)
        g = z[:, :FH]
        z_ref[...] = z.astype(bf16)
        a_ref[...] = (g * jax.nn.sigmoid(g) * z[:, FH:]).astype(bf16)

    return pl.pallas_call(
        body, name=name, grid=(2, T // tm),
        in_specs=[pl.BlockSpec((tm, D), lambda j, i: (i, 0)), pl.BlockSpec((D, 2 * FH), lambda j, i: (0, j))],
        out_specs=[pl.BlockSpec((tm, 2 * FH), lambda j, i: (i, j)), pl.BlockSpec((tm, FH), lambda j, i: (i, j))],
        out_shape=[jax.ShapeDtypeStruct((T, 2 * DFF), bf16), jax.ShapeDtypeStruct((T, DFF), bf16)],
        compiler_params=_cparams(2),
    )(h, w)


def _dact_swiglu_bwd(name, dfo, w, z, T):
    tm = _pick(T, (512, 256, 128))

    def body(d_ref, w_ref, z_ref, o_ref):
        for c0, c1 in ((0, FFN_SPLIT), (FFN_SPLIT, FH)):
            da = lax.dot_general(d_ref[...], w_ref[c0:c1, :], (((1,), (1,)), ((), ())), preferred_element_type=f32)
            g = z_ref[:, c0:c1].astype(f32)
            u = z_ref[:, FH + c0:FH + c1].astype(f32)
            sg = jax.nn.sigmoid(g)
            o_ref[:, c0:c1] = (da * u * sg * (1.0 + g * (1.0 - sg))).astype(bf16)
            o_ref[:, FH + c0:FH + c1] = (da * g * sg).astype(bf16)

    return pl.pallas_call(
        body, name=name, grid=(2, T // tm),
        in_specs=[pl.BlockSpec((tm, D), lambda j, i: (i, 0)), pl.BlockSpec((FH, D), lambda j, i: (j, 0)),
                  pl.BlockSpec((tm, 2 * FH), lambda j, i: (i, j))],
        out_specs=pl.BlockSpec((tm, 2 * FH), lambda j, i: (i, j)),
        out_shape=jax.ShapeDtypeStruct((T, 2 * DFF), bf16), compiler_params=_cparams(2),
    )(dfo, w, z)


def _final(name, x, g, m, gfin, tgt, T, tb):
    def body(x_ref, m_ref, t_ref, g_ref, gf_ref, dx_ref, dm_ref, loss_ref, dgf_ref, dg_ref):
        _init_acc(pl.program_id(0), loss_ref, dgf_ref, dg_ref)
        mv = m_ref[...]
        xv = x_ref[...] + g_ref[...] * mv
        rstd = lax.rsqrt(jnp.mean(xv * xv, axis=-1, keepdims=True) + EPS)
        nv = xv * rstd
        err = nv * gf_ref[...] - t_ref[...]
        loss_ref[...] += 0.5 * jnp.sum(jnp.mean(err * err, axis=-1, keepdims=True), axis=0, keepdims=True)
        dy = err * (1.0 / D)
        dgf_ref[...] += _colsum(dy * nv)
        dn = dy * gf_ref[...]
        dxv = rstd * (dn - nv * jnp.mean(dn * nv, axis=-1, keepdims=True))
        dx_ref[...] = dxv
        dm_ref[...] = (dxv * g_ref[...]).astype(bf16)
        dg_ref[...] += _colsum(dxv * mv)

    return _rowwise(name, body, T, tb, [(x, D, 0), (m, D, 0), (tgt, D, 0)], [g, gfin], [(D, f32), (D, bf16)],
                    [((1, 1), f32), ((1, D), f32), ((1, D), f32)])


def _adamw(name, w, g, m, v):
    R, C = w.shape
    tb = _pick(R, (256, 128, 64, 32, 16, 8))
    c1 = 1.0 - ADAM_B1 ** ADAM_STEP
    c2 = 1.0 - ADAM_B2 ** ADAM_STEP

    def body(w_ref, g_ref, m_ref, v_ref, d_ref, m2_ref, v2_ref):
        gv = g_ref[...]
        m2 = ADAM_B1 * m_ref[...] + (1.0 - ADAM_B1) * gv
        v2 = ADAM_B2 * v_ref[...] + (1.0 - ADAM_B2) * (gv * gv)
        m2_ref[...] = m2
        v2_ref[...] = v2
        d_ref[...] = -ADAM_LR * ((m2 / c1) / (jnp.sqrt(v2 / c2) + ADAM_EPS) + ADAM_WD * w_ref[...])

    return _rowwise(name, body, R, tb, [(w, C, 0), (g, C, 0), (m, C, 0), (v, C, 0)], [], [(C, f32)] * 3)


def _ada_fwd(name, c_all, w_ada, b_sh):
    nl, _, ns = w_ada.shape

    def body(c_ref, w_ref, b_ref, o_ref):
        o_ref[0] = jnp.dot(c_ref[...].astype(bf16), w_ref[0].astype(bf16), preferred_element_type=f32) + b_ref[0]

    return pl.pallas_call(
        body, name=name, grid=(nl,),
        in_specs=[pl.BlockSpec((8, D), lambda l: (0, 0)), pl.BlockSpec((1, D, ns), lambda l: (l, 0, 0)),
                  pl.BlockSpec((1, 1, ns), lambda l: (l, 0, 0))],
        out_specs=pl.BlockSpec((1, 8, ns), lambda l: (l, 0, 0)),
        out_shape=jax.ShapeDtypeStruct((nl, 8, ns), f32), compiler_params=_cparams(1),
    )(c_all, w_ada, b_sh)


def _ada_bwd(name, c_t, dm_sh):
    nl, _, ns = dm_sh.shape

    def body(c_ref, dm_ref, o_ref):
        acc = c_ref[:, 0:1] * dm_ref[0, 0:1, :]
        for d in range(1, 8):
            acc = acc + c_ref[:, d:d + 1] * dm_ref[0, d:d + 1, :]
        o_ref[0] = acc

    return pl.pallas_call(
        body, name=name, grid=(nl,),
        in_specs=[pl.BlockSpec((D, 8), lambda l: (0, 0)), pl.BlockSpec((1, 8, ns), lambda l: (l, 0, 0))],
        out_specs=pl.BlockSpec((1, D, ns), lambda l: (l, 0, 0)),
        out_shape=jax.ShapeDtypeStruct((nl, D, ns), f32), compiler_params=_cparams(1),
    )(c_t, dm_sh)


def _sum_slots(name, a):
    n, R, C = a.shape
    tb = _pick(R, (256, 128, 64, 32, 16, 8))

    def body(a_ref, o_ref):
        acc = a_ref[0].astype(f32)
        for s in range(1, n):
            acc = acc + a_ref[s].astype(f32)
        o_ref[...] = acc

    return pl.pallas_call(
        body, name=name, grid=(R // tb,), in_specs=[pl.BlockSpec((n, tb, C), lambda i: (0, i, 0))],
        out_specs=pl.BlockSpec((tb, C), lambda i: (i, 0)), out_shape=jax.ShapeDtypeStruct((R, C), f32),
        compiler_params=_cparams(1),
    )(a)


def _chunks(rows, dtype, want):
    unit = 16 if dtype == bf16 else 8
    for n in range(want, 0, -1):
        if rows % n == 0 and (rows // n) % unit == 0:
            return n, rows // n
    return 1, rows


def _my_chip():
    return 2 * lax.axis_index("x") + lax.axis_index("y")


def _pair_sum(name, arrs, gots):
    na = len(arrs)

    def body(*refs):
        for k in range(na):
            a_ref, g_ref, o_ref = refs[k], refs[na + k], refs[2 * na + k]
            o_ref[...] = (a_ref[...].astype(f32) + g_ref[...].astype(f32)).astype(o_ref.dtype)

    in_specs = [pl.BlockSpec((1, a.shape[1] // 4, a.shape[2]), lambda s, i: (s, 2 * lax.axis_index("c") + i, 0))
                for a in arrs]
    half_specs = [pl.BlockSpec((1, g.shape[1] // 2, g.shape[2]), lambda s, i: (s, i, 0)) for g in gots]
    return pl.pallas_call(
        body, name=name, grid=(4, 2), in_specs=in_specs + half_specs, out_specs=half_specs,
        out_shape=[jax.ShapeDtypeStruct(g.shape, a.dtype) for a, g in zip(arrs, gots)], compiler_params=_cparams(2),
    )(*arrs, *gots)


def _chip_sum(name, sums, lands, layer, bufs=None):
    na = len(sums)

    def body(*refs):
        for k in range(na):
            s_ref, l_ref, o_ref = refs[k], refs[na + k], refs[len(refs) - na + k]
            acc = s_ref[0].astype(f32)
            for j in range(l_ref.shape[0]):
                acc = acc + l_ref[j].astype(f32)
            o_ref[...] = acc.reshape(o_ref.shape)

    in_specs = [pl.BlockSpec((1, s.shape[1] // 2, s.shape[2]), lambda i: (_my_chip(), i, 0)) for s in sums]
    in_specs += [pl.BlockSpec((3, l.shape[1] // 2, l.shape[2]), lambda i: (0, i, 0)) for l in lands]
    if layer is None:
        out_specs = [pl.BlockSpec((s.shape[1] // 2, s.shape[2]), lambda i: (i, 0)) for s in sums]
        out_shape = [jax.ShapeDtypeStruct(s.shape[1:], f32) for s in sums]
    else:
        out_specs = [pl.BlockSpec((1, 1, s.shape[1] // 2, s.shape[2]), lambda i: (layer, lax.axis_index("c"), i, 0))
                     for s in sums]
        out_shape = [jax.ShapeDtypeStruct((2, 2) + s.shape[1:], f32) for s in sums]
    aliases = {}
    operands = list(sums) + list(lands)
    if bufs is not None:
        in_specs += [ANY] * na
        aliases = {2 * na + k: k for k in range(na)}
        operands += list(bufs)
    return pl.pallas_call(
        body, name=name, grid=(2,), in_specs=in_specs, out_specs=out_specs, out_shape=out_shape,
        input_output_aliases=aliases, compiler_params=_cparams(1),
    )(*operands)


def _coords():
    return lax.axis_index("x"), lax.axis_index("y"), lax.axis_index("c")


def _flip(v, m):
    return (1 - v) if m else v


def _peer(x, y, c, mask):
    return (_flip(x, mask[0]), _flip(y, mask[1]), _flip(c, mask[2]))


def _all_gather(name, arr, masks, whole_mesh):
    nslots = 8 if whole_mesh else 4

    def slot(x, y, c):
        return 4 * x + 2 * y + c if whole_mesh else 2 * x + y

    def body(a_ref, o_ref, send_sems, recv_sems, local_sem):
        x, y, c = _coords()
        mine = pltpu.make_async_copy(a_ref, o_ref.at[slot(x, y, c)], local_sem)
        mine.start()
        copies = []
        for k, mask in enumerate(masks):
            cp = pltpu.make_async_remote_copy(
                src_ref=a_ref, dst_ref=o_ref.at[slot(x, y, c)], send_sem=send_sems.at[k], recv_sem=recv_sems.at[k],
                device_id=_peer(x, y, c, mask), device_id_type=MESH)
            cp.start()
            copies.append(cp)
        for cp in copies:
            cp.wait()
        mine.wait()

    return pl.pallas_call(
        body, name=name, in_specs=[ANY], out_specs=ANY,
        out_shape=jax.ShapeDtypeStruct((nslots,) + arr.shape, arr.dtype),
        scratch_shapes=[pltpu.SemaphoreType.DMA((len(masks),)), pltpu.SemaphoreType.DMA((len(masks),)),
                        pltpu.SemaphoreType.DMA],
    )(arr)


def _gather_weights(name, arrs):
    na = len(arrs)

    def body(*refs):
        mine, first, passed = _gather_copies(refs[:na], refs[na:2 * na], *refs[2 * na:])
        _gather_start(mine, first)
        _gather_forward(first, passed)
        _gather_finish(mine, first, passed)

    return pl.pallas_call(
        body, name=name, in_specs=[ANY] * na, out_specs=[ANY] * na,
        out_shape=[jax.ShapeDtypeStruct((4,) + a.shape, a.dtype) for a in arrs], scratch_shapes=_gather_sems(arrs),
    )(*arrs)


def _gather_plan(shape, dtype):
    big = (shape[0] // 2) * shape[1] * 2 >= (2 << 20)
    return _chunks(shape[0] // 2, dtype, 3 if big else 1), _chunks(shape[0], dtype, 4 if big else 1)


def _gather_sems(arrs):
    plans = [_gather_plan(a.shape, a.dtype) for a in arrs]
    nremote = sum(len(CHIP_MASKS) * p[0][0] for p in plans)
    nlocal = sum(p[1][0] for p in plans)
    return [pltpu.SemaphoreType.DMA((nremote,))] * 4 + [pltpu.SemaphoreType.DMA((nlocal,))]


def _gather_copies(w_refs, o_refs, sa, ra, sb, rb, local_sems):
    x, y, c = _coords()
    chip = 2 * x + y
    mine, first, passed = [], [], []
    for w_ref, o_ref in zip(w_refs, o_refs):
        (nch, rows), (nloc, lrows) = _gather_plan(w_ref.shape, w_ref.dtype)
        rh = w_ref.shape[0] // 2
        for k in range(nloc):
            part = pl.ds(k * lrows, lrows)
            mine.append(pltpu.make_async_copy(w_ref.at[part], o_ref.at[chip, part], local_sems.at[len(mine)]))
        for mask in CHIP_MASKS:
            px, py, _ = _peer(x, y, c, mask)
            for ch in range(nch):
                k = len(first)
                part = pl.ds(c * rh + ch * rows, rows)
                first.append(pltpu.make_async_remote_copy(
                    src_ref=w_ref.at[part], dst_ref=o_ref.at[chip, part], send_sem=sa.at[k], recv_sem=ra.at[k],
                    device_id=(px, py, c), device_id_type=MESH))
                landed = o_ref.at[2 * px + py, part]
                passed.append(pltpu.make_async_remote_copy(
                    src_ref=landed, dst_ref=landed, send_sem=sb.at[k], recv_sem=rb.at[k],
                    device_id=(x, y, 1 - c), device_id_type=MESH))
    return mine, first, passed


def _gather_start(mine, first):
    for cp in mine + first:
        cp.start()


def _gather_forward(first, passed):
    for a, b in zip(first, passed):
        a.wait_recv()
        b.start()


def _gather_finish(mine, first, passed):
    for a, b in zip(first, passed):
        a.wait_send()
        b.wait()
    for cp in mine:
        cp.wait()


def _swap_halves(name, arrs):
    na = len(arrs)
    plan = []
    for a in arrs:
        h = a.shape[1] // 2
        nch, rows = _chunks(h, a.dtype, 2 if h * a.shape[2] * 2 >= (1 << 20) else 1)
        plan.append([(s, ch * rows, rows) for s in range(a.shape[0]) for ch in range(nch)])
    ncopies = sum(len(p) for p in plan)

    def body(*refs):
        a_refs, got_refs = refs[:na], refs[na:2 * na]
        send_sems, recv_sems = refs[2 * na:]
        x, y, c = _coords()
        cps = []
        for ai in range(na):
            h = arrs[ai].shape[1] // 2
            for (s, r0, rows) in plan[ai]:
                k = len(cps)
                cp = pltpu.make_async_remote_copy(
                    src_ref=a_refs[ai].at[s, pl.ds((1 - c) * h + r0, rows)], dst_ref=got_refs[ai].at[s, pl.ds(r0, rows)],
                    send_sem=send_sems.at[k], recv_sem=recv_sems.at[k], device_id=(x, y, 1 - c), device_id_type=MESH)
                cp.start()
                cps.append(cp)
        for cp in cps:
            cp.wait()

    halves = [jax.ShapeDtypeStruct((a.shape[0], a.shape[1] // 2, a.shape[2]), a.dtype) for a in arrs]
    return pl.pallas_call(
        body, name=name, in_specs=[ANY] * na, out_specs=[ANY] * na, out_shape=halves,
        scratch_shapes=[pltpu.SemaphoreType.DMA((ncopies,))] * 2,
    )(*arrs)


def _scatter_shapes(arrs):
    return [jax.ShapeDtypeStruct((len(CHIP_MASKS),) + a.shape[1:], a.dtype) for a in arrs]


def _scatter_sems(arrs):
    return [pltpu.SemaphoreType.DMA((len(arrs), len(CHIP_MASKS)))] * 2


def _scatter_copies(a_refs, o_refs, send_sems, recv_sems):
    x, y, c = _coords()
    cps = []
    for k in range(len(a_refs)):
        for j, mask in enumerate(CHIP_MASKS):
            px, py, pc = _peer(x, y, c, mask)
            cps.append(pltpu.make_async_remote_copy(
                src_ref=a_refs[k].at[2 * px + py], dst_ref=o_refs[k].at[j],
                send_sem=send_sems.at[k, j], recv_sem=recv_sems.at[k, j],
                device_id=(px, py, pc), device_id_type=MESH))
    return cps


def _join_halves(name, bufs, small):
    na = len(bufs)
    nm = len(ALL_MASKS)

    def body(*refs):
        small_ref, o_refs, all_ref = refs[na], refs[na + 1:2 * na + 1], refs[2 * na + 1]
        send_sems, recv_sems, gs_sems, gr_sems, local_sem = refs[2 * na + 2:]
        x, y, c = _coords()
        cps = []
        for k in range(na):
            for l in range(2):
                part = o_refs[k].at[l, c]
                cps.append(pltpu.make_async_remote_copy(
                    src_ref=part, dst_ref=part, send_sem=send_sems.at[k, l], recv_sem=recv_sems.at[k, l],
                    device_id=(x, y, 1 - c), device_id_type=MESH))
        slot = all_ref.at[4 * x + 2 * y + c]
        for k, mask in enumerate(ALL_MASKS):
            cps.append(pltpu.make_async_remote_copy(
                src_ref=small_ref, dst_ref=slot, send_sem=gs_sems.at[k], recv_sem=gr_sems.at[k],
                device_id=_peer(x, y, c, mask), device_id_type=MESH))
        mine = pltpu.make_async_copy(small_ref, slot, local_sem)
        mine.start()
        for cp in cps:
            cp.start()
        for cp in cps:
            cp.wait()
        mine.wait()

    return pl.pallas_call(
        body, name=name, in_specs=[ANY] * (na + 1), out_specs=[ANY] * (na + 1),
        out_shape=[jax.ShapeDtypeStruct(b.shape, b.dtype) for b in bufs]
        + [jax.ShapeDtypeStruct((8,) + small.shape, small.dtype)],
        input_output_aliases={k: k for k in range(na)},
        scratch_shapes=[pltpu.SemaphoreType.DMA((na, 2))] * 2 + [pltpu.SemaphoreType.DMA((nm,))] * 2
        + [pltpu.SemaphoreType.DMA],
    )(*bufs, small)


BIG = ("w_in", "w_uq", "w_ukv", "w_out", "w_ffn_in", "w_ffn_out")
SMALL = ("conv_w", "conv_b", "lru_wa", "lru_ba", "lru_wx", "lru_bx", "lru_a_param", "q_norm_g", "kv_norm_g",
         "final_norm_g")
SMALL_SHAPES = {"conv_w": (2, CONV, D), "conv_b": (2, D), "lru_wa": (2, NB, BW, BW), "lru_ba": (2, NB, BW),
                "lru_wx": (2, NB, BW, BW), "lru_bx": (2, NB, BW), "lru_a_param": (2, D), "q_norm_g": (2, QR),
                "kv_norm_g": (2, KVR), "final_norm_g": (D,)}
SMALL_ROWS = 640
FFN_CHIP_ORDER = (0, 2, 1, 3)


def _size(shape):
    n = 1
    for s in shape:
        n *= s
    return n


def _pack_rows(parts, rows, dtype):
    flat = jnp.concatenate([p.reshape(-1).astype(dtype) for p in parts])
    pad = rows * 1024 - flat.shape[0]
    if pad:
        flat = jnp.concatenate([flat, jnp.zeros((pad,), dtype)])
    return flat.reshape(rows, 1024)


def _unpack_rows(pack, shapes):
    flat = pack.reshape(-1)
    out, off = [], 0
    for s in shapes:
        n = _size(s)
        out.append(flat[off:off + n].reshape(s))
        off += n
    return out


def _inproj_weights(g_in, g_uq, g_ukv):
    w_in = jnp.concatenate([g_in[s] for s in range(4)], axis=1)
    win_p = jnp.concatenate([w_in[:, :D], w_in[:, 1472:2496], w_in[:, 2496:], w_in[:, D:1472],
                             jnp.zeros((D, 64), w_in.dtype)], axis=1)
    w_uq = g_uq.reshape(QR, NH, DN + DR)
    wq_p = jnp.concatenate([w_uq, jnp.zeros((QR, NH, DH - DN - DR), w_uq.dtype)], axis=2).reshape(QR, NH * DH)
    w_ukv = jnp.transpose(g_ukv.reshape(4, KVR, NH, (DN + DV) // 4), (1, 2, 0, 3)).reshape(KVR, NH, DN + DV)
    wk_p = jnp.concatenate([w_ukv[..., :DN], jnp.zeros((KVR, NH, DH - DN), w_ukv.dtype)], axis=2).reshape(KVR, NH * DH)
    wv = w_ukv[..., DN:].reshape(KVR, NH * DV)
    return dict(win=win_p, wq=wq_p, wk=wk_p, wv=wv)


def _other_weights(g_out, g_fi, g_fo):
    ko, ki, kf = g_out.shape[1] // 2, g_fi.shape[1] // 2, g_fo.shape[1] // 2
    wout = [g_out[:, l * ko:(l + 1) * ko].reshape(4 * ko, D) for l in range(2)]
    wfi = [jnp.concatenate([g_fi[s, l * ki:(l + 1) * ki] for s in FFN_CHIP_ORDER], axis=1) for l in range(2)]
    wfo = [g_fo[:, l * kf:(l + 1) * kf].reshape(4 * kf, D) for l in range(2)]
    return dict(wout=wout, wfi=wfi, wfo=wfo)


def _shard_grads_inproj(dwin, dwq, dwk, dwv):
    w_in = jnp.concatenate([dwin[:, :D], dwin[:, QKV0:QKV0 + 448], dwin[:, D:QKV0]], axis=1)
    w_in = jnp.transpose(w_in.reshape(D, 4, DIN // 4), (1, 0, 2))
    w_uq = dwq.reshape(QR, NH, DH)[..., :DN + DR].reshape(4, QR // 4, NH * (DN + DR))
    w_ukv = jnp.concatenate([dwk.reshape(KVR, NH, DH)[..., :DN], dwv.reshape(KVR, NH, DV)], axis=2)
    w_ukv = jnp.transpose(w_ukv.reshape(KVR, NH, 4, (DN + DV) // 4), (2, 0, 1, 3)).reshape(4, KVR, NH * (DN + DV) // 4)
    return [w_in, w_uq, w_ukv]


def _shard_grads_rest(dwout, dwfi, dwfo):
    tiles = jnp.split(dwfi, 4, axis=1)
    wfi = jnp.stack([tiles[FFN_CHIP_ORDER.index(s)] for s in range(4)])
    return [dwout.reshape(4, D // 4, D), wfi, dwfo.reshape(4, DFF // 4, D)]


def kernel(x, c, positions, w_ada, b_ada, w_in, conv_w, conv_b, lru_wa, lru_ba, lru_wx, lru_bx, lru_a_param, q_norm_g, kv_norm_g, w_uq, w_ukv, w_out, w_ffn_in, w_ffn_out, final_norm_g, loss_target, m_w_ada, m_b_ada, m_w_in, m_conv_w, m_conv_b, m_lru_wa, m_lru_ba, m_lru_wx, m_lru_bx, m_lru_a_param, m_q_norm_g, m_kv_norm_g, m_w_uq, m_w_ukv, m_w_out, m_w_ffn_in, m_w_ffn_out, m_final_norm_g, v_w_ada, v_b_ada, v_w_in, v_conv_w, v_conv_b, v_lru_wa, v_lru_ba, v_lru_wx, v_lru_bx, v_lru_a_param, v_q_norm_g, v_kv_norm_g, v_w_uq, v_w_ukv, v_w_out, v_w_ffn_in, v_w_ffn_out, v_final_norm_g):
    T = x.shape[1]
    tb = min(ROW_BLOCK, T)
    tq = min(ATTN_BLOCK, T)
    nq = T // tq
    mx, my, mc = _coords()
    chip = 2 * mx + my
    dev = 4 * mx + 2 * my + mc
    weights = dict(w_ada=w_ada, b_ada=b_ada, w_in=w_in, conv_w=conv_w, conv_b=conv_b, lru_wa=lru_wa, lru_ba=lru_ba,
                   lru_wx=lru_wx, lru_bx=lru_bx, lru_a_param=lru_a_param, q_norm_g=q_norm_g, kv_norm_g=kv_norm_g,
                   w_uq=w_uq, w_ukv=w_ukv, w_out=w_out, w_ffn_in=w_ffn_in, w_ffn_out=w_ffn_out,
                   final_norm_g=final_norm_g)
    mom = dict(w_ada=m_w_ada, b_ada=m_b_ada, w_in=m_w_in, conv_w=m_conv_w, conv_b=m_conv_b, lru_wa=m_lru_wa,
               lru_ba=m_lru_ba, lru_wx=m_lru_wx, lru_bx=m_lru_bx, lru_a_param=m_lru_a_param, q_norm_g=m_q_norm_g,
               kv_norm_g=m_kv_norm_g, w_uq=m_w_uq, w_ukv=m_w_ukv, w_out=m_w_out, w_ffn_in=m_w_ffn_in,
               w_ffn_out=m_w_ffn_out, final_norm_g=m_final_norm_g)
    var = dict(w_ada=v_w_ada, b_ada=v_b_ada, w_in=v_w_in, conv_w=v_conv_w, conv_b=v_conv_b, lru_wa=v_lru_wa,
               lru_ba=v_lru_ba, lru_wx=v_lru_wx, lru_bx=v_lru_bx, lru_a_param=v_lru_a_param, q_norm_g=v_q_norm_g,
               kv_norm_g=v_kv_norm_g, w_uq=v_w_uq, w_ukv=v_w_ukv, w_out=v_w_out, w_ffn_in=v_w_ffn_in,
               w_ffn_out=v_w_ffn_out, final_norm_g=v_final_norm_g)
    order = list(weights)

    ns_ada = w_ada.shape[2]
    c_all = _all_gather("gather_c", c, ALL_MASKS, True).reshape(8, D)
    b_sh = lax.dynamic_slice_in_dim(b_ada, chip * ns_ada, ns_ada, axis=1).reshape(2, 1, ns_ada)
    mod_sh = _ada_fwd("ada_fwd", c_all, w_ada, b_sh)
    mod_all = _all_gather("gather_mod", mod_sh.reshape(16, ns_ada), CHIP_MASKS, False)
    mod_mine = lax.dynamic_index_in_dim(mod_all.reshape(4, 2, 8, ns_ada), dev, axis=2, keepdims=False)
    mod = jnp.transpose(mod_mine, (1, 0, 2)).reshape(2, 6, 1, D)

    flat = lambda w: w.astype(bf16).reshape(w.shape[0], w.shape[1], -1)
    b_in, b_uq, b_ukv = flat(w_in), flat(w_uq), flat(w_ukv)
    late = [b_in[1], b_uq[1], b_ukv[1]] + [flat(w).reshape(-1, w.shape[-1]) for w in (w_out, w_ffn_in, w_ffn_out)]
    wa_b = lru_wa.astype(bf16)
    wx_b = lru_wx.astype(bf16)

    pos = positions.reshape(T, 1)
    invf = ROPE_THETA ** (-jnp.arange(0, DR, 2, dtype=f32) / DR)
    invf_p = jnp.concatenate([jnp.zeros((DN,), f32), invf, invf, jnp.zeros((DH - DN - DR,), f32)]).reshape(1, DH)
    cosp, sap, sbp, *early_all = _rope_tables("rope_tables", pos, invf_p, T, tb, [b_in[0], b_uq[0], b_ukv[0]])
    W0 = _inproj_weights(*early_all)
    W = {n: [W0[n], None] for n in W0}

    ns_cw = conv_w.shape[2]
    cw_all = _all_gather("gather_conv_w", conv_w.reshape(2 * CONV, ns_cw), CHIP_MASKS, False)
    cw_full = jnp.transpose(cw_all, (1, 0, 2)).reshape(2, CONV, D)
    vec = lambda a, l: a[l].reshape(1, -1)

    xl = x.reshape(T, D)
    saved = []
    h1 = _norm_mod("norm_mod_0", xl, mod[0, 1], mod[0, 0], T, tb)
    for l in range(2):
        sh_m, sc_m, g_m, sh_f, sc_f, g_f = [mod[l, j] for j in range(6)]
        proj = _matmul(f"proj_{l}", h1, W["win"][l], "nn", bf16)
        u, r, ig, ya = _lru_fwd(f"lru_fwd_{l}", proj, pos, cw_full[l], vec(conv_b, l), wa_b[l], vec(lru_ba, l),
                                wx_b[l], vec(lru_bx, l), vec(lru_a_param, l), T, tb)
        qp, kp, vp, cq, ckv = _mla_prep(f"mla_prep_{l}", proj, cosp, sap, sbp, vec(q_norm_g, l), vec(kv_norm_g, l),
                                        W["wq"][l], W["wk"][l], W["wv"][l], T, tb)
        if l == 0:
            o, lse, *late_all = _flash_fwd("flash_fwd_0", qp, kp, vp, T, tq, late)
            W1 = _inproj_weights(*late_all[:3])
            for n in W1:
                W[n][1] = W1[n]
            W.update(_other_weights(*late_all[3:]))
        else:
            o, lse = _flash_fwd("flash_fwd_1", qp, kp, vp, T, tq)
        y = _gate_mix(f"gate_mix_{l}", proj, ya, o, T, tb)
        mo = _matmul(f"out_proj_{l}", y, W["wout"][l], "nn")
        x2, h2 = _resid_norm_mod(f"resid_norm_f_{l}", xl, g_m, mo, sc_f, sh_f, T, tb)
        gu, act = _ffn_in_swiglu(f"ffn_in_{l}", h2, W["wfi"][l], T)
        fo = _matmul(f"ffn_out_{l}", act, W["wfo"][l], "nn")
        saved.append(dict(x=xl, h1=h1, proj=proj, u=u, r=r, ig=ig, ya=ya, qp=qp, kp=kp, vp=vp, cq=cq, ckv=ckv, o=o,
                          lse=lse, y=y, mo=mo, x2=x2, h2=h2, gu=gu, act=act, fo=fo))
        if l == 0:
            xl, h1 = _resid_norm_mod("resid_norm_m_1", x2, g_f, fo, mod[1, 1], mod[1, 0], T, tb)

    dx, dfo, loss_part, dgfin, dg_f = _final("final", saved[1]["x2"], mod[1, 5], saved[1]["fo"],
                                             final_norm_g.reshape(1, D), loss_target.reshape(T, D), T, tb)
    loss = lax.psum(loss_part[0, 0], ("x", "y", "c"))

    gl = {n: [None, None] for n in ("win", "wq", "wk", "wv", "wout", "wfi", "wfo", "conv_w", "conv_b", "lru_wa",
                                    "lru_ba", "lru_wx", "lru_bx", "lru_a_param", "q_norm_g", "kv_norm_g")}
    dmod = [None, None]

    def pair_sums(tag, arrs):
        return list(_pair_sum(f"rs_pair_sum_{tag}", arrs, _swap_halves(f"rs_swap_halves_{tag}", arrs)))

    def inproj_pieces(l):
        return _shard_grads_inproj(gl["win"][l], gl["wq"][l], gl["wk"][l], gl["wv"][l])

    def rest_pieces(l):
        return _shard_grads_rest(gl["wout"][l], gl["wfi"][l], gl["wfo"][l])

    for l in (1, 0):
        s = saved[l]
        sh_m, sc_m, g_m, sh_f, sc_f, g_f = [mod[l, j] for j in range(6)]
        dgu = _dact_swiglu_bwd(f"d_act_{l}", dfo, W["wfo"][l], s["gu"], T)
        gl["wfo"][l] = _matmul(f"dw_ffn_out_{l}", s["act"], dfo, "tn", bf16)
        dh2 = _matmul(f"d_h2_{l}", dgu, W["wfi"][l], "nt")
        gl["wfi"][l] = _matmul(f"dw_ffn_in_{l}", s["h2"], dgu, "tn", bf16)
        dx2, dmo, dsc_f, dsh_f, dg_m = _norm_mod_bwd(f"norm_bwd_f_{l}", dh2, s["x2"], sc_f, dx, T, tb, (g_m, s["mo"]))
        dy = _matmul(f"d_y_{l}", dmo, W["wout"][l], "nt")
        gl["wout"][l] = _matmul(f"dw_out_{l}", s["y"], dmo, "tn", bf16)
        dga, dgb, dya, do, dlt = _gate_mix_bwd(f"gate_mix_bwd_{l}", dy, s["proj"], s["ya"], s["o"], T, tb)
        lse_r = s["lse"].reshape(NH, nq, 1, tq)
        dl_r = jnp.transpose(dlt).reshape(NH, nq, 1, tq)
        if l == 0:
            sums0a = pair_sums("0a", rest_pieces(0))
            dk, dv, dq, *lands = _flash_bwd("flash_bwd_0", s["qp"], s["kp"], s["vp"], do, lse_r, dl_r, T, tq,
                                            sums1 + sums0a)
            rs_bufs = list(_chip_sum("rs_chip_sum_1", sums1, lands[:6], 1))
            rs_bufs[3:] = _chip_sum("rs_chip_sum_0a", sums0a, lands[6:], 0, rs_bufs[3:])
        else:
            dk, dv, dq = _flash_bwd("flash_bwd_1", s["qp"], s["kp"], s["vp"], do, lse_r, dl_r, T, tq)
        dqkv, dqp, dkp, dgq, dgkv = _mla_bwd(f"mla_bwd_{l}", dq, dk, dv, s["proj"], cosp, sap, sbp,
                                              vec(q_norm_g, l), vec(kv_norm_g, l), W["wq"][l], W["wk"][l], W["wv"][l],
                                              T, tb)
        gl["wq"][l] = _matmul(f"dw_uq_{l}", s["cq"], dqp, "tn", bf16)
        gl["wk"][l] = _matmul(f"dw_uk_{l}", s["ckv"], dkp, "tn", bf16)
        gl["wv"][l] = _matmul(f"dw_uv_{l}", s["ckv"], dv, "tn", bf16)
        dxl, dcw, dcb, dwa, dba, dwx, dbx, dsp = _lru_bwd(
            f"lru_bwd_{l}", dya, s["proj"], s["u"], s["r"], s["ig"], s["ya"], pos, cw_full[l], wa_b[l], wx_b[l],
            vec(lru_a_param, l), T, tb)
        dproj = jnp.concatenate([dxl, dga, dgb, dqkv], axis=1)
        gl["win"][l] = _matmul(f"dw_in_{l}", s["h1"], dproj, "tn", bf16)
        gl["conv_w"][l], gl["conv_b"][l] = dcw, dcb[0]
        gl["lru_wa"][l], gl["lru_ba"][l], gl["lru_wx"][l], gl["lru_bx"][l] = dwa, dba[0], dwx, dbx[0]
        gl["lru_a_param"][l] = dsp[0]
        gl["q_norm_g"][l], gl["kv_norm_g"][l] = dgq[0], dgkv[0]
        if l == 1:
            dh1 = _matmul("d_h1_1", dproj, W["win"][l], "nt")
            dx, dfo_below, dsc_m, dsh_m, dg_f_below = _norm_mod_bwd(
                "norm_bwd_m_1", dh1, s["x"], sc_m, dx2, T, tb, (mod[0, 5], saved[0]["fo"]))
            sums1 = pair_sums("1", inproj_pieces(1) + rest_pieces(1))
        else:
            gfull = {n: jnp.stack(gl[n]) for n in SMALL if n != "final_norm_g"}
            gfull["final_norm_g"] = dgfin[0]
            gsmall = _pack_rows([gfull[n] for n in SMALL], SMALL_ROWS, f32).reshape(4, SMALL_ROWS // 4, 1024)
            sums0b = pair_sums("0b", inproj_pieces(0) + [gsmall])
            dh1, *lands0b = _matmul("d_h1_0", dproj, W["win"][l], "nt", scatter=sums0b)
            dx, dsc_m, dsh_m = _norm_mod_bwd("norm_bwd_m_0", dh1, s["x"], sc_m, dx2, T, tb)
        dmod[l] = jnp.concatenate([dsh_m, dsc_m, dg_m, dsh_f, dsc_f, dg_f], axis=1)
        if l == 1:
            dfo, dg_f = dfo_below, dg_f_below
    grad_x = dx.reshape(1, T, D)

    dmod_all = _all_gather("gather_dmod", jnp.concatenate(dmod, axis=1), ALL_MASKS, True).reshape(8, 2, 6, D)
    dmod_sum = _sum_slots("sum_dmod", dmod_all.reshape(8, 12, D)).reshape(2, 6 * D)
    dm_sh = lax.dynamic_slice_in_dim(jnp.transpose(dmod_all.reshape(8, 2, 6 * D), (1, 0, 2)), chip * ns_ada, ns_ada, axis=2)
    grads = {"w_ada": _ada_bwd("ada_bwd", jnp.transpose(c_all), dm_sh), "b_ada": dmod_sum}

    rs_bufs[:3] = _chip_sum("rs_chip_sum_0b", sums0b[:3], lands0b[:3], 0, rs_bufs[:3])
    tot_s = _chip_sum("rs_chip_sum_small", sums0b[3:], lands0b[3:], None)[0]
    *bufs, small_all = _join_halves("rs_join_halves", rs_bufs, tot_s)
    small_all = small_all.reshape(SMALL_ROWS, 1024)
    for n, buf in zip(BIG, bufs):
        grads[n] = buf.reshape(weights[n].shape)
    for n, gs in zip(SMALL, _unpack_rows(small_all, [SMALL_SHAPES[n] for n in SMALL])):
        grads[n] = gs
    grads["conv_w"] = lax.dynamic_slice_in_dim(grads["conv_w"], chip * ns_cw, ns_cw, axis=2)

    deltas, new_m, new_v = {}, {}, {}
    for n in order:
        w = weights[n]
        view = (1, w.shape[0]) if w.ndim == 1 else (_size(w.shape[:-1]), w.shape[-1]) if w.shape[-1] >= 128 else (w.shape[0], _size(w.shape[1:]))
        d, m2, v2 = _adamw(f"adamw_{n}", w.reshape(view), grads[n].reshape(view), mom[n].reshape(view), var[n].reshape(view))
        deltas[n], new_m[n], new_v[n] = d.reshape(w.shape), m2.reshape(w.shape), v2.reshape(w.shape)
    return (loss, grad_x, *[grads[n] for n in order], *[deltas[n] for n in order], *[new_m[n] for n in order],
            *[new_v[n] for n in order])
```
